```python
import math
import jax, jax.numpy as jnp
from jax import lax
import numpy as np

D_MODEL = 1024
BATCH = 8
SEQ = 8192
DEPTH = 2

MEM_LEN = 256
D_MIX = D_MODEL
SSD_WIDTH = D_MIX // 2
SSD_HEAD_DIM = 64
SSD_HEADS = SSD_WIDTH // SSD_HEAD_DIM
SSD_GROUPS = 2
SSD_HEADS_PER_GROUP = SSD_HEADS // SSD_GROUPS
SSD_STATE = 128
SSD_CONV = 4
SSD_CHUNK = 128
SSD_XBC = SSD_WIDTH + 2 * SSD_GROUPS * SSD_STATE

S5_WIDTH = D_MIX // 4
S5_GROUP_CH = 16
S5_GROUPS = S5_WIDTH // S5_GROUP_CH
S5_STATE = 64

RG_WIDTH = D_MIX - SSD_WIDTH - S5_WIDTH
RG_BLOCKS = 4
RG_BLOCK_DIM = RG_WIDTH // RG_BLOCKS
RG_CONV = 4
RG_C = 8.0

XA_HEADS = 4
XA_HEAD_DIM = D_MODEL // XA_HEADS
D_FF = 4 * D_MODEL

ALPHA = (2.0 * DEPTH) ** 0.25
BETA = (8.0 * DEPTH) ** -0.25
LN_EPS = 1e-5

IN_COLS = (SSD_WIDTH, SSD_XBC, SSD_HEADS, S5_WIDTH, RG_WIDTH, RG_WIDTH)
D_IN = SSD_WIDTH + SSD_XBC + SSD_HEADS + S5_WIDTH + RG_WIDTH + RG_WIDTH

kernel_name = "hybrid_ssd_s5_rglru_deepnorm"


def layer_norm(x, g, b):
    x32 = x.astype(jnp.float32)
    mu = jnp.mean(x32, axis=-1, keepdims=True)
    var = jnp.mean(jnp.square(x32 - mu), axis=-1, keepdims=True)
    return (x32 - mu) * lax.rsqrt(var + LN_EPS) * g.astype(jnp.float32) + b.astype(jnp.float32)


def causal_dwconv(x, w, b):
    k, c = w.shape
    y = lax.conv_general_dilated(x, w[:, None, :].astype(x.dtype), window_strides=(1,),
                                 padding=[(k - 1, 0)], dimension_numbers=('NWC', 'WIO', 'NWC'),
                                 feature_group_count=c)
    return y + b.astype(x.dtype)


def _lin_combine(left, right):
    a1, b1 = left
    a2, b2 = right
    return a1 * a2, a2 * b1 + b2


def linear_scan(a, b):
    return lax.associative_scan(_lin_combine, (a, b), axis=1)[1]


def ssd_mixer(z, xbc, dt_raw, conv_w, conv_b, dt_bias, a_log, d_skip, norm_w):
    bsz, seq, _ = z.shape
    nc = seq // SSD_CHUNK
    xbc = jax.nn.silu(causal_dwconv(xbc, conv_w.astype(jnp.float32), conv_b.astype(jnp.float32)))
    xs = xbc[..., :SSD_WIDTH]
    bm = xbc[..., SSD_WIDTH:SSD_WIDTH + SSD_GROUPS * SSD_STATE]
    cm = xbc[..., SSD_WIDTH + SSD_GROUPS * SSD_STATE:]
    dt = jax.nn.softplus(dt_raw + dt_bias.astype(jnp.float32))
    a = -jnp.exp(a_log.astype(jnp.float32))
    g, hg, p, n, q = SSD_GROUPS, SSD_HEADS_PER_GROUP, SSD_HEAD_DIM, SSD_STATE, SSD_CHUNK
    xh = xs.reshape(bsz, seq, SSD_HEADS, p)
    xdt = (xh * dt[..., None]).reshape(bsz, nc, q, g, hg, p)
    adt = (dt * a).reshape(bsz, nc, q, g, hg).transpose(0, 1, 3, 4, 2)
    bc = bm.reshape(bsz, nc, q, g, n)
    cc = cm.reshape(bsz, nc, q, g, n)
    a_cs = jnp.cumsum(adt, axis=-1)
    mask = jnp.tril(jnp.ones((q, q), dtype=bool))
    seg = a_cs[..., :, None] - a_cs[..., None, :]
    lmat = jnp.exp(jnp.where(mask, seg, -jnp.inf))
    cb = jnp.einsum('bclgn,bcsgn->bcgls', cc, bc)
    y_diag = jnp.einsum('bcgls,bcghls,bcsghp->bclghp', cb, lmat, xdt)
    decay_states = jnp.exp(a_cs[..., -1:] - a_cs)
    states = jnp.einsum('bclgn,bcghl,bclghp->bcghpn', bc, decay_states, xdt)
    chunk_decay = jnp.exp(a_cs[..., -1])

    def step(s, inp):
        st, dec = inp
        return s * dec[..., None, None] + st, s

    init = jnp.zeros((bsz, g, hg, p, n), jnp.float32)
    _, prev = lax.scan(step, init, (jnp.moveaxis(states, 1, 0), jnp.moveaxis(chunk_decay, 1, 0)))
    prev = jnp.moveaxis(prev, 0, 1)
    y_off = jnp.einsum('bclgn,bcghpn,bcghl->bclghp', cc, prev, jnp.exp(a_cs))
    y = (y_diag + y_off).reshape(bsz, seq, SSD_HEADS, p) + xh * d_skip.astype(jnp.float32)[:, None]
    y = y.reshape(bsz, seq, SSD_WIDTH) * jax.nn.silu(z)
    y = y * lax.rsqrt(jnp.mean(jnp.square(y), axis=-1, keepdims=True) + LN_EPS)
    return y * norm_w.astype(jnp.float32)


def s5_mixer(u, lam_re, lam_im, log_step, b_re, b_im, c_re, c_im, d_skip, glu_w, glu_b):
    bsz, seq, _ = u.shape
    f32 = jnp.float32
    ug = u.reshape(bsz, seq, S5_GROUPS, S5_GROUP_CH).astype(jnp.complex64)
    lam = lax.complex(lam_re.astype(f32), lam_im.astype(f32))
    step = jnp.exp(log_step.astype(f32))[:, None]
    lam_bar = jnp.exp(lam * step)
    bmat = lax.complex(b_re.astype(f32), b_im.astype(f32))
    b_bar = ((lam_bar - 1.0) / lam)[..., None] * bmat
    bu = jnp.einsum('gpc,blgc->blgp', b_bar, ug)
    h = linear_scan(jnp.broadcast_to(lam_bar, bu.shape), bu)
    cmat = lax.complex(c_re.astype(f32), c_im.astype(f32))
    y = jnp.real(jnp.einsum('gcp,blgp->blgc', cmat, h)).reshape(bsz, seq, S5_WIDTH)
    y = jax.nn.gelu(y + d_skip.astype(f32) * u)
    return y * jax.nn.sigmoid(jnp.einsum('blc,ce->ble', y, glu_w.astype(f32)) + glu_b.astype(f32))


def rglru_mixer(xr, gate_in, conv_w, conv_b, wa, ba, wx, bx, lam):
    bsz, seq, _ = xr.shape
    f32 = jnp.float32
    xc = causal_dwconv(xr, conv_w.astype(f32), conv_b.astype(f32))
    xh = xc.reshape(bsz, seq, RG_BLOCKS, RG_BLOCK_DIM)
    r = jax.nn.sigmoid(jnp.einsum('blhi,hij->blhj', xh, wa.astype(f32)) + ba.astype(f32)).reshape(bsz, seq, RG_WIDTH)
    i = jax.nn.sigmoid(jnp.einsum('blhi,hij->blhj', xh, wx.astype(f32)) + bx.astype(f32)).reshape(bsz, seq, RG_WIDTH)
    log_a = -RG_C * r * jax.nn.softplus(-lam.astype(f32))
    a = jnp.exp(log_a)
    mult = jnp.sqrt(-jnp.expm1(2.0 * log_a))
    h = linear_scan(a, mult * (i * xc))
    return h * jax.nn.gelu(gate_in)


def cross_attention(x, mem, wq, wk, wv, wo):
    bsz, seq, _ = x.shape
    f32 = jnp.float32
    q = jnp.einsum('bld,de->ble', x, wq.astype(f32)).reshape(bsz, seq, XA_HEADS, XA_HEAD_DIM)
    k = jnp.einsum('bmd,de->bme', mem, wk.astype(f32)).reshape(bsz, -1, XA_HEADS, XA_HEAD_DIM)
    v = jnp.einsum('bmd,de->bme', mem, wv.astype(f32)).reshape(bsz, -1, XA_HEADS, XA_HEAD_DIM)
    s = jnp.einsum('blhd,bmhd->bhlm', q, k) * (1.0 / math.sqrt(XA_HEAD_DIM))
    pr = jax.nn.softmax(s, axis=-1)
    o = jnp.einsum('bhlm,bmhd->blhd', pr, v).reshape(bsz, seq, D_MODEL)
    return jnp.einsum('ble,ed->bld', o, wo.astype(f32))


def squared_relu_mlp(x, w1, w2):
    hdn = jnp.square(jax.nn.relu(jnp.einsum('bld,df->blf', x, w1.astype(jnp.float32))))
    return jnp.einsum('blf,fd->bld', hdn, w2.astype(jnp.float32))


def _fwd_setup_inputs(seed: int = 0) -> dict:
    key = jax.random.key(seed)
    ks = iter(jax.random.split(key, 64))
    f32 = jnp.float32

    def nrm(shape, scale):
        return jax.random.normal(next(ks), shape, f32) * scale

    def uni(shape, lo, hi):
        return jax.random.uniform(next(ks), shape, f32, lo, hi)

    L = DEPTH
    x = nrm((BATCH, SEQ, D_MODEL), 1.0)
    mem = nrm((BATCH, MEM_LEN, D_MODEL), 1.0)
    dt0 = jnp.exp(uni((L, SSD_HEADS), math.log(1e-3), math.log(1e-1)))
    a_rg = uni((L, RG_WIDTH), 0.9, 0.999) ** (1.0 / RG_C)
    n_idx = jnp.arange(S5_STATE, dtype=f32)
    return {
        "x": x,
        "mem": mem,
        "w_in": nrm((L, D_MODEL, D_IN), D_MODEL ** -0.5),
        "w_out": nrm((L, D_MIX, D_MODEL), BETA * D_MIX ** -0.5),
        "ssd_conv_w": nrm((L, SSD_CONV, SSD_XBC), SSD_CONV ** -0.5),
        "ssd_conv_b": nrm((L, SSD_XBC), 0.02),
        "ssd_dt_bias": dt0 + jnp.log(-jnp.expm1(-dt0)),
        "ssd_a_log": jnp.log(uni((L, SSD_HEADS), 1.0, 16.0)),
        "ssd_d": 1.0 + nrm((L, SSD_HEADS), 0.02),
        "ssd_norm_w": 1.0 + nrm((L, SSD_WIDTH), 0.02),
        "s5_lam_re": -0.5 + nrm((L, S5_GROUPS, S5_STATE), 0.01),
        "s5_lam_im": jnp.pi * n_idx + nrm((L, S5_GROUPS, S5_STATE), 0.01),
        "s5_log_step": uni((L, S5_GROUPS), math.log(1e-3), math.log(1e-1)),
        "s5_b_re": nrm((L, S5_GROUPS, S5_STATE, S5_GROUP_CH), (2.0 * S5_GROUP_CH) ** -0.5),
        "s5_b_im": nrm((L, S5_GROUPS, S5_STATE, S5_GROUP_CH), (2.0 * S5_GROUP_CH) ** -0.5),
        "s5_c_re": nrm((L, S5_GROUPS, S5_GROUP_CH, S5_STATE), (2.0 * S5_STATE) ** -0.5),
        "s5_c_im": nrm((L, S5_GROUPS, S5_GROUP_CH, S5_STATE), (2.0 * S5_STATE) ** -0.5),
        "s5_d": nrm((L, S5_WIDTH), 1.0),
        "s5_glu_w": nrm((L, S5_WIDTH, S5_WIDTH), S5_WIDTH ** -0.5),
        "s5_glu_b": nrm((L, S5_WIDTH), 0.02),
        "rg_conv_w": nrm((L, RG_CONV, RG_WIDTH), RG_CONV ** -0.5),
        "rg_conv_b": nrm((L, RG_WIDTH), 0.02),
        "rg_wa": nrm((L, RG_BLOCKS, RG_BLOCK_DIM, RG_BLOCK_DIM), RG_BLOCK_DIM ** -0.5),
        "rg_ba": nrm((L, RG_BLOCKS, RG_BLOCK_DIM), 0.02),
        "rg_wx": nrm((L, RG_BLOCKS, RG_BLOCK_DIM, RG_BLOCK_DIM), RG_BLOCK_DIM ** -0.5),
        "rg_bx": nrm((L, RG_BLOCKS, RG_BLOCK_DIM), 0.02),
        "rg_lambda": jnp.log(a_rg / (1.0 - a_rg)),
        "ln1_g": 1.0 + nrm((L, D_MODEL), 0.02),
        "ln1_b": nrm((L, D_MODEL), 0.02),
        "xa_wq": nrm((L, D_MODEL, D_MODEL), D_MODEL ** -0.5),
        "xa_wk": nrm((L, D_MODEL, D_MODEL), D_MODEL ** -0.5),
        "xa_wv": nrm((L, D_MODEL, D_MODEL), BETA * D_MODEL ** -0.5),
        "xa_wo": nrm((L, D_MODEL, D_MODEL), BETA * D_MODEL ** -0.5),
        "ln2_g": 1.0 + nrm((L, D_MODEL), 0.02),
        "ln2_b": nrm((L, D_MODEL), 0.02),
        "mlp_w1": nrm((L, D_MODEL, D_FF), BETA * D_MODEL ** -0.5),
        "mlp_w2": nrm((L, D_FF, D_MODEL), BETA * D_FF ** -0.5),
        "ln3_g": 1.0 + nrm((L, D_MODEL), 0.02),
        "ln3_b": nrm((L, D_MODEL), 0.02),
    }


def _fwd_reference(x, mem, w_in, w_out, ssd_conv_w, ssd_conv_b, ssd_dt_bias, ssd_a_log, ssd_d, ssd_norm_w,
              s5_lam_re, s5_lam_im, s5_log_step, s5_b_re, s5_b_im, s5_c_re, s5_c_im, s5_d, s5_glu_w, s5_glu_b,
              rg_conv_w, rg_conv_b, rg_wa, rg_ba, rg_wx, rg_bx, rg_lambda, ln1_g, ln1_b,
              xa_wq, xa_wk, xa_wv, xa_wo, ln2_g, ln2_b, mlp_w1, mlp_w2, ln3_g, ln3_b):
    f32 = jnp.float32
    out_dtype = x.dtype
    h = x.astype(f32)
    memf = mem.astype(f32)
    split_idx = []
    acc = 0
    for w in IN_COLS[:-1]:
        acc += w
        split_idx.append(acc)
    for l in range(DEPTH):
        proj = jnp.einsum('bld,dk->blk', h, w_in[l].astype(f32))
        z, xbc, dt_raw, u_s5, x_rg, g_rg = jnp.split(proj, split_idx, axis=-1)
        y_ssd = ssd_mixer(z, xbc, dt_raw, ssd_conv_w[l], ssd_conv_b[l], ssd_dt_bias[l], ssd_a_log[l],
                          ssd_d[l], ssd_norm_w[l])
        y_s5 = s5_mixer(u_s5, s5_lam_re[l], s5_lam_im[l], s5_log_step[l], s5_b_re[l], s5_b_im[l],
                        s5_c_re[l], s5_c_im[l], s5_d[l], s5_glu_w[l], s5_glu_b[l])
        y_rg = rglru_mixer(x_rg, g_rg, rg_conv_w[l], rg_conv_b[l], rg_wa[l], rg_ba[l], rg_wx[l], rg_bx[l],
                           rg_lambda[l])
        y = jnp.concatenate([y_ssd, y_s5, y_rg], axis=-1)
        y = jnp.einsum('ble,ed->bld', y, w_out[l].astype(f32))
        h = layer_norm(ALPHA * h + y, ln1_g[l], ln1_b[l])
        h = layer_norm(ALPHA * h + cross_attention(h, memf, xa_wq[l], xa_wk[l], xa_wv[l], xa_wo[l]),
                       ln2_g[l], ln2_b[l])
        h = layer_norm(ALPHA * h + squared_relu_mlp(h, mlp_w1[l], mlp_w2[l]), ln3_g[l], ln3_b[l])
    return h.astype(out_dtype)


import jax as _jax
import jax.numpy as _jnp

TWIN_FORMAT = 'train_step'
FWD_PARAMS = ['x', 'mem', 'w_in', 'w_out', 'ssd_conv_w', 'ssd_conv_b', 'ssd_dt_bias', 'ssd_a_log', 'ssd_d', 'ssd_norm_w', 's5_lam_re', 's5_lam_im', 's5_log_step', 's5_b_re', 's5_b_im', 's5_c_re', 's5_c_im', 's5_d', 's5_glu_w', 's5_glu_b', 'rg_conv_w', 'rg_conv_b', 'rg_wa', 'rg_ba', 'rg_wx', 'rg_bx', 'rg_lambda', 'ln1_g', 'ln1_b', 'xa_wq', 'xa_wk', 'xa_wv', 'xa_wo', 'ln2_g', 'ln2_b', 'mlp_w1', 'mlp_w2', 'ln3_g', 'ln3_b']
TWIN_WEIGHTS = ['w_in', 'w_out', 'ssd_conv_w', 'ssd_conv_b', 'ssd_dt_bias', 'ssd_a_log', 'ssd_d', 'ssd_norm_w', 's5_lam_re', 's5_lam_im', 's5_log_step', 's5_b_re', 's5_b_im', 's5_c_re', 's5_c_im', 's5_d', 's5_glu_w', 's5_glu_b', 'rg_conv_w', 'rg_conv_b', 'rg_wa', 'rg_ba', 'rg_wx', 'rg_bx', 'rg_lambda', 'ln1_g', 'ln1_b', 'xa_wq', 'xa_wk', 'xa_wv', 'xa_wo', 'ln2_g', 'ln2_b', 'mlp_w1', 'mlp_w2', 'ln3_g', 'ln3_b']
TWIN_DIFF_INPUT = 'x'
TWIN_INPUTS = ['x', 'mem', 'w_in', 'w_out', 'ssd_conv_w', 'ssd_conv_b', 'ssd_dt_bias', 'ssd_a_log', 'ssd_d', 'ssd_norm_w', 's5_lam_re', 's5_lam_im', 's5_log_step', 's5_b_re', 's5_b_im', 's5_c_re', 's5_c_im', 's5_d', 's5_glu_w', 's5_glu_b', 'rg_conv_w', 'rg_conv_b', 'rg_wa', 'rg_ba', 'rg_wx', 'rg_bx', 'rg_lambda', 'ln1_g', 'ln1_b', 'xa_wq', 'xa_wk', 'xa_wv', 'xa_wo', 'ln2_g', 'ln2_b', 'mlp_w1', 'mlp_w2', 'ln3_g', 'ln3_b', 'loss_target', 'm_w_in', 'm_w_out', 'm_ssd_conv_w', 'm_ssd_conv_b', 'm_ssd_dt_bias', 'm_ssd_a_log', 'm_ssd_d', 'm_ssd_norm_w', 'm_s5_lam_re', 'm_s5_lam_im', 'm_s5_log_step', 'm_s5_b_re', 'm_s5_b_im', 'm_s5_c_re', 'm_s5_c_im', 'm_s5_d', 'm_s5_glu_w', 'm_s5_glu_b', 'm_rg_conv_w', 'm_rg_conv_b', 'm_rg_wa', 'm_rg_ba', 'm_rg_wx', 'm_rg_bx', 'm_rg_lambda', 'm_ln1_g', 'm_ln1_b', 'm_xa_wq', 'm_xa_wk', 'm_xa_wv', 'm_xa_wo', 'm_ln2_g', 'm_ln2_b', 'm_mlp_w1', 'm_mlp_w2', 'm_ln3_g', 'm_ln3_b', 'v_w_in', 'v_w_out', 'v_ssd_conv_w', 'v_ssd_conv_b', 'v_ssd_dt_bias', 'v_ssd_a_log', 'v_ssd_d', 'v_ssd_norm_w', 'v_s5_lam_re', 'v_s5_lam_im', 'v_s5_log_step', 'v_s5_b_re', 'v_s5_b_im', 'v_s5_c_re', 'v_s5_c_im', 'v_s5_d', 'v_s5_glu_w', 'v_s5_glu_b', 'v_rg_conv_w', 'v_rg_conv_b', 'v_rg_wa', 'v_rg_ba', 'v_rg_wx', 'v_rg_bx', 'v_rg_lambda', 'v_ln1_g', 'v_ln1_b', 'v_xa_wq', 'v_xa_wk', 'v_xa_wv', 'v_xa_wo', 'v_ln2_g', 'v_ln2_b', 'v_mlp_w1', 'v_mlp_w2', 'v_ln3_g', 'v_ln3_b']
TWIN_OUTPUTS = ['loss', 'grad_x', 'grad_w_in', 'grad_w_out', 'grad_ssd_conv_w', 'grad_ssd_conv_b', 'grad_ssd_dt_bias', 'grad_ssd_a_log', 'grad_ssd_d', 'grad_ssd_norm_w', 'grad_s5_lam_re', 'grad_s5_lam_im', 'grad_s5_log_step', 'grad_s5_b_re', 'grad_s5_b_im', 'grad_s5_c_re', 'grad_s5_c_im', 'grad_s5_d', 'grad_s5_glu_w', 'grad_s5_glu_b', 'grad_rg_conv_w', 'grad_rg_conv_b', 'grad_rg_wa', 'grad_rg_ba', 'grad_rg_wx', 'grad_rg_bx', 'grad_rg_lambda', 'grad_ln1_g', 'grad_ln1_b', 'grad_xa_wq', 'grad_xa_wk', 'grad_xa_wv', 'grad_xa_wo', 'grad_ln2_g', 'grad_ln2_b', 'grad_mlp_w1', 'grad_mlp_w2', 'grad_ln3_g', 'grad_ln3_b', 'delta_w_in', 'delta_w_out', 'delta_ssd_conv_w', 'delta_ssd_conv_b', 'delta_ssd_dt_bias', 'delta_ssd_a_log', 'delta_ssd_d', 'delta_ssd_norm_w', 'delta_s5_lam_re', 'delta_s5_lam_im', 'delta_s5_log_step', 'delta_s5_b_re', 'delta_s5_b_im', 'delta_s5_c_re', 'delta_s5_c_im', 'delta_s5_d', 'delta_s5_glu_w', 'delta_s5_glu_b', 'delta_rg_conv_w', 'delta_rg_conv_b', 'delta_rg_wa', 'delta_rg_ba', 'delta_rg_wx', 'delta_rg_bx', 'delta_rg_lambda', 'delta_ln1_g', 'delta_ln1_b', 'delta_xa_wq', 'delta_xa_wk', 'delta_xa_wv', 'delta_xa_wo', 'delta_ln2_g', 'delta_ln2_b', 'delta_mlp_w1', 'delta_mlp_w2', 'delta_ln3_g', 'delta_ln3_b', 'new_m_w_in', 'new_m_w_out', 'new_m_ssd_conv_w', 'new_m_ssd_conv_b', 'new_m_ssd_dt_bias', 'new_m_ssd_a_log', 'new_m_ssd_d', 'new_m_ssd_norm_w', 'new_m_s5_lam_re', 'new_m_s5_lam_im', 'new_m_s5_log_step', 'new_m_s5_b_re', 'new_m_s5_b_im', 'new_m_s5_c_re', 'new_m_s5_c_im', 'new_m_s5_d', 'new_m_s5_glu_w', 'new_m_s5_glu_b', 'new_m_rg_conv_w', 'new_m_rg_conv_b', 'new_m_rg_wa', 'new_m_rg_ba', 'new_m_rg_wx', 'new_m_rg_bx', 'new_m_rg_lambda', 'new_m_ln1_g', 'new_m_ln1_b', 'new_m_xa_wq', 'new_m_xa_wk', 'new_m_xa_wv', 'new_m_xa_wo', 'new_m_ln2_g', 'new_m_ln2_b', 'new_m_mlp_w1', 'new_m_mlp_w2', 'new_m_ln3_g', 'new_m_ln3_b', 'new_v_w_in', 'new_v_w_out', 'new_v_ssd_conv_w', 'new_v_ssd_conv_b', 'new_v_ssd_dt_bias', 'new_v_ssd_a_log', 'new_v_ssd_d', 'new_v_ssd_norm_w', 'new_v_s5_lam_re', 'new_v_s5_lam_im', 'new_v_s5_log_step', 'new_v_s5_b_re', 'new_v_s5_b_im', 'new_v_s5_c_re', 'new_v_s5_c_im', 'new_v_s5_d', 'new_v_s5_glu_w', 'new_v_s5_glu_b', 'new_v_rg_conv_w', 'new_v_rg_conv_b', 'new_v_rg_wa', 'new_v_rg_ba', 'new_v_rg_wx', 'new_v_rg_bx', 'new_v_rg_lambda', 'new_v_ln1_g', 'new_v_ln1_b', 'new_v_xa_wq', 'new_v_xa_wk', 'new_v_xa_wv', 'new_v_xa_wo', 'new_v_ln2_g', 'new_v_ln2_b', 'new_v_mlp_w1', 'new_v_mlp_w2', 'new_v_ln3_g', 'new_v_ln3_b']
TWIN_LEAF_KINDS = {'loss': 'loss', 'grad_x': 'grad_x', 'grad_w_in': 'grad_w', 'grad_w_out': 'grad_w', 'grad_ssd_conv_w': 'grad_w', 'grad_ssd_conv_b': 'grad_w', 'grad_ssd_dt_bias': 'grad_w', 'grad_ssd_a_log': 'grad_w', 'grad_ssd_d': 'grad_w', 'grad_ssd_norm_w': 'grad_w', 'grad_s5_lam_re': 'grad_w', 'grad_s5_lam_im': 'grad_w', 'grad_s5_log_step': 'grad_w', 'grad_s5_b_re': 'grad_w', 'grad_s5_b_im': 'grad_w', 'grad_s5_c_re': 'grad_w', 'grad_s5_c_im': 'grad_w', 'grad_s5_d': 'grad_w', 'grad_s5_glu_w': 'grad_w', 'grad_s5_glu_b': 'grad_w', 'grad_rg_conv_w': 'grad_w', 'grad_rg_conv_b': 'grad_w', 'grad_rg_wa': 'grad_w', 'grad_rg_ba': 'grad_w', 'grad_rg_wx': 'grad_w', 'grad_rg_bx': 'grad_w', 'grad_rg_lambda': 'grad_w', 'grad_ln1_g': 'grad_w', 'grad_ln1_b': 'grad_w', 'grad_xa_wq': 'grad_w', 'grad_xa_wk': 'grad_w', 'grad_xa_wv': 'grad_w', 'grad_xa_wo': 'grad_w', 'grad_ln2_g': 'grad_w', 'grad_ln2_b': 'grad_w', 'grad_mlp_w1': 'grad_w', 'grad_mlp_w2': 'grad_w', 'grad_ln3_g': 'grad_w', 'grad_ln3_b': 'grad_w', 'delta_w_in': 'delta_w', 'delta_w_out': 'delta_w', 'delta_ssd_conv_w': 'delta_w', 'delta_ssd_conv_b': 'delta_w', 'delta_ssd_dt_bias': 'delta_w', 'delta_ssd_a_log': 'delta_w', 'delta_ssd_d': 'delta_w', 'delta_ssd_norm_w': 'delta_w', 'delta_s5_lam_re': 'delta_w', 'delta_s5_lam_im': 'delta_w', 'delta_s5_log_step': 'delta_w', 'delta_s5_b_re': 'delta_w', 'delta_s5_b_im': 'delta_w', 'delta_s5_c_re': 'delta_w', 'delta_s5_c_im': 'delta_w', 'delta_s5_d': 'delta_w', 'delta_s5_glu_w': 'delta_w', 'delta_s5_glu_b': 'delta_w', 'delta_rg_conv_w': 'delta_w', 'delta_rg_conv_b': 'delta_w', 'delta_rg_wa': 'delta_w', 'delta_rg_ba': 'delta_w', 'delta_rg_wx': 'delta_w', 'delta_rg_bx': 'delta_w', 'delta_rg_lambda': 'delta_w', 'delta_ln1_g': 'delta_w', 'delta_ln1_b': 'delta_w', 'delta_xa_wq': 'delta_w', 'delta_xa_wk': 'delta_w', 'delta_xa_wv': 'delta_w', 'delta_xa_wo': 'delta_w', 'delta_ln2_g': 'delta_w', 'delta_ln2_b': 'delta_w', 'delta_mlp_w1': 'delta_w', 'delta_mlp_w2': 'delta_w', 'delta_ln3_g': 'delta_w', 'delta_ln3_b': 'delta_w', 'new_m_w_in': 'new_m', 'new_m_w_out': 'new_m', 'new_m_ssd_conv_w': 'new_m', 'new_m_ssd_conv_b': 'new_m', 'new_m_ssd_dt_bias': 'new_m', 'new_m_ssd_a_log': 'new_m', 'new_m_ssd_d': 'new_m', 'new_m_ssd_norm_w': 'new_m', 'new_m_s5_lam_re': 'new_m', 'new_m_s5_lam_im': 'new_m', 'new_m_s5_log_step': 'new_m', 'new_m_s5_b_re': 'new_m', 'new_m_s5_b_im': 'new_m', 'new_m_s5_c_re': 'new_m', 'new_m_s5_c_im': 'new_m', 'new_m_s5_d': 'new_m', 'new_m_s5_glu_w': 'new_m', 'new_m_s5_glu_b': 'new_m', 'new_m_rg_conv_w': 'new_m', 'new_m_rg_conv_b': 'new_m', 'new_m_rg_wa': 'new_m', 'new_m_rg_ba': 'new_m', 'new_m_rg_wx': 'new_m', 'new_m_rg_bx': 'new_m', 'new_m_rg_lambda': 'new_m', 'new_m_ln1_g': 'new_m', 'new_m_ln1_b': 'new_m', 'new_m_xa_wq': 'new_m', 'new_m_xa_wk': 'new_m', 'new_m_xa_wv': 'new_m', 'new_m_xa_wo': 'new_m', 'new_m_ln2_g': 'new_m', 'new_m_ln2_b': 'new_m', 'new_m_mlp_w1': 'new_m', 'new_m_mlp_w2': 'new_m', 'new_m_ln3_g': 'new_m', 'new_m_ln3_b': 'new_m', 'new_v_w_in': 'new_v', 'new_v_w_out': 'new_v', 'new_v_ssd_conv_w': 'new_v', 'new_v_ssd_conv_b': 'new_v', 'new_v_ssd_dt_bias': 'new_v', 'new_v_ssd_a_log': 'new_v', 'new_v_ssd_d': 'new_v', 'new_v_ssd_norm_w': 'new_v', 'new_v_s5_lam_re': 'new_v', 'new_v_s5_lam_im': 'new_v', 'new_v_s5_log_step': 'new_v', 'new_v_s5_b_re': 'new_v', 'new_v_s5_b_im': 'new_v', 'new_v_s5_c_re': 'new_v', 'new_v_s5_c_im': 'new_v', 'new_v_s5_d': 'new_v', 'new_v_s5_glu_w': 'new_v', 'new_v_s5_glu_b': 'new_v', 'new_v_rg_conv_w': 'new_v', 'new_v_rg_conv_b': 'new_v', 'new_v_rg_wa': 'new_v', 'new_v_rg_ba': 'new_v', 'new_v_rg_wx': 'new_v', 'new_v_rg_bx': 'new_v', 'new_v_rg_lambda': 'new_v', 'new_v_ln1_g': 'new_v', 'new_v_ln1_b': 'new_v', 'new_v_xa_wq': 'new_v', 'new_v_xa_wk': 'new_v', 'new_v_xa_wv': 'new_v', 'new_v_xa_wo': 'new_v', 'new_v_ln2_g': 'new_v', 'new_v_ln2_b': 'new_v', 'new_v_mlp_w1': 'new_v', 'new_v_mlp_w2': 'new_v', 'new_v_ln3_g': 'new_v', 'new_v_ln3_b': 'new_v'}


def _forward(args):
    return _fwd_reference(*[args[k] for k in FWD_PARAMS])


def _output_shape():
    def fwd():
        inp = _fwd_setup_inputs(0)
        return _fwd_reference(*[inp[k] for k in FWD_PARAMS])
    out = _jax.eval_shape(fwd)
    return out.shape, out.dtype

N_MICROBATCH = 1
ADAM_LR = 0.001
ADAM_B1 = 0.9
ADAM_B2 = 0.999
ADAM_EPS = 1e-08
ADAM_WD = 0.01
ADAM_STEP = 10
PER_EXAMPLE_BATCH_AXIS = {'x': 0, 'mem': 0, 'loss_target': 0}
SHARED_INPUTS = []
_WEIGHT_DTYPES = {'w_in': _jnp.float32, 'w_out': _jnp.float32, 'ssd_conv_w': _jnp.float32, 'ssd_conv_b': _jnp.float32, 'ssd_dt_bias': _jnp.float32, 'ssd_a_log': _jnp.float32, 'ssd_d': _jnp.float32, 'ssd_norm_w': _jnp.float32, 's5_lam_re': _jnp.float32, 's5_lam_im': _jnp.float32, 's5_log_step': _jnp.float32, 's5_b_re': _jnp.float32, 's5_b_im': _jnp.float32, 's5_c_re': _jnp.float32, 's5_c_im': _jnp.float32, 's5_d': _jnp.float32, 's5_glu_w': _jnp.float32, 's5_glu_b': _jnp.float32, 'rg_conv_w': _jnp.float32, 'rg_conv_b': _jnp.float32, 'rg_wa': _jnp.float32, 'rg_ba': _jnp.float32, 'rg_wx': _jnp.float32, 'rg_bx': _jnp.float32, 'rg_lambda': _jnp.float32, 'ln1_g': _jnp.float32, 'ln1_b': _jnp.float32, 'xa_wq': _jnp.float32, 'xa_wk': _jnp.float32, 'xa_wv': _jnp.float32, 'xa_wo': _jnp.float32, 'ln2_g': _jnp.float32, 'ln2_b': _jnp.float32, 'mlp_w1': _jnp.float32, 'mlp_w2': _jnp.float32, 'ln3_g': _jnp.float32, 'ln3_b': _jnp.float32}
MOMENT_SCALE = {'w_in': 6.580967e-02, 'w_out': 1.626278e-01, 'ssd_conv_w': 7.090474e-02, 'ssd_conv_b': 1.270983e-01, 'ssd_dt_bias': 1.579112e-01, 'ssd_a_log': 6.682095e-01, 'ssd_d': 9.146154e-01, 'ssd_norm_w': 9.718053e-02, 's5_lam_re': 1.892135e-03, 's5_lam_im': 1.910305e-03, 's5_log_step': 1.603545e+00, 's5_b_re': 1.141886e-03, 's5_b_im': 1.110744e-03, 's5_c_re': 2.266236e-03, 's5_c_im': 2.298424e-03, 's5_d': 5.362556e-02, 's5_glu_w': 9.700519e-03, 's5_glu_b': 1.856435e-02, 'rg_conv_w': 6.727304e-02, 'rg_conv_b': 6.756143e-01, 'rg_wa': 2.347773e-02, 'rg_ba': 1.561340e-02, 'rg_wx': 4.131258e-02, 'rg_bx': 2.563004e-02, 'rg_lambda': 3.277156e-02, 'ln1_g': 2.123354e+00, 'ln1_b': 9.521677e-01, 'xa_wq': 4.736376e-03, 'xa_wk': 4.763020e-03, 'xa_wv': 1.133722e-02, 'xa_wo': 1.138357e-02, 'ln2_g': 2.127408e+00, 'ln2_b': 9.514541e-01, 'mlp_w1': 3.215546e-02, 'mlp_w2': 8.637132e-02, 'ln3_g': 4.539405e+01, 'ln3_b': 5.104448e+00}


def _to_microbatches(a, axis):
    t = _jnp.moveaxis(a, axis, 0)
    t = t.reshape((N_MICROBATCH, t.shape[0] // N_MICROBATCH) + t.shape[1:])
    return _jnp.moveaxis(t, 1, axis + 1)


def setup_inputs(seed: int = 0) -> dict:
    inp = _fwd_setup_inputs(seed)
    key = _jax.random.fold_in(_jax.random.key(seed), 7919)
    shape, _ = _output_shape()
    out = dict(inp)
    out["loss_target"] = _jax.random.normal(_jax.random.fold_in(key, 0), shape, _jnp.float32)
    for i, name in enumerate(TWIN_WEIGHTS):
        w = inp[name].astype(_jnp.float32)
        if MOMENT_SCALE is None:
            s = _jnp.sqrt(_jnp.mean(_jnp.square(w)) + 1e-30)
        else:
            s = MOMENT_SCALE[name]
        km, kv = _jax.random.split(_jax.random.fold_in(key, i + 1))
        out[name] = w
        out["m_" + name] = s * _jax.random.normal(km, w.shape, _jnp.float32)
        out["v_" + name] = (s * s) * _jax.random.uniform(kv, w.shape, _jnp.float32, 0.5, 1.5)
    if N_MICROBATCH > 1:
        for name, axis in PER_EXAMPLE_BATCH_AXIS.items():
            out[name] = _to_microbatches(out[name], axis)
    return {'x': out['x'], 'mem': out['mem'], 'w_in': out['w_in'], 'w_out': out['w_out'], 'ssd_conv_w': out['ssd_conv_w'], 'ssd_conv_b': out['ssd_conv_b'], 'ssd_dt_bias': out['ssd_dt_bias'], 'ssd_a_log': out['ssd_a_log'], 'ssd_d': out['ssd_d'], 'ssd_norm_w': out['ssd_norm_w'], 's5_lam_re': out['s5_lam_re'], 's5_lam_im': out['s5_lam_im'], 's5_log_step': out['s5_log_step'], 's5_b_re': out['s5_b_re'], 's5_b_im': out['s5_b_im'], 's5_c_re': out['s5_c_re'], 's5_c_im': out['s5_c_im'], 's5_d': out['s5_d'], 's5_glu_w': out['s5_glu_w'], 's5_glu_b': out['s5_glu_b'], 'rg_conv_w': out['rg_conv_w'], 'rg_conv_b': out['rg_conv_b'], 'rg_wa': out['rg_wa'], 'rg_ba': out['rg_ba'], 'rg_wx': out['rg_wx'], 'rg_bx': out['rg_bx'], 'rg_lambda': out['rg_lambda'], 'ln1_g': out['ln1_g'], 'ln1_b': out['ln1_b'], 'xa_wq': out['xa_wq'], 'xa_wk': out['xa_wk'], 'xa_wv': out['xa_wv'], 'xa_wo': out['xa_wo'], 'ln2_g': out['ln2_g'], 'ln2_b': out['ln2_b'], 'mlp_w1': out['mlp_w1'], 'mlp_w2': out['mlp_w2'], 'ln3_g': out['ln3_g'], 'ln3_b': out['ln3_b'], 'loss_target': out['loss_target'], 'm_w_in': out['m_w_in'], 'm_w_out': out['m_w_out'], 'm_ssd_conv_w': out['m_ssd_conv_w'], 'm_ssd_conv_b': out['m_ssd_conv_b'], 'm_ssd_dt_bias': out['m_ssd_dt_bias'], 'm_ssd_a_log': out['m_ssd_a_log'], 'm_ssd_d': out['m_ssd_d'], 'm_ssd_norm_w': out['m_ssd_norm_w'], 'm_s5_lam_re': out['m_s5_lam_re'], 'm_s5_lam_im': out['m_s5_lam_im'], 'm_s5_log_step': out['m_s5_log_step'], 'm_s5_b_re': out['m_s5_b_re'], 'm_s5_b_im': out['m_s5_b_im'], 'm_s5_c_re': out['m_s5_c_re'], 'm_s5_c_im': out['m_s5_c_im'], 'm_s5_d': out['m_s5_d'], 'm_s5_glu_w': out['m_s5_glu_w'], 'm_s5_glu_b': out['m_s5_glu_b'], 'm_rg_conv_w': out['m_rg_conv_w'], 'm_rg_conv_b': out['m_rg_conv_b'], 'm_rg_wa': out['m_rg_wa'], 'm_rg_ba': out['m_rg_ba'], 'm_rg_wx': out['m_rg_wx'], 'm_rg_bx': out['m_rg_bx'], 'm_rg_lambda': out['m_rg_lambda'], 'm_ln1_g': out['m_ln1_g'], 'm_ln1_b': out['m_ln1_b'], 'm_xa_wq': out['m_xa_wq'], 'm_xa_wk': out['m_xa_wk'], 'm_xa_wv': out['m_xa_wv'], 'm_xa_wo': out['m_xa_wo'], 'm_ln2_g': out['m_ln2_g'], 'm_ln2_b': out['m_ln2_b'], 'm_mlp_w1': out['m_mlp_w1'], 'm_mlp_w2': out['m_mlp_w2'], 'm_ln3_g': out['m_ln3_g'], 'm_ln3_b': out['m_ln3_b'], 'v_w_in': out['v_w_in'], 'v_w_out': out['v_w_out'], 'v_ssd_conv_w': out['v_ssd_conv_w'], 'v_ssd_conv_b': out['v_ssd_conv_b'], 'v_ssd_dt_bias': out['v_ssd_dt_bias'], 'v_ssd_a_log': out['v_ssd_a_log'], 'v_ssd_d': out['v_ssd_d'], 'v_ssd_norm_w': out['v_ssd_norm_w'], 'v_s5_lam_re': out['v_s5_lam_re'], 'v_s5_lam_im': out['v_s5_lam_im'], 'v_s5_log_step': out['v_s5_log_step'], 'v_s5_b_re': out['v_s5_b_re'], 'v_s5_b_im': out['v_s5_b_im'], 'v_s5_c_re': out['v_s5_c_re'], 'v_s5_c_im': out['v_s5_c_im'], 'v_s5_d': out['v_s5_d'], 'v_s5_glu_w': out['v_s5_glu_w'], 'v_s5_glu_b': out['v_s5_glu_b'], 'v_rg_conv_w': out['v_rg_conv_w'], 'v_rg_conv_b': out['v_rg_conv_b'], 'v_rg_wa': out['v_rg_wa'], 'v_rg_ba': out['v_rg_ba'], 'v_rg_wx': out['v_rg_wx'], 'v_rg_bx': out['v_rg_bx'], 'v_rg_lambda': out['v_rg_lambda'], 'v_ln1_g': out['v_ln1_g'], 'v_ln1_b': out['v_ln1_b'], 'v_xa_wq': out['v_xa_wq'], 'v_xa_wk': out['v_xa_wk'], 'v_xa_wv': out['v_xa_wv'], 'v_xa_wo': out['v_xa_wo'], 'v_ln2_g': out['v_ln2_g'], 'v_ln2_b': out['v_ln2_b'], 'v_mlp_w1': out['v_mlp_w1'], 'v_mlp_w2': out['v_mlp_w2'], 'v_ln3_g': out['v_ln3_g'], 'v_ln3_b': out['v_ln3_b']}


def _loss(weights, diff, rest, loss_target):
    with _jax.named_scope("forward"):
        args = {**rest, TWIN_DIFF_INPUT: diff, **{k: w.astype(_WEIGHT_DTYPES[k]) for k, w in weights.items()}}
        y = _forward(args)
    with _jax.named_scope("loss_head"):
        err = _jnp.square(y.astype(_jnp.float32) - loss_target)
        return 0.5 * _jnp.sum(_jnp.mean(err, axis=-1)) if err.ndim else 0.5 * err


def _adamw(w, g, m, v):
    m = ADAM_B1 * m + (1.0 - ADAM_B1) * g
    v = ADAM_B2 * v + (1.0 - ADAM_B2) * _jnp.square(g)
    m_hat = m / (1.0 - ADAM_B1 ** ADAM_STEP)
    v_hat = v / (1.0 - ADAM_B2 ** ADAM_STEP)
    delta = -ADAM_LR * (m_hat / (_jnp.sqrt(v_hat) + ADAM_EPS) + ADAM_WD * w)
    return delta, m, v


def reference(x, mem, w_in, w_out, ssd_conv_w, ssd_conv_b, ssd_dt_bias, ssd_a_log, ssd_d, ssd_norm_w, s5_lam_re, s5_lam_im, s5_log_step, s5_b_re, s5_b_im, s5_c_re, s5_c_im, s5_d, s5_glu_w, s5_glu_b, rg_conv_w, rg_conv_b, rg_wa, rg_ba, rg_wx, rg_bx, rg_lambda, ln1_g, ln1_b, xa_wq, xa_wk, xa_wv, xa_wo, ln2_g, ln2_b, mlp_w1, mlp_w2, ln3_g, ln3_b, loss_target, m_w_in, m_w_out, m_ssd_conv_w, m_ssd_conv_b, m_ssd_dt_bias, m_ssd_a_log, m_ssd_d, m_ssd_norm_w, m_s5_lam_re, m_s5_lam_im, m_s5_log_step, m_s5_b_re, m_s5_b_im, m_s5_c_re, m_s5_c_im, m_s5_d, m_s5_glu_w, m_s5_glu_b, m_rg_conv_w, m_rg_conv_b, m_rg_wa, m_rg_ba, m_rg_wx, m_rg_bx, m_rg_lambda, m_ln1_g, m_ln1_b, m_xa_wq, m_xa_wk, m_xa_wv, m_xa_wo, m_ln2_g, m_ln2_b, m_mlp_w1, m_mlp_w2, m_ln3_g, m_ln3_b, v_w_in, v_w_out, v_ssd_conv_w, v_ssd_conv_b, v_ssd_dt_bias, v_ssd_a_log, v_ssd_d, v_ssd_norm_w, v_s5_lam_re, v_s5_lam_im, v_s5_log_step, v_s5_b_re, v_s5_b_im, v_s5_c_re, v_s5_c_im, v_s5_d, v_s5_glu_w, v_s5_glu_b, v_rg_conv_w, v_rg_conv_b, v_rg_wa, v_rg_ba, v_rg_wx, v_rg_bx, v_rg_lambda, v_ln1_g, v_ln1_b, v_xa_wq, v_xa_wk, v_xa_wv, v_xa_wo, v_ln2_g, v_ln2_b, v_mlp_w1, v_mlp_w2, v_ln3_g, v_ln3_b):
    given = dict(x=x, mem=mem, w_in=w_in, w_out=w_out, ssd_conv_w=ssd_conv_w, ssd_conv_b=ssd_conv_b, ssd_dt_bias=ssd_dt_bias, ssd_a_log=ssd_a_log, ssd_d=ssd_d, ssd_norm_w=ssd_norm_w, s5_lam_re=s5_lam_re, s5_lam_im=s5_lam_im, s5_log_step=s5_log_step, s5_b_re=s5_b_re, s5_b_im=s5_b_im, s5_c_re=s5_c_re, s5_c_im=s5_c_im, s5_d=s5_d, s5_glu_w=s5_glu_w, s5_glu_b=s5_glu_b, rg_conv_w=rg_conv_w, rg_conv_b=rg_conv_b, rg_wa=rg_wa, rg_ba=rg_ba, rg_wx=rg_wx, rg_bx=rg_bx, rg_lambda=rg_lambda, ln1_g=ln1_g, ln1_b=ln1_b, xa_wq=xa_wq, xa_wk=xa_wk, xa_wv=xa_wv, xa_wo=xa_wo, ln2_g=ln2_g, ln2_b=ln2_b, mlp_w1=mlp_w1, mlp_w2=mlp_w2, ln3_g=ln3_g, ln3_b=ln3_b, loss_target=loss_target, m_w_in=m_w_in, m_w_out=m_w_out, m_ssd_conv_w=m_ssd_conv_w, m_ssd_conv_b=m_ssd_conv_b, m_ssd_dt_bias=m_ssd_dt_bias, m_ssd_a_log=m_ssd_a_log, m_ssd_d=m_ssd_d, m_ssd_norm_w=m_ssd_norm_w, m_s5_lam_re=m_s5_lam_re, m_s5_lam_im=m_s5_lam_im, m_s5_log_step=m_s5_log_step, m_s5_b_re=m_s5_b_re, m_s5_b_im=m_s5_b_im, m_s5_c_re=m_s5_c_re, m_s5_c_im=m_s5_c_im, m_s5_d=m_s5_d, m_s5_glu_w=m_s5_glu_w, m_s5_glu_b=m_s5_glu_b, m_rg_conv_w=m_rg_conv_w, m_rg_conv_b=m_rg_conv_b, m_rg_wa=m_rg_wa, m_rg_ba=m_rg_ba, m_rg_wx=m_rg_wx, m_rg_bx=m_rg_bx, m_rg_lambda=m_rg_lambda, m_ln1_g=m_ln1_g, m_ln1_b=m_ln1_b, m_xa_wq=m_xa_wq, m_xa_wk=m_xa_wk, m_xa_wv=m_xa_wv, m_xa_wo=m_xa_wo, m_ln2_g=m_ln2_g, m_ln2_b=m_ln2_b, m_mlp_w1=m_mlp_w1, m_mlp_w2=m_mlp_w2, m_ln3_g=m_ln3_g, m_ln3_b=m_ln3_b, v_w_in=v_w_in, v_w_out=v_w_out, v_ssd_conv_w=v_ssd_conv_w, v_ssd_conv_b=v_ssd_conv_b, v_ssd_dt_bias=v_ssd_dt_bias, v_ssd_a_log=v_ssd_a_log, v_ssd_d=v_ssd_d, v_ssd_norm_w=v_ssd_norm_w, v_s5_lam_re=v_s5_lam_re, v_s5_lam_im=v_s5_lam_im, v_s5_log_step=v_s5_log_step, v_s5_b_re=v_s5_b_re, v_s5_b_im=v_s5_b_im, v_s5_c_re=v_s5_c_re, v_s5_c_im=v_s5_c_im, v_s5_d=v_s5_d, v_s5_glu_w=v_s5_glu_w, v_s5_glu_b=v_s5_glu_b, v_rg_conv_w=v_rg_conv_w, v_rg_conv_b=v_rg_conv_b, v_rg_wa=v_rg_wa, v_rg_ba=v_rg_ba, v_rg_wx=v_rg_wx, v_rg_bx=v_rg_bx, v_rg_lambda=v_rg_lambda, v_ln1_g=v_ln1_g, v_ln1_b=v_ln1_b, v_xa_wq=v_xa_wq, v_xa_wk=v_xa_wk, v_xa_wv=v_xa_wv, v_xa_wo=v_xa_wo, v_ln2_g=v_ln2_g, v_ln2_b=v_ln2_b, v_mlp_w1=v_mlp_w1, v_mlp_w2=v_mlp_w2, v_ln3_g=v_ln3_g, v_ln3_b=v_ln3_b)
    weights = {n: given[n] for n in TWIN_WEIGHTS}
    shared = {n: given[n] for n in SHARED_INPUTS}
    per_example = {n: given[n] for n in ['x', 'mem']}
    grad_fn = _jax.value_and_grad(_loss, argnums=(0, 1))

    def one_microbatch(ex, loss_target):
        ex = dict(ex)
        diff = ex.pop(TWIN_DIFF_INPUT)
        return grad_fn(weights, diff, {**shared, **ex}, loss_target)

    if N_MICROBATCH == 1:
        loss, (grad_w, grad_x) = one_microbatch(per_example, given["loss_target"])
    else:
        def body(carry, xs):
            loss_sum, grad_sum = carry
            l_k, (gw_k, gx_k) = one_microbatch(xs[0], xs[1])
            with _jax.named_scope("update"):
                return (loss_sum + l_k, _jax.tree.map(_jnp.add, grad_sum, gw_k)), gx_k

        init = (_jnp.zeros((), _jnp.float32), _jax.tree.map(_jnp.zeros_like, weights))
        (loss, grad_w), grad_x = _jax.lax.scan(body, init, (per_example, given["loss_target"]))
    with _jax.named_scope("update"):
        delta_w, new_m, new_v = {}, {}, {}
        for n in TWIN_WEIGHTS:
            delta_w[n], new_m[n], new_v[n] = _adamw(weights[n], grad_w[n], given["m_" + n], given["v_" + n])
    return (loss, grad_x, *[grad_w[n] for n in TWIN_WEIGHTS], *[delta_w[n] for n in TWIN_WEIGHTS],
            *[new_m[n] for n in TWIN_WEIGHTS], *[new_v[n] for n in TWIN_WEIGHTS])
```

```python
import math

import jax
import jax.numpy as jnp
from jax import lax
from jax.experimental import pallas as pl
from jax.experimental.pallas import tpu as pltpu

F32 = jnp.float32
BF16 = jnp.bfloat16
HI = lax.Precision.HIGHEST

N_DEV = 8
D_MODEL = 1024
DEPTH = 2
SSD_WIDTH = 512
SSD_HEADS = 8
SSD_HEAD_DIM = 64
SSD_STATE = 128
SSD_CHUNK = 128
SSD_XBC = 1024
S5_WIDTH = 256
S5_GROUPS = 16
S5_GROUP_CH = 16
S5_STATE = 64
S5_NSTATE = S5_GROUPS * S5_STATE
RG_WIDTH = 256
RG_BLOCKS = 4
RG_BLOCK_DIM = 64
RG_C = 8.0
XA_HEADS = 4
XA_HEAD_DIM = 256
ALPHA = (2.0 * DEPTH) ** 0.25
LN_EPS = 1e-5
ADAM_LR, ADAM_B1, ADAM_B2, ADAM_EPS, ADAM_WD, ADAM_STEP = 0.001, 0.9, 0.999, 1e-08, 0.01, 10

P_XBC, P_Z, P_U, P_XR, P_G, P_DT = 0, 1024, 1536, 1792, 2048, 2304
D_INP = 2560
LANE = 128
VMEM_LIMIT = 56 * 1024 * 1024
ROW_TILE = 512

_NN = ((1,), (0,))
_NT = ((1,), (1,))
_TN = ((0,), (0,))


def _dot(a, b, dims=_NN, hi=False):
    if hi:
        return lax.dot_general(a, b, (dims, ((), ())), precision=HI, preferred_element_type=F32)
    return lax.dot_general(a.astype(BF16), b.astype(BF16), (dims, ((), ())), preferred_element_type=F32)


def _sigmoid(x):
    return 1.0 / (1.0 + jnp.exp(-x))


def _silu(x):
    return x * _sigmoid(x)


def _dsilu(x):
    s = _sigmoid(x)
    return s * (1.0 + x * (1.0 - s))


_GK = math.sqrt(2.0 / math.pi)
_GC = 0.044715


def _gelu(x):
    return 0.5 * x * (1.0 + jnp.tanh(_GK * (x + _GC * x * x * x)))


def _dgelu(x):
    th = jnp.tanh(_GK * (x + _GC * x * x * x))
    return 0.5 * (1.0 + th) + 0.5 * x * (1.0 - th * th) * _GK * (1.0 + 3.0 * _GC * x * x)


def _log1p_pos(e):
    return jnp.where(e < 1e-2, e * (1.0 - e * (0.5 - e * (1.0 / 3.0))), jnp.log(1.0 + e))


def _softplus(x):
    return jnp.maximum(x, 0.0) + _log1p_pos(jnp.exp(-jnp.abs(x)))


def _neg_expm1(x):
    poly = -x * (1.0 + x * (0.5 + x * (1.0 / 6.0 + x * (1.0 / 24.0 + x * (1.0 / 120.0)))))
    return jnp.where(x > -0.05, poly, 1.0 - jnp.exp(x))


def _params(sem):
    return pltpu.CompilerParams(dimension_semantics=sem, vmem_limit_bytes=VMEM_LIMIT)


def mm(a, b, *, name, ta=False, tb=False, a_extra=(), fa=None, o_extra=(), fo=None,
       a_off=0, m=None, k=None, tm=512, tn=512, tk=512):
    if ta:
        assert k is None
        kdim, m = a.shape[0], (a.shape[1] if m is None else m)
    else:
        assert m is None
        m, kdim = a.shape[0], (a.shape[1] if k is None else k)
    n = b.shape[0] if tb else b.shape[1]
    tm, tn, tk = min(tm, m), min(tn, n), min(tk, kdim)
    assert m % tm == 0 and n % tn == 0 and kdim % tk == 0, (name, m, n, kdim, tm, tn, tk)
    nk = kdim // tk
    na, no = 1 + len(a_extra), len(o_extra)
    if ta:
        assert a_off % tm == 0
        a_spec = pl.BlockSpec((tk, tm), lambda i, j, kk: (kk, i + a_off // tm))
    else:
        assert a_off % tk == 0
        a_spec = pl.BlockSpec((tm, tk), lambda i, j, kk: (i, kk + a_off // tk))
    b_spec = (pl.BlockSpec((tn, tk), lambda i, j, kk: (j, kk)) if tb
              else pl.BlockSpec((tk, tn), lambda i, j, kk: (kk, j)))
    o_spec = pl.BlockSpec((tm, tn), lambda i, j, kk: (i, j))
    dims = ((0,) if ta else (1,), (1,) if tb else (0,))

    def body(*refs):
        a_refs, b_ref = refs[:na], refs[na]
        o_refs = refs[na + 1:na + 1 + no]
        out_ref, acc_ref = refs[na + 1 + no], refs[na + 2 + no]
        kk = pl.program_id(2)

        @pl.when(kk == 0)
        def _():
            acc_ref[...] = jnp.zeros_like(acc_ref)

        av = a_refs[0][...] if fa is None else fa(*[r[...] for r in a_refs])
        acc_ref[...] += _dot(av, b_ref[...], dims)

        @pl.when(kk == nk - 1)
        def _():
            acc = acc_ref[...]
            out_ref[...] = acc if fo is None else fo(acc, *[r[...] for r in o_refs])

    return pl.pallas_call(
        body, name=name, grid=(m // tm, n // tn, nk),
        in_specs=[a_spec] * na + [b_spec] + [o_spec] * no,
        out_specs=o_spec, out_shape=jax.ShapeDtypeStruct((m, n), F32),
        scratch_shapes=[pltpu.VMEM((tm, tn), F32)],
        compiler_params=_params(("parallel", "parallel", "arbitrary")),
    )(a, *a_extra, b, *o_extra)


def rowk(fn, tiled, full, out_w, acc_shapes, *, rows, name):
    tt = min(ROW_TILE, rows)
    n = rows // tt
    assert rows % tt == 0
    nt, nf, no = len(tiled), len(full), len(out_w)

    def tspec(w, cb):
        return pl.BlockSpec((tt, w), lambda i: (i, cb))

    def fspec(a):
        nd = a.ndim
        return pl.BlockSpec(a.shape, lambda i: (0,) * nd)

    def body(*refs):
        ins, fulls = refs[:nt], refs[nt:nt + nf]
        outs, accs = refs[nt + nf:nt + nf + no], refs[nt + nf + no:]
        res_t, res_a = fn(*[r[...] for r in ins], *[r[...] for r in fulls])
        for r, v in zip(outs, res_t):
            r[...] = v
        if accs:
            @pl.when(pl.program_id(0) == 0)
            def _():
                for r in accs:
                    r[...] = jnp.zeros_like(r)
            for r, v in zip(accs, res_a):
                r[...] += v

    outs = pl.pallas_call(
        body, name=name, grid=(n,),
        in_specs=[tspec(w, cb) for (_, w, cb) in tiled] + [fspec(a) for a in full],
        out_specs=[tspec(w, 0) for w in out_w] + [pl.BlockSpec(s, lambda i, nd=len(s): (0,) * nd) for s in acc_shapes],
        out_shape=[jax.ShapeDtypeStruct((rows, w), F32) for w in out_w] + [jax.ShapeDtypeStruct(s, F32) for s in acc_shapes],
        compiler_params=_params(("arbitrary",)),
    )(*[a for (a, _, _) in tiled], *full)
    return outs[:no], outs[no:]


def _colsum(x):
    return jnp.sum(x, axis=0, keepdims=True)


def _rowsum(x):
    return jnp.sum(x, axis=1, keepdims=True)


def _ln_fwd_fn(resid, y, g, b):
    pre = ALPHA * resid + y
    mu = jnp.mean(pre, axis=1, keepdims=True)
    xc = pre - mu
    var = jnp.mean(xc * xc, axis=1, keepdims=True)
    return (xc * lax.rsqrt(var + LN_EPS) * g + b,), ()


def _ln_bwd_fn(resid, y, dout, g):
    pre = ALPHA * resid + y
    mu = jnp.mean(pre, axis=1, keepdims=True)
    xc = pre - mu
    var = jnp.mean(xc * xc, axis=1, keepdims=True)
    rstd = lax.rsqrt(var + LN_EPS)
    xhat = xc * rstd
    dxh = dout * g
    dpre = rstd * (dxh - jnp.mean(dxh, axis=1, keepdims=True) - xhat * jnp.mean(dxh * xhat, axis=1, keepdims=True))
    return (dpre,), (_colsum(dout * xhat), _colsum(dout))


def ln_fwd(resid, y, g, b, *, name):
    (out,), _ = rowk(_ln_fwd_fn, [(resid, D_MODEL, 0), (y, D_MODEL, 0)], [g, b], [D_MODEL], [],
                     rows=resid.shape[0], name=name)
    return out


def ln_bwd(resid, y, dout, g, *, name):
    (dpre,), (dg, db) = rowk(_ln_bwd_fn, [(resid, D_MODEL, 0), (y, D_MODEL, 0), (dout, D_MODEL, 0)], [g],
                             [D_MODEL], [(1, D_MODEL), (1, D_MODEL)], rows=resid.shape[0], name=name)
    return dpre, dg, db


def _loss_fn(y, tgt):
    e = y - tgt
    part = _colsum(_rowsum(e * e)) * (0.5 / D_MODEL)
    return (e * (1.0 / D_MODEL),), (part,)


_XA_SCALE = 1.0 / math.sqrt(XA_HEAD_DIM)


def _attn_probs(qh, kh):
    s = _dot(qh, kh, _NT) * _XA_SCALE
    e = jnp.exp(s - jnp.max(s, axis=1, keepdims=True))
    return e / _rowsum(e)


def _attn_fwd_fn(q, k, v):
    outs = []
    for hd in range(XA_HEADS):
        sl = slice(hd * XA_HEAD_DIM, (hd + 1) * XA_HEAD_DIM)
        outs.append(_dot(_attn_probs(q[:, sl], k[:, sl]), v[:, sl]))
    return (jnp.concatenate(outs, axis=1),), ()


def _attn_bwd_fn(q, do, k, v):
    dqs, dks, dvs = [], [], []
    for hd in range(XA_HEADS):
        sl = slice(hd * XA_HEAD_DIM, (hd + 1) * XA_HEAD_DIM)
        qh, kh, vh, doh = q[:, sl], k[:, sl], v[:, sl], do[:, sl]
        p = _attn_probs(qh, kh)
        dp = _dot(doh, vh, _NT)
        ds = p * (dp - _rowsum(p * dp)) * _XA_SCALE
        dqs.append(_dot(ds, kh))
        dks.append(_dot(ds, qh, _TN))
        dvs.append(_dot(p, doh, _TN))
    cat = lambda xs: jnp.concatenate(xs, axis=1)
    return (cat(dqs),), (cat(dks), cat(dvs))


def _s5_post_fwd_fn(ylin, u, dskip, gw, gb):
    yg = _gelu(ylin + dskip * u)
    return (yg * _sigmoid(_dot(yg, gw) + gb),), ()


def _s5_post_bwd_fn(ylin, u, dout, dskip, gw, gb):
    pre = ylin + dskip * u
    yg = _gelu(pre)
    sg = _sigmoid(_dot(yg, gw) + gb)
    dlin = dout * yg * sg * (1.0 - sg)
    dyg = dout * sg + _dot(dlin, gw, _NT)
    dpre = dyg * _dgelu(pre)
    return (dpre, dpre * dskip), (_colsum(dpre * u), _dot(yg, dlin, _TN), _colsum(dlin))


def _rg_gates(xc, wa, wx, ba, bx, lam):
    r = _sigmoid(_dot(xc, wa) + ba)
    i = _sigmoid(_dot(xc, wx) + bx)
    sp = _softplus(-lam)
    log_a = -RG_C * r * sp
    a = jnp.exp(log_a)
    mult = jnp.sqrt(_neg_expm1(2.0 * log_a))
    return r, i, sp, a, mult


def _rg_pre_fwd_fn(xc, wa, wx, ba, bx, lam):
    r, i, sp, a, mult = _rg_gates(xc, wa, wx, ba, bx, lam)
    return (a, mult * (i * xc)), ()


def _rg_pre_bwd_fn(xc, gsc, hprev, wa, wx, ba, bx, lam):
    r, i, sp, a, mult = _rg_gates(xc, wa, wx, ba, bx, lam)
    da = gsc * hprev
    db = gsc
    dmult = db * i * xc
    di = db * mult * xc
    dxc = db * mult * i
    dlog_a = da * a - a * a * dmult / mult
    dr = dlog_a * (-RG_C * sp)
    dsp = _colsum(dlog_a * (-RG_C * r))
    dlam = dsp * (-_sigmoid(-lam))
    dpr = dr * r * (1.0 - r)
    dpi = di * i * (1.0 - i)
    dxc = dxc + _dot(dpr, wa, _NT) + _dot(dpi, wx, _NT)
    return (dxc,), (_dot(xc, dpr, _TN), _dot(xc, dpi, _TN), _colsum(dpr), _colsum(dpi), dlam)


def _rg_out_fwd_fn(h, g):
    return (h * _gelu(g),), ()


def _rg_out_bwd_fn(h, g, dy):
    return (dy * _gelu(g), dy * h * _dgelu(g)), ()


def _shift_down(x, prev, j, rows):
    return jnp.where(rows < j, pltpu.roll(prev, j, 0), pltpu.roll(x, j, 0))


def _shift_up(x, nxt, j, rows):
    t = x.shape[0]
    return jnp.where(rows >= t - j, pltpu.roll(nxt, t - j, 0), pltpu.roll(x, t - j, 0))


def conv_fwd(src, cb, w, b, *, width, act, name):
    t = src.shape[0]
    tt = min(ROW_TILE, t)
    n = t // tt

    def body(x_ref, w_ref, b_ref, y_ref, prev_ref):
        @pl.when(pl.program_id(0) == 0)
        def _():
            prev_ref[...] = jnp.zeros_like(prev_ref)

        x = x_ref[...]
        prev = prev_ref[...]
        rows = lax.broadcasted_iota(jnp.int32, x.shape, 0)
        wv = w_ref[...]
        y = b_ref[...] + wv[3:4, :] * x
        for j in (1, 2, 3):
            y = y + wv[3 - j:4 - j, :] * _shift_down(x, prev, j, rows)
        y_ref[...] = _silu(y) if act else y
        prev_ref[...] = x

    return pl.pallas_call(
        body, name=name, grid=(n,),
        in_specs=[pl.BlockSpec((tt, width), lambda i: (i, cb)),
                  pl.BlockSpec((4, width), lambda i: (0, 0)), pl.BlockSpec((1, width), lambda i: (0, 0))],
        out_specs=pl.BlockSpec((tt, width), lambda i: (i, 0)),
        out_shape=jax.ShapeDtypeStruct((t, width), F32),
        scratch_shapes=[pltpu.VMEM((tt, width), F32)],
        compiler_params=_params(("arbitrary",)),
    )(src, w, b)


def conv_bwd(src, cb, dy, w, b, *, width, act, name):
    t = src.shape[0]
    tt = min(ROW_TILE, t)
    n = t // tt

    def body(x_ref, xp_ref, dy_ref, w_ref, b_ref, dx_ref, dw_ref, db_ref, nxt_ref):
        i = pl.program_id(0)

        @pl.when(i == 0)
        def _():
            nxt_ref[...] = jnp.zeros_like(nxt_ref)
            dw_ref[...] = jnp.zeros_like(dw_ref)
            db_ref[...] = jnp.zeros_like(db_ref)

        x = x_ref[...]
        prev = jnp.where(i == n - 1, 0.0, xp_ref[...])
        rows = lax.broadcasted_iota(jnp.int32, x.shape, 0)
        wv = w_ref[...]
        xs = [x] + [_shift_down(x, prev, j, rows) for j in (1, 2, 3)]
        dpre = dy_ref[...]
        if act:
            pre = b_ref[...] + wv[3:4, :] * xs[0]
            for j in (1, 2, 3):
                pre = pre + wv[3 - j:4 - j, :] * xs[j]
            dpre = dpre * _dsilu(pre)
        nxt = nxt_ref[...]
        dx = wv[3:4, :] * dpre
        for j in (1, 2, 3):
            dx = dx + wv[3 - j:4 - j, :] * _shift_up(dpre, nxt, j, rows)
        dx_ref[...] = dx
        dw_ref[...] += jnp.concatenate([_colsum(dpre * xs[3 - kk]) for kk in range(4)], axis=0)
        db_ref[...] += _colsum(dpre)
        nxt_ref[...] = dpre

    return pl.pallas_call(
        body, name=name, grid=(n,),
        in_specs=[pl.BlockSpec((tt, width), lambda i: (n - 1 - i, cb)),
                  pl.BlockSpec((tt, width), lambda i: (jnp.maximum(n - 2 - i, 0), cb)),
                  pl.BlockSpec((tt, width), lambda i: (n - 1 - i, 0)),
                  pl.BlockSpec((4, width), lambda i: (0, 0)), pl.BlockSpec((1, width), lambda i: (0, 0))],
        out_specs=[pl.BlockSpec((tt, width), lambda i: (n - 1 - i, 0)),
                   pl.BlockSpec((4, width), lambda i: (0, 0)), pl.BlockSpec((1, width), lambda i: (0, 0))],
        out_shape=[jax.ShapeDtypeStruct((t, width), F32), jax.ShapeDtypeStruct((4, width), F32),
                   jax.ShapeDtypeStruct((1, width), F32)],
        scratch_shapes=[pltpu.VMEM((tt, width), F32)],
        compiler_params=_params(("arbitrary",)),
    )(src, src, dy, w, b)


SCAN_CW = 512


def scan_complex(bu, lam, *, reverse, name):
    t, w2 = bu.shape
    w = w2 // 2
    tt = min(ROW_TILE, t)
    n, nb, cw = t // tt, tt // 8, min(SCAN_CW, w)

    def body(b_ref, lam_ref, o_ref, st_ref):
        @pl.when(pl.program_id(0) == 0)
        def _():
            st_ref[...] = jnp.zeros_like(st_ref)

        rows = lax.broadcasted_iota(jnp.int32, (8, cw), 0)
        for c0 in range(0, w, cw):
            re, im = pl.ds(c0, cw), pl.ds(w + c0, cw)
            ar = jnp.broadcast_to(lam_ref[:, re], (8, cw))
            ai = jnp.broadcast_to(lam_ref[:, im], (8, cw))

            def blk(i, carry):
                hr, hi = carry
                base = pl.multiple_of((nb - 1 - i if reverse else i) * 8, 8)
                tr, ti = b_ref[pl.ds(base, 8), re], b_ref[pl.ds(base, 8), im]
                outr, outi = jnp.zeros((8, cw), F32), jnp.zeros((8, cw), F32)
                for j in (range(7, -1, -1) if reverse else range(8)):
                    br = jnp.broadcast_to(tr[j:j + 1, :], (8, cw))
                    bi = jnp.broadcast_to(ti[j:j + 1, :], (8, cw))
                    hr, hi = ar * hr - ai * hi + br, ar * hi + ai * hr + bi
                    outr = jnp.where(rows == j, hr, outr)
                    outi = jnp.where(rows == j, hi, outi)
                o_ref[pl.ds(base, 8), re] = outr
                o_ref[pl.ds(base, 8), im] = outi
                return hr, hi

            hr, hi = lax.fori_loop(0, nb, blk, (st_ref[:, re], st_ref[:, im]))
            st_ref[:, re] = hr
            st_ref[:, im] = hi

    idx = (lambda i: (n - 1 - i, 0)) if reverse else (lambda i: (i, 0))
    return pl.pallas_call(
        body, name=name, grid=(n,),
        in_specs=[pl.BlockSpec((tt, w2), idx), pl.BlockSpec((1, w2), lambda i: (0, 0))],
        out_specs=pl.BlockSpec((tt, w2), idx), out_shape=jax.ShapeDtypeStruct((t, w2), F32),
        scratch_shapes=[pltpu.VMEM((8, w2), F32)],
        compiler_params=_params(("arbitrary",)),
    )(bu, lam)


def scan_real(a, b, *, reverse, name):
    t, w = b.shape
    tt = min(ROW_TILE, t)
    n, nb = t // tt, tt // 8

    def body(a_ref, b_ref, o_ref, st_ref):
        @pl.when(pl.program_id(0) == 0)
        def _():
            st_ref[...] = jnp.zeros_like(st_ref)

        rows = lax.broadcasted_iota(jnp.int32, (8, w), 0)

        def blk(i, h):
            base = pl.multiple_of((nb - 1 - i if reverse else i) * 8, 8)
            ta_, tb_ = a_ref[pl.ds(base, 8), :], b_ref[pl.ds(base, 8), :]
            out = jnp.zeros((8, w), F32)
            for j in (range(7, -1, -1) if reverse else range(8)):
                h = jnp.broadcast_to(ta_[j:j + 1, :], (8, w)) * h + jnp.broadcast_to(tb_[j:j + 1, :], (8, w))
                out = jnp.where(rows == j, h, out)
            o_ref[pl.ds(base, 8), :] = out
            return h

        st_ref[...] = lax.fori_loop(0, nb, blk, st_ref[...])

    idx = (lambda i: (n - 1 - i, 0)) if reverse else (lambda i: (i, 0))
    return pl.pallas_call(
        body, name=name, grid=(n,),
        in_specs=[pl.BlockSpec((tt, w), idx), pl.BlockSpec((tt, w), idx)],
        out_specs=pl.BlockSpec((tt, w), idx), out_shape=jax.ShapeDtypeStruct((t, w), F32),
        scratch_shapes=[pltpu.VMEM((8, w), F32)],
        compiler_params=_params(("arbitrary",)),
    )(a, b)


def s5_dlam(g, hprev, *, name):
    t, w2 = g.shape
    w = w2 // 2

    def fn(gt, ht):
        gr, gi, hr, hi = gt[:, :w], gt[:, w:], ht[:, :w], ht[:, w:]
        return (), (_colsum(gr * hr + gi * hi), _colsum(gi * hr - gr * hi))

    _, (dar, dai) = rowk(fn, [(g, w2, 0), (hprev, w2, 0)], [], [], [(1, w), (1, w)], rows=t, name=name)
    return dar, dai


def _ssd_common(dt_ref, dtT_ref, prow_ref, pcol_ref):
    q = SSD_CHUNK
    r = lax.broadcasted_iota(jnp.int32, (q, q), 0)
    c = lax.broadcasted_iota(jnp.int32, (q, q), 1)
    low = r >= c
    tril = low.astype(F32)
    triu = (r <= c).astype(F32)
    bias_r, alog_r = prow_ref[0:1, :], prow_ref[1:2, :]
    raw_c = dt_ref[...] + bias_r
    dt_c = _softplus(raw_c)
    a_r = -jnp.exp(alog_r)
    cs_c = _dot(tril, dt_c * a_r, hi=True)
    dt_r = _softplus(dtT_ref[...] + pcol_ref[:, 0:1])
    cs_r = _dot(dt_r * (-jnp.exp(pcol_ref[:, 1:2])), triu, hi=True)
    return low, tril, triu, raw_c, dt_c, a_r, cs_c, cs_r


def _ssd_gate(yraw, z, nw):
    yg = yraw * _silu(z)
    r = lax.rsqrt(jnp.mean(yg * yg, axis=1, keepdims=True) + LN_EPS)
    return yg, r


def ssd_fwd(xbc, proj, dtT, prow, pcol, nw, *, name):
    t = xbc.shape[0]
    q, p, ns = SSD_CHUNK, SSD_HEAD_DIM, SSD_STATE
    nc = t // q

    def body(xbc_ref, z_ref, dt_ref, dtT_ref, prow_ref, pcol_ref, nw_ref, y_ref, yraw_ref, sall_ref, s_ref, ybuf):
        @pl.when(pl.program_id(0) == 0)
        def _():
            s_ref[...] = jnp.zeros_like(s_ref)

        sall_ref[0] = s_ref[...]
        low, tril, triu, raw_c, dt_c, a_r, cs_c, cs_r = _ssd_common(dt_ref, dtT_ref, prow_ref, pcol_ref)
        d_r = prow_ref[2:3, :]
        bm = [xbc_ref[:, pl.ds(SSD_WIDTH + g * ns, ns)] for g in range(2)]
        cm = [xbc_ref[:, pl.ds(SSD_WIDTH + 2 * ns + g * ns, ns)] for g in range(2)]
        cb = [_dot(cm[g], bm[g], _NT, hi=True) for g in range(2)]
        for h in range(SSD_HEADS):
            g = h // 4
            hs = pl.ds(h * p, p)
            csc, csr = cs_c[:, h:h + 1], cs_r[h:h + 1, :]
            lmat = jnp.exp(jnp.where(low, csc - csr, -1e30))
            xs = xbc_ref[:, hs]
            xdt = xs * dt_c[:, h:h + 1]
            sh = s_ref[hs, :]
            y = (_dot(cb[g] * lmat, xdt, hi=True) + jnp.exp(csc) * _dot(cm[g], sh, _NT, hi=True)
                 + xs * d_r[:, h:h + 1])
            ybuf[:, hs] = y
            cl = csc[q - 1:q, :]
            s_ref[hs, :] = jnp.exp(cl) * sh + _dot(xdt * jnp.exp(cl - csc), bm[g], _TN, hi=True)
        yraw = ybuf[...]
        yraw_ref[...] = yraw
        yg, r = _ssd_gate(yraw, z_ref[...], nw_ref[...])
        y_ref[...] = yg * r * nw_ref[...]

    return pl.pallas_call(
        body, name=name, grid=(nc,),
        in_specs=[pl.BlockSpec((q, SSD_XBC), lambda i: (i, 0)),
                  pl.BlockSpec((q, SSD_WIDTH), lambda i: (i, P_Z // SSD_WIDTH)),
                  pl.BlockSpec((q, LANE), lambda i: (i, P_DT // LANE)),
                  pl.BlockSpec((SSD_HEADS, q), lambda i: (0, i)),
                  pl.BlockSpec((8, LANE), lambda i: (0, 0)), pl.BlockSpec((8, LANE), lambda i: (0, 0)),
                  pl.BlockSpec((1, SSD_WIDTH), lambda i: (0, 0))],
        out_specs=[pl.BlockSpec((q, SSD_WIDTH), lambda i: (i, 0)), pl.BlockSpec((q, SSD_WIDTH), lambda i: (i, 0)),
                   pl.BlockSpec((1, SSD_WIDTH, ns), lambda i: (i, 0, 0))],
        out_shape=[jax.ShapeDtypeStruct((t, SSD_WIDTH), F32), jax.ShapeDtypeStruct((t, SSD_WIDTH), F32),
                   jax.ShapeDtypeStruct((nc, SSD_WIDTH, ns), F32)],
        scratch_shapes=[pltpu.VMEM((SSD_WIDTH, ns), F32), pltpu.VMEM((q, SSD_WIDTH), F32)],
        compiler_params=_params(("arbitrary",)),
    )(xbc, proj, proj, dtT, prow, pcol, nw)


def ssd_bwd(xbc, proj, dtT, prow, pcol, nw, yraw, sall, dout, *, name):
    t = xbc.shape[0]
    q, p, ns = SSD_CHUNK, SSD_HEAD_DIM, SSD_STATE
    nc = t // q

    def body(xbc_ref, z_ref, dt_ref, dtT_ref, prow_ref, pcol_ref, nw_ref, yraw_ref, sall_ref, dout_ref,
             dxbc_ref, dz_ref, ddt_ref, dprm_ref, dnw_ref, ds_ref, dyb):
        @pl.when(pl.program_id(0) == 0)
        def _():
            ds_ref[...] = jnp.zeros_like(ds_ref)
            dprm_ref[...] = jnp.zeros_like(dprm_ref)
            dnw_ref[...] = jnp.zeros_like(dnw_ref)

        yraw, z, nwv, dout = yraw_ref[...], z_ref[...], nw_ref[...], dout_ref[...]
        yg, r = _ssd_gate(yraw, z, nwv)
        dnw_ref[...] += _colsum(dout * yg * r)
        dyn = dout * nwv
        dyg = r * dyn - yg * (r * r * r) * jnp.mean(dyn * yg, axis=1, keepdims=True)
        dyb[...] = dyg * _silu(z)
        dz_ref[...] = dyg * yraw * _dsilu(z)

        low, tril, triu, raw_c, dt_c, a_r, cs_c, cs_r = _ssd_common(dt_ref, dtT_ref, prow_ref, pcol_ref)
        d_r = prow_ref[2:3, :]
        lane = lax.broadcasted_iota(jnp.int32, (1, LANE), 1)
        ones = jnp.ones((q, LANE), F32)
        last = (lax.broadcasted_iota(jnp.int32, (q, 1), 0) == q - 1).astype(F32)
        bm = [xbc_ref[:, pl.ds(SSD_WIDTH + g * ns, ns)] for g in range(2)]
        cm = [xbc_ref[:, pl.ds(SSD_WIDTH + 2 * ns + g * ns, ns)] for g in range(2)]
        cb = [_dot(cm[g], bm[g], _NT, hi=True) for g in range(2)]
        dbm = [jnp.zeros((q, ns), F32) for _ in range(2)]
        dcm = [jnp.zeros((q, ns), F32) for _ in range(2)]
        dcs_all = jnp.zeros((q, LANE), F32)
        ddt_all = jnp.zeros((q, LANE), F32)
        dd_all = jnp.zeros((1, LANE), F32)
        for h in range(SSD_HEADS):
            g = h // 4
            hs = pl.ds(h * p, p)
            onehot = (lane == h).astype(F32)
            csc, csr = cs_c[:, h:h + 1], cs_r[h:h + 1, :]
            lmat = jnp.exp(jnp.where(low, csc - csr, -1e30))
            xs = xbc_ref[:, hs]
            dth = dt_c[:, h:h + 1]
            xdt = xs * dth
            sh = sall_ref[0, hs, :]
            dy = dyb[:, hs]
            ecs = jnp.exp(csc)
            cl = csc[q - 1:q, :]
            ecl = jnp.exp(cl)
            wdec = jnp.exp(cl - csc)
            wmat = cb[g] * lmat
            dwm = _dot(dy, xdt, _NT, hi=True)
            dx = _dot(wmat, dy, _TN, hi=True)
            emat = dwm * wmat
            dmm = dwm * lmat
            dcm[g] = dcm[g] + _dot(dmm, bm[g], hi=True)
            dbm[g] = dbm[g] + _dot(dmm, cm[g], _TN, hi=True)
            dcs = _rowsum(emat) - _dot(emat, ones, _TN, hi=True)[:, 0:1]
            zmat = _dot(cm[g], sh, _NT, hi=True)
            dzm = ecs * dy
            dcm[g] = dcm[g] + _dot(dzm, sh, hi=True)
            dsp = _dot(dzm, cm[g], _TN, hi=True)
            dcs = dcs + _rowsum(dzm * zmat)
            dsn = ds_ref[hs, :]
            dsp = dsp + ecl * dsn
            dcl = _colsum(_rowsum(dsn * sh)) * ecl
            xw = xdt * wdec
            dxw = _dot(bm[g], dsn, _NT, hi=True)
            dbm[g] = dbm[g] + _dot(xw, dsn, hi=True)
            dx = dx + wdec * dxw
            tw = _rowsum(dxw * xdt) * wdec
            dcl = dcl + _colsum(tw)
            dcs = dcs - tw + last * dcl
            ds_ref[hs, :] = dsp
            dxbc_ref[:, hs] = dx * dth + dy * d_r[:, h:h + 1]
            dcs_all = dcs_all + dcs * onehot
            ddt_all = ddt_all + _rowsum(dx * xs) * onehot
            dd_all = dd_all + _colsum(_rowsum(dy * xs)) * onehot
        for g in range(2):
            dxbc_ref[:, pl.ds(SSD_WIDTH + g * ns, ns)] = dbm[g]
            dxbc_ref[:, pl.ds(SSD_WIDTH + 2 * ns + g * ns, ns)] = dcm[g]
        dadt = _dot(triu, dcs_all, hi=True)
        ddt = ddt_all + dadt * a_r
        draw = ddt * _sigmoid(raw_c)
        ddt_ref[...] = draw
        zero = jnp.zeros((5, LANE), F32)
        dprm_ref[...] += jnp.concatenate([_colsum(draw), _colsum(dadt * dt_c) * a_r, dd_all, zero], axis=0)

    rev = lambda cbk: (lambda i: (nc - 1 - i, cbk))
    return pl.pallas_call(
        body, name=name, grid=(nc,),
        in_specs=[pl.BlockSpec((q, SSD_XBC), rev(0)),
                  pl.BlockSpec((q, SSD_WIDTH), rev(P_Z // SSD_WIDTH)),
                  pl.BlockSpec((q, LANE), rev(P_DT // LANE)),
                  pl.BlockSpec((SSD_HEADS, q), lambda i: (0, nc - 1 - i)),
                  pl.BlockSpec((8, LANE), lambda i: (0, 0)), pl.BlockSpec((8, LANE), lambda i: (0, 0)),
                  pl.BlockSpec((1, SSD_WIDTH), lambda i: (0, 0)),
                  pl.BlockSpec((q, SSD_WIDTH), rev(0)),
                  pl.BlockSpec((1, SSD_WIDTH, ns), lambda i: (nc - 1 - i, 0, 0)),
                  pl.BlockSpec((q, SSD_WIDTH), rev(0))],
        out_specs=[pl.BlockSpec((q, SSD_XBC), rev(0)), pl.BlockSpec((q, SSD_WIDTH), rev(0)),
                   pl.BlockSpec((q, LANE), rev(0)),
                   pl.BlockSpec((8, LANE), lambda i: (0, 0)), pl.BlockSpec((1, SSD_WIDTH), lambda i: (0, 0))],
        out_shape=[jax.ShapeDtypeStruct((t, SSD_XBC), F32), jax.ShapeDtypeStruct((t, SSD_WIDTH), F32),
                   jax.ShapeDtypeStruct((t, LANE), F32), jax.ShapeDtypeStruct((8, LANE), F32),
                   jax.ShapeDtypeStruct((1, SSD_WIDTH), F32)],
        scratch_shapes=[pltpu.VMEM((SSD_WIDTH, ns), F32), pltpu.VMEM((q, SSD_WIDTH), F32)],
        compiler_params=_params(("arbitrary",)),
    )(xbc, proj, proj, dtT, prow, pcol, nw, yraw, sall, dout)


def _me():
    return lax.axis_index("x"), lax.axis_index("y"), lax.axis_index("c")


def _peer(x, y, c, p):
    return (1 - x if p & 4 else x), (1 - y if p & 2 else y), (1 - c if p & 1 else c)


_COMM_SCRATCH = [pltpu.SemaphoreType.DMA((N_DEV - 1,)), pltpu.SemaphoreType.DMA((N_DEV - 1,)),
                 pltpu.SemaphoreType.DMA(())]
_ANY = pl.BlockSpec(memory_space=pl.ANY)


def all_gather(block, *, name):
    def body(src, dst, send_sems, recv_sems, local_sem):
        x, y, c = _me()
        me = 4 * x + 2 * y + c
        local = pltpu.make_async_copy(src, dst.at[me], local_sem)
        local.start()
        copies = []
        for p in range(1, N_DEV):
            cp = pltpu.make_async_remote_copy(
                src_ref=src, dst_ref=dst.at[me], send_sem=send_sems.at[p - 1], recv_sem=recv_sems.at[p - 1],
                device_id=_peer(x, y, c, p), device_id_type=pl.DeviceIdType.MESH)
            cp.start()
            copies.append(cp)
        for cp in copies:
            cp.wait()
        local.wait()

    return pl.pallas_call(
        body, name=name, in_specs=[_ANY], out_specs=_ANY,
        out_shape=jax.ShapeDtypeStruct((N_DEV,) + block.shape, block.dtype),
        scratch_shapes=list(_COMM_SCRATCH),
    )(block)


def all_to_all(slabs, *, name):
    def body(src, dst, send_sems, recv_sems, local_sem):
        x, y, c = _me()
        me = 4 * x + 2 * y + c
        local = pltpu.make_async_copy(src.at[me], dst.at[me], local_sem)
        local.start()
        copies = []
        for p in range(1, N_DEV):
            px, py, pc = _peer(x, y, c, p)
            cp = pltpu.make_async_remote_copy(
                src_ref=src.at[4 * px + 2 * py + pc], dst_ref=dst.at[me],
                send_sem=send_sems.at[p - 1], recv_sem=recv_sems.at[p - 1],
                device_id=(px, py, pc), device_id_type=pl.DeviceIdType.MESH)
            cp.start()
            copies.append(cp)
        for cp in copies:
            cp.wait()
        local.wait()

    return pl.pallas_call(
        body, name=name, in_specs=[_ANY], out_specs=_ANY,
        out_shape=jax.ShapeDtypeStruct(slabs.shape, slabs.dtype),
        scratch_shapes=list(_COMM_SCRATCH),
    )(slabs)


def adamw(slabs, w, m, v, *, name, tt):
    r = w.shape[0]
    tt = min(tt, r)
    assert r % tt == 0

    def body(s_ref, w_ref, m_ref, v_ref, g_ref, d_ref, nm_ref, nv_ref):
        g = s_ref[0]
        for kdev in range(1, N_DEV):
            g = g + s_ref[kdev]
        wv = w_ref[...]
        nm = ADAM_B1 * m_ref[...] + (1.0 - ADAM_B1) * g
        nv = ADAM_B2 * v_ref[...] + (1.0 - ADAM_B2) * (g * g)
        m_hat = nm / (1.0 - ADAM_B1 ** ADAM_STEP)
        v_hat = nv / (1.0 - ADAM_B2 ** ADAM_STEP)
        g_ref[...] = g
        d_ref[...] = -ADAM_LR * (m_hat / (jnp.sqrt(v_hat) + ADAM_EPS) + ADAM_WD * wv)
        nm_ref[...] = nm
        nv_ref[...] = nv

    spec = pl.BlockSpec((tt, LANE), lambda i: (i, 0))
    return pl.pallas_call(
        body, name=name, grid=(r // tt,),
        in_specs=[pl.BlockSpec((N_DEV, tt, LANE), lambda i: (0, i, 0)), spec, spec, spec],
        out_specs=[spec] * 4, out_shape=[jax.ShapeDtypeStruct((r, LANE), F32)] * 4,
        compiler_params=_params(("parallel",)),
    )(slabs, w, m, v)


SHARDED = [("w_in", 2), ("w_out", 1), ("ssd_conv_w", 2), ("s5_glu_w", 1), ("rg_conv_w", 2),
           ("xa_wq", 1), ("xa_wk", 1), ("xa_wv", 1), ("xa_wo", 1), ("mlp_w1", 2), ("mlp_w2", 1)]
KEEP_F32 = ("ssd_conv_w", "rg_conv_w")
SMALL = ["ssd_conv_b", "ssd_dt_bias", "ssd_a_log", "ssd_d", "ssd_norm_w", "s5_lam_re", "s5_lam_im",
         "s5_log_step", "s5_b_re", "s5_b_im", "s5_c_re", "s5_c_im", "s5_d", "s5_glu_b", "rg_conv_b",
         "rg_wa", "rg_ba", "rg_wx", "rg_bx", "rg_lambda", "ln1_g", "ln1_b", "ln2_g", "ln2_b", "ln3_g", "ln3_b"]
WEIGHTS = ['w_in', 'w_out', 'ssd_conv_w', 'ssd_conv_b', 'ssd_dt_bias', 'ssd_a_log', 'ssd_d', 'ssd_norm_w',
           's5_lam_re', 's5_lam_im', 's5_log_step', 's5_b_re', 's5_b_im', 's5_c_re', 's5_c_im', 's5_d',
           's5_glu_w', 's5_glu_b', 'rg_conv_w', 'rg_conv_b', 'rg_wa', 'rg_ba', 'rg_wx', 'rg_bx', 'rg_lambda',
           'ln1_g', 'ln1_b', 'xa_wq', 'xa_wk', 'xa_wv', 'xa_wo', 'ln2_g', 'ln2_b', 'mlp_w1', 'mlp_w2',
           'ln3_g', 'ln3_b']


def _pack_rows(flat, mult):
    n = flat.shape[-1]
    r = -(-n // (LANE * mult)) * mult
    pad = [(0, 0)] * (flat.ndim - 1) + [(0, r * LANE - n)]
    return jnp.pad(flat, pad).reshape(flat.shape[:-1] + (r, LANE))


def _unpack(packed, shapes):
    lead = packed.shape[:-2]
    flat = packed.reshape(lead + (-1,))
    out, off = [], 0
    for s in shapes:
        n = math.prod(s)
        out.append(flat[..., off:off + n].reshape(lead + tuple(s)))
        off += n
    return out


def _to_full(gathered, axis):
    g = jnp.moveaxis(gathered, 0, axis)
    s = g.shape
    return g.reshape(s[:axis] + (s[axis] * s[axis + 1],) + s[axis + 2:])


def _to_slabs(full, axis):
    s = full.shape
    g = full.reshape(s[:axis] + (N_DEV, s[axis] // N_DEV) + s[axis + 1:])
    return jnp.moveaxis(g, axis, 0)


def _blockdiag(w):
    h, i, j = w.shape
    eye = jnp.eye(h, dtype=w.dtype)
    return (w[:, :, None, :] * eye[:, None, :, None]).reshape(h * i, h * j)


def _blockdiag_extract(m, h):
    i, j = m.shape[0] // h, m.shape[1] // h
    eye = jnp.eye(h, dtype=m.dtype)
    return (m.reshape(h, i, h, j) * eye[:, None, :, None]).sum(axis=2)


def _s5_disc(lr, li, ls, bre, bim):
    step = jnp.exp(ls)[:, None]
    er = jnp.exp(lr * step)
    ar, ai = er * jnp.cos(li * step), er * jnp.sin(li * step)
    nr, ni, den = ar - 1.0, ai, lr * lr + li * li
    qr, qi = (nr * lr + ni * li) / den, (ni * lr - nr * li) / den
    bbr = qr[..., None] * bre - qi[..., None] * bim
    bbi = qr[..., None] * bim + qi[..., None] * bre
    return ar, ai, bbr, bbi


def _row(v, width=None):
    v = v.reshape(1, -1)
    if width is not None and v.shape[1] < width:
        v = jnp.pad(v, ((0, 0), (0, width - v.shape[1])))
    return v


def _relu2(a):
    r = jnp.maximum(a, 0.0)
    return r * r


def _add_alpha(acc, d):
    return acc + ALPHA * d


def _shift_rows_down(x):
    return jnp.concatenate([jnp.zeros((1, x.shape[1]), x.dtype), x[:-1]], axis=0)


def _shift_rows_up(x):
    return jnp.concatenate([x[1:], jnp.zeros((1, x.shape[1]), x.dtype)], axis=0)


def _layer_params(full, small, l):
    p = {}
    w_in = full["w_in"][l]
    z, xbc, dt, u, xr, g = (w_in[:, 0:512], w_in[:, 512:1536], w_in[:, 1536:1544], w_in[:, 1544:1800],
                            w_in[:, 1800:2056], w_in[:, 2056:2312])
    p["w_inp"] = jnp.concatenate([xbc, z, u, xr, g, dt, jnp.zeros((D_MODEL, D_INP - P_DT - 8), w_in.dtype)], axis=1)
    for k_ in ("w_out", "xa_wq", "xa_wk", "xa_wv", "xa_wo", "mlp_w1", "mlp_w2", "s5_glu_w"):
        p[k_] = full[k_][l]
    p["ssd_cw"], p["ssd_cb"] = full["ssd_conv_w"][l], _row(small["ssd_conv_b"][l])
    dtb, alog, dsk = small["ssd_dt_bias"][l], small["ssd_a_log"][l], small["ssd_d"][l]
    p["prow"] = jnp.concatenate([_row(dtb, LANE), _row(alog, LANE), _row(dsk, LANE), jnp.zeros((5, LANE), F32)], axis=0)
    p["pcol"] = jnp.pad(jnp.stack([dtb, alog], axis=1), ((0, 0), (0, LANE - 2)))
    p["ssd_nw"] = _row(small["ssd_norm_w"][l])
    s5_in = (small["s5_lam_re"][l], small["s5_lam_im"][l], small["s5_log_step"][l], small["s5_b_re"][l], small["s5_b_im"][l])
    (ar, ai, bbr, bbi), p["s5_vjp"] = jax.vjp(_s5_disc, *s5_in)
    p["lam_fwd"] = jnp.concatenate([_row(ar), _row(ai)], axis=1)
    p["lam_adj"] = jnp.concatenate([_row(ar), _row(-ai)], axis=1)
    p["bcat"] = jnp.concatenate([_blockdiag(jnp.swapaxes(bbr, 1, 2)), _blockdiag(jnp.swapaxes(bbi, 1, 2))], axis=1)
    p["ccat"] = jnp.concatenate([_blockdiag(jnp.swapaxes(small["s5_c_re"][l], 1, 2)),
                                 -_blockdiag(jnp.swapaxes(small["s5_c_im"][l], 1, 2))], axis=0)
    p["s5_d"], p["s5_glu_b"] = _row(small["s5_d"][l]), _row(small["s5_glu_b"][l])
    p["rg_cw"], p["rg_cb"] = full["rg_conv_w"][l], _row(small["rg_conv_b"][l])
    p["rg_wa"], p["rg_wx"] = _blockdiag(small["rg_wa"][l]), _blockdiag(small["rg_wx"][l])
    p["rg_ba"], p["rg_bx"], p["rg_lam"] = _row(small["rg_ba"][l]), _row(small["rg_bx"][l]), _row(small["rg_lambda"][l])
    for i in (1, 2, 3):
        p[f"g{i}"], p[f"b{i}"] = _row(small[f"ln{i}_g"][l]), _row(small[f"ln{i}_b"][l])
    return p


def _layer_fwd(h0, mem, p):
    t = h0.shape[0]
    s = {"h0": h0}
    proj = mm(h0, p["w_inp"], name="in_proj")
    dtT = proj[:, P_DT:P_DT + SSD_HEADS].T
    xbc = conv_fwd(proj, 0, p["ssd_cw"], p["ssd_cb"], width=SSD_XBC, act=True, name="ssd_conv_fwd")
    y_ssd, yraw, sall = ssd_fwd(xbc, proj, dtT, p["prow"], p["pcol"], p["ssd_nw"], name="ssd_fwd")
    bu = mm(proj, p["bcat"], a_off=P_U, k=S5_WIDTH, name="s5_bu")
    hs5 = scan_complex(bu, p["lam_fwd"], reverse=False, name="s5_scan_fwd")
    ylin = mm(hs5, p["ccat"], name="s5_ylin")
    (y_s5,), _ = rowk(_s5_post_fwd_fn, [(ylin, S5_WIDTH, 0), (proj, S5_WIDTH, P_U // S5_WIDTH)],
                      [p["s5_d"], p["s5_glu_w"], p["s5_glu_b"]], [S5_WIDTH], [], rows=t, name="s5_post_fwd")
    xc = conv_fwd(proj, P_XR // RG_WIDTH, p["rg_cw"], p["rg_cb"], width=RG_WIDTH, act=False, name="rg_conv_fwd")
    rg_full = [p["rg_wa"], p["rg_wx"], p["rg_ba"], p["rg_bx"], p["rg_lam"]]
    (a_rg, b_rg), _ = rowk(_rg_pre_fwd_fn, [(xc, RG_WIDTH, 0)], rg_full, [RG_WIDTH, RG_WIDTH], [], rows=t, name="rg_pre_fwd")
    h_rg = scan_real(a_rg, b_rg, reverse=False, name="rg_scan_fwd")
    (y_rg,), _ = rowk(_rg_out_fwd_fn, [(h_rg, RG_WIDTH, 0), (proj, RG_WIDTH, P_G // RG_WIDTH)], [], [RG_WIDTH], [],
                      rows=t, name="rg_out_fwd")
    ycat = jnp.concatenate([y_ssd, y_s5, y_rg], axis=1)
    mix = mm(ycat, p["w_out"], name="out_proj")
    h1 = ln_fwd(h0, mix, p["g1"], p["b1"], name="ln_fwd")
    q = mm(h1, p["xa_wq"], name="xa_q")
    k = mm(mem, p["xa_wk"], name="xa_kv")
    v = mm(mem, p["xa_wv"], name="xa_kv")
    (o,), _ = rowk(_attn_fwd_fn, [(q, D_MODEL, 0)], [k, v], [D_MODEL], [], rows=t, name="xa_fwd")
    att = mm(o, p["xa_wo"], name="xa_o")
    h2 = ln_fwd(h1, att, p["g2"], p["b2"], name="ln_fwd")
    a_mlp = mm(h2, p["mlp_w1"], name="mlp_up")
    m_out = mm(a_mlp, p["mlp_w2"], fa=_relu2, name="mlp_down")
    h3 = ln_fwd(h2, m_out, p["g3"], p["b3"], name="ln_fwd")
    s.update(proj=proj, dtT=dtT, xbc=xbc, yraw=yraw, sall=sall, hs5=hs5, ylin=ylin, xc=xc, a_rg=a_rg, h_rg=h_rg,
             ycat=ycat, mix=mix, h1=h1, q=q, k=k, v=v, o=o, att=att, h2=h2, a_mlp=a_mlp, m_out=m_out)
    return h3, s


def _layer_bwd(dh3, mem, p, s, l, gfull, gsmall):
    t = dh3.shape[0]
    proj = s["proj"]
    dpre3, dg3, db3 = ln_bwd(s["h2"], s["m_out"], dh3, p["g3"], name="ln_bwd")
    da = mm(dpre3, p["mlp_w2"], tb=True, o_extra=(s["a_mlp"],), fo=lambda acc, a: acc * 2.0 * jnp.maximum(a, 0.0), name="mlp_da")
    gfull["mlp_w2"][l] = mm(s["a_mlp"], dpre3, ta=True, fa=_relu2, name="mlp_dw2")
    gfull["mlp_w1"][l] = mm(s["h2"], da, ta=True, name="mlp_dw1")
    dh2 = mm(da, p["mlp_w1"], tb=True, o_extra=(dpre3,), fo=_add_alpha, name="mlp_dx")
    dpre2, dg2, db2 = ln_bwd(s["h1"], s["att"], dh2, p["g2"], name="ln_bwd")
    do = mm(dpre2, p["xa_wo"], tb=True, name="xa_do")
    gfull["xa_wo"][l] = mm(s["o"], dpre2, ta=True, name="dw_sq")
    (dq,), (dk, dv) = rowk(_attn_bwd_fn, [(s["q"], D_MODEL, 0), (do, D_MODEL, 0)], [s["k"], s["v"]], [D_MODEL],
                           [(256, D_MODEL), (256, D_MODEL)], rows=t, name="xa_bwd")
    gfull["xa_wq"][l] = mm(s["h1"], dq, ta=True, name="dw_sq")
    gfull["xa_wk"][l] = mm(mem, dk, ta=True, name="dw_kv")
    gfull["xa_wv"][l] = mm(mem, dv, ta=True, name="dw_kv")
    dh1 = mm(dq, p["xa_wq"], tb=True, o_extra=(dpre2,), fo=_add_alpha, name="dx_sq")
    dpre1, dg1, db1 = ln_bwd(s["h0"], s["mix"], dh1, p["g1"], name="ln_bwd")
    dycat = mm(dpre1, p["w_out"], tb=True, name="xa_do")
    gfull["w_out"][l] = mm(s["ycat"], dpre1, ta=True, name="dw_sq")
    (dh_rg, dg_rg), _ = rowk(_rg_out_bwd_fn, [(s["h_rg"], RG_WIDTH, 0), (proj, RG_WIDTH, P_G // RG_WIDTH), (dycat, RG_WIDTH, 3)],
                             [], [RG_WIDTH, RG_WIDTH], [], rows=t, name="rg_out_bwd")
    g_rg = scan_real(_shift_rows_up(s["a_rg"]), dh_rg, reverse=True, name="rg_scan_bwd")
    rg_full = [p["rg_wa"], p["rg_wx"], p["rg_ba"], p["rg_bx"], p["rg_lam"]]
    (dxc,), (dwa, dwx, dba, dbx, dlam) = rowk(
        _rg_pre_bwd_fn, [(s["xc"], RG_WIDTH, 0), (g_rg, RG_WIDTH, 0), (_shift_rows_down(s["h_rg"]), RG_WIDTH, 0)], rg_full,
        [RG_WIDTH], [(RG_WIDTH, RG_WIDTH), (RG_WIDTH, RG_WIDTH), (1, RG_WIDTH), (1, RG_WIDTH), (1, RG_WIDTH)],
        rows=t, name="rg_pre_bwd")
    dxr, d_rgcw, d_rgcb = conv_bwd(proj, P_XR // RG_WIDTH, dxc, p["rg_cw"], p["rg_cb"], width=RG_WIDTH, act=False, name="rg_conv_bwd")
    (dylin, du_a), (d_s5d, d_gluw, d_glub) = rowk(
        _s5_post_bwd_fn, [(s["ylin"], S5_WIDTH, 0), (proj, S5_WIDTH, P_U // S5_WIDTH), (dycat, S5_WIDTH, 2)],
        [p["s5_d"], p["s5_glu_w"], p["s5_glu_b"]], [S5_WIDTH, S5_WIDTH],
        [(1, S5_WIDTH), (S5_WIDTH, S5_WIDTH), (1, S5_WIDTH)], rows=t, name="s5_post_bwd")
    dhs = mm(dylin, p["ccat"], tb=True, name="s5_dh")
    dccat = mm(s["hs5"], dylin, ta=True, name="s5_dc")
    gs5 = scan_complex(dhs, p["lam_adj"], reverse=True, name="s5_scan_bwd")
    dar, dai = s5_dlam(gs5, _shift_rows_down(s["hs5"]), name="s5_dlam")
    du = mm(gs5, p["bcat"], tb=True, o_extra=(du_a,), fo=lambda acc, d: acc + d, name="s5_du")
    dbcat = mm(proj, gs5, ta=True, a_off=P_U, m=S5_WIDTH, name="s5_db")
    dxbc_act, dz, ddt, dprm, dnw = ssd_bwd(s["xbc"], proj, s["dtT"], p["prow"], p["pcol"], p["ssd_nw"], s["yraw"],
                                          s["sall"], dycat, name="ssd_bwd")
    dxbc, d_scw, d_scb = conv_bwd(proj, 0, dxbc_act, p["ssd_cw"], p["ssd_cb"], width=SSD_XBC, act=True, name="ssd_conv_bwd")
    dproj = jnp.concatenate([dxbc, dz, du, dxr, dg_rg, ddt, jnp.zeros((t, D_INP - P_DT - LANE), F32)], axis=1)
    dh0 = mm(dproj, p["w_inp"], tb=True, o_extra=(dpre1,), fo=_add_alpha, name="in_proj_dx")
    dwp = mm(s["h0"], dproj, ta=True, name="in_proj_dw")
    gfull["w_in"][l] = jnp.concatenate([dwp[:, P_Z:P_Z + 512], dwp[:, P_XBC:P_XBC + 1024], dwp[:, P_DT:P_DT + 8],
                                        dwp[:, P_U:P_U + 256], dwp[:, P_XR:P_XR + 256], dwp[:, P_G:P_G + 256]], axis=1)
    gfull["ssd_conv_w"][l], gfull["rg_conv_w"][l], gfull["s5_glu_w"][l] = d_scw, d_rgcw, d_gluw
    ng, ns = S5_GROUPS, S5_STATE
    dbbr = jnp.swapaxes(_blockdiag_extract(dbcat[:, :S5_NSTATE], ng), 1, 2)
    dbbi = jnp.swapaxes(_blockdiag_extract(dbcat[:, S5_NSTATE:], ng), 1, 2)
    d_lr, d_li, d_ls, d_bre, d_bim = p["s5_vjp"]((dar.reshape(ng, ns), dai.reshape(ng, ns), dbbr, dbbi))
    gsmall["s5_lam_re"][l], gsmall["s5_lam_im"][l], gsmall["s5_log_step"][l] = d_lr, d_li, d_ls
    gsmall["s5_b_re"][l], gsmall["s5_b_im"][l] = d_bre, d_bim
    gsmall["s5_c_re"][l] = jnp.swapaxes(_blockdiag_extract(dccat[:S5_NSTATE], ng), 1, 2)
    gsmall["s5_c_im"][l] = -jnp.swapaxes(_blockdiag_extract(dccat[S5_NSTATE:], ng), 1, 2)
    gsmall["s5_d"][l], gsmall["s5_glu_b"][l] = d_s5d[0], d_glub[0]
    gsmall["ssd_conv_b"][l], gsmall["rg_conv_b"][l] = d_scb[0], d_rgcb[0]
    gsmall["ssd_dt_bias"][l], gsmall["ssd_a_log"][l], gsmall["ssd_d"][l] = dprm[0, :8], dprm[1, :8], dprm[2, :8]
    gsmall["ssd_norm_w"][l] = dnw[0]
    gsmall["rg_wa"][l], gsmall["rg_wx"][l] = _blockdiag_extract(dwa, RG_BLOCKS), _blockdiag_extract(dwx, RG_BLOCKS)
    gsmall["rg_ba"][l], gsmall["rg_bx"][l] = dba.reshape(RG_BLOCKS, RG_BLOCK_DIM), dbx.reshape(RG_BLOCKS, RG_BLOCK_DIM)
    gsmall["rg_lambda"][l] = dlam[0]
    for i, (dg, db) in zip((1, 2, 3), ((dg1, db1), (dg2, db2), (dg3, db3))):
        gsmall[f"ln{i}_g"][l], gsmall[f"ln{i}_b"][l] = dg[0], db[0]
    return dh0


def _step(a):
    h = a["x"][0]
    mem = a["mem"][0]
    t = h.shape[0]
    pieces = []
    for name, _ in SHARDED:
        w = a[name]
        if name in KEEP_F32:
            pieces.append(lax.bitcast_convert_type(w, BF16).reshape(-1))
        else:
            pieces.append(w.astype(BF16).reshape(-1))
    gathered = all_gather(_pack_rows(jnp.concatenate(pieces), 16), name="ag_weights")
    shapes = [a[name].shape + ((2,) if name in KEEP_F32 else ()) for name, _ in SHARDED]
    full = {}
    for (name, axis), g in zip(SHARDED, _unpack(gathered, shapes)):
        if name in KEEP_F32:
            g = lax.bitcast_convert_type(g, F32)
        full[name] = _to_full(g, axis)
    small = {name: a[name] for name in SMALL}
    params, saved = [], []
    for l in range(DEPTH):
        p = _layer_params(full, small, l)
        h, s = _layer_fwd(h, mem, p)
        params.append(p)
        saved.append(s)
    (dh,), (loss_part,) = rowk(_loss_fn, [(h, D_MODEL, 0), (a["loss_target"][0], D_MODEL, 0)], [], [D_MODEL], [(1, 1)],
                               rows=t, name="loss_head")
    loss = lax.psum(loss_part[0, 0], ("x", "y", "c"))
    gfull = {name: [None] * DEPTH for name, _ in SHARDED}
    gsmall = {name: [None] * DEPTH for name in SMALL}
    for l in reversed(range(DEPTH)):
        dh = _layer_bwd(dh, mem, params[l], saved[l], l, gfull, gsmall)
    grad_x = dh[None]
    slabs = jnp.concatenate([_to_slabs(jnp.stack(gfull[name]), axis).reshape(N_DEV, -1) for name, axis in SHARDED], axis=1)
    slabs = all_to_all(_pack_rows(slabs, 1024), name="a2a_grads")
    pk = lambda pre: _pack_rows(jnp.concatenate([a[pre + name].reshape(-1) for name, _ in SHARDED]), 1024)
    big = adamw(slabs, pk(""), pk("m_"), pk("v_"), name="adamw_sharded", tt=1024)
    gs = _pack_rows(jnp.concatenate([jnp.stack(gsmall[name]).reshape(-1) for name in SMALL]), 8)
    gs = all_gather(gs, name="ag_small_grads")
    pks = lambda pre: _pack_rows(jnp.concatenate([a[pre + name].reshape(-1) for name in SMALL]), 8)
    sm = adamw(gs, pks(""), pks("m_"), pks("v_"), name="adamw_replicated", tt=gs.shape[1])
    out = {}
    for kind, bg, sg in zip(("grad_", "delta_", "new_m_", "new_v_"), big, sm):
        for (name, _), arr in zip(SHARDED, _unpack(bg, [a[name].shape for name, _ in SHARDED])):
            out[kind + name] = arr
        for name, arr in zip(SMALL, _unpack(sg, [a[name].shape for name in SMALL])):
            out[kind + name] = arr
    return (loss, grad_x) + tuple(out[kind + name] for kind in ("grad_", "delta_", "new_m_", "new_v_") for name in WEIGHTS)


def kernel(x, mem, w_in, w_out, ssd_conv_w, ssd_conv_b, ssd_dt_bias, ssd_a_log, ssd_d, ssd_norm_w, s5_lam_re, s5_lam_im, s5_log_step, s5_b_re, s5_b_im, s5_c_re, s5_c_im, s5_d, s5_glu_w, s5_glu_b, rg_conv_w, rg_conv_b, rg_wa, rg_ba, rg_wx, rg_bx, rg_lambda, ln1_g, ln1_b, xa_wq, xa_wk, xa_wv, xa_wo, ln2_g, ln2_b, mlp_w1, mlp_w2, ln3_g, ln3_b, loss_target, m_w_in, m_w_out, m_ssd_conv_w, m_ssd_conv_b, m_ssd_dt_bias, m_ssd_a_log, m_ssd_d, m_ssd_norm_w, m_s5_lam_re, m_s5_lam_im, m_s5_log_step, m_s5_b_re, m_s5_b_im, m_s5_c_re, m_s5_c_im, m_s5_d, m_s5_glu_w, m_s5_glu_b, m_rg_conv_w, m_rg_conv_b, m_rg_wa, m_rg_ba, m_rg_wx, m_rg_bx, m_rg_lambda, m_ln1_g, m_ln1_b, m_xa_wq, m_xa_wk, m_xa_wv, m_xa_wo, m_ln2_g, m_ln2_b, m_mlp_w1, m_mlp_w2, m_ln3_g, m_ln3_b, v_w_in, v_w_out, v_ssd_conv_w, v_ssd_conv_b, v_ssd_dt_bias, v_ssd_a_log, v_ssd_d, v_ssd_norm_w, v_s5_lam_re, v_s5_lam_im, v_s5_log_step, v_s5_b_re, v_s5_b_im, v_s5_c_re, v_s5_c_im, v_s5_d, v_s5_glu_w, v_s5_glu_b, v_rg_conv_w, v_rg_conv_b, v_rg_wa, v_rg_ba, v_rg_wx, v_rg_bx, v_rg_lambda, v_ln1_g, v_ln1_b, v_xa_wq, v_xa_wk, v_xa_wv, v_xa_wo, v_ln2_g, v_ln2_b, v_mlp_w1, v_mlp_w2, v_ln3_g, v_ln3_b):
    return _step(dict(locals()))
```

```python
import math

import jax
import jax.numpy as jnp
from jax import lax
from jax.experimental import pallas as pl
from jax.experimental.pallas import tpu as pltpu

F32 = jnp.float32
BF16 = jnp.bfloat16

N_DEV = 8
D_MODEL = 1024
DEPTH = 2
SSD_WIDTH = 512
SSD_HEADS = 8
SSD_HEAD_DIM = 64
SSD_STATE = 128
SSD_CHUNK = 128
SSD_XBC = 1024
S5_WIDTH = 256
S5_GROUPS = 16
S5_GROUP_CH = 16
S5_STATE = 64
S5_NSTATE = S5_GROUPS * S5_STATE
RG_WIDTH = 256
RG_BLOCKS = 4
RG_BLOCK_DIM = 64
RG_C = 8.0
XA_HEADS = 4
XA_HEAD_DIM = 256
ALPHA = (2.0 * DEPTH) ** 0.25
LN_EPS = 1e-5
ADAM_LR, ADAM_B1, ADAM_B2, ADAM_EPS, ADAM_WD, ADAM_STEP = 0.001, 0.9, 0.999, 1e-08, 0.01, 10

P_XBC, P_Z, P_U, P_XR, P_G, P_DT = 0, 1024, 1536, 1792, 2048, 2304
D_INP = 2560
LANE = 128
VMEM_LIMIT = 56 * 1024 * 1024
ROW_TILE = 512

_NN = ((1,), (0,))
_NT = ((1,), (1,))
_TN = ((0,), (0,))


def _dot(a, b, dims=_NN):
    return lax.dot_general(a.astype(BF16), b.astype(BF16), (dims, ((), ())), preferred_element_type=F32)


def _split_bf16(x, parts):
    out, rem = [], x
    for _ in range(parts):
        piece = rem.astype(BF16)
        out.append(piece)
        rem = rem - piece.astype(F32)
    return out


def _dot_mask(a, b, dims=_NN, *, mask_left, parts):
    if mask_left:
        return sum(_dot(a, piece, dims) for piece in _split_bf16(b, parts))
    return sum(_dot(piece, b, dims) for piece in _split_bf16(a, parts))


def _sigmoid(x):
    return 1.0 / (1.0 + jnp.exp(-x))


def _silu(x):
    return x * _sigmoid(x)


def _dsilu(x):
    s = _sigmoid(x)
    return s * (1.0 + x * (1.0 - s))


_GK = math.sqrt(2.0 / math.pi)
_GC = 0.044715


def _gelu(x):
    return 0.5 * x * (1.0 + jnp.tanh(_GK * (x + _GC * x * x * x)))


def _dgelu(x):
    th = jnp.tanh(_GK * (x + _GC * x * x * x))
    return 0.5 * (1.0 + th) + 0.5 * x * (1.0 - th * th) * _GK * (1.0 + 3.0 * _GC * x * x)


def _log1p_pos(e):
    return jnp.where(e < 1e-2, e * (1.0 - e * (0.5 - e * (1.0 / 3.0))), jnp.log(1.0 + e))


def _softplus(x):
    return jnp.maximum(x, 0.0) + _log1p_pos(jnp.exp(-jnp.abs(x)))


def _neg_expm1(x):
    poly = -x * (1.0 + x * (0.5 + x * (1.0 / 6.0 + x * (1.0 / 24.0 + x * (1.0 / 120.0)))))
    return jnp.where(x > -0.05, poly, 1.0 - jnp.exp(x))


def _params(sem):
    return pltpu.CompilerParams(dimension_semantics=sem, vmem_limit_bytes=VMEM_LIMIT)


RESIDENT_BYTES = 8 * 1024 * 1024
STREAM_BYTES = 4 * 1024 * 1024


def _halve_to_fit(dims, bytes_per, limit):
    dims = list(dims)
    while math.prod(dims) * bytes_per > limit:
        i = max(range(len(dims)), key=lambda d: dims[d])
        assert dims[i] % 256 == 0, dims
        dims[i] //= 2
    return dims


def mm(a, b, *, name, ta=False, tb=False, a_extra=(), fa=None, o_extra=(), fo=None, a_off=0, m=None, k=None):
    n = b.shape[0] if tb else b.shape[1]
    na, no = 1 + len(a_extra), len(o_extra)
    if not ta:
        assert m is None
        m, kdim = a.shape[0], (a.shape[1] if k is None else k)
        assert a_off % kdim == 0
        (tn,) = _halve_to_fit([n], kdim * b.dtype.itemsize, RESIDENT_BYTES)
        (tm,) = _halve_to_fit([min(512, m)], max(tn, kdim) * 4, STREAM_BYTES)
        a_spec = pl.BlockSpec((tm, kdim), lambda i, j: (i, a_off // kdim))
        b_spec = pl.BlockSpec((tn, kdim), lambda i, j: (j, 0)) if tb else pl.BlockSpec((kdim, tn), lambda i, j: (0, j))
        o_spec = pl.BlockSpec((tm, tn), lambda i, j: (i, j))
        dims = _NT if tb else _NN

        def body(*refs):
            a_refs, b_ref, o_refs, out_ref = refs[:na], refs[na], refs[na + 1:na + 1 + no], refs[na + 1 + no]
            av = a_refs[0][...] if fa is None else fa(*[r[...] for r in a_refs])
            acc = _dot(av, b_ref[...], dims)
            out_ref[...] = acc if fo is None else fo(acc, *[r[...] for r in o_refs])

        grid, sem = (m // tm, n // tn), ("parallel", "parallel")
    else:
        assert k is None and not tb and fo is None and not o_extra
        kdim, m = a.shape[0], (a.shape[1] if m is None else m)
        tm, tn = _halve_to_fit([m, n], 4, RESIDENT_BYTES)
        (tk,) = _halve_to_fit([min(512, kdim)], max(tm, tn) * 4, STREAM_BYTES)
        assert a_off % tm == 0
        a_spec = pl.BlockSpec((tk, tm), lambda i, j, kk: (kk, i + a_off // tm))
        b_spec = pl.BlockSpec((tk, tn), lambda i, j, kk: (kk, j))
        o_spec = pl.BlockSpec((tm, tn), lambda i, j, kk: (i, j))

        def body(*refs):
            a_refs, b_ref, out_ref = refs[:na], refs[na], refs[na + 1]

            @pl.when(pl.program_id(2) == 0)
            def _():
                out_ref[...] = jnp.zeros_like(out_ref)

            av = a_refs[0][...] if fa is None else fa(*[r[...] for r in a_refs])
            out_ref[...] += _dot(av, b_ref[...], _TN)

        grid, sem = (m // tm, n // tn, kdim // tk), ("parallel", "parallel", "arbitrary")
    assert m % tm == 0 and n % tn == 0, (name, m, n, tm, tn)
    return pl.pallas_call(
        body, name=name, grid=grid,
        in_specs=[a_spec] * na + [b_spec] + [o_spec] * no,
        out_specs=o_spec, out_shape=jax.ShapeDtypeStruct((m, n), F32),
        compiler_params=_params(sem),
    )(a, *a_extra, b, *o_extra)


def rowk(fn, tiled, full, out_w, acc_shapes, *, rows, name):
    tt = min(ROW_TILE, rows)
    n = rows // tt
    assert rows % tt == 0
    nt, nf, no = len(tiled), len(full), len(out_w)

    def tspec(w, cb):
        return pl.BlockSpec((tt, w), lambda i: (i, cb))

    def fspec(a):
        nd = a.ndim
        return pl.BlockSpec(a.shape, lambda i: (0,) * nd)

    def body(*refs):
        ins, fulls = refs[:nt], refs[nt:nt + nf]
        outs, accs = refs[nt + nf:nt + nf + no], refs[nt + nf + no:]
        res_t, res_a = fn(*[r[...] for r in ins], *[r[...] for r in fulls])
        for r, v in zip(outs, res_t):
            r[...] = v
        if accs:
            @pl.when(pl.program_id(0) == 0)
            def _():
                for r in accs:
                    r[...] = jnp.zeros_like(r)
            for r, v in zip(accs, res_a):
                r[...] += v

    outs = pl.pallas_call(
        body, name=name, grid=(n,),
        in_specs=[tspec(w, cb) for (_, w, cb) in tiled] + [fspec(a) for a in full],
        out_specs=[tspec(w, 0) for w in out_w] + [pl.BlockSpec(s, lambda i, nd=len(s): (0,) * nd) for s in acc_shapes],
        out_shape=[jax.ShapeDtypeStruct((rows, w), F32) for w in out_w] + [jax.ShapeDtypeStruct(s, F32) for s in acc_shapes],
        compiler_params=_params(("arbitrary",)),
    )(*[a for (a, _, _) in tiled], *full)
    return outs[:no], outs[no:]


def _colsum(x):
    return jnp.sum(x, axis=0, keepdims=True)


def _rowsum(x):
    return jnp.sum(x, axis=1, keepdims=True)


def _ln_fwd_fn(resid, y, g, b):
    pre = ALPHA * resid + y
    mu = jnp.mean(pre, axis=1, keepdims=True)
    xc = pre - mu
    var = jnp.mean(xc * xc, axis=1, keepdims=True)
    return (xc * lax.rsqrt(var + LN_EPS) * g + b,), ()


def _ln_bwd_fn(resid, y, dout, g):
    pre = ALPHA * resid + y
    mu = jnp.mean(pre, axis=1, keepdims=True)
    xc = pre - mu
    var = jnp.mean(xc * xc, axis=1, keepdims=True)
    rstd = lax.rsqrt(var + LN_EPS)
    xhat = xc * rstd
    dxh = dout * g
    dpre = rstd * (dxh - jnp.mean(dxh, axis=1, keepdims=True) - xhat * jnp.mean(dxh * xhat, axis=1, keepdims=True))
    return (dpre,), (_colsum(dout * xhat), _colsum(dout))


def ln_fwd(resid, y, g, b, *, name):
    (out,), _ = rowk(_ln_fwd_fn, [(resid, D_MODEL, 0), (y, D_MODEL, 0)], [g, b], [D_MODEL], [],
                     rows=resid.shape[0], name=name)
    return out


def ln_bwd(resid, y, dout, g, *, name):
    (dpre,), (dg, db) = rowk(_ln_bwd_fn, [(resid, D_MODEL, 0), (y, D_MODEL, 0), (dout, D_MODEL, 0)], [g],
                             [D_MODEL], [(1, D_MODEL), (1, D_MODEL)], rows=resid.shape[0], name=name)
    return dpre, dg, db


def _loss_fn(y, tgt):
    e = y - tgt
    part = _colsum(_rowsum(e * e)) * (0.5 / D_MODEL)
    return (e * (1.0 / D_MODEL),), (part,)


_XA_SCALE = 1.0 / math.sqrt(XA_HEAD_DIM)


def _attn_probs(qh, kh):
    s = _dot(qh, kh, _NT) * _XA_SCALE
    e = jnp.exp(s - jnp.max(s, axis=1, keepdims=True))
    return e / _rowsum(e)


def _attn_fwd_fn(q, k, v):
    outs = []
    for hd in range(XA_HEADS):
        sl = slice(hd * XA_HEAD_DIM, (hd + 1) * XA_HEAD_DIM)
        outs.append(_dot(_attn_probs(q[:, sl], k[:, sl]), v[:, sl]))
    return (jnp.concatenate(outs, axis=1),), ()


def _attn_bwd_fn(q, do, k, v):
    dqs, dks, dvs = [], [], []
    for hd in range(XA_HEADS):
        sl = slice(hd * XA_HEAD_DIM, (hd + 1) * XA_HEAD_DIM)
        qh, kh, vh, doh = q[:, sl], k[:, sl], v[:, sl], do[:, sl]
        p = _attn_probs(qh, kh)
        dp = _dot(doh, vh, _NT)
        ds = p * (dp - _rowsum(p * dp)) * _XA_SCALE
        dqs.append(_dot(ds, kh))
        dks.append(_dot(ds, qh, _TN))
        dvs.append(_dot(p, doh, _TN))
    cat = lambda xs: jnp.concatenate(xs, axis=1)
    return (cat(dqs),), (cat(dks), cat(dvs))


def _s5_post_fwd_fn(ylin, u, dskip, gw, gb):
    yg = _gelu(ylin + dskip * u)
    return (yg * _sigmoid(_dot(yg, gw) + gb),), ()


def _s5_post_bwd_fn(ylin, u, dout, dskip, gw, gb):
    pre = ylin + dskip * u
    yg = _gelu(pre)
    sg = _sigmoid(_dot(yg, gw) + gb)
    dlin = dout * yg * sg * (1.0 - sg)
    dyg = dout * sg + _dot(dlin, gw, _NT)
    dpre = dyg * _dgelu(pre)
    return (dpre, dpre * dskip), (_colsum(dpre * u), _dot(yg, dlin, _TN), _colsum(dlin))


def _rg_gates(xc, wa, wx, ba, bx, lam):
    r = _sigmoid(_dot(xc, wa) + ba)
    i = _sigmoid(_dot(xc, wx) + bx)
    sp = _softplus(-lam)
    log_a = -RG_C * r * sp
    a = jnp.exp(log_a)
    mult = jnp.sqrt(_neg_expm1(2.0 * log_a))
    return r, i, sp, a, mult


def _rg_pre_fwd_fn(xc, wa, wx, ba, bx, lam):
    r, i, sp, a, mult = _rg_gates(xc, wa, wx, ba, bx, lam)
    return (a, mult * (i * xc)), ()


def _rg_pre_bwd_fn(xc, gsc, hprev, wa, wx, ba, bx, lam):
    r, i, sp, a, mult = _rg_gates(xc, wa, wx, ba, bx, lam)
    da = gsc * hprev
    db = gsc
    dmult = db * i * xc
    di = db * mult * xc
    dxc = db * mult * i
    dlog_a = da * a - a * a * dmult / mult
    dr = dlog_a * (-RG_C * sp)
    dsp = _colsum(dlog_a * (-RG_C * r))
    dlam = dsp * (-_sigmoid(-lam))
    dpr = dr * r * (1.0 - r)
    dpi = di * i * (1.0 - i)
    dxc = dxc + _dot(dpr, wa, _NT) + _dot(dpi, wx, _NT)
    return (dxc,), (_dot(xc, dpr, _TN), _dot(xc, dpi, _TN), _colsum(dpr), _colsum(dpi), dlam)


def _rg_out_fwd_fn(h, g):
    return (h * _gelu(g),), ()


def _rg_out_bwd_fn(h, g, dy):
    return (dy * _gelu(g), dy * h * _dgelu(g)), ()


def _shift_down(x, prev, j, rows):
    return jnp.where(rows < j, pltpu.roll(prev, j, 0), pltpu.roll(x, j, 0))


def _shift_up(x, nxt, j, rows):
    t = x.shape[0]
    return jnp.where(rows >= t - j, pltpu.roll(nxt, t - j, 0), pltpu.roll(x, t - j, 0))


def conv_fwd(src, cb, w, b, *, width, act, name):
    t = src.shape[0]
    tt = min(ROW_TILE, t)
    n = t // tt

    def body(x_ref, w_ref, b_ref, y_ref, prev_ref):
        @pl.when(pl.program_id(0) == 0)
        def _():
            prev_ref[...] = jnp.zeros_like(prev_ref)

        x = x_ref[...]
        prev = prev_ref[...]
        rows = lax.broadcasted_iota(jnp.int32, x.shape, 0)
        wv = w_ref[...]
        y = b_ref[...] + wv[3:4, :] * x
        for j in (1, 2, 3):
            y = y + wv[3 - j:4 - j, :] * _shift_down(x, prev, j, rows)
        y_ref[...] = _silu(y) if act else y
        prev_ref[...] = x

    return pl.pallas_call(
        body, name=name, grid=(n,),
        in_specs=[pl.BlockSpec((tt, width), lambda i: (i, cb)),
                  pl.BlockSpec((4, width), lambda i: (0, 0)), pl.BlockSpec((1, width), lambda i: (0, 0))],
        out_specs=pl.BlockSpec((tt, width), lambda i: (i, 0)),
        out_shape=jax.ShapeDtypeStruct((t, width), F32),
        scratch_shapes=[pltpu.VMEM((tt, width), F32)],
        compiler_params=_params(("arbitrary",)),
    )(src, w, b)


def conv_bwd(src, cb, dy, w, b, *, width, act, name):
    t = src.shape[0]
    tt = min(ROW_TILE, t)
    n = t // tt

    def body(x_ref, xp_ref, dy_ref, w_ref, b_ref, dx_ref, dw_ref, db_ref, nxt_ref):
        i = pl.program_id(0)

        @pl.when(i == 0)
        def _():
            nxt_ref[...] = jnp.zeros_like(nxt_ref)
            dw_ref[...] = jnp.zeros_like(dw_ref)
            db_ref[...] = jnp.zeros_like(db_ref)

        x = x_ref[...]
        prev = jnp.where(i == n - 1, 0.0, xp_ref[...])
        rows = lax.broadcasted_iota(jnp.int32, x.shape, 0)
        wv = w_ref[...]
        xs = [x] + [_shift_down(x, prev, j, rows) for j in (1, 2, 3)]
        dpre = dy_ref[...]
        if act:
            pre = b_ref[...] + wv[3:4, :] * xs[0]
            for j in (1, 2, 3):
                pre = pre + wv[3 - j:4 - j, :] * xs[j]
            dpre = dpre * _dsilu(pre)
        nxt = nxt_ref[...]
        dx = wv[3:4, :] * dpre
        for j in (1, 2, 3):
            dx = dx + wv[3 - j:4 - j, :] * _shift_up(dpre, nxt, j, rows)
        dx_ref[...] = dx
        dw_ref[...] += jnp.concatenate([_colsum(dpre * xs[3 - kk]) for kk in range(4)], axis=0)
        db_ref[...] += _colsum(dpre)
        nxt_ref[...] = dpre

    return pl.pallas_call(
        body, name=name, grid=(n,),
        in_specs=[pl.BlockSpec((tt, width), lambda i: (n - 1 - i, cb)),
                  pl.BlockSpec((tt, width), lambda i: (jnp.maximum(n - 2 - i, 0), cb)),
                  pl.BlockSpec((tt, width), lambda i: (n - 1 - i, 0)),
                  pl.BlockSpec((4, width), lambda i: (0, 0)), pl.BlockSpec((1, width), lambda i: (0, 0))],
        out_specs=[pl.BlockSpec((tt, width), lambda i: (n - 1 - i, 0)),
                   pl.BlockSpec((4, width), lambda i: (0, 0)), pl.BlockSpec((1, width), lambda i: (0, 0))],
        out_shape=[jax.ShapeDtypeStruct((t, width), F32), jax.ShapeDtypeStruct((4, width), F32),
                   jax.ShapeDtypeStruct((1, width), F32)],
        scratch_shapes=[pltpu.VMEM((tt, width), F32)],
        compiler_params=_params(("arbitrary",)),
    )(src, src, dy, w, b)


SCAN_CW = 512


def scan_complex(bu, lam, *, reverse, name):
    t, w2 = bu.shape
    w = w2 // 2
    tt = min(ROW_TILE, t)
    n, nb, cw = t // tt, tt // 8, min(SCAN_CW, w)

    def body(b_ref, lam_ref, o_ref, st_ref):
        @pl.when(pl.program_id(0) == 0)
        def _():
            st_ref[...] = jnp.zeros_like(st_ref)

        rows = lax.broadcasted_iota(jnp.int32, (8, cw), 0)
        for c0 in range(0, w, cw):
            re, im = pl.ds(c0, cw), pl.ds(w + c0, cw)
            ar = jnp.broadcast_to(lam_ref[:, re], (8, cw))
            ai = jnp.broadcast_to(lam_ref[:, im], (8, cw))

            def blk(i, carry):
                hr, hi = carry
                base = pl.multiple_of((nb - 1 - i if reverse else i) * 8, 8)
                tr, ti = b_ref[pl.ds(base, 8), re], b_ref[pl.ds(base, 8), im]
                outr, outi = jnp.zeros((8, cw), F32), jnp.zeros((8, cw), F32)
                for j in (range(7, -1, -1) if reverse else range(8)):
                    br = jnp.broadcast_to(tr[j:j + 1, :], (8, cw))
                    bi = jnp.broadcast_to(ti[j:j + 1, :], (8, cw))
                    hr, hi = ar * hr - ai * hi + br, ar * hi + ai * hr + bi
                    outr = jnp.where(rows == j, hr, outr)
                    outi = jnp.where(rows == j, hi, outi)
                o_ref[pl.ds(base, 8), re] = outr
                o_ref[pl.ds(base, 8), im] = outi
                return hr, hi

            hr, hi = lax.fori_loop(0, nb, blk, (st_ref[:, re], st_ref[:, im]))
            st_ref[:, re] = hr
            st_ref[:, im] = hi

    idx = (lambda i: (n - 1 - i, 0)) if reverse else (lambda i: (i, 0))
    return pl.pallas_call(
        body, name=name, grid=(n,),
        in_specs=[pl.BlockSpec((tt, w2), idx), pl.BlockSpec((1, w2), lambda i: (0, 0))],
        out_specs=pl.BlockSpec((tt, w2), idx), out_shape=jax.ShapeDtypeStruct((t, w2), F32),
        scratch_shapes=[pltpu.VMEM((8, w2), F32)],
        compiler_params=_params(("arbitrary",)),
    )(bu, lam)


def scan_real(a, b, *, reverse, name):
    t, w = b.shape
    tt = min(ROW_TILE, t)
    n, nb = t // tt, tt // 8

    def body(a_ref, b_ref, o_ref, st_ref):
        @pl.when(pl.program_id(0) == 0)
        def _():
            st_ref[...] = jnp.zeros_like(st_ref)

        rows = lax.broadcasted_iota(jnp.int32, (8, w), 0)

        def blk(i, h):
            base = pl.multiple_of((nb - 1 - i if reverse else i) * 8, 8)
            ta_, tb_ = a_ref[pl.ds(base, 8), :], b_ref[pl.ds(base, 8), :]
            out = jnp.zeros((8, w), F32)
            for j in (range(7, -1, -1) if reverse else range(8)):
                h = jnp.broadcast_to(ta_[j:j + 1, :], (8, w)) * h + jnp.broadcast_to(tb_[j:j + 1, :], (8, w))
                out = jnp.where(rows == j, h, out)
            o_ref[pl.ds(base, 8), :] = out
            return h

        st_ref[...] = lax.fori_loop(0, nb, blk, st_ref[...])

    idx = (lambda i: (n - 1 - i, 0)) if reverse else (lambda i: (i, 0))
    return pl.pallas_call(
        body, name=name, grid=(n,),
        in_specs=[pl.BlockSpec((tt, w), idx), pl.BlockSpec((tt, w), idx)],
        out_specs=pl.BlockSpec((tt, w), idx), out_shape=jax.ShapeDtypeStruct((t, w), F32),
        scratch_shapes=[pltpu.VMEM((8, w), F32)],
        compiler_params=_params(("arbitrary",)),
    )(a, b)


def s5_dlam(g, hprev, *, name):
    t, w2 = g.shape
    w = w2 // 2

    def fn(gt, ht):
        gr, gi, hr, hi = gt[:, :w], gt[:, w:], ht[:, :w], ht[:, w:]
        return (), (_colsum(gr * hr + gi * hi), _colsum(gi * hr - gr * hi))

    _, (dar, dai) = rowk(fn, [(g, w2, 0), (hprev, w2, 0)], [], [], [(1, w), (1, w)], rows=t, name=name)
    return dar, dai


def _ssd_common(dt_ref, dtT_ref, prow_ref, pcol_ref):
    q = SSD_CHUNK
    r = lax.broadcasted_iota(jnp.int32, (q, q), 0)
    c = lax.broadcasted_iota(jnp.int32, (q, q), 1)
    low = r >= c
    tril = low.astype(F32)
    triu = (r <= c).astype(F32)
    bias_r, alog_r = prow_ref[0:1, :], prow_ref[1:2, :]
    raw_c = dt_ref[...] + bias_r
    dt_c = _softplus(raw_c)
    a_r = -jnp.exp(alog_r)
    cs_c = _dot_mask(tril, dt_c * a_r, mask_left=True, parts=3)
    dt_r = _softplus(dtT_ref[...] + pcol_ref[:, 0:1])
    cs_r = _dot_mask(dt_r * (-jnp.exp(pcol_ref[:, 1:2])), triu, mask_left=False, parts=3)
    return low, tril, triu, raw_c, dt_c, a_r, cs_c, cs_r


def _ssd_gate(yraw, z, nw):
    yg = yraw * _silu(z)
    r = lax.rsqrt(jnp.mean(yg * yg, axis=1, keepdims=True) + LN_EPS)
    return yg, r


def ssd_fwd(xbc, proj, dtT, prow, pcol, nw, *, name):
    t = xbc.shape[0]
    q, p, ns = SSD_CHUNK, SSD_HEAD_DIM, SSD_STATE
    nc = t // q

    def body(xbc_ref, z_ref, dt_ref, dtT_ref, prow_ref, pcol_ref, nw_ref, y_ref, yraw_ref, sall_ref, s_ref, ybuf):
        @pl.when(pl.program_id(0) == 0)
        def _():
            s_ref[...] = jnp.zeros_like(s_ref)

        sall_ref[0] = s_ref[...]
        low, tril, triu, raw_c, dt_c, a_r, cs_c, cs_r = _ssd_common(dt_ref, dtT_ref, prow_ref, pcol_ref)
        d_r = prow_ref[2:3, :]
        bm = [xbc_ref[:, pl.ds(SSD_WIDTH + g * ns, ns)] for g in range(2)]
        cm = [xbc_ref[:, pl.ds(SSD_WIDTH + 2 * ns + g * ns, ns)] for g in range(2)]
        cb = [_dot(cm[g], bm[g], _NT) for g in range(2)]
        for h in range(SSD_HEADS):
            g = h // 4
            hs = pl.ds(h * p, p)
            csc, csr = cs_c[:, h:h + 1], cs_r[h:h + 1, :]
            lmat = jnp.exp(jnp.where(low, csc - csr, -1e30))
            xs = xbc_ref[:, hs]
            xdt = xs * dt_c[:, h:h + 1]
            sh = s_ref[hs, :]
            y = (_dot(cb[g] * lmat, xdt) + jnp.exp(csc) * _dot(cm[g], sh, _NT)
                 + xs * d_r[:, h:h + 1])
            ybuf[:, hs] = y
            cl = csc[q - 1:q, :]
            s_ref[hs, :] = jnp.exp(cl) * sh + _dot(xdt * jnp.exp(cl - csc), bm[g], _TN)
        yraw = ybuf[...]
        yraw_ref[...] = yraw
        yg, r = _ssd_gate(yraw, z_ref[...], nw_ref[...])
        y_ref[...] = yg * r * nw_ref[...]

    return pl.pallas_call(
        body, name=name, grid=(nc,),
        in_specs=[pl.BlockSpec((q, SSD_XBC), lambda i: (i, 0)),
                  pl.BlockSpec((q, SSD_WIDTH), lambda i: (i, P_Z // SSD_WIDTH)),
                  pl.BlockSpec((q, LANE), lambda i: (i, P_DT // LANE)),
                  pl.BlockSpec((SSD_HEADS, q), lambda i: (0, i)),
                  pl.BlockSpec((8, LANE), lambda i: (0, 0)), pl.BlockSpec((8, LANE), lambda i: (0, 0)),
                  pl.BlockSpec((1, SSD_WIDTH), lambda i: (0, 0))],
        out_specs=[pl.BlockSpec((q, SSD_WIDTH), lambda i: (i, 0)), pl.BlockSpec((q, SSD_WIDTH), lambda i: (i, 0)),
                   pl.BlockSpec((1, SSD_WIDTH, ns), lambda i: (i, 0, 0))],
        out_shape=[jax.ShapeDtypeStruct((t, SSD_WIDTH), F32), jax.ShapeDtypeStruct((t, SSD_WIDTH), F32),
                   jax.ShapeDtypeStruct((nc, SSD_WIDTH, ns), F32)],
        scratch_shapes=[pltpu.VMEM((SSD_WIDTH, ns), F32), pltpu.VMEM((q, SSD_WIDTH), F32)],
        compiler_params=_params(("arbitrary",)),
    )(xbc, proj, proj, dtT, prow, pcol, nw)


def ssd_bwd(xbc, proj, dtT, prow, pcol, nw, yraw, sall, dout, *, name):
    t = xbc.shape[0]
    q, p, ns = SSD_CHUNK, SSD_HEAD_DIM, SSD_STATE
    nc = t // q

    def body(xbc_ref, z_ref, dt_ref, dtT_ref, prow_ref, pcol_ref, nw_ref, yraw_ref, sall_ref, dout_ref,
             dxbc_ref, dz_ref, ddt_ref, dprm_ref, dnw_ref, ds_ref, dyb):
        @pl.when(pl.program_id(0) == 0)
        def _():
            ds_ref[...] = jnp.zeros_like(ds_ref)
            dprm_ref[...] = jnp.zeros_like(dprm_ref)
            dnw_ref[...] = jnp.zeros_like(dnw_ref)

        yraw, z, nwv, dout = yraw_ref[...], z_ref[...], nw_ref[...], dout_ref[...]
        yg, r = _ssd_gate(yraw, z, nwv)
        dnw_ref[...] += _colsum(dout * yg * r)
        dyn = dout * nwv
        dyg = r * dyn - yg * (r * r * r) * jnp.mean(dyn * yg, axis=1, keepdims=True)
        dyb[...] = dyg * _silu(z)
        dz_ref[...] = dyg * yraw * _dsilu(z)

        low, tril, triu, raw_c, dt_c, a_r, cs_c, cs_r = _ssd_common(dt_ref, dtT_ref, prow_ref, pcol_ref)
        d_r = prow_ref[2:3, :]
        lane = lax.broadcasted_iota(jnp.int32, (1, LANE), 1)
        ones = jnp.ones((q, LANE), F32)
        last = (lax.broadcasted_iota(jnp.int32, (q, 1), 0) == q - 1).astype(F32)
        bm = [xbc_ref[:, pl.ds(SSD_WIDTH + g * ns, ns)] for g in range(2)]
        cm = [xbc_ref[:, pl.ds(SSD_WIDTH + 2 * ns + g * ns, ns)] for g in range(2)]
        cb = [_dot(cm[g], bm[g], _NT) for g in range(2)]
        dbm = [jnp.zeros((q, ns), F32) for _ in range(2)]
        dcm = [jnp.zeros((q, ns), F32) for _ in range(2)]
        dcs_all = jnp.zeros((q, LANE), F32)
        ddt_all = jnp.zeros((q, LANE), F32)
        dd_all = jnp.zeros((1, LANE), F32)
        for h in range(SSD_HEADS):
            g = h // 4
            hs = pl.ds(h * p, p)
            onehot = (lane == h).astype(F32)
            csc, csr = cs_c[:, h:h + 1], cs_r[h:h + 1, :]
            lmat = jnp.exp(jnp.where(low, csc - csr, -1e30))
            xs = xbc_ref[:, hs]
            dth = dt_c[:, h:h + 1]
            xdt = xs * dth
            sh = sall_ref[0, hs, :]
            dy = dyb[:, hs]
            ecs = jnp.exp(csc)
            cl = csc[q - 1:q, :]
            ecl = jnp.exp(cl)
            wdec = jnp.exp(cl - csc)
            wmat = cb[g] * lmat
            dwm = _dot(dy, xdt, _NT)
            dx = _dot(wmat, dy, _TN)
            emat = dwm * wmat
            dmm = dwm * lmat
            dcm[g] = dcm[g] + _dot(dmm, bm[g])
            dbm[g] = dbm[g] + _dot(dmm, cm[g], _TN)
            dcs = _rowsum(emat) - _dot_mask(emat, ones, _TN, mask_left=False, parts=2)[:, 0:1]
            zmat = _dot(cm[g], sh, _NT)
            dzm = ecs * dy
            dcm[g] = dcm[g] + _dot(dzm, sh)
            dsp = _dot(dzm, cm[g], _TN)
            dcs = dcs + _rowsum(dzm * zmat)
            dsn = ds_ref[hs, :]
            dsp = dsp + ecl * dsn
            dcl = _colsum(_rowsum(dsn * sh)) * ecl
            xw = xdt * wdec
            dxw = _dot(bm[g], dsn, _NT)
            dbm[g] = dbm[g] + _dot(xw, dsn)
            dx = dx + wdec * dxw
            tw = _rowsum(dxw * xdt) * wdec
            dcl = dcl + _colsum(tw)
            dcs = dcs - tw + last * dcl
            ds_ref[hs, :] = dsp
            dxbc_ref[:, hs] = dx * dth + dy * d_r[:, h:h + 1]
            dcs_all = dcs_all + dcs * onehot
            ddt_all = ddt_all + _rowsum(dx * xs) * onehot
            dd_all = dd_all + _colsum(_rowsum(dy * xs)) * onehot
        for g in range(2):
            dxbc_ref[:, pl.ds(SSD_WIDTH + g * ns, ns)] = dbm[g]
            dxbc_ref[:, pl.ds(SSD_WIDTH + 2 * ns + g * ns, ns)] = dcm[g]
        dadt = _dot_mask(triu, dcs_all, mask_left=True, parts=2)
        ddt = ddt_all + dadt * a_r
        draw = ddt * _sigmoid(raw_c)
        ddt_ref[...] = draw
        zero = jnp.zeros((5, LANE), F32)
        dprm_ref[...] += jnp.concatenate([_colsum(draw), _colsum(dadt * dt_c) * a_r, dd_all, zero], axis=0)

    rev = lambda cbk: (lambda i: (nc - 1 - i, cbk))
    return pl.pallas_call(
        body, name=name, grid=(nc,),
        in_specs=[pl.BlockSpec((q, SSD_XBC), rev(0)),
                  pl.BlockSpec((q, SSD_WIDTH), rev(P_Z // SSD_WIDTH)),
                  pl.BlockSpec((q, LANE), rev(P_DT // LANE)),
                  pl.BlockSpec((SSD_HEADS, q), lambda i: (0, nc - 1 - i)),
                  pl.BlockSpec((8, LANE), lambda i: (0, 0)), pl.BlockSpec((8, LANE), lambda i: (0, 0)),
                  pl.BlockSpec((1, SSD_WIDTH), lambda i: (0, 0)),
                  pl.BlockSpec((q, SSD_WIDTH), rev(0)),
                  pl.BlockSpec((1, SSD_WIDTH, ns), lambda i: (nc - 1 - i, 0, 0)),
                  pl.BlockSpec((q, SSD_WIDTH), rev(0))],
        out_specs=[pl.BlockSpec((q, SSD_XBC), rev(0)), pl.BlockSpec((q, SSD_WIDTH), rev(0)),
                   pl.BlockSpec((q, LANE), rev(0)),
                   pl.BlockSpec((8, LANE), lambda i: (0, 0)), pl.BlockSpec((1, SSD_WIDTH), lambda i: (0, 0))],
        out_shape=[jax.ShapeDtypeStruct((t, SSD_XBC), F32), jax.ShapeDtypeStruct((t, SSD_WIDTH), F32),
                   jax.ShapeDtypeStruct((t, LANE), F32), jax.ShapeDtypeStruct((8, LANE), F32),
                   jax.ShapeDtypeStruct((1, SSD_WIDTH), F32)],
        scratch_shapes=[pltpu.VMEM((SSD_WIDTH, ns), F32), pltpu.VMEM((q, SSD_WIDTH), F32)],
        compiler_params=_params(("arbitrary",)),
    )(xbc, proj, proj, dtT, prow, pcol, nw, yraw, sall, dout)


def _me():
    return lax.axis_index("x"), lax.axis_index("y"), lax.axis_index("c")


_ANY = pl.BlockSpec(memory_space=pl.ANY)
_MESH = pl.DeviceIdType.MESH


def all_gather(block, *, name):
    def body(src, dst, send_sems, recv_sems, local_sem):
        x, y, c = _me()
        me, sibling = (x, y, c), (x, y, 1 - c)
        chips = [(1 - x, y), (x, 1 - y), (1 - x, 1 - y)]

        def slot(px, py, pc):
            return dst.at[4 * px + 2 * py + pc]

        def copy(kk, blk, to, from_src=False):
            return pltpu.make_async_remote_copy(
                src_ref=src if from_src else slot(*blk), dst_ref=slot(*blk),
                send_sem=send_sems.at[kk], recv_sem=recv_sems.at[kk], device_id=to, device_id_type=_MESH)

        mine = pltpu.make_async_copy(src, slot(*me), local_sem)
        mine.start()
        first = [copy(0, me, sibling, True)] + [copy(1 + j, me, (*chip, c), True) for j, chip in enumerate(chips)]
        for cp in first:
            cp.start()
        passed = [copy(4 + j, (*chip, c), sibling) for j, chip in enumerate(chips)]
        for j, chip in enumerate(chips):
            copy(1 + j, (*chip, c), me).wait_recv()
            passed[j].start()
        copy(0, sibling, me).wait_recv()
        for j, chip in enumerate(chips):
            copy(4 + j, (*chip, 1 - c), me).wait_recv()
        for cp in first + passed:
            cp.wait_send()
        mine.wait()

    return pl.pallas_call(
        body, name=name, in_specs=[_ANY], out_specs=_ANY,
        out_shape=jax.ShapeDtypeStruct((N_DEV,) + block.shape, block.dtype),
        scratch_shapes=[pltpu.SemaphoreType.DMA((7,)), pltpu.SemaphoreType.DMA((7,)), pltpu.SemaphoreType.DMA(())],
    )(block)


def rs_sibling_exchange(halves, *, name):
    def body(src, dst, send_sem, recv_sem, local_sem):
        x, y, c = _me()
        local = pltpu.make_async_copy(src.at[c], dst.at[0], local_sem)
        local.start()
        cp = pltpu.make_async_remote_copy(src_ref=src.at[1 - c], dst_ref=dst.at[1], send_sem=send_sem, recv_sem=recv_sem,
                                          device_id=(x, y, 1 - c), device_id_type=_MESH)
        cp.start()
        cp.wait()
        local.wait()

    return pl.pallas_call(
        body, name=name, in_specs=[_ANY], out_specs=_ANY,
        out_shape=jax.ShapeDtypeStruct(halves.shape, halves.dtype),
        scratch_shapes=[pltpu.SemaphoreType.DMA(()), pltpu.SemaphoreType.DMA(()), pltpu.SemaphoreType.DMA(())],
    )(halves)


def pair_sum_bf16(pair, *, name, tt=1024):
    _, nq, r, _ = pair.shape

    def body(p_ref, o_ref):
        o_ref[...] = (p_ref[0] + p_ref[1]).astype(BF16)

    return pl.pallas_call(
        body, name=name, grid=(nq, r // tt),
        in_specs=[pl.BlockSpec((2, None, tt, LANE), lambda q, i: (0, q, i, 0))],
        out_specs=pl.BlockSpec((None, tt, LANE), lambda q, i: (q, i, 0)),
        out_shape=jax.ShapeDtypeStruct((nq, r, LANE), BF16),
        compiler_params=_params(("parallel", "parallel")),
    )(pair)


def rs_chip_exchange(part, *, name):
    def body(src, dst, send_sems, recv_sems, local_sem):
        x, y, c = _me()
        q_me = 2 * x + y
        local = pltpu.make_async_copy(src.at[q_me], dst.at[q_me], local_sem)
        local.start()
        copies = []
        for j, (px, py) in enumerate([(1 - x, y), (x, 1 - y), (1 - x, 1 - y)]):
            cp = pltpu.make_async_remote_copy(src_ref=src.at[2 * px + py], dst_ref=dst.at[q_me], send_sem=send_sems.at[j],
                                              recv_sem=recv_sems.at[j], device_id=(px, py, c), device_id_type=_MESH)
            cp.start()
            copies.append(cp)
        for cp in copies:
            cp.wait()
        local.wait()

    return pl.pallas_call(
        body, name=name, in_specs=[_ANY], out_specs=_ANY,
        out_shape=jax.ShapeDtypeStruct(part.shape, part.dtype),
        scratch_shapes=[pltpu.SemaphoreType.DMA((3,)), pltpu.SemaphoreType.DMA((3,)), pltpu.SemaphoreType.DMA(())],
    )(part)


def adamw(slabs, w, m, v, *, name, tt):
    ns, r = slabs.shape[0], w.shape[0]
    tt = min(tt, r)
    assert r % tt == 0

    def body(s_ref, w_ref, m_ref, v_ref, g_ref, d_ref, nm_ref, nv_ref):
        g = s_ref[0].astype(F32)
        for kdev in range(1, ns):
            g = g + s_ref[kdev].astype(F32)
        wv = w_ref[...]
        nm = ADAM_B1 * m_ref[...] + (1.0 - ADAM_B1) * g
        nv = ADAM_B2 * v_ref[...] + (1.0 - ADAM_B2) * (g * g)
        m_hat = nm / (1.0 - ADAM_B1 ** ADAM_STEP)
        v_hat = nv / (1.0 - ADAM_B2 ** ADAM_STEP)
        g_ref[...] = g
        d_ref[...] = -ADAM_LR * (m_hat / (jnp.sqrt(v_hat) + ADAM_EPS) + ADAM_WD * wv)
        nm_ref[...] = nm
        nv_ref[...] = nv

    spec = pl.BlockSpec((tt, LANE), lambda i: (i, 0))
    return pl.pallas_call(
        body, name=name, grid=(r // tt,),
        in_specs=[pl.BlockSpec((ns, tt, LANE), lambda i: (0, i, 0)), spec, spec, spec],
        out_specs=[spec] * 4, out_shape=[jax.ShapeDtypeStruct((r, LANE), F32)] * 4,
        compiler_params=_params(("parallel",)),
    )(slabs, w, m, v)


SHARDED = [("w_in", 1), ("w_out", 1), ("ssd_conv_w", 2), ("s5_glu_w", 1), ("rg_conv_w", 2),
           ("xa_wq", 1), ("xa_wk", 1), ("xa_wv", 1), ("xa_wo", 1), ("mlp_w1", 2), ("mlp_w2", 1)]
KEEP_F32 = ("ssd_conv_w", "rg_conv_w")
SMALL = ["ssd_conv_b", "ssd_dt_bias", "ssd_a_log", "ssd_d", "ssd_norm_w", "s5_lam_re", "s5_lam_im",
         "s5_log_step", "s5_b_re", "s5_b_im", "s5_c_re", "s5_c_im", "s5_d", "s5_glu_b", "rg_conv_b",
         "rg_wa", "rg_ba", "rg_wx", "rg_bx", "rg_lambda", "ln1_g", "ln1_b", "ln2_g", "ln2_b", "ln3_g", "ln3_b"]
WEIGHTS = ['w_in', 'w_out', 'ssd_conv_w', 'ssd_conv_b', 'ssd_dt_bias', 'ssd_a_log', 'ssd_d', 'ssd_norm_w',
           's5_lam_re', 's5_lam_im', 's5_log_step', 's5_b_re', 's5_b_im', 's5_c_re', 's5_c_im', 's5_d',
           's5_glu_w', 's5_glu_b', 'rg_conv_w', 'rg_conv_b', 'rg_wa', 'rg_ba', 'rg_wx', 'rg_bx', 'rg_lambda',
           'ln1_g', 'ln1_b', 'xa_wq', 'xa_wk', 'xa_wv', 'xa_wo', 'ln2_g', 'ln2_b', 'mlp_w1', 'mlp_w2',
           'ln3_g', 'ln3_b']


def _pack_rows(flat, mult):
    n = flat.shape[-1]
    r = -(-n // (LANE * mult)) * mult
    pad = [(0, 0)] * (flat.ndim - 1) + [(0, r * LANE - n)]
    return jnp.pad(flat, pad).reshape(flat.shape[:-1] + (r, LANE))


def _unpack(packed, shapes):
    lead = packed.shape[:-2]
    flat = packed.reshape(lead + (-1,))
    out, off = [], 0
    for s in shapes:
        n = math.prod(s)
        out.append(flat[..., off:off + n].reshape(lead + tuple(s)))
        off += n
    return out


def _to_full(gathered, axis):
    g = jnp.moveaxis(gathered, 0, axis)
    s = g.shape
    return g.reshape(s[:axis] + (s[axis] * s[axis + 1],) + s[axis + 2:])


def _to_slabs(full, axis):
    s = full.shape
    g = full.reshape(s[:axis] + (N_DEV, s[axis] // N_DEV) + s[axis + 1:])
    return jnp.moveaxis(g, axis, 0)


def _blockdiag(w):
    h, i, j = w.shape
    eye = jnp.eye(h, dtype=w.dtype)
    return (w[:, :, None, :] * eye[:, None, :, None]).reshape(h * i, h * j)


def _blockdiag_extract(m, h):
    i, j = m.shape[0] // h, m.shape[1] // h
    eye = jnp.eye(h, dtype=m.dtype)
    return (m.reshape(h, i, h, j) * eye[:, None, :, None]).sum(axis=2)


def _s5_disc(lr, li, ls, bre, bim):
    step = jnp.exp(ls)[:, None]
    er = jnp.exp(lr * step)
    ar, ai = er * jnp.cos(li * step), er * jnp.sin(li * step)
    nr, ni, den = ar - 1.0, ai, lr * lr + li * li
    qr, qi = (nr * lr + ni * li) / den, (ni * lr - nr * li) / den
    bbr = qr[..., None] * bre - qi[..., None] * bim
    bbi = qr[..., None] * bim + qi[..., None] * bre
    return ar, ai, bbr, bbi


def _row(v, width=None):
    v = v.reshape(1, -1)
    if width is not None and v.shape[1] < width:
        v = jnp.pad(v, ((0, 0), (0, width - v.shape[1])))
    return v


def _relu2(a):
    r = jnp.maximum(a, 0.0)
    return r * r


def _add_alpha(acc, d):
    return acc + ALPHA * d


def _shift_rows_down(x):
    return jnp.concatenate([jnp.zeros((1, x.shape[1]), x.dtype), x[:-1]], axis=0)


def _shift_rows_up(x):
    return jnp.concatenate([x[1:], jnp.zeros((1, x.shape[1]), x.dtype)], axis=0)


def _layer_params(full, small, l):
    p = {}
    w_in = full["w_in"][l]
    z, xbc, dt, u, xr, g = w_in[0:512], w_in[512:1536], w_in[1536:1544], w_in[1544:1800], w_in[1800:2056], w_in[2056:2312]
    p["w_inp"] = jnp.concatenate([xbc, z, u, xr, g, dt, jnp.zeros((D_INP - P_DT - 8, D_MODEL), w_in.dtype)], axis=0)
    for k_ in ("w_out", "xa_wq", "xa_wk", "xa_wv", "xa_wo", "mlp_w1", "mlp_w2", "s5_glu_w"):
        p[k_] = full[k_][l]
    p["ssd_cw"], p["ssd_cb"] = full["ssd_conv_w"][l], _row(small["ssd_conv_b"][l])
    dtb, alog, dsk = small["ssd_dt_bias"][l], small["ssd_a_log"][l], small["ssd_d"][l]
    p["prow"] = jnp.concatenate([_row(dtb, LANE), _row(alog, LANE), _row(dsk, LANE), jnp.zeros((5, LANE), F32)], axis=0)
    p["pcol"] = jnp.pad(jnp.stack([dtb, alog], axis=1), ((0, 0), (0, LANE - 2)))
    p["ssd_nw"] = _row(small["ssd_norm_w"][l])
    s5_in = (small["s5_lam_re"][l], small["s5_lam_im"][l], small["s5_log_step"][l], small["s5_b_re"][l], small["s5_b_im"][l])
    (ar, ai, bbr, bbi), p["s5_vjp"] = jax.vjp(_s5_disc, *s5_in)
    p["lam_fwd"] = jnp.concatenate([_row(ar), _row(ai)], axis=1)
    p["lam_adj"] = jnp.concatenate([_row(ar), _row(-ai)], axis=1)
    p["bcat"] = jnp.concatenate([_blockdiag(jnp.swapaxes(bbr, 1, 2)), _blockdiag(jnp.swapaxes(bbi, 1, 2))], axis=1)
    p["ccat"] = jnp.concatenate([_blockdiag(jnp.swapaxes(small["s5_c_re"][l], 1, 2)),
                                 -_blockdiag(jnp.swapaxes(small["s5_c_im"][l], 1, 2))], axis=0)
    p["s5_d"], p["s5_glu_b"] = _row(small["s5_d"][l]), _row(small["s5_glu_b"][l])
    p["rg_cw"], p["rg_cb"] = full["rg_conv_w"][l], _row(small["rg_conv_b"][l])
    p["rg_wa"], p["rg_wx"] = _blockdiag(small["rg_wa"][l]), _blockdiag(small["rg_wx"][l])
    p["rg_ba"], p["rg_bx"], p["rg_lam"] = _row(small["rg_ba"][l]), _row(small["rg_bx"][l]), _row(small["rg_lambda"][l])
    for i in (1, 2, 3):
        p[f"g{i}"], p[f"b{i}"] = _row(small[f"ln{i}_g"][l]), _row(small[f"ln{i}_b"][l])
    return p


def _layer_fwd(h0, mem, p):
    t = h0.shape[0]
    s = {"h0": h0}
    proj = mm(h0, p["w_inp"], tb=True, name="in_proj")
    dtT = proj[:, P_DT:P_DT + SSD_HEADS].T
    xbc = conv_fwd(proj, 0, p["ssd_cw"], p["ssd_cb"], width=SSD_XBC, act=True, name="ssd_conv_fwd")
    y_ssd, yraw, sall = ssd_fwd(xbc, proj, dtT, p["prow"], p["pcol"], p["ssd_nw"], name="ssd_fwd")
    bu = mm(proj, p["bcat"], a_off=P_U, k=S5_WIDTH, name="s5_bu")
    hs5 = scan_complex(bu, p["lam_fwd"], reverse=False, name="s5_scan_fwd")
    ylin = mm(hs5, p["ccat"], name="s5_ylin")
    (y_s5,), _ = rowk(_s5_post_fwd_fn, [(ylin, S5_WIDTH, 0), (proj, S5_WIDTH, P_U // S5_WIDTH)],
                      [p["s5_d"], p["s5_glu_w"], p["s5_glu_b"]], [S5_WIDTH], [], rows=t, name="s5_post_fwd")
    xc = conv_fwd(proj, P_XR // RG_WIDTH, p["rg_cw"], p["rg_cb"], width=RG_WIDTH, act=False, name="rg_conv_fwd")
    rg_full = [p["rg_wa"], p["rg_wx"], p["rg_ba"], p["rg_bx"], p["rg_lam"]]
    (a_rg, b_rg), _ = rowk(_rg_pre_fwd_fn, [(xc, RG_WIDTH, 0)], rg_full, [RG_WIDTH, RG_WIDTH], [], rows=t, name="rg_pre_fwd")
    h_rg = scan_real(a_rg, b_rg, reverse=False, name="rg_scan_fwd")
    (y_rg,), _ = rowk(_rg_out_fwd_fn, [(h_rg, RG_WIDTH, 0), (proj, RG_WIDTH, P_G // RG_WIDTH)], [], [RG_WIDTH], [],
                      rows=t, name="rg_out_fwd")
    ycat = jnp.concatenate([y_ssd, y_s5, y_rg], axis=1)
    mix = mm(ycat, p["w_out"], name="out_proj")
    h1 = ln_fwd(h0, mix, p["g1"], p["b1"], name="ln_fwd")
    q = mm(h1, p["xa_wq"], name="xa_q")
    k = mm(mem, p["xa_wk"], name="xa_kv")
    v = mm(mem, p["xa_wv"], name="xa_kv")
    (o,), _ = rowk(_attn_fwd_fn, [(q, D_MODEL, 0)], [k, v], [D_MODEL], [], rows=t, name="xa_fwd")
    att = mm(o, p["xa_wo"], name="xa_o")
    h2 = ln_fwd(h1, att, p["g2"], p["b2"], name="ln_fwd")
    a_mlp = mm(h2, p["mlp_w1"], name="mlp_up")
    m_out = mm(a_mlp, p["mlp_w2"], fa=_relu2, name="mlp_down")
    h3 = ln_fwd(h2, m_out, p["g3"], p["b3"], name="ln_fwd")
    s.update(proj=proj, dtT=dtT, xbc=xbc, yraw=yraw, sall=sall, hs5=hs5, ylin=ylin, xc=xc, a_rg=a_rg, h_rg=h_rg,
             ycat=ycat, mix=mix, h1=h1, q=q, k=k, v=v, o=o, att=att, h2=h2, a_mlp=a_mlp, m_out=m_out)
    return h3, s


def _layer_bwd(dh3, mem, p, s, l, gfull, gsmall):
    t = dh3.shape[0]
    proj = s["proj"]
    dpre3, dg3, db3 = ln_bwd(s["h2"], s["m_out"], dh3, p["g3"], name="ln_bwd")
    da = mm(dpre3, p["mlp_w2"], tb=True, o_extra=(s["a_mlp"],), fo=lambda acc, a: acc * 2.0 * jnp.maximum(a, 0.0), name="mlp_da")
    gfull["mlp_w2"][l] = mm(s["a_mlp"], dpre3, ta=True, fa=_relu2, name="mlp_dw2")
    gfull["mlp_w1"][l] = mm(s["h2"], da, ta=True, name="mlp_dw1")
    dh2 = mm(da, p["mlp_w1"], tb=True, o_extra=(dpre3,), fo=_add_alpha, name="mlp_dx")
    dpre2, dg2, db2 = ln_bwd(s["h1"], s["att"], dh2, p["g2"], name="ln_bwd")
    do = mm(dpre2, p["xa_wo"], tb=True, name="xa_do")
    gfull["xa_wo"][l] = mm(s["o"], dpre2, ta=True, name="dw_sq")
    (dq,), (dk, dv) = rowk(_attn_bwd_fn, [(s["q"], D_MODEL, 0), (do, D_MODEL, 0)], [s["k"], s["v"]], [D_MODEL],
                           [(256, D_MODEL), (256, D_MODEL)], rows=t, name="xa_bwd")
    gfull["xa_wq"][l] = mm(s["h1"], dq, ta=True, name="dw_sq")
    gfull["xa_wk"][l] = mm(mem, dk, ta=True, name="dw_kv")
    gfull["xa_wv"][l] = mm(mem, dv, ta=True, name="dw_kv")
    dh1 = mm(dq, p["xa_wq"], tb=True, o_extra=(dpre2,), fo=_add_alpha, name="dx_sq")
    dpre1, dg1, db1 = ln_bwd(s["h0"], s["mix"], dh1, p["g1"], name="ln_bwd")
    dycat = mm(dpre1, p["w_out"], tb=True, name="xa_do")
    gfull["w_out"][l] = mm(s["ycat"], dpre1, ta=True, name="dw_sq")
    (dh_rg, dg_rg), _ = rowk(_rg_out_bwd_fn, [(s["h_rg"], RG_WIDTH, 0), (proj, RG_WIDTH, P_G // RG_WIDTH), (dycat, RG_WIDTH, 3)],
                             [], [RG_WIDTH, RG_WIDTH], [], rows=t, name="rg_out_bwd")
    g_rg = scan_real(_shift_rows_up(s["a_rg"]), dh_rg, reverse=True, name="rg_scan_bwd")
    rg_full = [p["rg_wa"], p["rg_wx"], p["rg_ba"], p["rg_bx"], p["rg_lam"]]
    (dxc,), (dwa, dwx, dba, dbx, dlam) = rowk(
        _rg_pre_bwd_fn, [(s["xc"], RG_WIDTH, 0), (g_rg, RG_WIDTH, 0), (_shift_rows_down(s["h_rg"]), RG_WIDTH, 0)], rg_full,
        [RG_WIDTH], [(RG_WIDTH, RG_WIDTH), (RG_WIDTH, RG_WIDTH), (1, RG_WIDTH), (1, RG_WIDTH), (1, RG_WIDTH)],
        rows=t, name="rg_pre_bwd")
    dxr, d_rgcw, d_rgcb = conv_bwd(proj, P_XR // RG_WIDTH, dxc, p["rg_cw"], p["rg_cb"], width=RG_WIDTH, act=False, name="rg_conv_bwd")
    (dylin, du_a), (d_s5d, d_gluw, d_glub) = rowk(
        _s5_post_bwd_fn, [(s["ylin"], S5_WIDTH, 0), (proj, S5_WIDTH, P_U // S5_WIDTH), (dycat, S5_WIDTH, 2)],
        [p["s5_d"], p["s5_glu_w"], p["s5_glu_b"]], [S5_WIDTH, S5_WIDTH],
        [(1, S5_WIDTH), (S5_WIDTH, S5_WIDTH), (1, S5_WIDTH)], rows=t, name="s5_post_bwd")
    dhs = mm(dylin, p["ccat"], tb=True, name="s5_dh")
    dccat = mm(s["hs5"], dylin, ta=True, name="s5_dc")
    gs5 = scan_complex(dhs, p["lam_adj"], reverse=True, name="s5_scan_bwd")
    dar, dai = s5_dlam(gs5, _shift_rows_down(s["hs5"]), name="s5_dlam")
    du = mm(gs5, p["bcat"], tb=True, o_extra=(du_a,), fo=lambda acc, d: acc + d, name="s5_du")
    dbcat = mm(proj, gs5, ta=True, a_off=P_U, m=S5_WIDTH, name="s5_db")
    dxbc_act, dz, ddt, dprm, dnw = ssd_bwd(s["xbc"], proj, s["dtT"], p["prow"], p["pcol"], p["ssd_nw"], s["yraw"],
                                          s["sall"], dycat, name="ssd_bwd")
    dxbc, d_scw, d_scb = conv_bwd(proj, 0, dxbc_act, p["ssd_cw"], p["ssd_cb"], width=SSD_XBC, act=True, name="ssd_conv_bwd")
    dproj = jnp.concatenate([dxbc, dz, du, dxr, dg_rg, ddt, jnp.zeros((t, D_INP - P_DT - LANE), F32)], axis=1)
    dh0 = mm(dproj, p["w_inp"], o_extra=(dpre1,), fo=_add_alpha, name="in_proj_dx")
    dwp = mm(dproj, s["h0"], ta=True, name="in_proj_dw")
    gfull["w_in"][l] = jnp.concatenate([dwp[P_Z:P_Z + 512], dwp[P_XBC:P_XBC + 1024], dwp[P_DT:P_DT + 8],
                                        dwp[P_U:P_U + 256], dwp[P_XR:P_XR + 256], dwp[P_G:P_G + 256]], axis=0)
    gfull["ssd_conv_w"][l], gfull["rg_conv_w"][l], gfull["s5_glu_w"][l] = d_scw, d_rgcw, d_gluw
    ng, ns = S5_GROUPS, S5_STATE
    dbbr = jnp.swapaxes(_blockdiag_extract(dbcat[:, :S5_NSTATE], ng), 1, 2)
    dbbi = jnp.swapaxes(_blockdiag_extract(dbcat[:, S5_NSTATE:], ng), 1, 2)
    d_lr, d_li, d_ls, d_bre, d_bim = p["s5_vjp"]((dar.reshape(ng, ns), dai.reshape(ng, ns), dbbr, dbbi))
    gsmall["s5_lam_re"][l], gsmall["s5_lam_im"][l], gsmall["s5_log_step"][l] = d_lr, d_li, d_ls
    gsmall["s5_b_re"][l], gsmall["s5_b_im"][l] = d_bre, d_bim
    gsmall["s5_c_re"][l] = jnp.swapaxes(_blockdiag_extract(dccat[:S5_NSTATE], ng), 1, 2)
    gsmall["s5_c_im"][l] = -jnp.swapaxes(_blockdiag_extract(dccat[S5_NSTATE:], ng), 1, 2)
    gsmall["s5_d"][l], gsmall["s5_glu_b"][l] = d_s5d[0], d_glub[0]
    gsmall["ssd_conv_b"][l], gsmall["rg_conv_b"][l] = d_scb[0], d_rgcb[0]
    gsmall["ssd_dt_bias"][l], gsmall["ssd_a_log"][l], gsmall["ssd_d"][l] = dprm[0, :8], dprm[1, :8], dprm[2, :8]
    gsmall["ssd_norm_w"][l] = dnw[0]
    gsmall["rg_wa"][l], gsmall["rg_wx"][l] = _blockdiag_extract(dwa, RG_BLOCKS), _blockdiag_extract(dwx, RG_BLOCKS)
    gsmall["rg_ba"][l], gsmall["rg_bx"][l] = dba.reshape(RG_BLOCKS, RG_BLOCK_DIM), dbx.reshape(RG_BLOCKS, RG_BLOCK_DIM)
    gsmall["rg_lambda"][l] = dlam[0]
    for i, (dg, db) in zip((1, 2, 3), ((dg1, db1), (dg2, db2), (dg3, db3))):
        gsmall[f"ln{i}_g"][l], gsmall[f"ln{i}_b"][l] = dg[0], db[0]
    return dh0


def _step(a):
    a = dict(a)
    for pre in ("", "m_", "v_"):
        a[pre + "w_in"] = jnp.swapaxes(a[pre + "w_in"], 1, 2)
    h = a["x"][0]
    mem = a["mem"][0]
    t = h.shape[0]
    pieces = []
    for name, _ in SHARDED:
        w = a[name]
        if name in KEEP_F32:
            pieces.append(lax.bitcast_convert_type(w, BF16).reshape(-1))
        else:
            pieces.append(w.astype(BF16).reshape(-1))
    gathered = all_gather(_pack_rows(jnp.concatenate(pieces), 16), name="ag_weights")
    shapes = [a[name].shape + ((2,) if name in KEEP_F32 else ()) for name, _ in SHARDED]
    full = {}
    for (name, axis), g in zip(SHARDED, _unpack(gathered, shapes)):
        if name in KEEP_F32:
            g = lax.bitcast_convert_type(g, F32)
        full[name] = _to_full(g, axis)
    small = {name: a[name] for name in SMALL}
    params, saved = [], []
    for l in range(DEPTH):
        p = _layer_params(full, small, l)
        h, s = _layer_fwd(h, mem, p)
        params.append(p)
        saved.append(s)
    (dh,), (loss_part,) = rowk(_loss_fn, [(h, D_MODEL, 0), (a["loss_target"][0], D_MODEL, 0)], [], [D_MODEL], [(1, 1)],
                               rows=t, name="loss_head")
    loss = lax.psum(loss_part[0, 0], ("x", "y", "c"))
    gfull = {name: [None] * DEPTH for name, _ in SHARDED}
    gsmall = {name: [None] * DEPTH for name in SMALL}
    for l in reversed(range(DEPTH)):
        dh = _layer_bwd(dh, mem, params[l], saved[l], l, gfull, gsmall)
    grad_x = dh[None]
    slabs = jnp.concatenate([_to_slabs(jnp.stack(gfull[name]), axis).reshape(N_DEV, -1) for name, axis in SHARDED], axis=1)
    halves = jnp.swapaxes(_pack_rows(slabs, 1024).reshape(4, 2, -1, LANE), 0, 1)
    pair = rs_sibling_exchange(halves, name="rs_sibling")
    slabs = rs_chip_exchange(pair_sum_bf16(pair, name="rs_pair_sum"), name="rs_chips")
    pk = lambda pre: _pack_rows(jnp.concatenate([a[pre + name].reshape(-1) for name, _ in SHARDED]), 1024)
    big = adamw(slabs, pk(""), pk("m_"), pk("v_"), name="adamw_sharded", tt=1024)
    gs = _pack_rows(jnp.concatenate([jnp.stack(gsmall[name]).reshape(-1) for name in SMALL]), 8)
    gs = all_gather(gs, name="ag_small_grads")
    pks = lambda pre: _pack_rows(jnp.concatenate([a[pre + name].reshape(-1) for name in SMALL]), 8)
    sm = adamw(gs, pks(""), pks("m_"), pks("v_"), name="adamw_replicated", tt=gs.shape[1])
    out = {}
    for kind, bg, sg in zip(("grad_", "delta_", "new_m_", "new_v_"), big, sm):
        for (name, _), arr in zip(SHARDED, _unpack(bg, [a[name].shape for name, _ in SHARDED])):
            out[kind + name] = jnp.swapaxes(arr, 1, 2) if name == "w_in" else arr
        for name, arr in zip(SMALL, _unpack(sg, [a[name].shape for name in SMALL])):
            out[kind + name] = arr
    return (loss, grad_x) + tuple(out[kind + name] for kind in ("grad_", "delta_", "new_m_", "new_v_") for name in WEIGHTS)


def kernel(x, mem, w_in, w_out, ssd_conv_w, ssd_conv_b, ssd_dt_bias, ssd_a_log, ssd_d, ssd_norm_w, s5_lam_re, s5_lam_im, s5_log_step, s5_b_re, s5_b_im, s5_c_re, s5_c_im, s5_d, s5_glu_w, s5_glu_b, rg_conv_w, rg_conv_b, rg_wa, rg_ba, rg_wx, rg_bx, rg_lambda, ln1_g, ln1_b, xa_wq, xa_wk, xa_wv, xa_wo, ln2_g, ln2_b, mlp_w1, mlp_w2, ln3_g, ln3_b, loss_target, m_w_in, m_w_out, m_ssd_conv_w, m_ssd_conv_b, m_ssd_dt_bias, m_ssd_a_log, m_ssd_d, m_ssd_norm_w, m_s5_lam_re, m_s5_lam_im, m_s5_log_step, m_s5_b_re, m_s5_b_im, m_s5_c_re, m_s5_c_im, m_s5_d, m_s5_glu_w, m_s5_glu_b, m_rg_conv_w, m_rg_conv_b, m_rg_wa, m_rg_ba, m_rg_wx, m_rg_bx, m_rg_lambda, m_ln1_g, m_ln1_b, m_xa_wq, m_xa_wk, m_xa_wv, m_xa_wo, m_ln2_g, m_ln2_b, m_mlp_w1, m_mlp_w2, m_ln3_g, m_ln3_b, v_w_in, v_w_out, v_ssd_conv_w, v_ssd_conv_b, v_ssd_dt_bias, v_ssd_a_log, v_ssd_d, v_ssd_norm_w, v_s5_lam_re, v_s5_lam_im, v_s5_log_step, v_s5_b_re, v_s5_b_im, v_s5_c_re, v_s5_c_im, v_s5_d, v_s5_glu_w, v_s5_glu_b, v_rg_conv_w, v_rg_conv_b, v_rg_wa, v_rg_ba, v_rg_wx, v_rg_bx, v_rg_lambda, v_ln1_g, v_ln1_b, v_xa_wq, v_xa_wk, v_xa_wv, v_xa_wo, v_ln2_g, v_ln2_b, v_mlp_w1, v_mlp_w2, v_ln3_g, v_ln3_b):
    return _step(dict(locals()))
```

```python
import math

import jax
import jax.numpy as jnp
from jax import lax
from jax.experimental import pallas as pl
from jax.experimental.pallas import tpu as pltpu

F32 = jnp.float32
BF16 = jnp.bfloat16

N_DEV = 8
D_MODEL = 1024
DEPTH = 2
SSD_WIDTH = 512
SSD_HEADS = 8
SSD_HEAD_DIM = 64
SSD_STATE = 128
SSD_CHUNK = 128
SSD_XBC = 1024
S5_WIDTH = 256
S5_GROUPS = 16
S5_GROUP_CH = 16
S5_STATE = 64
S5_NSTATE = S5_GROUPS * S5_STATE
RG_WIDTH = 256
RG_BLOCKS = 4
RG_BLOCK_DIM = 64
RG_C = 8.0
XA_HEADS = 4
XA_HEAD_DIM = 256
ALPHA = (2.0 * DEPTH) ** 0.25
LN_EPS = 1e-5
ADAM_LR, ADAM_B1, ADAM_B2, ADAM_EPS, ADAM_WD, ADAM_STEP = 0.001, 0.9, 0.999, 1e-08, 0.01, 10

P_XBC, P_Z, P_U, P_XR, P_G, P_DT = 0, 1024, 1536, 1792, 2048, 2304
D_INP = 2560
LANE = 128
VMEM_LIMIT = 56 * 1024 * 1024
ROW_TILE = 512

_NN = ((1,), (0,))
_NT = ((1,), (1,))
_TN = ((0,), (0,))


def _dot(a, b, dims=_NN):
    return lax.dot_general(a.astype(BF16), b.astype(BF16), (dims, ((), ())), preferred_element_type=F32)


def _split_bf16(x, parts):
    out, rem = [], x
    for _ in range(parts):
        piece = rem.astype(BF16)
        out.append(piece)
        rem = rem - piece.astype(F32)
    return out


def _dot_mask(a, b, dims=_NN, *, mask_left, parts):
    if mask_left:
        return sum(_dot(a, piece, dims) for piece in _split_bf16(b, parts))
    return sum(_dot(piece, b, dims) for piece in _split_bf16(a, parts))


def _sigmoid(x):
    return 1.0 / (1.0 + jnp.exp(-x))


def _silu(x):
    return x * _sigmoid(x)


def _dsilu(x):
    s = _sigmoid(x)
    return s * (1.0 + x * (1.0 - s))


_GK = math.sqrt(2.0 / math.pi)
_GC = 0.044715


def _gelu(x):
    return 0.5 * x * (1.0 + jnp.tanh(_GK * (x + _GC * x * x * x)))


def _dgelu(x):
    th = jnp.tanh(_GK * (x + _GC * x * x * x))
    return 0.5 * (1.0 + th) + 0.5 * x * (1.0 - th * th) * _GK * (1.0 + 3.0 * _GC * x * x)


def _log1p_pos(e):
    return jnp.where(e < 1e-2, e * (1.0 - e * (0.5 - e * (1.0 / 3.0))), jnp.log(1.0 + e))


def _softplus(x):
    return jnp.maximum(x, 0.0) + _log1p_pos(jnp.exp(-jnp.abs(x)))


def _neg_expm1(x):
    poly = -x * (1.0 + x * (0.5 + x * (1.0 / 6.0 + x * (1.0 / 24.0 + x * (1.0 / 120.0)))))
    return jnp.where(x > -0.05, poly, 1.0 - jnp.exp(x))


def _params(sem):
    return pltpu.CompilerParams(dimension_semantics=sem, vmem_limit_bytes=VMEM_LIMIT)


RESIDENT_BYTES = 8 * 1024 * 1024
STREAM_BYTES = 4 * 1024 * 1024


def _halve_to_fit(dims, bytes_per, limit):
    dims = list(dims)
    while math.prod(dims) * bytes_per > limit:
        i = max(range(len(dims)), key=lambda d: dims[d])
        assert dims[i] % 256 == 0, dims
        dims[i] //= 2
    return dims


def mm(a, b, *, name, ta=False, tb=False, a_extra=(), fa=None, o_extra=(), fo=None, a_off=0, m=None, k=None):
    n = b.shape[0] if tb else b.shape[1]
    na, no = 1 + len(a_extra), len(o_extra)
    if not ta:
        assert m is None
        m, kdim = a.shape[0], (a.shape[1] if k is None else k)
        assert a_off % kdim == 0
        (tn,) = _halve_to_fit([n], kdim * b.dtype.itemsize, RESIDENT_BYTES)
        (tm,) = _halve_to_fit([min(512, m)], max(tn, kdim) * 4, STREAM_BYTES)
        a_spec = pl.BlockSpec((tm, kdim), lambda i, j: (i, a_off // kdim))
        b_spec = pl.BlockSpec((tn, kdim), lambda i, j: (j, 0)) if tb else pl.BlockSpec((kdim, tn), lambda i, j: (0, j))
        o_spec = pl.BlockSpec((tm, tn), lambda i, j: (i, j))
        dims = _NT if tb else _NN

        def body(*refs):
            a_refs, b_ref, o_refs, out_ref = refs[:na], refs[na], refs[na + 1:na + 1 + no], refs[na + 1 + no]
            av = a_refs[0][...] if fa is None else fa(*[r[...] for r in a_refs])
            acc = _dot(av, b_ref[...], dims)
            out_ref[...] = acc if fo is None else fo(acc, *[r[...] for r in o_refs])

        grid, sem = (m // tm, n // tn), ("parallel", "parallel")
    else:
        assert k is None and not tb and fo is None and not o_extra
        kdim, m = a.shape[0], (a.shape[1] if m is None else m)
        tm, tn = _halve_to_fit([m, n], 4, RESIDENT_BYTES)
        (tk,) = _halve_to_fit([min(512, kdim)], max(tm, tn) * 4, STREAM_BYTES)
        assert a_off % tm == 0
        a_spec = pl.BlockSpec((tk, tm), lambda i, j, kk: (kk, i + a_off // tm))
        b_spec = pl.BlockSpec((tk, tn), lambda i, j, kk: (kk, j))
        o_spec = pl.BlockSpec((tm, tn), lambda i, j, kk: (i, j))

        def body(*refs):
            a_refs, b_ref, out_ref = refs[:na], refs[na], refs[na + 1]

            @pl.when(pl.program_id(2) == 0)
            def _():
                out_ref[...] = jnp.zeros_like(out_ref)

            av = a_refs[0][...] if fa is None else fa(*[r[...] for r in a_refs])
            out_ref[...] += _dot(av, b_ref[...], _TN)

        grid, sem = (m // tm, n // tn, kdim // tk), ("parallel", "parallel", "arbitrary")
    assert m % tm == 0 and n % tn == 0, (name, m, n, tm, tn)
    return pl.pallas_call(
        body, name=name, grid=grid,
        in_specs=[a_spec] * na + [b_spec] + [o_spec] * no,
        out_specs=o_spec, out_shape=jax.ShapeDtypeStruct((m, n), F32),
        compiler_params=_params(sem),
    )(a, *a_extra, b, *o_extra)


def rowk(fn, tiled, full, out_w, acc_shapes, *, rows, name):
    tt = min(ROW_TILE, rows)
    n = rows // tt
    assert rows % tt == 0
    nt, nf, no = len(tiled), len(full), len(out_w)

    def tspec(w, cb):
        return pl.BlockSpec((tt, w), lambda i: (i, cb))

    def fspec(a):
        nd = a.ndim
        return pl.BlockSpec(a.shape, lambda i: (0,) * nd)

    def body(*refs):
        ins, fulls = refs[:nt], refs[nt:nt + nf]
        outs, accs = refs[nt + nf:nt + nf + no], refs[nt + nf + no:]
        res_t, res_a = fn(*[r[...] for r in ins], *[r[...] for r in fulls])
        for r, v in zip(outs, res_t):
            r[...] = v
        if accs:
            @pl.when(pl.program_id(0) == 0)
            def _():
                for r in accs:
                    r[...] = jnp.zeros_like(r)
            for r, v in zip(accs, res_a):
                r[...] += v

    outs = pl.pallas_call(
        body, name=name, grid=(n,),
        in_specs=[tspec(w, cb) for (_, w, cb) in tiled] + [fspec(a) for a in full],
        out_specs=[tspec(w, 0) for w in out_w] + [pl.BlockSpec(s, lambda i, nd=len(s): (0,) * nd) for s in acc_shapes],
        out_shape=[jax.ShapeDtypeStruct((rows, w), F32) for w in out_w] + [jax.ShapeDtypeStruct(s, F32) for s in acc_shapes],
        compiler_params=_params(("arbitrary",)),
    )(*[a for (a, _, _) in tiled], *full)
    return outs[:no], outs[no:]


def _colsum(x):
    return jnp.sum(x, axis=0, keepdims=True)


def _rowsum(x):
    return jnp.sum(x, axis=1, keepdims=True)


def _ln_fwd_fn(resid, y, g, b):
    pre = ALPHA * resid + y
    mu = jnp.mean(pre, axis=1, keepdims=True)
    xc = pre - mu
    var = jnp.mean(xc * xc, axis=1, keepdims=True)
    return (xc * lax.rsqrt(var + LN_EPS) * g + b,), ()


def _ln_bwd_fn(resid, y, dout, g):
    pre = ALPHA * resid + y
    mu = jnp.mean(pre, axis=1, keepdims=True)
    xc = pre - mu
    var = jnp.mean(xc * xc, axis=1, keepdims=True)
    rstd = lax.rsqrt(var + LN_EPS)
    xhat = xc * rstd
    dxh = dout * g
    dpre = rstd * (dxh - jnp.mean(dxh, axis=1, keepdims=True) - xhat * jnp.mean(dxh * xhat, axis=1, keepdims=True))
    return (dpre,), (_colsum(dout * xhat), _colsum(dout))


def ln_fwd(resid, y, g, b, *, name):
    (out,), _ = rowk(_ln_fwd_fn, [(resid, D_MODEL, 0), (y, D_MODEL, 0)], [g, b], [D_MODEL], [],
                     rows=resid.shape[0], name=name)
    return out


def ln_bwd(resid, y, dout, g, *, name):
    (dpre,), (dg, db) = rowk(_ln_bwd_fn, [(resid, D_MODEL, 0), (y, D_MODEL, 0), (dout, D_MODEL, 0)], [g],
                             [D_MODEL], [(1, D_MODEL), (1, D_MODEL)], rows=resid.shape[0], name=name)
    return dpre, dg, db


def _loss_fn(y, tgt):
    e = y - tgt
    part = _colsum(_rowsum(e * e)) * (0.5 / D_MODEL)
    return (e * (1.0 / D_MODEL),), (part,)


_XA_SCALE = 1.0 / math.sqrt(XA_HEAD_DIM)


def _attn_probs(qh, kh):
    s = _dot(qh, kh, _NT) * _XA_SCALE
    e = jnp.exp(s - jnp.max(s, axis=1, keepdims=True))
    return e / _rowsum(e)


def _attn_fwd_fn(q, k, v):
    outs = []
    for hd in range(XA_HEADS):
        sl = slice(hd * XA_HEAD_DIM, (hd + 1) * XA_HEAD_DIM)
        outs.append(_dot(_attn_probs(q[:, sl], k[:, sl]), v[:, sl]))
    return (jnp.concatenate(outs, axis=1),), ()


def _attn_bwd_fn(q, do, k, v):
    dqs, dks, dvs = [], [], []
    for hd in range(XA_HEADS):
        sl = slice(hd * XA_HEAD_DIM, (hd + 1) * XA_HEAD_DIM)
        qh, kh, vh, doh = q[:, sl], k[:, sl], v[:, sl], do[:, sl]
        p = _attn_probs(qh, kh)
        dp = _dot(doh, vh, _NT)
        ds = p * (dp - _rowsum(p * dp)) * _XA_SCALE
        dqs.append(_dot(ds, kh))
        dks.append(_dot(ds, qh, _TN))
        dvs.append(_dot(p, doh, _TN))
    cat = lambda xs: jnp.concatenate(xs, axis=1)
    return (cat(dqs),), (cat(dks), cat(dvs))


def _s5_post_fwd_fn(ylin, u, dskip, gw, gb):
    yg = _gelu(ylin + dskip * u)
    return (yg * _sigmoid(_dot(yg, gw) + gb),), ()


def _s5_post_bwd_fn(ylin, u, dout, dskip, gw, gb):
    pre = ylin + dskip * u
    yg = _gelu(pre)
    sg = _sigmoid(_dot(yg, gw) + gb)
    dlin = dout * yg * sg * (1.0 - sg)
    dyg = dout * sg + _dot(dlin, gw, _NT)
    dpre = dyg * _dgelu(pre)
    return (dpre, dpre * dskip), (_colsum(dpre * u), _dot(yg, dlin, _TN), _colsum(dlin))


def _rg_gates(xc, wa, wx, ba, bx, lam):
    r = _sigmoid(_dot(xc, wa) + ba)
    i = _sigmoid(_dot(xc, wx) + bx)
    sp = _softplus(-lam)
    log_a = -RG_C * r * sp
    a = jnp.exp(log_a)
    mult = jnp.sqrt(_neg_expm1(2.0 * log_a))
    return r, i, sp, a, mult


def _rg_pre_fwd_fn(xc, wa, wx, ba, bx, lam):
    r, i, sp, a, mult = _rg_gates(xc, wa, wx, ba, bx, lam)
    return (a, mult * (i * xc)), ()


def _rg_pre_bwd_fn(xc, gsc, hprev, wa, wx, ba, bx, lam):
    r, i, sp, a, mult = _rg_gates(xc, wa, wx, ba, bx, lam)
    da = gsc * hprev
    db = gsc
    dmult = db * i * xc
    di = db * mult * xc
    dxc = db * mult * i
    dlog_a = da * a - a * a * dmult / mult
    dr = dlog_a * (-RG_C * sp)
    dsp = _colsum(dlog_a * (-RG_C * r))
    dlam = dsp * (-_sigmoid(-lam))
    dpr = dr * r * (1.0 - r)
    dpi = di * i * (1.0 - i)
    dxc = dxc + _dot(dpr, wa, _NT) + _dot(dpi, wx, _NT)
    return (dxc,), (_dot(xc, dpr, _TN), _dot(xc, dpi, _TN), _colsum(dpr), _colsum(dpi), dlam)


def _rg_out_fwd_fn(h, g):
    return (h * _gelu(g),), ()


def _rg_out_bwd_fn(h, g, dy):
    return (dy * _gelu(g), dy * h * _dgelu(g)), ()


def _shift_down(x, prev, j, rows):
    return jnp.where(rows < j, pltpu.roll(prev, j, 0), pltpu.roll(x, j, 0))


def _shift_up(x, nxt, j, rows):
    t = x.shape[0]
    return jnp.where(rows >= t - j, pltpu.roll(nxt, t - j, 0), pltpu.roll(x, t - j, 0))


def conv_fwd(src, cb, w, b, *, width, act, name):
    t = src.shape[0]
    tt = min(ROW_TILE, t)
    n = t // tt

    def body(x_ref, w_ref, b_ref, y_ref, prev_ref):
        @pl.when(pl.program_id(0) == 0)
        def _():
            prev_ref[...] = jnp.zeros_like(prev_ref)

        x = x_ref[...]
        prev = prev_ref[...]
        rows = lax.broadcasted_iota(jnp.int32, x.shape, 0)
        wv = w_ref[...]
        y = b_ref[...] + wv[3:4, :] * x
        for j in (1, 2, 3):
            y = y + wv[3 - j:4 - j, :] * _shift_down(x, prev, j, rows)
        y_ref[...] = _silu(y) if act else y
        prev_ref[...] = x

    return pl.pallas_call(
        body, name=name, grid=(n,),
        in_specs=[pl.BlockSpec((tt, width), lambda i: (i, cb)),
                  pl.BlockSpec((4, width), lambda i: (0, 0)), pl.BlockSpec((1, width), lambda i: (0, 0))],
        out_specs=pl.BlockSpec((tt, width), lambda i: (i, 0)),
        out_shape=jax.ShapeDtypeStruct((t, width), F32),
        scratch_shapes=[pltpu.VMEM((tt, width), F32)],
        compiler_params=_params(("arbitrary",)),
    )(src, w, b)


def conv_bwd(src, cb, dy, w, b, *, width, act, name):
    t = src.shape[0]
    tt = min(ROW_TILE, t)
    n = t // tt

    def body(x_ref, xp_ref, dy_ref, w_ref, b_ref, dx_ref, dw_ref, db_ref, nxt_ref):
        i = pl.program_id(0)

        @pl.when(i == 0)
        def _():
            nxt_ref[...] = jnp.zeros_like(nxt_ref)
            dw_ref[...] = jnp.zeros_like(dw_ref)
            db_ref[...] = jnp.zeros_like(db_ref)

        x = x_ref[...]
        prev = jnp.where(i == n - 1, 0.0, xp_ref[...])
        rows = lax.broadcasted_iota(jnp.int32, x.shape, 0)
        wv = w_ref[...]
        xs = [x] + [_shift_down(x, prev, j, rows) for j in (1, 2, 3)]
        dpre = dy_ref[...]
        if act:
            pre = b_ref[...] + wv[3:4, :] * xs[0]
            for j in (1, 2, 3):
                pre = pre + wv[3 - j:4 - j, :] * xs[j]
            dpre = dpre * _dsilu(pre)
        nxt = nxt_ref[...]
        dx = wv[3:4, :] * dpre
        for j in (1, 2, 3):
            dx = dx + wv[3 - j:4 - j, :] * _shift_up(dpre, nxt, j, rows)
        dx_ref[...] = dx
        dw_ref[...] += jnp.concatenate([_colsum(dpre * xs[3 - kk]) for kk in range(4)], axis=0)
        db_ref[...] += _colsum(dpre)
        nxt_ref[...] = dpre

    return pl.pallas_call(
        body, name=name, grid=(n,),
        in_specs=[pl.BlockSpec((tt, width), lambda i: (n - 1 - i, cb)),
                  pl.BlockSpec((tt, width), lambda i: (jnp.maximum(n - 2 - i, 0), cb)),
                  pl.BlockSpec((tt, width), lambda i: (n - 1 - i, 0)),
                  pl.BlockSpec((4, width), lambda i: (0, 0)), pl.BlockSpec((1, width), lambda i: (0, 0))],
        out_specs=[pl.BlockSpec((tt, width), lambda i: (n - 1 - i, 0)),
                   pl.BlockSpec((4, width), lambda i: (0, 0)), pl.BlockSpec((1, width), lambda i: (0, 0))],
        out_shape=[jax.ShapeDtypeStruct((t, width), F32), jax.ShapeDtypeStruct((4, width), F32),
                   jax.ShapeDtypeStruct((1, width), F32)],
        scratch_shapes=[pltpu.VMEM((tt, width), F32)],
        compiler_params=_params(("arbitrary",)),
    )(src, src, dy, w, b)


SCAN_CW = 512


def scan_complex(bu, lam, *, reverse, name):
    t, w2 = bu.shape
    w = w2 // 2
    tt = min(ROW_TILE, t)
    n, nb, cw = t // tt, tt // 8, min(SCAN_CW, w)

    def body(b_ref, lam_ref, o_ref, st_ref):
        @pl.when(pl.program_id(0) == 0)
        def _():
            st_ref[...] = jnp.zeros_like(st_ref)

        rows = lax.broadcasted_iota(jnp.int32, (8, cw), 0)
        for c0 in range(0, w, cw):
            re, im = pl.ds(c0, cw), pl.ds(w + c0, cw)
            ar = jnp.broadcast_to(lam_ref[:, re], (8, cw))
            ai = jnp.broadcast_to(lam_ref[:, im], (8, cw))

            def blk(i, carry):
                hr, hi = carry
                base = pl.multiple_of((nb - 1 - i if reverse else i) * 8, 8)
                tr, ti = b_ref[pl.ds(base, 8), re], b_ref[pl.ds(base, 8), im]
                outr, outi = jnp.zeros((8, cw), F32), jnp.zeros((8, cw), F32)
                for j in (range(7, -1, -1) if reverse else range(8)):
                    br = jnp.broadcast_to(tr[j:j + 1, :], (8, cw))
                    bi = jnp.broadcast_to(ti[j:j + 1, :], (8, cw))
                    hr, hi = ar * hr - ai * hi + br, ar * hi + ai * hr + bi
                    outr = jnp.where(rows == j, hr, outr)
                    outi = jnp.where(rows == j, hi, outi)
                o_ref[pl.ds(base, 8), re] = outr
                o_ref[pl.ds(base, 8), im] = outi
                return hr, hi

            hr, hi = lax.fori_loop(0, nb, blk, (st_ref[:, re], st_ref[:, im]))
            st_ref[:, re] = hr
            st_ref[:, im] = hi

    idx = (lambda i: (n - 1 - i, 0)) if reverse else (lambda i: (i, 0))
    return pl.pallas_call(
        body, name=name, grid=(n,),
        in_specs=[pl.BlockSpec((tt, w2), idx), pl.BlockSpec((1, w2), lambda i: (0, 0))],
        out_specs=pl.BlockSpec((tt, w2), idx), out_shape=jax.ShapeDtypeStruct((t, w2), F32),
        scratch_shapes=[pltpu.VMEM((8, w2), F32)],
        compiler_params=_params(("arbitrary",)),
    )(bu, lam)


def scan_real(a, b, *, reverse, name):
    t, w = b.shape
    tt = min(ROW_TILE, t)
    n, nb = t // tt, tt // 8

    def body(a_ref, b_ref, o_ref, st_ref):
        @pl.when(pl.program_id(0) == 0)
        def _():
            st_ref[...] = jnp.zeros_like(st_ref)

        rows = lax.broadcasted_iota(jnp.int32, (8, w), 0)

        def blk(i, h):
            base = pl.multiple_of((nb - 1 - i if reverse else i) * 8, 8)
            ta_, tb_ = a_ref[pl.ds(base, 8), :], b_ref[pl.ds(base, 8), :]
            out = jnp.zeros((8, w), F32)
            for j in (range(7, -1, -1) if reverse else range(8)):
                h = jnp.broadcast_to(ta_[j:j + 1, :], (8, w)) * h + jnp.broadcast_to(tb_[j:j + 1, :], (8, w))
                out = jnp.where(rows == j, h, out)
            o_ref[pl.ds(base, 8), :] = out
            return h

        st_ref[...] = lax.fori_loop(0, nb, blk, st_ref[...])

    idx = (lambda i: (n - 1 - i, 0)) if reverse else (lambda i: (i, 0))
    return pl.pallas_call(
        body, name=name, grid=(n,),
        in_specs=[pl.BlockSpec((tt, w), idx), pl.BlockSpec((tt, w), idx)],
        out_specs=pl.BlockSpec((tt, w), idx), out_shape=jax.ShapeDtypeStruct((t, w), F32),
        scratch_shapes=[pltpu.VMEM((8, w), F32)],
        compiler_params=_params(("arbitrary",)),
    )(a, b)


def s5_dlam(g, hprev, *, name):
    t, w2 = g.shape
    w = w2 // 2

    def fn(gt, ht):
        gr, gi, hr, hi = gt[:, :w], gt[:, w:], ht[:, :w], ht[:, w:]
        return (), (_colsum(gr * hr + gi * hi), _colsum(gi * hr - gr * hi))

    _, (dar, dai) = rowk(fn, [(g, w2, 0), (hprev, w2, 0)], [], [], [(1, w), (1, w)], rows=t, name=name)
    return dar, dai


def _ssd_common(dt_ref, dtT_ref, prow_ref, pcol_ref):
    q = SSD_CHUNK
    r = lax.broadcasted_iota(jnp.int32, (q, q), 0)
    c = lax.broadcasted_iota(jnp.int32, (q, q), 1)
    low = r >= c
    tril = low.astype(F32)
    triu = (r <= c).astype(F32)
    bias_r, alog_r = prow_ref[0:1, :], prow_ref[1:2, :]
    raw_c = dt_ref[...] + bias_r
    dt_c = _softplus(raw_c)
    a_r = -jnp.exp(alog_r)
    cs_c = _dot_mask(tril, dt_c * a_r, mask_left=True, parts=3)
    dt_r = _softplus(dtT_ref[...] + pcol_ref[:, 0:1])
    cs_r = _dot_mask(dt_r * (-jnp.exp(pcol_ref[:, 1:2])), triu, mask_left=False, parts=3)
    return low, tril, triu, raw_c, dt_c, a_r, cs_c, cs_r


def _ssd_gate(yraw, z, nw):
    yg = yraw * _silu(z)
    r = lax.rsqrt(jnp.mean(yg * yg, axis=1, keepdims=True) + LN_EPS)
    return yg, r


def ssd_fwd(xbc, proj, dtT, prow, pcol, nw, *, name):
    t = xbc.shape[0]
    q, p, ns = SSD_CHUNK, SSD_HEAD_DIM, SSD_STATE
    nc = t // q

    def body(xbc_ref, z_ref, dt_ref, dtT_ref, prow_ref, pcol_ref, nw_ref, y_ref, yraw_ref, sall_ref, s_ref, ybuf):
        @pl.when(pl.program_id(0) == 0)
        def _():
            s_ref[...] = jnp.zeros_like(s_ref)

        sall_ref[0] = s_ref[...]
        low, tril, triu, raw_c, dt_c, a_r, cs_c, cs_r = _ssd_common(dt_ref, dtT_ref, prow_ref, pcol_ref)
        d_r = prow_ref[2:3, :]
        bm = [xbc_ref[:, pl.ds(SSD_WIDTH + g * ns, ns)] for g in range(2)]
        cm = [xbc_ref[:, pl.ds(SSD_WIDTH + 2 * ns + g * ns, ns)] for g in range(2)]
        cb = [_dot(cm[g], bm[g], _NT) for g in range(2)]
        for h in range(SSD_HEADS):
            g = h // 4
            hs = pl.ds(h * p, p)
            csc, csr = cs_c[:, h:h + 1], cs_r[h:h + 1, :]
            lmat = jnp.exp(jnp.where(low, csc - csr, -1e30))
            xs = xbc_ref[:, hs]
            xdt = xs * dt_c[:, h:h + 1]
            sh = s_ref[hs, :]
            y = (_dot(cb[g] * lmat, xdt) + jnp.exp(csc) * _dot(cm[g], sh, _NT)
                 + xs * d_r[:, h:h + 1])
            ybuf[:, hs] = y
            cl = csc[q - 1:q, :]
            s_ref[hs, :] = jnp.exp(cl) * sh + _dot(xdt * jnp.exp(cl - csc), bm[g], _TN)
        yraw = ybuf[...]
        yraw_ref[...] = yraw
        yg, r = _ssd_gate(yraw, z_ref[...], nw_ref[...])
        y_ref[...] = yg * r * nw_ref[...]

    return pl.pallas_call(
        body, name=name, grid=(nc,),
        in_specs=[pl.BlockSpec((q, SSD_XBC), lambda i: (i, 0)),
                  pl.BlockSpec((q, SSD_WIDTH), lambda i: (i, P_Z // SSD_WIDTH)),
                  pl.BlockSpec((q, LANE), lambda i: (i, P_DT // LANE)),
                  pl.BlockSpec((SSD_HEADS, q), lambda i: (0, i)),
                  pl.BlockSpec((8, LANE), lambda i: (0, 0)), pl.BlockSpec((8, LANE), lambda i: (0, 0)),
                  pl.BlockSpec((1, SSD_WIDTH), lambda i: (0, 0))],
        out_specs=[pl.BlockSpec((q, SSD_WIDTH), lambda i: (i, 0)), pl.BlockSpec((q, SSD_WIDTH), lambda i: (i, 0)),
                   pl.BlockSpec((1, SSD_WIDTH, ns), lambda i: (i, 0, 0))],
        out_shape=[jax.ShapeDtypeStruct((t, SSD_WIDTH), F32), jax.ShapeDtypeStruct((t, SSD_WIDTH), F32),
                   jax.ShapeDtypeStruct((nc, SSD_WIDTH, ns), F32)],
        scratch_shapes=[pltpu.VMEM((SSD_WIDTH, ns), F32), pltpu.VMEM((q, SSD_WIDTH), F32)],
        compiler_params=_params(("arbitrary",)),
    )(xbc, proj, proj, dtT, prow, pcol, nw)


def ssd_bwd(xbc, proj, dtT, prow, pcol, nw, yraw, sall, dout, *, name):
    t = xbc.shape[0]
    q, p, ns = SSD_CHUNK, SSD_HEAD_DIM, SSD_STATE
    nc = t // q

    def body(xbc_ref, z_ref, dt_ref, dtT_ref, prow_ref, pcol_ref, nw_ref, yraw_ref, sall_ref, dout_ref,
             dxbc_ref, dz_ref, ddt_ref, dprm_ref, dnw_ref, ds_ref, dyb):
        @pl.when(pl.program_id(0) == 0)
        def _():
            ds_ref[...] = jnp.zeros_like(ds_ref)
            dprm_ref[...] = jnp.zeros_like(dprm_ref)
            dnw_ref[...] = jnp.zeros_like(dnw_ref)

        yraw, z, nwv, dout = yraw_ref[...], z_ref[...], nw_ref[...], dout_ref[...]
        yg, r = _ssd_gate(yraw, z, nwv)
        dnw_ref[...] += _colsum(dout * yg * r)
        dyn = dout * nwv
        dyg = r * dyn - yg * (r * r * r) * jnp.mean(dyn * yg, axis=1, keepdims=True)
        dyb[...] = dyg * _silu(z)
        dz_ref[...] = dyg * yraw * _dsilu(z)

        low, tril, triu, raw_c, dt_c, a_r, cs_c, cs_r = _ssd_common(dt_ref, dtT_ref, prow_ref, pcol_ref)
        d_r = prow_ref[2:3, :]
        lane = lax.broadcasted_iota(jnp.int32, (1, LANE), 1)
        ones = jnp.ones((q, LANE), F32)
        last = (lax.broadcasted_iota(jnp.int32, (q, 1), 0) == q - 1).astype(F32)
        bm = [xbc_ref[:, pl.ds(SSD_WIDTH + g * ns, ns)] for g in range(2)]
        cm = [xbc_ref[:, pl.ds(SSD_WIDTH + 2 * ns + g * ns, ns)] for g in range(2)]
        cb = [_dot(cm[g], bm[g], _NT) for g in range(2)]
        dbm = [jnp.zeros((q, ns), F32) for _ in range(2)]
        dcm = [jnp.zeros((q, ns), F32) for _ in range(2)]
        dcs_all = jnp.zeros((q, LANE), F32)
        ddt_all = jnp.zeros((q, LANE), F32)
        dd_all = jnp.zeros((1, LANE), F32)
        for h in range(SSD_HEADS):
            g = h // 4
            hs = pl.ds(h * p, p)
            onehot = (lane == h).astype(F32)
            csc, csr = cs_c[:, h:h + 1], cs_r[h:h + 1, :]
            lmat = jnp.exp(jnp.where(low, csc - csr, -1e30))
            xs = xbc_ref[:, hs]
            dth = dt_c[:, h:h + 1]
            xdt = xs * dth
            sh = sall_ref[0, hs, :]
            dy = dyb[:, hs]
            ecs = jnp.exp(csc)
            cl = csc[q - 1:q, :]
            ecl = jnp.exp(cl)
            wdec = jnp.exp(cl - csc)
            wmat = cb[g] * lmat
            dwm = _dot(dy, xdt, _NT)
            dx = _dot(wmat, dy, _TN)
            emat = dwm * wmat
            dmm = dwm * lmat
            dcm[g] = dcm[g] + _dot(dmm, bm[g])
            dbm[g] = dbm[g] + _dot(dmm, cm[g], _TN)
            dcs = _rowsum(emat) - _dot_mask(emat, ones, _TN, mask_left=False, parts=2)[:, 0:1]
            zmat = _dot(cm[g], sh, _NT)
            dzm = ecs * dy
            dcm[g] = dcm[g] + _dot(dzm, sh)
            dsp = _dot(dzm, cm[g], _TN)
            dcs = dcs + _rowsum(dzm * zmat)
            dsn = ds_ref[hs, :]
            dsp = dsp + ecl * dsn
            dcl = _colsum(_rowsum(dsn * sh)) * ecl
            xw = xdt * wdec
            dxw = _dot(bm[g], dsn, _NT)
            dbm[g] = dbm[g] + _dot(xw, dsn)
            dx = dx + wdec * dxw
            tw = _rowsum(dxw * xdt) * wdec
            dcl = dcl + _colsum(tw)
            dcs = dcs - tw + last * dcl
            ds_ref[hs, :] = dsp
            dxbc_ref[:, hs] = dx * dth + dy * d_r[:, h:h + 1]
            dcs_all = dcs_all + dcs * onehot
            ddt_all = ddt_all + _rowsum(dx * xs) * onehot
            dd_all = dd_all + _colsum(_rowsum(dy * xs)) * onehot
        for g in range(2):
            dxbc_ref[:, pl.ds(SSD_WIDTH + g * ns, ns)] = dbm[g]
            dxbc_ref[:, pl.ds(SSD_WIDTH + 2 * ns + g * ns, ns)] = dcm[g]
        dadt = _dot_mask(triu, dcs_all, mask_left=True, parts=2)
        ddt = ddt_all + dadt * a_r
        draw = ddt * _sigmoid(raw_c)
        ddt_ref[...] = draw
        zero = jnp.zeros((5, LANE), F32)
        dprm_ref[...] += jnp.concatenate([_colsum(draw), _colsum(dadt * dt_c) * a_r, dd_all, zero], axis=0)

    rev = lambda cbk: (lambda i: (nc - 1 - i, cbk))
    return pl.pallas_call(
        body, name=name, grid=(nc,),
        in_specs=[pl.BlockSpec((q, SSD_XBC), rev(0)),
                  pl.BlockSpec((q, SSD_WIDTH), rev(P_Z // SSD_WIDTH)),
                  pl.BlockSpec((q, LANE), rev(P_DT // LANE)),
                  pl.BlockSpec((SSD_HEADS, q), lambda i: (0, nc - 1 - i)),
                  pl.BlockSpec((8, LANE), lambda i: (0, 0)), pl.BlockSpec((8, LANE), lambda i: (0, 0)),
                  pl.BlockSpec((1, SSD_WIDTH), lambda i: (0, 0)),
                  pl.BlockSpec((q, SSD_WIDTH), rev(0)),
                  pl.BlockSpec((1, SSD_WIDTH, ns), lambda i: (nc - 1 - i, 0, 0)),
                  pl.BlockSpec((q, SSD_WIDTH), rev(0))],
        out_specs=[pl.BlockSpec((q, SSD_XBC), rev(0)), pl.BlockSpec((q, SSD_WIDTH), rev(0)),
                   pl.BlockSpec((q, LANE), rev(0)),
                   pl.BlockSpec((8, LANE), lambda i: (0, 0)), pl.BlockSpec((1, SSD_WIDTH), lambda i: (0, 0))],
        out_shape=[jax.ShapeDtypeStruct((t, SSD_XBC), F32), jax.ShapeDtypeStruct((t, SSD_WIDTH), F32),
                   jax.ShapeDtypeStruct((t, LANE), F32), jax.ShapeDtypeStruct((8, LANE), F32),
                   jax.ShapeDtypeStruct((1, SSD_WIDTH), F32)],
        scratch_shapes=[pltpu.VMEM((SSD_WIDTH, ns), F32), pltpu.VMEM((q, SSD_WIDTH), F32)],
        compiler_params=_params(("arbitrary",)),
    )(xbc, proj, proj, dtT, prow, pcol, nw, yraw, sall, dout)


def _me():
    return lax.axis_index("x"), lax.axis_index("y"), lax.axis_index("c")


_ANY = pl.BlockSpec(memory_space=pl.ANY)
_MESH = pl.DeviceIdType.MESH


def all_gather(block, *, name):
    def body(src, dst, send_sems, recv_sems, local_sem):
        x, y, c = _me()
        me, sibling = (x, y, c), (x, y, 1 - c)
        chips = [(1 - x, y), (x, 1 - y), (1 - x, 1 - y)]

        def slot(px, py, pc):
            return dst.at[4 * px + 2 * py + pc]

        def copy(kk, blk, to, from_src=False):
            return pltpu.make_async_remote_copy(
                src_ref=src if from_src else slot(*blk), dst_ref=slot(*blk),
                send_sem=send_sems.at[kk], recv_sem=recv_sems.at[kk], device_id=to, device_id_type=_MESH)

        mine = pltpu.make_async_copy(src, slot(*me), local_sem)
        mine.start()
        first = [copy(0, me, sibling, True)] + [copy(1 + j, me, (*chip, c), True) for j, chip in enumerate(chips)]
        for cp in first:
            cp.start()
        passed = [copy(4 + j, (*chip, c), sibling) for j, chip in enumerate(chips)]
        for j, chip in enumerate(chips):
            copy(1 + j, (*chip, c), me).wait_recv()
            passed[j].start()
        copy(0, sibling, me).wait_recv()
        for j, chip in enumerate(chips):
            copy(4 + j, (*chip, 1 - c), me).wait_recv()
        for cp in first + passed:
            cp.wait_send()
        mine.wait()

    return pl.pallas_call(
        body, name=name, in_specs=[_ANY], out_specs=_ANY,
        out_shape=jax.ShapeDtypeStruct((N_DEV,) + block.shape, block.dtype),
        scratch_shapes=[pltpu.SemaphoreType.DMA((7,)), pltpu.SemaphoreType.DMA((7,)), pltpu.SemaphoreType.DMA(())],
    )(block)


RS_PIECES = 4


def rs_sibling_exchange(halves, *, name):
    _, nq, r, l = halves.shape
    rows = r // RS_PIECES
    assert r % RS_PIECES == 0 and rows % 16 == 0

    def body(src, dst, send_sems, recv_sems):
        x, y, c = _me()
        copies = []
        for q in range(nq):
            for i in range(RS_PIECES):
                kk = q * RS_PIECES + i
                cp = pltpu.make_async_remote_copy(
                    src_ref=src.at[1 - c, q, pl.ds(i * rows, rows)], dst_ref=dst.at[q, pl.ds(i * rows, rows)],
                    send_sem=send_sems.at[kk], recv_sem=recv_sems.at[kk], device_id=(x, y, 1 - c), device_id_type=_MESH)
                cp.start()
                copies.append(cp)
        for cp in copies:
            cp.wait()

    n_copies = nq * RS_PIECES
    return pl.pallas_call(
        body, name=name, in_specs=[_ANY], out_specs=_ANY,
        out_shape=jax.ShapeDtypeStruct((nq, r, l), halves.dtype),
        scratch_shapes=[pltpu.SemaphoreType.DMA((n_copies,)), pltpu.SemaphoreType.DMA((n_copies,))],
    )(halves)


def pair_sum_bf16(halves, theirs, *, name, tt=1024):
    _, nq, r, _ = halves.shape
    parity = lax.axis_index("c").astype(jnp.int32).reshape(1)

    def body(c_ref, own_ref, sib_ref, o_ref):
        o_ref[...] = (own_ref[...] + sib_ref[...]).astype(BF16)

    return pl.pallas_call(
        body, name=name,
        grid_spec=pltpu.PrefetchScalarGridSpec(
            num_scalar_prefetch=1, grid=(nq, r // tt),
            in_specs=[pl.BlockSpec((None, None, tt, LANE), lambda q, i, c: (c[0], q, i, 0)),
                      pl.BlockSpec((None, tt, LANE), lambda q, i, c: (q, i, 0))],
            out_specs=pl.BlockSpec((None, tt, LANE), lambda q, i, c: (q, i, 0))),
        out_shape=jax.ShapeDtypeStruct((nq, r, LANE), BF16),
        compiler_params=_params(("parallel", "parallel")),
    )(parity, halves, theirs)


def rs_chip_exchange(part, *, name):
    def body(src, dst, send_sems, recv_sems, local_sem):
        x, y, c = _me()
        q_me = 2 * x + y
        local = pltpu.make_async_copy(src.at[q_me], dst.at[q_me], local_sem)
        local.start()
        copies = []
        for j, (px, py) in enumerate([(1 - x, y), (x, 1 - y), (1 - x, 1 - y)]):
            cp = pltpu.make_async_remote_copy(src_ref=src.at[2 * px + py], dst_ref=dst.at[q_me], send_sem=send_sems.at[j],
                                              recv_sem=recv_sems.at[j], device_id=(px, py, c), device_id_type=_MESH)
            cp.start()
            copies.append(cp)
        for cp in copies:
            cp.wait()
        local.wait()

    return pl.pallas_call(
        body, name=name, in_specs=[_ANY], out_specs=_ANY,
        out_shape=jax.ShapeDtypeStruct(part.shape, part.dtype),
        scratch_shapes=[pltpu.SemaphoreType.DMA((3,)), pltpu.SemaphoreType.DMA((3,)), pltpu.SemaphoreType.DMA(())],
    )(part)


def adamw(slabs, w, m, v, *, name, tt):
    ns, r = slabs.shape[0], w.shape[0]
    tt = min(tt, r)
    assert r % tt == 0

    def body(s_ref, w_ref, m_ref, v_ref, g_ref, d_ref, nm_ref, nv_ref):
        g = s_ref[0].astype(F32)
        for kdev in range(1, ns):
            g = g + s_ref[kdev].astype(F32)
        wv = w_ref[...]
        nm = ADAM_B1 * m_ref[...] + (1.0 - ADAM_B1) * g
        nv = ADAM_B2 * v_ref[...] + (1.0 - ADAM_B2) * (g * g)
        m_hat = nm / (1.0 - ADAM_B1 ** ADAM_STEP)
        v_hat = nv / (1.0 - ADAM_B2 ** ADAM_STEP)
        g_ref[...] = g
        d_ref[...] = -ADAM_LR * (m_hat / (jnp.sqrt(v_hat) + ADAM_EPS) + ADAM_WD * wv)
        nm_ref[...] = nm
        nv_ref[...] = nv

    spec = pl.BlockSpec((tt, LANE), lambda i: (i, 0))
    return pl.pallas_call(
        body, name=name, grid=(r // tt,),
        in_specs=[pl.BlockSpec((ns, tt, LANE), lambda i: (0, i, 0)), spec, spec, spec],
        out_specs=[spec] * 4, out_shape=[jax.ShapeDtypeStruct((r, LANE), F32)] * 4,
        compiler_params=_params(("parallel",)),
    )(slabs, w, m, v)


SHARDED = [("w_in", 1), ("w_out", 1), ("ssd_conv_w", 2), ("s5_glu_w", 1), ("rg_conv_w", 2),
           ("xa_wq", 1), ("xa_wk", 1), ("xa_wv", 1), ("xa_wo", 1), ("mlp_w1", 2), ("mlp_w2", 1)]
KEEP_F32 = ("ssd_conv_w", "rg_conv_w")
SMALL = ["ssd_conv_b", "ssd_dt_bias", "ssd_a_log", "ssd_d", "ssd_norm_w", "s5_lam_re", "s5_lam_im",
         "s5_log_step", "s5_b_re", "s5_b_im", "s5_c_re", "s5_c_im", "s5_d", "s5_glu_b", "rg_conv_b",
         "rg_wa", "rg_ba", "rg_wx", "rg_bx", "rg_lambda", "ln1_g", "ln1_b", "ln2_g", "ln2_b", "ln3_g", "ln3_b"]
WEIGHTS = ['w_in', 'w_out', 'ssd_conv_w', 'ssd_conv_b', 'ssd_dt_bias', 'ssd_a_log', 'ssd_d', 'ssd_norm_w',
           's5_lam_re', 's5_lam_im', 's5_log_step', 's5_b_re', 's5_b_im', 's5_c_re', 's5_c_im', 's5_d',
           's5_glu_w', 's5_glu_b', 'rg_conv_w', 'rg_conv_b', 'rg_wa', 'rg_ba', 'rg_wx', 'rg_bx', 'rg_lambda',
           'ln1_g', 'ln1_b', 'xa_wq', 'xa_wk', 'xa_wv', 'xa_wo', 'ln2_g', 'ln2_b', 'mlp_w1', 'mlp_w2',
           'ln3_g', 'ln3_b']


def _pack_rows(flat, mult):
    n = flat.shape[-1]
    r = -(-n // (LANE * mult)) * mult
    pad = [(0, 0)] * (flat.ndim - 1) + [(0, r * LANE - n)]
    return jnp.pad(flat, pad).reshape(flat.shape[:-1] + (r, LANE))


def _unpack(packed, shapes):
    lead = packed.shape[:-2]
    flat = packed.reshape(lead + (-1,))
    out, off = [], 0
    for s in shapes:
        n = math.prod(s)
        out.append(flat[..., off:off + n].reshape(lead + tuple(s)))
        off += n
    return out


def _to_full(gathered, axis):
    g = jnp.moveaxis(gathered, 0, axis)
    s = g.shape
    return g.reshape(s[:axis] + (s[axis] * s[axis + 1],) + s[axis + 2:])


def _to_slabs(full, axis):
    s = full.shape
    g = full.reshape(s[:axis] + (N_DEV, s[axis] // N_DEV) + s[axis + 1:])
    return jnp.moveaxis(g, axis, 0)


def _blockdiag(w):
    h, i, j = w.shape
    eye = jnp.eye(h, dtype=w.dtype)
    return (w[:, :, None, :] * eye[:, None, :, None]).reshape(h * i, h * j)


def _blockdiag_extract(m, h):
    i, j = m.shape[0] // h, m.shape[1] // h
    eye = jnp.eye(h, dtype=m.dtype)
    return (m.reshape(h, i, h, j) * eye[:, None, :, None]).sum(axis=2)


def _s5_disc(lr, li, ls, bre, bim):
    step = jnp.exp(ls)[:, None]
    er = jnp.exp(lr * step)
    ar, ai = er * jnp.cos(li * step), er * jnp.sin(li * step)
    nr, ni, den = ar - 1.0, ai, lr * lr + li * li
    qr, qi = (nr * lr + ni * li) / den, (ni * lr - nr * li) / den
    bbr = qr[..., None] * bre - qi[..., None] * bim
    bbi = qr[..., None] * bim + qi[..., None] * bre
    return ar, ai, bbr, bbi


def _row(v, width=None):
    v = v.reshape(1, -1)
    if width is not None and v.shape[1] < width:
        v = jnp.pad(v, ((0, 0), (0, width - v.shape[1])))
    return v


def _relu2(a):
    r = jnp.maximum(a, 0.0)
    return r * r


def _add_alpha(acc, d):
    return acc + ALPHA * d


def _shift_rows_down(x):
    return jnp.concatenate([jnp.zeros((1, x.shape[1]), x.dtype), x[:-1]], axis=0)


def _shift_rows_up(x):
    return jnp.concatenate([x[1:], jnp.zeros((1, x.shape[1]), x.dtype)], axis=0)


def _layer_params(full, small, l):
    p = {}
    w_in = full["w_in"][l]
    z, xbc, dt, u, xr, g = w_in[0:512], w_in[512:1536], w_in[1536:1544], w_in[1544:1800], w_in[1800:2056], w_in[2056:2312]
    p["w_inp"] = jnp.concatenate([xbc, z, u, xr, g, dt, jnp.zeros((D_INP - P_DT - 8, D_MODEL), w_in.dtype)], axis=0)
    for k_ in ("w_out", "xa_wq", "xa_wk", "xa_wv", "xa_wo", "mlp_w1", "mlp_w2", "s5_glu_w"):
        p[k_] = full[k_][l]
    p["ssd_cw"], p["ssd_cb"] = full["ssd_conv_w"][l], _row(small["ssd_conv_b"][l])
    dtb, alog, dsk = small["ssd_dt_bias"][l], small["ssd_a_log"][l], small["ssd_d"][l]
    p["prow"] = jnp.concatenate([_row(dtb, LANE), _row(alog, LANE), _row(dsk, LANE), jnp.zeros((5, LANE), F32)], axis=0)
    p["pcol"] = jnp.pad(jnp.stack([dtb, alog], axis=1), ((0, 0), (0, LANE - 2)))
    p["ssd_nw"] = _row(small["ssd_norm_w"][l])
    s5_in = (small["s5_lam_re"][l], small["s5_lam_im"][l], small["s5_log_step"][l], small["s5_b_re"][l], small["s5_b_im"][l])
    (ar, ai, bbr, bbi), p["s5_vjp"] = jax.vjp(_s5_disc, *s5_in)
    p["lam_fwd"] = jnp.concatenate([_row(ar), _row(ai)], axis=1)
    p["lam_adj"] = jnp.concatenate([_row(ar), _row(-ai)], axis=1)
    p["bcat"] = jnp.concatenate([_blockdiag(jnp.swapaxes(bbr, 1, 2)), _blockdiag(jnp.swapaxes(bbi, 1, 2))], axis=1)
    p["ccat"] = jnp.concatenate([_blockdiag(jnp.swapaxes(small["s5_c_re"][l], 1, 2)),
                                 -_blockdiag(jnp.swapaxes(small["s5_c_im"][l], 1, 2))], axis=0)
    p["s5_d"], p["s5_glu_b"] = _row(small["s5_d"][l]), _row(small["s5_glu_b"][l])
    p["rg_cw"], p["rg_cb"] = full["rg_conv_w"][l], _row(small["rg_conv_b"][l])
    p["rg_wa"], p["rg_wx"] = _blockdiag(small["rg_wa"][l]), _blockdiag(small["rg_wx"][l])
    p["rg_ba"], p["rg_bx"], p["rg_lam"] = _row(small["rg_ba"][l]), _row(small["rg_bx"][l]), _row(small["rg_lambda"][l])
    for i in (1, 2, 3):
        p[f"g{i}"], p[f"b{i}"] = _row(small[f"ln{i}_g"][l]), _row(small[f"ln{i}_b"][l])
    return p


def _layer_fwd(h0, mem, p):
    t = h0.shape[0]
    s = {"h0": h0}
    proj = mm(h0, p["w_inp"], tb=True, name="in_proj")
    dtT = proj[:, P_DT:P_DT + SSD_HEADS].T
    xbc = conv_fwd(proj, 0, p["ssd_cw"], p["ssd_cb"], width=SSD_XBC, act=True, name="ssd_conv_fwd")
    y_ssd, yraw, sall = ssd_fwd(xbc, proj, dtT, p["prow"], p["pcol"], p["ssd_nw"], name="ssd_fwd")
    bu = mm(proj, p["bcat"], a_off=P_U, k=S5_WIDTH, name="s5_bu")
    hs5 = scan_complex(bu, p["lam_fwd"], reverse=False, name="s5_scan_fwd")
    ylin = mm(hs5, p["ccat"], name="s5_ylin")
    (y_s5,), _ = rowk(_s5_post_fwd_fn, [(ylin, S5_WIDTH, 0), (proj, S5_WIDTH, P_U // S5_WIDTH)],
                      [p["s5_d"], p["s5_glu_w"], p["s5_glu_b"]], [S5_WIDTH], [], rows=t, name="s5_post_fwd")
    xc = conv_fwd(proj, P_XR // RG_WIDTH, p["rg_cw"], p["rg_cb"], width=RG_WIDTH, act=False, name="rg_conv_fwd")
    rg_full = [p["rg_wa"], p["rg_wx"], p["rg_ba"], p["rg_bx"], p["rg_lam"]]
    (a_rg, b_rg), _ = rowk(_rg_pre_fwd_fn, [(xc, RG_WIDTH, 0)], rg_full, [RG_WIDTH, RG_WIDTH], [], rows=t, name="rg_pre_fwd")
    h_rg = scan_real(a_rg, b_rg, reverse=False, name="rg_scan_fwd")
    (y_rg,), _ = rowk(_rg_out_fwd_fn, [(h_rg, RG_WIDTH, 0), (proj, RG_WIDTH, P_G // RG_WIDTH)], [], [RG_WIDTH], [],
                      rows=t, name="rg_out_fwd")
    ycat = jnp.concatenate([y_ssd, y_s5, y_rg], axis=1)
    mix = mm(ycat, p["w_out"], name="out_proj")
    h1 = ln_fwd(h0, mix, p["g1"], p["b1"], name="ln_fwd")
    q = mm(h1, p["xa_wq"], name="xa_q")
    k = mm(mem, p["xa_wk"], name="xa_kv")
    v = mm(mem, p["xa_wv"], name="xa_kv")
    (o,), _ = rowk(_attn_fwd_fn, [(q, D_MODEL, 0)], [k, v], [D_MODEL], [], rows=t, name="xa_fwd")
    att = mm(o, p["xa_wo"], name="xa_o")
    h2 = ln_fwd(h1, att, p["g2"], p["b2"], name="ln_fwd")
    a_mlp = mm(h2, p["mlp_w1"], name="mlp_up")
    m_out = mm(a_mlp, p["mlp_w2"], fa=_relu2, name="mlp_down")
    h3 = ln_fwd(h2, m_out, p["g3"], p["b3"], name="ln_fwd")
    s.update(proj=proj, dtT=dtT, xbc=xbc, yraw=yraw, sall=sall, hs5=hs5, ylin=ylin, xc=xc, a_rg=a_rg, h_rg=h_rg,
             ycat=ycat, mix=mix, h1=h1, q=q, k=k, v=v, o=o, att=att, h2=h2, a_mlp=a_mlp, m_out=m_out)
    return h3, s


def _layer_bwd(dh3, mem, p, s, l, gfull, gsmall):
    t = dh3.shape[0]
    proj = s["proj"]
    dpre3, dg3, db3 = ln_bwd(s["h2"], s["m_out"], dh3, p["g3"], name="ln_bwd")
    da = mm(dpre3, p["mlp_w2"], tb=True, o_extra=(s["a_mlp"],), fo=lambda acc, a: acc * 2.0 * jnp.maximum(a, 0.0), name="mlp_da")
    gfull["mlp_w2"][l] = mm(s["a_mlp"], dpre3, ta=True, fa=_relu2, name="mlp_dw2")
    gfull["mlp_w1"][l] = mm(s["h2"], da, ta=True, name="mlp_dw1")
    dh2 = mm(da, p["mlp_w1"], tb=True, o_extra=(dpre3,), fo=_add_alpha, name="mlp_dx")
    dpre2, dg2, db2 = ln_bwd(s["h1"], s["att"], dh2, p["g2"], name="ln_bwd")
    do = mm(dpre2, p["xa_wo"], tb=True, name="xa_do")
    gfull["xa_wo"][l] = mm(s["o"], dpre2, ta=True, name="dw_sq")
    (dq,), (dk, dv) = rowk(_attn_bwd_fn, [(s["q"], D_MODEL, 0), (do, D_MODEL, 0)], [s["k"], s["v"]], [D_MODEL],
                           [(256, D_MODEL), (256, D_MODEL)], rows=t, name="xa_bwd")
    gfull["xa_wq"][l] = mm(s["h1"], dq, ta=True, name="dw_sq")
    gfull["xa_wk"][l] = mm(mem, dk, ta=True, name="dw_kv")
    gfull["xa_wv"][l] = mm(mem, dv, ta=True, name="dw_kv")
    dh1 = mm(dq, p["xa_wq"], tb=True, o_extra=(dpre2,), fo=_add_alpha, name="dx_sq")
    dpre1, dg1, db1 = ln_bwd(s["h0"], s["mix"], dh1, p["g1"], name="ln_bwd")
    dycat = mm(dpre1, p["w_out"], tb=True, name="xa_do")
    gfull["w_out"][l] = mm(s["ycat"], dpre1, ta=True, name="dw_sq")
    (dh_rg, dg_rg), _ = rowk(_rg_out_bwd_fn, [(s["h_rg"], RG_WIDTH, 0), (proj, RG_WIDTH, P_G // RG_WIDTH), (dycat, RG_WIDTH, 3)],
                             [], [RG_WIDTH, RG_WIDTH], [], rows=t, name="rg_out_bwd")
    g_rg = scan_real(_shift_rows_up(s["a_rg"]), dh_rg, reverse=True, name="rg_scan_bwd")
    rg_full = [p["rg_wa"], p["rg_wx"], p["rg_ba"], p["rg_bx"], p["rg_lam"]]
    (dxc,), (dwa, dwx, dba, dbx, dlam) = rowk(
        _rg_pre_bwd_fn, [(s["xc"], RG_WIDTH, 0), (g_rg, RG_WIDTH, 0), (_shift_rows_down(s["h_rg"]), RG_WIDTH, 0)], rg_full,
        [RG_WIDTH], [(RG_WIDTH, RG_WIDTH), (RG_WIDTH, RG_WIDTH), (1, RG_WIDTH), (1, RG_WIDTH), (1, RG_WIDTH)],
        rows=t, name="rg_pre_bwd")
    dxr, d_rgcw, d_rgcb = conv_bwd(proj, P_XR // RG_WIDTH, dxc, p["rg_cw"], p["rg_cb"], width=RG_WIDTH, act=False, name="rg_conv_bwd")
    (dylin, du_a), (d_s5d, d_gluw, d_glub) = rowk(
        _s5_post_bwd_fn, [(s["ylin"], S5_WIDTH, 0), (proj, S5_WIDTH, P_U // S5_WIDTH), (dycat, S5_WIDTH, 2)],
        [p["s5_d"], p["s5_glu_w"], p["s5_glu_b"]], [S5_WIDTH, S5_WIDTH],
        [(1, S5_WIDTH), (S5_WIDTH, S5_WIDTH), (1, S5_WIDTH)], rows=t, name="s5_post_bwd")
    dhs = mm(dylin, p["ccat"], tb=True, name="s5_dh")
    dccat = mm(s["hs5"], dylin, ta=True, name="s5_dc")
    gs5 = scan_complex(dhs, p["lam_adj"], reverse=True, name="s5_scan_bwd")
    dar, dai = s5_dlam(gs5, _shift_rows_down(s["hs5"]), name="s5_dlam")
    du = mm(gs5, p["bcat"], tb=True, o_extra=(du_a,), fo=lambda acc, d: acc + d, name="s5_du")
    dbcat = mm(proj, gs5, ta=True, a_off=P_U, m=S5_WIDTH, name="s5_db")
    dxbc_act, dz, ddt, dprm, dnw = ssd_bwd(s["xbc"], proj, s["dtT"], p["prow"], p["pcol"], p["ssd_nw"], s["yraw"],
                                          s["sall"], dycat, name="ssd_bwd")
    dxbc, d_scw, d_scb = conv_bwd(proj, 0, dxbc_act, p["ssd_cw"], p["ssd_cb"], width=SSD_XBC, act=True, name="ssd_conv_bwd")
    dproj = jnp.concatenate([dxbc, dz, du, dxr, dg_rg, ddt, jnp.zeros((t, D_INP - P_DT - LANE), F32)], axis=1)
    dh0 = mm(dproj, p["w_inp"], o_extra=(dpre1,), fo=_add_alpha, name="in_proj_dx")
    dwp = mm(dproj, s["h0"], ta=True, name="in_proj_dw")
    gfull["w_in"][l] = jnp.concatenate([dwp[P_Z:P_Z + 512], dwp[P_XBC:P_XBC + 1024], dwp[P_DT:P_DT + 8],
                                        dwp[P_U:P_U + 256], dwp[P_XR:P_XR + 256], dwp[P_G:P_G + 256]], axis=0)
    gfull["ssd_conv_w"][l], gfull["rg_conv_w"][l], gfull["s5_glu_w"][l] = d_scw, d_rgcw, d_gluw
    ng, ns = S5_GROUPS, S5_STATE
    dbbr = jnp.swapaxes(_blockdiag_extract(dbcat[:, :S5_NSTATE], ng), 1, 2)
    dbbi = jnp.swapaxes(_blockdiag_extract(dbcat[:, S5_NSTATE:], ng), 1, 2)
    d_lr, d_li, d_ls, d_bre, d_bim = p["s5_vjp"]((dar.reshape(ng, ns), dai.reshape(ng, ns), dbbr, dbbi))
    gsmall["s5_lam_re"][l], gsmall["s5_lam_im"][l], gsmall["s5_log_step"][l] = d_lr, d_li, d_ls
    gsmall["s5_b_re"][l], gsmall["s5_b_im"][l] = d_bre, d_bim
    gsmall["s5_c_re"][l] = jnp.swapaxes(_blockdiag_extract(dccat[:S5_NSTATE], ng), 1, 2)
    gsmall["s5_c_im"][l] = -jnp.swapaxes(_blockdiag_extract(dccat[S5_NSTATE:], ng), 1, 2)
    gsmall["s5_d"][l], gsmall["s5_glu_b"][l] = d_s5d[0], d_glub[0]
    gsmall["ssd_conv_b"][l], gsmall["rg_conv_b"][l] = d_scb[0], d_rgcb[0]
    gsmall["ssd_dt_bias"][l], gsmall["ssd_a_log"][l], gsmall["ssd_d"][l] = dprm[0, :8], dprm[1, :8], dprm[2, :8]
    gsmall["ssd_norm_w"][l] = dnw[0]
    gsmall["rg_wa"][l], gsmall["rg_wx"][l] = _blockdiag_extract(dwa, RG_BLOCKS), _blockdiag_extract(dwx, RG_BLOCKS)
    gsmall["rg_ba"][l], gsmall["rg_bx"][l] = dba.reshape(RG_BLOCKS, RG_BLOCK_DIM), dbx.reshape(RG_BLOCKS, RG_BLOCK_DIM)
    gsmall["rg_lambda"][l] = dlam[0]
    for i, (dg, db) in zip((1, 2, 3), ((dg1, db1), (dg2, db2), (dg3, db3))):
        gsmall[f"ln{i}_g"][l], gsmall[f"ln{i}_b"][l] = dg[0], db[0]
    return dh0


def _step(a):
    a = dict(a)
    for pre in ("", "m_", "v_"):
        a[pre + "w_in"] = jnp.swapaxes(a[pre + "w_in"], 1, 2)
    h = a["x"][0]
    mem = a["mem"][0]
    t = h.shape[0]
    pieces = []
    for name, _ in SHARDED:
        w = a[name]
        if name in KEEP_F32:
            pieces.append(lax.bitcast_convert_type(w, BF16).reshape(-1))
        else:
            pieces.append(w.astype(BF16).reshape(-1))
    gathered = all_gather(_pack_rows(jnp.concatenate(pieces), 16), name="ag_weights")
    shapes = [a[name].shape + ((2,) if name in KEEP_F32 else ()) for name, _ in SHARDED]
    full = {}
    for (name, axis), g in zip(SHARDED, _unpack(gathered, shapes)):
        if name in KEEP_F32:
            g = lax.bitcast_convert_type(g, F32)
        full[name] = _to_full(g, axis)
    small = {name: a[name] for name in SMALL}
    params, saved = [], []
    for l in range(DEPTH):
        p = _layer_params(full, small, l)
        h, s = _layer_fwd(h, mem, p)
        params.append(p)
        saved.append(s)
    (dh,), (loss_part,) = rowk(_loss_fn, [(h, D_MODEL, 0), (a["loss_target"][0], D_MODEL, 0)], [], [D_MODEL], [(1, 1)],
                               rows=t, name="loss_head")
    loss = lax.psum(loss_part[0, 0], ("x", "y", "c"))
    gfull = {name: [None] * DEPTH for name, _ in SHARDED}
    gsmall = {name: [None] * DEPTH for name in SMALL}
    for l in reversed(range(DEPTH)):
        dh = _layer_bwd(dh, mem, params[l], saved[l], l, gfull, gsmall)
    grad_x = dh[None]
    def by_parity(name, axis):
        return jnp.swapaxes(_to_slabs(jnp.stack(gfull[name]), axis).reshape(4, 2, -1), 0, 1)

    halves = _pack_rows(jnp.concatenate([by_parity(name, axis) for name, axis in SHARDED], axis=2), 1024)
    theirs = rs_sibling_exchange(halves, name="rs_sibling")
    slabs = rs_chip_exchange(pair_sum_bf16(halves, theirs, name="rs_pair_sum"), name="rs_chips")
    pk = lambda pre: _pack_rows(jnp.concatenate([a[pre + name].reshape(-1) for name, _ in SHARDED]), 1024)
    big = adamw(slabs, pk(""), pk("m_"), pk("v_"), name="adamw_sharded", tt=1024)
    gs = _pack_rows(jnp.concatenate([jnp.stack(gsmall[name]).reshape(-1) for name in SMALL]), 8)
    gs = all_gather(gs, name="ag_small_grads")
    pks = lambda pre: _pack_rows(jnp.concatenate([a[pre + name].reshape(-1) for name in SMALL]), 8)
    sm = adamw(gs, pks(""), pks("m_"), pks("v_"), name="adamw_replicated", tt=gs.shape[1])
    out = {}
    for kind, bg, sg in zip(("grad_", "delta_", "new_m_", "new_v_"), big, sm):
        for (name, _), arr in zip(SHARDED, _unpack(bg, [a[name].shape for name, _ in SHARDED])):
            out[kind + name] = jnp.swapaxes(arr, 1, 2) if name == "w_in" else arr
        for name, arr in zip(SMALL, _unpack(sg, [a[name].shape for name in SMALL])):
            out[kind + name] = arr
    return (loss, grad_x) + tuple(out[kind + name] for kind in ("grad_", "delta_", "new_m_", "new_v_") for name in WEIGHTS)


def kernel(x, mem, w_in, w_out, ssd_conv_w, ssd_conv_b, ssd_dt_bias, ssd_a_log, ssd_d, ssd_norm_w, s5_lam_re, s5_lam_im, s5_log_step, s5_b_re, s5_b_im, s5_c_re, s5_c_im, s5_d, s5_glu_w, s5_glu_b, rg_conv_w, rg_conv_b, rg_wa, rg_ba, rg_wx, rg_bx, rg_lambda, ln1_g, ln1_b, xa_wq, xa_wk, xa_wv, xa_wo, ln2_g, ln2_b, mlp_w1, mlp_w2, ln3_g, ln3_b, loss_target, m_w_in, m_w_out, m_ssd_conv_w, m_ssd_conv_b, m_ssd_dt_bias, m_ssd_a_log, m_ssd_d, m_ssd_norm_w, m_s5_lam_re, m_s5_lam_im, m_s5_log_step, m_s5_b_re, m_s5_b_im, m_s5_c_re, m_s5_c_im, m_s5_d, m_s5_glu_w, m_s5_glu_b, m_rg_conv_w, m_rg_conv_b, m_rg_wa, m_rg_ba, m_rg_wx, m_rg_bx, m_rg_lambda, m_ln1_g, m_ln1_b, m_xa_wq, m_xa_wk, m_xa_wv, m_xa_wo, m_ln2_g, m_ln2_b, m_mlp_w1, m_mlp_w2, m_ln3_g, m_ln3_b, v_w_in, v_w_out, v_ssd_conv_w, v_ssd_conv_b, v_ssd_dt_bias, v_ssd_a_log, v_ssd_d, v_ssd_norm_w, v_s5_lam_re, v_s5_lam_im, v_s5_log_step, v_s5_b_re, v_s5_b_im, v_s5_c_re, v_s5_c_im, v_s5_d, v_s5_glu_w, v_s5_glu_b, v_rg_conv_w, v_rg_conv_b, v_rg_wa, v_rg_ba, v_rg_wx, v_rg_bx, v_rg_lambda, v_ln1_g, v_ln1_b, v_xa_wq, v_xa_wk, v_xa_wv, v_xa_wo, v_ln2_g, v_ln2_b, v_mlp_w1, v_mlp_w2, v_ln3_g, v_ln3_b):
    return _step(dict(locals()))
```

```python
import math

import jax
import jax.numpy as jnp
from jax import lax
from jax.experimental import pallas as pl
from jax.experimental.pallas import tpu as pltpu

F32 = jnp.float32
BF16 = jnp.bfloat16

N_DEV = 8
D_MODEL = 1024
DEPTH = 2
SSD_WIDTH = 512
SSD_HEADS = 8
SSD_HEAD_DIM = 64
SSD_STATE = 128
SSD_CHUNK = 128
SSD_XBC = 1024
S5_WIDTH = 256
S5_GROUPS = 16
S5_GROUP_CH = 16
S5_STATE = 64
S5_NSTATE = S5_GROUPS * S5_STATE
RG_WIDTH = 256
RG_BLOCKS = 4
RG_BLOCK_DIM = 64
RG_C = 8.0
XA_HEADS = 4
XA_HEAD_DIM = 256
ALPHA = (2.0 * DEPTH) ** 0.25
LN_EPS = 1e-5
ADAM_LR, ADAM_B1, ADAM_B2, ADAM_EPS, ADAM_WD, ADAM_STEP = 0.001, 0.9, 0.999, 1e-08, 0.01, 10

P_XBC, P_Z, P_U, P_XR, P_G, P_DT = 0, 1024, 1536, 1792, 2048, 2304
D_INP = 2560
LANE = 128
VMEM_LIMIT = 56 * 1024 * 1024
ROW_TILE = 512

_NN = ((1,), (0,))
_NT = ((1,), (1,))
_TN = ((0,), (0,))


def _dot(a, b, dims=_NN):
    return lax.dot_general(a.astype(BF16), b.astype(BF16), (dims, ((), ())), preferred_element_type=F32)


def _split_bf16(x, parts):
    out, rem = [], x
    for _ in range(parts):
        piece = rem.astype(BF16)
        out.append(piece)
        rem = rem - piece.astype(F32)
    return out


def _dot_mask(a, b, dims=_NN, *, mask_left, parts):
    if mask_left:
        return sum(_dot(a, piece, dims) for piece in _split_bf16(b, parts))
    return sum(_dot(piece, b, dims) for piece in _split_bf16(a, parts))


def _sigmoid(x):
    return 1.0 / (1.0 + jnp.exp(-x))


def _silu(x):
    return x * _sigmoid(x)


def _dsilu(x):
    s = _sigmoid(x)
    return s * (1.0 + x * (1.0 - s))


_GK = math.sqrt(2.0 / math.pi)
_GC = 0.044715


def _gelu(x):
    return 0.5 * x * (1.0 + jnp.tanh(_GK * (x + _GC * x * x * x)))


def _dgelu(x):
    th = jnp.tanh(_GK * (x + _GC * x * x * x))
    return 0.5 * (1.0 + th) + 0.5 * x * (1.0 - th * th) * _GK * (1.0 + 3.0 * _GC * x * x)


def _log1p_pos(e):
    return jnp.where(e < 1e-2, e * (1.0 - e * (0.5 - e * (1.0 / 3.0))), jnp.log(1.0 + e))


def _softplus(x):
    return jnp.maximum(x, 0.0) + _log1p_pos(jnp.exp(-jnp.abs(x)))


def _neg_expm1(x):
    poly = -x * (1.0 + x * (0.5 + x * (1.0 / 6.0 + x * (1.0 / 24.0 + x * (1.0 / 120.0)))))
    return jnp.where(x > -0.05, poly, 1.0 - jnp.exp(x))


def _params(sem):
    return pltpu.CompilerParams(dimension_semantics=sem, vmem_limit_bytes=VMEM_LIMIT)


RESIDENT_BYTES = 8 * 1024 * 1024
STREAM_BYTES = 4 * 1024 * 1024


def _halve_to_fit(dims, bytes_per, limit):
    dims = list(dims)
    while math.prod(dims) * bytes_per > limit:
        i = max(range(len(dims)), key=lambda d: dims[d])
        assert dims[i] % 256 == 0, dims
        dims[i] //= 2
    return dims


def mm(a, b, *, name, ta=False, tb=False, a_extra=(), fa=None, o_extra=(), fo=None, a_off=0, m=None, k=None):
    n = b.shape[0] if tb else b.shape[1]
    na, no = 1 + len(a_extra), len(o_extra)
    if not ta:
        assert m is None
        m, kdim = a.shape[0], (a.shape[1] if k is None else k)
        assert a_off % kdim == 0
        (tn,) = _halve_to_fit([n], kdim * b.dtype.itemsize, RESIDENT_BYTES)
        (tm,) = _halve_to_fit([min(512, m)], max(tn, kdim) * 4, STREAM_BYTES)
        a_spec = pl.BlockSpec((tm, kdim), lambda i, j: (i, a_off // kdim))
        b_spec = pl.BlockSpec((tn, kdim), lambda i, j: (j, 0)) if tb else pl.BlockSpec((kdim, tn), lambda i, j: (0, j))
        o_spec = pl.BlockSpec((tm, tn), lambda i, j: (i, j))
        dims = _NT if tb else _NN

        def body(*refs):
            a_refs, b_ref, o_refs, out_ref = refs[:na], refs[na], refs[na + 1:na + 1 + no], refs[na + 1 + no]
            av = a_refs[0][...] if fa is None else fa(*[r[...] for r in a_refs])
            acc = _dot(av, b_ref[...], dims)
            out_ref[...] = acc if fo is None else fo(acc, *[r[...] for r in o_refs])

        grid, sem = (m // tm, n // tn), ("parallel", "parallel")
    else:
        assert k is None and not tb and fo is None and not o_extra
        kdim, m = a.shape[0], (a.shape[1] if m is None else m)
        tm, tn = _halve_to_fit([m, n], 4, RESIDENT_BYTES)
        (tk,) = _halve_to_fit([min(512, kdim)], max(tm, tn) * 4, STREAM_BYTES)
        assert a_off % tm == 0
        a_spec = pl.BlockSpec((tk, tm), lambda i, j, kk: (kk, i + a_off // tm))
        b_spec = pl.BlockSpec((tk, tn), lambda i, j, kk: (kk, j))
        o_spec = pl.BlockSpec((tm, tn), lambda i, j, kk: (i, j))

        def body(*refs):
            a_refs, b_ref, out_ref = refs[:na], refs[na], refs[na + 1]

            @pl.when(pl.program_id(2) == 0)
            def _():
                out_ref[...] = jnp.zeros_like(out_ref)

            av = a_refs[0][...] if fa is None else fa(*[r[...] for r in a_refs])
            out_ref[...] += _dot(av, b_ref[...], _TN)

        grid, sem = (m // tm, n // tn, kdim // tk), ("parallel", "parallel", "arbitrary")
    assert m % tm == 0 and n % tn == 0, (name, m, n, tm, tn)
    return pl.pallas_call(
        body, name=name, grid=grid,
        in_specs=[a_spec] * na + [b_spec] + [o_spec] * no,
        out_specs=o_spec, out_shape=jax.ShapeDtypeStruct((m, n), F32),
        compiler_params=_params(sem),
    )(a, *a_extra, b, *o_extra)


def rowk(fn, tiled, full, out_w, acc_shapes, *, rows, name):
    tt = min(ROW_TILE, rows)
    n = rows // tt
    assert rows % tt == 0
    nt, nf, no = len(tiled), len(full), len(out_w)

    def tspec(w, cb):
        return pl.BlockSpec((tt, w), lambda i: (i, cb))

    def fspec(a):
        nd = a.ndim
        return pl.BlockSpec(a.shape, lambda i: (0,) * nd)

    def body(*refs):
        ins, fulls = refs[:nt], refs[nt:nt + nf]
        outs, accs = refs[nt + nf:nt + nf + no], refs[nt + nf + no:]
        res_t, res_a = fn(*[r[...] for r in ins], *[r[...] for r in fulls])
        for r, v in zip(outs, res_t):
            r[...] = v
        if accs:
            @pl.when(pl.program_id(0) == 0)
            def _():
                for r in accs:
                    r[...] = jnp.zeros_like(r)
            for r, v in zip(accs, res_a):
                r[...] += v

    outs = pl.pallas_call(
        body, name=name, grid=(n,),
        in_specs=[tspec(w, cb) for (_, w, cb) in tiled] + [fspec(a) for a in full],
        out_specs=[tspec(w, 0) for w in out_w] + [pl.BlockSpec(s, lambda i, nd=len(s): (0,) * nd) for s in acc_shapes],
        out_shape=[jax.ShapeDtypeStruct((rows, w), F32) for w in out_w] + [jax.ShapeDtypeStruct(s, F32) for s in acc_shapes],
        compiler_params=_params(("arbitrary",)),
    )(*[a for (a, _, _) in tiled], *full)
    return outs[:no], outs[no:]


def _colsum(x):
    return jnp.sum(x, axis=0, keepdims=True)


def _rowsum(x):
    return jnp.sum(x, axis=1, keepdims=True)


def _ln_fwd_fn(resid, y, g, b):
    pre = ALPHA * resid + y
    mu = jnp.mean(pre, axis=1, keepdims=True)
    xc = pre - mu
    var = jnp.mean(xc * xc, axis=1, keepdims=True)
    return (xc * lax.rsqrt(var + LN_EPS) * g + b,), ()


def _ln_bwd_fn(resid, y, dout, g):
    pre = ALPHA * resid + y
    mu = jnp.mean(pre, axis=1, keepdims=True)
    xc = pre - mu
    var = jnp.mean(xc * xc, axis=1, keepdims=True)
    rstd = lax.rsqrt(var + LN_EPS)
    xhat = xc * rstd
    dxh = dout * g
    dpre = rstd * (dxh - jnp.mean(dxh, axis=1, keepdims=True) - xhat * jnp.mean(dxh * xhat, axis=1, keepdims=True))
    return (dpre,), (_colsum(dout * xhat), _colsum(dout))


def ln_fwd(resid, y, g, b, *, name):
    (out,), _ = rowk(_ln_fwd_fn, [(resid, D_MODEL, 0), (y, D_MODEL, 0)], [g, b], [D_MODEL], [],
                     rows=resid.shape[0], name=name)
    return out


def ln_bwd(resid, y, dout, g, *, name):
    (dpre,), (dg, db) = rowk(_ln_bwd_fn, [(resid, D_MODEL, 0), (y, D_MODEL, 0), (dout, D_MODEL, 0)], [g],
                             [D_MODEL], [(1, D_MODEL), (1, D_MODEL)], rows=resid.shape[0], name=name)
    return dpre, dg, db


def _loss_fn(y, tgt):
    e = y - tgt
    part = _colsum(_rowsum(e * e)) * (0.5 / D_MODEL)
    return (e * (1.0 / D_MODEL),), (part,)


_XA_SCALE = 1.0 / math.sqrt(XA_HEAD_DIM)


def _attn_probs(qh, kh):
    s = _dot(qh, kh, _NT) * _XA_SCALE
    e = jnp.exp(s - jnp.max(s, axis=1, keepdims=True))
    return e / _rowsum(e)


def _attn_fwd_fn(q, k, v):
    outs = []
    for hd in range(XA_HEADS):
        sl = slice(hd * XA_HEAD_DIM, (hd + 1) * XA_HEAD_DIM)
        outs.append(_dot(_attn_probs(q[:, sl], k[:, sl]), v[:, sl]))
    return (jnp.concatenate(outs, axis=1),), ()


def _attn_bwd_fn(q, do, k, v):
    dqs, dks, dvs = [], [], []
    for hd in range(XA_HEADS):
        sl = slice(hd * XA_HEAD_DIM, (hd + 1) * XA_HEAD_DIM)
        qh, kh, vh, doh = q[:, sl], k[:, sl], v[:, sl], do[:, sl]
        p = _attn_probs(qh, kh)
        dp = _dot(doh, vh, _NT)
        ds = p * (dp - _rowsum(p * dp)) * _XA_SCALE
        dqs.append(_dot(ds, kh))
        dks.append(_dot(ds, qh, _TN))
        dvs.append(_dot(p, doh, _TN))
    cat = lambda xs: jnp.concatenate(xs, axis=1)
    return (cat(dqs),), (cat(dks), cat(dvs))


def _s5_post_fwd_fn(ylin, u, dskip, gw, gb):
    yg = _gelu(ylin + dskip * u)
    return (yg * _sigmoid(_dot(yg, gw) + gb),), ()


def _s5_post_bwd_fn(ylin, u, dout, dskip, gw, gb):
    pre = ylin + dskip * u
    yg = _gelu(pre)
    sg = _sigmoid(_dot(yg, gw) + gb)
    dlin = dout * yg * sg * (1.0 - sg)
    dyg = dout * sg + _dot(dlin, gw, _NT)
    dpre = dyg * _dgelu(pre)
    return (dpre, dpre * dskip), (_colsum(dpre * u), _dot(yg, dlin, _TN), _colsum(dlin))


def _rg_gates(xc, wa, wx, ba, bx, lam):
    r = _sigmoid(_dot(xc, wa) + ba)
    i = _sigmoid(_dot(xc, wx) + bx)
    sp = _softplus(-lam)
    log_a = -RG_C * r * sp
    a = jnp.exp(log_a)
    mult = jnp.sqrt(_neg_expm1(2.0 * log_a))
    return r, i, sp, a, mult


def _rg_pre_fwd_fn(xc, wa, wx, ba, bx, lam):
    r, i, sp, a, mult = _rg_gates(xc, wa, wx, ba, bx, lam)
    return (a, mult * (i * xc)), ()


def _rg_pre_bwd_fn(xc, gsc, hprev, wa, wx, ba, bx, lam):
    r, i, sp, a, mult = _rg_gates(xc, wa, wx, ba, bx, lam)
    da = gsc * hprev
    db = gsc
    dmult = db * i * xc
    di = db * mult * xc
    dxc = db * mult * i
    dlog_a = da * a - a * a * dmult / mult
    dr = dlog_a * (-RG_C * sp)
    dsp = _colsum(dlog_a * (-RG_C * r))
    dlam = dsp * (-_sigmoid(-lam))
    dpr = dr * r * (1.0 - r)
    dpi = di * i * (1.0 - i)
    dxc = dxc + _dot(dpr, wa, _NT) + _dot(dpi, wx, _NT)
    return (dxc,), (_dot(xc, dpr, _TN), _dot(xc, dpi, _TN), _colsum(dpr), _colsum(dpi), dlam)


def _rg_out_fwd_fn(h, g):
    return (h * _gelu(g),), ()


def _rg_out_bwd_fn(h, g, dy):
    return (dy * _gelu(g), dy * h * _dgelu(g)), ()


def _shift_down(x, prev, j, rows):
    return jnp.where(rows < j, pltpu.roll(prev, j, 0), pltpu.roll(x, j, 0))


def _shift_up(x, nxt, j, rows):
    t = x.shape[0]
    return jnp.where(rows >= t - j, pltpu.roll(nxt, t - j, 0), pltpu.roll(x, t - j, 0))


def conv_fwd(src, cb, w, b, *, width, act, name):
    t = src.shape[0]
    tt = min(ROW_TILE, t)
    n = t // tt

    def body(x_ref, w_ref, b_ref, y_ref, prev_ref):
        @pl.when(pl.program_id(0) == 0)
        def _():
            prev_ref[...] = jnp.zeros_like(prev_ref)

        x = x_ref[...]
        prev = prev_ref[...]
        rows = lax.broadcasted_iota(jnp.int32, x.shape, 0)
        wv = w_ref[...]
        y = b_ref[...] + wv[3:4, :] * x
        for j in (1, 2, 3):
            y = y + wv[3 - j:4 - j, :] * _shift_down(x, prev, j, rows)
        y_ref[...] = _silu(y) if act else y
        prev_ref[...] = x

    return pl.pallas_call(
        body, name=name, grid=(n,),
        in_specs=[pl.BlockSpec((tt, width), lambda i: (i, cb)),
                  pl.BlockSpec((4, width), lambda i: (0, 0)), pl.BlockSpec((1, width), lambda i: (0, 0))],
        out_specs=pl.BlockSpec((tt, width), lambda i: (i, 0)),
        out_shape=jax.ShapeDtypeStruct((t, width), F32),
        scratch_shapes=[pltpu.VMEM((tt, width), F32)],
        compiler_params=_params(("arbitrary",)),
    )(src, w, b)


def conv_bwd(src, cb, dy, w, b, *, width, act, name):
    t = src.shape[0]
    tt = min(ROW_TILE, t)
    n = t // tt

    def body(x_ref, xp_ref, dy_ref, w_ref, b_ref, dx_ref, dw_ref, db_ref, nxt_ref):
        i = pl.program_id(0)

        @pl.when(i == 0)
        def _():
            nxt_ref[...] = jnp.zeros_like(nxt_ref)
            dw_ref[...] = jnp.zeros_like(dw_ref)
            db_ref[...] = jnp.zeros_like(db_ref)

        x = x_ref[...]
        prev = jnp.where(i == n - 1, 0.0, xp_ref[...])
        rows = lax.broadcasted_iota(jnp.int32, x.shape, 0)
        wv = w_ref[...]
        xs = [x] + [_shift_down(x, prev, j, rows) for j in (1, 2, 3)]
        dpre = dy_ref[...]
        if act:
            pre = b_ref[...] + wv[3:4, :] * xs[0]
            for j in (1, 2, 3):
                pre = pre + wv[3 - j:4 - j, :] * xs[j]
            dpre = dpre * _dsilu(pre)
        nxt = nxt_ref[...]
        dx = wv[3:4, :] * dpre
        for j in (1, 2, 3):
            dx = dx + wv[3 - j:4 - j, :] * _shift_up(dpre, nxt, j, rows)
        dx_ref[...] = dx
        dw_ref[...] += jnp.concatenate([_colsum(dpre * xs[3 - kk]) for kk in range(4)], axis=0)
        db_ref[...] += _colsum(dpre)
        nxt_ref[...] = dpre

    return pl.pallas_call(
        body, name=name, grid=(n,),
        in_specs=[pl.BlockSpec((tt, width), lambda i: (n - 1 - i, cb)),
                  pl.BlockSpec((tt, width), lambda i: (jnp.maximum(n - 2 - i, 0), cb)),
                  pl.BlockSpec((tt, width), lambda i: (n - 1 - i, 0)),
                  pl.BlockSpec((4, width), lambda i: (0, 0)), pl.BlockSpec((1, width), lambda i: (0, 0))],
        out_specs=[pl.BlockSpec((tt, width), lambda i: (n - 1 - i, 0)),
                   pl.BlockSpec((4, width), lambda i: (0, 0)), pl.BlockSpec((1, width), lambda i: (0, 0))],
        out_shape=[jax.ShapeDtypeStruct((t, width), F32), jax.ShapeDtypeStruct((4, width), F32),
                   jax.ShapeDtypeStruct((1, width), F32)],
        scratch_shapes=[pltpu.VMEM((tt, width), F32)],
        compiler_params=_params(("arbitrary",)),
    )(src, src, dy, w, b)


SCAN_CW = 512


def scan_complex(bu, lam, *, reverse, name):
    t, w2 = bu.shape
    w = w2 // 2
    tt = min(ROW_TILE, t)
    n, nb, cw = t // tt, tt // 8, min(SCAN_CW, w)

    def body(b_ref, lam_ref, o_ref, st_ref):
        @pl.when(pl.program_id(0) == 0)
        def _():
            st_ref[...] = jnp.zeros_like(st_ref)

        rows = lax.broadcasted_iota(jnp.int32, (8, cw), 0)
        for c0 in range(0, w, cw):
            re, im = pl.ds(c0, cw), pl.ds(w + c0, cw)
            ar = jnp.broadcast_to(lam_ref[:, re], (8, cw))
            ai = jnp.broadcast_to(lam_ref[:, im], (8, cw))

            def blk(i, carry):
                hr, hi = carry
                base = pl.multiple_of((nb - 1 - i if reverse else i) * 8, 8)
                tr, ti = b_ref[pl.ds(base, 8), re], b_ref[pl.ds(base, 8), im]
                outr, outi = jnp.zeros((8, cw), F32), jnp.zeros((8, cw), F32)
                for j in (range(7, -1, -1) if reverse else range(8)):
                    br = jnp.broadcast_to(tr[j:j + 1, :], (8, cw))
                    bi = jnp.broadcast_to(ti[j:j + 1, :], (8, cw))
                    hr, hi = ar * hr - ai * hi + br, ar * hi + ai * hr + bi
                    outr = jnp.where(rows == j, hr, outr)
                    outi = jnp.where(rows == j, hi, outi)
                o_ref[pl.ds(base, 8), re] = outr
                o_ref[pl.ds(base, 8), im] = outi
                return hr, hi

            hr, hi = lax.fori_loop(0, nb, blk, (st_ref[:, re], st_ref[:, im]))
            st_ref[:, re] = hr
            st_ref[:, im] = hi

    idx = (lambda i: (n - 1 - i, 0)) if reverse else (lambda i: (i, 0))
    return pl.pallas_call(
        body, name=name, grid=(n,),
        in_specs=[pl.BlockSpec((tt, w2), idx), pl.BlockSpec((1, w2), lambda i: (0, 0))],
        out_specs=pl.BlockSpec((tt, w2), idx), out_shape=jax.ShapeDtypeStruct((t, w2), F32),
        scratch_shapes=[pltpu.VMEM((8, w2), F32)],
        compiler_params=_params(("arbitrary",)),
    )(bu, lam)


def scan_real(a, b, *, reverse, name):
    t, w = b.shape
    tt = min(ROW_TILE, t)
    n, nb = t // tt, tt // 8

    def body(a_ref, b_ref, o_ref, st_ref):
        @pl.when(pl.program_id(0) == 0)
        def _():
            st_ref[...] = jnp.zeros_like(st_ref)

        rows = lax.broadcasted_iota(jnp.int32, (8, w), 0)

        def blk(i, h):
            base = pl.multiple_of((nb - 1 - i if reverse else i) * 8, 8)
            ta_, tb_ = a_ref[pl.ds(base, 8), :], b_ref[pl.ds(base, 8), :]
            out = jnp.zeros((8, w), F32)
            for j in (range(7, -1, -1) if reverse else range(8)):
                h = jnp.broadcast_to(ta_[j:j + 1, :], (8, w)) * h + jnp.broadcast_to(tb_[j:j + 1, :], (8, w))
                out = jnp.where(rows == j, h, out)
            o_ref[pl.ds(base, 8), :] = out
            return h

        st_ref[...] = lax.fori_loop(0, nb, blk, st_ref[...])

    idx = (lambda i: (n - 1 - i, 0)) if reverse else (lambda i: (i, 0))
    return pl.pallas_call(
        body, name=name, grid=(n,),
        in_specs=[pl.BlockSpec((tt, w), idx), pl.BlockSpec((tt, w), idx)],
        out_specs=pl.BlockSpec((tt, w), idx), out_shape=jax.ShapeDtypeStruct((t, w), F32),
        scratch_shapes=[pltpu.VMEM((8, w), F32)],
        compiler_params=_params(("arbitrary",)),
    )(a, b)


def s5_dlam(g, hprev, *, name):
    t, w2 = g.shape
    w = w2 // 2

    def fn(gt, ht):
        gr, gi, hr, hi = gt[:, :w], gt[:, w:], ht[:, :w], ht[:, w:]
        return (), (_colsum(gr * hr + gi * hi), _colsum(gi * hr - gr * hi))

    _, (dar, dai) = rowk(fn, [(g, w2, 0), (hprev, w2, 0)], [], [], [(1, w), (1, w)], rows=t, name=name)
    return dar, dai


def _ssd_common(dt_ref, dtT_ref, prow_ref, pcol_ref):
    q = SSD_CHUNK
    r = lax.broadcasted_iota(jnp.int32, (q, q), 0)
    c = lax.broadcasted_iota(jnp.int32, (q, q), 1)
    low = r >= c
    tril = low.astype(F32)
    triu = (r <= c).astype(F32)
    bias_r, alog_r = prow_ref[0:1, :], prow_ref[1:2, :]
    raw_c = dt_ref[...] + bias_r
    dt_c = _softplus(raw_c)
    a_r = -jnp.exp(alog_r)
    cs_c = _dot_mask(tril, dt_c * a_r, mask_left=True, parts=3)
    dt_r = _softplus(dtT_ref[...] + pcol_ref[:, 0:1])
    cs_r = _dot_mask(dt_r * (-jnp.exp(pcol_ref[:, 1:2])), triu, mask_left=False, parts=3)
    return low, tril, triu, raw_c, dt_c, a_r, cs_c, cs_r


def _ssd_gate(yraw, z, nw):
    yg = yraw * _silu(z)
    r = lax.rsqrt(jnp.mean(yg * yg, axis=1, keepdims=True) + LN_EPS)
    return yg, r


def ssd_fwd(xbc, proj, dtT, prow, pcol, nw, *, name):
    t = xbc.shape[0]
    q, p, ns = SSD_CHUNK, SSD_HEAD_DIM, SSD_STATE
    nc = t // q

    def body(xbc_ref, z_ref, dt_ref, dtT_ref, prow_ref, pcol_ref, nw_ref, y_ref, yraw_ref, sall_ref, s_ref, ybuf):
        @pl.when(pl.program_id(0) == 0)
        def _():
            s_ref[...] = jnp.zeros_like(s_ref)

        sall_ref[0] = s_ref[...]
        low, tril, triu, raw_c, dt_c, a_r, cs_c, cs_r = _ssd_common(dt_ref, dtT_ref, prow_ref, pcol_ref)
        d_r = prow_ref[2:3, :]
        bm = [xbc_ref[:, pl.ds(SSD_WIDTH + g * ns, ns)] for g in range(2)]
        cm = [xbc_ref[:, pl.ds(SSD_WIDTH + 2 * ns + g * ns, ns)] for g in range(2)]
        cb = [_dot(cm[g], bm[g], _NT) for g in range(2)]
        for h in range(SSD_HEADS):
            g = h // 4
            hs = pl.ds(h * p, p)
            csc, csr = cs_c[:, h:h + 1], cs_r[h:h + 1, :]
            lmat = jnp.exp(jnp.where(low, csc - csr, -1e30))
            xs = xbc_ref[:, hs]
            xdt = xs * dt_c[:, h:h + 1]
            sh = s_ref[hs, :]
            y = (_dot(cb[g] * lmat, xdt) + jnp.exp(csc) * _dot(cm[g], sh, _NT)
                 + xs * d_r[:, h:h + 1])
            ybuf[:, hs] = y
            cl = csc[q - 1:q, :]
            s_ref[hs, :] = jnp.exp(cl) * sh + _dot(xdt * jnp.exp(cl - csc), bm[g], _TN)
        yraw = ybuf[...]
        yraw_ref[...] = yraw
        yg, r = _ssd_gate(yraw, z_ref[...], nw_ref[...])
        y_ref[...] = yg * r * nw_ref[...]

    return pl.pallas_call(
        body, name=name, grid=(nc,),
        in_specs=[pl.BlockSpec((q, SSD_XBC), lambda i: (i, 0)),
                  pl.BlockSpec((q, SSD_WIDTH), lambda i: (i, P_Z // SSD_WIDTH)),
                  pl.BlockSpec((q, LANE), lambda i: (i, P_DT // LANE)),
                  pl.BlockSpec((SSD_HEADS, q), lambda i: (0, i)),
                  pl.BlockSpec((8, LANE), lambda i: (0, 0)), pl.BlockSpec((8, LANE), lambda i: (0, 0)),
                  pl.BlockSpec((1, SSD_WIDTH), lambda i: (0, 0))],
        out_specs=[pl.BlockSpec((q, SSD_WIDTH), lambda i: (i, 0)), pl.BlockSpec((q, SSD_WIDTH), lambda i: (i, 0)),
                   pl.BlockSpec((1, SSD_WIDTH, ns), lambda i: (i, 0, 0))],
        out_shape=[jax.ShapeDtypeStruct((t, SSD_WIDTH), F32), jax.ShapeDtypeStruct((t, SSD_WIDTH), F32),
                   jax.ShapeDtypeStruct((nc, SSD_WIDTH, ns), F32)],
        scratch_shapes=[pltpu.VMEM((SSD_WIDTH, ns), F32), pltpu.VMEM((q, SSD_WIDTH), F32)],
        compiler_params=_params(("arbitrary",)),
    )(xbc, proj, proj, dtT, prow, pcol, nw)


def ssd_bwd(xbc, proj, dtT, prow, pcol, nw, yraw, sall, dout, *, name):
    t = xbc.shape[0]
    q, p, ns = SSD_CHUNK, SSD_HEAD_DIM, SSD_STATE
    nc = t // q

    def body(xbc_ref, z_ref, dt_ref, dtT_ref, prow_ref, pcol_ref, nw_ref, yraw_ref, sall_ref, dout_ref,
             dxbc_ref, dz_ref, ddt_ref, dprm_ref, dnw_ref, ds_ref, dyb):
        @pl.when(pl.program_id(0) == 0)
        def _():
            ds_ref[...] = jnp.zeros_like(ds_ref)
            dprm_ref[...] = jnp.zeros_like(dprm_ref)
            dnw_ref[...] = jnp.zeros_like(dnw_ref)

        yraw, z, nwv, dout = yraw_ref[...], z_ref[...], nw_ref[...], dout_ref[...]
        yg, r = _ssd_gate(yraw, z, nwv)
        dnw_ref[...] += _colsum(dout * yg * r)
        dyn = dout * nwv
        dyg = r * dyn - yg * (r * r * r) * jnp.mean(dyn * yg, axis=1, keepdims=True)
        dyb[...] = dyg * _silu(z)
        dz_ref[...] = dyg * yraw * _dsilu(z)

        low, tril, triu, raw_c, dt_c, a_r, cs_c, cs_r = _ssd_common(dt_ref, dtT_ref, prow_ref, pcol_ref)
        d_r = prow_ref[2:3, :]
        lane = lax.broadcasted_iota(jnp.int32, (1, LANE), 1)
        ones = jnp.ones((q, LANE), F32)
        last = (lax.broadcasted_iota(jnp.int32, (q, 1), 0) == q - 1).astype(F32)
        bm = [xbc_ref[:, pl.ds(SSD_WIDTH + g * ns, ns)] for g in range(2)]
        cm = [xbc_ref[:, pl.ds(SSD_WIDTH + 2 * ns + g * ns, ns)] for g in range(2)]
        cb = [_dot(cm[g], bm[g], _NT) for g in range(2)]
        dbm = [jnp.zeros((q, ns), F32) for _ in range(2)]
        dcm = [jnp.zeros((q, ns), F32) for _ in range(2)]
        dcs_all = jnp.zeros((q, LANE), F32)
        ddt_all = jnp.zeros((q, LANE), F32)
        dd_all = jnp.zeros((1, LANE), F32)
        for h in range(SSD_HEADS):
            g = h // 4
            hs = pl.ds(h * p, p)
            onehot = (lane == h).astype(F32)
            csc, csr = cs_c[:, h:h + 1], cs_r[h:h + 1, :]
            lmat = jnp.exp(jnp.where(low, csc - csr, -1e30))
            xs = xbc_ref[:, hs]
            dth = dt_c[:, h:h + 1]
            xdt = xs * dth
            sh = sall_ref[0, hs, :]
            dy = dyb[:, hs]
            ecs = jnp.exp(csc)
            cl = csc[q - 1:q, :]
            ecl = jnp.exp(cl)
            wdec = jnp.exp(cl - csc)
            wmat = cb[g] * lmat
            dwm = _dot(dy, xdt, _NT)
            dx = _dot(wmat, dy, _TN)
            emat = dwm * wmat
            dmm = dwm * lmat
            dcm[g] = dcm[g] + _dot(dmm, bm[g])
            dbm[g] = dbm[g] + _dot(dmm, cm[g], _TN)
            dcs = _rowsum(emat) - _dot_mask(emat, ones, _TN, mask_left=False, parts=2)[:, 0:1]
            zmat = _dot(cm[g], sh, _NT)
            dzm = ecs * dy
            dcm[g] = dcm[g] + _dot(dzm, sh)
            dsp = _dot(dzm, cm[g], _TN)
            dcs = dcs + _rowsum(dzm * zmat)
            dsn = ds_ref[hs, :]
            dsp = dsp + ecl * dsn
            dcl = _colsum(_rowsum(dsn * sh)) * ecl
            xw = xdt * wdec
            dxw = _dot(bm[g], dsn, _NT)
            dbm[g] = dbm[g] + _dot(xw, dsn)
            dx = dx + wdec * dxw
            tw = _rowsum(dxw * xdt) * wdec
            dcl = dcl + _colsum(tw)
            dcs = dcs - tw + last * dcl
            ds_ref[hs, :] = dsp
            dxbc_ref[:, hs] = dx * dth + dy * d_r[:, h:h + 1]
            dcs_all = dcs_all + dcs * onehot
            ddt_all = ddt_all + _rowsum(dx * xs) * onehot
            dd_all = dd_all + _colsum(_rowsum(dy * xs)) * onehot
        for g in range(2):
            dxbc_ref[:, pl.ds(SSD_WIDTH + g * ns, ns)] = dbm[g]
            dxbc_ref[:, pl.ds(SSD_WIDTH + 2 * ns + g * ns, ns)] = dcm[g]
        dadt = _dot_mask(triu, dcs_all, mask_left=True, parts=2)
        ddt = ddt_all + dadt * a_r
        draw = ddt * _sigmoid(raw_c)
        ddt_ref[...] = draw
        zero = jnp.zeros((5, LANE), F32)
        dprm_ref[...] += jnp.concatenate([_colsum(draw), _colsum(dadt * dt_c) * a_r, dd_all, zero], axis=0)

    rev = lambda cbk: (lambda i: (nc - 1 - i, cbk))
    return pl.pallas_call(
        body, name=name, grid=(nc,),
        in_specs=[pl.BlockSpec((q, SSD_XBC), rev(0)),
                  pl.BlockSpec((q, SSD_WIDTH), rev(P_Z // SSD_WIDTH)),
                  pl.BlockSpec((q, LANE), rev(P_DT // LANE)),
                  pl.BlockSpec((SSD_HEADS, q), lambda i: (0, nc - 1 - i)),
                  pl.BlockSpec((8, LANE), lambda i: (0, 0)), pl.BlockSpec((8, LANE), lambda i: (0, 0)),
                  pl.BlockSpec((1, SSD_WIDTH), lambda i: (0, 0)),
                  pl.BlockSpec((q, SSD_WIDTH), rev(0)),
                  pl.BlockSpec((1, SSD_WIDTH, ns), lambda i: (nc - 1 - i, 0, 0)),
                  pl.BlockSpec((q, SSD_WIDTH), rev(0))],
        out_specs=[pl.BlockSpec((q, SSD_XBC), rev(0)), pl.BlockSpec((q, SSD_WIDTH), rev(0)),
                   pl.BlockSpec((q, LANE), rev(0)),
                   pl.BlockSpec((8, LANE), lambda i: (0, 0)), pl.BlockSpec((1, SSD_WIDTH), lambda i: (0, 0))],
        out_shape=[jax.ShapeDtypeStruct((t, SSD_XBC), F32), jax.ShapeDtypeStruct((t, SSD_WIDTH), F32),
                   jax.ShapeDtypeStruct((t, LANE), F32), jax.ShapeDtypeStruct((8, LANE), F32),
                   jax.ShapeDtypeStruct((1, SSD_WIDTH), F32)],
        scratch_shapes=[pltpu.VMEM((SSD_WIDTH, ns), F32), pltpu.VMEM((q, SSD_WIDTH), F32)],
        compiler_params=_params(("arbitrary",)),
    )(xbc, proj, proj, dtT, prow, pcol, nw, yraw, sall, dout)


def _me():
    return lax.axis_index("x"), lax.axis_index("y"), lax.axis_index("c")


_ANY = pl.BlockSpec(memory_space=pl.ANY)
_MESH = pl.DeviceIdType.MESH


def all_gather(block, *, name):
    def body(src, dst, send_sems, recv_sems, local_sem):
        x, y, c = _me()
        me, sibling = (x, y, c), (x, y, 1 - c)
        chips = [(1 - x, y), (x, 1 - y), (1 - x, 1 - y)]

        def slot(px, py, pc):
            return dst.at[4 * px + 2 * py + pc]

        def copy(kk, blk, to, from_src=False):
            return pltpu.make_async_remote_copy(
                src_ref=src if from_src else slot(*blk), dst_ref=slot(*blk),
                send_sem=send_sems.at[kk], recv_sem=recv_sems.at[kk], device_id=to, device_id_type=_MESH)

        mine = pltpu.make_async_copy(src, slot(*me), local_sem)
        mine.start()
        first = [copy(0, me, sibling, True)] + [copy(1 + j, me, (*chip, c), True) for j, chip in enumerate(chips)]
        for cp in first:
            cp.start()
        passed = [copy(4 + j, (*chip, c), sibling) for j, chip in enumerate(chips)]
        for j, chip in enumerate(chips):
            copy(1 + j, (*chip, c), me).wait_recv()
            passed[j].start()
        copy(0, sibling, me).wait_recv()
        for j, chip in enumerate(chips):
            copy(4 + j, (*chip, 1 - c), me).wait_recv()
        for cp in first + passed:
            cp.wait_send()
        mine.wait()

    return pl.pallas_call(
        body, name=name, in_specs=[_ANY], out_specs=_ANY,
        out_shape=jax.ShapeDtypeStruct((N_DEV,) + block.shape, block.dtype),
        scratch_shapes=[pltpu.SemaphoreType.DMA((7,)), pltpu.SemaphoreType.DMA((7,)), pltpu.SemaphoreType.DMA(())],
    )(block)


RS_PIECES = 4


def rs_sibling_exchange(halves, *, name):
    _, nq, r, l = halves.shape
    rows = r // RS_PIECES
    assert r % RS_PIECES == 0 and rows % 16 == 0

    def body(src, dst, send_sems, recv_sems):
        x, y, c = _me()
        copies = []
        for q in range(nq):
            for i in range(RS_PIECES):
                kk = q * RS_PIECES + i
                cp = pltpu.make_async_remote_copy(
                    src_ref=src.at[1 - c, q, pl.ds(i * rows, rows)], dst_ref=dst.at[q, pl.ds(i * rows, rows)],
                    send_sem=send_sems.at[kk], recv_sem=recv_sems.at[kk], device_id=(x, y, 1 - c), device_id_type=_MESH)
                cp.start()
                copies.append(cp)
        for cp in copies:
            cp.wait()

    n_copies = nq * RS_PIECES
    return pl.pallas_call(
        body, name=name, in_specs=[_ANY], out_specs=_ANY,
        out_shape=jax.ShapeDtypeStruct((nq, r, l), halves.dtype),
        scratch_shapes=[pltpu.SemaphoreType.DMA((n_copies,)), pltpu.SemaphoreType.DMA((n_copies,))],
    )(halves)


def pair_sum_bf16(halves, theirs, *, name, tt=128):
    _, nq, r, wd = halves.shape
    tt = min(tt, r)
    parity = lax.axis_index("c").astype(jnp.int32).reshape(1)

    def body(c_ref, own_ref, sib_ref, o_ref):
        o_ref[...] = (own_ref[...] + sib_ref[...]).astype(BF16)

    return pl.pallas_call(
        body, name=name,
        grid_spec=pltpu.PrefetchScalarGridSpec(
            num_scalar_prefetch=1, grid=(nq, r // tt),
            in_specs=[pl.BlockSpec((None, None, tt, wd), lambda q, i, c: (c[0], q, i, 0)),
                      pl.BlockSpec((None, tt, wd), lambda q, i, c: (q, i, 0))],
            out_specs=pl.BlockSpec((None, tt, wd), lambda q, i, c: (q, i, 0))),
        out_shape=jax.ShapeDtypeStruct((nq, r, wd), BF16),
        compiler_params=_params(("parallel", "parallel")),
    )(parity, halves, theirs)


def rs_chip_exchange(part, *, name):
    def body(src, dst, send_sems, recv_sems, local_sem):
        x, y, c = _me()
        q_me = 2 * x + y
        local = pltpu.make_async_copy(src.at[q_me], dst.at[q_me], local_sem)
        local.start()
        copies = []
        for j, (px, py) in enumerate([(1 - x, y), (x, 1 - y), (1 - x, 1 - y)]):
            cp = pltpu.make_async_remote_copy(src_ref=src.at[2 * px + py], dst_ref=dst.at[q_me], send_sem=send_sems.at[j],
                                              recv_sem=recv_sems.at[j], device_id=(px, py, c), device_id_type=_MESH)
            cp.start()
            copies.append(cp)
        for cp in copies:
            cp.wait()
        local.wait()

    return pl.pallas_call(
        body, name=name, in_specs=[_ANY], out_specs=_ANY,
        out_shape=jax.ShapeDtypeStruct(part.shape, part.dtype),
        scratch_shapes=[pltpu.SemaphoreType.DMA((3,)), pltpu.SemaphoreType.DMA((3,)), pltpu.SemaphoreType.DMA(())],
    )(part)


def adamw(slabs, w, m, v, *, name, tt):
    ns, (r, wd) = slabs.shape[0], w.shape
    tt = min(tt, r)
    assert r % tt == 0

    def body(s_ref, w_ref, m_ref, v_ref, g_ref, d_ref, nm_ref, nv_ref):
        g = s_ref[0].astype(F32)
        for kdev in range(1, ns):
            g = g + s_ref[kdev].astype(F32)
        wv = w_ref[...]
        nm = ADAM_B1 * m_ref[...] + (1.0 - ADAM_B1) * g
        nv = ADAM_B2 * v_ref[...] + (1.0 - ADAM_B2) * (g * g)
        m_hat = nm / (1.0 - ADAM_B1 ** ADAM_STEP)
        v_hat = nv / (1.0 - ADAM_B2 ** ADAM_STEP)
        g_ref[...] = g
        d_ref[...] = -ADAM_LR * (m_hat / (jnp.sqrt(v_hat) + ADAM_EPS) + ADAM_WD * wv)
        nm_ref[...] = nm
        nv_ref[...] = nv

    spec = pl.BlockSpec((tt, wd), lambda i: (i, 0))
    return pl.pallas_call(
        body, name=name, grid=(r // tt,),
        in_specs=[pl.BlockSpec((ns, tt, wd), lambda i: (0, i, 0)), spec, spec, spec],
        out_specs=[spec] * 4, out_shape=[jax.ShapeDtypeStruct((r, wd), F32)] * 4,
        compiler_params=_params(("parallel",)),
    )(slabs, w, m, v)


WIDE = 1024
BIG = [("w_in", True, 289), ("w_out", False, 128), ("xa_wq", False, 128), ("xa_wk", False, 128), ("xa_wv", False, 128),
       ("xa_wo", False, 128), ("mlp_w2", False, 512), ("mlp_w1", True, 512)]
TINY = [("ssd_conv_w", 2), ("s5_glu_w", 1), ("rg_conv_w", 2)]
KEEP_F32 = ("ssd_conv_w", "rg_conv_w")
TINY_ROWS = 32
SHARDED = [name for name, _, _ in BIG] + [name for name, _ in TINY]
SMALL = ["ssd_conv_b", "ssd_dt_bias", "ssd_a_log", "ssd_d", "ssd_norm_w", "s5_lam_re", "s5_lam_im",
         "s5_log_step", "s5_b_re", "s5_b_im", "s5_c_re", "s5_c_im", "s5_d", "s5_glu_b", "rg_conv_b",
         "rg_wa", "rg_ba", "rg_wx", "rg_bx", "rg_lambda", "ln1_g", "ln1_b", "ln2_g", "ln2_b", "ln3_g", "ln3_b"]
WEIGHTS = ['w_in', 'w_out', 'ssd_conv_w', 'ssd_conv_b', 'ssd_dt_bias', 'ssd_a_log', 'ssd_d', 'ssd_norm_w',
           's5_lam_re', 's5_lam_im', 's5_log_step', 's5_b_re', 's5_b_im', 's5_c_re', 's5_c_im', 's5_d',
           's5_glu_w', 's5_glu_b', 'rg_conv_w', 'rg_conv_b', 'rg_wa', 'rg_ba', 'rg_wx', 'rg_bx', 'rg_lambda',
           'ln1_g', 'ln1_b', 'xa_wq', 'xa_wk', 'xa_wv', 'xa_wo', 'ln2_g', 'ln2_b', 'mlp_w1', 'mlp_w2',
           'ln3_g', 'ln3_b']


def _pad16(rows):
    return -(-rows // 16) * 16


def _pack_rows(flat, mult):
    n = flat.shape[-1]
    r = -(-n // (LANE * mult)) * mult
    pad = [(0, 0)] * (flat.ndim - 1) + [(0, r * LANE - n)]
    return jnp.pad(flat, pad).reshape(flat.shape[:-1] + (r, LANE))


def _unpack(packed, shapes):
    lead = packed.shape[:-2]
    flat = packed.reshape(lead + (-1,))
    out, off = [], 0
    for s in shapes:
        n = math.prod(s)
        out.append(flat[..., off:off + n].reshape(lead + tuple(s)))
        off += n
    return out


def _big_block(x, rows):
    pad = [(0, 0)] * (x.ndim - 2) + [(0, _pad16(rows) - rows), (0, 0)]
    x = jnp.pad(x, pad)
    return x.reshape(x.shape[:-3] + (DEPTH * _pad16(rows), WIDE))


def _tiny_block(flat):
    pad = [(0, 0)] * (flat.ndim - 1) + [(0, TINY_ROWS * WIDE - flat.shape[-1])]
    return jnp.pad(flat, pad).reshape(flat.shape[:-1] + (TINY_ROWS, WIDE))


def _pack_wide(big, tiny_flat):
    blocks = [_big_block(big[name], rows) for name, _, rows in BIG] + [_tiny_block(tiny_flat)]
    return jnp.concatenate(blocks, axis=-2)


def _unpack_wide(packed):
    lead, big, off = packed.shape[:-2], {}, 0
    for name, _, rows in BIG:
        rp = _pad16(rows)
        big[name] = packed[..., off:off + DEPTH * rp, :].reshape(lead + (DEPTH, rp, WIDE))[..., :rows, :]
        off += DEPTH * rp
    return big, packed[..., off:off + TINY_ROWS, :].reshape(lead + (TINY_ROWS * WIDE,))


def _split_flat(flat, shapes):
    out, off = [], 0
    for s in shapes:
        n = math.prod(s)
        out.append(flat[..., off:off + n].reshape(flat.shape[:-1] + tuple(s)))
        off += n
    return out


def _to_full(gathered, axis):
    g = jnp.moveaxis(gathered, 0, axis)
    s = g.shape
    return g.reshape(s[:axis] + (s[axis] * s[axis + 1],) + s[axis + 2:])


def _to_slabs(full, axis):
    s = full.shape
    g = full.reshape(s[:axis] + (N_DEV, s[axis] // N_DEV) + s[axis + 1:])
    return jnp.moveaxis(g, axis, 0)


def _blockdiag(w):
    h, i, j = w.shape
    eye = jnp.eye(h, dtype=w.dtype)
    return (w[:, :, None, :] * eye[:, None, :, None]).reshape(h * i, h * j)


def _blockdiag_extract(m, h):
    i, j = m.shape[0] // h, m.shape[1] // h
    eye = jnp.eye(h, dtype=m.dtype)
    return (m.reshape(h, i, h, j) * eye[:, None, :, None]).sum(axis=2)


def _s5_disc(lr, li, ls, bre, bim):
    step = jnp.exp(ls)[:, None]
    er = jnp.exp(lr * step)
    ar, ai = er * jnp.cos(li * step), er * jnp.sin(li * step)
    nr, ni, den = ar - 1.0, ai, lr * lr + li * li
    qr, qi = (nr * lr + ni * li) / den, (ni * lr - nr * li) / den
    bbr = qr[..., None] * bre - qi[..., None] * bim
    bbi = qr[..., None] * bim + qi[..., None] * bre
    return ar, ai, bbr, bbi


def _row(v, width=None):
    v = v.reshape(1, -1)
    if width is not None and v.shape[1] < width:
        v = jnp.pad(v, ((0, 0), (0, width - v.shape[1])))
    return v


def _relu2(a):
    r = jnp.maximum(a, 0.0)
    return r * r


def _add_alpha(acc, d):
    return acc + ALPHA * d


def _shift_rows_down(x):
    return jnp.concatenate([jnp.zeros((1, x.shape[1]), x.dtype), x[:-1]], axis=0)


def _shift_rows_up(x):
    return jnp.concatenate([x[1:], jnp.zeros((1, x.shape[1]), x.dtype)], axis=0)


def _layer_params(full, small, l):
    p = {}
    w_in = full["w_in"][l]
    z, xbc, dt, u, xr, g = w_in[0:512], w_in[512:1536], w_in[1536:1544], w_in[1544:1800], w_in[1800:2056], w_in[2056:2312]
    p["w_inp"] = jnp.concatenate([xbc, z, u, xr, g, dt, jnp.zeros((D_INP - P_DT - 8, D_MODEL), w_in.dtype)], axis=0)
    for k_ in ("w_out", "xa_wq", "xa_wk", "xa_wv", "xa_wo", "mlp_w1", "mlp_w2", "s5_glu_w"):
        p[k_] = full[k_][l]
    p["ssd_cw"], p["ssd_cb"] = full["ssd_conv_w"][l], _row(small["ssd_conv_b"][l])
    dtb, alog, dsk = small["ssd_dt_bias"][l], small["ssd_a_log"][l], small["ssd_d"][l]
    p["prow"] = jnp.concatenate([_row(dtb, LANE), _row(alog, LANE), _row(dsk, LANE), jnp.zeros((5, LANE), F32)], axis=0)
    p["pcol"] = jnp.pad(jnp.stack([dtb, alog], axis=1), ((0, 0), (0, LANE - 2)))
    p["ssd_nw"] = _row(small["ssd_norm_w"][l])
    s5_in = (small["s5_lam_re"][l], small["s5_lam_im"][l], small["s5_log_step"][l], small["s5_b_re"][l], small["s5_b_im"][l])
    (ar, ai, bbr, bbi), p["s5_vjp"] = jax.vjp(_s5_disc, *s5_in)
    p["lam_fwd"] = jnp.concatenate([_row(ar), _row(ai)], axis=1)
    p["lam_adj"] = jnp.concatenate([_row(ar), _row(-ai)], axis=1)
    p["bcat"] = jnp.concatenate([_blockdiag(jnp.swapaxes(bbr, 1, 2)), _blockdiag(jnp.swapaxes(bbi, 1, 2))], axis=1)
    p["ccat"] = jnp.concatenate([_blockdiag(jnp.swapaxes(small["s5_c_re"][l], 1, 2)),
                                 -_blockdiag(jnp.swapaxes(small["s5_c_im"][l], 1, 2))], axis=0)
    p["s5_d"], p["s5_glu_b"] = _row(small["s5_d"][l]), _row(small["s5_glu_b"][l])
    p["rg_cw"], p["rg_cb"] = full["rg_conv_w"][l], _row(small["rg_conv_b"][l])
    p["rg_wa"], p["rg_wx"] = _blockdiag(small["rg_wa"][l]), _blockdiag(small["rg_wx"][l])
    p["rg_ba"], p["rg_bx"], p["rg_lam"] = _row(small["rg_ba"][l]), _row(small["rg_bx"][l]), _row(small["rg_lambda"][l])
    for i in (1, 2, 3):
        p[f"g{i}"], p[f"b{i}"] = _row(small[f"ln{i}_g"][l]), _row(small[f"ln{i}_b"][l])
    return p


def _layer_fwd(h0, mem, p):
    t = h0.shape[0]
    s = {"h0": h0}
    proj = mm(h0, p["w_inp"], tb=True, name="in_proj")
    dtT = proj[:, P_DT:P_DT + SSD_HEADS].T
    xbc = conv_fwd(proj, 0, p["ssd_cw"], p["ssd_cb"], width=SSD_XBC, act=True, name="ssd_conv_fwd")
    y_ssd, yraw, sall = ssd_fwd(xbc, proj, dtT, p["prow"], p["pcol"], p["ssd_nw"], name="ssd_fwd")
    bu = mm(proj, p["bcat"], a_off=P_U, k=S5_WIDTH, name="s5_bu")
    hs5 = scan_complex(bu, p["lam_fwd"], reverse=False, name="s5_scan_fwd")
    ylin = mm(hs5, p["ccat"], name="s5_ylin")
    (y_s5,), _ = rowk(_s5_post_fwd_fn, [(ylin, S5_WIDTH, 0), (proj, S5_WIDTH, P_U // S5_WIDTH)],
                      [p["s5_d"], p["s5_glu_w"], p["s5_glu_b"]], [S5_WIDTH], [], rows=t, name="s5_post_fwd")
    xc = conv_fwd(proj, P_XR // RG_WIDTH, p["rg_cw"], p["rg_cb"], width=RG_WIDTH, act=False, name="rg_conv_fwd")
    rg_full = [p["rg_wa"], p["rg_wx"], p["rg_ba"], p["rg_bx"], p["rg_lam"]]
    (a_rg, b_rg), _ = rowk(_rg_pre_fwd_fn, [(xc, RG_WIDTH, 0)], rg_full, [RG_WIDTH, RG_WIDTH], [], rows=t, name="rg_pre_fwd")
    h_rg = scan_real(a_rg, b_rg, reverse=False, name="rg_scan_fwd")
    (y_rg,), _ = rowk(_rg_out_fwd_fn, [(h_rg, RG_WIDTH, 0), (proj, RG_WIDTH, P_G // RG_WIDTH)], [], [RG_WIDTH], [],
                      rows=t, name="rg_out_fwd")
    ycat = jnp.concatenate([y_ssd, y_s5, y_rg], axis=1)
    mix = mm(ycat, p["w_out"], name="out_proj")
    h1 = ln_fwd(h0, mix, p["g1"], p["b1"], name="ln_fwd")
    q = mm(h1, p["xa_wq"], name="xa_q")
    k = mm(mem, p["xa_wk"], name="xa_kv")
    v = mm(mem, p["xa_wv"], name="xa_kv")
    (o,), _ = rowk(_attn_fwd_fn, [(q, D_MODEL, 0)], [k, v], [D_MODEL], [], rows=t, name="xa_fwd")
    att = mm(o, p["xa_wo"], name="xa_o")
    h2 = ln_fwd(h1, att, p["g2"], p["b2"], name="ln_fwd")
    a_mlp = mm(h2, p["mlp_w1"], tb=True, name="mlp_up")
    m_out = mm(a_mlp, p["mlp_w2"], fa=_relu2, name="mlp_down")
    h3 = ln_fwd(h2, m_out, p["g3"], p["b3"], name="ln_fwd")
    s.update(proj=proj, dtT=dtT, xbc=xbc, yraw=yraw, sall=sall, hs5=hs5, ylin=ylin, xc=xc, a_rg=a_rg, h_rg=h_rg,
             ycat=ycat, mix=mix, h1=h1, q=q, k=k, v=v, o=o, att=att, h2=h2, a_mlp=a_mlp, m_out=m_out)
    return h3, s


def _layer_bwd(dh3, mem, p, s, l, gfull, gsmall):
    t = dh3.shape[0]
    proj = s["proj"]
    dpre3, dg3, db3 = ln_bwd(s["h2"], s["m_out"], dh3, p["g3"], name="ln_bwd")
    da = mm(dpre3, p["mlp_w2"], tb=True, o_extra=(s["a_mlp"],), fo=lambda acc, a: acc * 2.0 * jnp.maximum(a, 0.0), name="mlp_da")
    gfull["mlp_w2"][l] = mm(s["a_mlp"], dpre3, ta=True, fa=_relu2, name="mlp_dw2")
    gfull["mlp_w1"][l] = mm(da, s["h2"], ta=True, name="mlp_dw1")
    dh2 = mm(da, p["mlp_w1"], o_extra=(dpre3,), fo=_add_alpha, name="mlp_dx")
    dpre2, dg2, db2 = ln_bwd(s["h1"], s["att"], dh2, p["g2"], name="ln_bwd")
    do = mm(dpre2, p["xa_wo"], tb=True, name="xa_do")
    gfull["xa_wo"][l] = mm(s["o"], dpre2, ta=True, name="dw_sq")
    (dq,), (dk, dv) = rowk(_attn_bwd_fn, [(s["q"], D_MODEL, 0), (do, D_MODEL, 0)], [s["k"], s["v"]], [D_MODEL],
                           [(256, D_MODEL), (256, D_MODEL)], rows=t, name="xa_bwd")
    gfull["xa_wq"][l] = mm(s["h1"], dq, ta=True, name="dw_sq")
    gfull["xa_wk"][l] = mm(mem, dk, ta=True, name="dw_kv")
    gfull["xa_wv"][l] = mm(mem, dv, ta=True, name="dw_kv")
    dh1 = mm(dq, p["xa_wq"], tb=True, o_extra=(dpre2,), fo=_add_alpha, name="dx_sq")
    dpre1, dg1, db1 = ln_bwd(s["h0"], s["mix"], dh1, p["g1"], name="ln_bwd")
    dycat = mm(dpre1, p["w_out"], tb=True, name="xa_do")
    gfull["w_out"][l] = mm(s["ycat"], dpre1, ta=True, name="dw_sq")
    (dh_rg, dg_rg), _ = rowk(_rg_out_bwd_fn, [(s["h_rg"], RG_WIDTH, 0), (proj, RG_WIDTH, P_G // RG_WIDTH), (dycat, RG_WIDTH, 3)],
                             [], [RG_WIDTH, RG_WIDTH], [], rows=t, name="rg_out_bwd")
    g_rg = scan_real(_shift_rows_up(s["a_rg"]), dh_rg, reverse=True, name="rg_scan_bwd")
    rg_full = [p["rg_wa"], p["rg_wx"], p["rg_ba"], p["rg_bx"], p["rg_lam"]]
    (dxc,), (dwa, dwx, dba, dbx, dlam) = rowk(
        _rg_pre_bwd_fn, [(s["xc"], RG_WIDTH, 0), (g_rg, RG_WIDTH, 0), (_shift_rows_down(s["h_rg"]), RG_WIDTH, 0)], rg_full,
        [RG_WIDTH], [(RG_WIDTH, RG_WIDTH), (RG_WIDTH, RG_WIDTH), (1, RG_WIDTH), (1, RG_WIDTH), (1, RG_WIDTH)],
        rows=t, name="rg_pre_bwd")
    dxr, d_rgcw, d_rgcb = conv_bwd(proj, P_XR // RG_WIDTH, dxc, p["rg_cw"], p["rg_cb"], width=RG_WIDTH, act=False, name="rg_conv_bwd")
    (dylin, du_a), (d_s5d, d_gluw, d_glub) = rowk(
        _s5_post_bwd_fn, [(s["ylin"], S5_WIDTH, 0), (proj, S5_WIDTH, P_U // S5_WIDTH), (dycat, S5_WIDTH, 2)],
        [p["s5_d"], p["s5_glu_w"], p["s5_glu_b"]], [S5_WIDTH, S5_WIDTH],
        [(1, S5_WIDTH), (S5_WIDTH, S5_WIDTH), (1, S5_WIDTH)], rows=t, name="s5_post_bwd")
    dhs = mm(dylin, p["ccat"], tb=True, name="s5_dh")
    dccat = mm(s["hs5"], dylin, ta=True, name="s5_dc")
    gs5 = scan_complex(dhs, p["lam_adj"], reverse=True, name="s5_scan_bwd")
    dar, dai = s5_dlam(gs5, _shift_rows_down(s["hs5"]), name="s5_dlam")
    du = mm(gs5, p["bcat"], tb=True, o_extra=(du_a,), fo=lambda acc, d: acc + d, name="s5_du")
    dbcat = mm(proj, gs5, ta=True, a_off=P_U, m=S5_WIDTH, name="s5_db")
    dxbc_act, dz, ddt, dprm, dnw = ssd_bwd(s["xbc"], proj, s["dtT"], p["prow"], p["pcol"], p["ssd_nw"], s["yraw"],
                                          s["sall"], dycat, name="ssd_bwd")
    dxbc, d_scw, d_scb = conv_bwd(proj, 0, dxbc_act, p["ssd_cw"], p["ssd_cb"], width=SSD_XBC, act=True, name="ssd_conv_bwd")
    dproj = jnp.concatenate([dxbc, dz, du, dxr, dg_rg, ddt, jnp.zeros((t, D_INP - P_DT - LANE), F32)], axis=1)
    dh0 = mm(dproj, p["w_inp"], o_extra=(dpre1,), fo=_add_alpha, name="in_proj_dx")
    dwp = mm(dproj, s["h0"], ta=True, name="in_proj_dw")
    gfull["w_in"][l] = jnp.concatenate([dwp[P_Z:P_Z + 512], dwp[P_XBC:P_XBC + 1024], dwp[P_DT:P_DT + 8],
                                        dwp[P_U:P_U + 256], dwp[P_XR:P_XR + 256], dwp[P_G:P_G + 256]], axis=0)
    gfull["ssd_conv_w"][l], gfull["rg_conv_w"][l], gfull["s5_glu_w"][l] = d_scw, d_rgcw, d_gluw
    ng, ns = S5_GROUPS, S5_STATE
    dbbr = jnp.swapaxes(_blockdiag_extract(dbcat[:, :S5_NSTATE], ng), 1, 2)
    dbbi = jnp.swapaxes(_blockdiag_extract(dbcat[:, S5_NSTATE:], ng), 1, 2)
    d_lr, d_li, d_ls, d_bre, d_bim = p["s5_vjp"]((dar.reshape(ng, ns), dai.reshape(ng, ns), dbbr, dbbi))
    gsmall["s5_lam_re"][l], gsmall["s5_lam_im"][l], gsmall["s5_log_step"][l] = d_lr, d_li, d_ls
    gsmall["s5_b_re"][l], gsmall["s5_b_im"][l] = d_bre, d_bim
    gsmall["s5_c_re"][l] = jnp.swapaxes(_blockdiag_extract(dccat[:S5_NSTATE], ng), 1, 2)
    gsmall["s5_c_im"][l] = -jnp.swapaxes(_blockdiag_extract(dccat[S5_NSTATE:], ng), 1, 2)
    gsmall["s5_d"][l], gsmall["s5_glu_b"][l] = d_s5d[0], d_glub[0]
    gsmall["ssd_conv_b"][l], gsmall["rg_conv_b"][l] = d_scb[0], d_rgcb[0]
    gsmall["ssd_dt_bias"][l], gsmall["ssd_a_log"][l], gsmall["ssd_d"][l] = dprm[0, :8], dprm[1, :8], dprm[2, :8]
    gsmall["ssd_norm_w"][l] = dnw[0]
    gsmall["rg_wa"][l], gsmall["rg_wx"][l] = _blockdiag_extract(dwa, RG_BLOCKS), _blockdiag_extract(dwx, RG_BLOCKS)
    gsmall["rg_ba"][l], gsmall["rg_bx"][l] = dba.reshape(RG_BLOCKS, RG_BLOCK_DIM), dbx.reshape(RG_BLOCKS, RG_BLOCK_DIM)
    gsmall["rg_lambda"][l] = dlam[0]
    for i, (dg, db) in zip((1, 2, 3), ((dg1, db1), (dg2, db2), (dg3, db3))):
        gsmall[f"ln{i}_g"][l], gsmall[f"ln{i}_b"][l] = dg[0], db[0]
    return dh0


def _step(a):
    h = a["x"][0]
    mem = a["mem"][0]
    t = h.shape[0]

    def my_shards(pre):
        return ({name: (jnp.swapaxes(a[pre + name], 1, 2) if tr else a[pre + name]) for name, tr, _ in BIG},
                [a[pre + name] for name, _ in TINY])

    big, tiny = my_shards("")
    tiny16 = [(lax.bitcast_convert_type(w, BF16) if name in KEEP_F32 else w.astype(BF16)).reshape(-1)
              for (name, _), w in zip(TINY, tiny)]
    packed = _pack_wide({name: w.astype(BF16) for name, w in big.items()}, jnp.concatenate(tiny16))
    gbig, gtiny = _unpack_wide(all_gather(packed, name="ag_weights"))
    full = {name: _to_full(gbig[name], 1) for name, _, _ in BIG}
    tiny_shapes = [w.shape + ((2,) if name in KEEP_F32 else ()) for (name, _), w in zip(TINY, tiny)]
    for (name, axis), g in zip(TINY, _split_flat(gtiny, tiny_shapes)):
        full[name] = _to_full(lax.bitcast_convert_type(g, F32) if name in KEEP_F32 else g, axis)
    small = {name: a[name] for name in SMALL}
    params, saved = [], []
    for l in range(DEPTH):
        p = _layer_params(full, small, l)
        h, s = _layer_fwd(h, mem, p)
        params.append(p)
        saved.append(s)
    (dh,), (loss_part,) = rowk(_loss_fn, [(h, D_MODEL, 0), (a["loss_target"][0], D_MODEL, 0)], [], [D_MODEL], [(1, 1)],
                               rows=t, name="loss_head")
    loss = lax.psum(loss_part[0, 0], ("x", "y", "c"))
    gfull = {name: [None] * DEPTH for name in SHARDED}
    gsmall = {name: [None] * DEPTH for name in SMALL}
    for l in reversed(range(DEPTH)):
        dh = _layer_bwd(dh, mem, params[l], saved[l], l, gfull, gsmall)
    grad_x = dh[None]
    gbig = {name: jnp.stack([g.reshape(N_DEV, rows, WIDE) for g in gfull[name]], axis=1) for name, _, rows in BIG}
    gtiny = jnp.concatenate([_to_slabs(jnp.stack(gfull[name]), axis).reshape(N_DEV, -1) for name, axis in TINY], axis=1)
    slabs = _pack_wide(gbig, gtiny)
    halves = jnp.swapaxes(slabs.reshape((4, 2) + slabs.shape[1:]), 0, 1)
    theirs = rs_sibling_exchange(halves, name="rs_sibling")
    slabs = rs_chip_exchange(pair_sum_bf16(halves, theirs, name="rs_pair_sum"), name="rs_chips")

    def pk(pre):
        big, tiny = my_shards(pre)
        return _pack_wide(big, jnp.concatenate([w.reshape(-1) for w in tiny]))

    bigs = adamw(slabs, pk(""), pk("m_"), pk("v_"), name="adamw_sharded", tt=128)
    gs = _pack_rows(jnp.concatenate([jnp.stack(gsmall[name]).reshape(-1) for name in SMALL]), 8)
    gs = all_gather(gs, name="ag_small_grads")
    pks = lambda pre: _pack_rows(jnp.concatenate([a[pre + name].reshape(-1) for name in SMALL]), 8)
    sm = adamw(gs, pks(""), pks("m_"), pks("v_"), name="adamw_replicated", tt=gs.shape[1])
    out = {}
    for kind, bg, sg in zip(("grad_", "delta_", "new_m_", "new_v_"), bigs, sm):
        obig, otiny = _unpack_wide(bg)
        for name, tr, _ in BIG:
            out[kind + name] = jnp.swapaxes(obig[name], 1, 2) if tr else obig[name]
        for (name, _), arr in zip(TINY, _split_flat(otiny, [w.shape for w in tiny])):
            out[kind + name] = arr
        for name, arr in zip(SMALL, _unpack(sg, [a[name].shape for name in SMALL])):
            out[kind + name] = arr
    return (loss, grad_x) + tuple(out[kind + name] for kind in ("grad_", "delta_", "new_m_", "new_v_") for name in WEIGHTS)


def kernel(x, mem, w_in, w_out, ssd_conv_w, ssd_conv_b, ssd_dt_bias, ssd_a_log, ssd_d, ssd_norm_w, s5_lam_re, s5_lam_im, s5_log_step, s5_b_re, s5_b_im, s5_c_re, s5_c_im, s5_d, s5_glu_w, s5_glu_b, rg_conv_w, rg_conv_b, rg_wa, rg_ba, rg_wx, rg_bx, rg_lambda, ln1_g, ln1_b, xa_wq, xa_wk, xa_wv, xa_wo, ln2_g, ln2_b, mlp_w1, mlp_w2, ln3_g, ln3_b, loss_target, m_w_in, m_w_out, m_ssd_conv_w, m_ssd_conv_b, m_ssd_dt_bias, m_ssd_a_log, m_ssd_d, m_ssd_norm_w, m_s5_lam_re, m_s5_lam_im, m_s5_log_step, m_s5_b_re, m_s5_b_im, m_s5_c_re, m_s5_c_im, m_s5_d, m_s5_glu_w, m_s5_glu_b, m_rg_conv_w, m_rg_conv_b, m_rg_wa, m_rg_ba, m_rg_wx, m_rg_bx, m_rg_lambda, m_ln1_g, m_ln1_b, m_xa_wq, m_xa_wk, m_xa_wv, m_xa_wo, m_ln2_g, m_ln2_b, m_mlp_w1, m_mlp_w2, m_ln3_g, m_ln3_b, v_w_in, v_w_out, v_ssd_conv_w, v_ssd_conv_b, v_ssd_dt_bias, v_ssd_a_log, v_ssd_d, v_ssd_norm_w, v_s5_lam_re, v_s5_lam_im, v_s5_log_step, v_s5_b_re, v_s5_b_im, v_s5_c_re, v_s5_c_im, v_s5_d, v_s5_glu_w, v_s5_glu_b, v_rg_conv_w, v_rg_conv_b, v_rg_wa, v_rg_ba, v_rg_wx, v_rg_bx, v_rg_lambda, v_ln1_g, v_ln1_b, v_xa_wq, v_xa_wk, v_xa_wv, v_xa_wo, v_ln2_g, v_ln2_b, v_mlp_w1, v_mlp_w2, v_ln3_g, v_ln3_b):
    return _step(dict(locals()))
```

```python
import math

import jax
import jax.numpy as jnp
from jax import lax
from jax.experimental import pallas as pl
from jax.experimental.pallas import tpu as pltpu

F32 = jnp.float32
BF16 = jnp.bfloat16

N_DEV = 8
D_MODEL = 1024
DEPTH = 2
SSD_WIDTH = 512
SSD_HEADS = 8
SSD_HEAD_DIM = 64
SSD_STATE = 128
SSD_CHUNK = 128
SSD_XBC = 1024
S5_WIDTH = 256
S5_GROUPS = 16
S5_GROUP_CH = 16
S5_STATE = 64
S5_NSTATE = S5_GROUPS * S5_STATE
RG_WIDTH = 256
RG_BLOCKS = 4
RG_BLOCK_DIM = 64
RG_C = 8.0
XA_HEADS = 4
XA_HEAD_DIM = 256
ALPHA = (2.0 * DEPTH) ** 0.25
LN_EPS = 1e-5
ADAM_LR, ADAM_B1, ADAM_B2, ADAM_EPS, ADAM_WD, ADAM_STEP = 0.001, 0.9, 0.999, 1e-08, 0.01, 10

P_XBC, P_Z, P_U, P_XR, P_G, P_DT = 0, 1024, 1536, 1792, 2048, 2304
D_INP = 2560
LANE = 128
VMEM_LIMIT = 56 * 1024 * 1024
ROW_TILE = 512

_NN = ((1,), (0,))
_NT = ((1,), (1,))
_TN = ((0,), (0,))


def _dot(a, b, dims=_NN):
    return lax.dot_general(a.astype(BF16), b.astype(BF16), (dims, ((), ())), preferred_element_type=F32)


def _split_bf16(x, parts):
    out, rem = [], x
    for _ in range(parts):
        piece = rem.astype(BF16)
        out.append(piece)
        rem = rem - piece.astype(F32)
    return out


def _dot_mask(a, b, dims=_NN, *, mask_left, parts):
    if mask_left:
        return sum(_dot(a, piece, dims) for piece in _split_bf16(b, parts))
    return sum(_dot(piece, b, dims) for piece in _split_bf16(a, parts))


def _sigmoid(x):
    return 1.0 / (1.0 + jnp.exp(-x))


def _silu(x):
    return x * _sigmoid(x)


def _dsilu(x):
    s = _sigmoid(x)
    return s * (1.0 + x * (1.0 - s))


_GK = math.sqrt(2.0 / math.pi)
_GC = 0.044715


def _gelu(x):
    return 0.5 * x * (1.0 + jnp.tanh(_GK * (x + _GC * x * x * x)))


def _dgelu(x):
    th = jnp.tanh(_GK * (x + _GC * x * x * x))
    return 0.5 * (1.0 + th) + 0.5 * x * (1.0 - th * th) * _GK * (1.0 + 3.0 * _GC * x * x)


def _log1p_pos(e):
    return jnp.where(e < 1e-2, e * (1.0 - e * (0.5 - e * (1.0 / 3.0))), jnp.log(1.0 + e))


def _softplus(x):
    return jnp.maximum(x, 0.0) + _log1p_pos(jnp.exp(-jnp.abs(x)))


def _neg_expm1(x):
    poly = -x * (1.0 + x * (0.5 + x * (1.0 / 6.0 + x * (1.0 / 24.0 + x * (1.0 / 120.0)))))
    return jnp.where(x > -0.05, poly, 1.0 - jnp.exp(x))


def _params(sem):
    return pltpu.CompilerParams(dimension_semantics=sem, vmem_limit_bytes=VMEM_LIMIT)


RESIDENT_BYTES = 8 * 1024 * 1024
STREAM_BYTES = 4 * 1024 * 1024


def _halve_to_fit(dims, bytes_per, limit):
    dims = list(dims)
    while math.prod(dims) * bytes_per > limit:
        i = max(range(len(dims)), key=lambda d: dims[d])
        assert dims[i] % 256 == 0, dims
        dims[i] //= 2
    return dims


def mm(a, b, *, name, ta=False, tb=False, a_extra=(), fa=None, o_extra=(), fo=None, a_off=0, m=None, k=None):
    n = b.shape[0] if tb else b.shape[1]
    na, no = 1 + len(a_extra), len(o_extra)
    if not ta:
        assert m is None
        m, kdim = a.shape[0], (a.shape[1] if k is None else k)
        assert a_off % kdim == 0
        (tn,) = _halve_to_fit([n], kdim * b.dtype.itemsize, RESIDENT_BYTES)
        (tm,) = _halve_to_fit([min(512, m)], max(tn, kdim) * 4, STREAM_BYTES)
        a_spec = pl.BlockSpec((tm, kdim), lambda i, j: (i, a_off // kdim))
        b_spec = pl.BlockSpec((tn, kdim), lambda i, j: (j, 0)) if tb else pl.BlockSpec((kdim, tn), lambda i, j: (0, j))
        o_spec = pl.BlockSpec((tm, tn), lambda i, j: (i, j))
        dims = _NT if tb else _NN

        def body(*refs):
            a_refs, b_ref, o_refs, out_ref = refs[:na], refs[na], refs[na + 1:na + 1 + no], refs[na + 1 + no]
            av = a_refs[0][...] if fa is None else fa(*[r[...] for r in a_refs])
            acc = _dot(av, b_ref[...], dims)
            out_ref[...] = acc if fo is None else fo(acc, *[r[...] for r in o_refs])

        grid, sem = (m // tm, n // tn), ("parallel", "parallel")
    else:
        assert k is None and not tb and fo is None and not o_extra
        kdim, m = a.shape[0], (a.shape[1] if m is None else m)
        tm, tn = _halve_to_fit([m, n], 4, RESIDENT_BYTES)
        (tk,) = _halve_to_fit([min(512, kdim)], max(tm, tn) * 4, STREAM_BYTES)
        assert a_off % tm == 0
        a_spec = pl.BlockSpec((tk, tm), lambda i, j, kk: (kk, i + a_off // tm))
        b_spec = pl.BlockSpec((tk, tn), lambda i, j, kk: (kk, j))
        o_spec = pl.BlockSpec((tm, tn), lambda i, j, kk: (i, j))

        def body(*refs):
            a_refs, b_ref, out_ref = refs[:na], refs[na], refs[na + 1]

            @pl.when(pl.program_id(2) == 0)
            def _():
                out_ref[...] = jnp.zeros_like(out_ref)

            av = a_refs[0][...] if fa is None else fa(*[r[...] for r in a_refs])
            out_ref[...] += _dot(av, b_ref[...], _TN)

        grid, sem = (m // tm, n // tn, kdim // tk), ("parallel", "parallel", "arbitrary")
    assert m % tm == 0 and n % tn == 0, (name, m, n, tm, tn)
    return pl.pallas_call(
        body, name=name, grid=grid,
        in_specs=[a_spec] * na + [b_spec] + [o_spec] * no,
        out_specs=o_spec, out_shape=jax.ShapeDtypeStruct((m, n), F32),
        compiler_params=_params(sem),
    )(a, *a_extra, b, *o_extra)


def rowk(fn, tiled, full, out_w, acc_shapes, *, rows, name):
    tt = min(ROW_TILE, rows)
    n = rows // tt
    assert rows % tt == 0
    nt, nf, no = len(tiled), len(full), len(out_w)

    def tspec(w, cb):
        return pl.BlockSpec((tt, w), lambda i: (i, cb))

    def fspec(a):
        nd = a.ndim
        return pl.BlockSpec(a.shape, lambda i: (0,) * nd)

    def body(*refs):
        ins, fulls = refs[:nt], refs[nt:nt + nf]
        outs, accs = refs[nt + nf:nt + nf + no], refs[nt + nf + no:]
        res_t, res_a = fn(*[r[...] for r in ins], *[r[...] for r in fulls])
        for r, v in zip(outs, res_t):
            r[...] = v
        if accs:
            @pl.when(pl.program_id(0) == 0)
            def _():
                for r in accs:
                    r[...] = jnp.zeros_like(r)
            for r, v in zip(accs, res_a):
                r[...] += v

    outs = pl.pallas_call(
        body, name=name, grid=(n,),
        in_specs=[tspec(w, cb) for (_, w, cb) in tiled] + [fspec(a) for a in full],
        out_specs=[tspec(w, 0) for w in out_w] + [pl.BlockSpec(s, lambda i, nd=len(s): (0,) * nd) for s in acc_shapes],
        out_shape=[jax.ShapeDtypeStruct((rows, w), F32) for w in out_w] + [jax.ShapeDtypeStruct(s, F32) for s in acc_shapes],
        compiler_params=_params(("arbitrary",)),
    )(*[a for (a, _, _) in tiled], *full)
    return outs[:no], outs[no:]


def _colsum(x):
    return jnp.sum(x, axis=0, keepdims=True)


def _rowsum(x):
    return jnp.sum(x, axis=1, keepdims=True)


def _ln_fwd_fn(resid, y, g, b):
    pre = ALPHA * resid + y
    mu = jnp.mean(pre, axis=1, keepdims=True)
    xc = pre - mu
    var = jnp.mean(xc * xc, axis=1, keepdims=True)
    return (xc * lax.rsqrt(var + LN_EPS) * g + b,), ()


def _ln_bwd_fn(resid, y, dout, g):
    pre = ALPHA * resid + y
    mu = jnp.mean(pre, axis=1, keepdims=True)
    xc = pre - mu
    var = jnp.mean(xc * xc, axis=1, keepdims=True)
    rstd = lax.rsqrt(var + LN_EPS)
    xhat = xc * rstd
    dxh = dout * g
    dpre = rstd * (dxh - jnp.mean(dxh, axis=1, keepdims=True) - xhat * jnp.mean(dxh * xhat, axis=1, keepdims=True))
    return (dpre,), (_colsum(dout * xhat), _colsum(dout))


def ln_fwd(resid, y, g, b, *, name):
    (out,), _ = rowk(_ln_fwd_fn, [(resid, D_MODEL, 0), (y, D_MODEL, 0)], [g, b], [D_MODEL], [],
                     rows=resid.shape[0], name=name)
    return out


def ln_bwd(resid, y, dout, g, *, name):
    (dpre,), (dg, db) = rowk(_ln_bwd_fn, [(resid, D_MODEL, 0), (y, D_MODEL, 0), (dout, D_MODEL, 0)], [g],
                             [D_MODEL], [(1, D_MODEL), (1, D_MODEL)], rows=resid.shape[0], name=name)
    return dpre, dg, db


def _loss_fn(y, tgt):
    e = y - tgt
    part = _colsum(_rowsum(e * e)) * (0.5 / D_MODEL)
    return (e * (1.0 / D_MODEL),), (part,)


_XA_SCALE = 1.0 / math.sqrt(XA_HEAD_DIM)


def _attn_probs(qh, kh):
    s = _dot(qh, kh, _NT) * _XA_SCALE
    e = jnp.exp(s - jnp.max(s, axis=1, keepdims=True))
    return e / _rowsum(e)


def _attn_fwd_fn(q, k, v):
    outs = []
    for hd in range(XA_HEADS):
        sl = slice(hd * XA_HEAD_DIM, (hd + 1) * XA_HEAD_DIM)
        outs.append(_dot(_attn_probs(q[:, sl], k[:, sl]), v[:, sl]))
    return (jnp.concatenate(outs, axis=1),), ()


def _attn_bwd_fn(q, do, k, v):
    dqs, dks, dvs = [], [], []
    for hd in range(XA_HEADS):
        sl = slice(hd * XA_HEAD_DIM, (hd + 1) * XA_HEAD_DIM)
        qh, kh, vh, doh = q[:, sl], k[:, sl], v[:, sl], do[:, sl]
        p = _attn_probs(qh, kh)
        dp = _dot(doh, vh, _NT)
        ds = p * (dp - _rowsum(p * dp)) * _XA_SCALE
        dqs.append(_dot(ds, kh))
        dks.append(_dot(ds, qh, _TN))
        dvs.append(_dot(p, doh, _TN))
    cat = lambda xs: jnp.concatenate(xs, axis=1)
    return (cat(dqs),), (cat(dks), cat(dvs))


def _s5_post_fwd_fn(ylin, u, dskip, gw, gb):
    yg = _gelu(ylin + dskip * u)
    return (yg * _sigmoid(_dot(yg, gw) + gb),), ()


def _s5_post_bwd_fn(ylin, u, dout, dskip, gw, gb):
    pre = ylin + dskip * u
    yg = _gelu(pre)
    sg = _sigmoid(_dot(yg, gw) + gb)
    dlin = dout * yg * sg * (1.0 - sg)
    dyg = dout * sg + _dot(dlin, gw, _NT)
    dpre = dyg * _dgelu(pre)
    return (dpre, dpre * dskip), (_colsum(dpre * u), _dot(yg, dlin, _TN), _colsum(dlin))


def _rg_gates(xc, wa, wx, ba, bx, lam):
    r = _sigmoid(_dot(xc, wa) + ba)
    i = _sigmoid(_dot(xc, wx) + bx)
    sp = _softplus(-lam)
    log_a = -RG_C * r * sp
    a = jnp.exp(log_a)
    mult = jnp.sqrt(_neg_expm1(2.0 * log_a))
    return r, i, sp, a, mult


def _rg_pre_fwd_fn(xc, wa, wx, ba, bx, lam):
    r, i, sp, a, mult = _rg_gates(xc, wa, wx, ba, bx, lam)
    return (a, mult * (i * xc)), ()


def _rg_pre_bwd_fn(xc, gsc, hprev, wa, wx, ba, bx, lam):
    r, i, sp, a, mult = _rg_gates(xc, wa, wx, ba, bx, lam)
    da = gsc * hprev
    db = gsc
    dmult = db * i * xc
    di = db * mult * xc
    dxc = db * mult * i
    dlog_a = da * a - a * a * dmult / mult
    dr = dlog_a * (-RG_C * sp)
    dsp = _colsum(dlog_a * (-RG_C * r))
    dlam = dsp * (-_sigmoid(-lam))
    dpr = dr * r * (1.0 - r)
    dpi = di * i * (1.0 - i)
    dxc = dxc + _dot(dpr, wa, _NT) + _dot(dpi, wx, _NT)
    return (dxc,), (_dot(xc, dpr, _TN), _dot(xc, dpi, _TN), _colsum(dpr), _colsum(dpi), dlam)


def _rg_out_fwd_fn(h, g):
    return (h * _gelu(g),), ()


def _rg_out_bwd_fn(h, g, dy):
    return (dy * _gelu(g), dy * h * _dgelu(g)), ()


def _shift_down(x, prev, j, rows):
    return jnp.where(rows < j, pltpu.roll(prev, j, 0), pltpu.roll(x, j, 0))


def _shift_up(x, nxt, j, rows):
    t = x.shape[0]
    return jnp.where(rows >= t - j, pltpu.roll(nxt, t - j, 0), pltpu.roll(x, t - j, 0))


def conv_fwd(src, cb, w, b, *, width, act, name):
    t = src.shape[0]
    tt = min(ROW_TILE, t)
    n = t // tt

    def body(x_ref, w_ref, b_ref, y_ref, prev_ref):
        @pl.when(pl.program_id(0) == 0)
        def _():
            prev_ref[...] = jnp.zeros_like(prev_ref)

        x = x_ref[...]
        prev = prev_ref[...]
        rows = lax.broadcasted_iota(jnp.int32, x.shape, 0)
        wv = w_ref[...]
        y = b_ref[...] + wv[3:4, :] * x
        for j in (1, 2, 3):
            y = y + wv[3 - j:4 - j, :] * _shift_down(x, prev, j, rows)
        y_ref[...] = _silu(y) if act else y
        prev_ref[...] = x

    return pl.pallas_call(
        body, name=name, grid=(n,),
        in_specs=[pl.BlockSpec((tt, width), lambda i: (i, cb)),
                  pl.BlockSpec((4, width), lambda i: (0, 0)), pl.BlockSpec((1, width), lambda i: (0, 0))],
        out_specs=pl.BlockSpec((tt, width), lambda i: (i, 0)),
        out_shape=jax.ShapeDtypeStruct((t, width), F32),
        scratch_shapes=[pltpu.VMEM((tt, width), F32)],
        compiler_params=_params(("arbitrary",)),
    )(src, w, b)


def conv_bwd(src, cb, dy, w, b, *, width, act, name):
    t = src.shape[0]
    tt = min(ROW_TILE, t)
    n = t // tt

    def body(x_ref, xp_ref, dy_ref, w_ref, b_ref, dx_ref, dw_ref, db_ref, nxt_ref):
        i = pl.program_id(0)

        @pl.when(i == 0)
        def _():
            nxt_ref[...] = jnp.zeros_like(nxt_ref)
            dw_ref[...] = jnp.zeros_like(dw_ref)
            db_ref[...] = jnp.zeros_like(db_ref)

        x = x_ref[...]
        prev = jnp.where(i == n - 1, 0.0, xp_ref[...])
        rows = lax.broadcasted_iota(jnp.int32, x.shape, 0)
        wv = w_ref[...]
        xs = [x] + [_shift_down(x, prev, j, rows) for j in (1, 2, 3)]
        dpre = dy_ref[...]
        if act:
            pre = b_ref[...] + wv[3:4, :] * xs[0]
            for j in (1, 2, 3):
                pre = pre + wv[3 - j:4 - j, :] * xs[j]
            dpre = dpre * _dsilu(pre)
        nxt = nxt_ref[...]
        dx = wv[3:4, :] * dpre
        for j in (1, 2, 3):
            dx = dx + wv[3 - j:4 - j, :] * _shift_up(dpre, nxt, j, rows)
        dx_ref[...] = dx
        dw_ref[...] += jnp.concatenate([_colsum(dpre * xs[3 - kk]) for kk in range(4)], axis=0)
        db_ref[...] += _colsum(dpre)
        nxt_ref[...] = dpre

    return pl.pallas_call(
        body, name=name, grid=(n,),
        in_specs=[pl.BlockSpec((tt, width), lambda i: (n - 1 - i, cb)),
                  pl.BlockSpec((tt, width), lambda i: (jnp.maximum(n - 2 - i, 0), cb)),
                  pl.BlockSpec((tt, width), lambda i: (n - 1 - i, 0)),
                  pl.BlockSpec((4, width), lambda i: (0, 0)), pl.BlockSpec((1, width), lambda i: (0, 0))],
        out_specs=[pl.BlockSpec((tt, width), lambda i: (n - 1 - i, 0)),
                   pl.BlockSpec((4, width), lambda i: (0, 0)), pl.BlockSpec((1, width), lambda i: (0, 0))],
        out_shape=[jax.ShapeDtypeStruct((t, width), F32), jax.ShapeDtypeStruct((4, width), F32),
                   jax.ShapeDtypeStruct((1, width), F32)],
        scratch_shapes=[pltpu.VMEM((tt, width), F32)],
        compiler_params=_params(("arbitrary",)),
    )(src, src, dy, w, b)


SCAN_CW = 512


def scan_complex(bu, lam, *, reverse, name):
    t, w2 = bu.shape
    w = w2 // 2
    tt = min(ROW_TILE, t)
    n, nb, cw = t // tt, tt // 8, min(SCAN_CW, w)

    def body(b_ref, lam_ref, o_ref, st_ref):
        @pl.when(pl.program_id(0) == 0)
        def _():
            st_ref[...] = jnp.zeros_like(st_ref)

        rows = lax.broadcasted_iota(jnp.int32, (8, cw), 0)
        for c0 in range(0, w, cw):
            re, im = pl.ds(c0, cw), pl.ds(w + c0, cw)
            ar = jnp.broadcast_to(lam_ref[:, re], (8, cw))
            ai = jnp.broadcast_to(lam_ref[:, im], (8, cw))

            def blk(i, carry):
                hr, hi = carry
                base = pl.multiple_of((nb - 1 - i if reverse else i) * 8, 8)
                tr, ti = b_ref[pl.ds(base, 8), re], b_ref[pl.ds(base, 8), im]
                outr, outi = jnp.zeros((8, cw), F32), jnp.zeros((8, cw), F32)
                for j in (range(7, -1, -1) if reverse else range(8)):
                    br = jnp.broadcast_to(tr[j:j + 1, :], (8, cw))
                    bi = jnp.broadcast_to(ti[j:j + 1, :], (8, cw))
                    hr, hi = ar * hr - ai * hi + br, ar * hi + ai * hr + bi
                    outr = jnp.where(rows == j, hr, outr)
                    outi = jnp.where(rows == j, hi, outi)
                o_ref[pl.ds(base, 8), re] = outr
                o_ref[pl.ds(base, 8), im] = outi
                return hr, hi

            hr, hi = lax.fori_loop(0, nb, blk, (st_ref[:, re], st_ref[:, im]))
            st_ref[:, re] = hr
            st_ref[:, im] = hi

    idx = (lambda i: (n - 1 - i, 0)) if reverse else (lambda i: (i, 0))
    return pl.pallas_call(
        body, name=name, grid=(n,),
        in_specs=[pl.BlockSpec((tt, w2), idx), pl.BlockSpec((1, w2), lambda i: (0, 0))],
        out_specs=pl.BlockSpec((tt, w2), idx), out_shape=jax.ShapeDtypeStruct((t, w2), F32),
        scratch_shapes=[pltpu.VMEM((8, w2), F32)],
        compiler_params=_params(("arbitrary",)),
    )(bu, lam)


def scan_real(a, b, *, reverse, name):
    t, w = b.shape
    tt = min(ROW_TILE, t)
    n, nb = t // tt, tt // 8

    def body(a_ref, b_ref, o_ref, st_ref):
        @pl.when(pl.program_id(0) == 0)
        def _():
            st_ref[...] = jnp.zeros_like(st_ref)

        rows = lax.broadcasted_iota(jnp.int32, (8, w), 0)

        def blk(i, h):
            base = pl.multiple_of((nb - 1 - i if reverse else i) * 8, 8)
            ta_, tb_ = a_ref[pl.ds(base, 8), :], b_ref[pl.ds(base, 8), :]
            out = jnp.zeros((8, w), F32)
            for j in (range(7, -1, -1) if reverse else range(8)):
                h = jnp.broadcast_to(ta_[j:j + 1, :], (8, w)) * h + jnp.broadcast_to(tb_[j:j + 1, :], (8, w))
                out = jnp.where(rows == j, h, out)
            o_ref[pl.ds(base, 8), :] = out
            return h

        st_ref[...] = lax.fori_loop(0, nb, blk, st_ref[...])

    idx = (lambda i: (n - 1 - i, 0)) if reverse else (lambda i: (i, 0))
    return pl.pallas_call(
        body, name=name, grid=(n,),
        in_specs=[pl.BlockSpec((tt, w), idx), pl.BlockSpec((tt, w), idx)],
        out_specs=pl.BlockSpec((tt, w), idx), out_shape=jax.ShapeDtypeStruct((t, w), F32),
        scratch_shapes=[pltpu.VMEM((8, w), F32)],
        compiler_params=_params(("arbitrary",)),
    )(a, b)


def s5_dlam(g, hprev, *, name):
    t, w2 = g.shape
    w = w2 // 2

    def fn(gt, ht):
        gr, gi, hr, hi = gt[:, :w], gt[:, w:], ht[:, :w], ht[:, w:]
        return (), (_colsum(gr * hr + gi * hi), _colsum(gi * hr - gr * hi))

    _, (dar, dai) = rowk(fn, [(g, w2, 0), (hprev, w2, 0)], [], [], [(1, w), (1, w)], rows=t, name=name)
    return dar, dai


SSD_QQ = SSD_HEADS * SSD_CHUNK
SSD_GP = SSD_WIDTH // 2
SSD_GQ = SSD_QQ // 2


def _ssd_spread():
    h = jnp.arange(LANE)[:, None]
    spread_p = (jnp.arange(SSD_WIDTH)[None, :] // SSD_HEAD_DIM == h).astype(BF16)
    spread_q = (jnp.arange(SSD_QQ)[None, :] // SSD_CHUNK == h).astype(BF16)
    return spread_p, spread_q


def _ssd_prologue(dt_ref, prow_ref, sp_ref, sq_ref):
    q = SSD_CHUNK
    r = lax.broadcasted_iota(jnp.int32, (q, q), 0)
    c = lax.broadcasted_iota(jnp.int32, (q, q), 1)
    raw_c = dt_ref[...] + prow_ref[0:1, :]
    dt_c = _softplus(raw_c)
    a_r = -jnp.exp(prow_ref[1:2, :])
    cs_c = _dot_mask((r >= c).astype(F32), dt_c * a_r, mask_left=True, parts=3)
    both = _dot_mask(jnp.concatenate([dt_c, cs_c], axis=0), sp_ref[...], mask_left=False, parts=3)
    dt_x, cs_x = both[:q], both[q:]
    csx = _dot_mask(cs_c, sq_ref[...], mask_left=False, parts=3)
    rr = lax.broadcasted_iota(jnp.int32, (q, SSD_QQ), 0)
    ss = lax.broadcasted_iota(jnp.int32, (q, SSD_QQ), 1) & (q - 1)
    diag = rr == ss
    cs_row = _colsum(jnp.where(diag, csx, 0.0))
    lcat = jnp.exp(jnp.where(rr >= ss, csx - cs_row, -1e30))
    cl = cs_x[q - 1:q, :]
    return dict(raw_c=raw_c, dt_c=dt_c, a_r=a_r, dt_x=dt_x, cs_x=cs_x, lcat=lcat, diag=diag,
                ecs=jnp.exp(cs_x), wdec=jnp.exp(cl - cs_x), ecl=jnp.exp(cl), triu=(r <= c).astype(F32))


def _ssd_group(xbc_ref, g, lcat, xdt):
    ns, q = SSD_STATE, SSD_CHUNK
    bm = xbc_ref[:, pl.ds(SSD_WIDTH + g * ns, ns)]
    cm = xbc_ref[:, pl.ds(SSD_WIDTH + 2 * ns + g * ns, ns)]
    cb = _dot(cm, bm, _NT)
    lg = lcat[:, g * SSD_GQ:(g + 1) * SSD_GQ]
    wcat = jnp.concatenate([cb] * 4, axis=1) * lg
    head = lax.broadcasted_iota(jnp.int32, (1, SSD_GP), 1) // SSD_HEAD_DIM
    xg = xdt[:, g * SSD_GP:(g + 1) * SSD_GP]
    xbd = jnp.concatenate([jnp.where(head == j, xg, 0.0) for j in range(4)], axis=0)
    return bm, cm, lg, wcat, xbd, head


def _ssd_gate(yraw, z, nw):
    yg = yraw * _silu(z)
    r = lax.rsqrt(jnp.mean(yg * yg, axis=1, keepdims=True) + LN_EPS)
    return yg, r


def _ssd_specs(q, idx):
    return [pl.BlockSpec((q, SSD_XBC), lambda i: (idx(i), 0)),
            pl.BlockSpec((q, SSD_WIDTH), lambda i: (idx(i), P_Z // SSD_WIDTH)),
            pl.BlockSpec((q, LANE), lambda i: (idx(i), P_DT // LANE)),
            pl.BlockSpec((8, LANE), lambda i: (0, 0)), pl.BlockSpec((1, SSD_WIDTH), lambda i: (0, 0)),
            pl.BlockSpec((1, SSD_WIDTH), lambda i: (0, 0)),
            pl.BlockSpec((LANE, SSD_WIDTH), lambda i: (0, 0)), pl.BlockSpec((LANE, SSD_QQ), lambda i: (0, 0))]


def ssd_fwd(xbc, proj, prow, d_x, nw, *, name):
    t = xbc.shape[0]
    q, ns = SSD_CHUNK, SSD_STATE
    nc = t // q
    spread_p, spread_q = _ssd_spread()

    def body(xbc_ref, z_ref, dt_ref, prow_ref, dx_ref, nw_ref, sp_ref, sq_ref, y_ref, yraw_ref, sall_ref, s_ref):
        @pl.when(pl.program_id(0) == 0)
        def _():
            s_ref[...] = jnp.zeros_like(s_ref)

        sall_ref[0] = s_ref[...]
        pr = _ssd_prologue(dt_ref, prow_ref, sp_ref, sq_ref)
        xs = xbc_ref[:, pl.ds(0, SSD_WIDTH)]
        xdt = xs * pr["dt_x"]
        xw = xdt * pr["wdec"]
        ys = []
        for g in range(2):
            gp = slice(g * SSD_GP, (g + 1) * SSD_GP)
            bm, cm, lg, wcat, xbd, head = _ssd_group(xbc_ref, g, pr["lcat"], xdt)
            st = s_ref[:, gp]
            ys.append(_dot(wcat, xbd) + pr["ecs"][:, gp] * _dot(cm, st) + xs[:, gp] * dx_ref[:, gp])
            s_ref[:, gp] = pr["ecl"][:, gp] * st + _dot(bm, xw[:, gp], _TN)
        yraw = jnp.concatenate(ys, axis=1)
        yraw_ref[...] = yraw
        yg, r = _ssd_gate(yraw, z_ref[...], nw_ref[...])
        y_ref[...] = yg * r * nw_ref[...]

    row = pl.BlockSpec((q, SSD_WIDTH), lambda i: (i, 0))
    return pl.pallas_call(
        body, name=name, grid=(nc,),
        in_specs=_ssd_specs(q, lambda i: i),
        out_specs=[row, row, pl.BlockSpec((1, ns, SSD_WIDTH), lambda i: (i, 0, 0))],
        out_shape=[jax.ShapeDtypeStruct((t, SSD_WIDTH), F32), jax.ShapeDtypeStruct((t, SSD_WIDTH), F32),
                   jax.ShapeDtypeStruct((nc, ns, SSD_WIDTH), F32)],
        scratch_shapes=[pltpu.VMEM((ns, SSD_WIDTH), F32)],
        compiler_params=_params(("arbitrary",)),
    )(xbc, proj, proj, prow, d_x, nw, spread_p, spread_q)


def ssd_bwd(xbc, proj, prow, d_x, nw, yraw, sall, dout, *, name):
    t = xbc.shape[0]
    q, ns = SSD_CHUNK, SSD_STATE
    nc = t // q
    spread_p, spread_q = _ssd_spread()

    def body(xbc_ref, z_ref, dt_ref, prow_ref, dx_ref, nw_ref, sp_ref, sq_ref, yraw_ref, sall_ref, dout_ref,
             dxbc_ref, dz_ref, ddt_ref, dprm_ref, ddx_ref, dnw_ref, ds_ref):
        @pl.when(pl.program_id(0) == 0)
        def _():
            ds_ref[...] = jnp.zeros_like(ds_ref)
            dprm_ref[...] = jnp.zeros_like(dprm_ref)
            ddx_ref[...] = jnp.zeros_like(ddx_ref)
            dnw_ref[...] = jnp.zeros_like(dnw_ref)

        yraw, z, nwv, dout = yraw_ref[...], z_ref[...], nw_ref[...], dout_ref[...]
        yg, r = _ssd_gate(yraw, z, nwv)
        dnw_ref[...] += _colsum(dout * yg * r)
        dyn = dout * nwv
        dyg = r * dyn - yg * (r * r * r) * jnp.mean(dyn * yg, axis=1, keepdims=True)
        dy = dyg * _silu(z)
        dz_ref[...] = dyg * yraw * _dsilu(z)

        pr = _ssd_prologue(dt_ref, prow_ref, sp_ref, sq_ref)
        xs = xbc_ref[:, pl.ds(0, SSD_WIDTH)]
        xdt = xs * pr["dt_x"]
        wdec, ecl = pr["wdec"], pr["ecl"]
        xw = xdt * wdec
        dzm_all = pr["ecs"] * dy
        last = (lax.broadcasted_iota(jnp.int32, (q, 1), 0) == q - 1).astype(F32)
        dxs, dcsxs, es = [], [], []
        for g in range(2):
            gp = slice(g * SSD_GP, (g + 1) * SSD_GP)
            bm, cm, lg, wcat, xbd, head = _ssd_group(xbc_ref, g, pr["lcat"], xdt)
            dyg_ = dy[:, gp]
            dwcat = _dot(dyg_, xbd, _NT)
            dxbd = _dot(wcat, dyg_, _TN)
            dxg = sum(jnp.where(head == j, dxbd[j * q:(j + 1) * q], 0.0) for j in range(4))
            es.append(dwcat * wcat)
            dmm = dwcat * lg
            dm = dmm[:, 0:q] + dmm[:, q:2 * q] + dmm[:, 2 * q:3 * q] + dmm[:, 3 * q:4 * q]
            dcm = _dot(dm, bm)
            dbm = _dot(dm, cm, _TN)
            st = sall_ref[0, :, gp]
            zmat = _dot(cm, st)
            dzm = dzm_all[:, gp]
            dcm = dcm + _dot(dzm, st, _NT)
            dst = _dot(cm, dzm, _TN)
            dcsx = dzm * zmat
            dsn = ds_ref[:, gp]
            dst = dst + ecl[:, gp] * dsn
            dclx = _colsum(dsn * st) * ecl[:, gp]
            dxw = _dot(bm, dsn)
            dbm = dbm + _dot(xw[:, gp], dsn, _NT)
            dxg = dxg + wdec[:, gp] * dxw
            tw = dxw * xdt[:, gp] * wdec[:, gp]
            dclx = dclx + _colsum(tw)
            dcsxs.append(dcsx - tw + last * dclx)
            ds_ref[:, gp] = dst
            dxs.append(dxg)
            dxbc_ref[:, pl.ds(SSD_WIDTH + g * ns, ns)] = dbm
            dxbc_ref[:, pl.ds(SSD_WIDTH + 2 * ns + g * ns, ns)] = dcm
        dx = jnp.concatenate(dxs, axis=1)
        dxbc_ref[:, pl.ds(0, SSD_WIDTH)] = dx * pr["dt_x"] + dy * dx_ref[...]
        ddx_ref[...] += _colsum(dy * xs)
        red = _dot_mask(jnp.concatenate([jnp.concatenate(dcsxs, axis=1), dx * xs], axis=0), sp_ref[...], _NT,
                        mask_left=False, parts=2)
        e_all = jnp.concatenate(es, axis=1)
        e_red = _dot_mask(e_all - jnp.where(pr["diag"], _colsum(e_all), 0.0), sq_ref[...], _NT, mask_left=False, parts=2)
        dadt = _dot_mask(pr["triu"], red[:q] + e_red, mask_left=True, parts=2)
        draw = (red[q:] + dadt * pr["a_r"]) * _sigmoid(pr["raw_c"])
        ddt_ref[...] = draw
        zero = jnp.zeros((6, LANE), F32)
        dprm_ref[...] += jnp.concatenate([_colsum(draw), _colsum(dadt * pr["dt_c"]) * pr["a_r"], zero], axis=0)

    rev = lambda i: nc - 1 - i
    row = lambda w: pl.BlockSpec((q, w), lambda i: (rev(i), 0))
    fixed = lambda shape: pl.BlockSpec(shape, lambda i: (0, 0))
    return pl.pallas_call(
        body, name=name, grid=(nc,),
        in_specs=_ssd_specs(q, rev) + [row(SSD_WIDTH), pl.BlockSpec((1, ns, SSD_WIDTH), lambda i: (rev(i), 0, 0)),
                                       row(SSD_WIDTH)],
        out_specs=[row(SSD_XBC), row(SSD_WIDTH), row(LANE), fixed((8, LANE)), fixed((1, SSD_WIDTH)), fixed((1, SSD_WIDTH))],
        out_shape=[jax.ShapeDtypeStruct((t, SSD_XBC), F32), jax.ShapeDtypeStruct((t, SSD_WIDTH), F32),
                   jax.ShapeDtypeStruct((t, LANE), F32), jax.ShapeDtypeStruct((8, LANE), F32),
                   jax.ShapeDtypeStruct((1, SSD_WIDTH), F32), jax.ShapeDtypeStruct((1, SSD_WIDTH), F32)],
        scratch_shapes=[pltpu.VMEM((ns, SSD_WIDTH), F32)],
        compiler_params=_params(("arbitrary",)),
    )(xbc, proj, proj, prow, d_x, nw, spread_p, spread_q, yraw, sall, dout)


def _me():
    return lax.axis_index("x"), lax.axis_index("y"), lax.axis_index("c")


_ANY = pl.BlockSpec(memory_space=pl.ANY)
_MESH = pl.DeviceIdType.MESH


def all_gather(block, *, name):
    def body(src, dst, send_sems, recv_sems, local_sem):
        x, y, c = _me()
        me, sibling = (x, y, c), (x, y, 1 - c)
        chips = [(1 - x, y), (x, 1 - y), (1 - x, 1 - y)]

        def slot(px, py, pc):
            return dst.at[4 * px + 2 * py + pc]

        def copy(kk, blk, to, from_src=False):
            return pltpu.make_async_remote_copy(
                src_ref=src if from_src else slot(*blk), dst_ref=slot(*blk),
                send_sem=send_sems.at[kk], recv_sem=recv_sems.at[kk], device_id=to, device_id_type=_MESH)

        mine = pltpu.make_async_copy(src, slot(*me), local_sem)
        mine.start()
        first = [copy(0, me, sibling, True)] + [copy(1 + j, me, (*chip, c), True) for j, chip in enumerate(chips)]
        for cp in first:
            cp.start()
        passed = [copy(4 + j, (*chip, c), sibling) for j, chip in enumerate(chips)]
        for j, chip in enumerate(chips):
            copy(1 + j, (*chip, c), me).wait_recv()
            passed[j].start()
        copy(0, sibling, me).wait_recv()
        for j, chip in enumerate(chips):
            copy(4 + j, (*chip, 1 - c), me).wait_recv()
        for cp in first + passed:
            cp.wait_send()
        mine.wait()

    return pl.pallas_call(
        body, name=name, in_specs=[_ANY], out_specs=_ANY,
        out_shape=jax.ShapeDtypeStruct((N_DEV,) + block.shape, block.dtype),
        scratch_shapes=[pltpu.SemaphoreType.DMA((7,)), pltpu.SemaphoreType.DMA((7,)), pltpu.SemaphoreType.DMA(())],
    )(block)


RS_PIECES = 4


def rs_sibling_exchange(halves, *, name):
    _, nq, r, l = halves.shape
    rows = r // RS_PIECES
    assert r % RS_PIECES == 0 and rows % 16 == 0

    def body(src, dst, send_sems, recv_sems):
        x, y, c = _me()
        copies = []
        for q in range(nq):
            for i in range(RS_PIECES):
                kk = q * RS_PIECES + i
                cp = pltpu.make_async_remote_copy(
                    src_ref=src.at[1 - c, q, pl.ds(i * rows, rows)], dst_ref=dst.at[q, pl.ds(i * rows, rows)],
                    send_sem=send_sems.at[kk], recv_sem=recv_sems.at[kk], device_id=(x, y, 1 - c), device_id_type=_MESH)
                cp.start()
                copies.append(cp)
        for cp in copies:
            cp.wait()

    n_copies = nq * RS_PIECES
    return pl.pallas_call(
        body, name=name, in_specs=[_ANY], out_specs=_ANY,
        out_shape=jax.ShapeDtypeStruct((nq, r, l), halves.dtype),
        scratch_shapes=[pltpu.SemaphoreType.DMA((n_copies,)), pltpu.SemaphoreType.DMA((n_copies,))],
    )(halves)


def pair_sum_bf16(halves, theirs, *, name, tt=128):
    _, nq, r, wd = halves.shape
    tt = min(tt, r)
    parity = lax.axis_index("c").astype(jnp.int32).reshape(1)

    def body(c_ref, own_ref, sib_ref, o_ref):
        o_ref[...] = (own_ref[...] + sib_ref[...]).astype(BF16)

    return pl.pallas_call(
        body, name=name,
        grid_spec=pltpu.PrefetchScalarGridSpec(
            num_scalar_prefetch=1, grid=(nq, r // tt),
            in_specs=[pl.BlockSpec((None, None, tt, wd), lambda q, i, c: (c[0], q, i, 0)),
                      pl.BlockSpec((None, tt, wd), lambda q, i, c: (q, i, 0))],
            out_specs=pl.BlockSpec((None, tt, wd), lambda q, i, c: (q, i, 0))),
        out_shape=jax.ShapeDtypeStruct((nq, r, wd), BF16),
        compiler_params=_params(("parallel", "parallel")),
    )(parity, halves, theirs)


def rs_chip_exchange(part, *, name):
    def body(src, dst, send_sems, recv_sems, local_sem):
        x, y, c = _me()
        q_me = 2 * x + y
        local = pltpu.make_async_copy(src.at[q_me], dst.at[q_me], local_sem)
        local.start()
        copies = []
        for j, (px, py) in enumerate([(1 - x, y), (x, 1 - y), (1 - x, 1 - y)]):
            cp = pltpu.make_async_remote_copy(src_ref=src.at[2 * px + py], dst_ref=dst.at[q_me], send_sem=send_sems.at[j],
                                              recv_sem=recv_sems.at[j], device_id=(px, py, c), device_id_type=_MESH)
            cp.start()
            copies.append(cp)
        for cp in copies:
            cp.wait()
        local.wait()

    return pl.pallas_call(
        body, name=name, in_specs=[_ANY], out_specs=_ANY,
        out_shape=jax.ShapeDtypeStruct(part.shape, part.dtype),
        scratch_shapes=[pltpu.SemaphoreType.DMA((3,)), pltpu.SemaphoreType.DMA((3,)), pltpu.SemaphoreType.DMA(())],
    )(part)


def adamw(slabs, w, m, v, *, name, tt):
    ns, (r, wd) = slabs.shape[0], w.shape
    tt = min(tt, r)
    assert r % tt == 0

    def body(s_ref, w_ref, m_ref, v_ref, g_ref, d_ref, nm_ref, nv_ref):
        g = s_ref[0].astype(F32)
        for kdev in range(1, ns):
            g = g + s_ref[kdev].astype(F32)
        wv = w_ref[...]
        nm = ADAM_B1 * m_ref[...] + (1.0 - ADAM_B1) * g
        nv = ADAM_B2 * v_ref[...] + (1.0 - ADAM_B2) * (g * g)
        m_hat = nm / (1.0 - ADAM_B1 ** ADAM_STEP)
        v_hat = nv / (1.0 - ADAM_B2 ** ADAM_STEP)
        g_ref[...] = g
        d_ref[...] = -ADAM_LR * (m_hat / (jnp.sqrt(v_hat) + ADAM_EPS) + ADAM_WD * wv)
        nm_ref[...] = nm
        nv_ref[...] = nv

    spec = pl.BlockSpec((tt, wd), lambda i: (i, 0))
    return pl.pallas_call(
        body, name=name, grid=(r // tt,),
        in_specs=[pl.BlockSpec((ns, tt, wd), lambda i: (0, i, 0)), spec, spec, spec],
        out_specs=[spec] * 4, out_shape=[jax.ShapeDtypeStruct((r, wd), F32)] * 4,
        compiler_params=_params(("parallel",)),
    )(slabs, w, m, v)


WIDE = 1024
BIG = [("w_in", True, 289), ("w_out", False, 128), ("xa_wq", False, 128), ("xa_wk", False, 128), ("xa_wv", False, 128),
       ("xa_wo", False, 128), ("mlp_w2", False, 512), ("mlp_w1", True, 512)]
TINY = [("ssd_conv_w", 2), ("s5_glu_w", 1), ("rg_conv_w", 2)]
KEEP_F32 = ("ssd_conv_w", "rg_conv_w")
TINY_ROWS = 32
SHARDED = [name for name, _, _ in BIG] + [name for name, _ in TINY]
SMALL = ["ssd_conv_b", "ssd_dt_bias", "ssd_a_log", "ssd_d", "ssd_norm_w", "s5_lam_re", "s5_lam_im",
         "s5_log_step", "s5_b_re", "s5_b_im", "s5_c_re", "s5_c_im", "s5_d", "s5_glu_b", "rg_conv_b",
         "rg_wa", "rg_ba", "rg_wx", "rg_bx", "rg_lambda", "ln1_g", "ln1_b", "ln2_g", "ln2_b", "ln3_g", "ln3_b"]
WEIGHTS = ['w_in', 'w_out', 'ssd_conv_w', 'ssd_conv_b', 'ssd_dt_bias', 'ssd_a_log', 'ssd_d', 'ssd_norm_w',
           's5_lam_re', 's5_lam_im', 's5_log_step', 's5_b_re', 's5_b_im', 's5_c_re', 's5_c_im', 's5_d',
           's5_glu_w', 's5_glu_b', 'rg_conv_w', 'rg_conv_b', 'rg_wa', 'rg_ba', 'rg_wx', 'rg_bx', 'rg_lambda',
           'ln1_g', 'ln1_b', 'xa_wq', 'xa_wk', 'xa_wv', 'xa_wo', 'ln2_g', 'ln2_b', 'mlp_w1', 'mlp_w2',
           'ln3_g', 'ln3_b']


def _pad16(rows):
    return -(-rows // 16) * 16


def _pack_rows(flat, mult):
    n = flat.shape[-1]
    r = -(-n // (LANE * mult)) * mult
    pad = [(0, 0)] * (flat.ndim - 1) + [(0, r * LANE - n)]
    return jnp.pad(flat, pad).reshape(flat.shape[:-1] + (r, LANE))


def _unpack(packed, shapes):
    lead = packed.shape[:-2]
    flat = packed.reshape(lead + (-1,))
    out, off = [], 0
    for s in shapes:
        n = math.prod(s)
        out.append(flat[..., off:off + n].reshape(lead + tuple(s)))
        off += n
    return out


def _big_block(x, rows):
    pad = [(0, 0)] * (x.ndim - 2) + [(0, _pad16(rows) - rows), (0, 0)]
    x = jnp.pad(x, pad)
    return x.reshape(x.shape[:-3] + (DEPTH * _pad16(rows), WIDE))


def _tiny_block(flat):
    pad = [(0, 0)] * (flat.ndim - 1) + [(0, TINY_ROWS * WIDE - flat.shape[-1])]
    return jnp.pad(flat, pad).reshape(flat.shape[:-1] + (TINY_ROWS, WIDE))


def _pack_wide(big, tiny_flat):
    blocks = [_big_block(big[name], rows) for name, _, rows in BIG] + [_tiny_block(tiny_flat)]
    return jnp.concatenate(blocks, axis=-2)


def _unpack_wide(packed):
    lead, big, off = packed.shape[:-2], {}, 0
    for name, _, rows in BIG:
        rp = _pad16(rows)
        big[name] = packed[..., off:off + DEPTH * rp, :].reshape(lead + (DEPTH, rp, WIDE))[..., :rows, :]
        off += DEPTH * rp
    return big, packed[..., off:off + TINY_ROWS, :].reshape(lead + (TINY_ROWS * WIDE,))


def _split_flat(flat, shapes):
    out, off = [], 0
    for s in shapes:
        n = math.prod(s)
        out.append(flat[..., off:off + n].reshape(flat.shape[:-1] + tuple(s)))
        off += n
    return out


def _to_full(gathered, axis):
    g = jnp.moveaxis(gathered, 0, axis)
    s = g.shape
    return g.reshape(s[:axis] + (s[axis] * s[axis + 1],) + s[axis + 2:])


def _to_slabs(full, axis):
    s = full.shape
    g = full.reshape(s[:axis] + (N_DEV, s[axis] // N_DEV) + s[axis + 1:])
    return jnp.moveaxis(g, axis, 0)


def _blockdiag(w):
    h, i, j = w.shape
    eye = jnp.eye(h, dtype=w.dtype)
    return (w[:, :, None, :] * eye[:, None, :, None]).reshape(h * i, h * j)


def _blockdiag_extract(m, h):
    i, j = m.shape[0] // h, m.shape[1] // h
    eye = jnp.eye(h, dtype=m.dtype)
    return (m.reshape(h, i, h, j) * eye[:, None, :, None]).sum(axis=2)


def _s5_disc(lr, li, ls, bre, bim):
    step = jnp.exp(ls)[:, None]
    er = jnp.exp(lr * step)
    ar, ai = er * jnp.cos(li * step), er * jnp.sin(li * step)
    nr, ni, den = ar - 1.0, ai, lr * lr + li * li
    qr, qi = (nr * lr + ni * li) / den, (ni * lr - nr * li) / den
    bbr = qr[..., None] * bre - qi[..., None] * bim
    bbi = qr[..., None] * bim + qi[..., None] * bre
    return ar, ai, bbr, bbi


def _row(v, width=None):
    v = v.reshape(1, -1)
    if width is not None and v.shape[1] < width:
        v = jnp.pad(v, ((0, 0), (0, width - v.shape[1])))
    return v


def _relu2(a):
    r = jnp.maximum(a, 0.0)
    return r * r


def _add_alpha(acc, d):
    return acc + ALPHA * d


def _shift_rows_down(x):
    return jnp.concatenate([jnp.zeros((1, x.shape[1]), x.dtype), x[:-1]], axis=0)


def _shift_rows_up(x):
    return jnp.concatenate([x[1:], jnp.zeros((1, x.shape[1]), x.dtype)], axis=0)


def _layer_params(full, small, l):
    p = {}
    w_in = full["w_in"][l]
    z, xbc, dt, u, xr, g = w_in[0:512], w_in[512:1536], w_in[1536:1544], w_in[1544:1800], w_in[1800:2056], w_in[2056:2312]
    p["w_inp"] = jnp.concatenate([xbc, z, u, xr, g, dt, jnp.zeros((D_INP - P_DT - 8, D_MODEL), w_in.dtype)], axis=0)
    for k_ in ("w_out", "xa_wq", "xa_wk", "xa_wv", "xa_wo", "mlp_w1", "mlp_w2", "s5_glu_w"):
        p[k_] = full[k_][l]
    p["ssd_cw"], p["ssd_cb"] = full["ssd_conv_w"][l], _row(small["ssd_conv_b"][l])
    dtb, alog, dsk = small["ssd_dt_bias"][l], small["ssd_a_log"][l], small["ssd_d"][l]
    p["prow"] = jnp.concatenate([_row(dtb, LANE), _row(alog, LANE), jnp.zeros((6, LANE), F32)], axis=0)
    p["ssd_dx"] = _row(jnp.repeat(dsk, SSD_HEAD_DIM))
    p["ssd_nw"] = _row(small["ssd_norm_w"][l])
    s5_in = (small["s5_lam_re"][l], small["s5_lam_im"][l], small["s5_log_step"][l], small["s5_b_re"][l], small["s5_b_im"][l])
    (ar, ai, bbr, bbi), p["s5_vjp"] = jax.vjp(_s5_disc, *s5_in)
    p["lam_fwd"] = jnp.concatenate([_row(ar), _row(ai)], axis=1)
    p["lam_adj"] = jnp.concatenate([_row(ar), _row(-ai)], axis=1)
    p["bcat"] = jnp.concatenate([_blockdiag(jnp.swapaxes(bbr, 1, 2)), _blockdiag(jnp.swapaxes(bbi, 1, 2))], axis=1)
    p["ccat"] = jnp.concatenate([_blockdiag(jnp.swapaxes(small["s5_c_re"][l], 1, 2)),
                                 -_blockdiag(jnp.swapaxes(small["s5_c_im"][l], 1, 2))], axis=0)
    p["s5_d"], p["s5_glu_b"] = _row(small["s5_d"][l]), _row(small["s5_glu_b"][l])
    p["rg_cw"], p["rg_cb"] = full["rg_conv_w"][l], _row(small["rg_conv_b"][l])
    p["rg_wa"], p["rg_wx"] = _blockdiag(small["rg_wa"][l]), _blockdiag(small["rg_wx"][l])
    p["rg_ba"], p["rg_bx"], p["rg_lam"] = _row(small["rg_ba"][l]), _row(small["rg_bx"][l]), _row(small["rg_lambda"][l])
    for i in (1, 2, 3):
        p[f"g{i}"], p[f"b{i}"] = _row(small[f"ln{i}_g"][l]), _row(small[f"ln{i}_b"][l])
    return p


def _layer_fwd(h0, mem, p):
    t = h0.shape[0]
    s = {"h0": h0}
    proj = mm(h0, p["w_inp"], tb=True, name="in_proj")
    xbc = conv_fwd(proj, 0, p["ssd_cw"], p["ssd_cb"], width=SSD_XBC, act=True, name="ssd_conv_fwd")
    y_ssd, yraw, sall = ssd_fwd(xbc, proj, p["prow"], p["ssd_dx"], p["ssd_nw"], name="ssd_fwd")
    bu = mm(proj, p["bcat"], a_off=P_U, k=S5_WIDTH, name="s5_bu")
    hs5 = scan_complex(bu, p["lam_fwd"], reverse=False, name="s5_scan_fwd")
    ylin = mm(hs5, p["ccat"], name="s5_ylin")
    (y_s5,), _ = rowk(_s5_post_fwd_fn, [(ylin, S5_WIDTH, 0), (proj, S5_WIDTH, P_U // S5_WIDTH)],
                      [p["s5_d"], p["s5_glu_w"], p["s5_glu_b"]], [S5_WIDTH], [], rows=t, name="s5_post_fwd")
    xc = conv_fwd(proj, P_XR // RG_WIDTH, p["rg_cw"], p["rg_cb"], width=RG_WIDTH, act=False, name="rg_conv_fwd")
    rg_full = [p["rg_wa"], p["rg_wx"], p["rg_ba"], p["rg_bx"], p["rg_lam"]]
    (a_rg, b_rg), _ = rowk(_rg_pre_fwd_fn, [(xc, RG_WIDTH, 0)], rg_full, [RG_WIDTH, RG_WIDTH], [], rows=t, name="rg_pre_fwd")
    h_rg = scan_real(a_rg, b_rg, reverse=False, name="rg_scan_fwd")
    (y_rg,), _ = rowk(_rg_out_fwd_fn, [(h_rg, RG_WIDTH, 0), (proj, RG_WIDTH, P_G // RG_WIDTH)], [], [RG_WIDTH], [],
                      rows=t, name="rg_out_fwd")
    ycat = jnp.concatenate([y_ssd, y_s5, y_rg], axis=1)
    mix = mm(ycat, p["w_out"], name="out_proj")
    h1 = ln_fwd(h0, mix, p["g1"], p["b1"], name="ln_fwd")
    q = mm(h1, p["xa_wq"], name="xa_q")
    k = mm(mem, p["xa_wk"], name="xa_kv")
    v = mm(mem, p["xa_wv"], name="xa_kv")
    (o,), _ = rowk(_attn_fwd_fn, [(q, D_MODEL, 0)], [k, v], [D_MODEL], [], rows=t, name="xa_fwd")
    att = mm(o, p["xa_wo"], name="xa_o")
    h2 = ln_fwd(h1, att, p["g2"], p["b2"], name="ln_fwd")
    a_mlp = mm(h2, p["mlp_w1"], tb=True, name="mlp_up")
    m_out = mm(a_mlp, p["mlp_w2"], fa=_relu2, name="mlp_down")
    h3 = ln_fwd(h2, m_out, p["g3"], p["b3"], name="ln_fwd")
    s.update(proj=proj, xbc=xbc, yraw=yraw, sall=sall, hs5=hs5, ylin=ylin, xc=xc, a_rg=a_rg, h_rg=h_rg,
             ycat=ycat, mix=mix, h1=h1, q=q, k=k, v=v, o=o, att=att, h2=h2, a_mlp=a_mlp, m_out=m_out)
    return h3, s


def _layer_bwd(dh3, mem, p, s, l, gfull, gsmall):
    t = dh3.shape[0]
    proj = s["proj"]
    dpre3, dg3, db3 = ln_bwd(s["h2"], s["m_out"], dh3, p["g3"], name="ln_bwd")
    da = mm(dpre3, p["mlp_w2"], tb=True, o_extra=(s["a_mlp"],), fo=lambda acc, a: acc * 2.0 * jnp.maximum(a, 0.0), name="mlp_da")
    gfull["mlp_w2"][l] = mm(s["a_mlp"], dpre3, ta=True, fa=_relu2, name="mlp_dw2")
    gfull["mlp_w1"][l] = mm(da, s["h2"], ta=True, name="mlp_dw1")
    dh2 = mm(da, p["mlp_w1"], o_extra=(dpre3,), fo=_add_alpha, name="mlp_dx")
    dpre2, dg2, db2 = ln_bwd(s["h1"], s["att"], dh2, p["g2"], name="ln_bwd")
    do = mm(dpre2, p["xa_wo"], tb=True, name="xa_do")
    gfull["xa_wo"][l] = mm(s["o"], dpre2, ta=True, name="dw_sq")
    (dq,), (dk, dv) = rowk(_attn_bwd_fn, [(s["q"], D_MODEL, 0), (do, D_MODEL, 0)], [s["k"], s["v"]], [D_MODEL],
                           [(256, D_MODEL), (256, D_MODEL)], rows=t, name="xa_bwd")
    gfull["xa_wq"][l] = mm(s["h1"], dq, ta=True, name="dw_sq")
    gfull["xa_wk"][l] = mm(mem, dk, ta=True, name="dw_kv")
    gfull["xa_wv"][l] = mm(mem, dv, ta=True, name="dw_kv")
    dh1 = mm(dq, p["xa_wq"], tb=True, o_extra=(dpre2,), fo=_add_alpha, name="dx_sq")
    dpre1, dg1, db1 = ln_bwd(s["h0"], s["mix"], dh1, p["g1"], name="ln_bwd")
    dycat = mm(dpre1, p["w_out"], tb=True, name="xa_do")
    gfull["w_out"][l] = mm(s["ycat"], dpre1, ta=True, name="dw_sq")
    (dh_rg, dg_rg), _ = rowk(_rg_out_bwd_fn, [(s["h_rg"], RG_WIDTH, 0), (proj, RG_WIDTH, P_G // RG_WIDTH), (dycat, RG_WIDTH, 3)],
                             [], [RG_WIDTH, RG_WIDTH], [], rows=t, name="rg_out_bwd")
    g_rg = scan_real(_shift_rows_up(s["a_rg"]), dh_rg, reverse=True, name="rg_scan_bwd")
    rg_full = [p["rg_wa"], p["rg_wx"], p["rg_ba"], p["rg_bx"], p["rg_lam"]]
    (dxc,), (dwa, dwx, dba, dbx, dlam) = rowk(
        _rg_pre_bwd_fn, [(s["xc"], RG_WIDTH, 0), (g_rg, RG_WIDTH, 0), (_shift_rows_down(s["h_rg"]), RG_WIDTH, 0)], rg_full,
        [RG_WIDTH], [(RG_WIDTH, RG_WIDTH), (RG_WIDTH, RG_WIDTH), (1, RG_WIDTH), (1, RG_WIDTH), (1, RG_WIDTH)],
        rows=t, name="rg_pre_bwd")
    dxr, d_rgcw, d_rgcb = conv_bwd(proj, P_XR // RG_WIDTH, dxc, p["rg_cw"], p["rg_cb"], width=RG_WIDTH, act=False, name="rg_conv_bwd")
    (dylin, du_a), (d_s5d, d_gluw, d_glub) = rowk(
        _s5_post_bwd_fn, [(s["ylin"], S5_WIDTH, 0), (proj, S5_WIDTH, P_U // S5_WIDTH), (dycat, S5_WIDTH, 2)],
        [p["s5_d"], p["s5_glu_w"], p["s5_glu_b"]], [S5_WIDTH, S5_WIDTH],
        [(1, S5_WIDTH), (S5_WIDTH, S5_WIDTH), (1, S5_WIDTH)], rows=t, name="s5_post_bwd")
    dhs = mm(dylin, p["ccat"], tb=True, name="s5_dh")
    dccat = mm(s["hs5"], dylin, ta=True, name="s5_dc")
    gs5 = scan_complex(dhs, p["lam_adj"], reverse=True, name="s5_scan_bwd")
    dar, dai = s5_dlam(gs5, _shift_rows_down(s["hs5"]), name="s5_dlam")
    du = mm(gs5, p["bcat"], tb=True, o_extra=(du_a,), fo=lambda acc, d: acc + d, name="s5_du")
    dbcat = mm(proj, gs5, ta=True, a_off=P_U, m=S5_WIDTH, name="s5_db")
    dxbc_act, dz, ddt, dprm, ddx, dnw = ssd_bwd(s["xbc"], proj, p["prow"], p["ssd_dx"], p["ssd_nw"], s["yraw"], s["sall"], dycat,
                                               name="ssd_bwd")
    dxbc, d_scw, d_scb = conv_bwd(proj, 0, dxbc_act, p["ssd_cw"], p["ssd_cb"], width=SSD_XBC, act=True, name="ssd_conv_bwd")
    dproj = jnp.concatenate([dxbc, dz, du, dxr, dg_rg, ddt, jnp.zeros((t, D_INP - P_DT - LANE), F32)], axis=1)
    dh0 = mm(dproj, p["w_inp"], o_extra=(dpre1,), fo=_add_alpha, name="in_proj_dx")
    dwp = mm(dproj, s["h0"], ta=True, name="in_proj_dw")
    gfull["w_in"][l] = jnp.concatenate([dwp[P_Z:P_Z + 512], dwp[P_XBC:P_XBC + 1024], dwp[P_DT:P_DT + 8],
                                        dwp[P_U:P_U + 256], dwp[P_XR:P_XR + 256], dwp[P_G:P_G + 256]], axis=0)
    gfull["ssd_conv_w"][l], gfull["rg_conv_w"][l], gfull["s5_glu_w"][l] = d_scw, d_rgcw, d_gluw
    ng, ns = S5_GROUPS, S5_STATE
    dbbr = jnp.swapaxes(_blockdiag_extract(dbcat[:, :S5_NSTATE], ng), 1, 2)
    dbbi = jnp.swapaxes(_blockdiag_extract(dbcat[:, S5_NSTATE:], ng), 1, 2)
    d_lr, d_li, d_ls, d_bre, d_bim = p["s5_vjp"]((dar.reshape(ng, ns), dai.reshape(ng, ns), dbbr, dbbi))
    gsmall["s5_lam_re"][l], gsmall["s5_lam_im"][l], gsmall["s5_log_step"][l] = d_lr, d_li, d_ls
    gsmall["s5_b_re"][l], gsmall["s5_b_im"][l] = d_bre, d_bim
    gsmall["s5_c_re"][l] = jnp.swapaxes(_blockdiag_extract(dccat[:S5_NSTATE], ng), 1, 2)
    gsmall["s5_c_im"][l] = -jnp.swapaxes(_blockdiag_extract(dccat[S5_NSTATE:], ng), 1, 2)
    gsmall["s5_d"][l], gsmall["s5_glu_b"][l] = d_s5d[0], d_glub[0]
    gsmall["ssd_conv_b"][l], gsmall["rg_conv_b"][l] = d_scb[0], d_rgcb[0]
    gsmall["ssd_dt_bias"][l], gsmall["ssd_a_log"][l] = dprm[0, :8], dprm[1, :8]
    gsmall["ssd_d"][l] = ddx.reshape(SSD_HEADS, SSD_HEAD_DIM).sum(axis=1)
    gsmall["ssd_norm_w"][l] = dnw[0]
    gsmall["rg_wa"][l], gsmall["rg_wx"][l] = _blockdiag_extract(dwa, RG_BLOCKS), _blockdiag_extract(dwx, RG_BLOCKS)
    gsmall["rg_ba"][l], gsmall["rg_bx"][l] = dba.reshape(RG_BLOCKS, RG_BLOCK_DIM), dbx.reshape(RG_BLOCKS, RG_BLOCK_DIM)
    gsmall["rg_lambda"][l] = dlam[0]
    for i, (dg, db) in zip((1, 2, 3), ((dg1, db1), (dg2, db2), (dg3, db3))):
        gsmall[f"ln{i}_g"][l], gsmall[f"ln{i}_b"][l] = dg[0], db[0]
    return dh0


def _step(a):
    h = a["x"][0]
    mem = a["mem"][0]
    t = h.shape[0]

    def my_shards(pre):
        return ({name: (jnp.swapaxes(a[pre + name], 1, 2) if tr else a[pre + name]) for name, tr, _ in BIG},
                [a[pre + name] for name, _ in TINY])

    big, tiny = my_shards("")
    tiny16 = [(lax.bitcast_convert_type(w, BF16) if name in KEEP_F32 else w.astype(BF16)).reshape(-1)
              for (name, _), w in zip(TINY, tiny)]
    packed = _pack_wide({name: w.astype(BF16) for name, w in big.items()}, jnp.concatenate(tiny16))
    gbig, gtiny = _unpack_wide(all_gather(packed, name="ag_weights"))
    full = {name: _to_full(gbig[name], 1) for name, _, _ in BIG}
    tiny_shapes = [w.shape + ((2,) if name in KEEP_F32 else ()) for (name, _), w in zip(TINY, tiny)]
    for (name, axis), g in zip(TINY, _split_flat(gtiny, tiny_shapes)):
        full[name] = _to_full(lax.bitcast_convert_type(g, F32) if name in KEEP_F32 else g, axis)
    small = {name: a[name] for name in SMALL}
    params, saved = [], []
    for l in range(DEPTH):
        p = _layer_params(full, small, l)
        h, s = _layer_fwd(h, mem, p)
        params.append(p)
        saved.append(s)
    (dh,), (loss_part,) = rowk(_loss_fn, [(h, D_MODEL, 0), (a["loss_target"][0], D_MODEL, 0)], [], [D_MODEL], [(1, 1)],
                               rows=t, name="loss_head")
    loss = lax.psum(loss_part[0, 0], ("x", "y", "c"))
    gfull = {name: [None] * DEPTH for name in SHARDED}
    gsmall = {name: [None] * DEPTH for name in SMALL}
    for l in reversed(range(DEPTH)):
        dh = _layer_bwd(dh, mem, params[l], saved[l], l, gfull, gsmall)
    grad_x = dh[None]
    gbig = {name: jnp.stack([g.reshape(N_DEV, rows, WIDE) for g in gfull[name]], axis=1) for name, _, rows in BIG}
    gtiny = jnp.concatenate([_to_slabs(jnp.stack(gfull[name]), axis).reshape(N_DEV, -1) for name, axis in TINY], axis=1)
    slabs = _pack_wide(gbig, gtiny)
    halves = jnp.swapaxes(slabs.reshape((4, 2) + slabs.shape[1:]), 0, 1)
    theirs = rs_sibling_exchange(halves, name="rs_sibling")
    slabs = rs_chip_exchange(pair_sum_bf16(halves, theirs, name="rs_pair_sum"), name="rs_chips")

    def pk(pre):
        big, tiny = my_shards(pre)
        return _pack_wide(big, jnp.concatenate([w.reshape(-1) for w in tiny]))

    bigs = adamw(slabs, pk(""), pk("m_"), pk("v_"), name="adamw_sharded", tt=128)
    gs = _pack_rows(jnp.concatenate([jnp.stack(gsmall[name]).reshape(-1) for name in SMALL]), 8)
    gs = all_gather(gs, name="ag_small_grads")
    pks = lambda pre: _pack_rows(jnp.concatenate([a[pre + name].reshape(-1) for name in SMALL]), 8)
    sm = adamw(gs, pks(""), pks("m_"), pks("v_"), name="adamw_replicated", tt=gs.shape[1])
    out = {}
    for kind, bg, sg in zip(("grad_", "delta_", "new_m_", "new_v_"), bigs, sm):
        obig, otiny = _unpack_wide(bg)
        for name, tr, _ in BIG:
            out[kind + name] = jnp.swapaxes(obig[name], 1, 2) if tr else obig[name]
        for (name, _), arr in zip(TINY, _split_flat(otiny, [w.shape for w in tiny])):
            out[kind + name] = arr
        for name, arr in zip(SMALL, _unpack(sg, [a[name].shape for name in SMALL])):
            out[kind + name] = arr
    return (loss, grad_x) + tuple(out[kind + name] for kind in ("grad_", "delta_", "new_m_", "new_v_") for name in WEIGHTS)


def kernel(x, mem, w_in, w_out, ssd_conv_w, ssd_conv_b, ssd_dt_bias, ssd_a_log, ssd_d, ssd_norm_w, s5_lam_re, s5_lam_im, s5_log_step, s5_b_re, s5_b_im, s5_c_re, s5_c_im, s5_d, s5_glu_w, s5_glu_b, rg_conv_w, rg_conv_b, rg_wa, rg_ba, rg_wx, rg_bx, rg_lambda, ln1_g, ln1_b, xa_wq, xa_wk, xa_wv, xa_wo, ln2_g, ln2_b, mlp_w1, mlp_w2, ln3_g, ln3_b, loss_target, m_w_in, m_w_out, m_ssd_conv_w, m_ssd_conv_b, m_ssd_dt_bias, m_ssd_a_log, m_ssd_d, m_ssd_norm_w, m_s5_lam_re, m_s5_lam_im, m_s5_log_step, m_s5_b_re, m_s5_b_im, m_s5_c_re, m_s5_c_im, m_s5_d, m_s5_glu_w, m_s5_glu_b, m_rg_conv_w, m_rg_conv_b, m_rg_wa, m_rg_ba, m_rg_wx, m_rg_bx, m_rg_lambda, m_ln1_g, m_ln1_b, m_xa_wq, m_xa_wk, m_xa_wv, m_xa_wo, m_ln2_g, m_ln2_b, m_mlp_w1, m_mlp_w2, m_ln3_g, m_ln3_b, v_w_in, v_w_out, v_ssd_conv_w, v_ssd_conv_b, v_ssd_dt_bias, v_ssd_a_log, v_ssd_d, v_ssd_norm_w, v_s5_lam_re, v_s5_lam_im, v_s5_log_step, v_s5_b_re, v_s5_b_im, v_s5_c_re, v_s5_c_im, v_s5_d, v_s5_glu_w, v_s5_glu_b, v_rg_conv_w, v_rg_conv_b, v_rg_wa, v_rg_ba, v_rg_wx, v_rg_bx, v_rg_lambda, v_ln1_g, v_ln1_b, v_xa_wq, v_xa_wk, v_xa_wv, v_xa_wo, v_ln2_g, v_ln2_b, v_mlp_w1, v_mlp_w2, v_ln3_g, v_ln3_b):
    return _step(dict(locals()))
```

```python
import math

import jax
import jax.numpy as jnp
from jax import lax
from jax.experimental import pallas as pl
from jax.experimental.pallas import tpu as pltpu

F32 = jnp.float32
BF16 = jnp.bfloat16

N_DEV = 8
D_MODEL = 1024
DEPTH = 2
SSD_WIDTH = 512
SSD_HEADS = 8
SSD_HEAD_DIM = 64
SSD_STATE = 128
SSD_CHUNK = 128
SSD_XBC = 1024
S5_WIDTH = 256
S5_GROUPS = 16
S5_GROUP_CH = 16
S5_STATE = 64
S5_NSTATE = S5_GROUPS * S5_STATE
RG_WIDTH = 256
RG_BLOCKS = 4
RG_BLOCK_DIM = 64
RG_C = 8.0
XA_HEADS = 4
XA_HEAD_DIM = 256
ALPHA = (2.0 * DEPTH) ** 0.25
LN_EPS = 1e-5
ADAM_LR, ADAM_B1, ADAM_B2, ADAM_EPS, ADAM_WD, ADAM_STEP = 0.001, 0.9, 0.999, 1e-08, 0.01, 10

P_XBC, P_Z, P_U, P_XR, P_G, P_DT = 0, 1024, 1536, 1792, 2048, 2304
D_INP = 2560
LANE = 128
VMEM_LIMIT = 56 * 1024 * 1024
ROW_TILE = 512

_NN = ((1,), (0,))
_NT = ((1,), (1,))
_TN = ((0,), (0,))


def _dot(a, b, dims=_NN):
    return lax.dot_general(a.astype(BF16), b.astype(BF16), (dims, ((), ())), preferred_element_type=F32)


def _split_bf16(x, parts):
    out, rem = [], x
    for _ in range(parts):
        piece = rem.astype(BF16)
        out.append(piece)
        rem = rem - piece.astype(F32)
    return out


def _dot_mask(a, b, dims=_NN, *, mask_left, parts):
    if mask_left:
        return sum(_dot(a, piece, dims) for piece in _split_bf16(b, parts))
    return sum(_dot(piece, b, dims) for piece in _split_bf16(a, parts))


def _sigmoid(x):
    return 1.0 / (1.0 + jnp.exp(-x))


def _silu(x):
    return x * _sigmoid(x)


def _dsilu(x):
    s = _sigmoid(x)
    return s * (1.0 + x * (1.0 - s))


_GK = math.sqrt(2.0 / math.pi)
_GC = 0.044715


def _gelu(x):
    return 0.5 * x * (1.0 + jnp.tanh(_GK * (x + _GC * x * x * x)))


def _dgelu(x):
    th = jnp.tanh(_GK * (x + _GC * x * x * x))
    return 0.5 * (1.0 + th) + 0.5 * x * (1.0 - th * th) * _GK * (1.0 + 3.0 * _GC * x * x)


def _log1p_pos(e):
    return jnp.where(e < 1e-2, e * (1.0 - e * (0.5 - e * (1.0 / 3.0))), jnp.log(1.0 + e))


def _softplus(x):
    return jnp.maximum(x, 0.0) + _log1p_pos(jnp.exp(-jnp.abs(x)))


def _neg_expm1(x):
    poly = -x * (1.0 + x * (0.5 + x * (1.0 / 6.0 + x * (1.0 / 24.0 + x * (1.0 / 120.0)))))
    return jnp.where(x > -0.05, poly, 1.0 - jnp.exp(x))


def _params(sem):
    return pltpu.CompilerParams(dimension_semantics=sem, vmem_limit_bytes=VMEM_LIMIT)


RESIDENT_BYTES = 8 * 1024 * 1024
STREAM_BYTES = 4 * 1024 * 1024


def _halve_to_fit(dims, bytes_per, limit):
    dims = list(dims)
    while math.prod(dims) * bytes_per > limit:
        i = max(range(len(dims)), key=lambda d: dims[d])
        assert dims[i] % 256 == 0, dims
        dims[i] //= 2
    return dims


def mm(a, b, *, name, ta=False, tb=False, a_extra=(), fa=None, o_extra=(), fo=None, a_off=0, m=None, k=None):
    n = b.shape[0] if tb else b.shape[1]
    na, no = 1 + len(a_extra), len(o_extra)
    if not ta:
        assert m is None
        m, kdim = a.shape[0], (a.shape[1] if k is None else k)
        assert a_off % kdim == 0
        (tn,) = _halve_to_fit([n], kdim * b.dtype.itemsize, RESIDENT_BYTES)
        (tm,) = _halve_to_fit([min(512, m)], max(tn, kdim) * 4, STREAM_BYTES)
        a_spec = pl.BlockSpec((tm, kdim), lambda i, j: (i, a_off // kdim))
        b_spec = pl.BlockSpec((tn, kdim), lambda i, j: (j, 0)) if tb else pl.BlockSpec((kdim, tn), lambda i, j: (0, j))
        o_spec = pl.BlockSpec((tm, tn), lambda i, j: (i, j))
        dims = _NT if tb else _NN

        def body(*refs):
            a_refs, b_ref, o_refs, out_ref = refs[:na], refs[na], refs[na + 1:na + 1 + no], refs[na + 1 + no]
            av = a_refs[0][...] if fa is None else fa(*[r[...] for r in a_refs])
            acc = _dot(av, b_ref[...], dims)
            out_ref[...] = acc if fo is None else fo(acc, *[r[...] for r in o_refs])

        grid, sem = (m // tm, n // tn), ("parallel", "parallel")
    else:
        assert k is None and not tb and fo is None and not o_extra
        kdim, m = a.shape[0], (a.shape[1] if m is None else m)
        tm, tn = _halve_to_fit([m, n], 4, RESIDENT_BYTES)
        (tk,) = _halve_to_fit([min(512, kdim)], max(tm, tn) * 4, STREAM_BYTES)
        assert a_off % tm == 0
        a_spec = pl.BlockSpec((tk, tm), lambda i, j, kk: (kk, i + a_off // tm))
        b_spec = pl.BlockSpec((tk, tn), lambda i, j, kk: (kk, j))
        o_spec = pl.BlockSpec((tm, tn), lambda i, j, kk: (i, j))

        def body(*refs):
            a_refs, b_ref, out_ref = refs[:na], refs[na], refs[na + 1]

            @pl.when(pl.program_id(2) == 0)
            def _():
                out_ref[...] = jnp.zeros_like(out_ref)

            av = a_refs[0][...] if fa is None else fa(*[r[...] for r in a_refs])
            out_ref[...] += _dot(av, b_ref[...], _TN)

        grid, sem = (m // tm, n // tn, kdim // tk), ("parallel", "parallel", "arbitrary")
    assert m % tm == 0 and n % tn == 0, (name, m, n, tm, tn)
    return pl.pallas_call(
        body, name=name, grid=grid,
        in_specs=[a_spec] * na + [b_spec] + [o_spec] * no,
        out_specs=o_spec, out_shape=jax.ShapeDtypeStruct((m, n), F32),
        compiler_params=_params(sem),
    )(a, *a_extra, b, *o_extra)


def rowk(fn, tiled, full, out_w, acc_shapes, *, rows, name):
    tt = min(ROW_TILE, rows)
    n = rows // tt
    assert rows % tt == 0
    nt, nf, no = len(tiled), len(full), len(out_w)

    def tspec(w, cb):
        return pl.BlockSpec((tt, w), lambda i: (i, cb))

    def fspec(a):
        nd = a.ndim
        return pl.BlockSpec(a.shape, lambda i: (0,) * nd)

    def body(*refs):
        ins, fulls = refs[:nt], refs[nt:nt + nf]
        outs, accs = refs[nt + nf:nt + nf + no], refs[nt + nf + no:]
        res_t, res_a = fn(*[r[...] for r in ins], *[r[...] for r in fulls])
        for r, v in zip(outs, res_t):
            r[...] = v
        if accs:
            @pl.when(pl.program_id(0) == 0)
            def _():
                for r in accs:
                    r[...] = jnp.zeros_like(r)
            for r, v in zip(accs, res_a):
                r[...] += v

    outs = pl.pallas_call(
        body, name=name, grid=(n,),
        in_specs=[tspec(w, cb) for (_, w, cb) in tiled] + [fspec(a) for a in full],
        out_specs=[tspec(w, 0) for w in out_w] + [pl.BlockSpec(s, lambda i, nd=len(s): (0,) * nd) for s in acc_shapes],
        out_shape=[jax.ShapeDtypeStruct((rows, w), F32) for w in out_w] + [jax.ShapeDtypeStruct(s, F32) for s in acc_shapes],
        compiler_params=_params(("arbitrary",)),
    )(*[a for (a, _, _) in tiled], *full)
    return outs[:no], outs[no:]


def _colsum(x):
    return jnp.sum(x, axis=0, keepdims=True)


def _rowsum(x):
    return jnp.sum(x, axis=1, keepdims=True)


def _ln_fwd_fn(resid, y, g, b):
    pre = ALPHA * resid + y
    mu = jnp.mean(pre, axis=1, keepdims=True)
    xc = pre - mu
    var = jnp.mean(xc * xc, axis=1, keepdims=True)
    return (xc * lax.rsqrt(var + LN_EPS) * g + b,), ()


def _ln_bwd_fn(resid, y, dout, g):
    pre = ALPHA * resid + y
    mu = jnp.mean(pre, axis=1, keepdims=True)
    xc = pre - mu
    var = jnp.mean(xc * xc, axis=1, keepdims=True)
    rstd = lax.rsqrt(var + LN_EPS)
    xhat = xc * rstd
    dxh = dout * g
    dpre = rstd * (dxh - jnp.mean(dxh, axis=1, keepdims=True) - xhat * jnp.mean(dxh * xhat, axis=1, keepdims=True))
    return (dpre,), (_colsum(dout * xhat), _colsum(dout))


def ln_fwd(resid, y, g, b, *, name):
    (out,), _ = rowk(_ln_fwd_fn, [(resid, D_MODEL, 0), (y, D_MODEL, 0)], [g, b], [D_MODEL], [],
                     rows=resid.shape[0], name=name)
    return out


def ln_bwd(resid, y, dout, g, *, name):
    (dpre,), (dg, db) = rowk(_ln_bwd_fn, [(resid, D_MODEL, 0), (y, D_MODEL, 0), (dout, D_MODEL, 0)], [g],
                             [D_MODEL], [(1, D_MODEL), (1, D_MODEL)], rows=resid.shape[0], name=name)
    return dpre, dg, db


def _loss_fn(y, tgt):
    e = y - tgt
    part = _colsum(_rowsum(e * e)) * (0.5 / D_MODEL)
    return (e * (1.0 / D_MODEL),), (part,)


_XA_SCALE = 1.0 / math.sqrt(XA_HEAD_DIM)


def _attn_probs(qh, kh):
    s = _dot(qh, kh, _NT) * _XA_SCALE
    e = jnp.exp(s - jnp.max(s, axis=1, keepdims=True))
    return e / _rowsum(e)


def _attn_fwd_fn(q, k, v):
    outs = []
    for hd in range(XA_HEADS):
        sl = slice(hd * XA_HEAD_DIM, (hd + 1) * XA_HEAD_DIM)
        outs.append(_dot(_attn_probs(q[:, sl], k[:, sl]), v[:, sl]))
    return (jnp.concatenate(outs, axis=1),), ()


def _attn_bwd_fn(q, do, k, v):
    dqs, dks, dvs = [], [], []
    for hd in range(XA_HEADS):
        sl = slice(hd * XA_HEAD_DIM, (hd + 1) * XA_HEAD_DIM)
        qh, kh, vh, doh = q[:, sl], k[:, sl], v[:, sl], do[:, sl]
        p = _attn_probs(qh, kh)
        dp = _dot(doh, vh, _NT)
        ds = p * (dp - _rowsum(p * dp)) * _XA_SCALE
        dqs.append(_dot(ds, kh))
        dks.append(_dot(ds, qh, _TN))
        dvs.append(_dot(p, doh, _TN))
    cat = lambda xs: jnp.concatenate(xs, axis=1)
    return (cat(dqs),), (cat(dks), cat(dvs))


def _s5_post_fwd_fn(ylin, u, dskip, gw, gb):
    yg = _gelu(ylin + dskip * u)
    return (yg * _sigmoid(_dot(yg, gw) + gb),), ()


def _s5_post_bwd_fn(ylin, u, dout, dskip, gw, gb):
    pre = ylin + dskip * u
    yg = _gelu(pre)
    sg = _sigmoid(_dot(yg, gw) + gb)
    dlin = dout * yg * sg * (1.0 - sg)
    dyg = dout * sg + _dot(dlin, gw, _NT)
    dpre = dyg * _dgelu(pre)
    return (dpre, dpre * dskip), (_colsum(dpre * u), _dot(yg, dlin, _TN), _colsum(dlin))


def _rg_gates(xc, wa, wx, ba, bx, lam):
    r = _sigmoid(_dot(xc, wa) + ba)
    i = _sigmoid(_dot(xc, wx) + bx)
    sp = _softplus(-lam)
    log_a = -RG_C * r * sp
    a = jnp.exp(log_a)
    mult = jnp.sqrt(_neg_expm1(2.0 * log_a))
    return r, i, sp, a, mult


def _rg_pre_fwd_fn(xc, wa, wx, ba, bx, lam):
    r, i, sp, a, mult = _rg_gates(xc, wa, wx, ba, bx, lam)
    return (a, mult * (i * xc)), ()


def _rg_pre_bwd_fn(xc, gsc, hprev, wa, wx, ba, bx, lam):
    r, i, sp, a, mult = _rg_gates(xc, wa, wx, ba, bx, lam)
    da = gsc * hprev
    db = gsc
    dmult = db * i * xc
    di = db * mult * xc
    dxc = db * mult * i
    dlog_a = da * a - a * a * dmult / mult
    dr = dlog_a * (-RG_C * sp)
    dsp = _colsum(dlog_a * (-RG_C * r))
    dlam = dsp * (-_sigmoid(-lam))
    dpr = dr * r * (1.0 - r)
    dpi = di * i * (1.0 - i)
    dxc = dxc + _dot(dpr, wa, _NT) + _dot(dpi, wx, _NT)
    return (dxc,), (_dot(xc, dpr, _TN), _dot(xc, dpi, _TN), _colsum(dpr), _colsum(dpi), dlam)


def _rg_out_fwd_fn(h, g):
    return (h * _gelu(g),), ()


def _rg_out_bwd_fn(h, g, dy):
    return (dy * _gelu(g), dy * h * _dgelu(g)), ()


def _shift_down(x, prev, j, rows):
    return jnp.where(rows < j, pltpu.roll(prev, j, 0), pltpu.roll(x, j, 0))


def _shift_up(x, nxt, j, rows):
    t = x.shape[0]
    return jnp.where(rows >= t - j, pltpu.roll(nxt, t - j, 0), pltpu.roll(x, t - j, 0))


def conv_fwd(src, cb, w, b, *, width, act, name):
    t = src.shape[0]
    tt = min(ROW_TILE, t)
    n = t // tt

    def body(x_ref, w_ref, b_ref, y_ref, prev_ref):
        @pl.when(pl.program_id(0) == 0)
        def _():
            prev_ref[...] = jnp.zeros_like(prev_ref)

        x = x_ref[...]
        prev = prev_ref[...]
        rows = lax.broadcasted_iota(jnp.int32, x.shape, 0)
        wv = w_ref[...]
        y = b_ref[...] + wv[3:4, :] * x
        for j in (1, 2, 3):
            y = y + wv[3 - j:4 - j, :] * _shift_down(x, prev, j, rows)
        y_ref[...] = _silu(y) if act else y
        prev_ref[...] = x

    return pl.pallas_call(
        body, name=name, grid=(n,),
        in_specs=[pl.BlockSpec((tt, width), lambda i: (i, cb)),
                  pl.BlockSpec((4, width), lambda i: (0, 0)), pl.BlockSpec((1, width), lambda i: (0, 0))],
        out_specs=pl.BlockSpec((tt, width), lambda i: (i, 0)),
        out_shape=jax.ShapeDtypeStruct((t, width), F32),
        scratch_shapes=[pltpu.VMEM((tt, width), F32)],
        compiler_params=_params(("arbitrary",)),
    )(src, w, b)


def conv_bwd(src, cb, dy, w, b, *, width, act, name):
    t = src.shape[0]
    tt = min(ROW_TILE, t)
    n = t // tt

    def body(x_ref, xp_ref, dy_ref, w_ref, b_ref, dx_ref, dw_ref, db_ref, nxt_ref):
        i = pl.program_id(0)

        @pl.when(i == 0)
        def _():
            nxt_ref[...] = jnp.zeros_like(nxt_ref)
            dw_ref[...] = jnp.zeros_like(dw_ref)
            db_ref[...] = jnp.zeros_like(db_ref)

        x = x_ref[...]
        prev = jnp.where(i == n - 1, 0.0, xp_ref[...])
        rows = lax.broadcasted_iota(jnp.int32, x.shape, 0)
        wv = w_ref[...]
        xs = [x] + [_shift_down(x, prev, j, rows) for j in (1, 2, 3)]
        dpre = dy_ref[...]
        if act:
            pre = b_ref[...] + wv[3:4, :] * xs[0]
            for j in (1, 2, 3):
                pre = pre + wv[3 - j:4 - j, :] * xs[j]
            dpre = dpre * _dsilu(pre)
        nxt = nxt_ref[...]
        dx = wv[3:4, :] * dpre
        for j in (1, 2, 3):
            dx = dx + wv[3 - j:4 - j, :] * _shift_up(dpre, nxt, j, rows)
        dx_ref[...] = dx
        dw_ref[...] += jnp.concatenate([_colsum(dpre * xs[3 - kk]) for kk in range(4)], axis=0)
        db_ref[...] += _colsum(dpre)
        nxt_ref[...] = dpre

    return pl.pallas_call(
        body, name=name, grid=(n,),
        in_specs=[pl.BlockSpec((tt, width), lambda i: (n - 1 - i, cb)),
                  pl.BlockSpec((tt, width), lambda i: (jnp.maximum(n - 2 - i, 0), cb)),
                  pl.BlockSpec((tt, width), lambda i: (n - 1 - i, 0)),
                  pl.BlockSpec((4, width), lambda i: (0, 0)), pl.BlockSpec((1, width), lambda i: (0, 0))],
        out_specs=[pl.BlockSpec((tt, width), lambda i: (n - 1 - i, 0)),
                   pl.BlockSpec((4, width), lambda i: (0, 0)), pl.BlockSpec((1, width), lambda i: (0, 0))],
        out_shape=[jax.ShapeDtypeStruct((t, width), F32), jax.ShapeDtypeStruct((4, width), F32),
                   jax.ShapeDtypeStruct((1, width), F32)],
        scratch_shapes=[pltpu.VMEM((tt, width), F32)],
        compiler_params=_params(("arbitrary",)),
    )(src, src, dy, w, b)


S5_CW = 256


def _cmul(ar, ai, br, bi):
    return ar * br - ai * bi, ar * bi + ai * br


def _scan8_complex(src_ref, dst_ref, lam_ref, st_ref, *, w, nb, reverse):
    rows = lax.broadcasted_iota(jnp.int32, (8, S5_CW), 0)
    b8 = lambda v: jnp.broadcast_to(v, (8, S5_CW))

    def shift(x, k):
        if reverse:
            return jnp.where(rows < 8 - k, pltpu.roll(x, 8 - k, 0), 0.0)
        return jnp.where(rows >= k, pltpu.roll(x, k, 0), 0.0)

    for c0 in range(0, w, S5_CW):
        re, im = pl.ds(c0, S5_CW), pl.ds(w + c0, S5_CW)
        pw = [(lam_ref[:, re], lam_ref[:, im])]
        for _ in range(7):
            pw.append(_cmul(*pw[-1], *pw[0]))
        pr, pi = b8(pw[7][0]), b8(pw[7][1])
        for j in range(7):
            sel = rows == (7 - j if reverse else j)
            pr, pi = jnp.where(sel, b8(pw[j][0]), pr), jnp.where(sel, b8(pw[j][1]), pi)
        steps = [(k, b8(pw[k - 1][0]), b8(pw[k - 1][1])) for k in (1, 2, 4)]
        edge = 0 if reverse else 7

        def blk(i, carry):
            hr, hi = carry
            base = pl.multiple_of((nb - 1 - i if reverse else i) * 8, 8)
            xr, xi = src_ref[pl.ds(base, 8), re], src_ref[pl.ds(base, 8), im]
            for k, kr, ki in steps:
                sr, si = shift(xr, k), shift(xi, k)
                xr, xi = xr + kr * sr - ki * si, xi + kr * si + ki * sr
            xr, xi = xr + pr * hr - pi * hi, xi + pr * hi + pi * hr
            dst_ref[pl.ds(base, 8), re] = xr
            dst_ref[pl.ds(base, 8), im] = xi
            return b8(xr[edge:edge + 1, :]), b8(xi[edge:edge + 1, :])

        hr, hi = lax.fori_loop(0, nb, blk, (st_ref[:, re], st_ref[:, im]), unroll=2)
        st_ref[:, re] = hr
        st_ref[:, im] = hi


def s5_fwd(proj, bcat, lam, ccat, *, name):
    t = proj.shape[0]
    tt = min(ROW_TILE, t)
    w2 = bcat.shape[1]

    def body(u_ref, b_ref, lam_ref, c_ref, h_ref, y_ref, bu_ref, st_ref):
        @pl.when(pl.program_id(0) == 0)
        def _():
            st_ref[...] = jnp.zeros_like(st_ref)

        bu_ref[...] = _dot(u_ref[...], b_ref[...])
        _scan8_complex(bu_ref, h_ref, lam_ref, st_ref, w=w2 // 2, nb=tt // 8, reverse=False)
        y_ref[...] = _dot(h_ref[...], c_ref[...])

    fixed = lambda a: pl.BlockSpec(a.shape, lambda i: (0, 0))
    return pl.pallas_call(
        body, name=name, grid=(t // tt,),
        in_specs=[pl.BlockSpec((tt, S5_WIDTH), lambda i: (i, P_U // S5_WIDTH)), fixed(bcat), fixed(lam), fixed(ccat)],
        out_specs=[pl.BlockSpec((tt, w2), lambda i: (i, 0)), pl.BlockSpec((tt, S5_WIDTH), lambda i: (i, 0))],
        out_shape=[jax.ShapeDtypeStruct((t, w2), F32), jax.ShapeDtypeStruct((t, S5_WIDTH), F32)],
        scratch_shapes=[pltpu.VMEM((tt, w2), F32), pltpu.VMEM((8, w2), F32)],
        compiler_params=_params(("arbitrary",)),
    )(proj, bcat, lam, ccat)


def s5_bwd(dylin, du_a, hs, proj, bcat, lam_adj, ccat, *, name):
    t = proj.shape[0]
    tt = min(ROW_TILE, t)
    n, w2 = t // tt, bcat.shape[1]
    w = w2 // 2

    def body(dy_ref, dua_ref, h_ref, hp_ref, u_ref, b_ref, lam_ref, c_ref,
             du_ref, dc_ref, db_ref, dar_ref, dai_ref, g_ref, st_ref):
        i = pl.program_id(0)

        @pl.when(i == 0)
        def _():
            for r in (st_ref, dc_ref, db_ref, dar_ref, dai_ref):
                r[...] = jnp.zeros_like(r)

        dy, h = dy_ref[...], h_ref[...]
        g_ref[...] = _dot(dy, c_ref[...], _NT)
        dc_ref[...] += _dot(h, dy, _TN)
        _scan8_complex(g_ref, g_ref, lam_ref, st_ref, w=w, nb=tt // 8, reverse=True)
        g = g_ref[...]
        du_ref[...] = dua_ref[...] + _dot(g, b_ref[...], _NT)
        db_ref[...] += _dot(u_ref[...], g, _TN)
        rows = lax.broadcasted_iota(jnp.int32, (tt, w2), 0)
        before = jnp.where(i == n - 1, 0.0, hp_ref[7:8, :])
        hprev = jnp.where(rows == 0, before, pltpu.roll(h, 1, 0))
        gr, gi, hr, hi = g[:, :w], g[:, w:], hprev[:, :w], hprev[:, w:]
        dar_ref[...] += _colsum(gr * hr + gi * hi)
        dai_ref[...] += _colsum(gi * hr - gr * hi)

    rev = lambda i: n - 1 - i
    row = lambda wd, cb=0: pl.BlockSpec((tt, wd), lambda i: (rev(i), cb))
    fixed = lambda shape: pl.BlockSpec(shape, lambda i: (0, 0))
    return pl.pallas_call(
        body, name=name, grid=(n,),
        in_specs=[row(S5_WIDTH), row(S5_WIDTH), row(w2),
                  pl.BlockSpec((8, w2), lambda i: (jnp.maximum(rev(i) * (tt // 8) - 1, 0), 0)),
                  row(S5_WIDTH, P_U // S5_WIDTH), fixed(bcat.shape), fixed(lam_adj.shape), fixed(ccat.shape)],
        out_specs=[row(S5_WIDTH), fixed(ccat.shape), fixed(bcat.shape), fixed((1, w)), fixed((1, w))],
        out_shape=[jax.ShapeDtypeStruct((t, S5_WIDTH), F32), jax.ShapeDtypeStruct(ccat.shape, F32),
                   jax.ShapeDtypeStruct(bcat.shape, F32), jax.ShapeDtypeStruct((1, w), F32), jax.ShapeDtypeStruct((1, w), F32)],
        scratch_shapes=[pltpu.VMEM((tt, w2), F32), pltpu.VMEM((8, w2), F32)],
        compiler_params=_params(("arbitrary",)),
    )(dylin, du_a, hs, hs, proj, bcat, lam_adj, ccat)


def scan_real(a, b, *, reverse, name):
    t, w = b.shape
    tt = min(ROW_TILE, t)
    n, nb = t // tt, tt // 8

    def body(a_ref, b_ref, o_ref, st_ref):
        @pl.when(pl.program_id(0) == 0)
        def _():
            st_ref[...] = jnp.zeros_like(st_ref)

        rows = lax.broadcasted_iota(jnp.int32, (8, w), 0)

        def blk(i, h):
            base = pl.multiple_of((nb - 1 - i if reverse else i) * 8, 8)
            ta_, tb_ = a_ref[pl.ds(base, 8), :], b_ref[pl.ds(base, 8), :]
            out = jnp.zeros((8, w), F32)
            for j in (range(7, -1, -1) if reverse else range(8)):
                h = jnp.broadcast_to(ta_[j:j + 1, :], (8, w)) * h + jnp.broadcast_to(tb_[j:j + 1, :], (8, w))
                out = jnp.where(rows == j, h, out)
            o_ref[pl.ds(base, 8), :] = out
            return h

        st_ref[...] = lax.fori_loop(0, nb, blk, st_ref[...])

    idx = (lambda i: (n - 1 - i, 0)) if reverse else (lambda i: (i, 0))
    return pl.pallas_call(
        body, name=name, grid=(n,),
        in_specs=[pl.BlockSpec((tt, w), idx), pl.BlockSpec((tt, w), idx)],
        out_specs=pl.BlockSpec((tt, w), idx), out_shape=jax.ShapeDtypeStruct((t, w), F32),
        scratch_shapes=[pltpu.VMEM((8, w), F32)],
        compiler_params=_params(("arbitrary",)),
    )(a, b)


SSD_QQ = SSD_HEADS * SSD_CHUNK
SSD_GP = SSD_WIDTH // 2
SSD_GQ = SSD_QQ // 2


def _ssd_spread():
    h = jnp.arange(LANE)[:, None]
    spread_p = (jnp.arange(SSD_WIDTH)[None, :] // SSD_HEAD_DIM == h).astype(BF16)
    spread_q = (jnp.arange(SSD_QQ)[None, :] // SSD_CHUNK == h).astype(BF16)
    return spread_p, spread_q


def _ssd_prologue(dt_ref, prow_ref, sp_ref, sq_ref):
    q = SSD_CHUNK
    r = lax.broadcasted_iota(jnp.int32, (q, q), 0)
    c = lax.broadcasted_iota(jnp.int32, (q, q), 1)
    raw_c = dt_ref[...] + prow_ref[0:1, :]
    dt_c = _softplus(raw_c)
    a_r = -jnp.exp(prow_ref[1:2, :])
    cs_c = _dot_mask((r >= c).astype(F32), dt_c * a_r, mask_left=True, parts=3)
    both = _dot_mask(jnp.concatenate([dt_c, cs_c], axis=0), sp_ref[...], mask_left=False, parts=3)
    dt_x, cs_x = both[:q], both[q:]
    csx = _dot_mask(cs_c, sq_ref[...], mask_left=False, parts=3)
    rr = lax.broadcasted_iota(jnp.int32, (q, SSD_QQ), 0)
    ss = lax.broadcasted_iota(jnp.int32, (q, SSD_QQ), 1) & (q - 1)
    diag = rr == ss
    cs_row = _colsum(jnp.where(diag, csx, 0.0))
    lcat = jnp.exp(jnp.where(rr >= ss, csx - cs_row, -1e30))
    cl = cs_x[q - 1:q, :]
    return dict(raw_c=raw_c, dt_c=dt_c, a_r=a_r, dt_x=dt_x, cs_x=cs_x, lcat=lcat, diag=diag,
                ecs=jnp.exp(cs_x), wdec=jnp.exp(cl - cs_x), ecl=jnp.exp(cl), triu=(r <= c).astype(F32))


def _ssd_group(xbc_ref, g, lcat, xdt):
    ns, q = SSD_STATE, SSD_CHUNK
    bm = xbc_ref[:, pl.ds(SSD_WIDTH + g * ns, ns)]
    cm = xbc_ref[:, pl.ds(SSD_WIDTH + 2 * ns + g * ns, ns)]
    cb = _dot(cm, bm, _NT)
    lg = lcat[:, g * SSD_GQ:(g + 1) * SSD_GQ]
    wcat = jnp.concatenate([cb] * 4, axis=1) * lg
    head = lax.broadcasted_iota(jnp.int32, (1, SSD_GP), 1) // SSD_HEAD_DIM
    xg = xdt[:, g * SSD_GP:(g + 1) * SSD_GP]
    xbd = jnp.concatenate([jnp.where(head == j, xg, 0.0) for j in range(4)], axis=0)
    return bm, cm, lg, wcat, xbd, head


def _ssd_gate(yraw, z, nw):
    yg = yraw * _silu(z)
    r = lax.rsqrt(jnp.mean(yg * yg, axis=1, keepdims=True) + LN_EPS)
    return yg, r


def _ssd_specs(q, idx):
    return [pl.BlockSpec((q, SSD_XBC), lambda i: (idx(i), 0)),
            pl.BlockSpec((q, SSD_WIDTH), lambda i: (idx(i), P_Z // SSD_WIDTH)),
            pl.BlockSpec((q, LANE), lambda i: (idx(i), P_DT // LANE)),
            pl.BlockSpec((8, LANE), lambda i: (0, 0)), pl.BlockSpec((1, SSD_WIDTH), lambda i: (0, 0)),
            pl.BlockSpec((1, SSD_WIDTH), lambda i: (0, 0)),
            pl.BlockSpec((LANE, SSD_WIDTH), lambda i: (0, 0)), pl.BlockSpec((LANE, SSD_QQ), lambda i: (0, 0))]


def ssd_fwd(xbc, proj, prow, d_x, nw, *, name):
    t = xbc.shape[0]
    q, ns = SSD_CHUNK, SSD_STATE
    nc = t // q
    spread_p, spread_q = _ssd_spread()

    def body(xbc_ref, z_ref, dt_ref, prow_ref, dx_ref, nw_ref, sp_ref, sq_ref, y_ref, yraw_ref, sall_ref, s_ref):
        @pl.when(pl.program_id(0) == 0)
        def _():
            s_ref[...] = jnp.zeros_like(s_ref)

        sall_ref[0] = s_ref[...]
        pr = _ssd_prologue(dt_ref, prow_ref, sp_ref, sq_ref)
        xs = xbc_ref[:, pl.ds(0, SSD_WIDTH)]
        xdt = xs * pr["dt_x"]
        xw = xdt * pr["wdec"]
        ys = []
        for g in range(2):
            gp = slice(g * SSD_GP, (g + 1) * SSD_GP)
            bm, cm, lg, wcat, xbd, head = _ssd_group(xbc_ref, g, pr["lcat"], xdt)
            st = s_ref[:, gp]
            ys.append(_dot(wcat, xbd) + pr["ecs"][:, gp] * _dot(cm, st) + xs[:, gp] * dx_ref[:, gp])
            s_ref[:, gp] = pr["ecl"][:, gp] * st + _dot(bm, xw[:, gp], _TN)
        yraw = jnp.concatenate(ys, axis=1)
        yraw_ref[...] = yraw
        yg, r = _ssd_gate(yraw, z_ref[...], nw_ref[...])
        y_ref[...] = yg * r * nw_ref[...]

    row = pl.BlockSpec((q, SSD_WIDTH), lambda i: (i, 0))
    return pl.pallas_call(
        body, name=name, grid=(nc,),
        in_specs=_ssd_specs(q, lambda i: i),
        out_specs=[row, row, pl.BlockSpec((1, ns, SSD_WIDTH), lambda i: (i, 0, 0))],
        out_shape=[jax.ShapeDtypeStruct((t, SSD_WIDTH), F32), jax.ShapeDtypeStruct((t, SSD_WIDTH), F32),
                   jax.ShapeDtypeStruct((nc, ns, SSD_WIDTH), F32)],
        scratch_shapes=[pltpu.VMEM((ns, SSD_WIDTH), F32)],
        compiler_params=_params(("arbitrary",)),
    )(xbc, proj, proj, prow, d_x, nw, spread_p, spread_q)


def ssd_bwd(xbc, proj, prow, d_x, nw, yraw, sall, dout, *, name):
    t = xbc.shape[0]
    q, ns = SSD_CHUNK, SSD_STATE
    nc = t // q
    spread_p, spread_q = _ssd_spread()

    def body(xbc_ref, z_ref, dt_ref, prow_ref, dx_ref, nw_ref, sp_ref, sq_ref, yraw_ref, sall_ref, dout_ref,
             dxbc_ref, dz_ref, ddt_ref, dprm_ref, ddx_ref, dnw_ref, ds_ref):
        @pl.when(pl.program_id(0) == 0)
        def _():
            ds_ref[...] = jnp.zeros_like(ds_ref)
            dprm_ref[...] = jnp.zeros_like(dprm_ref)
            ddx_ref[...] = jnp.zeros_like(ddx_ref)
            dnw_ref[...] = jnp.zeros_like(dnw_ref)

        yraw, z, nwv, dout = yraw_ref[...], z_ref[...], nw_ref[...], dout_ref[...]
        yg, r = _ssd_gate(yraw, z, nwv)
        dnw_ref[...] += _colsum(dout * yg * r)
        dyn = dout * nwv
        dyg = r * dyn - yg * (r * r * r) * jnp.mean(dyn * yg, axis=1, keepdims=True)
        dy = dyg * _silu(z)
        dz_ref[...] = dyg * yraw * _dsilu(z)

        pr = _ssd_prologue(dt_ref, prow_ref, sp_ref, sq_ref)
        xs = xbc_ref[:, pl.ds(0, SSD_WIDTH)]
        xdt = xs * pr["dt_x"]
        wdec, ecl = pr["wdec"], pr["ecl"]
        xw = xdt * wdec
        dzm_all = pr["ecs"] * dy
        last = (lax.broadcasted_iota(jnp.int32, (q, 1), 0) == q - 1).astype(F32)
        dxs, dcsxs, es = [], [], []
        for g in range(2):
            gp = slice(g * SSD_GP, (g + 1) * SSD_GP)
            bm, cm, lg, wcat, xbd, head = _ssd_group(xbc_ref, g, pr["lcat"], xdt)
            dyg_ = dy[:, gp]
            dwcat = _dot(dyg_, xbd, _NT)
            dxbd = _dot(wcat, dyg_, _TN)
            dxg = sum(jnp.where(head == j, dxbd[j * q:(j + 1) * q], 0.0) for j in range(4))
            es.append(dwcat * wcat)
            dmm = dwcat * lg
            dm = dmm[:, 0:q] + dmm[:, q:2 * q] + dmm[:, 2 * q:3 * q] + dmm[:, 3 * q:4 * q]
            dcm = _dot(dm, bm)
            dbm = _dot(dm, cm, _TN)
            st = sall_ref[0, :, gp]
            zmat = _dot(cm, st)
            dzm = dzm_all[:, gp]
            dcm = dcm + _dot(dzm, st, _NT)
            dst = _dot(cm, dzm, _TN)
            dcsx = dzm * zmat
            dsn = ds_ref[:, gp]
            dst = dst + ecl[:, gp] * dsn
            dclx = _colsum(dsn * st) * ecl[:, gp]
            dxw = _dot(bm, dsn)
            dbm = dbm + _dot(xw[:, gp], dsn, _NT)
            dxg = dxg + wdec[:, gp] * dxw
            tw = dxw * xdt[:, gp] * wdec[:, gp]
            dclx = dclx + _colsum(tw)
            dcsxs.append(dcsx - tw + last * dclx)
            ds_ref[:, gp] = dst
            dxs.append(dxg)
            dxbc_ref[:, pl.ds(SSD_WIDTH + g * ns, ns)] = dbm
            dxbc_ref[:, pl.ds(SSD_WIDTH + 2 * ns + g * ns, ns)] = dcm
        dx = jnp.concatenate(dxs, axis=1)
        dxbc_ref[:, pl.ds(0, SSD_WIDTH)] = dx * pr["dt_x"] + dy * dx_ref[...]
        ddx_ref[...] += _colsum(dy * xs)
        red = _dot_mask(jnp.concatenate([jnp.concatenate(dcsxs, axis=1), dx * xs], axis=0), sp_ref[...], _NT,
                        mask_left=False, parts=2)
        e_all = jnp.concatenate(es, axis=1)
        e_red = _dot_mask(e_all - jnp.where(pr["diag"], _colsum(e_all), 0.0), sq_ref[...], _NT, mask_left=False, parts=2)
        dadt = _dot_mask(pr["triu"], red[:q] + e_red, mask_left=True, parts=2)
        draw = (red[q:] + dadt * pr["a_r"]) * _sigmoid(pr["raw_c"])
        ddt_ref[...] = draw
        zero = jnp.zeros((6, LANE), F32)
        dprm_ref[...] += jnp.concatenate([_colsum(draw), _colsum(dadt * pr["dt_c"]) * pr["a_r"], zero], axis=0)

    rev = lambda i: nc - 1 - i
    row = lambda w: pl.BlockSpec((q, w), lambda i: (rev(i), 0))
    fixed = lambda shape: pl.BlockSpec(shape, lambda i: (0, 0))
    return pl.pallas_call(
        body, name=name, grid=(nc,),
        in_specs=_ssd_specs(q, rev) + [row(SSD_WIDTH), pl.BlockSpec((1, ns, SSD_WIDTH), lambda i: (rev(i), 0, 0)),
                                       row(SSD_WIDTH)],
        out_specs=[row(SSD_XBC), row(SSD_WIDTH), row(LANE), fixed((8, LANE)), fixed((1, SSD_WIDTH)), fixed((1, SSD_WIDTH))],
        out_shape=[jax.ShapeDtypeStruct((t, SSD_XBC), F32), jax.ShapeDtypeStruct((t, SSD_WIDTH), F32),
                   jax.ShapeDtypeStruct((t, LANE), F32), jax.ShapeDtypeStruct((8, LANE), F32),
                   jax.ShapeDtypeStruct((1, SSD_WIDTH), F32), jax.ShapeDtypeStruct((1, SSD_WIDTH), F32)],
        scratch_shapes=[pltpu.VMEM((ns, SSD_WIDTH), F32)],
        compiler_params=_params(("arbitrary",)),
    )(xbc, proj, proj, prow, d_x, nw, spread_p, spread_q, yraw, sall, dout)


def _me():
    return lax.axis_index("x"), lax.axis_index("y"), lax.axis_index("c")


_ANY = pl.BlockSpec(memory_space=pl.ANY)
_MESH = pl.DeviceIdType.MESH


def all_gather(block, *, name):
    def body(src, dst, send_sems, recv_sems, local_sem):
        x, y, c = _me()
        me, sibling = (x, y, c), (x, y, 1 - c)
        chips = [(1 - x, y), (x, 1 - y), (1 - x, 1 - y)]

        def slot(px, py, pc):
            return dst.at[4 * px + 2 * py + pc]

        def copy(kk, blk, to, from_src=False):
            return pltpu.make_async_remote_copy(
                src_ref=src if from_src else slot(*blk), dst_ref=slot(*blk),
                send_sem=send_sems.at[kk], recv_sem=recv_sems.at[kk], device_id=to, device_id_type=_MESH)

        mine = pltpu.make_async_copy(src, slot(*me), local_sem)
        mine.start()
        first = [copy(0, me, sibling, True)] + [copy(1 + j, me, (*chip, c), True) for j, chip in enumerate(chips)]
        for cp in first:
            cp.start()
        passed = [copy(4 + j, (*chip, c), sibling) for j, chip in enumerate(chips)]
        for j, chip in enumerate(chips):
            copy(1 + j, (*chip, c), me).wait_recv()
            passed[j].start()
        copy(0, sibling, me).wait_recv()
        for j, chip in enumerate(chips):
            copy(4 + j, (*chip, 1 - c), me).wait_recv()
        for cp in first + passed:
            cp.wait_send()
        mine.wait()

    return pl.pallas_call(
        body, name=name, in_specs=[_ANY], out_specs=_ANY,
        out_shape=jax.ShapeDtypeStruct((N_DEV,) + block.shape, block.dtype),
        scratch_shapes=[pltpu.SemaphoreType.DMA((7,)), pltpu.SemaphoreType.DMA((7,)), pltpu.SemaphoreType.DMA(())],
    )(block)


RS_PIECES = 4


def rs_sibling_exchange(halves, *, name):
    _, nq, r, l = halves.shape
    rows = r // RS_PIECES
    assert r % RS_PIECES == 0 and rows % 16 == 0

    def body(src, dst, send_sems, recv_sems):
        x, y, c = _me()
        copies = []
        for q in range(nq):
            for i in range(RS_PIECES):
                kk = q * RS_PIECES + i
                cp = pltpu.make_async_remote_copy(
                    src_ref=src.at[1 - c, q, pl.ds(i * rows, rows)], dst_ref=dst.at[q, pl.ds(i * rows, rows)],
                    send_sem=send_sems.at[kk], recv_sem=recv_sems.at[kk], device_id=(x, y, 1 - c), device_id_type=_MESH)
                cp.start()
                copies.append(cp)
        for cp in copies:
            cp.wait()

    n_copies = nq * RS_PIECES
    return pl.pallas_call(
        body, name=name, in_specs=[_ANY], out_specs=_ANY,
        out_shape=jax.ShapeDtypeStruct((nq, r, l), halves.dtype),
        scratch_shapes=[pltpu.SemaphoreType.DMA((n_copies,)), pltpu.SemaphoreType.DMA((n_copies,))],
    )(halves)


def pair_sum_bf16(halves, theirs, *, name, tt=128):
    _, nq, r, wd = halves.shape
    tt = min(tt, r)
    parity = lax.axis_index("c").astype(jnp.int32).reshape(1)

    def body(c_ref, own_ref, sib_ref, o_ref):
        o_ref[...] = (own_ref[...] + sib_ref[...]).astype(BF16)

    return pl.pallas_call(
        body, name=name,
        grid_spec=pltpu.PrefetchScalarGridSpec(
            num_scalar_prefetch=1, grid=(nq, r // tt),
            in_specs=[pl.BlockSpec((None, None, tt, wd), lambda q, i, c: (c[0], q, i, 0)),
                      pl.BlockSpec((None, tt, wd), lambda q, i, c: (q, i, 0))],
            out_specs=pl.BlockSpec((None, tt, wd), lambda q, i, c: (q, i, 0))),
        out_shape=jax.ShapeDtypeStruct((nq, r, wd), BF16),
        compiler_params=_params(("parallel", "parallel")),
    )(parity, halves, theirs)


def rs_chip_exchange(part, *, name):
    def body(src, dst, send_sems, recv_sems, local_sem):
        x, y, c = _me()
        q_me = 2 * x + y
        local = pltpu.make_async_copy(src.at[q_me], dst.at[q_me], local_sem)
        local.start()
        copies = []
        for j, (px, py) in enumerate([(1 - x, y), (x, 1 - y), (1 - x, 1 - y)]):
            cp = pltpu.make_async_remote_copy(src_ref=src.at[2 * px + py], dst_ref=dst.at[q_me], send_sem=send_sems.at[j],
                                              recv_sem=recv_sems.at[j], device_id=(px, py, c), device_id_type=_MESH)
            cp.start()
            copies.append(cp)
        for cp in copies:
            cp.wait()
        local.wait()

    return pl.pallas_call(
        body, name=name, in_specs=[_ANY], out_specs=_ANY,
        out_shape=jax.ShapeDtypeStruct(part.shape, part.dtype),
        scratch_shapes=[pltpu.SemaphoreType.DMA((3,)), pltpu.SemaphoreType.DMA((3,)), pltpu.SemaphoreType.DMA(())],
    )(part)


def adamw(slabs, w, m, v, *, name, tt):
    ns, (r, wd) = slabs.shape[0], w.shape
    tt = min(tt, r)
    assert r % tt == 0

    def body(s_ref, w_ref, m_ref, v_ref, g_ref, d_ref, nm_ref, nv_ref):
        g = s_ref[0].astype(F32)
        for kdev in range(1, ns):
            g = g + s_ref[kdev].astype(F32)
        wv = w_ref[...]
        nm = ADAM_B1 * m_ref[...] + (1.0 - ADAM_B1) * g
        nv = ADAM_B2 * v_ref[...] + (1.0 - ADAM_B2) * (g * g)
        m_hat = nm / (1.0 - ADAM_B1 ** ADAM_STEP)
        v_hat = nv / (1.0 - ADAM_B2 ** ADAM_STEP)
        g_ref[...] = g
        d_ref[...] = -ADAM_LR * (m_hat / (jnp.sqrt(v_hat) + ADAM_EPS) + ADAM_WD * wv)
        nm_ref[...] = nm
        nv_ref[...] = nv

    spec = pl.BlockSpec((tt, wd), lambda i: (i, 0))
    return pl.pallas_call(
        body, name=name, grid=(r // tt,),
        in_specs=[pl.BlockSpec((ns, tt, wd), lambda i: (0, i, 0)), spec, spec, spec],
        out_specs=[spec] * 4, out_shape=[jax.ShapeDtypeStruct((r, wd), F32)] * 4,
        compiler_params=_params(("parallel",)),
    )(slabs, w, m, v)


WIDE = 1024
BIG = [("w_in", True, 289), ("w_out", False, 128), ("xa_wq", False, 128), ("xa_wk", False, 128), ("xa_wv", False, 128),
       ("xa_wo", False, 128), ("mlp_w2", False, 512), ("mlp_w1", True, 512)]
TINY = [("ssd_conv_w", 2), ("s5_glu_w", 1), ("rg_conv_w", 2)]
KEEP_F32 = ("ssd_conv_w", "rg_conv_w")
TINY_ROWS = 32
SHARDED = [name for name, _, _ in BIG] + [name for name, _ in TINY]
SMALL = ["ssd_conv_b", "ssd_dt_bias", "ssd_a_log", "ssd_d", "ssd_norm_w", "s5_lam_re", "s5_lam_im",
         "s5_log_step", "s5_b_re", "s5_b_im", "s5_c_re", "s5_c_im", "s5_d", "s5_glu_b", "rg_conv_b",
         "rg_wa", "rg_ba", "rg_wx", "rg_bx", "rg_lambda", "ln1_g", "ln1_b", "ln2_g", "ln2_b", "ln3_g", "ln3_b"]
WEIGHTS = ['w_in', 'w_out', 'ssd_conv_w', 'ssd_conv_b', 'ssd_dt_bias', 'ssd_a_log', 'ssd_d', 'ssd_norm_w',
           's5_lam_re', 's5_lam_im', 's5_log_step', 's5_b_re', 's5_b_im', 's5_c_re', 's5_c_im', 's5_d',
           's5_glu_w', 's5_glu_b', 'rg_conv_w', 'rg_conv_b', 'rg_wa', 'rg_ba', 'rg_wx', 'rg_bx', 'rg_lambda',
           'ln1_g', 'ln1_b', 'xa_wq', 'xa_wk', 'xa_wv', 'xa_wo', 'ln2_g', 'ln2_b', 'mlp_w1', 'mlp_w2',
           'ln3_g', 'ln3_b']


def _pad16(rows):
    return -(-rows // 16) * 16


def _pack_rows(flat, mult):
    n = flat.shape[-1]
    r = -(-n // (LANE * mult)) * mult
    pad = [(0, 0)] * (flat.ndim - 1) + [(0, r * LANE - n)]
    return jnp.pad(flat, pad).reshape(flat.shape[:-1] + (r, LANE))


def _unpack(packed, shapes):
    lead = packed.shape[:-2]
    flat = packed.reshape(lead + (-1,))
    out, off = [], 0
    for s in shapes:
        n = math.prod(s)
        out.append(flat[..., off:off + n].reshape(lead + tuple(s)))
        off += n
    return out


def _big_block(x, rows):
    pad = [(0, 0)] * (x.ndim - 2) + [(0, _pad16(rows) - rows), (0, 0)]
    x = jnp.pad(x, pad)
    return x.reshape(x.shape[:-3] + (DEPTH * _pad16(rows), WIDE))


def _tiny_block(flat):
    pad = [(0, 0)] * (flat.ndim - 1) + [(0, TINY_ROWS * WIDE - flat.shape[-1])]
    return jnp.pad(flat, pad).reshape(flat.shape[:-1] + (TINY_ROWS, WIDE))


def _pack_wide(big, tiny_flat):
    blocks = [_big_block(big[name], rows) for name, _, rows in BIG] + [_tiny_block(tiny_flat)]
    return jnp.concatenate(blocks, axis=-2)


def _unpack_wide(packed):
    lead, big, off = packed.shape[:-2], {}, 0
    for name, _, rows in BIG:
        rp = _pad16(rows)
        big[name] = packed[..., off:off + DEPTH * rp, :].reshape(lead + (DEPTH, rp, WIDE))[..., :rows, :]
        off += DEPTH * rp
    return big, packed[..., off:off + TINY_ROWS, :].reshape(lead + (TINY_ROWS * WIDE,))


def _split_flat(flat, shapes):
    out, off = [], 0
    for s in shapes:
        n = math.prod(s)
        out.append(flat[..., off:off + n].reshape(flat.shape[:-1] + tuple(s)))
        off += n
    return out


def _to_full(gathered, axis):
    g = jnp.moveaxis(gathered, 0, axis)
    s = g.shape
    return g.reshape(s[:axis] + (s[axis] * s[axis + 1],) + s[axis + 2:])


def _to_slabs(full, axis):
    s = full.shape
    g = full.reshape(s[:axis] + (N_DEV, s[axis] // N_DEV) + s[axis + 1:])
    return jnp.moveaxis(g, axis, 0)


def _blockdiag(w):
    h, i, j = w.shape
    eye = jnp.eye(h, dtype=w.dtype)
    return (w[:, :, None, :] * eye[:, None, :, None]).reshape(h * i, h * j)


def _blockdiag_extract(m, h):
    i, j = m.shape[0] // h, m.shape[1] // h
    eye = jnp.eye(h, dtype=m.dtype)
    return (m.reshape(h, i, h, j) * eye[:, None, :, None]).sum(axis=2)


def _s5_disc(lr, li, ls, bre, bim):
    step = jnp.exp(ls)[:, None]
    er = jnp.exp(lr * step)
    ar, ai = er * jnp.cos(li * step), er * jnp.sin(li * step)
    nr, ni, den = ar - 1.0, ai, lr * lr + li * li
    qr, qi = (nr * lr + ni * li) / den, (ni * lr - nr * li) / den
    bbr = qr[..., None] * bre - qi[..., None] * bim
    bbi = qr[..., None] * bim + qi[..., None] * bre
    return ar, ai, bbr, bbi


def _row(v, width=None):
    v = v.reshape(1, -1)
    if width is not None and v.shape[1] < width:
        v = jnp.pad(v, ((0, 0), (0, width - v.shape[1])))
    return v


def _relu2(a):
    r = jnp.maximum(a, 0.0)
    return r * r


def _add_alpha(acc, d):
    return acc + ALPHA * d


def _shift_rows_down(x):
    return jnp.concatenate([jnp.zeros((1, x.shape[1]), x.dtype), x[:-1]], axis=0)


def _shift_rows_up(x):
    return jnp.concatenate([x[1:], jnp.zeros((1, x.shape[1]), x.dtype)], axis=0)


def _layer_params(full, small, l):
    p = {}
    w_in = full["w_in"][l]
    z, xbc, dt, u, xr, g = w_in[0:512], w_in[512:1536], w_in[1536:1544], w_in[1544:1800], w_in[1800:2056], w_in[2056:2312]
    p["w_inp"] = jnp.concatenate([xbc, z, u, xr, g, dt, jnp.zeros((D_INP - P_DT - 8, D_MODEL), w_in.dtype)], axis=0)
    for k_ in ("w_out", "xa_wq", "xa_wk", "xa_wv", "xa_wo", "mlp_w1", "mlp_w2", "s5_glu_w"):
        p[k_] = full[k_][l]
    p["ssd_cw"], p["ssd_cb"] = full["ssd_conv_w"][l], _row(small["ssd_conv_b"][l])
    dtb, alog, dsk = small["ssd_dt_bias"][l], small["ssd_a_log"][l], small["ssd_d"][l]
    p["prow"] = jnp.concatenate([_row(dtb, LANE), _row(alog, LANE), jnp.zeros((6, LANE), F32)], axis=0)
    p["ssd_dx"] = _row(jnp.repeat(dsk, SSD_HEAD_DIM))
    p["ssd_nw"] = _row(small["ssd_norm_w"][l])
    s5_in = (small["s5_lam_re"][l], small["s5_lam_im"][l], small["s5_log_step"][l], small["s5_b_re"][l], small["s5_b_im"][l])
    (ar, ai, bbr, bbi), p["s5_vjp"] = jax.vjp(_s5_disc, *s5_in)
    p["lam_fwd"] = jnp.concatenate([_row(ar), _row(ai)], axis=1)
    p["lam_adj"] = jnp.concatenate([_row(ar), _row(-ai)], axis=1)
    p["bcat"] = jnp.concatenate([_blockdiag(jnp.swapaxes(bbr, 1, 2)), _blockdiag(jnp.swapaxes(bbi, 1, 2))], axis=1)
    p["ccat"] = jnp.concatenate([_blockdiag(jnp.swapaxes(small["s5_c_re"][l], 1, 2)),
                                 -_blockdiag(jnp.swapaxes(small["s5_c_im"][l], 1, 2))], axis=0)
    p["s5_d"], p["s5_glu_b"] = _row(small["s5_d"][l]), _row(small["s5_glu_b"][l])
    p["rg_cw"], p["rg_cb"] = full["rg_conv_w"][l], _row(small["rg_conv_b"][l])
    p["rg_wa"], p["rg_wx"] = _blockdiag(small["rg_wa"][l]), _blockdiag(small["rg_wx"][l])
    p["rg_ba"], p["rg_bx"], p["rg_lam"] = _row(small["rg_ba"][l]), _row(small["rg_bx"][l]), _row(small["rg_lambda"][l])
    for i in (1, 2, 3):
        p[f"g{i}"], p[f"b{i}"] = _row(small[f"ln{i}_g"][l]), _row(small[f"ln{i}_b"][l])
    return p


def _layer_fwd(h0, mem, p):
    t = h0.shape[0]
    s = {"h0": h0}
    proj = mm(h0, p["w_inp"], tb=True, name="in_proj")
    xbc = conv_fwd(proj, 0, p["ssd_cw"], p["ssd_cb"], width=SSD_XBC, act=True, name="ssd_conv_fwd")
    y_ssd, yraw, sall = ssd_fwd(xbc, proj, p["prow"], p["ssd_dx"], p["ssd_nw"], name="ssd_fwd")
    hs5, ylin = s5_fwd(proj, p["bcat"], p["lam_fwd"], p["ccat"], name="s5_fwd")
    (y_s5,), _ = rowk(_s5_post_fwd_fn, [(ylin, S5_WIDTH, 0), (proj, S5_WIDTH, P_U // S5_WIDTH)],
                      [p["s5_d"], p["s5_glu_w"], p["s5_glu_b"]], [S5_WIDTH], [], rows=t, name="s5_post_fwd")
    xc = conv_fwd(proj, P_XR // RG_WIDTH, p["rg_cw"], p["rg_cb"], width=RG_WIDTH, act=False, name="rg_conv_fwd")
    rg_full = [p["rg_wa"], p["rg_wx"], p["rg_ba"], p["rg_bx"], p["rg_lam"]]
    (a_rg, b_rg), _ = rowk(_rg_pre_fwd_fn, [(xc, RG_WIDTH, 0)], rg_full, [RG_WIDTH, RG_WIDTH], [], rows=t, name="rg_pre_fwd")
    h_rg = scan_real(a_rg, b_rg, reverse=False, name="rg_scan_fwd")
    (y_rg,), _ = rowk(_rg_out_fwd_fn, [(h_rg, RG_WIDTH, 0), (proj, RG_WIDTH, P_G // RG_WIDTH)], [], [RG_WIDTH], [],
                      rows=t, name="rg_out_fwd")
    ycat = jnp.concatenate([y_ssd, y_s5, y_rg], axis=1)
    mix = mm(ycat, p["w_out"], name="out_proj")
    h1 = ln_fwd(h0, mix, p["g1"], p["b1"], name="ln_fwd")
    q = mm(h1, p["xa_wq"], name="xa_q")
    k = mm(mem, p["xa_wk"], name="xa_kv")
    v = mm(mem, p["xa_wv"], name="xa_kv")
    (o,), _ = rowk(_attn_fwd_fn, [(q, D_MODEL, 0)], [k, v], [D_MODEL], [], rows=t, name="xa_fwd")
    att = mm(o, p["xa_wo"], name="xa_o")
    h2 = ln_fwd(h1, att, p["g2"], p["b2"], name="ln_fwd")
    a_mlp = mm(h2, p["mlp_w1"], tb=True, name="mlp_up")
    m_out = mm(a_mlp, p["mlp_w2"], fa=_relu2, name="mlp_down")
    h3 = ln_fwd(h2, m_out, p["g3"], p["b3"], name="ln_fwd")
    s.update(proj=proj, xbc=xbc, yraw=yraw, sall=sall, hs5=hs5, ylin=ylin, xc=xc, a_rg=a_rg, h_rg=h_rg,
             ycat=ycat, mix=mix, h1=h1, q=q, k=k, v=v, o=o, att=att, h2=h2, a_mlp=a_mlp, m_out=m_out)
    return h3, s


def _layer_bwd(dh3, mem, p, s, l, gfull, gsmall):
    t = dh3.shape[0]
    proj = s["proj"]
    dpre3, dg3, db3 = ln_bwd(s["h2"], s["m_out"], dh3, p["g3"], name="ln_bwd")
    da = mm(dpre3, p["mlp_w2"], tb=True, o_extra=(s["a_mlp"],), fo=lambda acc, a: acc * 2.0 * jnp.maximum(a, 0.0), name="mlp_da")
    gfull["mlp_w2"][l] = mm(s["a_mlp"], dpre3, ta=True, fa=_relu2, name="mlp_dw2")
    gfull["mlp_w1"][l] = mm(da, s["h2"], ta=True, name="mlp_dw1")
    dh2 = mm(da, p["mlp_w1"], o_extra=(dpre3,), fo=_add_alpha, name="mlp_dx")
    dpre2, dg2, db2 = ln_bwd(s["h1"], s["att"], dh2, p["g2"], name="ln_bwd")
    do = mm(dpre2, p["xa_wo"], tb=True, name="xa_do")
    gfull["xa_wo"][l] = mm(s["o"], dpre2, ta=True, name="dw_sq")
    (dq,), (dk, dv) = rowk(_attn_bwd_fn, [(s["q"], D_MODEL, 0), (do, D_MODEL, 0)], [s["k"], s["v"]], [D_MODEL],
                           [(256, D_MODEL), (256, D_MODEL)], rows=t, name="xa_bwd")
    gfull["xa_wq"][l] = mm(s["h1"], dq, ta=True, name="dw_sq")
    gfull["xa_wk"][l] = mm(mem, dk, ta=True, name="dw_kv")
    gfull["xa_wv"][l] = mm(mem, dv, ta=True, name="dw_kv")
    dh1 = mm(dq, p["xa_wq"], tb=True, o_extra=(dpre2,), fo=_add_alpha, name="dx_sq")
    dpre1, dg1, db1 = ln_bwd(s["h0"], s["mix"], dh1, p["g1"], name="ln_bwd")
    dycat = mm(dpre1, p["w_out"], tb=True, name="xa_do")
    gfull["w_out"][l] = mm(s["ycat"], dpre1, ta=True, name="dw_sq")
    (dh_rg, dg_rg), _ = rowk(_rg_out_bwd_fn, [(s["h_rg"], RG_WIDTH, 0), (proj, RG_WIDTH, P_G // RG_WIDTH), (dycat, RG_WIDTH, 3)],
                             [], [RG_WIDTH, RG_WIDTH], [], rows=t, name="rg_out_bwd")
    g_rg = scan_real(_shift_rows_up(s["a_rg"]), dh_rg, reverse=True, name="rg_scan_bwd")
    rg_full = [p["rg_wa"], p["rg_wx"], p["rg_ba"], p["rg_bx"], p["rg_lam"]]
    (dxc,), (dwa, dwx, dba, dbx, dlam) = rowk(
        _rg_pre_bwd_fn, [(s["xc"], RG_WIDTH, 0), (g_rg, RG_WIDTH, 0), (_shift_rows_down(s["h_rg"]), RG_WIDTH, 0)], rg_full,
        [RG_WIDTH], [(RG_WIDTH, RG_WIDTH), (RG_WIDTH, RG_WIDTH), (1, RG_WIDTH), (1, RG_WIDTH), (1, RG_WIDTH)],
        rows=t, name="rg_pre_bwd")
    dxr, d_rgcw, d_rgcb = conv_bwd(proj, P_XR // RG_WIDTH, dxc, p["rg_cw"], p["rg_cb"], width=RG_WIDTH, act=False, name="rg_conv_bwd")
    (dylin, du_a), (d_s5d, d_gluw, d_glub) = rowk(
        _s5_post_bwd_fn, [(s["ylin"], S5_WIDTH, 0), (proj, S5_WIDTH, P_U // S5_WIDTH), (dycat, S5_WIDTH, 2)],
        [p["s5_d"], p["s5_glu_w"], p["s5_glu_b"]], [S5_WIDTH, S5_WIDTH],
        [(1, S5_WIDTH), (S5_WIDTH, S5_WIDTH), (1, S5_WIDTH)], rows=t, name="s5_post_bwd")
    du, dccat, dbcat, dar, dai = s5_bwd(dylin, du_a, s["hs5"], proj, p["bcat"], p["lam_adj"], p["ccat"], name="s5_bwd")
    dxbc_act, dz, ddt, dprm, ddx, dnw = ssd_bwd(s["xbc"], proj, p["prow"], p["ssd_dx"], p["ssd_nw"], s["yraw"], s["sall"], dycat,
                                               name="ssd_bwd")
    dxbc, d_scw, d_scb = conv_bwd(proj, 0, dxbc_act, p["ssd_cw"], p["ssd_cb"], width=SSD_XBC, act=True, name="ssd_conv_bwd")
    dproj = jnp.concatenate([dxbc, dz, du, dxr, dg_rg, ddt, jnp.zeros((t, D_INP - P_DT - LANE), F32)], axis=1)
    dh0 = mm(dproj, p["w_inp"], o_extra=(dpre1,), fo=_add_alpha, name="in_proj_dx")
    dwp = mm(dproj, s["h0"], ta=True, name="in_proj_dw")
    gfull["w_in"][l] = jnp.concatenate([dwp[P_Z:P_Z + 512], dwp[P_XBC:P_XBC + 1024], dwp[P_DT:P_DT + 8],
                                        dwp[P_U:P_U + 256], dwp[P_XR:P_XR + 256], dwp[P_G:P_G + 256]], axis=0)
    gfull["ssd_conv_w"][l], gfull["rg_conv_w"][l], gfull["s5_glu_w"][l] = d_scw, d_rgcw, d_gluw
    ng, ns = S5_GROUPS, S5_STATE
    dbbr = jnp.swapaxes(_blockdiag_extract(dbcat[:, :S5_NSTATE], ng), 1, 2)
    dbbi = jnp.swapaxes(_blockdiag_extract(dbcat[:, S5_NSTATE:], ng), 1, 2)
    d_lr, d_li, d_ls, d_bre, d_bim = p["s5_vjp"]((dar.reshape(ng, ns), dai.reshape(ng, ns), dbbr, dbbi))
    gsmall["s5_lam_re"][l], gsmall["s5_lam_im"][l], gsmall["s5_log_step"][l] = d_lr, d_li, d_ls
    gsmall["s5_b_re"][l], gsmall["s5_b_im"][l] = d_bre, d_bim
    gsmall["s5_c_re"][l] = jnp.swapaxes(_blockdiag_extract(dccat[:S5_NSTATE], ng), 1, 2)
    gsmall["s5_c_im"][l] = -jnp.swapaxes(_blockdiag_extract(dccat[S5_NSTATE:], ng), 1, 2)
    gsmall["s5_d"][l], gsmall["s5_glu_b"][l] = d_s5d[0], d_glub[0]
    gsmall["ssd_conv_b"][l], gsmall["rg_conv_b"][l] = d_scb[0], d_rgcb[0]
    gsmall["ssd_dt_bias"][l], gsmall["ssd_a_log"][l] = dprm[0, :8], dprm[1, :8]
    gsmall["ssd_d"][l] = ddx.reshape(SSD_HEADS, SSD_HEAD_DIM).sum(axis=1)
    gsmall["ssd_norm_w"][l] = dnw[0]
    gsmall["rg_wa"][l], gsmall["rg_wx"][l] = _blockdiag_extract(dwa, RG_BLOCKS), _blockdiag_extract(dwx, RG_BLOCKS)
    gsmall["rg_ba"][l], gsmall["rg_bx"][l] = dba.reshape(RG_BLOCKS, RG_BLOCK_DIM), dbx.reshape(RG_BLOCKS, RG_BLOCK_DIM)
    gsmall["rg_lambda"][l] = dlam[0]
    for i, (dg, db) in zip((1, 2, 3), ((dg1, db1), (dg2, db2), (dg3, db3))):
        gsmall[f"ln{i}_g"][l], gsmall[f"ln{i}_b"][l] = dg[0], db[0]
    return dh0


def _step(a):
    h = a["x"][0]
    mem = a["mem"][0]
    t = h.shape[0]

    def my_shards(pre):
        return ({name: (jnp.swapaxes(a[pre + name], 1, 2) if tr else a[pre + name]) for name, tr, _ in BIG},
                [a[pre + name] for name, _ in TINY])

    big, tiny = my_shards("")
    tiny16 = [(lax.bitcast_convert_type(w, BF16) if name in KEEP_F32 else w.astype(BF16)).reshape(-1)
              for (name, _), w in zip(TINY, tiny)]
    packed = _pack_wide({name: w.astype(BF16) for name, w in big.items()}, jnp.concatenate(tiny16))
    gbig, gtiny = _unpack_wide(all_gather(packed, name="ag_weights"))
    full = {name: _to_full(gbig[name], 1) for name, _, _ in BIG}
    tiny_shapes = [w.shape + ((2,) if name in KEEP_F32 else ()) for (name, _), w in zip(TINY, tiny)]
    for (name, axis), g in zip(TINY, _split_flat(gtiny, tiny_shapes)):
        full[name] = _to_full(lax.bitcast_convert_type(g, F32) if name in KEEP_F32 else g, axis)
    small = {name: a[name] for name in SMALL}
    params, saved = [], []
    for l in range(DEPTH):
        p = _layer_params(full, small, l)
        h, s = _layer_fwd(h, mem, p)
        params.append(p)
        saved.append(s)
    (dh,), (loss_part,) = rowk(_loss_fn, [(h, D_MODEL, 0), (a["loss_target"][0], D_MODEL, 0)], [], [D_MODEL], [(1, 1)],
                               rows=t, name="loss_head")
    loss = lax.psum(loss_part[0, 0], ("x", "y", "c"))
    gfull = {name: [None] * DEPTH for name in SHARDED}
    gsmall = {name: [None] * DEPTH for name in SMALL}
    for l in reversed(range(DEPTH)):
        dh = _layer_bwd(dh, mem, params[l], saved[l], l, gfull, gsmall)
    grad_x = dh[None]
    gbig = {name: jnp.stack([g.reshape(N_DEV, rows, WIDE) for g in gfull[name]], axis=1) for name, _, rows in BIG}
    gtiny = jnp.concatenate([_to_slabs(jnp.stack(gfull[name]), axis).reshape(N_DEV, -1) for name, axis in TINY], axis=1)
    slabs = _pack_wide(gbig, gtiny)
    halves = jnp.swapaxes(slabs.reshape((4, 2) + slabs.shape[1:]), 0, 1)
    theirs = rs_sibling_exchange(halves, name="rs_sibling")
    slabs = rs_chip_exchange(pair_sum_bf16(halves, theirs, name="rs_pair_sum"), name="rs_chips")

    def pk(pre):
        big, tiny = my_shards(pre)
        return _pack_wide(big, jnp.concatenate([w.reshape(-1) for w in tiny]))

    bigs = adamw(slabs, pk(""), pk("m_"), pk("v_"), name="adamw_sharded", tt=128)
    gs = _pack_rows(jnp.concatenate([jnp.stack(gsmall[name]).reshape(-1) for name in SMALL]), 8)
    gs = all_gather(gs, name="ag_small_grads")
    pks = lambda pre: _pack_rows(jnp.concatenate([a[pre + name].reshape(-1) for name in SMALL]), 8)
    sm = adamw(gs, pks(""), pks("m_"), pks("v_"), name="adamw_replicated", tt=gs.shape[1])
    out = {}
    for kind, bg, sg in zip(("grad_", "delta_", "new_m_", "new_v_"), bigs, sm):
        obig, otiny = _unpack_wide(bg)
        for name, tr, _ in BIG:
            out[kind + name] = jnp.swapaxes(obig[name], 1, 2) if tr else obig[name]
        for (name, _), arr in zip(TINY, _split_flat(otiny, [w.shape for w in tiny])):
            out[kind + name] = arr
        for name, arr in zip(SMALL, _unpack(sg, [a[name].shape for name in SMALL])):
            out[kind + name] = arr
    return (loss, grad_x) + tuple(out[kind + name] for kind in ("grad_", "delta_", "new_m_", "new_v_") for name in WEIGHTS)


def kernel(x, mem, w_in, w_out, ssd_conv_w, ssd_conv_b, ssd_dt_bias, ssd_a_log, ssd_d, ssd_norm_w, s5_lam_re, s5_lam_im, s5_log_step, s5_b_re, s5_b_im, s5_c_re, s5_c_im, s5_d, s5_glu_w, s5_glu_b, rg_conv_w, rg_conv_b, rg_wa, rg_ba, rg_wx, rg_bx, rg_lambda, ln1_g, ln1_b, xa_wq, xa_wk, xa_wv, xa_wo, ln2_g, ln2_b, mlp_w1, mlp_w2, ln3_g, ln3_b, loss_target, m_w_in, m_w_out, m_ssd_conv_w, m_ssd_conv_b, m_ssd_dt_bias, m_ssd_a_log, m_ssd_d, m_ssd_norm_w, m_s5_lam_re, m_s5_lam_im, m_s5_log_step, m_s5_b_re, m_s5_b_im, m_s5_c_re, m_s5_c_im, m_s5_d, m_s5_glu_w, m_s5_glu_b, m_rg_conv_w, m_rg_conv_b, m_rg_wa, m_rg_ba, m_rg_wx, m_rg_bx, m_rg_lambda, m_ln1_g, m_ln1_b, m_xa_wq, m_xa_wk, m_xa_wv, m_xa_wo, m_ln2_g, m_ln2_b, m_mlp_w1, m_mlp_w2, m_ln3_g, m_ln3_b, v_w_in, v_w_out, v_ssd_conv_w, v_ssd_conv_b, v_ssd_dt_bias, v_ssd_a_log, v_ssd_d, v_ssd_norm_w, v_s5_lam_re, v_s5_lam_im, v_s5_log_step, v_s5_b_re, v_s5_b_im, v_s5_c_re, v_s5_c_im, v_s5_d, v_s5_glu_w, v_s5_glu_b, v_rg_conv_w, v_rg_conv_b, v_rg_wa, v_rg_ba, v_rg_wx, v_rg_bx, v_rg_lambda, v_ln1_g, v_ln1_b, v_xa_wq, v_xa_wk, v_xa_wv, v_xa_wo, v_ln2_g, v_ln2_b, v_mlp_w1, v_mlp_w2, v_ln3_g, v_ln3_b):
    return _step(dict(locals()))
```

```python
import math

import jax
import jax.numpy as jnp
from jax import lax
from jax.experimental import pallas as pl
from jax.experimental.pallas import tpu as pltpu

F32 = jnp.float32
BF16 = jnp.bfloat16

N_DEV = 8
D_MODEL = 1024
DEPTH = 2
SSD_WIDTH = 512
SSD_HEADS = 8
SSD_HEAD_DIM = 64
SSD_STATE = 128
SSD_CHUNK = 128
SSD_XBC = 1024
S5_WIDTH = 256
S5_GROUPS = 16
S5_GROUP_CH = 16
S5_STATE = 64
S5_NSTATE = S5_GROUPS * S5_STATE
RG_WIDTH = 256
RG_BLOCKS = 4
RG_BLOCK_DIM = 64
RG_C = 8.0
XA_HEADS = 4
XA_HEAD_DIM = 256
ALPHA = (2.0 * DEPTH) ** 0.25
LN_EPS = 1e-5
ADAM_LR, ADAM_B1, ADAM_B2, ADAM_EPS, ADAM_WD, ADAM_STEP = 0.001, 0.9, 0.999, 1e-08, 0.01, 10

P_XBC, P_Z, P_U, P_XR, P_G, P_DT = 0, 1024, 1536, 1792, 2048, 2304
D_INP = 2560
LANE = 128
VMEM_LIMIT = 56 * 1024 * 1024
ROW_TILE = 512

_NN = ((1,), (0,))
_NT = ((1,), (1,))
_TN = ((0,), (0,))


def _dot(a, b, dims=_NN):
    return lax.dot_general(a.astype(BF16), b.astype(BF16), (dims, ((), ())), preferred_element_type=F32)


def _split_bf16(x, parts):
    out, rem = [], x
    for _ in range(parts):
        piece = rem.astype(BF16)
        out.append(piece)
        rem = rem - piece.astype(F32)
    return out


def _dot_mask(a, b, dims=_NN, *, mask_left, parts):
    if mask_left:
        return sum(_dot(a, piece, dims) for piece in _split_bf16(b, parts))
    return sum(_dot(piece, b, dims) for piece in _split_bf16(a, parts))


def _sigmoid(x):
    return 1.0 / (1.0 + jnp.exp(-x))


def _silu(x):
    return x * _sigmoid(x)


def _dsilu(x):
    s = _sigmoid(x)
    return s * (1.0 + x * (1.0 - s))


_GK = math.sqrt(2.0 / math.pi)
_GC = 0.044715


def _gelu(x):
    return 0.5 * x * (1.0 + jnp.tanh(_GK * (x + _GC * x * x * x)))


def _dgelu(x):
    th = jnp.tanh(_GK * (x + _GC * x * x * x))
    return 0.5 * (1.0 + th) + 0.5 * x * (1.0 - th * th) * _GK * (1.0 + 3.0 * _GC * x * x)


def _log1p_pos(e):
    return jnp.where(e < 1e-2, e * (1.0 - e * (0.5 - e * (1.0 / 3.0))), jnp.log(1.0 + e))


def _softplus(x):
    return jnp.maximum(x, 0.0) + _log1p_pos(jnp.exp(-jnp.abs(x)))


def _neg_expm1(x):
    poly = -x * (1.0 + x * (0.5 + x * (1.0 / 6.0 + x * (1.0 / 24.0 + x * (1.0 / 120.0)))))
    return jnp.where(x > -0.05, poly, 1.0 - jnp.exp(x))


def _params(sem):
    return pltpu.CompilerParams(dimension_semantics=sem, vmem_limit_bytes=VMEM_LIMIT)


RESIDENT_BYTES = 8 * 1024 * 1024
STREAM_BYTES = 4 * 1024 * 1024


def _halve_to_fit(dims, bytes_per, limit):
    dims = list(dims)
    while math.prod(dims) * bytes_per > limit:
        i = max(range(len(dims)), key=lambda d: dims[d])
        assert dims[i] % 256 == 0, dims
        dims[i] //= 2
    return dims


def mm(a, b, *, name, ta=False, tb=False, a_extra=(), fa=None, o_extra=(), r_extra=(), fo=None, n_out=1,
       a_off=0, m=None, k=None):
    n = b.shape[0] if tb else b.shape[1]
    na, no, nr = 1 + len(a_extra), len(o_extra), len(r_extra)
    if not ta:
        assert m is None
        m, kdim = a.shape[0], (a.shape[1] if k is None else k)
        assert a_off % kdim == 0
        (tn,) = _halve_to_fit([n], kdim * b.dtype.itemsize, RESIDENT_BYTES)
        (tm,) = _halve_to_fit([min(512, m)], max(tn, kdim) * 4, STREAM_BYTES)
        a_spec = pl.BlockSpec((tm, kdim), lambda i, j: (i, a_off // kdim))
        b_spec = pl.BlockSpec((tn, kdim), lambda i, j: (j, 0)) if tb else pl.BlockSpec((kdim, tn), lambda i, j: (0, j))
        o_spec = pl.BlockSpec((tm, tn), lambda i, j: (i, j))
        dims = _NT if tb else _NN

        r_spec = pl.BlockSpec((1, tn), lambda i, j: (0, j))

        def body(*refs):
            a_refs, b_ref, e_refs, out_refs = refs[:na], refs[na], refs[na + 1:na + 1 + no + nr], refs[na + 1 + no + nr:]
            av = a_refs[0][...] if fa is None else fa(*[r[...] for r in a_refs])
            acc = _dot(av, b_ref[...], dims)
            res = acc if fo is None else fo(acc, *[r[...] for r in e_refs])
            for r, v in zip(out_refs, res if n_out > 1 else (res,)):
                r[...] = v

        grid, sem = (m // tm, n // tn), ("parallel", "parallel")
    else:
        assert k is None and not tb and fo is None and not o_extra and not r_extra and n_out == 1
        kdim, m = a.shape[0], (a.shape[1] if m is None else m)
        r_spec = None
        tm, tn = _halve_to_fit([m, n], 4, RESIDENT_BYTES)
        (tk,) = _halve_to_fit([min(512, kdim)], max(tm, tn) * 4, STREAM_BYTES)
        assert a_off % tm == 0
        a_spec = pl.BlockSpec((tk, tm), lambda i, j, kk: (kk, i + a_off // tm))
        b_spec = pl.BlockSpec((tk, tn), lambda i, j, kk: (kk, j))
        o_spec = pl.BlockSpec((tm, tn), lambda i, j, kk: (i, j))

        def body(*refs):
            a_refs, b_ref, out_ref = refs[:na], refs[na], refs[na + 1]

            @pl.when(pl.program_id(2) == 0)
            def _():
                out_ref[...] = jnp.zeros_like(out_ref)

            av = a_refs[0][...] if fa is None else fa(*[r[...] for r in a_refs])
            out_ref[...] += _dot(av, b_ref[...], _TN)

        grid, sem = (m // tm, n // tn, kdim // tk), ("parallel", "parallel", "arbitrary")
    assert m % tm == 0 and n % tn == 0, (name, m, n, tm, tn)
    out = jax.ShapeDtypeStruct((m, n), F32)
    return pl.pallas_call(
        body, name=name, grid=grid,
        in_specs=[a_spec] * na + [b_spec] + [o_spec] * no + [r_spec] * nr,
        out_specs=o_spec if n_out == 1 else [o_spec] * n_out, out_shape=out if n_out == 1 else [out] * n_out,
        compiler_params=_params(sem),
    )(a, *a_extra, b, *o_extra, *r_extra)


def rowk(fn, tiled, full, out_w, acc_shapes, *, rows, name):
    tt = min(ROW_TILE, rows)
    n = rows // tt
    assert rows % tt == 0
    nt, nf, no = len(tiled), len(full), len(out_w)

    def tspec(w, cb):
        return pl.BlockSpec((tt, w), lambda i: (i, cb))

    def fspec(a):
        nd = a.ndim
        return pl.BlockSpec(a.shape, lambda i: (0,) * nd)

    def body(*refs):
        ins, fulls = refs[:nt], refs[nt:nt + nf]
        outs, accs = refs[nt + nf:nt + nf + no], refs[nt + nf + no:]
        res_t, res_a = fn(*[r[...] for r in ins], *[r[...] for r in fulls])
        for r, v in zip(outs, res_t):
            r[...] = v
        if accs:
            @pl.when(pl.program_id(0) == 0)
            def _():
                for r in accs:
                    r[...] = jnp.zeros_like(r)
            for r, v in zip(accs, res_a):
                r[...] += v

    outs = pl.pallas_call(
        body, name=name, grid=(n,),
        in_specs=[tspec(w, cb) for (_, w, cb) in tiled] + [fspec(a) for a in full],
        out_specs=[tspec(w, 0) for w in out_w] + [pl.BlockSpec(s, lambda i, nd=len(s): (0,) * nd) for s in acc_shapes],
        out_shape=[jax.ShapeDtypeStruct((rows, w), F32) for w in out_w] + [jax.ShapeDtypeStruct(s, F32) for s in acc_shapes],
        compiler_params=_params(("arbitrary",)),
    )(*[a for (a, _, _) in tiled], *full)
    return outs[:no], outs[no:]


def _colsum(x):
    return jnp.sum(x, axis=0, keepdims=True)


def _rowsum(x):
    return jnp.sum(x, axis=1, keepdims=True)


def _ln_epilogue(acc, resid, g, b):
    pre = ALPHA * resid + acc
    mu = jnp.mean(pre, axis=1, keepdims=True)
    xc = pre - mu
    var = jnp.mean(xc * xc, axis=1, keepdims=True)
    return pre, xc * lax.rsqrt(var + LN_EPS) * g + b


def _ln_bwd_fn(pre, dout, g):
    mu = jnp.mean(pre, axis=1, keepdims=True)
    xc = pre - mu
    var = jnp.mean(xc * xc, axis=1, keepdims=True)
    rstd = lax.rsqrt(var + LN_EPS)
    xhat = xc * rstd
    dxh = dout * g
    dpre = rstd * (dxh - jnp.mean(dxh, axis=1, keepdims=True) - xhat * jnp.mean(dxh * xhat, axis=1, keepdims=True))
    return (dpre,), (_colsum(dout * xhat), _colsum(dout))


def mm_ln(a, w, resid, g, b, *, name, fa=None):
    assert w.shape[1] == D_MODEL
    return mm(a, w, fa=fa, o_extra=(resid,), r_extra=(g, b), fo=_ln_epilogue, n_out=2, name=name)


def ln_bwd(pre, dout, g, *, name):
    (dpre,), (dg, db) = rowk(_ln_bwd_fn, [(pre, D_MODEL, 0), (dout, D_MODEL, 0)], [g],
                             [D_MODEL], [(1, D_MODEL), (1, D_MODEL)], rows=pre.shape[0], name=name)
    return dpre, dg, db


def _loss_fn(y, tgt):
    e = y - tgt
    part = _colsum(_rowsum(e * e)) * (0.5 / D_MODEL)
    return (e * (1.0 / D_MODEL),), (part,)


_XA_SCALE = 1.0 / math.sqrt(XA_HEAD_DIM)


def _attn_probs(qh, kh):
    s = _dot(qh, kh, _NT) * _XA_SCALE
    e = jnp.exp(s - jnp.max(s, axis=1, keepdims=True))
    return e / _rowsum(e)


def _attn_fwd_fn(q, k, v):
    outs = []
    for hd in range(XA_HEADS):
        sl = slice(hd * XA_HEAD_DIM, (hd + 1) * XA_HEAD_DIM)
        outs.append(_dot(_attn_probs(q[:, sl], k[:, sl]), v[:, sl]))
    return (jnp.concatenate(outs, axis=1),), ()


def _attn_bwd_fn(q, do, k, v):
    dqs, dks, dvs = [], [], []
    for hd in range(XA_HEADS):
        sl = slice(hd * XA_HEAD_DIM, (hd + 1) * XA_HEAD_DIM)
        qh, kh, vh, doh = q[:, sl], k[:, sl], v[:, sl], do[:, sl]
        p = _attn_probs(qh, kh)
        dp = _dot(doh, vh, _NT)
        ds = p * (dp - _rowsum(p * dp)) * _XA_SCALE
        dqs.append(_dot(ds, kh))
        dks.append(_dot(ds, qh, _TN))
        dvs.append(_dot(p, doh, _TN))
    cat = lambda xs: jnp.concatenate(xs, axis=1)
    return (cat(dqs),), (cat(dks), cat(dvs))


def _s5_post_fwd_fn(ylin, u, dskip, gw, gb):
    yg = _gelu(ylin + dskip * u)
    return (yg * _sigmoid(_dot(yg, gw) + gb),), ()


def _s5_post_bwd_fn(ylin, u, dout, dskip, gw, gb):
    pre = ylin + dskip * u
    yg = _gelu(pre)
    sg = _sigmoid(_dot(yg, gw) + gb)
    dlin = dout * yg * sg * (1.0 - sg)
    dyg = dout * sg + _dot(dlin, gw, _NT)
    dpre = dyg * _dgelu(pre)
    return (dpre, dpre * dskip), (_colsum(dpre * u), _dot(yg, dlin, _TN), _colsum(dlin))


def _rg_gates(xc, wa, wx, ba, bx, lam):
    r = _sigmoid(_dot(xc, wa) + ba)
    i = _sigmoid(_dot(xc, wx) + bx)
    sp = _softplus(-lam)
    log_a = -RG_C * r * sp
    a = jnp.exp(log_a)
    mult = jnp.sqrt(_neg_expm1(2.0 * log_a))
    return r, i, sp, a, mult


def _rg_pre_fwd_fn(xc, wa, wx, ba, bx, lam):
    r, i, sp, a, mult = _rg_gates(xc, wa, wx, ba, bx, lam)
    return (a, mult * (i * xc)), ()


def _rg_pre_bwd_fn(xc, gsc, hprev, wa, wx, ba, bx, lam):
    r, i, sp, a, mult = _rg_gates(xc, wa, wx, ba, bx, lam)
    da = gsc * hprev
    db = gsc
    dmult = db * i * xc
    di = db * mult * xc
    dxc = db * mult * i
    dlog_a = da * a - a * a * dmult / mult
    dr = dlog_a * (-RG_C * sp)
    dsp = _colsum(dlog_a * (-RG_C * r))
    dlam = dsp * (-_sigmoid(-lam))
    dpr = dr * r * (1.0 - r)
    dpi = di * i * (1.0 - i)
    dxc = dxc + _dot(dpr, wa, _NT) + _dot(dpi, wx, _NT)
    return (dxc,), (_dot(xc, dpr, _TN), _dot(xc, dpi, _TN), _colsum(dpr), _colsum(dpi), dlam)


def _rg_out_fwd_fn(h, g):
    return (h * _gelu(g),), ()


def _rg_out_bwd_fn(h, g, dy):
    return (dy * _gelu(g), dy * h * _dgelu(g)), ()


def _shift_down(x, prev, j, rows):
    return jnp.where(rows < j, pltpu.roll(prev, j, 0), pltpu.roll(x, j, 0))


def _shift_up(x, nxt, j, rows):
    t = x.shape[0]
    return jnp.where(rows >= t - j, pltpu.roll(nxt, t - j, 0), pltpu.roll(x, t - j, 0))


def conv_fwd(src, cb, w, b, *, width, act, name):
    t = src.shape[0]
    tt = min(ROW_TILE, t)
    n = t // tt

    def body(x_ref, w_ref, b_ref, y_ref, prev_ref):
        @pl.when(pl.program_id(0) == 0)
        def _():
            prev_ref[...] = jnp.zeros_like(prev_ref)

        x = x_ref[...]
        prev = prev_ref[...]
        rows = lax.broadcasted_iota(jnp.int32, x.shape, 0)
        wv = w_ref[...]
        y = b_ref[...] + wv[3:4, :] * x
        for j in (1, 2, 3):
            y = y + wv[3 - j:4 - j, :] * _shift_down(x, prev, j, rows)
        y_ref[...] = _silu(y) if act else y
        prev_ref[...] = x

    return pl.pallas_call(
        body, name=name, grid=(n,),
        in_specs=[pl.BlockSpec((tt, width), lambda i: (i, cb)),
                  pl.BlockSpec((4, width), lambda i: (0, 0)), pl.BlockSpec((1, width), lambda i: (0, 0))],
        out_specs=pl.BlockSpec((tt, width), lambda i: (i, 0)),
        out_shape=jax.ShapeDtypeStruct((t, width), F32),
        scratch_shapes=[pltpu.VMEM((tt, width), F32)],
        compiler_params=_params(("arbitrary",)),
    )(src, w, b)


def conv_bwd(src, cb, dy, w, b, *, width, act, name):
    t = src.shape[0]
    tt = min(ROW_TILE, t)
    n = t // tt

    def body(x_ref, xp_ref, dy_ref, w_ref, b_ref, dx_ref, dw_ref, db_ref, nxt_ref):
        i = pl.program_id(0)

        @pl.when(i == 0)
        def _():
            nxt_ref[...] = jnp.zeros_like(nxt_ref)
            dw_ref[...] = jnp.zeros_like(dw_ref)
            db_ref[...] = jnp.zeros_like(db_ref)

        x = x_ref[...]
        prev = jnp.where(i == n - 1, 0.0, xp_ref[...])
        rows = lax.broadcasted_iota(jnp.int32, x.shape, 0)
        wv = w_ref[...]
        xs = [x] + [_shift_down(x, prev, j, rows) for j in (1, 2, 3)]
        dpre = dy_ref[...]
        if act:
            pre = b_ref[...] + wv[3:4, :] * xs[0]
            for j in (1, 2, 3):
                pre = pre + wv[3 - j:4 - j, :] * xs[j]
            dpre = dpre * _dsilu(pre)
        nxt = nxt_ref[...]
        dx = wv[3:4, :] * dpre
        for j in (1, 2, 3):
            dx = dx + wv[3 - j:4 - j, :] * _shift_up(dpre, nxt, j, rows)
        dx_ref[...] = dx
        dw_ref[...] += jnp.concatenate([_colsum(dpre * xs[3 - kk]) for kk in range(4)], axis=0)
        db_ref[...] += _colsum(dpre)
        nxt_ref[...] = dpre

    return pl.pallas_call(
        body, name=name, grid=(n,),
        in_specs=[pl.BlockSpec((tt, width), lambda i: (n - 1 - i, cb)),
                  pl.BlockSpec((tt, width), lambda i: (jnp.maximum(n - 2 - i, 0), cb)),
                  pl.BlockSpec((tt, width), lambda i: (n - 1 - i, 0)),
                  pl.BlockSpec((4, width), lambda i: (0, 0)), pl.BlockSpec((1, width), lambda i: (0, 0))],
        out_specs=[pl.BlockSpec((tt, width), lambda i: (n - 1 - i, 0)),
                   pl.BlockSpec((4, width), lambda i: (0, 0)), pl.BlockSpec((1, width), lambda i: (0, 0))],
        out_shape=[jax.ShapeDtypeStruct((t, width), F32), jax.ShapeDtypeStruct((4, width), F32),
                   jax.ShapeDtypeStruct((1, width), F32)],
        scratch_shapes=[pltpu.VMEM((tt, width), F32)],
        compiler_params=_params(("arbitrary",)),
    )(src, src, dy, w, b)


S5_CW = 256


def _cmul(ar, ai, br, bi):
    return ar * br - ai * bi, ar * bi + ai * br


def _scan8_complex(src_ref, dst_ref, lam_ref, st_ref, *, w, nb, reverse):
    rows = lax.broadcasted_iota(jnp.int32, (8, S5_CW), 0)
    b8 = lambda v: jnp.broadcast_to(v, (8, S5_CW))

    def shift(x, k):
        if reverse:
            return jnp.where(rows < 8 - k, pltpu.roll(x, 8 - k, 0), 0.0)
        return jnp.where(rows >= k, pltpu.roll(x, k, 0), 0.0)

    for c0 in range(0, w, S5_CW):
        re, im = pl.ds(c0, S5_CW), pl.ds(w + c0, S5_CW)
        pw = [(lam_ref[:, re], lam_ref[:, im])]
        for _ in range(7):
            pw.append(_cmul(*pw[-1], *pw[0]))
        pr, pi = b8(pw[7][0]), b8(pw[7][1])
        for j in range(7):
            sel = rows == (7 - j if reverse else j)
            pr, pi = jnp.where(sel, b8(pw[j][0]), pr), jnp.where(sel, b8(pw[j][1]), pi)
        steps = [(k, b8(pw[k - 1][0]), b8(pw[k - 1][1])) for k in (1, 2, 4)]
        edge = 0 if reverse else 7

        def blk(i, carry):
            hr, hi = carry
            base = pl.multiple_of((nb // 2 - 1 - i if reverse else i) * 16, 16)
            pend = []
            for off in ((8, 0) if reverse else (0, 8)):
                at = pl.ds(base + off, 8)
                xr, xi = src_ref[at, re], src_ref[at, im]
                for k, kr, ki in steps:
                    sr, si = shift(xr, k), shift(xi, k)
                    xr, xi = xr + kr * sr - ki * si, xi + kr * si + ki * sr
                pend.append((at, xr, xi))
            for at, xr, xi in pend:
                xr, xi = xr + pr * hr - pi * hi, xi + pr * hi + pi * hr
                dst_ref[at, re] = xr
                dst_ref[at, im] = xi
                hr, hi = b8(xr[edge:edge + 1, :]), b8(xi[edge:edge + 1, :])
            return hr, hi

        hr, hi = lax.fori_loop(0, nb // 2, blk, (st_ref[:, re], st_ref[:, im]))
        st_ref[:, re] = hr
        st_ref[:, im] = hi


def s5_fwd(proj, bcat, lam, ccat, *, name):
    t = proj.shape[0]
    tt = min(ROW_TILE, t)
    w2 = bcat.shape[1]

    def body(u_ref, b_ref, lam_ref, c_ref, h_ref, y_ref, bu_ref, st_ref):
        @pl.when(pl.program_id(0) == 0)
        def _():
            st_ref[...] = jnp.zeros_like(st_ref)

        bu_ref[...] = _dot(u_ref[...], b_ref[...])
        _scan8_complex(bu_ref, h_ref, lam_ref, st_ref, w=w2 // 2, nb=tt // 8, reverse=False)
        y_ref[...] = _dot(h_ref[...], c_ref[...])

    fixed = lambda a: pl.BlockSpec(a.shape, lambda i: (0, 0))
    return pl.pallas_call(
        body, name=name, grid=(t // tt,),
        in_specs=[pl.BlockSpec((tt, S5_WIDTH), lambda i: (i, P_U // S5_WIDTH)), fixed(bcat), fixed(lam), fixed(ccat)],
        out_specs=[pl.BlockSpec((tt, w2), lambda i: (i, 0)), pl.BlockSpec((tt, S5_WIDTH), lambda i: (i, 0))],
        out_shape=[jax.ShapeDtypeStruct((t, w2), F32), jax.ShapeDtypeStruct((t, S5_WIDTH), F32)],
        scratch_shapes=[pltpu.VMEM((tt, w2), F32), pltpu.VMEM((8, w2), F32)],
        compiler_params=_params(("arbitrary",)),
    )(proj, bcat, lam, ccat)


def s5_bwd(dylin, du_a, hs, proj, bcat, lam_adj, ccat, *, name):
    t = proj.shape[0]
    tt = min(ROW_TILE, t)
    n, w2 = t // tt, bcat.shape[1]
    w = w2 // 2

    def body(dy_ref, dua_ref, h_ref, hp_ref, u_ref, b_ref, lam_ref, c_ref,
             du_ref, dc_ref, db_ref, dar_ref, dai_ref, g_ref, st_ref):
        i = pl.program_id(0)

        @pl.when(i == 0)
        def _():
            for r in (st_ref, dc_ref, db_ref, dar_ref, dai_ref):
                r[...] = jnp.zeros_like(r)

        dy, h = dy_ref[...], h_ref[...]
        g_ref[...] = _dot(dy, c_ref[...], _NT)
        dc_ref[...] += _dot(h, dy, _TN)
        _scan8_complex(g_ref, g_ref, lam_ref, st_ref, w=w, nb=tt // 8, reverse=True)
        g = g_ref[...]
        du_ref[...] = dua_ref[...] + _dot(g, b_ref[...], _NT)
        db_ref[...] += _dot(u_ref[...], g, _TN)
        rows = lax.broadcasted_iota(jnp.int32, (tt, w2), 0)
        before = jnp.where(i == n - 1, 0.0, hp_ref[7:8, :])
        hprev = jnp.where(rows == 0, before, pltpu.roll(h, 1, 0))
        gr, gi, hr, hi = g[:, :w], g[:, w:], hprev[:, :w], hprev[:, w:]
        dar_ref[...] += _colsum(gr * hr + gi * hi)
        dai_ref[...] += _colsum(gi * hr - gr * hi)

    rev = lambda i: n - 1 - i
    row = lambda wd, cb=0: pl.BlockSpec((tt, wd), lambda i: (rev(i), cb))
    fixed = lambda shape: pl.BlockSpec(shape, lambda i: (0, 0))
    return pl.pallas_call(
        body, name=name, grid=(n,),
        in_specs=[row(S5_WIDTH), row(S5_WIDTH), row(w2),
                  pl.BlockSpec((8, w2), lambda i: (jnp.maximum(rev(i) * (tt // 8) - 1, 0), 0)),
                  row(S5_WIDTH, P_U // S5_WIDTH), fixed(bcat.shape), fixed(lam_adj.shape), fixed(ccat.shape)],
        out_specs=[row(S5_WIDTH), fixed(ccat.shape), fixed(bcat.shape), fixed((1, w)), fixed((1, w))],
        out_shape=[jax.ShapeDtypeStruct((t, S5_WIDTH), F32), jax.ShapeDtypeStruct(ccat.shape, F32),
                   jax.ShapeDtypeStruct(bcat.shape, F32), jax.ShapeDtypeStruct((1, w), F32), jax.ShapeDtypeStruct((1, w), F32)],
        scratch_shapes=[pltpu.VMEM((tt, w2), F32), pltpu.VMEM((8, w2), F32)],
        compiler_params=_params(("arbitrary",)),
    )(dylin, du_a, hs, hs, proj, bcat, lam_adj, ccat)


def scan_real(a, b, *, reverse, name):
    t, w = b.shape
    tt = min(ROW_TILE, t)
    n, nb = t // tt, tt // 8

    def body(a_ref, b_ref, o_ref, st_ref):
        @pl.when(pl.program_id(0) == 0)
        def _():
            st_ref[...] = jnp.zeros_like(st_ref)

        rows = lax.broadcasted_iota(jnp.int32, (8, w), 0)

        def blk(i, h):
            base = pl.multiple_of((nb - 1 - i if reverse else i) * 8, 8)
            ta_, tb_ = a_ref[pl.ds(base, 8), :], b_ref[pl.ds(base, 8), :]
            out = jnp.zeros((8, w), F32)
            for j in (range(7, -1, -1) if reverse else range(8)):
                h = jnp.broadcast_to(ta_[j:j + 1, :], (8, w)) * h + jnp.broadcast_to(tb_[j:j + 1, :], (8, w))
                out = jnp.where(rows == j, h, out)
            o_ref[pl.ds(base, 8), :] = out
            return h

        st_ref[...] = lax.fori_loop(0, nb, blk, st_ref[...])

    idx = (lambda i: (n - 1 - i, 0)) if reverse else (lambda i: (i, 0))
    return pl.pallas_call(
        body, name=name, grid=(n,),
        in_specs=[pl.BlockSpec((tt, w), idx), pl.BlockSpec((tt, w), idx)],
        out_specs=pl.BlockSpec((tt, w), idx), out_shape=jax.ShapeDtypeStruct((t, w), F32),
        scratch_shapes=[pltpu.VMEM((8, w), F32)],
        compiler_params=_params(("arbitrary",)),
    )(a, b)


SSD_QQ = SSD_HEADS * SSD_CHUNK
SSD_GP = SSD_WIDTH // 2
SSD_GQ = SSD_QQ // 2


def _ssd_spread():
    h = jnp.arange(LANE)[:, None]
    spread_p = (jnp.arange(SSD_WIDTH)[None, :] // SSD_HEAD_DIM == h).astype(BF16)
    spread_q = (jnp.arange(SSD_QQ)[None, :] // SSD_CHUNK == h).astype(BF16)
    return spread_p, spread_q


def _ssd_prologue(dt_ref, prow_ref, sp_ref, sq_ref):
    q = SSD_CHUNK
    r = lax.broadcasted_iota(jnp.int32, (q, q), 0)
    c = lax.broadcasted_iota(jnp.int32, (q, q), 1)
    raw_c = dt_ref[...] + prow_ref[0:1, :]
    dt_c = _softplus(raw_c)
    a_r = -jnp.exp(prow_ref[1:2, :])
    cs_c = _dot_mask((r >= c).astype(F32), dt_c * a_r, mask_left=True, parts=3)
    both = _dot_mask(jnp.concatenate([dt_c, cs_c], axis=0), sp_ref[...], mask_left=False, parts=3)
    dt_x, cs_x = both[:q], both[q:]
    csx = _dot_mask(cs_c, sq_ref[...], mask_left=False, parts=3)
    rr = lax.broadcasted_iota(jnp.int32, (q, SSD_QQ), 0)
    ss = lax.broadcasted_iota(jnp.int32, (q, SSD_QQ), 1) & (q - 1)
    diag = rr == ss
    cs_row = _colsum(jnp.where(diag, csx, 0.0))
    lcat = jnp.exp(jnp.where(rr >= ss, csx - cs_row, -1e30))
    cl = cs_x[q - 1:q, :]
    return dict(raw_c=raw_c, dt_c=dt_c, a_r=a_r, dt_x=dt_x, cs_x=cs_x, lcat=lcat, diag=diag,
                ecs=jnp.exp(cs_x), wdec=jnp.exp(cl - cs_x), ecl=jnp.exp(cl), triu=(r <= c).astype(F32))


def _ssd_group(xbc_ref, g, lcat, xdt):
    ns, q = SSD_STATE, SSD_CHUNK
    bm = xbc_ref[:, pl.ds(SSD_WIDTH + g * ns, ns)]
    cm = xbc_ref[:, pl.ds(SSD_WIDTH + 2 * ns + g * ns, ns)]
    cb = _dot(cm, bm, _NT)
    lg = lcat[:, g * SSD_GQ:(g + 1) * SSD_GQ]
    wcat = jnp.concatenate([cb] * 4, axis=1) * lg
    head = lax.broadcasted_iota(jnp.int32, (1, SSD_GP), 1) // SSD_HEAD_DIM
    xg = xdt[:, g * SSD_GP:(g + 1) * SSD_GP]
    xbd = jnp.concatenate([jnp.where(head == j, xg, 0.0) for j in range(4)], axis=0)
    return bm, cm, lg, wcat, xbd, head


def _ssd_gate(yraw, z, nw):
    yg = yraw * _silu(z)
    r = lax.rsqrt(jnp.mean(yg * yg, axis=1, keepdims=True) + LN_EPS)
    return yg, r


def _ssd_specs(q, idx):
    return [pl.BlockSpec((q, SSD_XBC), lambda i: (idx(i), 0)),
            pl.BlockSpec((q, SSD_WIDTH), lambda i: (idx(i), P_Z // SSD_WIDTH)),
            pl.BlockSpec((q, LANE), lambda i: (idx(i), P_DT // LANE)),
            pl.BlockSpec((8, LANE), lambda i: (0, 0)), pl.BlockSpec((1, SSD_WIDTH), lambda i: (0, 0)),
            pl.BlockSpec((1, SSD_WIDTH), lambda i: (0, 0)),
            pl.BlockSpec((LANE, SSD_WIDTH), lambda i: (0, 0)), pl.BlockSpec((LANE, SSD_QQ), lambda i: (0, 0))]


def ssd_fwd(xbc, proj, prow, d_x, nw, *, name):
    t = xbc.shape[0]
    q, ns = SSD_CHUNK, SSD_STATE
    nc = t // q
    spread_p, spread_q = _ssd_spread()

    def body(xbc_ref, z_ref, dt_ref, prow_ref, dx_ref, nw_ref, sp_ref, sq_ref, y_ref, yraw_ref, sall_ref, s_ref):
        @pl.when(pl.program_id(0) == 0)
        def _():
            s_ref[...] = jnp.zeros_like(s_ref)

        sall_ref[0] = s_ref[...]
        pr = _ssd_prologue(dt_ref, prow_ref, sp_ref, sq_ref)
        xs = xbc_ref[:, pl.ds(0, SSD_WIDTH)]
        xdt = xs * pr["dt_x"]
        xw = xdt * pr["wdec"]
        ys = []
        for g in range(2):
            gp = slice(g * SSD_GP, (g + 1) * SSD_GP)
            bm, cm, lg, wcat, xbd, head = _ssd_group(xbc_ref, g, pr["lcat"], xdt)
            st = s_ref[:, gp]
            ys.append(_dot(wcat, xbd) + pr["ecs"][:, gp] * _dot(cm, st) + xs[:, gp] * dx_ref[:, gp])
            s_ref[:, gp] = pr["ecl"][:, gp] * st + _dot(bm, xw[:, gp], _TN)
        yraw = jnp.concatenate(ys, axis=1)
        yraw_ref[...] = yraw
        yg, r = _ssd_gate(yraw, z_ref[...], nw_ref[...])
        y_ref[...] = yg * r * nw_ref[...]

    row = pl.BlockSpec((q, SSD_WIDTH), lambda i: (i, 0))
    return pl.pallas_call(
        body, name=name, grid=(nc,),
        in_specs=_ssd_specs(q, lambda i: i),
        out_specs=[row, row, pl.BlockSpec((1, ns, SSD_WIDTH), lambda i: (i, 0, 0))],
        out_shape=[jax.ShapeDtypeStruct((t, SSD_WIDTH), F32), jax.ShapeDtypeStruct((t, SSD_WIDTH), F32),
                   jax.ShapeDtypeStruct((nc, ns, SSD_WIDTH), F32)],
        scratch_shapes=[pltpu.VMEM((ns, SSD_WIDTH), F32)],
        compiler_params=_params(("arbitrary",)),
    )(xbc, proj, proj, prow, d_x, nw, spread_p, spread_q)


def ssd_bwd(xbc, proj, prow, d_x, nw, yraw, sall, dout, *, name):
    t = xbc.shape[0]
    q, ns = SSD_CHUNK, SSD_STATE
    nc = t // q
    spread_p, spread_q = _ssd_spread()

    def body(xbc_ref, z_ref, dt_ref, prow_ref, dx_ref, nw_ref, sp_ref, sq_ref, yraw_ref, sall_ref, dout_ref,
             dxbc_ref, dz_ref, ddt_ref, dprm_ref, ddx_ref, dnw_ref, ds_ref):
        @pl.when(pl.program_id(0) == 0)
        def _():
            ds_ref[...] = jnp.zeros_like(ds_ref)
            dprm_ref[...] = jnp.zeros_like(dprm_ref)
            ddx_ref[...] = jnp.zeros_like(ddx_ref)
            dnw_ref[...] = jnp.zeros_like(dnw_ref)

        yraw, z, nwv, dout = yraw_ref[...], z_ref[...], nw_ref[...], dout_ref[...]
        yg, r = _ssd_gate(yraw, z, nwv)
        dnw_ref[...] += _colsum(dout * yg * r)
        dyn = dout * nwv
        dyg = r * dyn - yg * (r * r * r) * jnp.mean(dyn * yg, axis=1, keepdims=True)
        dy = dyg * _silu(z)
        dz_ref[...] = dyg * yraw * _dsilu(z)

        pr = _ssd_prologue(dt_ref, prow_ref, sp_ref, sq_ref)
        xs = xbc_ref[:, pl.ds(0, SSD_WIDTH)]
        xdt = xs * pr["dt_x"]
        wdec, ecl = pr["wdec"], pr["ecl"]
        xw = xdt * wdec
        dzm_all = pr["ecs"] * dy
        last = (lax.broadcasted_iota(jnp.int32, (q, 1), 0) == q - 1).astype(F32)
        dxs, dcsxs, es = [], [], []
        for g in range(2):
            gp = slice(g * SSD_GP, (g + 1) * SSD_GP)
            bm, cm, lg, wcat, xbd, head = _ssd_group(xbc_ref, g, pr["lcat"], xdt)
            dyg_ = dy[:, gp]
            dwcat = _dot(dyg_, xbd, _NT)
            dxbd = _dot(wcat, dyg_, _TN)
            dxg = sum(jnp.where(head == j, dxbd[j * q:(j + 1) * q], 0.0) for j in range(4))
            es.append(dwcat * wcat)
            dmm = dwcat * lg
            dm = dmm[:, 0:q] + dmm[:, q:2 * q] + dmm[:, 2 * q:3 * q] + dmm[:, 3 * q:4 * q]
            dcm = _dot(dm, bm)
            dbm = _dot(dm, cm, _TN)
            st = sall_ref[0, :, gp]
            zmat = _dot(cm, st)
            dzm = dzm_all[:, gp]
            dcm = dcm + _dot(dzm, st, _NT)
            dst = _dot(cm, dzm, _TN)
            dcsx = dzm * zmat
            dsn = ds_ref[:, gp]
            dst = dst + ecl[:, gp] * dsn
            dclx = _colsum(dsn * st) * ecl[:, gp]
            dxw = _dot(bm, dsn)
            dbm = dbm + _dot(xw[:, gp], dsn, _NT)
            dxg = dxg + wdec[:, gp] * dxw
            tw = dxw * xdt[:, gp] * wdec[:, gp]
            dclx = dclx + _colsum(tw)
            dcsxs.append(dcsx - tw + last * dclx)
            ds_ref[:, gp] = dst
            dxs.append(dxg)
            dxbc_ref[:, pl.ds(SSD_WIDTH + g * ns, ns)] = dbm
            dxbc_ref[:, pl.ds(SSD_WIDTH + 2 * ns + g * ns, ns)] = dcm
        dx = jnp.concatenate(dxs, axis=1)
        dxbc_ref[:, pl.ds(0, SSD_WIDTH)] = dx * pr["dt_x"] + dy * dx_ref[...]
        ddx_ref[...] += _colsum(dy * xs)
        red = _dot_mask(jnp.concatenate([jnp.concatenate(dcsxs, axis=1), dx * xs], axis=0), sp_ref[...], _NT,
                        mask_left=False, parts=2)
        e_all = jnp.concatenate(es, axis=1)
        e_red = _dot_mask(e_all - jnp.where(pr["diag"], _colsum(e_all), 0.0), sq_ref[...], _NT, mask_left=False, parts=2)
        dadt = _dot_mask(pr["triu"], red[:q] + e_red, mask_left=True, parts=2)
        draw = (red[q:] + dadt * pr["a_r"]) * _sigmoid(pr["raw_c"])
        ddt_ref[...] = draw
        zero = jnp.zeros((6, LANE), F32)
        dprm_ref[...] += jnp.concatenate([_colsum(draw), _colsum(dadt * pr["dt_c"]) * pr["a_r"], zero], axis=0)

    rev = lambda i: nc - 1 - i
    row = lambda w: pl.BlockSpec((q, w), lambda i: (rev(i), 0))
    fixed = lambda shape: pl.BlockSpec(shape, lambda i: (0, 0))
    return pl.pallas_call(
        body, name=name, grid=(nc,),
        in_specs=_ssd_specs(q, rev) + [row(SSD_WIDTH), pl.BlockSpec((1, ns, SSD_WIDTH), lambda i: (rev(i), 0, 0)),
                                       row(SSD_WIDTH)],
        out_specs=[row(SSD_XBC), row(SSD_WIDTH), row(LANE), fixed((8, LANE)), fixed((1, SSD_WIDTH)), fixed((1, SSD_WIDTH))],
        out_shape=[jax.ShapeDtypeStruct((t, SSD_XBC), F32), jax.ShapeDtypeStruct((t, SSD_WIDTH), F32),
                   jax.ShapeDtypeStruct((t, LANE), F32), jax.ShapeDtypeStruct((8, LANE), F32),
                   jax.ShapeDtypeStruct((1, SSD_WIDTH), F32), jax.ShapeDtypeStruct((1, SSD_WIDTH), F32)],
        scratch_shapes=[pltpu.VMEM((ns, SSD_WIDTH), F32)],
        compiler_params=_params(("arbitrary",)),
    )(xbc, proj, proj, prow, d_x, nw, spread_p, spread_q, yraw, sall, dout)


def _me():
    return lax.axis_index("x"), lax.axis_index("y"), lax.axis_index("c")


_ANY = pl.BlockSpec(memory_space=pl.ANY)
_MESH = pl.DeviceIdType.MESH


def all_gather(block, *, name):
    def body(src, dst, send_sems, recv_sems, local_sem):
        x, y, c = _me()
        me, sibling = (x, y, c), (x, y, 1 - c)
        chips = [(1 - x, y), (x, 1 - y), (1 - x, 1 - y)]

        def slot(px, py, pc):
            return dst.at[4 * px + 2 * py + pc]

        def copy(kk, blk, to, from_src=False):
            return pltpu.make_async_remote_copy(
                src_ref=src if from_src else slot(*blk), dst_ref=slot(*blk),
                send_sem=send_sems.at[kk], recv_sem=recv_sems.at[kk], device_id=to, device_id_type=_MESH)

        mine = pltpu.make_async_copy(src, slot(*me), local_sem)
        mine.start()
        first = [copy(0, me, sibling, True)] + [copy(1 + j, me, (*chip, c), True) for j, chip in enumerate(chips)]
        for cp in first:
            cp.start()
        passed = [copy(4 + j, (*chip, c), sibling) for j, chip in enumerate(chips)]
        for j, chip in enumerate(chips):
            copy(1 + j, (*chip, c), me).wait_recv()
            passed[j].start()
        copy(0, sibling, me).wait_recv()
        for j, chip in enumerate(chips):
            copy(4 + j, (*chip, 1 - c), me).wait_recv()
        for cp in first + passed:
            cp.wait_send()
        mine.wait()

    return pl.pallas_call(
        body, name=name, in_specs=[_ANY], out_specs=_ANY,
        out_shape=jax.ShapeDtypeStruct((N_DEV,) + block.shape, block.dtype),
        scratch_shapes=[pltpu.SemaphoreType.DMA((7,)), pltpu.SemaphoreType.DMA((7,)), pltpu.SemaphoreType.DMA(())],
    )(block)


RS_PIECES = 4


def rs_sibling_exchange(halves, *, name):
    _, nq, r, l = halves.shape
    rows = r // RS_PIECES
    assert r % RS_PIECES == 0 and rows % 16 == 0

    def body(src, dst, send_sems, recv_sems):
        x, y, c = _me()
        copies = []
        for q in range(nq):
            for i in range(RS_PIECES):
                kk = q * RS_PIECES + i
                cp = pltpu.make_async_remote_copy(
                    src_ref=src.at[1 - c, q, pl.ds(i * rows, rows)], dst_ref=dst.at[q, pl.ds(i * rows, rows)],
                    send_sem=send_sems.at[kk], recv_sem=recv_sems.at[kk], device_id=(x, y, 1 - c), device_id_type=_MESH)
                cp.start()
                copies.append(cp)
        for cp in copies:
            cp.wait()

    n_copies = nq * RS_PIECES
    return pl.pallas_call(
        body, name=name, in_specs=[_ANY], out_specs=_ANY,
        out_shape=jax.ShapeDtypeStruct((nq, r, l), halves.dtype),
        scratch_shapes=[pltpu.SemaphoreType.DMA((n_copies,)), pltpu.SemaphoreType.DMA((n_copies,))],
    )(halves)


def pair_sum_bf16(halves, theirs, *, name, tt=128):
    _, nq, r, wd = halves.shape
    tt = min(tt, r)
    parity = lax.axis_index("c").astype(jnp.int32).reshape(1)

    def body(c_ref, own_ref, sib_ref, o_ref):
        o_ref[...] = (own_ref[...] + sib_ref[...]).astype(BF16)

    return pl.pallas_call(
        body, name=name,
        grid_spec=pltpu.PrefetchScalarGridSpec(
            num_scalar_prefetch=1, grid=(nq, r // tt),
            in_specs=[pl.BlockSpec((None, None, tt, wd), lambda q, i, c: (c[0], q, i, 0)),
                      pl.BlockSpec((None, tt, wd), lambda q, i, c: (q, i, 0))],
            out_specs=pl.BlockSpec((None, tt, wd), lambda q, i, c: (q, i, 0))),
        out_shape=jax.ShapeDtypeStruct((nq, r, wd), BF16),
        compiler_params=_params(("parallel", "parallel")),
    )(parity, halves, theirs)


def rs_chip_exchange(part, *, name):
    def body(src, dst, send_sems, recv_sems, local_sem):
        x, y, c = _me()
        q_me = 2 * x + y
        local = pltpu.make_async_copy(src.at[q_me], dst.at[q_me], local_sem)
        local.start()
        copies = []
        for j, (px, py) in enumerate([(1 - x, y), (x, 1 - y), (1 - x, 1 - y)]):
            cp = pltpu.make_async_remote_copy(src_ref=src.at[2 * px + py], dst_ref=dst.at[q_me], send_sem=send_sems.at[j],
                                              recv_sem=recv_sems.at[j], device_id=(px, py, c), device_id_type=_MESH)
            cp.start()
            copies.append(cp)
        for cp in copies:
            cp.wait()
        local.wait()

    return pl.pallas_call(
        body, name=name, in_specs=[_ANY], out_specs=_ANY,
        out_shape=jax.ShapeDtypeStruct(part.shape, part.dtype),
        scratch_shapes=[pltpu.SemaphoreType.DMA((3,)), pltpu.SemaphoreType.DMA((3,)), pltpu.SemaphoreType.DMA(())],
    )(part)


def adamw(slabs, w, m, v, *, name, tt):
    ns, (r, wd) = slabs.shape[0], w.shape
    tt = min(tt, r)
    assert r % tt == 0

    def body(s_ref, w_ref, m_ref, v_ref, g_ref, d_ref, nm_ref, nv_ref):
        g = s_ref[0].astype(F32)
        for kdev in range(1, ns):
            g = g + s_ref[kdev].astype(F32)
        wv = w_ref[...]
        nm = ADAM_B1 * m_ref[...] + (1.0 - ADAM_B1) * g
        nv = ADAM_B2 * v_ref[...] + (1.0 - ADAM_B2) * (g * g)
        m_hat = nm / (1.0 - ADAM_B1 ** ADAM_STEP)
        v_hat = nv / (1.0 - ADAM_B2 ** ADAM_STEP)
        g_ref[...] = g
        d_ref[...] = -ADAM_LR * (m_hat / (jnp.sqrt(v_hat) + ADAM_EPS) + ADAM_WD * wv)
        nm_ref[...] = nm
        nv_ref[...] = nv

    spec = pl.BlockSpec((tt, wd), lambda i: (i, 0))
    return pl.pallas_call(
        body, name=name, grid=(r // tt,),
        in_specs=[pl.BlockSpec((ns, tt, wd), lambda i: (0, i, 0)), spec, spec, spec],
        out_specs=[spec] * 4, out_shape=[jax.ShapeDtypeStruct((r, wd), F32)] * 4,
        compiler_params=_params(("parallel",)),
    )(slabs, w, m, v)


WIDE = 1024
BIG = [("w_in", True, 289), ("w_out", False, 128), ("xa_wq", False, 128), ("xa_wk", False, 128), ("xa_wv", False, 128),
       ("xa_wo", False, 128), ("mlp_w2", False, 512), ("mlp_w1", True, 512)]
TINY = [("ssd_conv_w", 2), ("s5_glu_w", 1), ("rg_conv_w", 2)]
KEEP_F32 = ("ssd_conv_w", "rg_conv_w")
TINY_ROWS = 32
SHARDED = [name for name, _, _ in BIG] + [name for name, _ in TINY]
SMALL = ["ssd_conv_b", "ssd_dt_bias", "ssd_a_log", "ssd_d", "ssd_norm_w", "s5_lam_re", "s5_lam_im",
         "s5_log_step", "s5_b_re", "s5_b_im", "s5_c_re", "s5_c_im", "s5_d", "s5_glu_b", "rg_conv_b",
         "rg_wa", "rg_ba", "rg_wx", "rg_bx", "rg_lambda", "ln1_g", "ln1_b", "ln2_g", "ln2_b", "ln3_g", "ln3_b"]
WEIGHTS = ['w_in', 'w_out', 'ssd_conv_w', 'ssd_conv_b', 'ssd_dt_bias', 'ssd_a_log', 'ssd_d', 'ssd_norm_w',
           's5_lam_re', 's5_lam_im', 's5_log_step', 's5_b_re', 's5_b_im', 's5_c_re', 's5_c_im', 's5_d',
           's5_glu_w', 's5_glu_b', 'rg_conv_w', 'rg_conv_b', 'rg_wa', 'rg_ba', 'rg_wx', 'rg_bx', 'rg_lambda',
           'ln1_g', 'ln1_b', 'xa_wq', 'xa_wk', 'xa_wv', 'xa_wo', 'ln2_g', 'ln2_b', 'mlp_w1', 'mlp_w2',
           'ln3_g', 'ln3_b']


def _pad16(rows):
    return -(-rows // 16) * 16


def _pack_rows(flat, mult):
    n = flat.shape[-1]
    r = -(-n // (LANE * mult)) * mult
    pad = [(0, 0)] * (flat.ndim - 1) + [(0, r * LANE - n)]
    return jnp.pad(flat, pad).reshape(flat.shape[:-1] + (r, LANE))


def _unpack(packed, shapes):
    lead = packed.shape[:-2]
    flat = packed.reshape(lead + (-1,))
    out, off = [], 0
    for s in shapes:
        n = math.prod(s)
        out.append(flat[..., off:off + n].reshape(lead + tuple(s)))
        off += n
    return out


def _big_block(x, rows):
    pad = [(0, 0)] * (x.ndim - 2) + [(0, _pad16(rows) - rows), (0, 0)]
    x = jnp.pad(x, pad)
    return x.reshape(x.shape[:-3] + (DEPTH * _pad16(rows), WIDE))


def _tiny_block(flat):
    pad = [(0, 0)] * (flat.ndim - 1) + [(0, TINY_ROWS * WIDE - flat.shape[-1])]
    return jnp.pad(flat, pad).reshape(flat.shape[:-1] + (TINY_ROWS, WIDE))


def _pack_wide(big, tiny_flat):
    blocks = [_big_block(big[name], rows) for name, _, rows in BIG] + [_tiny_block(tiny_flat)]
    return jnp.concatenate(blocks, axis=-2)


def _unpack_wide(packed):
    lead, big, off = packed.shape[:-2], {}, 0
    for name, _, rows in BIG:
        rp = _pad16(rows)
        big[name] = packed[..., off:off + DEPTH * rp, :].reshape(lead + (DEPTH, rp, WIDE))[..., :rows, :]
        off += DEPTH * rp
    return big, packed[..., off:off + TINY_ROWS, :].reshape(lead + (TINY_ROWS * WIDE,))


def _split_flat(flat, shapes):
    out, off = [], 0
    for s in shapes:
        n = math.prod(s)
        out.append(flat[..., off:off + n].reshape(flat.shape[:-1] + tuple(s)))
        off += n
    return out


def _to_full(gathered, axis):
    g = jnp.moveaxis(gathered, 0, axis)
    s = g.shape
    return g.reshape(s[:axis] + (s[axis] * s[axis + 1],) + s[axis + 2:])


def _to_slabs(full, axis):
    s = full.shape
    g = full.reshape(s[:axis] + (N_DEV, s[axis] // N_DEV) + s[axis + 1:])
    return jnp.moveaxis(g, axis, 0)


def _blockdiag(w):
    h, i, j = w.shape
    eye = jnp.eye(h, dtype=w.dtype)
    return (w[:, :, None, :] * eye[:, None, :, None]).reshape(h * i, h * j)


def _blockdiag_extract(m, h):
    i, j = m.shape[0] // h, m.shape[1] // h
    eye = jnp.eye(h, dtype=m.dtype)
    return (m.reshape(h, i, h, j) * eye[:, None, :, None]).sum(axis=2)


def _s5_disc(lr, li, ls, bre, bim):
    step = jnp.exp(ls)[:, None]
    er = jnp.exp(lr * step)
    ar, ai = er * jnp.cos(li * step), er * jnp.sin(li * step)
    nr, ni, den = ar - 1.0, ai, lr * lr + li * li
    qr, qi = (nr * lr + ni * li) / den, (ni * lr - nr * li) / den
    bbr = qr[..., None] * bre - qi[..., None] * bim
    bbi = qr[..., None] * bim + qi[..., None] * bre
    return ar, ai, bbr, bbi


def _row(v, width=None):
    v = v.reshape(1, -1)
    if width is not None and v.shape[1] < width:
        v = jnp.pad(v, ((0, 0), (0, width - v.shape[1])))
    return v


def _relu2(a):
    r = jnp.maximum(a, 0.0)
    return r * r


def _add_alpha(acc, d):
    return acc + ALPHA * d


def _shift_rows_down(x):
    return jnp.concatenate([jnp.zeros((1, x.shape[1]), x.dtype), x[:-1]], axis=0)


def _shift_rows_up(x):
    return jnp.concatenate([x[1:], jnp.zeros((1, x.shape[1]), x.dtype)], axis=0)


def _layer_params(full, small, l):
    p = {}
    w_in = full["w_in"][l]
    z, xbc, dt, u, xr, g = w_in[0:512], w_in[512:1536], w_in[1536:1544], w_in[1544:1800], w_in[1800:2056], w_in[2056:2312]
    p["w_inp"] = jnp.concatenate([xbc, z, u, xr, g, dt, jnp.zeros((D_INP - P_DT - 8, D_MODEL), w_in.dtype)], axis=0)
    for k_ in ("w_out", "xa_wq", "xa_wk", "xa_wv", "xa_wo", "mlp_w1", "mlp_w2", "s5_glu_w"):
        p[k_] = full[k_][l]
    p["ssd_cw"], p["ssd_cb"] = full["ssd_conv_w"][l], _row(small["ssd_conv_b"][l])
    dtb, alog, dsk = small["ssd_dt_bias"][l], small["ssd_a_log"][l], small["ssd_d"][l]
    p["prow"] = jnp.concatenate([_row(dtb, LANE), _row(alog, LANE), jnp.zeros((6, LANE), F32)], axis=0)
    p["ssd_dx"] = _row(jnp.repeat(dsk, SSD_HEAD_DIM))
    p["ssd_nw"] = _row(small["ssd_norm_w"][l])
    s5_in = (small["s5_lam_re"][l], small["s5_lam_im"][l], small["s5_log_step"][l], small["s5_b_re"][l], small["s5_b_im"][l])
    (ar, ai, bbr, bbi), p["s5_vjp"] = jax.vjp(_s5_disc, *s5_in)
    p["lam_fwd"] = jnp.concatenate([_row(ar), _row(ai)], axis=1)
    p["lam_adj"] = jnp.concatenate([_row(ar), _row(-ai)], axis=1)
    p["bcat"] = jnp.concatenate([_blockdiag(jnp.swapaxes(bbr, 1, 2)), _blockdiag(jnp.swapaxes(bbi, 1, 2))], axis=1)
    p["ccat"] = jnp.concatenate([_blockdiag(jnp.swapaxes(small["s5_c_re"][l], 1, 2)),
                                 -_blockdiag(jnp.swapaxes(small["s5_c_im"][l], 1, 2))], axis=0)
    p["s5_d"], p["s5_glu_b"] = _row(small["s5_d"][l]), _row(small["s5_glu_b"][l])
    p["rg_cw"], p["rg_cb"] = full["rg_conv_w"][l], _row(small["rg_conv_b"][l])
    p["rg_wa"], p["rg_wx"] = _blockdiag(small["rg_wa"][l]), _blockdiag(small["rg_wx"][l])
    p["rg_ba"], p["rg_bx"], p["rg_lam"] = _row(small["rg_ba"][l]), _row(small["rg_bx"][l]), _row(small["rg_lambda"][l])
    for i in (1, 2, 3):
        p[f"g{i}"], p[f"b{i}"] = _row(small[f"ln{i}_g"][l]), _row(small[f"ln{i}_b"][l])
    return p


def _layer_fwd(h0, mem, p):
    t = h0.shape[0]
    s = {"h0": h0}
    proj = mm(h0, p["w_inp"], tb=True, name="in_proj")
    xbc = conv_fwd(proj, 0, p["ssd_cw"], p["ssd_cb"], width=SSD_XBC, act=True, name="ssd_conv_fwd")
    y_ssd, yraw, sall = ssd_fwd(xbc, proj, p["prow"], p["ssd_dx"], p["ssd_nw"], name="ssd_fwd")
    hs5, ylin = s5_fwd(proj, p["bcat"], p["lam_fwd"], p["ccat"], name="s5_fwd")
    (y_s5,), _ = rowk(_s5_post_fwd_fn, [(ylin, S5_WIDTH, 0), (proj, S5_WIDTH, P_U // S5_WIDTH)],
                      [p["s5_d"], p["s5_glu_w"], p["s5_glu_b"]], [S5_WIDTH], [], rows=t, name="s5_post_fwd")
    xc = conv_fwd(proj, P_XR // RG_WIDTH, p["rg_cw"], p["rg_cb"], width=RG_WIDTH, act=False, name="rg_conv_fwd")
    rg_full = [p["rg_wa"], p["rg_wx"], p["rg_ba"], p["rg_bx"], p["rg_lam"]]
    (a_rg, b_rg), _ = rowk(_rg_pre_fwd_fn, [(xc, RG_WIDTH, 0)], rg_full, [RG_WIDTH, RG_WIDTH], [], rows=t, name="rg_pre_fwd")
    h_rg = scan_real(a_rg, b_rg, reverse=False, name="rg_scan_fwd")
    (y_rg,), _ = rowk(_rg_out_fwd_fn, [(h_rg, RG_WIDTH, 0), (proj, RG_WIDTH, P_G // RG_WIDTH)], [], [RG_WIDTH], [],
                      rows=t, name="rg_out_fwd")
    ycat = jnp.concatenate([y_ssd, y_s5, y_rg], axis=1)
    pre1, h1 = mm_ln(ycat, p["w_out"], h0, p["g1"], p["b1"], name="out_proj")
    q = mm(h1, p["xa_wq"], name="xa_q")
    k = mm(mem, p["xa_wk"], name="xa_kv")
    v = mm(mem, p["xa_wv"], name="xa_kv")
    (o,), _ = rowk(_attn_fwd_fn, [(q, D_MODEL, 0)], [k, v], [D_MODEL], [], rows=t, name="xa_fwd")
    pre2, h2 = mm_ln(o, p["xa_wo"], h1, p["g2"], p["b2"], name="xa_o")
    a_mlp = mm(h2, p["mlp_w1"], tb=True, name="mlp_up")
    pre3, h3 = mm_ln(a_mlp, p["mlp_w2"], h2, p["g3"], p["b3"], fa=_relu2, name="mlp_down")
    s.update(proj=proj, xbc=xbc, yraw=yraw, sall=sall, hs5=hs5, ylin=ylin, xc=xc, a_rg=a_rg, h_rg=h_rg,
             ycat=ycat, pre1=pre1, h1=h1, q=q, k=k, v=v, o=o, pre2=pre2, h2=h2, a_mlp=a_mlp, pre3=pre3)
    return h3, s


def _layer_bwd(dh3, mem, p, s, l, gfull, gsmall):
    t = dh3.shape[0]
    proj = s["proj"]
    dpre3, dg3, db3 = ln_bwd(s["pre3"], dh3, p["g3"], name="ln_bwd")
    da = mm(dpre3, p["mlp_w2"], tb=True, o_extra=(s["a_mlp"],), fo=lambda acc, a: acc * 2.0 * jnp.maximum(a, 0.0), name="mlp_da")
    gfull["mlp_w2"][l] = mm(s["a_mlp"], dpre3, ta=True, fa=_relu2, name="mlp_dw2")
    gfull["mlp_w1"][l] = mm(da, s["h2"], ta=True, name="mlp_dw1")
    dh2 = mm(da, p["mlp_w1"], o_extra=(dpre3,), fo=_add_alpha, name="mlp_dx")
    dpre2, dg2, db2 = ln_bwd(s["pre2"], dh2, p["g2"], name="ln_bwd")
    do = mm(dpre2, p["xa_wo"], tb=True, name="xa_do")
    gfull["xa_wo"][l] = mm(s["o"], dpre2, ta=True, name="dw_sq")
    (dq,), (dk, dv) = rowk(_attn_bwd_fn, [(s["q"], D_MODEL, 0), (do, D_MODEL, 0)], [s["k"], s["v"]], [D_MODEL],
                           [(256, D_MODEL), (256, D_MODEL)], rows=t, name="xa_bwd")
    gfull["xa_wq"][l] = mm(s["h1"], dq, ta=True, name="dw_sq")
    gfull["xa_wk"][l] = mm(mem, dk, ta=True, name="dw_kv")
    gfull["xa_wv"][l] = mm(mem, dv, ta=True, name="dw_kv")
    dh1 = mm(dq, p["xa_wq"], tb=True, o_extra=(dpre2,), fo=_add_alpha, name="dx_sq")
    dpre1, dg1, db1 = ln_bwd(s["pre1"], dh1, p["g1"], name="ln_bwd")
    dycat = mm(dpre1, p["w_out"], tb=True, name="xa_do")
    gfull["w_out"][l] = mm(s["ycat"], dpre1, ta=True, name="dw_sq")
    (dh_rg, dg_rg), _ = rowk(_rg_out_bwd_fn, [(s["h_rg"], RG_WIDTH, 0), (proj, RG_WIDTH, P_G // RG_WIDTH), (dycat, RG_WIDTH, 3)],
                             [], [RG_WIDTH, RG_WIDTH], [], rows=t, name="rg_out_bwd")
    g_rg = scan_real(_shift_rows_up(s["a_rg"]), dh_rg, reverse=True, name="rg_scan_bwd")
    rg_full = [p["rg_wa"], p["rg_wx"], p["rg_ba"], p["rg_bx"], p["rg_lam"]]
    (dxc,), (dwa, dwx, dba, dbx, dlam) = rowk(
        _rg_pre_bwd_fn, [(s["xc"], RG_WIDTH, 0), (g_rg, RG_WIDTH, 0), (_shift_rows_down(s["h_rg"]), RG_WIDTH, 0)], rg_full,
        [RG_WIDTH], [(RG_WIDTH, RG_WIDTH), (RG_WIDTH, RG_WIDTH), (1, RG_WIDTH), (1, RG_WIDTH), (1, RG_WIDTH)],
        rows=t, name="rg_pre_bwd")
    dxr, d_rgcw, d_rgcb = conv_bwd(proj, P_XR // RG_WIDTH, dxc, p["rg_cw"], p["rg_cb"], width=RG_WIDTH, act=False, name="rg_conv_bwd")
    (dylin, du_a), (d_s5d, d_gluw, d_glub) = rowk(
        _s5_post_bwd_fn, [(s["ylin"], S5_WIDTH, 0), (proj, S5_WIDTH, P_U // S5_WIDTH), (dycat, S5_WIDTH, 2)],
        [p["s5_d"], p["s5_glu_w"], p["s5_glu_b"]], [S5_WIDTH, S5_WIDTH],
        [(1, S5_WIDTH), (S5_WIDTH, S5_WIDTH), (1, S5_WIDTH)], rows=t, name="s5_post_bwd")
    du, dccat, dbcat, dar, dai = s5_bwd(dylin, du_a, s["hs5"], proj, p["bcat"], p["lam_adj"], p["ccat"], name="s5_bwd")
    dxbc_act, dz, ddt, dprm, ddx, dnw = ssd_bwd(s["xbc"], proj, p["prow"], p["ssd_dx"], p["ssd_nw"], s["yraw"], s["sall"], dycat,
                                               name="ssd_bwd")
    dxbc, d_scw, d_scb = conv_bwd(proj, 0, dxbc_act, p["ssd_cw"], p["ssd_cb"], width=SSD_XBC, act=True, name="ssd_conv_bwd")
    dproj = jnp.concatenate([dxbc, dz, du, dxr, dg_rg, ddt, jnp.zeros((t, D_INP - P_DT - LANE), F32)], axis=1)
    dh0 = mm(dproj, p["w_inp"], o_extra=(dpre1,), fo=_add_alpha, name="in_proj_dx")
    dwp = mm(dproj, s["h0"], ta=True, name="in_proj_dw")
    gfull["w_in"][l] = jnp.concatenate([dwp[P_Z:P_Z + 512], dwp[P_XBC:P_XBC + 1024], dwp[P_DT:P_DT + 8],
                                        dwp[P_U:P_U + 256], dwp[P_XR:P_XR + 256], dwp[P_G:P_G + 256]], axis=0)
    gfull["ssd_conv_w"][l], gfull["rg_conv_w"][l], gfull["s5_glu_w"][l] = d_scw, d_rgcw, d_gluw
    ng, ns = S5_GROUPS, S5_STATE
    dbbr = jnp.swapaxes(_blockdiag_extract(dbcat[:, :S5_NSTATE], ng), 1, 2)
    dbbi = jnp.swapaxes(_blockdiag_extract(dbcat[:, S5_NSTATE:], ng), 1, 2)
    d_lr, d_li, d_ls, d_bre, d_bim = p["s5_vjp"]((dar.reshape(ng, ns), dai.reshape(ng, ns), dbbr, dbbi))
    gsmall["s5_lam_re"][l], gsmall["s5_lam_im"][l], gsmall["s5_log_step"][l] = d_lr, d_li, d_ls
    gsmall["s5_b_re"][l], gsmall["s5_b_im"][l] = d_bre, d_bim
    gsmall["s5_c_re"][l] = jnp.swapaxes(_blockdiag_extract(dccat[:S5_NSTATE], ng), 1, 2)
    gsmall["s5_c_im"][l] = -jnp.swapaxes(_blockdiag_extract(dccat[S5_NSTATE:], ng), 1, 2)
    gsmall["s5_d"][l], gsmall["s5_glu_b"][l] = d_s5d[0], d_glub[0]
    gsmall["ssd_conv_b"][l], gsmall["rg_conv_b"][l] = d_scb[0], d_rgcb[0]
    gsmall["ssd_dt_bias"][l], gsmall["ssd_a_log"][l] = dprm[0, :8], dprm[1, :8]
    gsmall["ssd_d"][l] = ddx.reshape(SSD_HEADS, SSD_HEAD_DIM).sum(axis=1)
    gsmall["ssd_norm_w"][l] = dnw[0]
    gsmall["rg_wa"][l], gsmall["rg_wx"][l] = _blockdiag_extract(dwa, RG_BLOCKS), _blockdiag_extract(dwx, RG_BLOCKS)
    gsmall["rg_ba"][l], gsmall["rg_bx"][l] = dba.reshape(RG_BLOCKS, RG_BLOCK_DIM), dbx.reshape(RG_BLOCKS, RG_BLOCK_DIM)
    gsmall["rg_lambda"][l] = dlam[0]
    for i, (dg, db) in zip((1, 2, 3), ((dg1, db1), (dg2, db2), (dg3, db3))):
        gsmall[f"ln{i}_g"][l], gsmall[f"ln{i}_b"][l] = dg[0], db[0]
    return dh0


def _step(a):
    h = a["x"][0]
    mem = a["mem"][0]
    t = h.shape[0]

    def my_shards(pre):
        return ({name: (jnp.swapaxes(a[pre + name], 1, 2) if tr else a[pre + name]) for name, tr, _ in BIG},
                [a[pre + name] for name, _ in TINY])

    big, tiny = my_shards("")
    tiny16 = [(lax.bitcast_convert_type(w, BF16) if name in KEEP_F32 else w.astype(BF16)).reshape(-1)
              for (name, _), w in zip(TINY, tiny)]
    packed = _pack_wide({name: w.astype(BF16) for name, w in big.items()}, jnp.concatenate(tiny16))
    gbig, gtiny = _unpack_wide(all_gather(packed, name="ag_weights"))
    full = {name: _to_full(gbig[name], 1) for name, _, _ in BIG}
    tiny_shapes = [w.shape + ((2,) if name in KEEP_F32 else ()) for (name, _), w in zip(TINY, tiny)]
    for (name, axis), g in zip(TINY, _split_flat(gtiny, tiny_shapes)):
        full[name] = _to_full(lax.bitcast_convert_type(g, F32) if name in KEEP_F32 else g, axis)
    small = {name: a[name] for name in SMALL}
    params, saved = [], []
    for l in range(DEPTH):
        p = _layer_params(full, small, l)
        h, s = _layer_fwd(h, mem, p)
        params.append(p)
        saved.append(s)
    (dh,), (loss_part,) = rowk(_loss_fn, [(h, D_MODEL, 0), (a["loss_target"][0], D_MODEL, 0)], [], [D_MODEL], [(1, 1)],
                               rows=t, name="loss_head")
    loss = lax.psum(loss_part[0, 0], ("x", "y", "c"))
    gfull = {name: [None] * DEPTH for name in SHARDED}
    gsmall = {name: [None] * DEPTH for name in SMALL}
    for l in reversed(range(DEPTH)):
        dh = _layer_bwd(dh, mem, params[l], saved[l], l, gfull, gsmall)
    grad_x = dh[None]
    gbig = {name: jnp.stack([g.reshape(N_DEV, rows, WIDE) for g in gfull[name]], axis=1) for name, _, rows in BIG}
    gtiny = jnp.concatenate([_to_slabs(jnp.stack(gfull[name]), axis).reshape(N_DEV, -1) for name, axis in TINY], axis=1)
    slabs = _pack_wide(gbig, gtiny)
    halves = jnp.swapaxes(slabs.reshape((4, 2) + slabs.shape[1:]), 0, 1)
    theirs = rs_sibling_exchange(halves, name="rs_sibling")
    slabs = rs_chip_exchange(pair_sum_bf16(halves, theirs, name="rs_pair_sum"), name="rs_chips")

    def pk(pre):
        big, tiny = my_shards(pre)
        return _pack_wide(big, jnp.concatenate([w.reshape(-1) for w in tiny]))

    bigs = adamw(slabs, pk(""), pk("m_"), pk("v_"), name="adamw_sharded", tt=128)
    gs = _pack_rows(jnp.concatenate([jnp.stack(gsmall[name]).reshape(-1) for name in SMALL]), 8)
    gs = all_gather(gs, name="ag_small_grads")
    pks = lambda pre: _pack_rows(jnp.concatenate([a[pre + name].reshape(-1) for name in SMALL]), 8)
    sm = adamw(gs, pks(""), pks("m_"), pks("v_"), name="adamw_replicated", tt=gs.shape[1])
    out = {}
    for kind, bg, sg in zip(("grad_", "delta_", "new_m_", "new_v_"), bigs, sm):
        obig, otiny = _unpack_wide(bg)
        for name, tr, _ in BIG:
            out[kind + name] = jnp.swapaxes(obig[name], 1, 2) if tr else obig[name]
        for (name, _), arr in zip(TINY, _split_flat(otiny, [w.shape for w in tiny])):
            out[kind + name] = arr
        for name, arr in zip(SMALL, _unpack(sg, [a[name].shape for name in SMALL])):
            out[kind + name] = arr
    return (loss, grad_x) + tuple(out[kind + name] for kind in ("grad_", "delta_", "new_m_", "new_v_") for name in WEIGHTS)


def kernel(x, mem, w_in, w_out, ssd_conv_w, ssd_conv_b, ssd_dt_bias, ssd_a_log, ssd_d, ssd_norm_w, s5_lam_re, s5_lam_im, s5_log_step, s5_b_re, s5_b_im, s5_c_re, s5_c_im, s5_d, s5_glu_w, s5_glu_b, rg_conv_w, rg_conv_b, rg_wa, rg_ba, rg_wx, rg_bx, rg_lambda, ln1_g, ln1_b, xa_wq, xa_wk, xa_wv, xa_wo, ln2_g, ln2_b, mlp_w1, mlp_w2, ln3_g, ln3_b, loss_target, m_w_in, m_w_out, m_ssd_conv_w, m_ssd_conv_b, m_ssd_dt_bias, m_ssd_a_log, m_ssd_d, m_ssd_norm_w, m_s5_lam_re, m_s5_lam_im, m_s5_log_step, m_s5_b_re, m_s5_b_im, m_s5_c_re, m_s5_c_im, m_s5_d, m_s5_glu_w, m_s5_glu_b, m_rg_conv_w, m_rg_conv_b, m_rg_wa, m_rg_ba, m_rg_wx, m_rg_bx, m_rg_lambda, m_ln1_g, m_ln1_b, m_xa_wq, m_xa_wk, m_xa_wv, m_xa_wo, m_ln2_g, m_ln2_b, m_mlp_w1, m_mlp_w2, m_ln3_g, m_ln3_b, v_w_in, v_w_out, v_ssd_conv_w, v_ssd_conv_b, v_ssd_dt_bias, v_ssd_a_log, v_ssd_d, v_ssd_norm_w, v_s5_lam_re, v_s5_lam_im, v_s5_log_step, v_s5_b_re, v_s5_b_im, v_s5_c_re, v_s5_c_im, v_s5_d, v_s5_glu_w, v_s5_glu_b, v_rg_conv_w, v_rg_conv_b, v_rg_wa, v_rg_ba, v_rg_wx, v_rg_bx, v_rg_lambda, v_ln1_g, v_ln1_b, v_xa_wq, v_xa_wk, v_xa_wv, v_xa_wo, v_ln2_g, v_ln2_b, v_mlp_w1, v_mlp_w2, v_ln3_g, v_ln3_b):
    return _step(dict(locals()))
```

```python
import math

import jax
import jax.numpy as jnp
from jax import lax
from jax.experimental import pallas as pl
from jax.experimental.pallas import tpu as pltpu

F32 = jnp.float32
BF16 = jnp.bfloat16

N_DEV = 8
D_MODEL = 1024
DEPTH = 2
SSD_WIDTH = 512
SSD_HEADS = 8
SSD_HEAD_DIM = 64
SSD_STATE = 128
SSD_CHUNK = 128
SSD_XBC = 1024
S5_WIDTH = 256
S5_GROUPS = 16
S5_GROUP_CH = 16
S5_STATE = 64
S5_NSTATE = S5_GROUPS * S5_STATE
RG_WIDTH = 256
RG_BLOCKS = 4
RG_BLOCK_DIM = 64
RG_C = 8.0
XA_HEADS = 4
XA_HEAD_DIM = 256
ALPHA = (2.0 * DEPTH) ** 0.25
LN_EPS = 1e-5
ADAM_LR, ADAM_B1, ADAM_B2, ADAM_EPS, ADAM_WD, ADAM_STEP = 0.001, 0.9, 0.999, 1e-08, 0.01, 10

P_XBC, P_Z, P_U, P_XR, P_G, P_DT = 0, 1024, 1536, 1792, 2048, 2304
D_INP = 2560
LANE = 128
VMEM_LIMIT = 56 * 1024 * 1024
ROW_TILE = 512

_NN = ((1,), (0,))
_NT = ((1,), (1,))
_TN = ((0,), (0,))


def _dot(a, b, dims=_NN):
    return lax.dot_general(a.astype(BF16), b.astype(BF16), (dims, ((), ())), preferred_element_type=F32)


def _split_bf16(x, parts):
    out, rem = [], x
    for _ in range(parts):
        piece = rem.astype(BF16)
        out.append(piece)
        rem = rem - piece.astype(F32)
    return out


def _dot_mask(a, b, dims=_NN, *, mask_left, parts):
    if mask_left:
        return sum(_dot(a, piece, dims) for piece in _split_bf16(b, parts))
    return sum(_dot(piece, b, dims) for piece in _split_bf16(a, parts))


def _sigmoid(x):
    return 1.0 / (1.0 + jnp.exp(-x))


def _silu(x):
    return x * _sigmoid(x)


def _dsilu(x):
    s = _sigmoid(x)
    return s * (1.0 + x * (1.0 - s))


_GK = math.sqrt(2.0 / math.pi)
_GC = 0.044715


def _gelu(x):
    return 0.5 * x * (1.0 + jnp.tanh(_GK * (x + _GC * x * x * x)))


def _dgelu(x):
    th = jnp.tanh(_GK * (x + _GC * x * x * x))
    return 0.5 * (1.0 + th) + 0.5 * x * (1.0 - th * th) * _GK * (1.0 + 3.0 * _GC * x * x)


def _log1p_pos(e):
    return jnp.where(e < 1e-2, e * (1.0 - e * (0.5 - e * (1.0 / 3.0))), jnp.log(1.0 + e))


def _softplus(x):
    return jnp.maximum(x, 0.0) + _log1p_pos(jnp.exp(-jnp.abs(x)))


def _neg_expm1(x):
    poly = -x * (1.0 + x * (0.5 + x * (1.0 / 6.0 + x * (1.0 / 24.0 + x * (1.0 / 120.0)))))
    return jnp.where(x > -0.05, poly, 1.0 - jnp.exp(x))


def _params(sem):
    return pltpu.CompilerParams(dimension_semantics=sem, vmem_limit_bytes=VMEM_LIMIT)


RESIDENT_BYTES = 8 * 1024 * 1024
STREAM_BYTES = 4 * 1024 * 1024


def _halve_to_fit(dims, bytes_per, limit):
    dims = list(dims)
    while math.prod(dims) * bytes_per > limit:
        i = max(range(len(dims)), key=lambda d: dims[d])
        assert dims[i] % 256 == 0, dims
        dims[i] //= 2
    return dims


def mm(a, b, *, name, ta=False, tb=False, a_extra=(), fa=None, o_extra=(), r_extra=(), fo=None, n_out=1,
       a_off=0, m=None, k=None, out_dtype=F32):
    n = b.shape[0] if tb else b.shape[1]
    na, no, nr = 1 + len(a_extra), len(o_extra), len(r_extra)
    if not ta:
        assert m is None
        m, kdim = a.shape[0], (a.shape[1] if k is None else k)
        assert a_off % kdim == 0
        (tn,) = _halve_to_fit([n], kdim * b.dtype.itemsize, RESIDENT_BYTES)
        (tm,) = _halve_to_fit([min(512, m)], max(tn, kdim) * 4, STREAM_BYTES)
        a_spec = pl.BlockSpec((tm, kdim), lambda i, j: (i, a_off // kdim))
        b_spec = pl.BlockSpec((tn, kdim), lambda i, j: (j, 0)) if tb else pl.BlockSpec((kdim, tn), lambda i, j: (0, j))
        o_spec = pl.BlockSpec((tm, tn), lambda i, j: (i, j))
        dims = _NT if tb else _NN

        r_spec = pl.BlockSpec((1, tn), lambda i, j: (0, j))

        def body(*refs):
            a_refs, b_ref, e_refs, out_refs = refs[:na], refs[na], refs[na + 1:na + 1 + no + nr], refs[na + 1 + no + nr:]
            av = a_refs[0][...] if fa is None else fa(*[r[...] for r in a_refs])
            acc = _dot(av, b_ref[...], dims)
            res = acc if fo is None else fo(acc, *[r[...] for r in e_refs])
            for r, v in zip(out_refs, res if n_out > 1 else (res,)):
                r[...] = v.astype(r.dtype)

        grid, sem = (m // tm, n // tn), ("parallel", "parallel")
    else:
        assert k is None and not tb and fo is None and not o_extra and not r_extra and n_out == 1 and out_dtype == F32
        kdim, m = a.shape[0], (a.shape[1] if m is None else m)
        r_spec = None
        tm, tn = _halve_to_fit([m, n], 4, RESIDENT_BYTES)
        (tk,) = _halve_to_fit([min(512, kdim)], max(tm, tn) * 4, STREAM_BYTES)
        assert a_off % tm == 0
        a_spec = pl.BlockSpec((tk, tm), lambda i, j, kk: (kk, i + a_off // tm))
        b_spec = pl.BlockSpec((tk, tn), lambda i, j, kk: (kk, j))
        o_spec = pl.BlockSpec((tm, tn), lambda i, j, kk: (i, j))

        def body(*refs):
            a_refs, b_ref, out_ref = refs[:na], refs[na], refs[na + 1]

            @pl.when(pl.program_id(2) == 0)
            def _():
                out_ref[...] = jnp.zeros_like(out_ref)

            av = a_refs[0][...] if fa is None else fa(*[r[...] for r in a_refs])
            out_ref[...] += _dot(av, b_ref[...], _TN)

        grid, sem = (m // tm, n // tn, kdim // tk), ("parallel", "parallel", "arbitrary")
    assert m % tm == 0 and n % tn == 0, (name, m, n, tm, tn)
    out = jax.ShapeDtypeStruct((m, n), out_dtype)
    return pl.pallas_call(
        body, name=name, grid=grid,
        in_specs=[a_spec] * na + [b_spec] + [o_spec] * no + [r_spec] * nr,
        out_specs=o_spec if n_out == 1 else [o_spec] * n_out, out_shape=out if n_out == 1 else [out] * n_out,
        compiler_params=_params(sem),
    )(a, *a_extra, b, *o_extra, *r_extra)


def rowk(fn, tiled, full, out_w, acc_shapes, *, rows, name, out_dtypes=None):
    tt = min(ROW_TILE, rows)
    n = rows // tt
    assert rows % tt == 0
    nt, nf, no = len(tiled), len(full), len(out_w)

    def tspec(w, cb):
        return pl.BlockSpec((tt, w), lambda i: (i, cb))

    def fspec(a):
        nd = a.ndim
        return pl.BlockSpec(a.shape, lambda i: (0,) * nd)

    def body(*refs):
        ins, fulls = refs[:nt], refs[nt:nt + nf]
        outs, accs = refs[nt + nf:nt + nf + no], refs[nt + nf + no:]
        res_t, res_a = fn(*[r[...] for r in ins], *[r[...] for r in fulls])
        for r, v in zip(outs, res_t):
            r[...] = v.astype(r.dtype)
        if accs:
            @pl.when(pl.program_id(0) == 0)
            def _():
                for r in accs:
                    r[...] = jnp.zeros_like(r)
            for r, v in zip(accs, res_a):
                r[...] += v

    outs = pl.pallas_call(
        body, name=name, grid=(n,),
        in_specs=[tspec(w, cb) for (_, w, cb) in tiled] + [fspec(a) for a in full],
        out_specs=[tspec(w, 0) for w in out_w] + [pl.BlockSpec(s, lambda i, nd=len(s): (0,) * nd) for s in acc_shapes],
        out_shape=[jax.ShapeDtypeStruct((rows, w), dt) for w, dt in zip(out_w, out_dtypes or [F32] * no)]
        + [jax.ShapeDtypeStruct(s, F32) for s in acc_shapes],
        compiler_params=_params(("arbitrary",)),
    )(*[a for (a, _, _) in tiled], *full)
    return outs[:no], outs[no:]


def _colsum(x):
    return jnp.sum(x, axis=0, keepdims=True)


def _rowsum(x):
    return jnp.sum(x, axis=1, keepdims=True)


def _ln_epilogue(acc, resid, g, b):
    pre = ALPHA * resid + acc
    mu = jnp.mean(pre, axis=1, keepdims=True)
    xc = pre - mu
    var = jnp.mean(xc * xc, axis=1, keepdims=True)
    return pre, xc * lax.rsqrt(var + LN_EPS) * g + b


def _ln_bwd_fn(pre, dout, g):
    mu = jnp.mean(pre, axis=1, keepdims=True)
    xc = pre - mu
    var = jnp.mean(xc * xc, axis=1, keepdims=True)
    rstd = lax.rsqrt(var + LN_EPS)
    xhat = xc * rstd
    dxh = dout * g
    dpre = rstd * (dxh - jnp.mean(dxh, axis=1, keepdims=True) - xhat * jnp.mean(dxh * xhat, axis=1, keepdims=True))
    return (dpre,), (_colsum(dout * xhat), _colsum(dout))


def mm_ln(a, w, resid, g, b, *, name, fa=None):
    assert w.shape[1] == D_MODEL
    return mm(a, w, fa=fa, o_extra=(resid,), r_extra=(g, b), fo=_ln_epilogue, n_out=2, name=name)


def ln_bwd(pre, dout, g, *, name):
    (dpre,), (dg, db) = rowk(_ln_bwd_fn, [(pre, D_MODEL, 0), (dout, D_MODEL, 0)], [g],
                             [D_MODEL], [(1, D_MODEL), (1, D_MODEL)], rows=pre.shape[0], name=name)
    return dpre, dg, db


def _loss_fn(y, tgt):
    e = y - tgt
    part = _colsum(_rowsum(e * e)) * (0.5 / D_MODEL)
    return (e * (1.0 / D_MODEL),), (part,)


_XA_SCALE = 1.0 / math.sqrt(XA_HEAD_DIM)


def _attn_probs(qh, kh):
    s = _dot(qh, kh, _NT) * _XA_SCALE
    e = jnp.exp(s - jnp.max(s, axis=1, keepdims=True))
    return e / _rowsum(e)


def _attn_fwd_fn(q, k, v):
    outs = []
    for hd in range(XA_HEADS):
        sl = slice(hd * XA_HEAD_DIM, (hd + 1) * XA_HEAD_DIM)
        outs.append(_dot(_attn_probs(q[:, sl], k[:, sl]), v[:, sl]))
    return (jnp.concatenate(outs, axis=1),), ()


def _attn_bwd_fn(q, do, k, v):
    dqs, dks, dvs = [], [], []
    for hd in range(XA_HEADS):
        sl = slice(hd * XA_HEAD_DIM, (hd + 1) * XA_HEAD_DIM)
        qh, kh, vh, doh = q[:, sl], k[:, sl], v[:, sl], do[:, sl]
        p = _attn_probs(qh, kh)
        dp = _dot(doh, vh, _NT)
        ds = p * (dp - _rowsum(p * dp)) * _XA_SCALE
        dqs.append(_dot(ds, kh))
        dks.append(_dot(ds, qh, _TN))
        dvs.append(_dot(p, doh, _TN))
    cat = lambda xs: jnp.concatenate(xs, axis=1)
    return (cat(dqs),), (cat(dks), cat(dvs))


def _s5_post_fwd_fn(ylin, u, dskip, gw, gb):
    yg = _gelu(ylin + dskip * u)
    return (yg * _sigmoid(_dot(yg, gw) + gb),), ()


def _s5_post_bwd_fn(ylin, u, dout, dskip, gw, gb):
    pre = ylin + dskip * u
    yg = _gelu(pre)
    sg = _sigmoid(_dot(yg, gw) + gb)
    dlin = dout * yg * sg * (1.0 - sg)
    dyg = dout * sg + _dot(dlin, gw, _NT)
    dpre = dyg * _dgelu(pre)
    return (dpre, dpre * dskip), (_colsum(dpre * u), _dot(yg, dlin, _TN), _colsum(dlin))


def _rg_gates(xc, wa, wx, ba, bx, lam):
    r = _sigmoid(_dot(xc, wa) + ba)
    i = _sigmoid(_dot(xc, wx) + bx)
    sp = _softplus(-lam)
    log_a = -RG_C * r * sp
    a = jnp.exp(log_a)
    mult = jnp.sqrt(_neg_expm1(2.0 * log_a))
    return r, i, sp, a, mult


def _rg_pre_fwd_fn(xc, wa, wx, ba, bx, lam):
    r, i, sp, a, mult = _rg_gates(xc, wa, wx, ba, bx, lam)
    return (a, mult * (i * xc)), ()


def _rg_pre_bwd_fn(xc, gsc, hprev, wa, wx, ba, bx, lam):
    r, i, sp, a, mult = _rg_gates(xc, wa, wx, ba, bx, lam)
    da = gsc * hprev
    db = gsc
    dmult = db * i * xc
    di = db * mult * xc
    dxc = db * mult * i
    dlog_a = da * a - a * a * dmult / mult
    dr = dlog_a * (-RG_C * sp)
    dsp = _colsum(dlog_a * (-RG_C * r))
    dlam = dsp * (-_sigmoid(-lam))
    dpr = dr * r * (1.0 - r)
    dpi = di * i * (1.0 - i)
    dxc = dxc + _dot(dpr, wa, _NT) + _dot(dpi, wx, _NT)
    return (dxc,), (_dot(xc, dpr, _TN), _dot(xc, dpi, _TN), _colsum(dpr), _colsum(dpi), dlam)


def _rg_out_fwd_fn(h, g):
    return (h * _gelu(g),), ()


def _rg_out_bwd_fn(h, g, dy):
    return (dy * _gelu(g), dy * h * _dgelu(g)), ()


def _shift_down(x, prev, j, rows):
    return jnp.where(rows < j, pltpu.roll(prev, j, 0), pltpu.roll(x, j, 0))


def _shift_up(x, nxt, j, rows):
    t = x.shape[0]
    return jnp.where(rows >= t - j, pltpu.roll(nxt, t - j, 0), pltpu.roll(x, t - j, 0))


def conv_fwd(src, cb, w, b, *, width, act, name):
    t = src.shape[0]
    tt = min(ROW_TILE, t)
    n = t // tt

    def body(x_ref, w_ref, b_ref, y_ref, prev_ref):
        @pl.when(pl.program_id(0) == 0)
        def _():
            prev_ref[...] = jnp.zeros_like(prev_ref)

        x = x_ref[...]
        prev = prev_ref[...]
        rows = lax.broadcasted_iota(jnp.int32, x.shape, 0)
        wv = w_ref[...]
        y = b_ref[...] + wv[3:4, :] * x
        for j in (1, 2, 3):
            y = y + wv[3 - j:4 - j, :] * _shift_down(x, prev, j, rows)
        y_ref[...] = _silu(y) if act else y
        prev_ref[...] = x

    return pl.pallas_call(
        body, name=name, grid=(n,),
        in_specs=[pl.BlockSpec((tt, width), lambda i: (i, cb)),
                  pl.BlockSpec((4, width), lambda i: (0, 0)), pl.BlockSpec((1, width), lambda i: (0, 0))],
        out_specs=pl.BlockSpec((tt, width), lambda i: (i, 0)),
        out_shape=jax.ShapeDtypeStruct((t, width), F32),
        scratch_shapes=[pltpu.VMEM((tt, width), F32)],
        compiler_params=_params(("arbitrary",)),
    )(src, w, b)


def conv_bwd(src, cb, dy, w, b, *, width, act, name):
    t = src.shape[0]
    tt = min(ROW_TILE, t)
    n = t // tt

    def body(x_ref, xp_ref, dy_ref, w_ref, b_ref, dx_ref, dw_ref, db_ref, nxt_ref):
        i = pl.program_id(0)

        @pl.when(i == 0)
        def _():
            nxt_ref[...] = jnp.zeros_like(nxt_ref)
            dw_ref[...] = jnp.zeros_like(dw_ref)
            db_ref[...] = jnp.zeros_like(db_ref)

        x = x_ref[...]
        prev = jnp.where(i == n - 1, 0.0, xp_ref[...])
        rows = lax.broadcasted_iota(jnp.int32, x.shape, 0)
        wv = w_ref[...]
        xs = [x] + [_shift_down(x, prev, j, rows) for j in (1, 2, 3)]
        dpre = dy_ref[...]
        if act:
            pre = b_ref[...] + wv[3:4, :] * xs[0]
            for j in (1, 2, 3):
                pre = pre + wv[3 - j:4 - j, :] * xs[j]
            dpre = dpre * _dsilu(pre)
        nxt = nxt_ref[...]
        dx = wv[3:4, :] * dpre
        for j in (1, 2, 3):
            dx = dx + wv[3 - j:4 - j, :] * _shift_up(dpre, nxt, j, rows)
        dx_ref[...] = dx.astype(dx_ref.dtype)
        dw_ref[...] += jnp.concatenate([_colsum(dpre * xs[3 - kk]) for kk in range(4)], axis=0)
        db_ref[...] += _colsum(dpre)
        nxt_ref[...] = dpre

    return pl.pallas_call(
        body, name=name, grid=(n,),
        in_specs=[pl.BlockSpec((tt, width), lambda i: (n - 1 - i, cb)),
                  pl.BlockSpec((tt, width), lambda i: (jnp.maximum(n - 2 - i, 0), cb)),
                  pl.BlockSpec((tt, width), lambda i: (n - 1 - i, 0)),
                  pl.BlockSpec((4, width), lambda i: (0, 0)), pl.BlockSpec((1, width), lambda i: (0, 0))],
        out_specs=[pl.BlockSpec((tt, width), lambda i: (n - 1 - i, 0)),
                   pl.BlockSpec((4, width), lambda i: (0, 0)), pl.BlockSpec((1, width), lambda i: (0, 0))],
        out_shape=[jax.ShapeDtypeStruct((t, width), BF16), jax.ShapeDtypeStruct((4, width), F32),
                   jax.ShapeDtypeStruct((1, width), F32)],
        scratch_shapes=[pltpu.VMEM((tt, width), F32)],
        compiler_params=_params(("arbitrary",)),
    )(src, src, dy, w, b)


S5_CW = 256


def _cmul(ar, ai, br, bi):
    return ar * br - ai * bi, ar * bi + ai * br


def _scan8_complex(src_ref, dst_ref, lam_ref, st_ref, *, w, nb, reverse):
    rows = lax.broadcasted_iota(jnp.int32, (8, S5_CW), 0)
    b8 = lambda v: jnp.broadcast_to(v, (8, S5_CW))

    def shift(x, k):
        if reverse:
            return jnp.where(rows < 8 - k, pltpu.roll(x, 8 - k, 0), 0.0)
        return jnp.where(rows >= k, pltpu.roll(x, k, 0), 0.0)

    for c0 in range(0, w, S5_CW):
        re, im = pl.ds(c0, S5_CW), pl.ds(w + c0, S5_CW)
        pw = [(lam_ref[:, re], lam_ref[:, im])]
        for _ in range(7):
            pw.append(_cmul(*pw[-1], *pw[0]))
        pr, pi = b8(pw[7][0]), b8(pw[7][1])
        for j in range(7):
            sel = rows == (7 - j if reverse else j)
            pr, pi = jnp.where(sel, b8(pw[j][0]), pr), jnp.where(sel, b8(pw[j][1]), pi)
        steps = [(k, b8(pw[k - 1][0]), b8(pw[k - 1][1])) for k in (1, 2, 4)]
        edge = 0 if reverse else 7

        def blk(i, carry):
            hr, hi = carry
            base = pl.multiple_of((nb // 2 - 1 - i if reverse else i) * 16, 16)
            pend = []
            for off in ((8, 0) if reverse else (0, 8)):
                at = pl.ds(base + off, 8)
                xr, xi = src_ref[at, re], src_ref[at, im]
                for k, kr, ki in steps:
                    sr, si = shift(xr, k), shift(xi, k)
                    xr, xi = xr + kr * sr - ki * si, xi + kr * si + ki * sr
                pend.append((at, xr, xi))
            for at, xr, xi in pend:
                xr, xi = xr + pr * hr - pi * hi, xi + pr * hi + pi * hr
                dst_ref[at, re] = xr
                dst_ref[at, im] = xi
                hr, hi = b8(xr[edge:edge + 1, :]), b8(xi[edge:edge + 1, :])
            return hr, hi

        hr, hi = lax.fori_loop(0, nb // 2, blk, (st_ref[:, re], st_ref[:, im]))
        st_ref[:, re] = hr
        st_ref[:, im] = hi


def s5_fwd(proj, bcat, lam, ccat, *, name):
    t = proj.shape[0]
    tt = min(ROW_TILE, t)
    w2 = bcat.shape[1]

    def body(u_ref, b_ref, lam_ref, c_ref, h_ref, y_ref, bu_ref, st_ref):
        @pl.when(pl.program_id(0) == 0)
        def _():
            st_ref[...] = jnp.zeros_like(st_ref)

        bu_ref[...] = _dot(u_ref[...], b_ref[...])
        _scan8_complex(bu_ref, h_ref, lam_ref, st_ref, w=w2 // 2, nb=tt // 8, reverse=False)
        y_ref[...] = _dot(h_ref[...], c_ref[...])

    fixed = lambda a: pl.BlockSpec(a.shape, lambda i: (0, 0))
    return pl.pallas_call(
        body, name=name, grid=(t // tt,),
        in_specs=[pl.BlockSpec((tt, S5_WIDTH), lambda i: (i, P_U // S5_WIDTH)), fixed(bcat), fixed(lam), fixed(ccat)],
        out_specs=[pl.BlockSpec((tt, w2), lambda i: (i, 0)), pl.BlockSpec((tt, S5_WIDTH), lambda i: (i, 0))],
        out_shape=[jax.ShapeDtypeStruct((t, w2), F32), jax.ShapeDtypeStruct((t, S5_WIDTH), F32)],
        scratch_shapes=[pltpu.VMEM((tt, w2), F32), pltpu.VMEM((8, w2), F32)],
        compiler_params=_params(("arbitrary",)),
    )(proj, bcat, lam, ccat)


def s5_bwd(dylin, du_a, hs, proj, bcat, lam_adj, ccat, *, name):
    t = proj.shape[0]
    tt = min(ROW_TILE, t)
    n, w2 = t // tt, bcat.shape[1]
    w = w2 // 2

    def body(dy_ref, dua_ref, h_ref, hp_ref, u_ref, b_ref, lam_ref, c_ref,
             du_ref, dc_ref, db_ref, dar_ref, dai_ref, g_ref, st_ref):
        i = pl.program_id(0)

        @pl.when(i == 0)
        def _():
            for r in (st_ref, dc_ref, db_ref, dar_ref, dai_ref):
                r[...] = jnp.zeros_like(r)

        dy, h = dy_ref[...], h_ref[...]
        g_ref[...] = _dot(dy, c_ref[...], _NT)
        dc_ref[...] += _dot(h, dy, _TN)
        _scan8_complex(g_ref, g_ref, lam_ref, st_ref, w=w, nb=tt // 8, reverse=True)
        g = g_ref[...]
        du_ref[...] = (dua_ref[...] + _dot(g, b_ref[...], _NT)).astype(du_ref.dtype)
        db_ref[...] += _dot(u_ref[...], g, _TN)
        rows = lax.broadcasted_iota(jnp.int32, (tt, w2), 0)
        before = jnp.where(i == n - 1, 0.0, hp_ref[7:8, :])
        hprev = jnp.where(rows == 0, before, pltpu.roll(h, 1, 0))
        gr, gi, hr, hi = g[:, :w], g[:, w:], hprev[:, :w], hprev[:, w:]
        dar_ref[...] += _colsum(gr * hr + gi * hi)
        dai_ref[...] += _colsum(gi * hr - gr * hi)

    rev = lambda i: n - 1 - i
    row = lambda wd, cb=0: pl.BlockSpec((tt, wd), lambda i: (rev(i), cb))
    fixed = lambda shape: pl.BlockSpec(shape, lambda i: (0, 0))
    return pl.pallas_call(
        body, name=name, grid=(n,),
        in_specs=[row(S5_WIDTH), row(S5_WIDTH), row(w2),
                  pl.BlockSpec((8, w2), lambda i: (jnp.maximum(rev(i) * (tt // 8) - 1, 0), 0)),
                  row(S5_WIDTH, P_U // S5_WIDTH), fixed(bcat.shape), fixed(lam_adj.shape), fixed(ccat.shape)],
        out_specs=[row(S5_WIDTH), fixed(ccat.shape), fixed(bcat.shape), fixed((1, w)), fixed((1, w))],
        out_shape=[jax.ShapeDtypeStruct((t, S5_WIDTH), BF16), jax.ShapeDtypeStruct(ccat.shape, F32),
                   jax.ShapeDtypeStruct(bcat.shape, F32), jax.ShapeDtypeStruct((1, w), F32), jax.ShapeDtypeStruct((1, w), F32)],
        scratch_shapes=[pltpu.VMEM((tt, w2), F32), pltpu.VMEM((8, w2), F32)],
        compiler_params=_params(("arbitrary",)),
    )(dylin, du_a, hs, hs, proj, bcat, lam_adj, ccat)


def scan_real(a, b, *, reverse, name):
    t, w = b.shape
    tt = min(ROW_TILE, t)
    n, nb = t // tt, tt // 8

    def body(a_ref, b_ref, o_ref, st_ref):
        @pl.when(pl.program_id(0) == 0)
        def _():
            st_ref[...] = jnp.zeros_like(st_ref)

        rows = lax.broadcasted_iota(jnp.int32, (8, w), 0)

        def blk(i, h):
            base = pl.multiple_of((nb - 1 - i if reverse else i) * 8, 8)
            ta_, tb_ = a_ref[pl.ds(base, 8), :], b_ref[pl.ds(base, 8), :]
            out = jnp.zeros((8, w), F32)
            for j in (range(7, -1, -1) if reverse else range(8)):
                h = jnp.broadcast_to(ta_[j:j + 1, :], (8, w)) * h + jnp.broadcast_to(tb_[j:j + 1, :], (8, w))
                out = jnp.where(rows == j, h, out)
            o_ref[pl.ds(base, 8), :] = out
            return h

        st_ref[...] = lax.fori_loop(0, nb, blk, st_ref[...])

    idx = (lambda i: (n - 1 - i, 0)) if reverse else (lambda i: (i, 0))
    return pl.pallas_call(
        body, name=name, grid=(n,),
        in_specs=[pl.BlockSpec((tt, w), idx), pl.BlockSpec((tt, w), idx)],
        out_specs=pl.BlockSpec((tt, w), idx), out_shape=jax.ShapeDtypeStruct((t, w), F32),
        scratch_shapes=[pltpu.VMEM((8, w), F32)],
        compiler_params=_params(("arbitrary",)),
    )(a, b)


SSD_QQ = SSD_HEADS * SSD_CHUNK
SSD_GP = SSD_WIDTH // 2
SSD_GQ = SSD_QQ // 2


def _ssd_spread():
    h = jnp.arange(LANE)[:, None]
    spread_p = (jnp.arange(SSD_WIDTH)[None, :] // SSD_HEAD_DIM == h).astype(BF16)
    spread_q = (jnp.arange(SSD_QQ)[None, :] // SSD_CHUNK == h).astype(BF16)
    return spread_p, spread_q


def _ssd_prologue(dt_ref, prow_ref, sp_ref, sq_ref):
    q = SSD_CHUNK
    r = lax.broadcasted_iota(jnp.int32, (q, q), 0)
    c = lax.broadcasted_iota(jnp.int32, (q, q), 1)
    raw_c = dt_ref[...] + prow_ref[0:1, :]
    dt_c = _softplus(raw_c)
    a_r = -jnp.exp(prow_ref[1:2, :])
    cs_c = _dot_mask((r >= c).astype(F32), dt_c * a_r, mask_left=True, parts=3)
    both = _dot_mask(jnp.concatenate([dt_c, cs_c], axis=0), sp_ref[...], mask_left=False, parts=3)
    dt_x, cs_x = both[:q], both[q:]
    csx = _dot_mask(cs_c, sq_ref[...], mask_left=False, parts=3)
    rr = lax.broadcasted_iota(jnp.int32, (q, SSD_QQ), 0)
    ss = lax.broadcasted_iota(jnp.int32, (q, SSD_QQ), 1) & (q - 1)
    diag = rr == ss
    cs_row = _colsum(jnp.where(diag, csx, 0.0))
    lcat = jnp.exp(jnp.where(rr >= ss, csx - cs_row, -1e30))
    cl = cs_x[q - 1:q, :]
    return dict(raw_c=raw_c, dt_c=dt_c, a_r=a_r, dt_x=dt_x, cs_x=cs_x, lcat=lcat, diag=diag,
                ecs=jnp.exp(cs_x), wdec=jnp.exp(cl - cs_x), ecl=jnp.exp(cl), triu=(r <= c).astype(F32))


def _ssd_group(xbc_ref, g, lcat, xdt):
    ns, q = SSD_STATE, SSD_CHUNK
    bm = xbc_ref[:, pl.ds(SSD_WIDTH + g * ns, ns)]
    cm = xbc_ref[:, pl.ds(SSD_WIDTH + 2 * ns + g * ns, ns)]
    cb = _dot(cm, bm, _NT)
    lg = lcat[:, g * SSD_GQ:(g + 1) * SSD_GQ]
    wcat = jnp.concatenate([cb] * 4, axis=1) * lg
    head = lax.broadcasted_iota(jnp.int32, (1, SSD_GP), 1) // SSD_HEAD_DIM
    xg = xdt[:, g * SSD_GP:(g + 1) * SSD_GP]
    xbd = jnp.concatenate([jnp.where(head == j, xg, 0.0) for j in range(4)], axis=0)
    return bm, cm, lg, wcat, xbd, head


def _ssd_gate(yraw, z, nw):
    yg = yraw * _silu(z)
    r = lax.rsqrt(jnp.mean(yg * yg, axis=1, keepdims=True) + LN_EPS)
    return yg, r


def _ssd_specs(q, idx):
    return [pl.BlockSpec((q, SSD_XBC), lambda i: (idx(i), 0)),
            pl.BlockSpec((q, SSD_WIDTH), lambda i: (idx(i), P_Z // SSD_WIDTH)),
            pl.BlockSpec((q, LANE), lambda i: (idx(i), P_DT // LANE)),
            pl.BlockSpec((8, LANE), lambda i: (0, 0)), pl.BlockSpec((1, SSD_WIDTH), lambda i: (0, 0)),
            pl.BlockSpec((1, SSD_WIDTH), lambda i: (0, 0)),
            pl.BlockSpec((LANE, SSD_WIDTH), lambda i: (0, 0)), pl.BlockSpec((LANE, SSD_QQ), lambda i: (0, 0))]


def ssd_fwd(xbc, proj, prow, d_x, nw, *, name):
    t = xbc.shape[0]
    q, ns = SSD_CHUNK, SSD_STATE
    nc = t // q
    spread_p, spread_q = _ssd_spread()

    def body(xbc_ref, z_ref, dt_ref, prow_ref, dx_ref, nw_ref, sp_ref, sq_ref, y_ref, yraw_ref, sall_ref, s_ref):
        @pl.when(pl.program_id(0) == 0)
        def _():
            s_ref[...] = jnp.zeros_like(s_ref)

        sall_ref[0] = s_ref[...]
        pr = _ssd_prologue(dt_ref, prow_ref, sp_ref, sq_ref)
        xs = xbc_ref[:, pl.ds(0, SSD_WIDTH)]
        xdt = xs * pr["dt_x"]
        xw = xdt * pr["wdec"]
        ys = []
        for g in range(2):
            gp = slice(g * SSD_GP, (g + 1) * SSD_GP)
            bm, cm, lg, wcat, xbd, head = _ssd_group(xbc_ref, g, pr["lcat"], xdt)
            st = s_ref[:, gp]
            ys.append(_dot(wcat, xbd) + pr["ecs"][:, gp] * _dot(cm, st) + xs[:, gp] * dx_ref[:, gp])
            s_ref[:, gp] = pr["ecl"][:, gp] * st + _dot(bm, xw[:, gp], _TN)
        yraw = jnp.concatenate(ys, axis=1)
        yraw_ref[...] = yraw
        yg, r = _ssd_gate(yraw, z_ref[...], nw_ref[...])
        y_ref[...] = (yg * r * nw_ref[...]).astype(y_ref.dtype)

    row = pl.BlockSpec((q, SSD_WIDTH), lambda i: (i, 0))
    return pl.pallas_call(
        body, name=name, grid=(nc,),
        in_specs=_ssd_specs(q, lambda i: i),
        out_specs=[row, row, pl.BlockSpec((1, ns, SSD_WIDTH), lambda i: (i, 0, 0))],
        out_shape=[jax.ShapeDtypeStruct((t, SSD_WIDTH), BF16), jax.ShapeDtypeStruct((t, SSD_WIDTH), F32),
                   jax.ShapeDtypeStruct((nc, ns, SSD_WIDTH), F32)],
        scratch_shapes=[pltpu.VMEM((ns, SSD_WIDTH), F32)],
        compiler_params=_params(("arbitrary",)),
    )(xbc, proj, proj, prow, d_x, nw, spread_p, spread_q)


def ssd_bwd(xbc, proj, prow, d_x, nw, yraw, sall, dout, *, name):
    t = xbc.shape[0]
    q, ns = SSD_CHUNK, SSD_STATE
    nc = t // q
    spread_p, spread_q = _ssd_spread()

    def body(xbc_ref, z_ref, dt_ref, prow_ref, dx_ref, nw_ref, sp_ref, sq_ref, yraw_ref, sall_ref, dout_ref,
             dxbc_ref, dz_ref, ddt_ref, dprm_ref, ddx_ref, dnw_ref, ds_ref):
        @pl.when(pl.program_id(0) == 0)
        def _():
            ds_ref[...] = jnp.zeros_like(ds_ref)
            dprm_ref[...] = jnp.zeros_like(dprm_ref)
            ddx_ref[...] = jnp.zeros_like(ddx_ref)
            dnw_ref[...] = jnp.zeros_like(dnw_ref)

        yraw, z, nwv, dout = yraw_ref[...], z_ref[...], nw_ref[...], dout_ref[...]
        yg, r = _ssd_gate(yraw, z, nwv)
        dnw_ref[...] += _colsum(dout * yg * r)
        dyn = dout * nwv
        dyg = r * dyn - yg * (r * r * r) * jnp.mean(dyn * yg, axis=1, keepdims=True)
        dy = dyg * _silu(z)
        dz_ref[...] = (dyg * yraw * _dsilu(z)).astype(dz_ref.dtype)

        pr = _ssd_prologue(dt_ref, prow_ref, sp_ref, sq_ref)
        xs = xbc_ref[:, pl.ds(0, SSD_WIDTH)]
        xdt = xs * pr["dt_x"]
        wdec, ecl = pr["wdec"], pr["ecl"]
        xw = xdt * wdec
        dzm_all = pr["ecs"] * dy
        last = (lax.broadcasted_iota(jnp.int32, (q, 1), 0) == q - 1).astype(F32)
        dxs, dcsxs, es = [], [], []
        for g in range(2):
            gp = slice(g * SSD_GP, (g + 1) * SSD_GP)
            bm, cm, lg, wcat, xbd, head = _ssd_group(xbc_ref, g, pr["lcat"], xdt)
            dyg_ = dy[:, gp]
            dwcat = _dot(dyg_, xbd, _NT)
            dxbd = _dot(wcat, dyg_, _TN)
            dxg = sum(jnp.where(head == j, dxbd[j * q:(j + 1) * q], 0.0) for j in range(4))
            es.append(dwcat * wcat)
            dmm = dwcat * lg
            dm = dmm[:, 0:q] + dmm[:, q:2 * q] + dmm[:, 2 * q:3 * q] + dmm[:, 3 * q:4 * q]
            dcm = _dot(dm, bm)
            dbm = _dot(dm, cm, _TN)
            st = sall_ref[0, :, gp]
            zmat = _dot(cm, st)
            dzm = dzm_all[:, gp]
            dcm = dcm + _dot(dzm, st, _NT)
            dst = _dot(cm, dzm, _TN)
            dcsx = dzm * zmat
            dsn = ds_ref[:, gp]
            dst = dst + ecl[:, gp] * dsn
            dclx = _colsum(dsn * st) * ecl[:, gp]
            dxw = _dot(bm, dsn)
            dbm = dbm + _dot(xw[:, gp], dsn, _NT)
            dxg = dxg + wdec[:, gp] * dxw
            tw = dxw * xdt[:, gp] * wdec[:, gp]
            dclx = dclx + _colsum(tw)
            dcsxs.append(dcsx - tw + last * dclx)
            ds_ref[:, gp] = dst
            dxs.append(dxg)
            dxbc_ref[:, pl.ds(SSD_WIDTH + g * ns, ns)] = dbm
            dxbc_ref[:, pl.ds(SSD_WIDTH + 2 * ns + g * ns, ns)] = dcm
        dx = jnp.concatenate(dxs, axis=1)
        dxbc_ref[:, pl.ds(0, SSD_WIDTH)] = dx * pr["dt_x"] + dy * dx_ref[...]
        ddx_ref[...] += _colsum(dy * xs)
        red = _dot_mask(jnp.concatenate([jnp.concatenate(dcsxs, axis=1), dx * xs], axis=0), sp_ref[...], _NT,
                        mask_left=False, parts=2)
        e_all = jnp.concatenate(es, axis=1)
        e_red = _dot_mask(e_all - jnp.where(pr["diag"], _colsum(e_all), 0.0), sq_ref[...], _NT, mask_left=False, parts=2)
        dadt = _dot_mask(pr["triu"], red[:q] + e_red, mask_left=True, parts=2)
        draw = (red[q:] + dadt * pr["a_r"]) * _sigmoid(pr["raw_c"])
        ddt_ref[...] = draw.astype(ddt_ref.dtype)
        zero = jnp.zeros((6, LANE), F32)
        dprm_ref[...] += jnp.concatenate([_colsum(draw), _colsum(dadt * pr["dt_c"]) * pr["a_r"], zero], axis=0)

    rev = lambda i: nc - 1 - i
    row = lambda w: pl.BlockSpec((q, w), lambda i: (rev(i), 0))
    fixed = lambda shape: pl.BlockSpec(shape, lambda i: (0, 0))
    return pl.pallas_call(
        body, name=name, grid=(nc,),
        in_specs=_ssd_specs(q, rev) + [row(SSD_WIDTH), pl.BlockSpec((1, ns, SSD_WIDTH), lambda i: (rev(i), 0, 0)),
                                       row(SSD_WIDTH)],
        out_specs=[row(SSD_XBC), row(SSD_WIDTH), row(LANE), fixed((8, LANE)), fixed((1, SSD_WIDTH)), fixed((1, SSD_WIDTH))],
        out_shape=[jax.ShapeDtypeStruct((t, SSD_XBC), F32), jax.ShapeDtypeStruct((t, SSD_WIDTH), BF16),
                   jax.ShapeDtypeStruct((t, LANE), BF16), jax.ShapeDtypeStruct((8, LANE), F32),
                   jax.ShapeDtypeStruct((1, SSD_WIDTH), F32), jax.ShapeDtypeStruct((1, SSD_WIDTH), F32)],
        scratch_shapes=[pltpu.VMEM((ns, SSD_WIDTH), F32)],
        compiler_params=_params(("arbitrary",)),
    )(xbc, proj, proj, prow, d_x, nw, spread_p, spread_q, yraw, sall, dout)


def _me():
    return lax.axis_index("x"), lax.axis_index("y"), lax.axis_index("c")


_ANY = pl.BlockSpec(memory_space=pl.ANY)
_MESH = pl.DeviceIdType.MESH


def all_gather(block, *, name):
    def body(src, dst, send_sems, recv_sems, local_sem):
        x, y, c = _me()
        me, sibling = (x, y, c), (x, y, 1 - c)
        chips = [(1 - x, y), (x, 1 - y), (1 - x, 1 - y)]

        def slot(px, py, pc):
            return dst.at[4 * px + 2 * py + pc]

        def copy(kk, blk, to, from_src=False):
            return pltpu.make_async_remote_copy(
                src_ref=src if from_src else slot(*blk), dst_ref=slot(*blk),
                send_sem=send_sems.at[kk], recv_sem=recv_sems.at[kk], device_id=to, device_id_type=_MESH)

        mine = pltpu.make_async_copy(src, slot(*me), local_sem)
        mine.start()
        first = [copy(0, me, sibling, True)] + [copy(1 + j, me, (*chip, c), True) for j, chip in enumerate(chips)]
        for cp in first:
            cp.start()
        passed = [copy(4 + j, (*chip, c), sibling) for j, chip in enumerate(chips)]
        for j, chip in enumerate(chips):
            copy(1 + j, (*chip, c), me).wait_recv()
            passed[j].start()
        copy(0, sibling, me).wait_recv()
        for j, chip in enumerate(chips):
            copy(4 + j, (*chip, 1 - c), me).wait_recv()
        for cp in first + passed:
            cp.wait_send()
        mine.wait()

    return pl.pallas_call(
        body, name=name, in_specs=[_ANY], out_specs=_ANY,
        out_shape=jax.ShapeDtypeStruct((N_DEV,) + block.shape, block.dtype),
        scratch_shapes=[pltpu.SemaphoreType.DMA((7,)), pltpu.SemaphoreType.DMA((7,)), pltpu.SemaphoreType.DMA(())],
    )(block)


RS_PIECES = 4


def rs_sibling_exchange(halves, *, name):
    _, nq, r, l = halves.shape
    rows = r // RS_PIECES
    assert r % RS_PIECES == 0 and rows % 16 == 0

    def body(src, dst, send_sems, recv_sems):
        x, y, c = _me()
        copies = []
        for q in range(nq):
            for i in range(RS_PIECES):
                kk = q * RS_PIECES + i
                cp = pltpu.make_async_remote_copy(
                    src_ref=src.at[1 - c, q, pl.ds(i * rows, rows)], dst_ref=dst.at[q, pl.ds(i * rows, rows)],
                    send_sem=send_sems.at[kk], recv_sem=recv_sems.at[kk], device_id=(x, y, 1 - c), device_id_type=_MESH)
                cp.start()
                copies.append(cp)
        for cp in copies:
            cp.wait()

    n_copies = nq * RS_PIECES
    return pl.pallas_call(
        body, name=name, in_specs=[_ANY], out_specs=_ANY,
        out_shape=jax.ShapeDtypeStruct((nq, r, l), halves.dtype),
        scratch_shapes=[pltpu.SemaphoreType.DMA((n_copies,)), pltpu.SemaphoreType.DMA((n_copies,))],
    )(halves)


def pair_sum_bf16(halves, theirs, *, name, tt=128):
    _, nq, r, wd = halves.shape
    tt = min(tt, r)
    parity = lax.axis_index("c").astype(jnp.int32).reshape(1)

    def body(c_ref, own_ref, sib_ref, o_ref):
        o_ref[...] = (own_ref[...] + sib_ref[...]).astype(BF16)

    return pl.pallas_call(
        body, name=name,
        grid_spec=pltpu.PrefetchScalarGridSpec(
            num_scalar_prefetch=1, grid=(nq, r // tt),
            in_specs=[pl.BlockSpec((None, None, tt, wd), lambda q, i, c: (c[0], q, i, 0)),
                      pl.BlockSpec((None, tt, wd), lambda q, i, c: (q, i, 0))],
            out_specs=pl.BlockSpec((None, tt, wd), lambda q, i, c: (q, i, 0))),
        out_shape=jax.ShapeDtypeStruct((nq, r, wd), BF16),
        compiler_params=_params(("parallel", "parallel")),
    )(parity, halves, theirs)


def rs_chip_exchange(part, *, name):
    def body(src, dst, send_sems, recv_sems, local_sem):
        x, y, c = _me()
        q_me = 2 * x + y
        local = pltpu.make_async_copy(src.at[q_me], dst.at[q_me], local_sem)
        local.start()
        copies = []
        for j, (px, py) in enumerate([(1 - x, y), (x, 1 - y), (1 - x, 1 - y)]):
            cp = pltpu.make_async_remote_copy(src_ref=src.at[2 * px + py], dst_ref=dst.at[q_me], send_sem=send_sems.at[j],
                                              recv_sem=recv_sems.at[j], device_id=(px, py, c), device_id_type=_MESH)
            cp.start()
            copies.append(cp)
        for cp in copies:
            cp.wait()
        local.wait()

    return pl.pallas_call(
        body, name=name, in_specs=[_ANY], out_specs=_ANY,
        out_shape=jax.ShapeDtypeStruct(part.shape, part.dtype),
        scratch_shapes=[pltpu.SemaphoreType.DMA((3,)), pltpu.SemaphoreType.DMA((3,)), pltpu.SemaphoreType.DMA(())],
    )(part)


def adamw(slabs, w, m, v, *, name, tt):
    ns, (r, wd) = slabs.shape[0], w.shape
    tt = min(tt, r)
    assert r % tt == 0

    def body(s_ref, w_ref, m_ref, v_ref, g_ref, d_ref, nm_ref, nv_ref):
        g = s_ref[0].astype(F32)
        for kdev in range(1, ns):
            g = g + s_ref[kdev].astype(F32)
        wv = w_ref[...]
        nm = ADAM_B1 * m_ref[...] + (1.0 - ADAM_B1) * g
        nv = ADAM_B2 * v_ref[...] + (1.0 - ADAM_B2) * (g * g)
        m_hat = nm / (1.0 - ADAM_B1 ** ADAM_STEP)
        v_hat = nv / (1.0 - ADAM_B2 ** ADAM_STEP)
        g_ref[...] = g
        d_ref[...] = -ADAM_LR * (m_hat / (jnp.sqrt(v_hat) + ADAM_EPS) + ADAM_WD * wv)
        nm_ref[...] = nm
        nv_ref[...] = nv

    spec = pl.BlockSpec((tt, wd), lambda i: (i, 0))
    return pl.pallas_call(
        body, name=name, grid=(r // tt,),
        in_specs=[pl.BlockSpec((ns, tt, wd), lambda i: (0, i, 0)), spec, spec, spec],
        out_specs=[spec] * 4, out_shape=[jax.ShapeDtypeStruct((r, wd), F32)] * 4,
        compiler_params=_params(("parallel",)),
    )(slabs, w, m, v)


WIDE = 1024
BIG = [("w_in", True, 289), ("w_out", False, 128), ("xa_wq", False, 128), ("xa_wk", False, 128), ("xa_wv", False, 128),
       ("xa_wo", False, 128), ("mlp_w2", False, 512), ("mlp_w1", True, 512)]
TINY = [("ssd_conv_w", 2), ("s5_glu_w", 1), ("rg_conv_w", 2)]
KEEP_F32 = ("ssd_conv_w", "rg_conv_w")
TINY_ROWS = 32
SHARDED = [name for name, _, _ in BIG] + [name for name, _ in TINY]
SMALL = ["ssd_conv_b", "ssd_dt_bias", "ssd_a_log", "ssd_d", "ssd_norm_w", "s5_lam_re", "s5_lam_im",
         "s5_log_step", "s5_b_re", "s5_b_im", "s5_c_re", "s5_c_im", "s5_d", "s5_glu_b", "rg_conv_b",
         "rg_wa", "rg_ba", "rg_wx", "rg_bx", "rg_lambda", "ln1_g", "ln1_b", "ln2_g", "ln2_b", "ln3_g", "ln3_b"]
WEIGHTS = ['w_in', 'w_out', 'ssd_conv_w', 'ssd_conv_b', 'ssd_dt_bias', 'ssd_a_log', 'ssd_d', 'ssd_norm_w',
           's5_lam_re', 's5_lam_im', 's5_log_step', 's5_b_re', 's5_b_im', 's5_c_re', 's5_c_im', 's5_d',
           's5_glu_w', 's5_glu_b', 'rg_conv_w', 'rg_conv_b', 'rg_wa', 'rg_ba', 'rg_wx', 'rg_bx', 'rg_lambda',
           'ln1_g', 'ln1_b', 'xa_wq', 'xa_wk', 'xa_wv', 'xa_wo', 'ln2_g', 'ln2_b', 'mlp_w1', 'mlp_w2',
           'ln3_g', 'ln3_b']


def _pad16(rows):
    return -(-rows // 16) * 16


def _pack_rows(flat, mult):
    n = flat.shape[-1]
    r = -(-n // (LANE * mult)) * mult
    pad = [(0, 0)] * (flat.ndim - 1) + [(0, r * LANE - n)]
    return jnp.pad(flat, pad).reshape(flat.shape[:-1] + (r, LANE))


def _unpack(packed, shapes):
    lead = packed.shape[:-2]
    flat = packed.reshape(lead + (-1,))
    out, off = [], 0
    for s in shapes:
        n = math.prod(s)
        out.append(flat[..., off:off + n].reshape(lead + tuple(s)))
        off += n
    return out


def _big_block(x, rows):
    pad = [(0, 0)] * (x.ndim - 2) + [(0, _pad16(rows) - rows), (0, 0)]
    x = jnp.pad(x, pad)
    return x.reshape(x.shape[:-3] + (DEPTH * _pad16(rows), WIDE))


def _tiny_block(flat):
    pad = [(0, 0)] * (flat.ndim - 1) + [(0, TINY_ROWS * WIDE - flat.shape[-1])]
    return jnp.pad(flat, pad).reshape(flat.shape[:-1] + (TINY_ROWS, WIDE))


def _pack_wide(big, tiny_flat):
    blocks = [_big_block(big[name], rows) for name, _, rows in BIG] + [_tiny_block(tiny_flat)]
    return jnp.concatenate(blocks, axis=-2)


def _unpack_wide(packed):
    lead, big, off = packed.shape[:-2], {}, 0
    for name, _, rows in BIG:
        rp = _pad16(rows)
        big[name] = packed[..., off:off + DEPTH * rp, :].reshape(lead + (DEPTH, rp, WIDE))[..., :rows, :]
        off += DEPTH * rp
    return big, packed[..., off:off + TINY_ROWS, :].reshape(lead + (TINY_ROWS * WIDE,))


def _split_flat(flat, shapes):
    out, off = [], 0
    for s in shapes:
        n = math.prod(s)
        out.append(flat[..., off:off + n].reshape(flat.shape[:-1] + tuple(s)))
        off += n
    return out


def _to_full(gathered, axis):
    g = jnp.moveaxis(gathered, 0, axis)
    s = g.shape
    return g.reshape(s[:axis] + (s[axis] * s[axis + 1],) + s[axis + 2:])


def _to_slabs(full, axis):
    s = full.shape
    g = full.reshape(s[:axis] + (N_DEV, s[axis] // N_DEV) + s[axis + 1:])
    return jnp.moveaxis(g, axis, 0)


def _blockdiag(w):
    h, i, j = w.shape
    eye = jnp.eye(h, dtype=w.dtype)
    return (w[:, :, None, :] * eye[:, None, :, None]).reshape(h * i, h * j)


def _blockdiag_extract(m, h):
    i, j = m.shape[0] // h, m.shape[1] // h
    eye = jnp.eye(h, dtype=m.dtype)
    return (m.reshape(h, i, h, j) * eye[:, None, :, None]).sum(axis=2)


def _s5_disc(lr, li, ls, bre, bim):
    step = jnp.exp(ls)[:, None]
    er = jnp.exp(lr * step)
    ar, ai = er * jnp.cos(li * step), er * jnp.sin(li * step)
    nr, ni, den = ar - 1.0, ai, lr * lr + li * li
    qr, qi = (nr * lr + ni * li) / den, (ni * lr - nr * li) / den
    bbr = qr[..., None] * bre - qi[..., None] * bim
    bbi = qr[..., None] * bim + qi[..., None] * bre
    return ar, ai, bbr, bbi


def _row(v, width=None):
    v = v.reshape(1, -1)
    if width is not None and v.shape[1] < width:
        v = jnp.pad(v, ((0, 0), (0, width - v.shape[1])))
    return v


def _relu2(a):
    r = jnp.maximum(a, 0.0)
    return r * r


def _add_alpha(acc, d):
    return acc + ALPHA * d


def _shift_rows_down(x):
    return jnp.concatenate([jnp.zeros((1, x.shape[1]), x.dtype), x[:-1]], axis=0)


def _shift_rows_up(x):
    return jnp.concatenate([x[1:], jnp.zeros((1, x.shape[1]), x.dtype)], axis=0)


def _layer_params(full, small, l):
    p = {}
    w_in = full["w_in"][l]
    z, xbc, dt, u, xr, g = w_in[0:512], w_in[512:1536], w_in[1536:1544], w_in[1544:1800], w_in[1800:2056], w_in[2056:2312]
    p["w_inp"] = jnp.concatenate([xbc, z, u, xr, g, dt, jnp.zeros((D_INP - P_DT - 8, D_MODEL), w_in.dtype)], axis=0)
    for k_ in ("w_out", "xa_wq", "xa_wk", "xa_wv", "xa_wo", "mlp_w1", "mlp_w2", "s5_glu_w"):
        p[k_] = full[k_][l]
    p["ssd_cw"], p["ssd_cb"] = full["ssd_conv_w"][l], _row(small["ssd_conv_b"][l])
    dtb, alog, dsk = small["ssd_dt_bias"][l], small["ssd_a_log"][l], small["ssd_d"][l]
    p["prow"] = jnp.concatenate([_row(dtb, LANE), _row(alog, LANE), jnp.zeros((6, LANE), F32)], axis=0)
    p["ssd_dx"] = _row(jnp.repeat(dsk, SSD_HEAD_DIM))
    p["ssd_nw"] = _row(small["ssd_norm_w"][l])
    s5_in = (small["s5_lam_re"][l], small["s5_lam_im"][l], small["s5_log_step"][l], small["s5_b_re"][l], small["s5_b_im"][l])
    (ar, ai, bbr, bbi), p["s5_vjp"] = jax.vjp(_s5_disc, *s5_in)
    p["lam_fwd"] = jnp.concatenate([_row(ar), _row(ai)], axis=1)
    p["lam_adj"] = jnp.concatenate([_row(ar), _row(-ai)], axis=1)
    p["bcat"] = jnp.concatenate([_blockdiag(jnp.swapaxes(bbr, 1, 2)), _blockdiag(jnp.swapaxes(bbi, 1, 2))], axis=1)
    p["ccat"] = jnp.concatenate([_blockdiag(jnp.swapaxes(small["s5_c_re"][l], 1, 2)),
                                 -_blockdiag(jnp.swapaxes(small["s5_c_im"][l], 1, 2))], axis=0)
    p["s5_d"], p["s5_glu_b"] = _row(small["s5_d"][l]), _row(small["s5_glu_b"][l])
    p["rg_cw"], p["rg_cb"] = full["rg_conv_w"][l], _row(small["rg_conv_b"][l])
    p["rg_wa"], p["rg_wx"] = _blockdiag(small["rg_wa"][l]), _blockdiag(small["rg_wx"][l])
    p["rg_ba"], p["rg_bx"], p["rg_lam"] = _row(small["rg_ba"][l]), _row(small["rg_bx"][l]), _row(small["rg_lambda"][l])
    for i in (1, 2, 3):
        p[f"g{i}"], p[f"b{i}"] = _row(small[f"ln{i}_g"][l]), _row(small[f"ln{i}_b"][l])
    return p


def _layer_fwd(h0, mem, p):
    t = h0.shape[0]
    s = {"h0": h0}
    proj = mm(h0, p["w_inp"], tb=True, name="in_proj")
    xbc = conv_fwd(proj, 0, p["ssd_cw"], p["ssd_cb"], width=SSD_XBC, act=True, name="ssd_conv_fwd")
    y_ssd, yraw, sall = ssd_fwd(xbc, proj, p["prow"], p["ssd_dx"], p["ssd_nw"], name="ssd_fwd")
    hs5, ylin = s5_fwd(proj, p["bcat"], p["lam_fwd"], p["ccat"], name="s5_fwd")
    (y_s5,), _ = rowk(_s5_post_fwd_fn, [(ylin, S5_WIDTH, 0), (proj, S5_WIDTH, P_U // S5_WIDTH)],
                      [p["s5_d"], p["s5_glu_w"], p["s5_glu_b"]], [S5_WIDTH], [], rows=t, name="s5_post_fwd", out_dtypes=[BF16])
    xc = conv_fwd(proj, P_XR // RG_WIDTH, p["rg_cw"], p["rg_cb"], width=RG_WIDTH, act=False, name="rg_conv_fwd")
    rg_full = [p["rg_wa"], p["rg_wx"], p["rg_ba"], p["rg_bx"], p["rg_lam"]]
    (a_rg, b_rg), _ = rowk(_rg_pre_fwd_fn, [(xc, RG_WIDTH, 0)], rg_full, [RG_WIDTH, RG_WIDTH], [], rows=t, name="rg_pre_fwd")
    h_rg = scan_real(a_rg, b_rg, reverse=False, name="rg_scan_fwd")
    (y_rg,), _ = rowk(_rg_out_fwd_fn, [(h_rg, RG_WIDTH, 0), (proj, RG_WIDTH, P_G // RG_WIDTH)], [], [RG_WIDTH], [],
                      rows=t, name="rg_out_fwd", out_dtypes=[BF16])
    ycat = jnp.concatenate([y_ssd, y_s5, y_rg], axis=1)
    pre1, h1 = mm_ln(ycat, p["w_out"], h0, p["g1"], p["b1"], name="out_proj")
    q = mm(h1, p["xa_wq"], name="xa_q", out_dtype=BF16)
    k = mm(mem, p["xa_wk"], name="xa_kv")
    v = mm(mem, p["xa_wv"], name="xa_kv")
    (o,), _ = rowk(_attn_fwd_fn, [(q, D_MODEL, 0)], [k, v], [D_MODEL], [], rows=t, name="xa_fwd", out_dtypes=[BF16])
    pre2, h2 = mm_ln(o, p["xa_wo"], h1, p["g2"], p["b2"], name="xa_o")
    a_mlp = mm(h2, p["mlp_w1"], tb=True, name="mlp_up")
    pre3, h3 = mm_ln(a_mlp, p["mlp_w2"], h2, p["g3"], p["b3"], fa=_relu2, name="mlp_down")
    s.update(proj=proj, xbc=xbc, yraw=yraw, sall=sall, hs5=hs5, ylin=ylin, xc=xc, a_rg=a_rg, h_rg=h_rg,
             ycat=ycat, pre1=pre1, h1=h1, q=q, k=k, v=v, o=o, pre2=pre2, h2=h2, a_mlp=a_mlp, pre3=pre3)
    return h3, s


def _layer_bwd(dh3, mem, p, s, l, gfull, gsmall):
    t = dh3.shape[0]
    proj = s["proj"]
    dpre3, dg3, db3 = ln_bwd(s["pre3"], dh3, p["g3"], name="ln_bwd")
    da = mm(dpre3, p["mlp_w2"], tb=True, o_extra=(s["a_mlp"],), fo=lambda acc, a: acc * 2.0 * jnp.maximum(a, 0.0), name="mlp_da",
            out_dtype=BF16)
    gfull["mlp_w2"][l] = mm(s["a_mlp"], dpre3, ta=True, fa=_relu2, name="mlp_dw2")
    gfull["mlp_w1"][l] = mm(da, s["h2"], ta=True, name="mlp_dw1")
    dh2 = mm(da, p["mlp_w1"], o_extra=(dpre3,), fo=_add_alpha, name="mlp_dx")
    dpre2, dg2, db2 = ln_bwd(s["pre2"], dh2, p["g2"], name="ln_bwd")
    do = mm(dpre2, p["xa_wo"], tb=True, name="xa_do", out_dtype=BF16)
    gfull["xa_wo"][l] = mm(s["o"], dpre2, ta=True, name="dw_sq")
    (dq,), (dk, dv) = rowk(_attn_bwd_fn, [(s["q"], D_MODEL, 0), (do, D_MODEL, 0)], [s["k"], s["v"]], [D_MODEL],
                           [(256, D_MODEL), (256, D_MODEL)], rows=t, name="xa_bwd", out_dtypes=[BF16])
    gfull["xa_wq"][l] = mm(s["h1"], dq, ta=True, name="dw_sq")
    gfull["xa_wk"][l] = mm(mem, dk, ta=True, name="dw_kv")
    gfull["xa_wv"][l] = mm(mem, dv, ta=True, name="dw_kv")
    dh1 = mm(dq, p["xa_wq"], tb=True, o_extra=(dpre2,), fo=_add_alpha, name="dx_sq")
    dpre1, dg1, db1 = ln_bwd(s["pre1"], dh1, p["g1"], name="ln_bwd")
    dycat = mm(dpre1, p["w_out"], tb=True, name="xa_do")
    gfull["w_out"][l] = mm(s["ycat"], dpre1, ta=True, name="dw_sq")
    (dh_rg, dg_rg), _ = rowk(_rg_out_bwd_fn, [(s["h_rg"], RG_WIDTH, 0), (proj, RG_WIDTH, P_G // RG_WIDTH), (dycat, RG_WIDTH, 3)],
                             [], [RG_WIDTH, RG_WIDTH], [], rows=t, name="rg_out_bwd", out_dtypes=[F32, BF16])
    g_rg = scan_real(_shift_rows_up(s["a_rg"]), dh_rg, reverse=True, name="rg_scan_bwd")
    rg_full = [p["rg_wa"], p["rg_wx"], p["rg_ba"], p["rg_bx"], p["rg_lam"]]
    (dxc,), (dwa, dwx, dba, dbx, dlam) = rowk(
        _rg_pre_bwd_fn, [(s["xc"], RG_WIDTH, 0), (g_rg, RG_WIDTH, 0), (_shift_rows_down(s["h_rg"]), RG_WIDTH, 0)], rg_full,
        [RG_WIDTH], [(RG_WIDTH, RG_WIDTH), (RG_WIDTH, RG_WIDTH), (1, RG_WIDTH), (1, RG_WIDTH), (1, RG_WIDTH)],
        rows=t, name="rg_pre_bwd")
    dxr, d_rgcw, d_rgcb = conv_bwd(proj, P_XR // RG_WIDTH, dxc, p["rg_cw"], p["rg_cb"], width=RG_WIDTH, act=False, name="rg_conv_bwd")
    (dylin, du_a), (d_s5d, d_gluw, d_glub) = rowk(
        _s5_post_bwd_fn, [(s["ylin"], S5_WIDTH, 0), (proj, S5_WIDTH, P_U // S5_WIDTH), (dycat, S5_WIDTH, 2)],
        [p["s5_d"], p["s5_glu_w"], p["s5_glu_b"]], [S5_WIDTH, S5_WIDTH],
        [(1, S5_WIDTH), (S5_WIDTH, S5_WIDTH), (1, S5_WIDTH)], rows=t, name="s5_post_bwd")
    du, dccat, dbcat, dar, dai = s5_bwd(dylin, du_a, s["hs5"], proj, p["bcat"], p["lam_adj"], p["ccat"], name="s5_bwd")
    dxbc_act, dz, ddt, dprm, ddx, dnw = ssd_bwd(s["xbc"], proj, p["prow"], p["ssd_dx"], p["ssd_nw"], s["yraw"], s["sall"], dycat,
                                               name="ssd_bwd")
    dxbc, d_scw, d_scb = conv_bwd(proj, 0, dxbc_act, p["ssd_cw"], p["ssd_cb"], width=SSD_XBC, act=True, name="ssd_conv_bwd")
    dproj = jnp.concatenate([dxbc, dz, du, dxr, dg_rg, ddt, jnp.zeros((t, D_INP - P_DT - LANE), BF16)], axis=1)
    dh0 = mm(dproj, p["w_inp"], o_extra=(dpre1,), fo=_add_alpha, name="in_proj_dx")
    dwp = mm(dproj, s["h0"], ta=True, name="in_proj_dw")
    gfull["w_in"][l] = jnp.concatenate([dwp[P_Z:P_Z + 512], dwp[P_XBC:P_XBC + 1024], dwp[P_DT:P_DT + 8],
                                        dwp[P_U:P_U + 256], dwp[P_XR:P_XR + 256], dwp[P_G:P_G + 256]], axis=0)
    gfull["ssd_conv_w"][l], gfull["rg_conv_w"][l], gfull["s5_glu_w"][l] = d_scw, d_rgcw, d_gluw
    ng, ns = S5_GROUPS, S5_STATE
    dbbr = jnp.swapaxes(_blockdiag_extract(dbcat[:, :S5_NSTATE], ng), 1, 2)
    dbbi = jnp.swapaxes(_blockdiag_extract(dbcat[:, S5_NSTATE:], ng), 1, 2)
    d_lr, d_li, d_ls, d_bre, d_bim = p["s5_vjp"]((dar.reshape(ng, ns), dai.reshape(ng, ns), dbbr, dbbi))
    gsmall["s5_lam_re"][l], gsmall["s5_lam_im"][l], gsmall["s5_log_step"][l] = d_lr, d_li, d_ls
    gsmall["s5_b_re"][l], gsmall["s5_b_im"][l] = d_bre, d_bim
    gsmall["s5_c_re"][l] = jnp.swapaxes(_blockdiag_extract(dccat[:S5_NSTATE], ng), 1, 2)
    gsmall["s5_c_im"][l] = -jnp.swapaxes(_blockdiag_extract(dccat[S5_NSTATE:], ng), 1, 2)
    gsmall["s5_d"][l], gsmall["s5_glu_b"][l] = d_s5d[0], d_glub[0]
    gsmall["ssd_conv_b"][l], gsmall["rg_conv_b"][l] = d_scb[0], d_rgcb[0]
    gsmall["ssd_dt_bias"][l], gsmall["ssd_a_log"][l] = dprm[0, :8], dprm[1, :8]
    gsmall["ssd_d"][l] = ddx.reshape(SSD_HEADS, SSD_HEAD_DIM).sum(axis=1)
    gsmall["ssd_norm_w"][l] = dnw[0]
    gsmall["rg_wa"][l], gsmall["rg_wx"][l] = _blockdiag_extract(dwa, RG_BLOCKS), _blockdiag_extract(dwx, RG_BLOCKS)
    gsmall["rg_ba"][l], gsmall["rg_bx"][l] = dba.reshape(RG_BLOCKS, RG_BLOCK_DIM), dbx.reshape(RG_BLOCKS, RG_BLOCK_DIM)
    gsmall["rg_lambda"][l] = dlam[0]
    for i, (dg, db) in zip((1, 2, 3), ((dg1, db1), (dg2, db2), (dg3, db3))):
        gsmall[f"ln{i}_g"][l], gsmall[f"ln{i}_b"][l] = dg[0], db[0]
    return dh0


def _step(a):
    h = a["x"][0]
    mem = a["mem"][0]
    t = h.shape[0]

    def my_shards(pre):
        return ({name: (jnp.swapaxes(a[pre + name], 1, 2) if tr else a[pre + name]) for name, tr, _ in BIG},
                [a[pre + name] for name, _ in TINY])

    big, tiny = my_shards("")
    tiny16 = [(lax.bitcast_convert_type(w, BF16) if name in KEEP_F32 else w.astype(BF16)).reshape(-1)
              for (name, _), w in zip(TINY, tiny)]
    packed = _pack_wide({name: w.astype(BF16) for name, w in big.items()}, jnp.concatenate(tiny16))
    gbig, gtiny = _unpack_wide(all_gather(packed, name="ag_weights"))
    full = {name: _to_full(gbig[name], 1) for name, _, _ in BIG}
    tiny_shapes = [w.shape + ((2,) if name in KEEP_F32 else ()) for (name, _), w in zip(TINY, tiny)]
    for (name, axis), g in zip(TINY, _split_flat(gtiny, tiny_shapes)):
        full[name] = _to_full(lax.bitcast_convert_type(g, F32) if name in KEEP_F32 else g, axis)
    small = {name: a[name] for name in SMALL}
    params, saved = [], []
    for l in range(DEPTH):
        p = _layer_params(full, small, l)
        h, s = _layer_fwd(h, mem, p)
        params.append(p)
        saved.append(s)
    (dh,), (loss_part,) = rowk(_loss_fn, [(h, D_MODEL, 0), (a["loss_target"][0], D_MODEL, 0)], [], [D_MODEL], [(1, 1)],
                               rows=t, name="loss_head")
    loss = lax.psum(loss_part[0, 0], ("x", "y", "c"))
    gfull = {name: [None] * DEPTH for name in SHARDED}
    gsmall = {name: [None] * DEPTH for name in SMALL}
    for l in reversed(range(DEPTH)):
        dh = _layer_bwd(dh, mem, params[l], saved[l], l, gfull, gsmall)
    grad_x = dh[None]
    gbig = {name: jnp.stack([g.reshape(N_DEV, rows, WIDE) for g in gfull[name]], axis=1) for name, _, rows in BIG}
    gtiny = jnp.concatenate([_to_slabs(jnp.stack(gfull[name]), axis).reshape(N_DEV, -1) for name, axis in TINY], axis=1)
    slabs = _pack_wide(gbig, gtiny)
    halves = jnp.swapaxes(slabs.reshape((4, 2) + slabs.shape[1:]), 0, 1)
    theirs = rs_sibling_exchange(halves, name="rs_sibling")
    slabs = rs_chip_exchange(pair_sum_bf16(halves, theirs, name="rs_pair_sum"), name="rs_chips")

    def pk(pre):
        big, tiny = my_shards(pre)
        return _pack_wide(big, jnp.concatenate([w.reshape(-1) for w in tiny]))

    bigs = adamw(slabs, pk(""), pk("m_"), pk("v_"), name="adamw_sharded", tt=128)
    gs = _pack_rows(jnp.concatenate([jnp.stack(gsmall[name]).reshape(-1) for name in SMALL]), 8)
    gs = all_gather(gs, name="ag_small_grads")
    pks = lambda pre: _pack_rows(jnp.concatenate([a[pre + name].reshape(-1) for name in SMALL]), 8)
    sm = adamw(gs, pks(""), pks("m_"), pks("v_"), name="adamw_replicated", tt=gs.shape[1])
    out = {}
    for kind, bg, sg in zip(("grad_", "delta_", "new_m_", "new_v_"), bigs, sm):
        obig, otiny = _unpack_wide(bg)
        for name, tr, _ in BIG:
            out[kind + name] = jnp.swapaxes(obig[name], 1, 2) if tr else obig[name]
        for (name, _), arr in zip(TINY, _split_flat(otiny, [w.shape for w in tiny])):
            out[kind + name] = arr
        for name, arr in zip(SMALL, _unpack(sg, [a[name].shape for name in SMALL])):
            out[kind + name] = arr
    return (loss, grad_x) + tuple(out[kind + name] for kind in ("grad_", "delta_", "new_m_", "new_v_") for name in WEIGHTS)


def kernel(x, mem, w_in, w_out, ssd_conv_w, ssd_conv_b, ssd_dt_bias, ssd_a_log, ssd_d, ssd_norm_w, s5_lam_re, s5_lam_im, s5_log_step, s5_b_re, s5_b_im, s5_c_re, s5_c_im, s5_d, s5_glu_w, s5_glu_b, rg_conv_w, rg_conv_b, rg_wa, rg_ba, rg_wx, rg_bx, rg_lambda, ln1_g, ln1_b, xa_wq, xa_wk, xa_wv, xa_wo, ln2_g, ln2_b, mlp_w1, mlp_w2, ln3_g, ln3_b, loss_target, m_w_in, m_w_out, m_ssd_conv_w, m_ssd_conv_b, m_ssd_dt_bias, m_ssd_a_log, m_ssd_d, m_ssd_norm_w, m_s5_lam_re, m_s5_lam_im, m_s5_log_step, m_s5_b_re, m_s5_b_im, m_s5_c_re, m_s5_c_im, m_s5_d, m_s5_glu_w, m_s5_glu_b, m_rg_conv_w, m_rg_conv_b, m_rg_wa, m_rg_ba, m_rg_wx, m_rg_bx, m_rg_lambda, m_ln1_g, m_ln1_b, m_xa_wq, m_xa_wk, m_xa_wv, m_xa_wo, m_ln2_g, m_ln2_b, m_mlp_w1, m_mlp_w2, m_ln3_g, m_ln3_b, v_w_in, v_w_out, v_ssd_conv_w, v_ssd_conv_b, v_ssd_dt_bias, v_ssd_a_log, v_ssd_d, v_ssd_norm_w, v_s5_lam_re, v_s5_lam_im, v_s5_log_step, v_s5_b_re, v_s5_b_im, v_s5_c_re, v_s5_c_im, v_s5_d, v_s5_glu_w, v_s5_glu_b, v_rg_conv_w, v_rg_conv_b, v_rg_wa, v_rg_ba, v_rg_wx, v_rg_bx, v_rg_lambda, v_ln1_g, v_ln1_b, v_xa_wq, v_xa_wk, v_xa_wv, v_xa_wo, v_ln2_g, v_ln2_b, v_mlp_w1, v_mlp_w2, v_ln3_g, v_ln3_b):
    return _step(dict(locals()))
```

```python
import math

import jax
import jax.numpy as jnp
from jax import lax
from jax.experimental import pallas as pl
from jax.experimental.pallas import tpu as pltpu

F32 = jnp.float32
BF16 = jnp.bfloat16

N_DEV = 8
D_MODEL = 1024
DEPTH = 2
SSD_WIDTH = 512
SSD_HEADS = 8
SSD_HEAD_DIM = 64
SSD_STATE = 128
SSD_CHUNK = 128
SSD_XBC = 1024
S5_WIDTH = 256
S5_GROUPS = 16
S5_GROUP_CH = 16
S5_STATE = 64
S5_NSTATE = S5_GROUPS * S5_STATE
RG_WIDTH = 256
RG_BLOCKS = 4
RG_BLOCK_DIM = 64
RG_C = 8.0
XA_HEADS = 4
XA_HEAD_DIM = 256
ALPHA = (2.0 * DEPTH) ** 0.25
LN_EPS = 1e-5
ADAM_LR, ADAM_B1, ADAM_B2, ADAM_EPS, ADAM_WD, ADAM_STEP = 0.001, 0.9, 0.999, 1e-08, 0.01, 10

P_XBC, P_Z, P_U, P_XR, P_G, P_DT = 0, 1024, 1536, 1792, 2048, 2304
D_INP = 2560
LANE = 128
VMEM_LIMIT = 56 * 1024 * 1024
ROW_TILE = 512

_NN = ((1,), (0,))
_NT = ((1,), (1,))
_TN = ((0,), (0,))


def _dot(a, b, dims=_NN):
    return lax.dot_general(a.astype(BF16), b.astype(BF16), (dims, ((), ())), preferred_element_type=F32)


def _split_bf16(x, parts):
    out, rem = [], x
    for _ in range(parts):
        piece = rem.astype(BF16)
        out.append(piece)
        rem = rem - piece.astype(F32)
    return out


def _dot_mask(a, b, dims=_NN, *, mask_left, parts):
    if mask_left:
        return sum(_dot(a, piece, dims) for piece in _split_bf16(b, parts))
    return sum(_dot(piece, b, dims) for piece in _split_bf16(a, parts))


def _sigmoid(x):
    return 1.0 / (1.0 + jnp.exp(-x))


def _silu(x):
    return x * _sigmoid(x)


def _dsilu(x):
    s = _sigmoid(x)
    return s * (1.0 + x * (1.0 - s))


_GK = math.sqrt(2.0 / math.pi)
_GC = 0.044715


def _gelu(x):
    return 0.5 * x * (1.0 + jnp.tanh(_GK * (x + _GC * x * x * x)))


def _dgelu(x):
    th = jnp.tanh(_GK * (x + _GC * x * x * x))
    return 0.5 * (1.0 + th) + 0.5 * x * (1.0 - th * th) * _GK * (1.0 + 3.0 * _GC * x * x)


def _log1p_pos(e):
    return jnp.where(e < 1e-2, e * (1.0 - e * (0.5 - e * (1.0 / 3.0))), jnp.log(1.0 + e))


def _softplus(x):
    return jnp.maximum(x, 0.0) + _log1p_pos(jnp.exp(-jnp.abs(x)))


def _neg_expm1(x):
    poly = -x * (1.0 + x * (0.5 + x * (1.0 / 6.0 + x * (1.0 / 24.0 + x * (1.0 / 120.0)))))
    return jnp.where(x > -0.05, poly, 1.0 - jnp.exp(x))


def _params(sem):
    return pltpu.CompilerParams(dimension_semantics=sem, vmem_limit_bytes=VMEM_LIMIT)


RESIDENT_BYTES = 8 * 1024 * 1024
STREAM_BYTES = 4 * 1024 * 1024


def _halve_to_fit(dims, bytes_per, limit):
    dims = list(dims)
    while math.prod(dims) * bytes_per > limit:
        i = max(range(len(dims)), key=lambda d: dims[d])
        assert dims[i] % 256 == 0, dims
        dims[i] //= 2
    return dims


def _side_exchange(side, src, dst, sems, step, nsteps):
    kind, _, r0, rows = side
    span = pl.ds(r0, rows)
    if kind == "gather":
        phases = lambda: _ag_phases(src.at[span], dst, *sems)
        when = (0, (5 * nsteps) // 8, nsteps - 1)
    else:
        phases = lambda: _rs_chip_phases(src, dst, *sems, rows=span)
        when = (0, nsteps - 1)
    for idx, at in enumerate(when):
        pl.when(step == at)(lambda idx=idx: phases()[idx]())


def mm(a, b, *, name, ta=False, tb=False, a_extra=(), fa=None, o_extra=(), r_extra=(), fo=None, n_out=1,
       a_off=0, m=None, k=None, out_dtype=F32, side=None):
    n = b.shape[0] if tb else b.shape[1]
    na, no, nr = 1 + len(a_extra), len(o_extra), len(r_extra)
    if not ta:
        assert m is None
        m, kdim = a.shape[0], (a.shape[1] if k is None else k)
        assert a_off % kdim == 0
        (tn,) = _halve_to_fit([n], kdim * b.dtype.itemsize, RESIDENT_BYTES)
        (tm,) = _halve_to_fit([min(512, m)], max(tn, kdim) * 4, STREAM_BYTES)
        a_spec = pl.BlockSpec((tm, kdim), lambda i, j: (i, a_off // kdim))
        b_spec = pl.BlockSpec((tn, kdim), lambda i, j: (j, 0)) if tb else pl.BlockSpec((kdim, tn), lambda i, j: (0, j))
        o_spec = pl.BlockSpec((tm, tn), lambda i, j: (i, j))
        dims = _NT if tb else _NN

        r_spec = pl.BlockSpec((1, tn), lambda i, j: (0, j))

        grid = (m // tm, n // tn)
        nin = na + 1 + no + nr

        def body(*refs):
            a_refs, b_ref, e_refs, out_refs = refs[:na], refs[na], refs[na + 1:nin], refs[nin + (side is not None):nin + (side is not None) + n_out]
            if side is not None:
                _side_exchange(side, refs[nin], refs[nin + 1 + n_out], refs[nin + 2 + n_out:],
                               pl.program_id(0) * grid[1] + pl.program_id(1), grid[0] * grid[1])
            av = a_refs[0][...] if fa is None else fa(*[r[...] for r in a_refs])
            acc = _dot(av, b_ref[...], dims)
            res = acc if fo is None else fo(acc, *[r[...] for r in e_refs])
            for r, v in zip(out_refs, res if n_out > 1 else (res,)):
                r[...] = v.astype(r.dtype)

        sem = ("parallel", "parallel") if side is None else ("arbitrary", "arbitrary")
    else:
        assert k is None and not tb and fo is None and not o_extra and not r_extra and n_out == 1 and out_dtype == F32
        assert side is None
        kdim, m = a.shape[0], (a.shape[1] if m is None else m)
        r_spec = None
        tm, tn = _halve_to_fit([m, n], 4, RESIDENT_BYTES)
        (tk,) = _halve_to_fit([min(512, kdim)], max(tm, tn) * 4, STREAM_BYTES)
        assert a_off % tm == 0
        a_spec = pl.BlockSpec((tk, tm), lambda i, j, kk: (kk, i + a_off // tm))
        b_spec = pl.BlockSpec((tk, tn), lambda i, j, kk: (kk, j))
        o_spec = pl.BlockSpec((tm, tn), lambda i, j, kk: (i, j))

        def body(*refs):
            a_refs, b_ref, out_ref = refs[:na], refs[na], refs[na + 1]

            @pl.when(pl.program_id(2) == 0)
            def _():
                out_ref[...] = jnp.zeros_like(out_ref)

            av = a_refs[0][...] if fa is None else fa(*[r[...] for r in a_refs])
            out_ref[...] += _dot(av, b_ref[...], _TN)

        grid, sem = (m // tm, n // tn, kdim // tk), ("parallel", "parallel", "arbitrary")
    assert m % tm == 0 and n % tn == 0, (name, m, n, tm, tn)
    out = jax.ShapeDtypeStruct((m, n), out_dtype)
    if side is None:
        return pl.pallas_call(
            body, name=name, grid=grid,
            in_specs=[a_spec] * na + [b_spec] + [o_spec] * no + [r_spec] * nr,
            out_specs=o_spec if n_out == 1 else [o_spec] * n_out, out_shape=out if n_out == 1 else [out] * n_out,
            compiler_params=_params(sem),
        )(a, *a_extra, b, *o_extra, *r_extra)
    kind, arr, _, rows = side
    landed = jax.ShapeDtypeStruct(((N_DEV, rows) if kind == "gather" else (4, rows)) + arr.shape[-1:], arr.dtype)
    return pl.pallas_call(
        body, name=name, grid=grid,
        in_specs=[a_spec] * na + [b_spec] + [o_spec] * no + [r_spec] * nr + [_ANY],
        out_specs=[o_spec] * n_out + [_ANY], out_shape=[out] * n_out + [landed],
        scratch_shapes=list(_AG_SEMS if kind == "gather" else _RS_SEMS),
        compiler_params=_params(sem),
    )(a, *a_extra, b, *o_extra, *r_extra, arr)


def rowk(fn, tiled, full, out_w, acc_shapes, *, rows, name, out_dtypes=None):
    tt = min(ROW_TILE, rows)
    n = rows // tt
    assert rows % tt == 0
    nt, nf, no = len(tiled), len(full), len(out_w)

    def tspec(w, cb):
        return pl.BlockSpec((tt, w), lambda i: (i, cb))

    def fspec(a):
        nd = a.ndim
        return pl.BlockSpec(a.shape, lambda i: (0,) * nd)

    def body(*refs):
        ins, fulls = refs[:nt], refs[nt:nt + nf]
        outs, accs = refs[nt + nf:nt + nf + no], refs[nt + nf + no:]
        res_t, res_a = fn(*[r[...] for r in ins], *[r[...] for r in fulls])
        for r, v in zip(outs, res_t):
            r[...] = v.astype(r.dtype)
        if accs:
            @pl.when(pl.program_id(0) == 0)
            def _():
                for r in accs:
                    r[...] = jnp.zeros_like(r)
            for r, v in zip(accs, res_a):
                r[...] += v

    outs = pl.pallas_call(
        body, name=name, grid=(n,),
        in_specs=[tspec(w, cb) for (_, w, cb) in tiled] + [fspec(a) for a in full],
        out_specs=[tspec(w, 0) for w in out_w] + [pl.BlockSpec(s, lambda i, nd=len(s): (0,) * nd) for s in acc_shapes],
        out_shape=[jax.ShapeDtypeStruct((rows, w), dt) for w, dt in zip(out_w, out_dtypes or [F32] * no)]
        + [jax.ShapeDtypeStruct(s, F32) for s in acc_shapes],
        compiler_params=_params(("arbitrary",)),
    )(*[a for (a, _, _) in tiled], *full)
    return outs[:no], outs[no:]


def _colsum(x):
    return jnp.sum(x, axis=0, keepdims=True)


def _rowsum(x):
    return jnp.sum(x, axis=1, keepdims=True)


def _ln_epilogue(acc, resid, g, b):
    pre = ALPHA * resid + acc
    mu = jnp.mean(pre, axis=1, keepdims=True)
    xc = pre - mu
    var = jnp.mean(xc * xc, axis=1, keepdims=True)
    return pre, xc * lax.rsqrt(var + LN_EPS) * g + b


def _ln_bwd_fn(pre, dout, g):
    mu = jnp.mean(pre, axis=1, keepdims=True)
    xc = pre - mu
    var = jnp.mean(xc * xc, axis=1, keepdims=True)
    rstd = lax.rsqrt(var + LN_EPS)
    xhat = xc * rstd
    dxh = dout * g
    dpre = rstd * (dxh - jnp.mean(dxh, axis=1, keepdims=True) - xhat * jnp.mean(dxh * xhat, axis=1, keepdims=True))
    return (dpre,), (_colsum(dout * xhat), _colsum(dout))


def mm_ln(a, w, resid, g, b, *, name, fa=None, side=None):
    assert w.shape[1] == D_MODEL
    return mm(a, w, fa=fa, o_extra=(resid,), r_extra=(g, b), fo=_ln_epilogue, n_out=2, name=name, side=side)


def ln_bwd(pre, dout, g, *, name):
    (dpre,), (dg, db) = rowk(_ln_bwd_fn, [(pre, D_MODEL, 0), (dout, D_MODEL, 0)], [g],
                             [D_MODEL], [(1, D_MODEL), (1, D_MODEL)], rows=pre.shape[0], name=name)
    return dpre, dg, db


def _loss_fn(y, tgt):
    e = y - tgt
    part = _colsum(_rowsum(e * e)) * (0.5 / D_MODEL)
    return (e * (1.0 / D_MODEL),), (part,)


_XA_SCALE = 1.0 / math.sqrt(XA_HEAD_DIM)


def _attn_probs(qh, kh):
    s = _dot(qh, kh, _NT) * _XA_SCALE
    e = jnp.exp(s - jnp.max(s, axis=1, keepdims=True))
    return e / _rowsum(e)


def _attn_fwd_fn(q, k, v):
    outs = []
    for hd in range(XA_HEADS):
        sl = slice(hd * XA_HEAD_DIM, (hd + 1) * XA_HEAD_DIM)
        outs.append(_dot(_attn_probs(q[:, sl], k[:, sl]), v[:, sl]))
    return (jnp.concatenate(outs, axis=1),), ()


def _attn_bwd_fn(q, do, k, v):
    dqs, dks, dvs = [], [], []
    for hd in range(XA_HEADS):
        sl = slice(hd * XA_HEAD_DIM, (hd + 1) * XA_HEAD_DIM)
        qh, kh, vh, doh = q[:, sl], k[:, sl], v[:, sl], do[:, sl]
        p = _attn_probs(qh, kh)
        dp = _dot(doh, vh, _NT)
        ds = p * (dp - _rowsum(p * dp)) * _XA_SCALE
        dqs.append(_dot(ds, kh))
        dks.append(_dot(ds, qh, _TN))
        dvs.append(_dot(p, doh, _TN))
    cat = lambda xs: jnp.concatenate(xs, axis=1)
    return (cat(dqs),), (cat(dks), cat(dvs))


def _s5_post_fwd_fn(ylin, u, dskip, gw, gb):
    yg = _gelu(ylin + dskip * u)
    return (yg * _sigmoid(_dot(yg, gw) + gb),), ()


def _s5_post_bwd_fn(ylin, u, dout, dskip, gw, gb):
    pre = ylin + dskip * u
    yg = _gelu(pre)
    sg = _sigmoid(_dot(yg, gw) + gb)
    dlin = dout * yg * sg * (1.0 - sg)
    dyg = dout * sg + _dot(dlin, gw, _NT)
    dpre = dyg * _dgelu(pre)
    return (dpre, dpre * dskip), (_colsum(dpre * u), _dot(yg, dlin, _TN), _colsum(dlin))


def _rg_gates(xc, wa, wx, ba, bx, lam):
    r = _sigmoid(_dot(xc, wa) + ba)
    i = _sigmoid(_dot(xc, wx) + bx)
    sp = _softplus(-lam)
    log_a = -RG_C * r * sp
    a = jnp.exp(log_a)
    mult = jnp.sqrt(_neg_expm1(2.0 * log_a))
    return r, i, sp, a, mult


def _rg_pre_fwd_fn(xc, wa, wx, ba, bx, lam):
    r, i, sp, a, mult = _rg_gates(xc, wa, wx, ba, bx, lam)
    return (a, mult * (i * xc)), ()


def _rg_pre_bwd_fn(xc, gsc, hprev, wa, wx, ba, bx, lam):
    r, i, sp, a, mult = _rg_gates(xc, wa, wx, ba, bx, lam)
    da = gsc * hprev
    db = gsc
    dmult = db * i * xc
    di = db * mult * xc
    dxc = db * mult * i
    dlog_a = da * a - a * a * dmult / mult
    dr = dlog_a * (-RG_C * sp)
    dsp = _colsum(dlog_a * (-RG_C * r))
    dlam = dsp * (-_sigmoid(-lam))
    dpr = dr * r * (1.0 - r)
    dpi = di * i * (1.0 - i)
    dxc = dxc + _dot(dpr, wa, _NT) + _dot(dpi, wx, _NT)
    return (dxc,), (_dot(xc, dpr, _TN), _dot(xc, dpi, _TN), _colsum(dpr), _colsum(dpi), dlam)


def _rg_out_fwd_fn(h, g):
    return (h * _gelu(g),), ()


def _rg_out_bwd_fn(h, g, dy):
    return (dy * _gelu(g), dy * h * _dgelu(g)), ()


def _shift_down(x, prev, j, rows):
    return jnp.where(rows < j, pltpu.roll(prev, j, 0), pltpu.roll(x, j, 0))


def _shift_up(x, nxt, j, rows):
    t = x.shape[0]
    return jnp.where(rows >= t - j, pltpu.roll(nxt, t - j, 0), pltpu.roll(x, t - j, 0))


def conv_fwd(src, cb, w, b, *, width, act, name):
    t = src.shape[0]
    tt = min(ROW_TILE, t)
    n = t // tt

    def body(x_ref, w_ref, b_ref, y_ref, prev_ref):
        @pl.when(pl.program_id(0) == 0)
        def _():
            prev_ref[...] = jnp.zeros_like(prev_ref)

        x = x_ref[...]
        prev = prev_ref[...]
        rows = lax.broadcasted_iota(jnp.int32, x.shape, 0)
        wv = w_ref[...]
        y = b_ref[...] + wv[3:4, :] * x
        for j in (1, 2, 3):
            y = y + wv[3 - j:4 - j, :] * _shift_down(x, prev, j, rows)
        y_ref[...] = _silu(y) if act else y
        prev_ref[...] = x

    return pl.pallas_call(
        body, name=name, grid=(n,),
        in_specs=[pl.BlockSpec((tt, width), lambda i: (i, cb)),
                  pl.BlockSpec((4, width), lambda i: (0, 0)), pl.BlockSpec((1, width), lambda i: (0, 0))],
        out_specs=pl.BlockSpec((tt, width), lambda i: (i, 0)),
        out_shape=jax.ShapeDtypeStruct((t, width), F32),
        scratch_shapes=[pltpu.VMEM((tt, width), F32)],
        compiler_params=_params(("arbitrary",)),
    )(src, w, b)


def conv_bwd(src, cb, dy, w, b, *, width, act, name):
    t = src.shape[0]
    tt = min(ROW_TILE, t)
    n = t // tt

    def body(x_ref, xp_ref, dy_ref, w_ref, b_ref, dx_ref, dw_ref, db_ref, nxt_ref):
        i = pl.program_id(0)

        @pl.when(i == 0)
        def _():
            nxt_ref[...] = jnp.zeros_like(nxt_ref)
            dw_ref[...] = jnp.zeros_like(dw_ref)
            db_ref[...] = jnp.zeros_like(db_ref)

        x = x_ref[...]
        prev = jnp.where(i == n - 1, 0.0, xp_ref[...])
        rows = lax.broadcasted_iota(jnp.int32, x.shape, 0)
        wv = w_ref[...]
        xs = [x] + [_shift_down(x, prev, j, rows) for j in (1, 2, 3)]
        dpre = dy_ref[...]
        if act:
            pre = b_ref[...] + wv[3:4, :] * xs[0]
            for j in (1, 2, 3):
                pre = pre + wv[3 - j:4 - j, :] * xs[j]
            dpre = dpre * _dsilu(pre)
        nxt = nxt_ref[...]
        dx = wv[3:4, :] * dpre
        for j in (1, 2, 3):
            dx = dx + wv[3 - j:4 - j, :] * _shift_up(dpre, nxt, j, rows)
        dx_ref[...] = dx.astype(dx_ref.dtype)
        dw_ref[...] += jnp.concatenate([_colsum(dpre * xs[3 - kk]) for kk in range(4)], axis=0)
        db_ref[...] += _colsum(dpre)
        nxt_ref[...] = dpre

    return pl.pallas_call(
        body, name=name, grid=(n,),
        in_specs=[pl.BlockSpec((tt, width), lambda i: (n - 1 - i, cb)),
                  pl.BlockSpec((tt, width), lambda i: (jnp.maximum(n - 2 - i, 0), cb)),
                  pl.BlockSpec((tt, width), lambda i: (n - 1 - i, 0)),
                  pl.BlockSpec((4, width), lambda i: (0, 0)), pl.BlockSpec((1, width), lambda i: (0, 0))],
        out_specs=[pl.BlockSpec((tt, width), lambda i: (n - 1 - i, 0)),
                   pl.BlockSpec((4, width), lambda i: (0, 0)), pl.BlockSpec((1, width), lambda i: (0, 0))],
        out_shape=[jax.ShapeDtypeStruct((t, width), BF16), jax.ShapeDtypeStruct((4, width), F32),
                   jax.ShapeDtypeStruct((1, width), F32)],
        scratch_shapes=[pltpu.VMEM((tt, width), F32)],
        compiler_params=_params(("arbitrary",)),
    )(src, src, dy, w, b)


S5_CW = 256


def _cmul(ar, ai, br, bi):
    return ar * br - ai * bi, ar * bi + ai * br


def _scan8_complex(src_ref, dst_ref, lam_ref, st_ref, *, w, nb, reverse):
    rows = lax.broadcasted_iota(jnp.int32, (8, S5_CW), 0)
    b8 = lambda v: jnp.broadcast_to(v, (8, S5_CW))

    def shift(x, k):
        if reverse:
            return jnp.where(rows < 8 - k, pltpu.roll(x, 8 - k, 0), 0.0)
        return jnp.where(rows >= k, pltpu.roll(x, k, 0), 0.0)

    for c0 in range(0, w, S5_CW):
        re, im = pl.ds(c0, S5_CW), pl.ds(w + c0, S5_CW)
        pw = [(lam_ref[:, re], lam_ref[:, im])]
        for _ in range(7):
            pw.append(_cmul(*pw[-1], *pw[0]))
        pr, pi = b8(pw[7][0]), b8(pw[7][1])
        for j in range(7):
            sel = rows == (7 - j if reverse else j)
            pr, pi = jnp.where(sel, b8(pw[j][0]), pr), jnp.where(sel, b8(pw[j][1]), pi)
        steps = [(k, b8(pw[k - 1][0]), b8(pw[k - 1][1])) for k in (1, 2, 4)]
        edge = 0 if reverse else 7

        def blk(i, carry):
            hr, hi = carry
            base = pl.multiple_of((nb // 2 - 1 - i if reverse else i) * 16, 16)
            pend = []
            for off in ((8, 0) if reverse else (0, 8)):
                at = pl.ds(base + off, 8)
                xr, xi = src_ref[at, re], src_ref[at, im]
                for k, kr, ki in steps:
                    sr, si = shift(xr, k), shift(xi, k)
                    xr, xi = xr + kr * sr - ki * si, xi + kr * si + ki * sr
                pend.append((at, xr, xi))
            for at, xr, xi in pend:
                xr, xi = xr + pr * hr - pi * hi, xi + pr * hi + pi * hr
                dst_ref[at, re] = xr
                dst_ref[at, im] = xi
                hr, hi = b8(xr[edge:edge + 1, :]), b8(xi[edge:edge + 1, :])
            return hr, hi

        hr, hi = lax.fori_loop(0, nb // 2, blk, (st_ref[:, re], st_ref[:, im]))
        st_ref[:, re] = hr
        st_ref[:, im] = hi


def s5_fwd(proj, bcat, lam, ccat, *, name):
    t = proj.shape[0]
    tt = min(ROW_TILE, t)
    w2 = bcat.shape[1]

    def body(u_ref, b_ref, lam_ref, c_ref, h_ref, y_ref, bu_ref, st_ref):
        @pl.when(pl.program_id(0) == 0)
        def _():
            st_ref[...] = jnp.zeros_like(st_ref)

        bu_ref[...] = _dot(u_ref[...], b_ref[...])
        _scan8_complex(bu_ref, h_ref, lam_ref, st_ref, w=w2 // 2, nb=tt // 8, reverse=False)
        y_ref[...] = _dot(h_ref[...], c_ref[...])

    fixed = lambda a: pl.BlockSpec(a.shape, lambda i: (0, 0))
    return pl.pallas_call(
        body, name=name, grid=(t // tt,),
        in_specs=[pl.BlockSpec((tt, S5_WIDTH), lambda i: (i, P_U // S5_WIDTH)), fixed(bcat), fixed(lam), fixed(ccat)],
        out_specs=[pl.BlockSpec((tt, w2), lambda i: (i, 0)), pl.BlockSpec((tt, S5_WIDTH), lambda i: (i, 0))],
        out_shape=[jax.ShapeDtypeStruct((t, w2), F32), jax.ShapeDtypeStruct((t, S5_WIDTH), F32)],
        scratch_shapes=[pltpu.VMEM((tt, w2), F32), pltpu.VMEM((8, w2), F32)],
        compiler_params=_params(("arbitrary",)),
    )(proj, bcat, lam, ccat)


def s5_bwd(dylin, du_a, hs, proj, bcat, lam_adj, ccat, *, name):
    t = proj.shape[0]
    tt = min(ROW_TILE, t)
    n, w2 = t // tt, bcat.shape[1]
    w = w2 // 2

    def body(dy_ref, dua_ref, h_ref, hp_ref, u_ref, b_ref, lam_ref, c_ref,
             du_ref, dc_ref, db_ref, dar_ref, dai_ref, g_ref, st_ref):
        i = pl.program_id(0)

        @pl.when(i == 0)
        def _():
            for r in (st_ref, dc_ref, db_ref, dar_ref, dai_ref):
                r[...] = jnp.zeros_like(r)

        dy, h = dy_ref[...], h_ref[...]
        g_ref[...] = _dot(dy, c_ref[...], _NT)
        dc_ref[...] += _dot(h, dy, _TN)
        _scan8_complex(g_ref, g_ref, lam_ref, st_ref, w=w, nb=tt // 8, reverse=True)
        g = g_ref[...]
        du_ref[...] = (dua_ref[...] + _dot(g, b_ref[...], _NT)).astype(du_ref.dtype)
        db_ref[...] += _dot(u_ref[...], g, _TN)
        rows = lax.broadcasted_iota(jnp.int32, (tt, w2), 0)
        before = jnp.where(i == n - 1, 0.0, hp_ref[7:8, :])
        hprev = jnp.where(rows == 0, before, pltpu.roll(h, 1, 0))
        gr, gi, hr, hi = g[:, :w], g[:, w:], hprev[:, :w], hprev[:, w:]
        dar_ref[...] += _colsum(gr * hr + gi * hi)
        dai_ref[...] += _colsum(gi * hr - gr * hi)

    rev = lambda i: n - 1 - i
    row = lambda wd, cb=0: pl.BlockSpec((tt, wd), lambda i: (rev(i), cb))
    fixed = lambda shape: pl.BlockSpec(shape, lambda i: (0, 0))
    return pl.pallas_call(
        body, name=name, grid=(n,),
        in_specs=[row(S5_WIDTH), row(S5_WIDTH), row(w2),
                  pl.BlockSpec((8, w2), lambda i: (jnp.maximum(rev(i) * (tt // 8) - 1, 0), 0)),
                  row(S5_WIDTH, P_U // S5_WIDTH), fixed(bcat.shape), fixed(lam_adj.shape), fixed(ccat.shape)],
        out_specs=[row(S5_WIDTH), fixed(ccat.shape), fixed(bcat.shape), fixed((1, w)), fixed((1, w))],
        out_shape=[jax.ShapeDtypeStruct((t, S5_WIDTH), BF16), jax.ShapeDtypeStruct(ccat.shape, F32),
                   jax.ShapeDtypeStruct(bcat.shape, F32), jax.ShapeDtypeStruct((1, w), F32), jax.ShapeDtypeStruct((1, w), F32)],
        scratch_shapes=[pltpu.VMEM((tt, w2), F32), pltpu.VMEM((8, w2), F32)],
        compiler_params=_params(("arbitrary",)),
    )(dylin, du_a, hs, hs, proj, bcat, lam_adj, ccat)


def scan_real(a, b, *, reverse, name):
    t, w = b.shape
    tt = min(ROW_TILE, t)
    n, nb = t // tt, tt // 8

    def body(a_ref, b_ref, o_ref, st_ref):
        @pl.when(pl.program_id(0) == 0)
        def _():
            st_ref[...] = jnp.zeros_like(st_ref)

        rows = lax.broadcasted_iota(jnp.int32, (8, w), 0)

        def blk(i, h):
            base = pl.multiple_of((nb - 1 - i if reverse else i) * 8, 8)
            ta_, tb_ = a_ref[pl.ds(base, 8), :], b_ref[pl.ds(base, 8), :]
            out = jnp.zeros((8, w), F32)
            for j in (range(7, -1, -1) if reverse else range(8)):
                h = jnp.broadcast_to(ta_[j:j + 1, :], (8, w)) * h + jnp.broadcast_to(tb_[j:j + 1, :], (8, w))
                out = jnp.where(rows == j, h, out)
            o_ref[pl.ds(base, 8), :] = out
            return h

        st_ref[...] = lax.fori_loop(0, nb, blk, st_ref[...])

    idx = (lambda i: (n - 1 - i, 0)) if reverse else (lambda i: (i, 0))
    return pl.pallas_call(
        body, name=name, grid=(n,),
        in_specs=[pl.BlockSpec((tt, w), idx), pl.BlockSpec((tt, w), idx)],
        out_specs=pl.BlockSpec((tt, w), idx), out_shape=jax.ShapeDtypeStruct((t, w), F32),
        scratch_shapes=[pltpu.VMEM((8, w), F32)],
        compiler_params=_params(("arbitrary",)),
    )(a, b)


SSD_QQ = SSD_HEADS * SSD_CHUNK
SSD_GP = SSD_WIDTH // 2
SSD_GQ = SSD_QQ // 2


def _ssd_spread():
    h = jnp.arange(LANE)[:, None]
    spread_p = (jnp.arange(SSD_WIDTH)[None, :] // SSD_HEAD_DIM == h).astype(BF16)
    spread_q = (jnp.arange(SSD_QQ)[None, :] // SSD_CHUNK == h).astype(BF16)
    return spread_p, spread_q


def _ssd_prologue(dt_ref, prow_ref, sp_ref, sq_ref):
    q = SSD_CHUNK
    r = lax.broadcasted_iota(jnp.int32, (q, q), 0)
    c = lax.broadcasted_iota(jnp.int32, (q, q), 1)
    raw_c = dt_ref[...] + prow_ref[0:1, :]
    dt_c = _softplus(raw_c)
    a_r = -jnp.exp(prow_ref[1:2, :])
    cs_c = _dot_mask((r >= c).astype(F32), dt_c * a_r, mask_left=True, parts=3)
    both = _dot_mask(jnp.concatenate([dt_c, cs_c], axis=0), sp_ref[...], mask_left=False, parts=3)
    dt_x, cs_x = both[:q], both[q:]
    csx = _dot_mask(cs_c, sq_ref[...], mask_left=False, parts=3)
    rr = lax.broadcasted_iota(jnp.int32, (q, SSD_QQ), 0)
    ss = lax.broadcasted_iota(jnp.int32, (q, SSD_QQ), 1) & (q - 1)
    diag = rr == ss
    cs_row = _colsum(jnp.where(diag, csx, 0.0))
    lcat = jnp.exp(jnp.where(rr >= ss, csx - cs_row, -1e30))
    cl = cs_x[q - 1:q, :]
    return dict(raw_c=raw_c, dt_c=dt_c, a_r=a_r, dt_x=dt_x, cs_x=cs_x, lcat=lcat, diag=diag,
                ecs=jnp.exp(cs_x), wdec=jnp.exp(cl - cs_x), ecl=jnp.exp(cl), triu=(r <= c).astype(F32))


def _ssd_group(xbc_ref, g, lcat, xdt):
    ns, q = SSD_STATE, SSD_CHUNK
    bm = xbc_ref[:, pl.ds(SSD_WIDTH + g * ns, ns)]
    cm = xbc_ref[:, pl.ds(SSD_WIDTH + 2 * ns + g * ns, ns)]
    cb = _dot(cm, bm, _NT)
    lg = lcat[:, g * SSD_GQ:(g + 1) * SSD_GQ]
    wcat = jnp.concatenate([cb] * 4, axis=1) * lg
    head = lax.broadcasted_iota(jnp.int32, (1, SSD_GP), 1) // SSD_HEAD_DIM
    xg = xdt[:, g * SSD_GP:(g + 1) * SSD_GP]
    xbd = jnp.concatenate([jnp.where(head == j, xg, 0.0) for j in range(4)], axis=0)
    return bm, cm, lg, wcat, xbd, head


def _ssd_gate(yraw, z, nw):
    yg = yraw * _silu(z)
    r = lax.rsqrt(jnp.mean(yg * yg, axis=1, keepdims=True) + LN_EPS)
    return yg, r


def _ssd_specs(q, idx):
    return [pl.BlockSpec((q, SSD_XBC), lambda i: (idx(i), 0)),
            pl.BlockSpec((q, SSD_WIDTH), lambda i: (idx(i), P_Z // SSD_WIDTH)),
            pl.BlockSpec((q, LANE), lambda i: (idx(i), P_DT // LANE)),
            pl.BlockSpec((8, LANE), lambda i: (0, 0)), pl.BlockSpec((1, SSD_WIDTH), lambda i: (0, 0)),
            pl.BlockSpec((1, SSD_WIDTH), lambda i: (0, 0)),
            pl.BlockSpec((LANE, SSD_WIDTH), lambda i: (0, 0)), pl.BlockSpec((LANE, SSD_QQ), lambda i: (0, 0))]


def ssd_fwd(xbc, proj, prow, d_x, nw, *, name):
    t = xbc.shape[0]
    q, ns = SSD_CHUNK, SSD_STATE
    nc = t // q
    spread_p, spread_q = _ssd_spread()

    def body(xbc_ref, z_ref, dt_ref, prow_ref, dx_ref, nw_ref, sp_ref, sq_ref, y_ref, yraw_ref, sall_ref, s_ref):
        @pl.when(pl.program_id(0) == 0)
        def _():
            s_ref[...] = jnp.zeros_like(s_ref)

        sall_ref[0] = s_ref[...]
        pr = _ssd_prologue(dt_ref, prow_ref, sp_ref, sq_ref)
        xs = xbc_ref[:, pl.ds(0, SSD_WIDTH)]
        xdt = xs * pr["dt_x"]
        xw = xdt * pr["wdec"]
        ys = []
        for g in range(2):
            gp = slice(g * SSD_GP, (g + 1) * SSD_GP)
            bm, cm, lg, wcat, xbd, head = _ssd_group(xbc_ref, g, pr["lcat"], xdt)
            st = s_ref[:, gp]
            ys.append(_dot(wcat, xbd) + pr["ecs"][:, gp] * _dot(cm, st) + xs[:, gp] * dx_ref[:, gp])
            s_ref[:, gp] = pr["ecl"][:, gp] * st + _dot(bm, xw[:, gp], _TN)
        yraw = jnp.concatenate(ys, axis=1)
        yraw_ref[...] = yraw
        yg, r = _ssd_gate(yraw, z_ref[...], nw_ref[...])
        y_ref[...] = (yg * r * nw_ref[...]).astype(y_ref.dtype)

    row = pl.BlockSpec((q, SSD_WIDTH), lambda i: (i, 0))
    return pl.pallas_call(
        body, name=name, grid=(nc,),
        in_specs=_ssd_specs(q, lambda i: i),
        out_specs=[row, row, pl.BlockSpec((1, ns, SSD_WIDTH), lambda i: (i, 0, 0))],
        out_shape=[jax.ShapeDtypeStruct((t, SSD_WIDTH), BF16), jax.ShapeDtypeStruct((t, SSD_WIDTH), F32),
                   jax.ShapeDtypeStruct((nc, ns, SSD_WIDTH), F32)],
        scratch_shapes=[pltpu.VMEM((ns, SSD_WIDTH), F32)],
        compiler_params=_params(("arbitrary",)),
    )(xbc, proj, proj, prow, d_x, nw, spread_p, spread_q)


def ssd_bwd(xbc, proj, prow, d_x, nw, yraw, sall, dout, *, name):
    t = xbc.shape[0]
    q, ns = SSD_CHUNK, SSD_STATE
    nc = t // q
    spread_p, spread_q = _ssd_spread()

    def body(xbc_ref, z_ref, dt_ref, prow_ref, dx_ref, nw_ref, sp_ref, sq_ref, yraw_ref, sall_ref, dout_ref,
             dxbc_ref, dz_ref, ddt_ref, dprm_ref, ddx_ref, dnw_ref, ds_ref):
        @pl.when(pl.program_id(0) == 0)
        def _():
            ds_ref[...] = jnp.zeros_like(ds_ref)
            dprm_ref[...] = jnp.zeros_like(dprm_ref)
            ddx_ref[...] = jnp.zeros_like(ddx_ref)
            dnw_ref[...] = jnp.zeros_like(dnw_ref)

        yraw, z, nwv, dout = yraw_ref[...], z_ref[...], nw_ref[...], dout_ref[...]
        yg, r = _ssd_gate(yraw, z, nwv)
        dnw_ref[...] += _colsum(dout * yg * r)
        dyn = dout * nwv
        dyg = r * dyn - yg * (r * r * r) * jnp.mean(dyn * yg, axis=1, keepdims=True)
        dy = dyg * _silu(z)
        dz_ref[...] = (dyg * yraw * _dsilu(z)).astype(dz_ref.dtype)

        pr = _ssd_prologue(dt_ref, prow_ref, sp_ref, sq_ref)
        xs = xbc_ref[:, pl.ds(0, SSD_WIDTH)]
        xdt = xs * pr["dt_x"]
        wdec, ecl = pr["wdec"], pr["ecl"]
        xw = xdt * wdec
        dzm_all = pr["ecs"] * dy
        last = (lax.broadcasted_iota(jnp.int32, (q, 1), 0) == q - 1).astype(F32)
        dxs, dcsxs, es = [], [], []
        for g in range(2):
            gp = slice(g * SSD_GP, (g + 1) * SSD_GP)
            bm, cm, lg, wcat, xbd, head = _ssd_group(xbc_ref, g, pr["lcat"], xdt)
            dyg_ = dy[:, gp]
            dwcat = _dot(dyg_, xbd, _NT)
            dxbd = _dot(wcat, dyg_, _TN)
            dxg = sum(jnp.where(head == j, dxbd[j * q:(j + 1) * q], 0.0) for j in range(4))
            es.append(dwcat * wcat)
            dmm = dwcat * lg
            dm = dmm[:, 0:q] + dmm[:, q:2 * q] + dmm[:, 2 * q:3 * q] + dmm[:, 3 * q:4 * q]
            dcm = _dot(dm, bm)
            dbm = _dot(dm, cm, _TN)
            st = sall_ref[0, :, gp]
            zmat = _dot(cm, st)
            dzm = dzm_all[:, gp]
            dcm = dcm + _dot(dzm, st, _NT)
            dst = _dot(cm, dzm, _TN)
            dcsx = dzm * zmat
            dsn = ds_ref[:, gp]
            dst = dst + ecl[:, gp] * dsn
            dclx = _colsum(dsn * st) * ecl[:, gp]
            dxw = _dot(bm, dsn)
            dbm = dbm + _dot(xw[:, gp], dsn, _NT)
            dxg = dxg + wdec[:, gp] * dxw
            tw = dxw * xdt[:, gp] * wdec[:, gp]
            dclx = dclx + _colsum(tw)
            dcsxs.append(dcsx - tw + last * dclx)
            ds_ref[:, gp] = dst
            dxs.append(dxg)
            dxbc_ref[:, pl.ds(SSD_WIDTH + g * ns, ns)] = dbm
            dxbc_ref[:, pl.ds(SSD_WIDTH + 2 * ns + g * ns, ns)] = dcm
        dx = jnp.concatenate(dxs, axis=1)
        dxbc_ref[:, pl.ds(0, SSD_WIDTH)] = dx * pr["dt_x"] + dy * dx_ref[...]
        ddx_ref[...] += _colsum(dy * xs)
        red = _dot_mask(jnp.concatenate([jnp.concatenate(dcsxs, axis=1), dx * xs], axis=0), sp_ref[...], _NT,
                        mask_left=False, parts=2)
        e_all = jnp.concatenate(es, axis=1)
        e_red = _dot_mask(e_all - jnp.where(pr["diag"], _colsum(e_all), 0.0), sq_ref[...], _NT, mask_left=False, parts=2)
        dadt = _dot_mask(pr["triu"], red[:q] + e_red, mask_left=True, parts=2)
        draw = (red[q:] + dadt * pr["a_r"]) * _sigmoid(pr["raw_c"])
        ddt_ref[...] = draw.astype(ddt_ref.dtype)
        zero = jnp.zeros((6, LANE), F32)
        dprm_ref[...] += jnp.concatenate([_colsum(draw), _colsum(dadt * pr["dt_c"]) * pr["a_r"], zero], axis=0)

    rev = lambda i: nc - 1 - i
    row = lambda w: pl.BlockSpec((q, w), lambda i: (rev(i), 0))
    fixed = lambda shape: pl.BlockSpec(shape, lambda i: (0, 0))
    return pl.pallas_call(
        body, name=name, grid=(nc,),
        in_specs=_ssd_specs(q, rev) + [row(SSD_WIDTH), pl.BlockSpec((1, ns, SSD_WIDTH), lambda i: (rev(i), 0, 0)),
                                       row(SSD_WIDTH)],
        out_specs=[row(SSD_XBC), row(SSD_WIDTH), row(LANE), fixed((8, LANE)), fixed((1, SSD_WIDTH)), fixed((1, SSD_WIDTH))],
        out_shape=[jax.ShapeDtypeStruct((t, SSD_XBC), F32), jax.ShapeDtypeStruct((t, SSD_WIDTH), BF16),
                   jax.ShapeDtypeStruct((t, LANE), BF16), jax.ShapeDtypeStruct((8, LANE), F32),
                   jax.ShapeDtypeStruct((1, SSD_WIDTH), F32), jax.ShapeDtypeStruct((1, SSD_WIDTH), F32)],
        scratch_shapes=[pltpu.VMEM((ns, SSD_WIDTH), F32)],
        compiler_params=_params(("arbitrary",)),
    )(xbc, proj, proj, prow, d_x, nw, spread_p, spread_q, yraw, sall, dout)


def _me():
    return lax.axis_index("x"), lax.axis_index("y"), lax.axis_index("c")


_ANY = pl.BlockSpec(memory_space=pl.ANY)
_MESH = pl.DeviceIdType.MESH


_AG_SEMS = [pltpu.SemaphoreType.DMA((7,)), pltpu.SemaphoreType.DMA((7,)), pltpu.SemaphoreType.DMA(())]
_RS_SEMS = [pltpu.SemaphoreType.DMA((3,)), pltpu.SemaphoreType.DMA((3,)), pltpu.SemaphoreType.DMA(())]


def _ag_phases(src, dst, send_sems, recv_sems, local_sem):
    x, y, c = _me()
    me, sibling = (x, y, c), (x, y, 1 - c)
    chips = [(1 - x, y), (x, 1 - y), (1 - x, 1 - y)]

    def slot(px, py, pc):
        return dst.at[4 * px + 2 * py + pc]

    def copy(kk, blk, to, from_src=False):
        return pltpu.make_async_remote_copy(
            src_ref=src if from_src else slot(*blk), dst_ref=slot(*blk),
            send_sem=send_sems.at[kk], recv_sem=recv_sems.at[kk], device_id=to, device_id_type=_MESH)

    mine = lambda: pltpu.make_async_copy(src, slot(*me), local_sem)
    first = lambda: [copy(0, me, sibling, True)] + [copy(1 + j, me, (*chip, c), True) for j, chip in enumerate(chips)]
    passed = lambda j: copy(4 + j, (*chips[j], c), sibling)

    def start():
        mine().start()
        for cp in first():
            cp.start()

    def forward():
        for j, chip in enumerate(chips):
            copy(1 + j, (*chip, c), me).wait_recv()
            passed(j).start()

    def finish():
        copy(0, sibling, me).wait_recv()
        for j, chip in enumerate(chips):
            copy(4 + j, (*chip, 1 - c), me).wait_recv()
        for cp in first() + [passed(j) for j in range(3)]:
            cp.wait_send()
        mine().wait()

    return start, forward, finish


def _rs_chip_phases(src, dst, send_sems, recv_sems, local_sem, rows=None):
    x, y, c = _me()
    q_me = 2 * x + y
    pick = (lambda q: src.at[q]) if rows is None else (lambda q: src.at[q, rows])
    local = lambda: pltpu.make_async_copy(pick(q_me), dst.at[q_me], local_sem)
    copies = lambda: [pltpu.make_async_remote_copy(src_ref=pick(2 * px + py), dst_ref=dst.at[q_me], send_sem=send_sems.at[j],
                                                   recv_sem=recv_sems.at[j], device_id=(px, py, c), device_id_type=_MESH)
                      for j, (px, py) in enumerate([(1 - x, y), (x, 1 - y), (1 - x, 1 - y)])]

    def start():
        local().start()
        for cp in copies():
            cp.start()

    def finish():
        for cp in copies():
            cp.wait()
        local().wait()

    return start, finish


def all_gather(block, *, name):
    def body(src, dst, send_sems, recv_sems, local_sem):
        for phase in _ag_phases(src, dst, send_sems, recv_sems, local_sem):
            phase()

    return pl.pallas_call(
        body, name=name, in_specs=[_ANY], out_specs=_ANY,
        out_shape=jax.ShapeDtypeStruct((N_DEV,) + block.shape, block.dtype), scratch_shapes=list(_AG_SEMS),
    )(block)


RS_PIECES = 4


def rs_sibling_exchange(halves, *, name):
    _, nq, r, l = halves.shape
    rows = r // RS_PIECES
    assert r % RS_PIECES == 0 and rows % 16 == 0

    def body(src, dst, send_sems, recv_sems):
        x, y, c = _me()
        copies = []
        for q in range(nq):
            for i in range(RS_PIECES):
                kk = q * RS_PIECES + i
                cp = pltpu.make_async_remote_copy(
                    src_ref=src.at[1 - c, q, pl.ds(i * rows, rows)], dst_ref=dst.at[q, pl.ds(i * rows, rows)],
                    send_sem=send_sems.at[kk], recv_sem=recv_sems.at[kk], device_id=(x, y, 1 - c), device_id_type=_MESH)
                cp.start()
                copies.append(cp)
        for cp in copies:
            cp.wait()

    n_copies = nq * RS_PIECES
    return pl.pallas_call(
        body, name=name, in_specs=[_ANY], out_specs=_ANY,
        out_shape=jax.ShapeDtypeStruct((nq, r, l), halves.dtype),
        scratch_shapes=[pltpu.SemaphoreType.DMA((n_copies,)), pltpu.SemaphoreType.DMA((n_copies,))],
    )(halves)


def pair_sum_bf16(halves, theirs, *, name, tt=128):
    _, nq, r, wd = halves.shape
    tt = min(tt, r)
    parity = lax.axis_index("c").astype(jnp.int32).reshape(1)

    def body(c_ref, own_ref, sib_ref, o_ref):
        o_ref[...] = (own_ref[...] + sib_ref[...]).astype(BF16)

    return pl.pallas_call(
        body, name=name,
        grid_spec=pltpu.PrefetchScalarGridSpec(
            num_scalar_prefetch=1, grid=(nq, r // tt),
            in_specs=[pl.BlockSpec((None, None, tt, wd), lambda q, i, c: (c[0], q, i, 0)),
                      pl.BlockSpec((None, tt, wd), lambda q, i, c: (q, i, 0))],
            out_specs=pl.BlockSpec((None, tt, wd), lambda q, i, c: (q, i, 0))),
        out_shape=jax.ShapeDtypeStruct((nq, r, wd), BF16),
        compiler_params=_params(("parallel", "parallel")),
    )(parity, halves, theirs)


def rs_chip_exchange(part, *, name):
    def body(src, dst, send_sems, recv_sems, local_sem):
        for phase in _rs_chip_phases(src, dst, send_sems, recv_sems, local_sem):
            phase()

    return pl.pallas_call(
        body, name=name, in_specs=[_ANY], out_specs=_ANY,
        out_shape=jax.ShapeDtypeStruct(part.shape, part.dtype), scratch_shapes=list(_RS_SEMS),
    )(part)


def adamw(slabs, w, m, v, *, name, tt):
    ns, (r, wd) = slabs.shape[0], w.shape
    tt = min(tt, r)
    assert r % tt == 0

    def body(s_ref, w_ref, m_ref, v_ref, g_ref, d_ref, nm_ref, nv_ref):
        g = s_ref[0].astype(F32)
        for kdev in range(1, ns):
            g = g + s_ref[kdev].astype(F32)
        wv = w_ref[...]
        nm = ADAM_B1 * m_ref[...] + (1.0 - ADAM_B1) * g
        nv = ADAM_B2 * v_ref[...] + (1.0 - ADAM_B2) * (g * g)
        m_hat = nm / (1.0 - ADAM_B1 ** ADAM_STEP)
        v_hat = nv / (1.0 - ADAM_B2 ** ADAM_STEP)
        g_ref[...] = g
        d_ref[...] = -ADAM_LR * (m_hat / (jnp.sqrt(v_hat) + ADAM_EPS) + ADAM_WD * wv)
        nm_ref[...] = nm
        nv_ref[...] = nv

    spec = pl.BlockSpec((tt, wd), lambda i: (i, 0))
    return pl.pallas_call(
        body, name=name, grid=(r // tt,),
        in_specs=[pl.BlockSpec((ns, tt, wd), lambda i: (0, i, 0)), spec, spec, spec],
        out_specs=[spec] * 4, out_shape=[jax.ShapeDtypeStruct((r, wd), F32)] * 4,
        compiler_params=_params(("parallel",)),
    )(slabs, w, m, v)


WIDE = 1024
BIG = [("w_in", True, 289), ("w_out", False, 128), ("xa_wq", False, 128), ("xa_wk", False, 128), ("xa_wv", False, 128),
       ("xa_wo", False, 128), ("mlp_w2", False, 512), ("mlp_w1", True, 512)]
TINY = [("ssd_conv_w", 2), ("s5_glu_w", 1), ("rg_conv_w", 2)]
KEEP_F32 = ("ssd_conv_w", "rg_conv_w")
TINY_ROWS = 32
SHARDED = [name for name, _, _ in BIG] + [name for name, _ in TINY]
SMALL = ["ssd_conv_b", "ssd_dt_bias", "ssd_a_log", "ssd_d", "ssd_norm_w", "s5_lam_re", "s5_lam_im",
         "s5_log_step", "s5_b_re", "s5_b_im", "s5_c_re", "s5_c_im", "s5_d", "s5_glu_b", "rg_conv_b",
         "rg_wa", "rg_ba", "rg_wx", "rg_bx", "rg_lambda", "ln1_g", "ln1_b", "ln2_g", "ln2_b", "ln3_g", "ln3_b"]
WEIGHTS = ['w_in', 'w_out', 'ssd_conv_w', 'ssd_conv_b', 'ssd_dt_bias', 'ssd_a_log', 'ssd_d', 'ssd_norm_w',
           's5_lam_re', 's5_lam_im', 's5_log_step', 's5_b_re', 's5_b_im', 's5_c_re', 's5_c_im', 's5_d',
           's5_glu_w', 's5_glu_b', 'rg_conv_w', 'rg_conv_b', 'rg_wa', 'rg_ba', 'rg_wx', 'rg_bx', 'rg_lambda',
           'ln1_g', 'ln1_b', 'xa_wq', 'xa_wk', 'xa_wv', 'xa_wo', 'ln2_g', 'ln2_b', 'mlp_w1', 'mlp_w2',
           'ln3_g', 'ln3_b']


def _pad16(rows):
    return -(-rows // 16) * 16


def _pack_rows(flat, mult):
    n = flat.shape[-1]
    r = -(-n // (LANE * mult)) * mult
    pad = [(0, 0)] * (flat.ndim - 1) + [(0, r * LANE - n)]
    return jnp.pad(flat, pad).reshape(flat.shape[:-1] + (r, LANE))


def _unpack(packed, shapes):
    lead = packed.shape[:-2]
    flat = packed.reshape(lead + (-1,))
    out, off = [], 0
    for s in shapes:
        n = math.prod(s)
        out.append(flat[..., off:off + n].reshape(lead + tuple(s)))
        off += n
    return out


PACK_ROWS = 2048


def _pad_rows(x, rows):
    return jnp.pad(x, [(0, 0)] * (x.ndim - 2) + [(0, rows - x.shape[-2]), (0, 0)])


def _tiny_block(flat):
    pad = [(0, 0)] * (flat.ndim - 1) + [(0, TINY_ROWS * WIDE - flat.shape[-1])]
    return jnp.pad(flat, pad).reshape(flat.shape[:-1] + (TINY_ROWS, WIDE))


def _pack_layer(big, tiny_flat=None):
    blocks = [_pad_rows(big[name], _pad16(rows)) for name, _, rows in BIG]
    if tiny_flat is not None:
        blocks.append(_tiny_block(tiny_flat))
    return _pad_rows(jnp.concatenate(blocks, axis=-2), PACK_ROWS)


def _unpack_layer(packed):
    big, off = {}, 0
    for name, _, rows in BIG:
        big[name] = packed[..., off:off + rows, :]
        off += _pad16(rows)
    return big, packed[..., off:off + TINY_ROWS, :].reshape(packed.shape[:-2] + (TINY_ROWS * WIDE,))


def _split_flat(flat, shapes):
    out, off = [], 0
    for s in shapes:
        n = math.prod(s)
        out.append(flat[..., off:off + n].reshape(flat.shape[:-1] + tuple(s)))
        off += n
    return out


def _to_full(gathered, axis):
    g = jnp.moveaxis(gathered, 0, axis)
    s = g.shape
    return g.reshape(s[:axis] + (s[axis] * s[axis + 1],) + s[axis + 2:])


def _to_slabs(full, axis):
    s = full.shape
    g = full.reshape(s[:axis] + (N_DEV, s[axis] // N_DEV) + s[axis + 1:])
    return jnp.moveaxis(g, axis, 0)


def _blockdiag(w):
    h, i, j = w.shape
    eye = jnp.eye(h, dtype=w.dtype)
    return (w[:, :, None, :] * eye[:, None, :, None]).reshape(h * i, h * j)


def _blockdiag_extract(m, h):
    i, j = m.shape[0] // h, m.shape[1] // h
    eye = jnp.eye(h, dtype=m.dtype)
    return (m.reshape(h, i, h, j) * eye[:, None, :, None]).sum(axis=2)


def _s5_disc(lr, li, ls, bre, bim):
    step = jnp.exp(ls)[:, None]
    er = jnp.exp(lr * step)
    ar, ai = er * jnp.cos(li * step), er * jnp.sin(li * step)
    nr, ni, den = ar - 1.0, ai, lr * lr + li * li
    qr, qi = (nr * lr + ni * li) / den, (ni * lr - nr * li) / den
    bbr = qr[..., None] * bre - qi[..., None] * bim
    bbi = qr[..., None] * bim + qi[..., None] * bre
    return ar, ai, bbr, bbi


def _row(v, width=None):
    v = v.reshape(1, -1)
    if width is not None and v.shape[1] < width:
        v = jnp.pad(v, ((0, 0), (0, width - v.shape[1])))
    return v


def _relu2(a):
    r = jnp.maximum(a, 0.0)
    return r * r


def _add_alpha(acc, d):
    return acc + ALPHA * d


def _shift_rows_down(x):
    return jnp.concatenate([jnp.zeros((1, x.shape[1]), x.dtype), x[:-1]], axis=0)


def _shift_rows_up(x):
    return jnp.concatenate([x[1:], jnp.zeros((1, x.shape[1]), x.dtype)], axis=0)


def _layer_params(full, small, l):
    p = {}
    w_in = full["w_in"]
    z, xbc, dt, u, xr, g = w_in[0:512], w_in[512:1536], w_in[1536:1544], w_in[1544:1800], w_in[1800:2056], w_in[2056:2312]
    p["w_inp"] = jnp.concatenate([xbc, z, u, xr, g, dt, jnp.zeros((D_INP - P_DT - 8, D_MODEL), w_in.dtype)], axis=0)
    for k_ in ("w_out", "xa_wq", "xa_wk", "xa_wv", "xa_wo", "mlp_w1", "mlp_w2"):
        p[k_] = full[k_]
    p["s5_glu_w"] = full["s5_glu_w"][l]
    p["ssd_cw"], p["ssd_cb"] = full["ssd_conv_w"][l], _row(small["ssd_conv_b"][l])
    dtb, alog, dsk = small["ssd_dt_bias"][l], small["ssd_a_log"][l], small["ssd_d"][l]
    p["prow"] = jnp.concatenate([_row(dtb, LANE), _row(alog, LANE), jnp.zeros((6, LANE), F32)], axis=0)
    p["ssd_dx"] = _row(jnp.repeat(dsk, SSD_HEAD_DIM))
    p["ssd_nw"] = _row(small["ssd_norm_w"][l])
    s5_in = (small["s5_lam_re"][l], small["s5_lam_im"][l], small["s5_log_step"][l], small["s5_b_re"][l], small["s5_b_im"][l])
    (ar, ai, bbr, bbi), p["s5_vjp"] = jax.vjp(_s5_disc, *s5_in)
    p["lam_fwd"] = jnp.concatenate([_row(ar), _row(ai)], axis=1)
    p["lam_adj"] = jnp.concatenate([_row(ar), _row(-ai)], axis=1)
    p["bcat"] = jnp.concatenate([_blockdiag(jnp.swapaxes(bbr, 1, 2)), _blockdiag(jnp.swapaxes(bbi, 1, 2))], axis=1)
    p["ccat"] = jnp.concatenate([_blockdiag(jnp.swapaxes(small["s5_c_re"][l], 1, 2)),
                                 -_blockdiag(jnp.swapaxes(small["s5_c_im"][l], 1, 2))], axis=0)
    p["s5_d"], p["s5_glu_b"] = _row(small["s5_d"][l]), _row(small["s5_glu_b"][l])
    p["rg_cw"], p["rg_cb"] = full["rg_conv_w"][l], _row(small["rg_conv_b"][l])
    p["rg_wa"], p["rg_wx"] = _blockdiag(small["rg_wa"][l]), _blockdiag(small["rg_wx"][l])
    p["rg_ba"], p["rg_bx"], p["rg_lam"] = _row(small["rg_ba"][l]), _row(small["rg_bx"][l]), _row(small["rg_lambda"][l])
    for i in (1, 2, 3):
        p[f"g{i}"], p[f"b{i}"] = _row(small[f"ln{i}_g"][l]), _row(small[f"ln{i}_b"][l])
    return p


def _layer_fwd(h0, mem, p, sides=(None, None)):
    t = h0.shape[0]
    s = {"h0": h0}
    proj = mm(h0, p["w_inp"], tb=True, name="in_proj")
    xbc = conv_fwd(proj, 0, p["ssd_cw"], p["ssd_cb"], width=SSD_XBC, act=True, name="ssd_conv_fwd")
    y_ssd, yraw, sall = ssd_fwd(xbc, proj, p["prow"], p["ssd_dx"], p["ssd_nw"], name="ssd_fwd")
    hs5, ylin = s5_fwd(proj, p["bcat"], p["lam_fwd"], p["ccat"], name="s5_fwd")
    (y_s5,), _ = rowk(_s5_post_fwd_fn, [(ylin, S5_WIDTH, 0), (proj, S5_WIDTH, P_U // S5_WIDTH)],
                      [p["s5_d"], p["s5_glu_w"], p["s5_glu_b"]], [S5_WIDTH], [], rows=t, name="s5_post_fwd", out_dtypes=[BF16])
    xc = conv_fwd(proj, P_XR // RG_WIDTH, p["rg_cw"], p["rg_cb"], width=RG_WIDTH, act=False, name="rg_conv_fwd")
    rg_full = [p["rg_wa"], p["rg_wx"], p["rg_ba"], p["rg_bx"], p["rg_lam"]]
    (a_rg, b_rg), _ = rowk(_rg_pre_fwd_fn, [(xc, RG_WIDTH, 0)], rg_full, [RG_WIDTH, RG_WIDTH], [], rows=t, name="rg_pre_fwd")
    h_rg = scan_real(a_rg, b_rg, reverse=False, name="rg_scan_fwd")
    (y_rg,), _ = rowk(_rg_out_fwd_fn, [(h_rg, RG_WIDTH, 0), (proj, RG_WIDTH, P_G // RG_WIDTH)], [], [RG_WIDTH], [],
                      rows=t, name="rg_out_fwd", out_dtypes=[BF16])
    ycat = jnp.concatenate([y_ssd, y_s5, y_rg], axis=1)
    pre1, h1 = mm_ln(ycat, p["w_out"], h0, p["g1"], p["b1"], name="out_proj")
    q = mm(h1, p["xa_wq"], name="xa_q", out_dtype=BF16)
    k = mm(mem, p["xa_wk"], name="xa_kv")
    v = mm(mem, p["xa_wv"], name="xa_kv")
    (o,), _ = rowk(_attn_fwd_fn, [(q, D_MODEL, 0)], [k, v], [D_MODEL], [], rows=t, name="xa_fwd", out_dtypes=[BF16])
    pre2, h2 = mm_ln(o, p["xa_wo"], h1, p["g2"], p["b2"], name="xa_o")
    a_mlp = mm(h2, p["mlp_w1"], tb=True, name="mlp_up", side=sides[0])
    got = []
    if sides[0] is not None:
        a_mlp, g_ = a_mlp
        got.append(g_)
    res = mm_ln(a_mlp, p["mlp_w2"], h2, p["g3"], p["b3"], fa=_relu2, name="mlp_down", side=sides[1])
    pre3, h3 = res[:2]
    got.extend(res[2:])
    s.update(proj=proj, xbc=xbc, yraw=yraw, sall=sall, hs5=hs5, ylin=ylin, xc=xc, a_rg=a_rg, h_rg=h_rg,
             ycat=ycat, pre1=pre1, h1=h1, q=q, k=k, v=v, o=o, pre2=pre2, h2=h2, a_mlp=a_mlp, pre3=pre3)
    return h3, s, got


def _layer_bwd(dh3, mem, p, s, l, gfull, gsmall, sides=(None, None)):
    t = dh3.shape[0]
    proj = s["proj"]
    dpre3, dg3, db3 = ln_bwd(s["pre3"], dh3, p["g3"], name="ln_bwd")
    da = mm(dpre3, p["mlp_w2"], tb=True, o_extra=(s["a_mlp"],), fo=lambda acc, a: acc * 2.0 * jnp.maximum(a, 0.0), name="mlp_da",
            out_dtype=BF16, side=sides[0])
    got = []
    if sides[0] is not None:
        da, g_ = da
        got.append(g_)
    gfull["mlp_w2"][l] = mm(s["a_mlp"], dpre3, ta=True, fa=_relu2, name="mlp_dw2")
    gfull["mlp_w1"][l] = mm(da, s["h2"], ta=True, name="mlp_dw1")
    dh2 = mm(da, p["mlp_w1"], o_extra=(dpre3,), fo=_add_alpha, name="mlp_dx", side=sides[1])
    if sides[1] is not None:
        dh2, g_ = dh2
        got.append(g_)
    dpre2, dg2, db2 = ln_bwd(s["pre2"], dh2, p["g2"], name="ln_bwd")
    do = mm(dpre2, p["xa_wo"], tb=True, name="xa_do", out_dtype=BF16)
    gfull["xa_wo"][l] = mm(s["o"], dpre2, ta=True, name="dw_sq")
    (dq,), (dk, dv) = rowk(_attn_bwd_fn, [(s["q"], D_MODEL, 0), (do, D_MODEL, 0)], [s["k"], s["v"]], [D_MODEL],
                           [(256, D_MODEL), (256, D_MODEL)], rows=t, name="xa_bwd", out_dtypes=[BF16])
    gfull["xa_wq"][l] = mm(s["h1"], dq, ta=True, name="dw_sq")
    gfull["xa_wk"][l] = mm(mem, dk, ta=True, name="dw_kv")
    gfull["xa_wv"][l] = mm(mem, dv, ta=True, name="dw_kv")
    dh1 = mm(dq, p["xa_wq"], tb=True, o_extra=(dpre2,), fo=_add_alpha, name="dx_sq")
    dpre1, dg1, db1 = ln_bwd(s["pre1"], dh1, p["g1"], name="ln_bwd")
    dycat = mm(dpre1, p["w_out"], tb=True, name="xa_do")
    gfull["w_out"][l] = mm(s["ycat"], dpre1, ta=True, name="dw_sq")
    (dh_rg, dg_rg), _ = rowk(_rg_out_bwd_fn, [(s["h_rg"], RG_WIDTH, 0), (proj, RG_WIDTH, P_G // RG_WIDTH), (dycat, RG_WIDTH, 3)],
                             [], [RG_WIDTH, RG_WIDTH], [], rows=t, name="rg_out_bwd", out_dtypes=[F32, BF16])
    g_rg = scan_real(_shift_rows_up(s["a_rg"]), dh_rg, reverse=True, name="rg_scan_bwd")
    rg_full = [p["rg_wa"], p["rg_wx"], p["rg_ba"], p["rg_bx"], p["rg_lam"]]
    (dxc,), (dwa, dwx, dba, dbx, dlam) = rowk(
        _rg_pre_bwd_fn, [(s["xc"], RG_WIDTH, 0), (g_rg, RG_WIDTH, 0), (_shift_rows_down(s["h_rg"]), RG_WIDTH, 0)], rg_full,
        [RG_WIDTH], [(RG_WIDTH, RG_WIDTH), (RG_WIDTH, RG_WIDTH), (1, RG_WIDTH), (1, RG_WIDTH), (1, RG_WIDTH)],
        rows=t, name="rg_pre_bwd")
    dxr, d_rgcw, d_rgcb = conv_bwd(proj, P_XR // RG_WIDTH, dxc, p["rg_cw"], p["rg_cb"], width=RG_WIDTH, act=False, name="rg_conv_bwd")
    (dylin, du_a), (d_s5d, d_gluw, d_glub) = rowk(
        _s5_post_bwd_fn, [(s["ylin"], S5_WIDTH, 0), (proj, S5_WIDTH, P_U // S5_WIDTH), (dycat, S5_WIDTH, 2)],
        [p["s5_d"], p["s5_glu_w"], p["s5_glu_b"]], [S5_WIDTH, S5_WIDTH],
        [(1, S5_WIDTH), (S5_WIDTH, S5_WIDTH), (1, S5_WIDTH)], rows=t, name="s5_post_bwd")
    du, dccat, dbcat, dar, dai = s5_bwd(dylin, du_a, s["hs5"], proj, p["bcat"], p["lam_adj"], p["ccat"], name="s5_bwd")
    dxbc_act, dz, ddt, dprm, ddx, dnw = ssd_bwd(s["xbc"], proj, p["prow"], p["ssd_dx"], p["ssd_nw"], s["yraw"], s["sall"], dycat,
                                               name="ssd_bwd")
    dxbc, d_scw, d_scb = conv_bwd(proj, 0, dxbc_act, p["ssd_cw"], p["ssd_cb"], width=SSD_XBC, act=True, name="ssd_conv_bwd")
    dproj = jnp.concatenate([dxbc, dz, du, dxr, dg_rg, ddt, jnp.zeros((t, D_INP - P_DT - LANE), BF16)], axis=1)
    dh0 = mm(dproj, p["w_inp"], o_extra=(dpre1,), fo=_add_alpha, name="in_proj_dx")
    dwp = mm(dproj, s["h0"], ta=True, name="in_proj_dw")
    gfull["w_in"][l] = jnp.concatenate([dwp[P_Z:P_Z + 512], dwp[P_XBC:P_XBC + 1024], dwp[P_DT:P_DT + 8],
                                        dwp[P_U:P_U + 256], dwp[P_XR:P_XR + 256], dwp[P_G:P_G + 256]], axis=0)
    gfull["ssd_conv_w"][l], gfull["rg_conv_w"][l], gfull["s5_glu_w"][l] = d_scw, d_rgcw, d_gluw
    ng, ns = S5_GROUPS, S5_STATE
    dbbr = jnp.swapaxes(_blockdiag_extract(dbcat[:, :S5_NSTATE], ng), 1, 2)
    dbbi = jnp.swapaxes(_blockdiag_extract(dbcat[:, S5_NSTATE:], ng), 1, 2)
    d_lr, d_li, d_ls, d_bre, d_bim = p["s5_vjp"]((dar.reshape(ng, ns), dai.reshape(ng, ns), dbbr, dbbi))
    gsmall["s5_lam_re"][l], gsmall["s5_lam_im"][l], gsmall["s5_log_step"][l] = d_lr, d_li, d_ls
    gsmall["s5_b_re"][l], gsmall["s5_b_im"][l] = d_bre, d_bim
    gsmall["s5_c_re"][l] = jnp.swapaxes(_blockdiag_extract(dccat[:S5_NSTATE], ng), 1, 2)
    gsmall["s5_c_im"][l] = -jnp.swapaxes(_blockdiag_extract(dccat[S5_NSTATE:], ng), 1, 2)
    gsmall["s5_d"][l], gsmall["s5_glu_b"][l] = d_s5d[0], d_glub[0]
    gsmall["ssd_conv_b"][l], gsmall["rg_conv_b"][l] = d_scb[0], d_rgcb[0]
    gsmall["ssd_dt_bias"][l], gsmall["ssd_a_log"][l] = dprm[0, :8], dprm[1, :8]
    gsmall["ssd_d"][l] = ddx.reshape(SSD_HEADS, SSD_HEAD_DIM).sum(axis=1)
    gsmall["ssd_norm_w"][l] = dnw[0]
    gsmall["rg_wa"][l], gsmall["rg_wx"][l] = _blockdiag_extract(dwa, RG_BLOCKS), _blockdiag_extract(dwx, RG_BLOCKS)
    gsmall["rg_ba"][l], gsmall["rg_bx"][l] = dba.reshape(RG_BLOCKS, RG_BLOCK_DIM), dbx.reshape(RG_BLOCKS, RG_BLOCK_DIM)
    gsmall["rg_lambda"][l] = dlam[0]
    for i, (dg, db) in zip((1, 2, 3), ((dg1, db1), (dg2, db2), (dg3, db3))):
        gsmall[f"ln{i}_g"][l], gsmall[f"ln{i}_b"][l] = dg[0], db[0]
    return dh0, got


def _step(a):
    h = a["x"][0]
    mem = a["mem"][0]
    t = h.shape[0]
    half = PACK_ROWS // 2

    def my_shards(pre):
        return ({name: (jnp.swapaxes(a[pre + name], 1, 2) if tr else a[pre + name]) for name, tr, _ in BIG},
                [a[pre + name] for name, _ in TINY])

    def my_pack(pre, l):
        big, tiny = my_shards(pre)
        return _pack_layer({name: w[l] for name, w in big.items()},
                           jnp.concatenate([w.reshape(-1) for w in tiny]) if l == 0 else None)

    big, tiny = my_shards("")
    tiny16 = [(lax.bitcast_convert_type(w, BF16) if name in KEEP_F32 else w.astype(BF16)).reshape(-1)
              for (name, _), w in zip(TINY, tiny)]
    packed = [_pack_layer({name: w[l].astype(BF16) for name, w in big.items()}, jnp.concatenate(tiny16) if l == 0 else None)
              for l in range(DEPTH)]
    small = {name: a[name] for name in SMALL}

    def gathered_weights(g):
        gbig, gtiny = _unpack_layer(g)
        return {name: w.reshape(-1, WIDE) for name, w in gbig.items()}, gtiny

    full, gtiny = gathered_weights(all_gather(packed[0], name="ag_weights"))
    tiny_shapes = [w.shape + ((2,) if name in KEEP_F32 else ()) for (name, _), w in zip(TINY, tiny)]
    tiny_full = {name: _to_full(lax.bitcast_convert_type(g, F32) if name in KEEP_F32 else g, axis)
                 for (name, axis), g in zip(TINY, _split_flat(gtiny, tiny_shapes))}
    p0 = _layer_params({**full, **tiny_full}, small, 0)
    h, s0, got = _layer_fwd(h, mem, p0, sides=(("gather", packed[1], 0, half), ("gather", packed[1], half, half)))
    full, _ = gathered_weights(jnp.concatenate(got, axis=1))
    p1 = _layer_params({**full, **tiny_full}, small, 1)
    h, s1, _ = _layer_fwd(h, mem, p1)
    (dh,), (loss_part,) = rowk(_loss_fn, [(h, D_MODEL, 0), (a["loss_target"][0], D_MODEL, 0)], [], [D_MODEL], [(1, 1)],
                               rows=t, name="loss_head")
    loss = lax.psum(loss_part[0, 0], ("x", "y", "c"))
    gfull = {name: [None] * DEPTH for name in SHARDED}
    gsmall = {name: [None] * DEPTH for name in SMALL}

    def chip_partials(l):
        gbig = {name: gfull[name][l].reshape(N_DEV, rows, WIDE) for name, _, rows in BIG}
        gtiny = None
        if l == 0:
            gtiny = jnp.concatenate([_to_slabs(jnp.stack(gfull[name]), axis).reshape(N_DEV, -1) for name, axis in TINY], axis=1)
        slabs = _pack_layer(gbig, gtiny)
        halves = jnp.swapaxes(slabs.reshape((4, 2) + slabs.shape[1:]), 0, 1)
        theirs = rs_sibling_exchange(halves, name="rs_sibling")
        return pair_sum_bf16(halves, theirs, name="rs_pair_sum")

    dh, _ = _layer_bwd(dh, mem, p1, s1, 1, gfull, gsmall)
    part1 = chip_partials(1)
    dh, got = _layer_bwd(dh, mem, p0, s0, 0, gfull, gsmall, sides=(("chips", part1, 0, half), ("chips", part1, half, half)))
    grad_x = dh[None]
    landed = [rs_chip_exchange(chip_partials(0), name="rs_chips"), jnp.concatenate(got, axis=1)]
    bigs = [adamw(landed[l], my_pack("", l), my_pack("m_", l), my_pack("v_", l), name="adamw_sharded", tt=128) for l in range(DEPTH)]
    gs = _pack_rows(jnp.concatenate([jnp.stack(gsmall[name]).reshape(-1) for name in SMALL]), 8)
    gs = all_gather(gs, name="ag_small_grads")
    pks = lambda pre: _pack_rows(jnp.concatenate([a[pre + name].reshape(-1) for name in SMALL]), 8)
    sm = adamw(gs, pks(""), pks("m_"), pks("v_"), name="adamw_replicated", tt=gs.shape[1])
    out = {}
    for i, kind in enumerate(("grad_", "delta_", "new_m_", "new_v_")):
        layers = [_unpack_layer(bigs[l][i]) for l in range(DEPTH)]
        for name, tr, _ in BIG:
            arr = jnp.stack([layers[l][0][name] for l in range(DEPTH)])
            out[kind + name] = jnp.swapaxes(arr, 1, 2) if tr else arr
        for (name, _), arr in zip(TINY, _split_flat(layers[0][1], [w.shape for w in tiny])):
            out[kind + name] = arr
        for name, arr in zip(SMALL, _unpack(sm[i], [a[name].shape for name in SMALL])):
            out[kind + name] = arr
    return (loss, grad_x) + tuple(out[kind + name] for kind in ("grad_", "delta_", "new_m_", "new_v_") for name in WEIGHTS)


def kernel(x, mem, w_in, w_out, ssd_conv_w, ssd_conv_b, ssd_dt_bias, ssd_a_log, ssd_d, ssd_norm_w, s5_lam_re, s5_lam_im, s5_log_step, s5_b_re, s5_b_im, s5_c_re, s5_c_im, s5_d, s5_glu_w, s5_glu_b, rg_conv_w, rg_conv_b, rg_wa, rg_ba, rg_wx, rg_bx, rg_lambda, ln1_g, ln1_b, xa_wq, xa_wk, xa_wv, xa_wo, ln2_g, ln2_b, mlp_w1, mlp_w2, ln3_g, ln3_b, loss_target, m_w_in, m_w_out, m_ssd_conv_w, m_ssd_conv_b, m_ssd_dt_bias, m_ssd_a_log, m_ssd_d, m_ssd_norm_w, m_s5_lam_re, m_s5_lam_im, m_s5_log_step, m_s5_b_re, m_s5_b_im, m_s5_c_re, m_s5_c_im, m_s5_d, m_s5_glu_w, m_s5_glu_b, m_rg_conv_w, m_rg_conv_b, m_rg_wa, m_rg_ba, m_rg_wx, m_rg_bx, m_rg_lambda, m_ln1_g, m_ln1_b, m_xa_wq, m_xa_wk, m_xa_wv, m_xa_wo, m_ln2_g, m_ln2_b, m_mlp_w1, m_mlp_w2, m_ln3_g, m_ln3_b, v_w_in, v_w_out, v_ssd_conv_w, v_ssd_conv_b, v_ssd_dt_bias, v_ssd_a_log, v_ssd_d, v_ssd_norm_w, v_s5_lam_re, v_s5_lam_im, v_s5_log_step, v_s5_b_re, v_s5_b_im, v_s5_c_re, v_s5_c_im, v_s5_d, v_s5_glu_w, v_s5_glu_b, v_rg_conv_w, v_rg_conv_b, v_rg_wa, v_rg_ba, v_rg_wx, v_rg_bx, v_rg_lambda, v_ln1_g, v_ln1_b, v_xa_wq, v_xa_wk, v_xa_wv, v_xa_wo, v_ln2_g, v_ln2_b, v_mlp_w1, v_mlp_w2, v_ln3_g, v_ln3_b):
    return _step(dict(locals()))
```

```python
import math

import jax
import jax.numpy as jnp
from jax import lax
from jax.experimental import pallas as pl
from jax.experimental.pallas import tpu as pltpu

F32 = jnp.float32
BF16 = jnp.bfloat16

N_DEV = 8
D_MODEL = 1024
DEPTH = 2
SSD_WIDTH = 512
SSD_HEADS = 8
SSD_HEAD_DIM = 64
SSD_STATE = 128
SSD_CHUNK = 128
SSD_XBC = 1024
S5_WIDTH = 256
S5_GROUPS = 16
S5_GROUP_CH = 16
S5_STATE = 64
S5_NSTATE = S5_GROUPS * S5_STATE
RG_WIDTH = 256
RG_BLOCKS = 4
RG_BLOCK_DIM = 64
RG_C = 8.0
XA_HEADS = 4
XA_HEAD_DIM = 256
ALPHA = (2.0 * DEPTH) ** 0.25
LN_EPS = 1e-5
ADAM_LR, ADAM_B1, ADAM_B2, ADAM_EPS, ADAM_WD, ADAM_STEP = 0.001, 0.9, 0.999, 1e-08, 0.01, 10

P_XBC, P_Z, P_U, P_XR, P_G, P_DT = 0, 1024, 1536, 1792, 2048, 2304
D_INP = 2560
LANE = 128
VMEM_LIMIT = 56 * 1024 * 1024
ROW_TILE = 512

_NN = ((1,), (0,))
_NT = ((1,), (1,))
_TN = ((0,), (0,))


def _dot(a, b, dims=_NN):
    return lax.dot_general(a.astype(BF16), b.astype(BF16), (dims, ((), ())), preferred_element_type=F32)


def _split_bf16(x, parts):
    out, rem = [], x
    for _ in range(parts):
        piece = rem.astype(BF16)
        out.append(piece)
        rem = rem - piece.astype(F32)
    return out


def _dot_mask(a, b, dims=_NN, *, mask_left, parts):
    if mask_left:
        return sum(_dot(a, piece, dims) for piece in _split_bf16(b, parts))
    return sum(_dot(piece, b, dims) for piece in _split_bf16(a, parts))


def _sigmoid(x):
    return 1.0 / (1.0 + jnp.exp(-x))


def _silu(x):
    return x * _sigmoid(x)


def _dsilu(x):
    s = _sigmoid(x)
    return s * (1.0 + x * (1.0 - s))


_GK = math.sqrt(2.0 / math.pi)
_GC = 0.044715


def _gelu(x):
    return 0.5 * x * (1.0 + jnp.tanh(_GK * (x + _GC * x * x * x)))


def _dgelu(x):
    th = jnp.tanh(_GK * (x + _GC * x * x * x))
    return 0.5 * (1.0 + th) + 0.5 * x * (1.0 - th * th) * _GK * (1.0 + 3.0 * _GC * x * x)


def _log1p_pos(e):
    return jnp.where(e < 1e-2, e * (1.0 - e * (0.5 - e * (1.0 / 3.0))), jnp.log(1.0 + e))


def _softplus(x):
    return jnp.maximum(x, 0.0) + _log1p_pos(jnp.exp(-jnp.abs(x)))


def _neg_expm1(x):
    poly = -x * (1.0 + x * (0.5 + x * (1.0 / 6.0 + x * (1.0 / 24.0 + x * (1.0 / 120.0)))))
    return jnp.where(x > -0.05, poly, 1.0 - jnp.exp(x))


def _params(sem):
    return pltpu.CompilerParams(dimension_semantics=sem, vmem_limit_bytes=VMEM_LIMIT)


RESIDENT_BYTES = 8 * 1024 * 1024
STREAM_BYTES = 4 * 1024 * 1024


def _halve_to_fit(dims, bytes_per, limit):
    dims = list(dims)
    while math.prod(dims) * bytes_per > limit:
        i = max(range(len(dims)), key=lambda d: dims[d])
        assert dims[i] % 256 == 0, dims
        dims[i] //= 2
    return dims


def _side_exchange(side, src, dst, sems, step, nsteps):
    kind, _, r0, rows = side
    span = pl.ds(r0, rows)
    if kind == "gather":
        phases = lambda: _ag_phases(src.at[span], dst, *sems)
        when = (0, (3 * nsteps) // 4, nsteps - 1)
    else:
        phases = lambda: _rs_chip_phases(src, dst, *sems, rows=span)
        when = (0, nsteps - 1)
    for idx, at in enumerate(when):
        pl.when(step == at)(lambda idx=idx: phases()[idx]())


def mm(a, b, *, name, ta=False, tb=False, a_extra=(), fa=None, o_extra=(), r_extra=(), fo=None, n_out=1,
       a_off=0, m=None, k=None, out_dtype=F32, side=None):
    n = b.shape[0] if tb else b.shape[1]
    na, no, nr = 1 + len(a_extra), len(o_extra), len(r_extra)
    if not ta:
        assert m is None
        m, kdim = a.shape[0], (a.shape[1] if k is None else k)
        assert a_off % kdim == 0
        (tn,) = _halve_to_fit([n], kdim * b.dtype.itemsize, RESIDENT_BYTES)
        (tm,) = _halve_to_fit([min(512, m)], max(tn, kdim) * 4, STREAM_BYTES)
        a_spec = pl.BlockSpec((tm, kdim), lambda i, j: (i, a_off // kdim))
        b_spec = pl.BlockSpec((tn, kdim), lambda i, j: (j, 0)) if tb else pl.BlockSpec((kdim, tn), lambda i, j: (0, j))
        o_spec = pl.BlockSpec((tm, tn), lambda i, j: (i, j))
        dims = _NT if tb else _NN

        r_spec = pl.BlockSpec((1, tn), lambda i, j: (0, j))

        grid = (m // tm, n // tn)
        nin = na + 1 + no + nr

        def body(*refs):
            a_refs, b_ref, e_refs, out_refs = refs[:na], refs[na], refs[na + 1:nin], refs[nin + (side is not None):nin + (side is not None) + n_out]
            if side is not None:
                _side_exchange(side, refs[nin], refs[nin + 1 + n_out], refs[nin + 2 + n_out:],
                               pl.program_id(0) * grid[1] + pl.program_id(1), grid[0] * grid[1])
            av = a_refs[0][...] if fa is None else fa(*[r[...] for r in a_refs])
            acc = _dot(av, b_ref[...], dims)
            res = acc if fo is None else fo(acc, *[r[...] for r in e_refs])
            for r, v in zip(out_refs, res if n_out > 1 else (res,)):
                r[...] = v.astype(r.dtype)

        sem = ("parallel", "parallel") if side is None else ("arbitrary", "arbitrary")
    else:
        assert k is None and not tb and fo is None and not o_extra and not r_extra and n_out == 1 and out_dtype == F32
        assert side is None
        kdim, m = a.shape[0], (a.shape[1] if m is None else m)
        r_spec = None
        tm, tn = _halve_to_fit([m, n], 4, RESIDENT_BYTES)
        (tk,) = _halve_to_fit([min(512, kdim)], max(tm, tn) * 4, STREAM_BYTES)
        assert a_off % tm == 0
        a_spec = pl.BlockSpec((tk, tm), lambda i, j, kk: (kk, i + a_off // tm))
        b_spec = pl.BlockSpec((tk, tn), lambda i, j, kk: (kk, j))
        o_spec = pl.BlockSpec((tm, tn), lambda i, j, kk: (i, j))

        def body(*refs):
            a_refs, b_ref, out_ref = refs[:na], refs[na], refs[na + 1]

            @pl.when(pl.program_id(2) == 0)
            def _():
                out_ref[...] = jnp.zeros_like(out_ref)

            av = a_refs[0][...] if fa is None else fa(*[r[...] for r in a_refs])
            out_ref[...] += _dot(av, b_ref[...], _TN)

        grid, sem = (m // tm, n // tn, kdim // tk), ("parallel", "parallel", "arbitrary")
    assert m % tm == 0 and n % tn == 0, (name, m, n, tm, tn)
    out = jax.ShapeDtypeStruct((m, n), out_dtype)
    if side is None:
        return pl.pallas_call(
            body, name=name, grid=grid,
            in_specs=[a_spec] * na + [b_spec] + [o_spec] * no + [r_spec] * nr,
            out_specs=o_spec if n_out == 1 else [o_spec] * n_out, out_shape=out if n_out == 1 else [out] * n_out,
            compiler_params=_params(sem),
        )(a, *a_extra, b, *o_extra, *r_extra)
    kind, arr, _, rows = side
    landed = jax.ShapeDtypeStruct(((N_DEV, rows) if kind == "gather" else (4, rows)) + arr.shape[-1:], arr.dtype)
    return pl.pallas_call(
        body, name=name, grid=grid,
        in_specs=[a_spec] * na + [b_spec] + [o_spec] * no + [r_spec] * nr + [_ANY],
        out_specs=[o_spec] * n_out + [_ANY], out_shape=[out] * n_out + [landed],
        scratch_shapes=list(_AG_SEMS if kind == "gather" else _RS_SEMS),
        compiler_params=_params(sem),
    )(a, *a_extra, b, *o_extra, *r_extra, arr)


def rowk(fn, tiled, full, out_w, acc_shapes, *, rows, name, out_dtypes=None):
    tt = min(ROW_TILE, rows)
    n = rows // tt
    assert rows % tt == 0
    nt, nf, no = len(tiled), len(full), len(out_w)

    def tspec(w, cb):
        return pl.BlockSpec((tt, w), lambda i: (i, cb))

    def fspec(a):
        nd = a.ndim
        return pl.BlockSpec(a.shape, lambda i: (0,) * nd)

    def body(*refs):
        ins, fulls = refs[:nt], refs[nt:nt + nf]
        outs, accs = refs[nt + nf:nt + nf + no], refs[nt + nf + no:]
        res_t, res_a = fn(*[r[...] for r in ins], *[r[...] for r in fulls])
        for r, v in zip(outs, res_t):
            r[...] = v.astype(r.dtype)
        if accs:
            @pl.when(pl.program_id(0) == 0)
            def _():
                for r in accs:
                    r[...] = jnp.zeros_like(r)
            for r, v in zip(accs, res_a):
                r[...] += v

    outs = pl.pallas_call(
        body, name=name, grid=(n,),
        in_specs=[tspec(w, cb) for (_, w, cb) in tiled] + [fspec(a) for a in full],
        out_specs=[tspec(w, 0) for w in out_w] + [pl.BlockSpec(s, lambda i, nd=len(s): (0,) * nd) for s in acc_shapes],
        out_shape=[jax.ShapeDtypeStruct((rows, w), dt) for w, dt in zip(out_w, out_dtypes or [F32] * no)]
        + [jax.ShapeDtypeStruct(s, F32) for s in acc_shapes],
        compiler_params=_params(("arbitrary",)),
    )(*[a for (a, _, _) in tiled], *full)
    return outs[:no], outs[no:]


def _colsum(x):
    return jnp.sum(x, axis=0, keepdims=True)


def _rowsum(x):
    return jnp.sum(x, axis=1, keepdims=True)


def _ln_epilogue(acc, resid, g, b):
    pre = ALPHA * resid + acc
    mu = jnp.mean(pre, axis=1, keepdims=True)
    xc = pre - mu
    var = jnp.mean(xc * xc, axis=1, keepdims=True)
    return pre, xc * lax.rsqrt(var + LN_EPS) * g + b


def _ln_bwd_fn(pre, dout, g):
    mu = jnp.mean(pre, axis=1, keepdims=True)
    xc = pre - mu
    var = jnp.mean(xc * xc, axis=1, keepdims=True)
    rstd = lax.rsqrt(var + LN_EPS)
    xhat = xc * rstd
    dxh = dout * g
    dpre = rstd * (dxh - jnp.mean(dxh, axis=1, keepdims=True) - xhat * jnp.mean(dxh * xhat, axis=1, keepdims=True))
    return (dpre,), (_colsum(dout * xhat), _colsum(dout))


def mm_ln(a, w, resid, g, b, *, name, fa=None, side=None):
    assert w.shape[1] == D_MODEL
    return mm(a, w, fa=fa, o_extra=(resid,), r_extra=(g, b), fo=_ln_epilogue, n_out=2, name=name, side=side)


def ln_bwd(pre, dout, g, *, name):
    (dpre,), (dg, db) = rowk(_ln_bwd_fn, [(pre, D_MODEL, 0), (dout, D_MODEL, 0)], [g],
                             [D_MODEL], [(1, D_MODEL), (1, D_MODEL)], rows=pre.shape[0], name=name)
    return dpre, dg, db


def _loss_fn(y, tgt):
    e = y - tgt
    part = _colsum(_rowsum(e * e)) * (0.5 / D_MODEL)
    return (e * (1.0 / D_MODEL),), (part,)


_XA_SCALE = 1.0 / math.sqrt(XA_HEAD_DIM)


def _attn_probs(qh, kh):
    s = _dot(qh, kh, _NT) * _XA_SCALE
    e = jnp.exp(s - jnp.max(s, axis=1, keepdims=True))
    return e / _rowsum(e)


def _attn_fwd_fn(q, k, v):
    outs = []
    for hd in range(XA_HEADS):
        sl = slice(hd * XA_HEAD_DIM, (hd + 1) * XA_HEAD_DIM)
        outs.append(_dot(_attn_probs(q[:, sl], k[:, sl]), v[:, sl]))
    return (jnp.concatenate(outs, axis=1),), ()


def _attn_bwd_fn(q, do, k, v):
    dqs, dks, dvs = [], [], []
    for hd in range(XA_HEADS):
        sl = slice(hd * XA_HEAD_DIM, (hd + 1) * XA_HEAD_DIM)
        qh, kh, vh, doh = q[:, sl], k[:, sl], v[:, sl], do[:, sl]
        p = _attn_probs(qh, kh)
        dp = _dot(doh, vh, _NT)
        ds = p * (dp - _rowsum(p * dp)) * _XA_SCALE
        dqs.append(_dot(ds, kh))
        dks.append(_dot(ds, qh, _TN))
        dvs.append(_dot(p, doh, _TN))
    cat = lambda xs: jnp.concatenate(xs, axis=1)
    return (cat(dqs),), (cat(dks), cat(dvs))


def _s5_post_fwd_fn(ylin, u, dskip, gw, gb):
    yg = _gelu(ylin + dskip * u)
    return (yg * _sigmoid(_dot(yg, gw) + gb),), ()


def _s5_post_bwd_fn(ylin, u, dout, dskip, gw, gb):
    pre = ylin + dskip * u
    yg = _gelu(pre)
    sg = _sigmoid(_dot(yg, gw) + gb)
    dlin = dout * yg * sg * (1.0 - sg)
    dyg = dout * sg + _dot(dlin, gw, _NT)
    dpre = dyg * _dgelu(pre)
    return (dpre, dpre * dskip), (_colsum(dpre * u), _dot(yg, dlin, _TN), _colsum(dlin))


def _rg_gates(xc, wa, wx, ba, bx, lam):
    r = _sigmoid(_dot(xc, wa) + ba)
    i = _sigmoid(_dot(xc, wx) + bx)
    sp = _softplus(-lam)
    log_a = -RG_C * r * sp
    a = jnp.exp(log_a)
    mult = jnp.sqrt(_neg_expm1(2.0 * log_a))
    return r, i, sp, a, mult


def _rg_pre_fwd_fn(xc, wa, wx, ba, bx, lam):
    r, i, sp, a, mult = _rg_gates(xc, wa, wx, ba, bx, lam)
    return (a, mult * (i * xc)), ()


def _rg_pre_bwd_fn(xc, gsc, hprev, wa, wx, ba, bx, lam):
    r, i, sp, a, mult = _rg_gates(xc, wa, wx, ba, bx, lam)
    da = gsc * hprev
    db = gsc
    dmult = db * i * xc
    di = db * mult * xc
    dxc = db * mult * i
    dlog_a = da * a - a * a * dmult / mult
    dr = dlog_a * (-RG_C * sp)
    dsp = _colsum(dlog_a * (-RG_C * r))
    dlam = dsp * (-_sigmoid(-lam))
    dpr = dr * r * (1.0 - r)
    dpi = di * i * (1.0 - i)
    dxc = dxc + _dot(dpr, wa, _NT) + _dot(dpi, wx, _NT)
    return (dxc,), (_dot(xc, dpr, _TN), _dot(xc, dpi, _TN), _colsum(dpr), _colsum(dpi), dlam)


def _rg_out_fwd_fn(h, g):
    return (h * _gelu(g),), ()


def _rg_out_bwd_fn(h, g, dy):
    return (dy * _gelu(g), dy * h * _dgelu(g)), ()


def _shift_down(x, prev, j, rows):
    return jnp.where(rows < j, pltpu.roll(prev, j, 0), pltpu.roll(x, j, 0))


def _shift_up(x, nxt, j, rows):
    t = x.shape[0]
    return jnp.where(rows >= t - j, pltpu.roll(nxt, t - j, 0), pltpu.roll(x, t - j, 0))


def conv_fwd(src, cb, w, b, *, width, act, name):
    t = src.shape[0]
    tt = min(ROW_TILE, t)
    n = t // tt

    def body(x_ref, w_ref, b_ref, y_ref, prev_ref):
        @pl.when(pl.program_id(0) == 0)
        def _():
            prev_ref[...] = jnp.zeros_like(prev_ref)

        x = x_ref[...]
        prev = prev_ref[...]
        rows = lax.broadcasted_iota(jnp.int32, x.shape, 0)
        wv = w_ref[...]
        y = b_ref[...] + wv[3:4, :] * x
        for j in (1, 2, 3):
            y = y + wv[3 - j:4 - j, :] * _shift_down(x, prev, j, rows)
        y_ref[...] = _silu(y) if act else y
        prev_ref[...] = x

    return pl.pallas_call(
        body, name=name, grid=(n,),
        in_specs=[pl.BlockSpec((tt, width), lambda i: (i, cb)),
                  pl.BlockSpec((4, width), lambda i: (0, 0)), pl.BlockSpec((1, width), lambda i: (0, 0))],
        out_specs=pl.BlockSpec((tt, width), lambda i: (i, 0)),
        out_shape=jax.ShapeDtypeStruct((t, width), F32),
        scratch_shapes=[pltpu.VMEM((tt, width), F32)],
        compiler_params=_params(("arbitrary",)),
    )(src, w, b)


def conv_bwd(src, cb, dy, w, b, *, width, act, name):
    t = src.shape[0]
    tt = min(ROW_TILE, t)
    n = t // tt

    def body(x_ref, xp_ref, dy_ref, w_ref, b_ref, dx_ref, dw_ref, db_ref, nxt_ref):
        i = pl.program_id(0)

        @pl.when(i == 0)
        def _():
            nxt_ref[...] = jnp.zeros_like(nxt_ref)
            dw_ref[...] = jnp.zeros_like(dw_ref)
            db_ref[...] = jnp.zeros_like(db_ref)

        x = x_ref[...]
        prev = jnp.where(i == n - 1, 0.0, xp_ref[...])
        rows = lax.broadcasted_iota(jnp.int32, x.shape, 0)
        wv = w_ref[...]
        xs = [x] + [_shift_down(x, prev, j, rows) for j in (1, 2, 3)]
        dpre = dy_ref[...]
        if act:
            pre = b_ref[...] + wv[3:4, :] * xs[0]
            for j in (1, 2, 3):
                pre = pre + wv[3 - j:4 - j, :] * xs[j]
            dpre = dpre * _dsilu(pre)
        nxt = nxt_ref[...]
        dx = wv[3:4, :] * dpre
        for j in (1, 2, 3):
            dx = dx + wv[3 - j:4 - j, :] * _shift_up(dpre, nxt, j, rows)
        dx_ref[...] = dx.astype(dx_ref.dtype)
        dw_ref[...] += jnp.concatenate([_colsum(dpre * xs[3 - kk]) for kk in range(4)], axis=0)
        db_ref[...] += _colsum(dpre)
        nxt_ref[...] = dpre

    return pl.pallas_call(
        body, name=name, grid=(n,),
        in_specs=[pl.BlockSpec((tt, width), lambda i: (n - 1 - i, cb)),
                  pl.BlockSpec((tt, width), lambda i: (jnp.maximum(n - 2 - i, 0), cb)),
                  pl.BlockSpec((tt, width), lambda i: (n - 1 - i, 0)),
                  pl.BlockSpec((4, width), lambda i: (0, 0)), pl.BlockSpec((1, width), lambda i: (0, 0))],
        out_specs=[pl.BlockSpec((tt, width), lambda i: (n - 1 - i, 0)),
                   pl.BlockSpec((4, width), lambda i: (0, 0)), pl.BlockSpec((1, width), lambda i: (0, 0))],
        out_shape=[jax.ShapeDtypeStruct((t, width), BF16), jax.ShapeDtypeStruct((4, width), F32),
                   jax.ShapeDtypeStruct((1, width), F32)],
        scratch_shapes=[pltpu.VMEM((tt, width), F32)],
        compiler_params=_params(("arbitrary",)),
    )(src, src, dy, w, b)


S5_CW = 256


def _cmul(ar, ai, br, bi):
    return ar * br - ai * bi, ar * bi + ai * br


def _scan8_complex(src_ref, dst_ref, lam_ref, st_ref, *, w, nb, reverse):
    rows = lax.broadcasted_iota(jnp.int32, (8, S5_CW), 0)
    b8 = lambda v: jnp.broadcast_to(v, (8, S5_CW))

    def shift(x, k):
        if reverse:
            return jnp.where(rows < 8 - k, pltpu.roll(x, 8 - k, 0), 0.0)
        return jnp.where(rows >= k, pltpu.roll(x, k, 0), 0.0)

    for c0 in range(0, w, S5_CW):
        re, im = pl.ds(c0, S5_CW), pl.ds(w + c0, S5_CW)
        pw = [(lam_ref[:, re], lam_ref[:, im])]
        for _ in range(7):
            pw.append(_cmul(*pw[-1], *pw[0]))
        pr, pi = b8(pw[7][0]), b8(pw[7][1])
        for j in range(7):
            sel = rows == (7 - j if reverse else j)
            pr, pi = jnp.where(sel, b8(pw[j][0]), pr), jnp.where(sel, b8(pw[j][1]), pi)
        steps = [(k, b8(pw[k - 1][0]), b8(pw[k - 1][1])) for k in (1, 2, 4)]
        edge = 0 if reverse else 7

        def blk(i, carry):
            hr, hi = carry
            base = pl.multiple_of((nb // 2 - 1 - i if reverse else i) * 16, 16)
            pend = []
            for off in ((8, 0) if reverse else (0, 8)):
                at = pl.ds(base + off, 8)
                xr, xi = src_ref[at, re], src_ref[at, im]
                for k, kr, ki in steps:
                    sr, si = shift(xr, k), shift(xi, k)
                    xr, xi = xr + kr * sr - ki * si, xi + kr * si + ki * sr
                pend.append((at, xr, xi))
            for at, xr, xi in pend:
                xr, xi = xr + pr * hr - pi * hi, xi + pr * hi + pi * hr
                dst_ref[at, re] = xr
                dst_ref[at, im] = xi
                hr, hi = b8(xr[edge:edge + 1, :]), b8(xi[edge:edge + 1, :])
            return hr, hi

        hr, hi = lax.fori_loop(0, nb // 2, blk, (st_ref[:, re], st_ref[:, im]))
        st_ref[:, re] = hr
        st_ref[:, im] = hi


def s5_fwd(proj, bcat, lam, ccat, *, name):
    t = proj.shape[0]
    tt = min(ROW_TILE, t)
    w2 = bcat.shape[1]

    def body(u_ref, b_ref, lam_ref, c_ref, h_ref, y_ref, bu_ref, st_ref):
        @pl.when(pl.program_id(0) == 0)
        def _():
            st_ref[...] = jnp.zeros_like(st_ref)

        bu_ref[...] = _dot(u_ref[...], b_ref[...])
        _scan8_complex(bu_ref, h_ref, lam_ref, st_ref, w=w2 // 2, nb=tt // 8, reverse=False)
        y_ref[...] = _dot(h_ref[...], c_ref[...])

    fixed = lambda a: pl.BlockSpec(a.shape, lambda i: (0, 0))
    return pl.pallas_call(
        body, name=name, grid=(t // tt,),
        in_specs=[pl.BlockSpec((tt, S5_WIDTH), lambda i: (i, P_U // S5_WIDTH)), fixed(bcat), fixed(lam), fixed(ccat)],
        out_specs=[pl.BlockSpec((tt, w2), lambda i: (i, 0)), pl.BlockSpec((tt, S5_WIDTH), lambda i: (i, 0))],
        out_shape=[jax.ShapeDtypeStruct((t, w2), F32), jax.ShapeDtypeStruct((t, S5_WIDTH), F32)],
        scratch_shapes=[pltpu.VMEM((tt, w2), F32), pltpu.VMEM((8, w2), F32)],
        compiler_params=_params(("arbitrary",)),
    )(proj, bcat, lam, ccat)


def s5_bwd(dylin, du_a, hs, proj, bcat, lam_adj, ccat, *, name):
    t = proj.shape[0]
    tt = min(ROW_TILE, t)
    n, w2 = t // tt, bcat.shape[1]
    w = w2 // 2

    def body(dy_ref, dua_ref, h_ref, hp_ref, u_ref, b_ref, lam_ref, c_ref,
             du_ref, dc_ref, db_ref, dar_ref, dai_ref, g_ref, st_ref):
        i = pl.program_id(0)

        @pl.when(i == 0)
        def _():
            for r in (st_ref, dc_ref, db_ref, dar_ref, dai_ref):
                r[...] = jnp.zeros_like(r)

        dy, h = dy_ref[...], h_ref[...]
        g_ref[...] = _dot(dy, c_ref[...], _NT)
        dc_ref[...] += _dot(h, dy, _TN)
        _scan8_complex(g_ref, g_ref, lam_ref, st_ref, w=w, nb=tt // 8, reverse=True)
        g = g_ref[...]
        du_ref[...] = (dua_ref[...] + _dot(g, b_ref[...], _NT)).astype(du_ref.dtype)
        db_ref[...] += _dot(u_ref[...], g, _TN)
        rows = lax.broadcasted_iota(jnp.int32, (tt, w2), 0)
        before = jnp.where(i == n - 1, 0.0, hp_ref[7:8, :])
        hprev = jnp.where(rows == 0, before, pltpu.roll(h, 1, 0))
        gr, gi, hr, hi = g[:, :w], g[:, w:], hprev[:, :w], hprev[:, w:]
        dar_ref[...] += _colsum(gr * hr + gi * hi)
        dai_ref[...] += _colsum(gi * hr - gr * hi)

    rev = lambda i: n - 1 - i
    row = lambda wd, cb=0: pl.BlockSpec((tt, wd), lambda i: (rev(i), cb))
    fixed = lambda shape: pl.BlockSpec(shape, lambda i: (0, 0))
    return pl.pallas_call(
        body, name=name, grid=(n,),
        in_specs=[row(S5_WIDTH), row(S5_WIDTH), row(w2),
                  pl.BlockSpec((8, w2), lambda i: (jnp.maximum(rev(i) * (tt // 8) - 1, 0), 0)),
                  row(S5_WIDTH, P_U // S5_WIDTH), fixed(bcat.shape), fixed(lam_adj.shape), fixed(ccat.shape)],
        out_specs=[row(S5_WIDTH), fixed(ccat.shape), fixed(bcat.shape), fixed((1, w)), fixed((1, w))],
        out_shape=[jax.ShapeDtypeStruct((t, S5_WIDTH), BF16), jax.ShapeDtypeStruct(ccat.shape, F32),
                   jax.ShapeDtypeStruct(bcat.shape, F32), jax.ShapeDtypeStruct((1, w), F32), jax.ShapeDtypeStruct((1, w), F32)],
        scratch_shapes=[pltpu.VMEM((tt, w2), F32), pltpu.VMEM((8, w2), F32)],
        compiler_params=_params(("arbitrary",)),
    )(dylin, du_a, hs, hs, proj, bcat, lam_adj, ccat)


def scan_real(a, b, *, reverse, name):
    t, w = b.shape
    tt = min(ROW_TILE, t)
    n, nb = t // tt, tt // 8

    def body(a_ref, b_ref, o_ref, st_ref):
        @pl.when(pl.program_id(0) == 0)
        def _():
            st_ref[...] = jnp.zeros_like(st_ref)

        rows = lax.broadcasted_iota(jnp.int32, (8, w), 0)

        def blk(i, h):
            base = pl.multiple_of((nb - 1 - i if reverse else i) * 8, 8)
            ta_, tb_ = a_ref[pl.ds(base, 8), :], b_ref[pl.ds(base, 8), :]
            out = jnp.zeros((8, w), F32)
            for j in (range(7, -1, -1) if reverse else range(8)):
                h = jnp.broadcast_to(ta_[j:j + 1, :], (8, w)) * h + jnp.broadcast_to(tb_[j:j + 1, :], (8, w))
                out = jnp.where(rows == j, h, out)
            o_ref[pl.ds(base, 8), :] = out
            return h

        st_ref[...] = lax.fori_loop(0, nb, blk, st_ref[...])

    idx = (lambda i: (n - 1 - i, 0)) if reverse else (lambda i: (i, 0))
    return pl.pallas_call(
        body, name=name, grid=(n,),
        in_specs=[pl.BlockSpec((tt, w), idx), pl.BlockSpec((tt, w), idx)],
        out_specs=pl.BlockSpec((tt, w), idx), out_shape=jax.ShapeDtypeStruct((t, w), F32),
        scratch_shapes=[pltpu.VMEM((8, w), F32)],
        compiler_params=_params(("arbitrary",)),
    )(a, b)


SSD_QQ = SSD_HEADS * SSD_CHUNK
SSD_GP = SSD_WIDTH // 2
SSD_GQ = SSD_QQ // 2


def _ssd_spread():
    h = jnp.arange(LANE)[:, None]
    spread_p = (jnp.arange(SSD_WIDTH)[None, :] // SSD_HEAD_DIM == h).astype(BF16)
    spread_q = (jnp.arange(SSD_QQ)[None, :] // SSD_CHUNK == h).astype(BF16)
    return spread_p, spread_q


def _ssd_prologue(dt_ref, prow_ref, sp_ref, sq_ref):
    q = SSD_CHUNK
    r = lax.broadcasted_iota(jnp.int32, (q, q), 0)
    c = lax.broadcasted_iota(jnp.int32, (q, q), 1)
    raw_c = dt_ref[...] + prow_ref[0:1, :]
    dt_c = _softplus(raw_c)
    a_r = -jnp.exp(prow_ref[1:2, :])
    cs_c = _dot_mask((r >= c).astype(F32), dt_c * a_r, mask_left=True, parts=3)
    both = _dot_mask(jnp.concatenate([dt_c, cs_c], axis=0), sp_ref[...], mask_left=False, parts=3)
    dt_x, cs_x = both[:q], both[q:]
    csx = _dot_mask(cs_c, sq_ref[...], mask_left=False, parts=3)
    rr = lax.broadcasted_iota(jnp.int32, (q, SSD_QQ), 0)
    ss = lax.broadcasted_iota(jnp.int32, (q, SSD_QQ), 1) & (q - 1)
    diag = rr == ss
    cs_row = _colsum(jnp.where(diag, csx, 0.0))
    lcat = jnp.exp(jnp.where(rr >= ss, csx - cs_row, -1e30))
    cl = cs_x[q - 1:q, :]
    return dict(raw_c=raw_c, dt_c=dt_c, a_r=a_r, dt_x=dt_x, cs_x=cs_x, lcat=lcat, diag=diag,
                ecs=jnp.exp(cs_x), wdec=jnp.exp(cl - cs_x), ecl=jnp.exp(cl), triu=(r <= c).astype(F32))


def _ssd_group(xbc_ref, g, lcat, xdt):
    ns, q = SSD_STATE, SSD_CHUNK
    bm = xbc_ref[:, pl.ds(SSD_WIDTH + g * ns, ns)]
    cm = xbc_ref[:, pl.ds(SSD_WIDTH + 2 * ns + g * ns, ns)]
    cb = _dot(cm, bm, _NT)
    lg = lcat[:, g * SSD_GQ:(g + 1) * SSD_GQ]
    wcat = jnp.concatenate([cb] * 4, axis=1) * lg
    head = lax.broadcasted_iota(jnp.int32, (1, SSD_GP), 1) // SSD_HEAD_DIM
    xg = xdt[:, g * SSD_GP:(g + 1) * SSD_GP]
    xbd = jnp.concatenate([jnp.where(head == j, xg, 0.0) for j in range(4)], axis=0)
    return bm, cm, lg, wcat, xbd, head


def _ssd_gate(yraw, z, nw):
    yg = yraw * _silu(z)
    r = lax.rsqrt(jnp.mean(yg * yg, axis=1, keepdims=True) + LN_EPS)
    return yg, r


def _ssd_specs(q, idx):
    return [pl.BlockSpec((q, SSD_XBC), lambda i: (idx(i), 0)),
            pl.BlockSpec((q, SSD_WIDTH), lambda i: (idx(i), P_Z // SSD_WIDTH)),
            pl.BlockSpec((q, LANE), lambda i: (idx(i), P_DT // LANE)),
            pl.BlockSpec((8, LANE), lambda i: (0, 0)), pl.BlockSpec((1, SSD_WIDTH), lambda i: (0, 0)),
            pl.BlockSpec((1, SSD_WIDTH), lambda i: (0, 0)),
            pl.BlockSpec((LANE, SSD_WIDTH), lambda i: (0, 0)), pl.BlockSpec((LANE, SSD_QQ), lambda i: (0, 0))]


def ssd_fwd(xbc, proj, prow, d_x, nw, *, name):
    t = xbc.shape[0]
    q, ns = SSD_CHUNK, SSD_STATE
    nc = t // q
    spread_p, spread_q = _ssd_spread()

    def body(xbc_ref, z_ref, dt_ref, prow_ref, dx_ref, nw_ref, sp_ref, sq_ref, y_ref, yraw_ref, sall_ref, s_ref):
        @pl.when(pl.program_id(0) == 0)
        def _():
            s_ref[...] = jnp.zeros_like(s_ref)

        sall_ref[0] = s_ref[...]
        pr = _ssd_prologue(dt_ref, prow_ref, sp_ref, sq_ref)
        xs = xbc_ref[:, pl.ds(0, SSD_WIDTH)]
        xdt = xs * pr["dt_x"]
        xw = xdt * pr["wdec"]
        ys = []
        for g in range(2):
            gp = slice(g * SSD_GP, (g + 1) * SSD_GP)
            bm, cm, lg, wcat, xbd, head = _ssd_group(xbc_ref, g, pr["lcat"], xdt)
            st = s_ref[:, gp]
            ys.append(_dot(wcat, xbd) + pr["ecs"][:, gp] * _dot(cm, st) + xs[:, gp] * dx_ref[:, gp])
            s_ref[:, gp] = pr["ecl"][:, gp] * st + _dot(bm, xw[:, gp], _TN)
        yraw = jnp.concatenate(ys, axis=1)
        yraw_ref[...] = yraw
        yg, r = _ssd_gate(yraw, z_ref[...], nw_ref[...])
        y_ref[...] = (yg * r * nw_ref[...]).astype(y_ref.dtype)

    row = pl.BlockSpec((q, SSD_WIDTH), lambda i: (i, 0))
    return pl.pallas_call(
        body, name=name, grid=(nc,),
        in_specs=_ssd_specs(q, lambda i: i),
        out_specs=[row, row, pl.BlockSpec((1, ns, SSD_WIDTH), lambda i: (i, 0, 0))],
        out_shape=[jax.ShapeDtypeStruct((t, SSD_WIDTH), BF16), jax.ShapeDtypeStruct((t, SSD_WIDTH), F32),
                   jax.ShapeDtypeStruct((nc, ns, SSD_WIDTH), F32)],
        scratch_shapes=[pltpu.VMEM((ns, SSD_WIDTH), F32)],
        compiler_params=_params(("arbitrary",)),
    )(xbc, proj, proj, prow, d_x, nw, spread_p, spread_q)


def ssd_bwd(xbc, proj, prow, d_x, nw, yraw, sall, dout, *, name):
    t = xbc.shape[0]
    q, ns = SSD_CHUNK, SSD_STATE
    nc = t // q
    spread_p, spread_q = _ssd_spread()

    def body(xbc_ref, z_ref, dt_ref, prow_ref, dx_ref, nw_ref, sp_ref, sq_ref, yraw_ref, sall_ref, dout_ref,
             dxbc_ref, dz_ref, ddt_ref, dprm_ref, ddx_ref, dnw_ref, ds_ref):
        @pl.when(pl.program_id(0) == 0)
        def _():
            ds_ref[...] = jnp.zeros_like(ds_ref)
            dprm_ref[...] = jnp.zeros_like(dprm_ref)
            ddx_ref[...] = jnp.zeros_like(ddx_ref)
            dnw_ref[...] = jnp.zeros_like(dnw_ref)

        yraw, z, nwv, dout = yraw_ref[...], z_ref[...], nw_ref[...], dout_ref[...]
        yg, r = _ssd_gate(yraw, z, nwv)
        dnw_ref[...] += _colsum(dout * yg * r)
        dyn = dout * nwv
        dyg = r * dyn - yg * (r * r * r) * jnp.mean(dyn * yg, axis=1, keepdims=True)
        dy = dyg * _silu(z)
        dz_ref[...] = (dyg * yraw * _dsilu(z)).astype(dz_ref.dtype)

        pr = _ssd_prologue(dt_ref, prow_ref, sp_ref, sq_ref)
        xs = xbc_ref[:, pl.ds(0, SSD_WIDTH)]
        xdt = xs * pr["dt_x"]
        wdec, ecl = pr["wdec"], pr["ecl"]
        xw = xdt * wdec
        dzm_all = pr["ecs"] * dy
        last = (lax.broadcasted_iota(jnp.int32, (q, 1), 0) == q - 1).astype(F32)
        dxs, dcsxs, es = [], [], []
        for g in range(2):
            gp = slice(g * SSD_GP, (g + 1) * SSD_GP)
            bm, cm, lg, wcat, xbd, head = _ssd_group(xbc_ref, g, pr["lcat"], xdt)
            dyg_ = dy[:, gp]
            dwcat = _dot(dyg_, xbd, _NT)
            dxbd = _dot(wcat, dyg_, _TN)
            dxg = sum(jnp.where(head == j, dxbd[j * q:(j + 1) * q], 0.0) for j in range(4))
            es.append(dwcat * wcat)
            dmm = dwcat * lg
            dm = dmm[:, 0:q] + dmm[:, q:2 * q] + dmm[:, 2 * q:3 * q] + dmm[:, 3 * q:4 * q]
            dcm = _dot(dm, bm)
            dbm = _dot(dm, cm, _TN)
            st = sall_ref[0, :, gp]
            zmat = _dot(cm, st)
            dzm = dzm_all[:, gp]
            dcm = dcm + _dot(dzm, st, _NT)
            dst = _dot(cm, dzm, _TN)
            dcsx = dzm * zmat
            dsn = ds_ref[:, gp]
            dst = dst + ecl[:, gp] * dsn
            dclx = _colsum(dsn * st) * ecl[:, gp]
            dxw = _dot(bm, dsn)
            dbm = dbm + _dot(xw[:, gp], dsn, _NT)
            dxg = dxg + wdec[:, gp] * dxw
            tw = dxw * xdt[:, gp] * wdec[:, gp]
            dclx = dclx + _colsum(tw)
            dcsxs.append(dcsx - tw + last * dclx)
            ds_ref[:, gp] = dst
            dxs.append(dxg)
            dxbc_ref[:, pl.ds(SSD_WIDTH + g * ns, ns)] = dbm
            dxbc_ref[:, pl.ds(SSD_WIDTH + 2 * ns + g * ns, ns)] = dcm
        dx = jnp.concatenate(dxs, axis=1)
        dxbc_ref[:, pl.ds(0, SSD_WIDTH)] = dx * pr["dt_x"] + dy * dx_ref[...]
        ddx_ref[...] += _colsum(dy * xs)
        red = _dot_mask(jnp.concatenate([jnp.concatenate(dcsxs, axis=1), dx * xs], axis=0), sp_ref[...], _NT,
                        mask_left=False, parts=2)
        e_all = jnp.concatenate(es, axis=1)
        e_red = _dot_mask(e_all - jnp.where(pr["diag"], _colsum(e_all), 0.0), sq_ref[...], _NT, mask_left=False, parts=2)
        dadt = _dot_mask(pr["triu"], red[:q] + e_red, mask_left=True, parts=2)
        draw = (red[q:] + dadt * pr["a_r"]) * _sigmoid(pr["raw_c"])
        ddt_ref[...] = draw.astype(ddt_ref.dtype)
        zero = jnp.zeros((6, LANE), F32)
        dprm_ref[...] += jnp.concatenate([_colsum(draw), _colsum(dadt * pr["dt_c"]) * pr["a_r"], zero], axis=0)

    rev = lambda i: nc - 1 - i
    row = lambda w: pl.BlockSpec((q, w), lambda i: (rev(i), 0))
    fixed = lambda shape: pl.BlockSpec(shape, lambda i: (0, 0))
    return pl.pallas_call(
        body, name=name, grid=(nc,),
        in_specs=_ssd_specs(q, rev) + [row(SSD_WIDTH), pl.BlockSpec((1, ns, SSD_WIDTH), lambda i: (rev(i), 0, 0)),
                                       row(SSD_WIDTH)],
        out_specs=[row(SSD_XBC), row(SSD_WIDTH), row(LANE), fixed((8, LANE)), fixed((1, SSD_WIDTH)), fixed((1, SSD_WIDTH))],
        out_shape=[jax.ShapeDtypeStruct((t, SSD_XBC), F32), jax.ShapeDtypeStruct((t, SSD_WIDTH), BF16),
                   jax.ShapeDtypeStruct((t, LANE), BF16), jax.ShapeDtypeStruct((8, LANE), F32),
                   jax.ShapeDtypeStruct((1, SSD_WIDTH), F32), jax.ShapeDtypeStruct((1, SSD_WIDTH), F32)],
        scratch_shapes=[pltpu.VMEM((ns, SSD_WIDTH), F32)],
        compiler_params=_params(("arbitrary",)),
    )(xbc, proj, proj, prow, d_x, nw, spread_p, spread_q, yraw, sall, dout)


def _me():
    return lax.axis_index("x"), lax.axis_index("y"), lax.axis_index("c")


_ANY = pl.BlockSpec(memory_space=pl.ANY)
_MESH = pl.DeviceIdType.MESH


_AG_SEMS = [pltpu.SemaphoreType.DMA((7,)), pltpu.SemaphoreType.DMA((7,)), pltpu.SemaphoreType.DMA(())]
_RS_SEMS = [pltpu.SemaphoreType.DMA((3,)), pltpu.SemaphoreType.DMA((3,)), pltpu.SemaphoreType.DMA(())]


def _ag_phases(src, dst, send_sems, recv_sems, local_sem):
    x, y, c = _me()
    me, sibling = (x, y, c), (x, y, 1 - c)
    chips = [(1 - x, y), (x, 1 - y), (1 - x, 1 - y)]

    def slot(px, py, pc):
        return dst.at[4 * px + 2 * py + pc]

    def copy(kk, blk, to, from_src=False):
        return pltpu.make_async_remote_copy(
            src_ref=src if from_src else slot(*blk), dst_ref=slot(*blk),
            send_sem=send_sems.at[kk], recv_sem=recv_sems.at[kk], device_id=to, device_id_type=_MESH)

    mine = lambda: pltpu.make_async_copy(src, slot(*me), local_sem)
    first = lambda: [copy(0, me, sibling, True)] + [copy(1 + j, me, (*chip, c), True) for j, chip in enumerate(chips)]
    passed = lambda j: copy(4 + j, (*chips[j], c), sibling)

    def start():
        mine().start()
        for cp in first():
            cp.start()

    def forward():
        for j, chip in enumerate(chips):
            copy(1 + j, (*chip, c), me).wait_recv()
            passed(j).start()

    def finish():
        copy(0, sibling, me).wait_recv()
        for j, chip in enumerate(chips):
            copy(4 + j, (*chip, 1 - c), me).wait_recv()
        for cp in first() + [passed(j) for j in range(3)]:
            cp.wait_send()
        mine().wait()

    return start, forward, finish


def _rs_chip_phases(src, dst, send_sems, recv_sems, local_sem, rows=None):
    x, y, c = _me()
    q_me = 2 * x + y
    pick = (lambda q: src.at[q]) if rows is None else (lambda q: src.at[q, rows])
    local = lambda: pltpu.make_async_copy(pick(q_me), dst.at[q_me], local_sem)
    copies = lambda: [pltpu.make_async_remote_copy(src_ref=pick(2 * px + py), dst_ref=dst.at[q_me], send_sem=send_sems.at[j],
                                                   recv_sem=recv_sems.at[j], device_id=(px, py, c), device_id_type=_MESH)
                      for j, (px, py) in enumerate([(1 - x, y), (x, 1 - y), (1 - x, 1 - y)])]

    def start():
        local().start()
        for cp in copies():
            cp.start()

    def finish():
        for cp in copies():
            cp.wait()
        local().wait()

    return start, finish


def all_gather(block, *, name):
    def body(src, dst, send_sems, recv_sems, local_sem):
        for phase in _ag_phases(src, dst, send_sems, recv_sems, local_sem):
            phase()

    return pl.pallas_call(
        body, name=name, in_specs=[_ANY], out_specs=_ANY,
        out_shape=jax.ShapeDtypeStruct((N_DEV,) + block.shape, block.dtype), scratch_shapes=list(_AG_SEMS),
    )(block)


RS_PIECES = 4


def rs_sibling_exchange(halves, *, name):
    _, nq, r, l = halves.shape
    rows = r // RS_PIECES
    assert r % RS_PIECES == 0 and rows % 16 == 0

    def body(src, dst, send_sems, recv_sems):
        x, y, c = _me()
        copies = []
        for q in range(nq):
            for i in range(RS_PIECES):
                kk = q * RS_PIECES + i
                cp = pltpu.make_async_remote_copy(
                    src_ref=src.at[1 - c, q, pl.ds(i * rows, rows)], dst_ref=dst.at[q, pl.ds(i * rows, rows)],
                    send_sem=send_sems.at[kk], recv_sem=recv_sems.at[kk], device_id=(x, y, 1 - c), device_id_type=_MESH)
                cp.start()
                copies.append(cp)
        for cp in copies:
            cp.wait()

    n_copies = nq * RS_PIECES
    return pl.pallas_call(
        body, name=name, in_specs=[_ANY], out_specs=_ANY,
        out_shape=jax.ShapeDtypeStruct((nq, r, l), halves.dtype),
        scratch_shapes=[pltpu.SemaphoreType.DMA((n_copies,)), pltpu.SemaphoreType.DMA((n_copies,))],
    )(halves)


def pair_sum_bf16(halves, theirs, *, name, tt=128):
    _, nq, r, wd = halves.shape
    tt = min(tt, r)
    parity = lax.axis_index("c").astype(jnp.int32).reshape(1)

    def body(c_ref, own_ref, sib_ref, o_ref):
        o_ref[...] = (own_ref[...] + sib_ref[...]).astype(BF16)

    return pl.pallas_call(
        body, name=name,
        grid_spec=pltpu.PrefetchScalarGridSpec(
            num_scalar_prefetch=1, grid=(nq, r // tt),
            in_specs=[pl.BlockSpec((None, None, tt, wd), lambda q, i, c: (c[0], q, i, 0)),
                      pl.BlockSpec((None, tt, wd), lambda q, i, c: (q, i, 0))],
            out_specs=pl.BlockSpec((None, tt, wd), lambda q, i, c: (q, i, 0))),
        out_shape=jax.ShapeDtypeStruct((nq, r, wd), BF16),
        compiler_params=_params(("parallel", "parallel")),
    )(parity, halves, theirs)


def rs_chip_exchange(part, *, name):
    def body(src, dst, send_sems, recv_sems, local_sem):
        for phase in _rs_chip_phases(src, dst, send_sems, recv_sems, local_sem):
            phase()

    return pl.pallas_call(
        body, name=name, in_specs=[_ANY], out_specs=_ANY,
        out_shape=jax.ShapeDtypeStruct(part.shape, part.dtype), scratch_shapes=list(_RS_SEMS),
    )(part)


def adamw(slabs, w, m, v, *, name, tt):
    ns, (r, wd) = slabs.shape[0], w.shape
    tt = min(tt, r)
    assert r % tt == 0

    def body(s_ref, w_ref, m_ref, v_ref, g_ref, d_ref, nm_ref, nv_ref):
        g = s_ref[0].astype(F32)
        for kdev in range(1, ns):
            g = g + s_ref[kdev].astype(F32)
        wv = w_ref[...]
        nm = ADAM_B1 * m_ref[...] + (1.0 - ADAM_B1) * g
        nv = ADAM_B2 * v_ref[...] + (1.0 - ADAM_B2) * (g * g)
        m_hat = nm / (1.0 - ADAM_B1 ** ADAM_STEP)
        v_hat = nv / (1.0 - ADAM_B2 ** ADAM_STEP)
        g_ref[...] = g
        d_ref[...] = -ADAM_LR * (m_hat / (jnp.sqrt(v_hat) + ADAM_EPS) + ADAM_WD * wv)
        nm_ref[...] = nm
        nv_ref[...] = nv

    spec = pl.BlockSpec((tt, wd), lambda i: (i, 0))
    return pl.pallas_call(
        body, name=name, grid=(r // tt,),
        in_specs=[pl.BlockSpec((ns, tt, wd), lambda i: (0, i, 0)), spec, spec, spec],
        out_specs=[spec] * 4, out_shape=[jax.ShapeDtypeStruct((r, wd), F32)] * 4,
        compiler_params=_params(("parallel",)),
    )(slabs, w, m, v)


WIDE = 1024
BIG = [("w_in", True, 289), ("w_out", False, 128), ("xa_wq", False, 128), ("xa_wk", False, 128), ("xa_wv", False, 128),
       ("xa_wo", False, 128), ("mlp_w2", False, 512), ("mlp_w1", True, 512)]
TINY = [("ssd_conv_w", 2), ("s5_glu_w", 1), ("rg_conv_w", 2)]
KEEP_F32 = ("ssd_conv_w", "rg_conv_w")
TINY_ROWS = 32
SHARDED = [name for name, _, _ in BIG] + [name for name, _ in TINY]
SMALL = ["ssd_conv_b", "ssd_dt_bias", "ssd_a_log", "ssd_d", "ssd_norm_w", "s5_lam_re", "s5_lam_im",
         "s5_log_step", "s5_b_re", "s5_b_im", "s5_c_re", "s5_c_im", "s5_d", "s5_glu_b", "rg_conv_b",
         "rg_wa", "rg_ba", "rg_wx", "rg_bx", "rg_lambda", "ln1_g", "ln1_b", "ln2_g", "ln2_b", "ln3_g", "ln3_b"]
WEIGHTS = ['w_in', 'w_out', 'ssd_conv_w', 'ssd_conv_b', 'ssd_dt_bias', 'ssd_a_log', 'ssd_d', 'ssd_norm_w',
           's5_lam_re', 's5_lam_im', 's5_log_step', 's5_b_re', 's5_b_im', 's5_c_re', 's5_c_im', 's5_d',
           's5_glu_w', 's5_glu_b', 'rg_conv_w', 'rg_conv_b', 'rg_wa', 'rg_ba', 'rg_wx', 'rg_bx', 'rg_lambda',
           'ln1_g', 'ln1_b', 'xa_wq', 'xa_wk', 'xa_wv', 'xa_wo', 'ln2_g', 'ln2_b', 'mlp_w1', 'mlp_w2',
           'ln3_g', 'ln3_b']


def _pad16(rows):
    return -(-rows // 16) * 16


def _pack_rows(flat, mult):
    n = flat.shape[-1]
    r = -(-n // (LANE * mult)) * mult
    pad = [(0, 0)] * (flat.ndim - 1) + [(0, r * LANE - n)]
    return jnp.pad(flat, pad).reshape(flat.shape[:-1] + (r, LANE))


def _unpack(packed, shapes):
    lead = packed.shape[:-2]
    flat = packed.reshape(lead + (-1,))
    out, off = [], 0
    for s in shapes:
        n = math.prod(s)
        out.append(flat[..., off:off + n].reshape(lead + tuple(s)))
        off += n
    return out


PACK_ROWS = 2048


def _tiny_block(flat):
    pad = [(0, 0)] * (flat.ndim - 1) + [(0, TINY_ROWS * WIDE - flat.shape[-1])]
    return jnp.pad(flat, pad).reshape(flat.shape[:-1] + (TINY_ROWS, WIDE))


def _pack_layer(big, tiny_flat=None):
    blocks, used = [], 0
    some = big[BIG[0][0]]

    def zeros(rows):
        return jnp.zeros(some.shape[:-2] + (rows, WIDE), some.dtype)

    for name, _, rows in BIG:
        blocks.append(jnp.pad(big[name], [(0, 0)] * (some.ndim - 2) + [(0, _pad16(rows) - rows), (0, 0)]))
        used += _pad16(rows)
    if tiny_flat is not None:
        blocks.append(_tiny_block(tiny_flat))
        used += TINY_ROWS
    return jnp.concatenate(blocks + [zeros(PACK_ROWS - used)], axis=-2)


def _unpack_layer(packed):
    big, off = {}, 0
    for name, _, rows in BIG:
        big[name] = packed[..., off:off + rows, :]
        off += _pad16(rows)
    return big, packed[..., off:off + TINY_ROWS, :].reshape(packed.shape[:-2] + (TINY_ROWS * WIDE,))


def _split_flat(flat, shapes):
    out, off = [], 0
    for s in shapes:
        n = math.prod(s)
        out.append(flat[..., off:off + n].reshape(flat.shape[:-1] + tuple(s)))
        off += n
    return out


def _to_full(gathered, axis):
    g = jnp.moveaxis(gathered, 0, axis)
    s = g.shape
    return g.reshape(s[:axis] + (s[axis] * s[axis + 1],) + s[axis + 2:])


def _to_slabs(full, axis):
    s = full.shape
    g = full.reshape(s[:axis] + (N_DEV, s[axis] // N_DEV) + s[axis + 1:])
    return jnp.moveaxis(g, axis, 0)


def _blockdiag(w):
    h, i, j = w.shape
    eye = jnp.eye(h, dtype=w.dtype)
    return (w[:, :, None, :] * eye[:, None, :, None]).reshape(h * i, h * j)


def _blockdiag_extract(m, h):
    i, j = m.shape[0] // h, m.shape[1] // h
    eye = jnp.eye(h, dtype=m.dtype)
    return (m.reshape(h, i, h, j) * eye[:, None, :, None]).sum(axis=2)


def _s5_disc(lr, li, ls, bre, bim):
    step = jnp.exp(ls)[:, None]
    er = jnp.exp(lr * step)
    ar, ai = er * jnp.cos(li * step), er * jnp.sin(li * step)
    nr, ni, den = ar - 1.0, ai, lr * lr + li * li
    qr, qi = (nr * lr + ni * li) / den, (ni * lr - nr * li) / den
    bbr = qr[..., None] * bre - qi[..., None] * bim
    bbi = qr[..., None] * bim + qi[..., None] * bre
    return ar, ai, bbr, bbi


def _row(v, width=None):
    v = v.reshape(1, -1)
    if width is not None and v.shape[1] < width:
        v = jnp.pad(v, ((0, 0), (0, width - v.shape[1])))
    return v


def _relu2(a):
    r = jnp.maximum(a, 0.0)
    return r * r


def _add_alpha(acc, d):
    return acc + ALPHA * d


def _shift_rows_down(x):
    return jnp.concatenate([jnp.zeros((1, x.shape[1]), x.dtype), x[:-1]], axis=0)


def _shift_rows_up(x):
    return jnp.concatenate([x[1:], jnp.zeros((1, x.shape[1]), x.dtype)], axis=0)


def _layer_params(full, small, l):
    p = {}
    w_in = full["w_in"]
    z, xbc, dt, u, xr, g = w_in[0:512], w_in[512:1536], w_in[1536:1544], w_in[1544:1800], w_in[1800:2056], w_in[2056:2312]
    p["w_inp"] = jnp.concatenate([xbc, z, u, xr, g, dt, jnp.zeros((D_INP - P_DT - 8, D_MODEL), w_in.dtype)], axis=0)
    for k_ in ("w_out", "xa_wq", "xa_wk", "xa_wv", "xa_wo", "mlp_w1", "mlp_w2"):
        p[k_] = full[k_]
    p["s5_glu_w"] = full["s5_glu_w"][l]
    p["ssd_cw"], p["ssd_cb"] = full["ssd_conv_w"][l], _row(small["ssd_conv_b"][l])
    dtb, alog, dsk = small["ssd_dt_bias"][l], small["ssd_a_log"][l], small["ssd_d"][l]
    p["prow"] = jnp.concatenate([_row(dtb, LANE), _row(alog, LANE), jnp.zeros((6, LANE), F32)], axis=0)
    p["ssd_dx"] = _row(jnp.repeat(dsk, SSD_HEAD_DIM))
    p["ssd_nw"] = _row(small["ssd_norm_w"][l])
    s5_in = (small["s5_lam_re"][l], small["s5_lam_im"][l], small["s5_log_step"][l], small["s5_b_re"][l], small["s5_b_im"][l])
    (ar, ai, bbr, bbi), p["s5_vjp"] = jax.vjp(_s5_disc, *s5_in)
    p["lam_fwd"] = jnp.concatenate([_row(ar), _row(ai)], axis=1)
    p["lam_adj"] = jnp.concatenate([_row(ar), _row(-ai)], axis=1)
    p["bcat"] = jnp.concatenate([_blockdiag(jnp.swapaxes(bbr, 1, 2)), _blockdiag(jnp.swapaxes(bbi, 1, 2))], axis=1)
    p["ccat"] = jnp.concatenate([_blockdiag(jnp.swapaxes(small["s5_c_re"][l], 1, 2)),
                                 -_blockdiag(jnp.swapaxes(small["s5_c_im"][l], 1, 2))], axis=0)
    p["s5_d"], p["s5_glu_b"] = _row(small["s5_d"][l]), _row(small["s5_glu_b"][l])
    p["rg_cw"], p["rg_cb"] = full["rg_conv_w"][l], _row(small["rg_conv_b"][l])
    p["rg_wa"], p["rg_wx"] = _blockdiag(small["rg_wa"][l]), _blockdiag(small["rg_wx"][l])
    p["rg_ba"], p["rg_bx"], p["rg_lam"] = _row(small["rg_ba"][l]), _row(small["rg_bx"][l]), _row(small["rg_lambda"][l])
    for i in (1, 2, 3):
        p[f"g{i}"], p[f"b{i}"] = _row(small[f"ln{i}_g"][l]), _row(small[f"ln{i}_b"][l])
    return p


def _take_side(res, n_out, got):
    res = res if isinstance(res, (list, tuple)) else (res,)
    got.extend(res[n_out:])
    return res[0] if n_out == 1 else res[:n_out]


def _layer_fwd(h0, mem, p, sides={}):
    t = h0.shape[0]
    s = {"h0": h0}
    got = []
    proj = _take_side(mm(h0, p["w_inp"], tb=True, name="in_proj", side=sides.get("in_proj")), 1, got)
    xbc = conv_fwd(proj, 0, p["ssd_cw"], p["ssd_cb"], width=SSD_XBC, act=True, name="ssd_conv_fwd")
    y_ssd, yraw, sall = ssd_fwd(xbc, proj, p["prow"], p["ssd_dx"], p["ssd_nw"], name="ssd_fwd")
    hs5, ylin = s5_fwd(proj, p["bcat"], p["lam_fwd"], p["ccat"], name="s5_fwd")
    (y_s5,), _ = rowk(_s5_post_fwd_fn, [(ylin, S5_WIDTH, 0), (proj, S5_WIDTH, P_U // S5_WIDTH)],
                      [p["s5_d"], p["s5_glu_w"], p["s5_glu_b"]], [S5_WIDTH], [], rows=t, name="s5_post_fwd", out_dtypes=[BF16])
    xc = conv_fwd(proj, P_XR // RG_WIDTH, p["rg_cw"], p["rg_cb"], width=RG_WIDTH, act=False, name="rg_conv_fwd")
    rg_full = [p["rg_wa"], p["rg_wx"], p["rg_ba"], p["rg_bx"], p["rg_lam"]]
    (a_rg, b_rg), _ = rowk(_rg_pre_fwd_fn, [(xc, RG_WIDTH, 0)], rg_full, [RG_WIDTH, RG_WIDTH], [], rows=t, name="rg_pre_fwd")
    h_rg = scan_real(a_rg, b_rg, reverse=False, name="rg_scan_fwd")
    (y_rg,), _ = rowk(_rg_out_fwd_fn, [(h_rg, RG_WIDTH, 0), (proj, RG_WIDTH, P_G // RG_WIDTH)], [], [RG_WIDTH], [],
                      rows=t, name="rg_out_fwd", out_dtypes=[BF16])
    ycat = jnp.concatenate([y_ssd, y_s5, y_rg], axis=1)
    pre1, h1 = mm_ln(ycat, p["w_out"], h0, p["g1"], p["b1"], name="out_proj")
    q = mm(h1, p["xa_wq"], name="xa_q", out_dtype=BF16)
    k = mm(mem, p["xa_wk"], name="xa_kv")
    v = mm(mem, p["xa_wv"], name="xa_kv")
    (o,), _ = rowk(_attn_fwd_fn, [(q, D_MODEL, 0)], [k, v], [D_MODEL], [], rows=t, name="xa_fwd", out_dtypes=[BF16])
    pre2, h2 = mm_ln(o, p["xa_wo"], h1, p["g2"], p["b2"], name="xa_o")
    a_mlp = _take_side(mm(h2, p["mlp_w1"], tb=True, name="mlp_up", side=sides.get("mlp_up")), 1, got)
    pre3, h3 = _take_side(mm_ln(a_mlp, p["mlp_w2"], h2, p["g3"], p["b3"], fa=_relu2, name="mlp_down",
                                side=sides.get("mlp_down")), 2, got)
    s.update(proj=proj, xbc=xbc, yraw=yraw, sall=sall, hs5=hs5, ylin=ylin, xc=xc, a_rg=a_rg, h_rg=h_rg,
             ycat=ycat, pre1=pre1, h1=h1, q=q, k=k, v=v, o=o, pre2=pre2, h2=h2, a_mlp=a_mlp, pre3=pre3)
    return h3, s, got


def _layer_bwd(dh3, mem, p, s, l, gfull, gsmall, sides={}):
    t = dh3.shape[0]
    proj = s["proj"]
    dpre3, dg3, db3 = ln_bwd(s["pre3"], dh3, p["g3"], name="ln_bwd")
    got = []
    da = _take_side(mm(dpre3, p["mlp_w2"], tb=True, o_extra=(s["a_mlp"],), fo=lambda acc, a: acc * 2.0 * jnp.maximum(a, 0.0),
                       name="mlp_da", out_dtype=BF16, side=sides.get("mlp_da")), 1, got)
    gfull["mlp_w2"][l] = mm(s["a_mlp"], dpre3, ta=True, fa=_relu2, name="mlp_dw2")
    gfull["mlp_w1"][l] = mm(da, s["h2"], ta=True, name="mlp_dw1")
    dh2 = _take_side(mm(da, p["mlp_w1"], o_extra=(dpre3,), fo=_add_alpha, name="mlp_dx", side=sides.get("mlp_dx")), 1, got)
    dpre2, dg2, db2 = ln_bwd(s["pre2"], dh2, p["g2"], name="ln_bwd")
    do = _take_side(mm(dpre2, p["xa_wo"], tb=True, name="xa_do", out_dtype=BF16, side=sides.get("xa_do")), 1, got)
    gfull["xa_wo"][l] = mm(s["o"], dpre2, ta=True, name="dw_sq")
    (dq,), (dk, dv) = rowk(_attn_bwd_fn, [(s["q"], D_MODEL, 0), (do, D_MODEL, 0)], [s["k"], s["v"]], [D_MODEL],
                           [(256, D_MODEL), (256, D_MODEL)], rows=t, name="xa_bwd", out_dtypes=[BF16])
    gfull["xa_wq"][l] = mm(s["h1"], dq, ta=True, name="dw_sq")
    gfull["xa_wk"][l] = mm(mem, dk, ta=True, name="dw_kv")
    gfull["xa_wv"][l] = mm(mem, dv, ta=True, name="dw_kv")
    dh1 = mm(dq, p["xa_wq"], tb=True, o_extra=(dpre2,), fo=_add_alpha, name="dx_sq")
    dpre1, dg1, db1 = ln_bwd(s["pre1"], dh1, p["g1"], name="ln_bwd")
    dycat = mm(dpre1, p["w_out"], tb=True, name="xa_do")
    gfull["w_out"][l] = mm(s["ycat"], dpre1, ta=True, name="dw_sq")
    (dh_rg, dg_rg), _ = rowk(_rg_out_bwd_fn, [(s["h_rg"], RG_WIDTH, 0), (proj, RG_WIDTH, P_G // RG_WIDTH), (dycat, RG_WIDTH, 3)],
                             [], [RG_WIDTH, RG_WIDTH], [], rows=t, name="rg_out_bwd", out_dtypes=[F32, BF16])
    g_rg = scan_real(_shift_rows_up(s["a_rg"]), dh_rg, reverse=True, name="rg_scan_bwd")
    rg_full = [p["rg_wa"], p["rg_wx"], p["rg_ba"], p["rg_bx"], p["rg_lam"]]
    (dxc,), (dwa, dwx, dba, dbx, dlam) = rowk(
        _rg_pre_bwd_fn, [(s["xc"], RG_WIDTH, 0), (g_rg, RG_WIDTH, 0), (_shift_rows_down(s["h_rg"]), RG_WIDTH, 0)], rg_full,
        [RG_WIDTH], [(RG_WIDTH, RG_WIDTH), (RG_WIDTH, RG_WIDTH), (1, RG_WIDTH), (1, RG_WIDTH), (1, RG_WIDTH)],
        rows=t, name="rg_pre_bwd")
    dxr, d_rgcw, d_rgcb = conv_bwd(proj, P_XR // RG_WIDTH, dxc, p["rg_cw"], p["rg_cb"], width=RG_WIDTH, act=False, name="rg_conv_bwd")
    (dylin, du_a), (d_s5d, d_gluw, d_glub) = rowk(
        _s5_post_bwd_fn, [(s["ylin"], S5_WIDTH, 0), (proj, S5_WIDTH, P_U // S5_WIDTH), (dycat, S5_WIDTH, 2)],
        [p["s5_d"], p["s5_glu_w"], p["s5_glu_b"]], [S5_WIDTH, S5_WIDTH],
        [(1, S5_WIDTH), (S5_WIDTH, S5_WIDTH), (1, S5_WIDTH)], rows=t, name="s5_post_bwd")
    du, dccat, dbcat, dar, dai = s5_bwd(dylin, du_a, s["hs5"], proj, p["bcat"], p["lam_adj"], p["ccat"], name="s5_bwd")
    dxbc_act, dz, ddt, dprm, ddx, dnw = ssd_bwd(s["xbc"], proj, p["prow"], p["ssd_dx"], p["ssd_nw"], s["yraw"], s["sall"], dycat,
                                               name="ssd_bwd")
    dxbc, d_scw, d_scb = conv_bwd(proj, 0, dxbc_act, p["ssd_cw"], p["ssd_cb"], width=SSD_XBC, act=True, name="ssd_conv_bwd")
    dproj = jnp.concatenate([dxbc, dz, du, dxr, dg_rg, ddt, jnp.zeros((t, D_INP - P_DT - LANE), BF16)], axis=1)
    dh0 = mm(dproj, p["w_inp"], o_extra=(dpre1,), fo=_add_alpha, name="in_proj_dx")
    dwp = mm(dproj, s["h0"], ta=True, name="in_proj_dw")
    gfull["w_in"][l] = jnp.concatenate([dwp[P_Z:P_Z + 512], dwp[P_XBC:P_XBC + 1024], dwp[P_DT:P_DT + 8],
                                        dwp[P_U:P_U + 256], dwp[P_XR:P_XR + 256], dwp[P_G:P_G + 256]], axis=0)
    gfull["ssd_conv_w"][l], gfull["rg_conv_w"][l], gfull["s5_glu_w"][l] = d_scw, d_rgcw, d_gluw
    ng, ns = S5_GROUPS, S5_STATE
    dbbr = jnp.swapaxes(_blockdiag_extract(dbcat[:, :S5_NSTATE], ng), 1, 2)
    dbbi = jnp.swapaxes(_blockdiag_extract(dbcat[:, S5_NSTATE:], ng), 1, 2)
    d_lr, d_li, d_ls, d_bre, d_bim = p["s5_vjp"]((dar.reshape(ng, ns), dai.reshape(ng, ns), dbbr, dbbi))
    gsmall["s5_lam_re"][l], gsmall["s5_lam_im"][l], gsmall["s5_log_step"][l] = d_lr, d_li, d_ls
    gsmall["s5_b_re"][l], gsmall["s5_b_im"][l] = d_bre, d_bim
    gsmall["s5_c_re"][l] = jnp.swapaxes(_blockdiag_extract(dccat[:S5_NSTATE], ng), 1, 2)
    gsmall["s5_c_im"][l] = -jnp.swapaxes(_blockdiag_extract(dccat[S5_NSTATE:], ng), 1, 2)
    gsmall["s5_d"][l], gsmall["s5_glu_b"][l] = d_s5d[0], d_glub[0]
    gsmall["ssd_conv_b"][l], gsmall["rg_conv_b"][l] = d_scb[0], d_rgcb[0]
    gsmall["ssd_dt_bias"][l], gsmall["ssd_a_log"][l] = dprm[0, :8], dprm[1, :8]
    gsmall["ssd_d"][l] = ddx.reshape(SSD_HEADS, SSD_HEAD_DIM).sum(axis=1)
    gsmall["ssd_norm_w"][l] = dnw[0]
    gsmall["rg_wa"][l], gsmall["rg_wx"][l] = _blockdiag_extract(dwa, RG_BLOCKS), _blockdiag_extract(dwx, RG_BLOCKS)
    gsmall["rg_ba"][l], gsmall["rg_bx"][l] = dba.reshape(RG_BLOCKS, RG_BLOCK_DIM), dbx.reshape(RG_BLOCKS, RG_BLOCK_DIM)
    gsmall["rg_lambda"][l] = dlam[0]
    for i, (dg, db) in zip((1, 2, 3), ((dg1, db1), (dg2, db2), (dg3, db3))):
        gsmall[f"ln{i}_g"][l], gsmall[f"ln{i}_b"][l] = dg[0], db[0]
    return dh0, got


def _step(a):
    h = a["x"][0]
    mem = a["mem"][0]
    t = h.shape[0]
    r4, r3 = PACK_ROWS // 4, 3 * PACK_ROWS // 8

    def my_shards(pre):
        return ({name: (jnp.swapaxes(a[pre + name], 1, 2) if tr else a[pre + name]) for name, tr, _ in BIG},
                [a[pre + name] for name, _ in TINY])

    def my_pack(pre, l):
        big, tiny = my_shards(pre)
        return _pack_layer({name: w[l] for name, w in big.items()},
                           jnp.concatenate([w.reshape(-1) for w in tiny]) if l == 0 else None)

    big, tiny = my_shards("")
    tiny16 = [(lax.bitcast_convert_type(w, BF16) if name in KEEP_F32 else w.astype(BF16)).reshape(-1)
              for (name, _), w in zip(TINY, tiny)]
    packed = [_pack_layer({name: w[l].astype(BF16) for name, w in big.items()}, jnp.concatenate(tiny16) if l == 0 else None)
              for l in range(DEPTH)]
    small = {name: a[name] for name in SMALL}

    def gathered_weights(g):
        gbig, gtiny = _unpack_layer(g)
        return {name: w.reshape(-1, WIDE) for name, w in gbig.items()}, gtiny

    full, gtiny = gathered_weights(all_gather(packed[0], name="ag_weights"))
    tiny_shapes = [w.shape + ((2,) if name in KEEP_F32 else ()) for (name, _), w in zip(TINY, tiny)]
    tiny_full = {name: _to_full(lax.bitcast_convert_type(g, F32) if name in KEEP_F32 else g, axis)
                 for (name, axis), g in zip(TINY, _split_flat(gtiny, tiny_shapes))}
    p0 = _layer_params({**full, **tiny_full}, small, 0)
    h, s0, got = _layer_fwd(h, mem, p0, sides={"in_proj": ("gather", packed[1], 0, r4), "mlp_up": ("gather", packed[1], r4, r3),
                                                "mlp_down": ("gather", packed[1], r4 + r3, r3)})
    full, _ = gathered_weights(jnp.concatenate(got, axis=1))
    p1 = _layer_params({**full, **tiny_full}, small, 1)
    h, s1, _ = _layer_fwd(h, mem, p1)
    (dh,), (loss_part,) = rowk(_loss_fn, [(h, D_MODEL, 0), (a["loss_target"][0], D_MODEL, 0)], [], [D_MODEL], [(1, 1)],
                               rows=t, name="loss_head")
    loss = lax.psum(loss_part[0, 0], ("x", "y", "c"))
    gfull = {name: [None] * DEPTH for name in SHARDED}
    gsmall = {name: [None] * DEPTH for name in SMALL}

    def chip_partials(l):
        gbig = {name: gfull[name][l].reshape(N_DEV, rows, WIDE) for name, _, rows in BIG}
        gtiny = None
        if l == 0:
            gtiny = jnp.concatenate([_to_slabs(jnp.stack(gfull[name]), axis).reshape(N_DEV, -1) for name, axis in TINY], axis=1)
        slabs = _pack_layer(gbig, gtiny)
        halves = jnp.swapaxes(slabs.reshape((4, 2) + slabs.shape[1:]), 0, 1)
        theirs = rs_sibling_exchange(halves, name="rs_sibling")
        return pair_sum_bf16(halves, theirs, name="rs_pair_sum")

    dh, _ = _layer_bwd(dh, mem, p1, s1, 1, gfull, gsmall)
    part1 = chip_partials(1)
    dh, got = _layer_bwd(dh, mem, p0, s0, 0, gfull, gsmall, sides={"mlp_da": ("chips", part1, 0, r3), "mlp_dx": ("chips", part1, r3, r3),
                                                                    "xa_do": ("chips", part1, 2 * r3, r4)})
    grad_x = dh[None]
    landed = [rs_chip_exchange(chip_partials(0), name="rs_chips"), jnp.concatenate(got, axis=1)]
    bigs = [adamw(landed[l], my_pack("", l), my_pack("m_", l), my_pack("v_", l), name="adamw_sharded", tt=128) for l in range(DEPTH)]
    gs = _pack_rows(jnp.concatenate([jnp.stack(gsmall[name]).reshape(-1) for name in SMALL]), 8)
    gs = all_gather(gs, name="ag_small_grads")
    pks = lambda pre: _pack_rows(jnp.concatenate([a[pre + name].reshape(-1) for name in SMALL]), 8)
    sm = adamw(gs, pks(""), pks("m_"), pks("v_"), name="adamw_replicated", tt=gs.shape[1])
    out = {}
    for i, kind in enumerate(("grad_", "delta_", "new_m_", "new_v_")):
        layers = [_unpack_layer(bigs[l][i]) for l in range(DEPTH)]
        for name, tr, _ in BIG:
            arr = jnp.stack([layers[l][0][name] for l in range(DEPTH)])
            out[kind + name] = jnp.swapaxes(arr, 1, 2) if tr else arr
        for (name, _), arr in zip(TINY, _split_flat(layers[0][1], [w.shape for w in tiny])):
            out[kind + name] = arr
        for name, arr in zip(SMALL, _unpack(sm[i], [a[name].shape for name in SMALL])):
            out[kind + name] = arr
    return (loss, grad_x) + tuple(out[kind + name] for kind in ("grad_", "delta_", "new_m_", "new_v_") for name in WEIGHTS)


def kernel(x, mem, w_in, w_out, ssd_conv_w, ssd_conv_b, ssd_dt_bias, ssd_a_log, ssd_d, ssd_norm_w, s5_lam_re, s5_lam_im, s5_log_step, s5_b_re, s5_b_im, s5_c_re, s5_c_im, s5_d, s5_glu_w, s5_glu_b, rg_conv_w, rg_conv_b, rg_wa, rg_ba, rg_wx, rg_bx, rg_lambda, ln1_g, ln1_b, xa_wq, xa_wk, xa_wv, xa_wo, ln2_g, ln2_b, mlp_w1, mlp_w2, ln3_g, ln3_b, loss_target, m_w_in, m_w_out, m_ssd_conv_w, m_ssd_conv_b, m_ssd_dt_bias, m_ssd_a_log, m_ssd_d, m_ssd_norm_w, m_s5_lam_re, m_s5_lam_im, m_s5_log_step, m_s5_b_re, m_s5_b_im, m_s5_c_re, m_s5_c_im, m_s5_d, m_s5_glu_w, m_s5_glu_b, m_rg_conv_w, m_rg_conv_b, m_rg_wa, m_rg_ba, m_rg_wx, m_rg_bx, m_rg_lambda, m_ln1_g, m_ln1_b, m_xa_wq, m_xa_wk, m_xa_wv, m_xa_wo, m_ln2_g, m_ln2_b, m_mlp_w1, m_mlp_w2, m_ln3_g, m_ln3_b, v_w_in, v_w_out, v_ssd_conv_w, v_ssd_conv_b, v_ssd_dt_bias, v_ssd_a_log, v_ssd_d, v_ssd_norm_w, v_s5_lam_re, v_s5_lam_im, v_s5_log_step, v_s5_b_re, v_s5_b_im, v_s5_c_re, v_s5_c_im, v_s5_d, v_s5_glu_w, v_s5_glu_b, v_rg_conv_w, v_rg_conv_b, v_rg_wa, v_rg_ba, v_rg_wx, v_rg_bx, v_rg_lambda, v_ln1_g, v_ln1_b, v_xa_wq, v_xa_wk, v_xa_wv, v_xa_wo, v_ln2_g, v_ln2_b, v_mlp_w1, v_mlp_w2, v_ln3_g, v_ln3_b):
    return _step(dict(locals()))
```

```python
import math

import jax
import jax.numpy as jnp
from jax import lax
from jax.experimental import pallas as pl
from jax.experimental.pallas import tpu as pltpu

F32 = jnp.float32
BF16 = jnp.bfloat16

N_DEV = 8
D_MODEL = 1024
DEPTH = 2
SSD_WIDTH = 512
SSD_HEADS = 8
SSD_HEAD_DIM = 64
SSD_STATE = 128
SSD_CHUNK = 128
SSD_XBC = 1024
S5_WIDTH = 256
S5_GROUPS = 16
S5_GROUP_CH = 16
S5_STATE = 64
S5_NSTATE = S5_GROUPS * S5_STATE
RG_WIDTH = 256
RG_BLOCKS = 4
RG_BLOCK_DIM = 64
RG_C = 8.0
XA_HEADS = 4
XA_HEAD_DIM = 256
ALPHA = (2.0 * DEPTH) ** 0.25
LN_EPS = 1e-5
ADAM_LR, ADAM_B1, ADAM_B2, ADAM_EPS, ADAM_WD, ADAM_STEP = 0.001, 0.9, 0.999, 1e-08, 0.01, 10

P_XBC, P_Z, P_U, P_XR, P_G, P_DT = 0, 1024, 1536, 1792, 2048, 2304
D_INP = 2560
LANE = 128
VMEM_LIMIT = 56 * 1024 * 1024
ROW_TILE = 512

_NN = ((1,), (0,))
_NT = ((1,), (1,))
_TN = ((0,), (0,))


def _dot(a, b, dims=_NN):
    return lax.dot_general(a.astype(BF16), b.astype(BF16), (dims, ((), ())), preferred_element_type=F32)


def _split_bf16(x, parts):
    out, rem = [], x
    for _ in range(parts):
        piece = rem.astype(BF16)
        out.append(piece)
        rem = rem - piece.astype(F32)
    return out


def _dot_mask(a, b, dims=_NN, *, mask_left, parts):
    if mask_left:
        return sum(_dot(a, piece, dims) for piece in _split_bf16(b, parts))
    return sum(_dot(piece, b, dims) for piece in _split_bf16(a, parts))


def _sigmoid(x):
    return 1.0 / (1.0 + jnp.exp(-x))


def _silu(x):
    return x * _sigmoid(x)


def _dsilu(x):
    s = _sigmoid(x)
    return s * (1.0 + x * (1.0 - s))


_GK = math.sqrt(2.0 / math.pi)
_GC = 0.044715


def _gelu(x):
    return 0.5 * x * (1.0 + jnp.tanh(_GK * (x + _GC * x * x * x)))


def _dgelu(x):
    th = jnp.tanh(_GK * (x + _GC * x * x * x))
    return 0.5 * (1.0 + th) + 0.5 * x * (1.0 - th * th) * _GK * (1.0 + 3.0 * _GC * x * x)


def _log1p_pos(e):
    return jnp.where(e < 1e-2, e * (1.0 - e * (0.5 - e * (1.0 / 3.0))), jnp.log(1.0 + e))


def _softplus(x):
    return jnp.maximum(x, 0.0) + _log1p_pos(jnp.exp(-jnp.abs(x)))


def _neg_expm1(x):
    poly = -x * (1.0 + x * (0.5 + x * (1.0 / 6.0 + x * (1.0 / 24.0 + x * (1.0 / 120.0)))))
    return jnp.where(x > -0.05, poly, 1.0 - jnp.exp(x))


def _params(sem):
    return pltpu.CompilerParams(dimension_semantics=sem, vmem_limit_bytes=VMEM_LIMIT)


RESIDENT_BYTES = 8 * 1024 * 1024
STREAM_BYTES = 4 * 1024 * 1024


def _halve_to_fit(dims, bytes_per, limit):
    dims = list(dims)
    while math.prod(dims) * bytes_per > limit:
        i = max(range(len(dims)), key=lambda d: dims[d])
        assert dims[i] % 256 == 0, dims
        dims[i] //= 2
    return dims


def _side_exchange(side, src, dst, sems, step, nsteps):
    kind, _, r0, rows = side
    span = pl.ds(r0, rows)
    if kind == "gather":
        phases = lambda: _ag_phases(src.at[span], dst, *sems)
        when = (0, (3 * nsteps) // 4, nsteps - 1)
    else:
        phases = lambda: _rs_chip_phases(src, dst, *sems, rows=span)
        when = (0, nsteps - 1)
    for idx, at in enumerate(when):
        pl.when(step == at)(lambda idx=idx: phases()[idx]())


def mm(a, b, *, name, ta=False, tb=False, a_extra=(), fa=None, o_extra=(), r_extra=(), fo=None, n_out=1,
       a_off=0, m=None, k=None, out_dtype=F32, side=None):
    n = b.shape[0] if tb else b.shape[1]
    na, no, nr = 1 + len(a_extra), len(o_extra), len(r_extra)
    if not ta:
        assert m is None
        m, kdim = a.shape[0], (a.shape[1] if k is None else k)
        assert a_off % kdim == 0
        (tn,) = _halve_to_fit([n], kdim * b.dtype.itemsize, RESIDENT_BYTES)
        (tm,) = _halve_to_fit([min(512, m)], max(tn, kdim) * 4, STREAM_BYTES)
        a_spec = pl.BlockSpec((tm, kdim), lambda i, j: (i, a_off // kdim))
        b_spec = pl.BlockSpec((tn, kdim), lambda i, j: (j, 0)) if tb else pl.BlockSpec((kdim, tn), lambda i, j: (0, j))
        o_spec = pl.BlockSpec((tm, tn), lambda i, j: (i, j))
        dims = _NT if tb else _NN

        r_spec = pl.BlockSpec((1, tn), lambda i, j: (0, j))

        grid = (m // tm, n // tn)
        nin = na + 1 + no + nr

        def body(*refs):
            a_refs, b_ref, e_refs, out_refs = refs[:na], refs[na], refs[na + 1:nin], refs[nin + (side is not None):nin + (side is not None) + n_out]
            if side is not None:
                _side_exchange(side, refs[nin], refs[nin + 1 + n_out], refs[nin + 2 + n_out:],
                               pl.program_id(0) * grid[1] + pl.program_id(1), grid[0] * grid[1])
            av = a_refs[0][...] if fa is None else fa(*[r[...] for r in a_refs])
            acc = _dot(av, b_ref[...], dims)
            res = acc if fo is None else fo(acc, *[r[...] for r in e_refs])
            for r, v in zip(out_refs, res if n_out > 1 else (res,)):
                r[...] = v.astype(r.dtype)

        sem = ("parallel", "parallel") if side is None else ("arbitrary", "arbitrary")
    else:
        assert k is None and not tb and fo is None and not o_extra and not r_extra and n_out == 1 and out_dtype == F32
        assert side is None
        kdim, m = a.shape[0], (a.shape[1] if m is None else m)
        r_spec = None
        tm, tn = _halve_to_fit([m, n], 4, RESIDENT_BYTES)
        (tk,) = _halve_to_fit([min(512, kdim)], max(tm, tn) * 4, STREAM_BYTES)
        assert a_off % tm == 0
        a_spec = pl.BlockSpec((tk, tm), lambda i, j, kk: (kk, i + a_off // tm))
        b_spec = pl.BlockSpec((tk, tn), lambda i, j, kk: (kk, j))
        o_spec = pl.BlockSpec((tm, tn), lambda i, j, kk: (i, j))

        def body(*refs):
            a_refs, b_ref, out_ref = refs[:na], refs[na], refs[na + 1]

            @pl.when(pl.program_id(2) == 0)
            def _():
                out_ref[...] = jnp.zeros_like(out_ref)

            av = a_refs[0][...] if fa is None else fa(*[r[...] for r in a_refs])
            out_ref[...] += _dot(av, b_ref[...], _TN)

        grid, sem = (m // tm, n // tn, kdim // tk), ("parallel", "parallel", "arbitrary")
    assert m % tm == 0 and n % tn == 0, (name, m, n, tm, tn)
    out = jax.ShapeDtypeStruct((m, n), out_dtype)
    if side is None:
        return pl.pallas_call(
            body, name=name, grid=grid,
            in_specs=[a_spec] * na + [b_spec] + [o_spec] * no + [r_spec] * nr,
            out_specs=o_spec if n_out == 1 else [o_spec] * n_out, out_shape=out if n_out == 1 else [out] * n_out,
            compiler_params=_params(sem),
        )(a, *a_extra, b, *o_extra, *r_extra)
    kind, arr, _, rows = side
    landed = jax.ShapeDtypeStruct(((N_DEV, rows) if kind == "gather" else (4, rows)) + arr.shape[-1:], arr.dtype)
    return pl.pallas_call(
        body, name=name, grid=grid,
        in_specs=[a_spec] * na + [b_spec] + [o_spec] * no + [r_spec] * nr + [_ANY],
        out_specs=[o_spec] * n_out + [_ANY], out_shape=[out] * n_out + [landed],
        scratch_shapes=list(_AG_SEMS if kind == "gather" else _RS_SEMS),
        compiler_params=_params(sem),
    )(a, *a_extra, b, *o_extra, *r_extra, arr)


def rowk(fn, tiled, full, out_w, acc_shapes, *, rows, name, out_dtypes=None):
    tt = min(ROW_TILE, rows)
    n = rows // tt
    assert rows % tt == 0
    nt, nf, no = len(tiled), len(full), len(out_w)

    def tspec(w, cb):
        return pl.BlockSpec((tt, w), lambda i: (i, cb))

    def fspec(a):
        nd = a.ndim
        return pl.BlockSpec(a.shape, lambda i: (0,) * nd)

    def body(*refs):
        ins, fulls = refs[:nt], refs[nt:nt + nf]
        outs, accs = refs[nt + nf:nt + nf + no], refs[nt + nf + no:]
        res_t, res_a = fn(*[r[...] for r in ins], *[r[...] for r in fulls])
        for r, v in zip(outs, res_t):
            r[...] = v.astype(r.dtype)
        if accs:
            @pl.when(pl.program_id(0) == 0)
            def _():
                for r in accs:
                    r[...] = jnp.zeros_like(r)
            for r, v in zip(accs, res_a):
                r[...] += v

    outs = pl.pallas_call(
        body, name=name, grid=(n,),
        in_specs=[tspec(w, cb) for (_, w, cb) in tiled] + [fspec(a) for a in full],
        out_specs=[tspec(w, 0) for w in out_w] + [pl.BlockSpec(s, lambda i, nd=len(s): (0,) * nd) for s in acc_shapes],
        out_shape=[jax.ShapeDtypeStruct((rows, w), dt) for w, dt in zip(out_w, out_dtypes or [F32] * no)]
        + [jax.ShapeDtypeStruct(s, F32) for s in acc_shapes],
        compiler_params=_params(("arbitrary",)),
    )(*[a for (a, _, _) in tiled], *full)
    return outs[:no], outs[no:]


def _colsum(x):
    return jnp.sum(x, axis=0, keepdims=True)


def _rowsum(x):
    return jnp.sum(x, axis=1, keepdims=True)


def _ln_epilogue(acc, resid, g, b):
    pre = ALPHA * resid + acc
    mu = jnp.mean(pre, axis=1, keepdims=True)
    xc = pre - mu
    var = jnp.mean(xc * xc, axis=1, keepdims=True)
    return pre, xc * lax.rsqrt(var + LN_EPS) * g + b


def _ln_bwd_fn(pre, dout, g):
    mu = jnp.mean(pre, axis=1, keepdims=True)
    xc = pre - mu
    var = jnp.mean(xc * xc, axis=1, keepdims=True)
    rstd = lax.rsqrt(var + LN_EPS)
    xhat = xc * rstd
    dxh = dout * g
    dpre = rstd * (dxh - jnp.mean(dxh, axis=1, keepdims=True) - xhat * jnp.mean(dxh * xhat, axis=1, keepdims=True))
    return (dpre,), (_colsum(dout * xhat), _colsum(dout))


def mm_ln(a, w, resid, g, b, *, name, fa=None, side=None):
    assert w.shape[1] == D_MODEL
    return mm(a, w, fa=fa, o_extra=(resid,), r_extra=(g, b), fo=_ln_epilogue, n_out=2, name=name, side=side)


def ln_bwd(pre, dout, g, *, name):
    (dpre,), (dg, db) = rowk(_ln_bwd_fn, [(pre, D_MODEL, 0), (dout, D_MODEL, 0)], [g],
                             [D_MODEL], [(1, D_MODEL), (1, D_MODEL)], rows=pre.shape[0], name=name)
    return dpre, dg, db


def _loss_fn(y, tgt):
    e = y - tgt
    part = _colsum(_rowsum(e * e)) * (0.5 / D_MODEL)
    return (e * (1.0 / D_MODEL),), (part,)


_XA_SCALE = 1.0 / math.sqrt(XA_HEAD_DIM)


def _attn_probs(qh, kh):
    s = _dot(qh, kh, _NT) * _XA_SCALE
    e = jnp.exp(s - jnp.max(s, axis=1, keepdims=True))
    return e / _rowsum(e)


def _attn_fwd_fn(q, k, v):
    outs = []
    for hd in range(XA_HEADS):
        sl = slice(hd * XA_HEAD_DIM, (hd + 1) * XA_HEAD_DIM)
        outs.append(_dot(_attn_probs(q[:, sl], k[:, sl]), v[:, sl]))
    return (jnp.concatenate(outs, axis=1),), ()


def _attn_bwd_fn(q, do, k, v):
    dqs, dks, dvs = [], [], []
    for hd in range(XA_HEADS):
        sl = slice(hd * XA_HEAD_DIM, (hd + 1) * XA_HEAD_DIM)
        qh, kh, vh, doh = q[:, sl], k[:, sl], v[:, sl], do[:, sl]
        p = _attn_probs(qh, kh)
        dp = _dot(doh, vh, _NT)
        ds = p * (dp - _rowsum(p * dp)) * _XA_SCALE
        dqs.append(_dot(ds, kh))
        dks.append(_dot(ds, qh, _TN))
        dvs.append(_dot(p, doh, _TN))
    cat = lambda xs: jnp.concatenate(xs, axis=1)
    return (cat(dqs),), (cat(dks), cat(dvs))


def _s5_post_fwd_fn(ylin, u, dskip, gw, gb):
    yg = _gelu(ylin + dskip * u)
    return (yg * _sigmoid(_dot(yg, gw) + gb),), ()


def _s5_post_bwd_fn(ylin, u, dout, dskip, gw, gb):
    pre = ylin + dskip * u
    yg = _gelu(pre)
    sg = _sigmoid(_dot(yg, gw) + gb)
    dlin = dout * yg * sg * (1.0 - sg)
    dyg = dout * sg + _dot(dlin, gw, _NT)
    dpre = dyg * _dgelu(pre)
    return (dpre, dpre * dskip), (_colsum(dpre * u), _dot(yg, dlin, _TN), _colsum(dlin))


def _rg_gates(xc, wa, wx, ba, bx, lam):
    r = _sigmoid(_dot(xc, wa) + ba)
    i = _sigmoid(_dot(xc, wx) + bx)
    sp = _softplus(-lam)
    log_a = -RG_C * r * sp
    a = jnp.exp(log_a)
    mult = jnp.sqrt(_neg_expm1(2.0 * log_a))
    return r, i, sp, a, mult


def _rg_pre_fwd_fn(xc, wa, wx, ba, bx, lam):
    r, i, sp, a, mult = _rg_gates(xc, wa, wx, ba, bx, lam)
    return (a, mult * (i * xc)), ()


def _rg_pre_bwd_fn(xc, gsc, hprev, wa, wx, ba, bx, lam):
    r, i, sp, a, mult = _rg_gates(xc, wa, wx, ba, bx, lam)
    da = gsc * hprev
    db = gsc
    dmult = db * i * xc
    di = db * mult * xc
    dxc = db * mult * i
    dlog_a = da * a - a * a * dmult / mult
    dr = dlog_a * (-RG_C * sp)
    dsp = _colsum(dlog_a * (-RG_C * r))
    dlam = dsp * (-_sigmoid(-lam))
    dpr = dr * r * (1.0 - r)
    dpi = di * i * (1.0 - i)
    dxc = dxc + _dot(dpr, wa, _NT) + _dot(dpi, wx, _NT)
    return (dxc,), (_dot(xc, dpr, _TN), _dot(xc, dpi, _TN), _colsum(dpr), _colsum(dpi), dlam)


def _rg_out_fwd_fn(h, g):
    return (h * _gelu(g),), ()


def _rg_out_bwd_fn(h, g, dy):
    return (dy * _gelu(g), dy * h * _dgelu(g)), ()


def _shift_down(x, prev, j, rows):
    return jnp.where(rows < j, pltpu.roll(prev, j, 0), pltpu.roll(x, j, 0))


def _shift_up(x, nxt, j, rows):
    t = x.shape[0]
    return jnp.where(rows >= t - j, pltpu.roll(nxt, t - j, 0), pltpu.roll(x, t - j, 0))


def conv_fwd(src, cb, w, b, *, width, act, name):
    t = src.shape[0]
    tt = min(ROW_TILE, t)
    n = t // tt

    def body(x_ref, w_ref, b_ref, y_ref, prev_ref):
        @pl.when(pl.program_id(0) == 0)
        def _():
            prev_ref[...] = jnp.zeros_like(prev_ref)

        x = x_ref[...]
        prev = prev_ref[...]
        rows = lax.broadcasted_iota(jnp.int32, x.shape, 0)
        wv = w_ref[...]
        y = b_ref[...] + wv[3:4, :] * x
        for j in (1, 2, 3):
            y = y + wv[3 - j:4 - j, :] * _shift_down(x, prev, j, rows)
        y_ref[...] = _silu(y) if act else y
        prev_ref[...] = x

    return pl.pallas_call(
        body, name=name, grid=(n,),
        in_specs=[pl.BlockSpec((tt, width), lambda i: (i, cb)),
                  pl.BlockSpec((4, width), lambda i: (0, 0)), pl.BlockSpec((1, width), lambda i: (0, 0))],
        out_specs=pl.BlockSpec((tt, width), lambda i: (i, 0)),
        out_shape=jax.ShapeDtypeStruct((t, width), F32),
        scratch_shapes=[pltpu.VMEM((tt, width), F32)],
        compiler_params=_params(("arbitrary",)),
    )(src, w, b)


def conv_bwd(src, cb, dy, w, b, *, width, act, name):
    t = src.shape[0]
    tt = min(ROW_TILE, t)
    n = t // tt

    def body(x_ref, xp_ref, dy_ref, w_ref, b_ref, dx_ref, dw_ref, db_ref, nxt_ref):
        i = pl.program_id(0)

        @pl.when(i == 0)
        def _():
            nxt_ref[...] = jnp.zeros_like(nxt_ref)
            dw_ref[...] = jnp.zeros_like(dw_ref)
            db_ref[...] = jnp.zeros_like(db_ref)

        x = x_ref[...]
        prev = jnp.where(i == n - 1, 0.0, xp_ref[...])
        rows = lax.broadcasted_iota(jnp.int32, x.shape, 0)
        wv = w_ref[...]
        xs = [x] + [_shift_down(x, prev, j, rows) for j in (1, 2, 3)]
        dpre = dy_ref[...]
        if act:
            pre = b_ref[...] + wv[3:4, :] * xs[0]
            for j in (1, 2, 3):
                pre = pre + wv[3 - j:4 - j, :] * xs[j]
            dpre = dpre * _dsilu(pre)
        nxt = nxt_ref[...]
        dx = wv[3:4, :] * dpre
        for j in (1, 2, 3):
            dx = dx + wv[3 - j:4 - j, :] * _shift_up(dpre, nxt, j, rows)
        dx_ref[...] = dx.astype(dx_ref.dtype)
        dw_ref[...] += jnp.concatenate([_colsum(dpre * xs[3 - kk]) for kk in range(4)], axis=0)
        db_ref[...] += _colsum(dpre)
        nxt_ref[...] = dpre

    return pl.pallas_call(
        body, name=name, grid=(n,),
        in_specs=[pl.BlockSpec((tt, width), lambda i: (n - 1 - i, cb)),
                  pl.BlockSpec((tt, width), lambda i: (jnp.maximum(n - 2 - i, 0), cb)),
                  pl.BlockSpec((tt, width), lambda i: (n - 1 - i, 0)),
                  pl.BlockSpec((4, width), lambda i: (0, 0)), pl.BlockSpec((1, width), lambda i: (0, 0))],
        out_specs=[pl.BlockSpec((tt, width), lambda i: (n - 1 - i, 0)),
                   pl.BlockSpec((4, width), lambda i: (0, 0)), pl.BlockSpec((1, width), lambda i: (0, 0))],
        out_shape=[jax.ShapeDtypeStruct((t, width), BF16), jax.ShapeDtypeStruct((4, width), F32),
                   jax.ShapeDtypeStruct((1, width), F32)],
        scratch_shapes=[pltpu.VMEM((tt, width), F32)],
        compiler_params=_params(("arbitrary",)),
    )(src, src, dy, w, b)


S5_CW = 256


def _cmul(ar, ai, br, bi):
    return ar * br - ai * bi, ar * bi + ai * br


def _scan8_complex(src_ref, dst_ref, lam_ref, st_ref, *, w, nb, reverse):
    rows = lax.broadcasted_iota(jnp.int32, (8, S5_CW), 0)
    b8 = lambda v: jnp.broadcast_to(v, (8, S5_CW))

    def shift(x, k):
        if reverse:
            return jnp.where(rows < 8 - k, pltpu.roll(x, 8 - k, 0), 0.0)
        return jnp.where(rows >= k, pltpu.roll(x, k, 0), 0.0)

    for c0 in range(0, w, S5_CW):
        re, im = pl.ds(c0, S5_CW), pl.ds(w + c0, S5_CW)
        pw = [(lam_ref[:, re], lam_ref[:, im])]
        for _ in range(7):
            pw.append(_cmul(*pw[-1], *pw[0]))
        pr, pi = b8(pw[7][0]), b8(pw[7][1])
        for j in range(7):
            sel = rows == (7 - j if reverse else j)
            pr, pi = jnp.where(sel, b8(pw[j][0]), pr), jnp.where(sel, b8(pw[j][1]), pi)
        steps = [(k, b8(pw[k - 1][0]), b8(pw[k - 1][1])) for k in (1, 2, 4)]
        edge = 0 if reverse else 7

        def blk(i, carry):
            hr, hi = carry
            base = pl.multiple_of((nb // 2 - 1 - i if reverse else i) * 16, 16)
            pend = []
            for off in ((8, 0) if reverse else (0, 8)):
                at = pl.ds(base + off, 8)
                xr, xi = src_ref[at, re], src_ref[at, im]
                for k, kr, ki in steps:
                    sr, si = shift(xr, k), shift(xi, k)
                    xr, xi = xr + kr * sr - ki * si, xi + kr * si + ki * sr
                pend.append((at, xr, xi))
            for at, xr, xi in pend:
                xr, xi = xr + pr * hr - pi * hi, xi + pr * hi + pi * hr
                dst_ref[at, re] = xr
                dst_ref[at, im] = xi
                hr, hi = b8(xr[edge:edge + 1, :]), b8(xi[edge:edge + 1, :])
            return hr, hi

        hr, hi = lax.fori_loop(0, nb // 2, blk, (st_ref[:, re], st_ref[:, im]))
        st_ref[:, re] = hr
        st_ref[:, im] = hi


def s5_fwd(proj, bcat, lam, ccat, *, name):
    t = proj.shape[0]
    tt = min(ROW_TILE, t)
    w2 = bcat.shape[1]

    def body(u_ref, b_ref, lam_ref, c_ref, h_ref, y_ref, bu_ref, st_ref):
        @pl.when(pl.program_id(0) == 0)
        def _():
            st_ref[...] = jnp.zeros_like(st_ref)

        bu_ref[...] = _dot(u_ref[...], b_ref[...])
        _scan8_complex(bu_ref, h_ref, lam_ref, st_ref, w=w2 // 2, nb=tt // 8, reverse=False)
        y_ref[...] = _dot(h_ref[...], c_ref[...])

    fixed = lambda a: pl.BlockSpec(a.shape, lambda i: (0, 0))
    return pl.pallas_call(
        body, name=name, grid=(t // tt,),
        in_specs=[pl.BlockSpec((tt, S5_WIDTH), lambda i: (i, P_U // S5_WIDTH)), fixed(bcat), fixed(lam), fixed(ccat)],
        out_specs=[pl.BlockSpec((tt, w2), lambda i: (i, 0)), pl.BlockSpec((tt, S5_WIDTH), lambda i: (i, 0))],
        out_shape=[jax.ShapeDtypeStruct((t, w2), F32), jax.ShapeDtypeStruct((t, S5_WIDTH), F32)],
        scratch_shapes=[pltpu.VMEM((tt, w2), F32), pltpu.VMEM((8, w2), F32)],
        compiler_params=_params(("arbitrary",)),
    )(proj, bcat, lam, ccat)


def s5_bwd(dylin, du_a, hs, proj, bcat, lam_adj, ccat, *, name):
    t = proj.shape[0]
    tt = min(ROW_TILE, t)
    n, w2 = t // tt, bcat.shape[1]
    w = w2 // 2

    def body(dy_ref, dua_ref, h_ref, hp_ref, u_ref, b_ref, lam_ref, c_ref,
             du_ref, dc_ref, db_ref, dar_ref, dai_ref, g_ref, st_ref):
        i = pl.program_id(0)

        @pl.when(i == 0)
        def _():
            for r in (st_ref, dc_ref, db_ref, dar_ref, dai_ref):
                r[...] = jnp.zeros_like(r)

        dy, h = dy_ref[...], h_ref[...]
        g_ref[...] = _dot(dy, c_ref[...], _NT)
        dc_ref[...] += _dot(h, dy, _TN)
        _scan8_complex(g_ref, g_ref, lam_ref, st_ref, w=w, nb=tt // 8, reverse=True)
        g = g_ref[...]
        du_ref[...] = (dua_ref[...] + _dot(g, b_ref[...], _NT)).astype(du_ref.dtype)
        db_ref[...] += _dot(u_ref[...], g, _TN)
        rows = lax.broadcasted_iota(jnp.int32, (tt, w2), 0)
        before = jnp.where(i == n - 1, 0.0, hp_ref[7:8, :])
        hprev = jnp.where(rows == 0, before, pltpu.roll(h, 1, 0))
        gr, gi, hr, hi = g[:, :w], g[:, w:], hprev[:, :w], hprev[:, w:]
        dar_ref[...] += _colsum(gr * hr + gi * hi)
        dai_ref[...] += _colsum(gi * hr - gr * hi)

    rev = lambda i: n - 1 - i
    row = lambda wd, cb=0: pl.BlockSpec((tt, wd), lambda i: (rev(i), cb))
    fixed = lambda shape: pl.BlockSpec(shape, lambda i: (0, 0))
    return pl.pallas_call(
        body, name=name, grid=(n,),
        in_specs=[row(S5_WIDTH), row(S5_WIDTH), row(w2),
                  pl.BlockSpec((8, w2), lambda i: (jnp.maximum(rev(i) * (tt // 8) - 1, 0), 0)),
                  row(S5_WIDTH, P_U // S5_WIDTH), fixed(bcat.shape), fixed(lam_adj.shape), fixed(ccat.shape)],
        out_specs=[row(S5_WIDTH), fixed(ccat.shape), fixed(bcat.shape), fixed((1, w)), fixed((1, w))],
        out_shape=[jax.ShapeDtypeStruct((t, S5_WIDTH), BF16), jax.ShapeDtypeStruct(ccat.shape, F32),
                   jax.ShapeDtypeStruct(bcat.shape, F32), jax.ShapeDtypeStruct((1, w), F32), jax.ShapeDtypeStruct((1, w), F32)],
        scratch_shapes=[pltpu.VMEM((tt, w2), F32), pltpu.VMEM((8, w2), F32)],
        compiler_params=_params(("arbitrary",)),
    )(dylin, du_a, hs, hs, proj, bcat, lam_adj, ccat)


def scan_real(a, b, *, reverse, name):
    t, w = b.shape
    tt = min(ROW_TILE, t)
    n, nb = t // tt, tt // 8

    def body(a_ref, b_ref, o_ref, st_ref):
        @pl.when(pl.program_id(0) == 0)
        def _():
            st_ref[...] = jnp.zeros_like(st_ref)

        rows = lax.broadcasted_iota(jnp.int32, (8, w), 0)

        def blk(i, h):
            base = pl.multiple_of((nb - 1 - i if reverse else i) * 8, 8)
            ta_, tb_ = a_ref[pl.ds(base, 8), :], b_ref[pl.ds(base, 8), :]
            out = jnp.zeros((8, w), F32)
            for j in (range(7, -1, -1) if reverse else range(8)):
                h = jnp.broadcast_to(ta_[j:j + 1, :], (8, w)) * h + jnp.broadcast_to(tb_[j:j + 1, :], (8, w))
                out = jnp.where(rows == j, h, out)
            o_ref[pl.ds(base, 8), :] = out
            return h

        st_ref[...] = lax.fori_loop(0, nb, blk, st_ref[...])

    idx = (lambda i: (n - 1 - i, 0)) if reverse else (lambda i: (i, 0))
    return pl.pallas_call(
        body, name=name, grid=(n,),
        in_specs=[pl.BlockSpec((tt, w), idx), pl.BlockSpec((tt, w), idx)],
        out_specs=pl.BlockSpec((tt, w), idx), out_shape=jax.ShapeDtypeStruct((t, w), F32),
        scratch_shapes=[pltpu.VMEM((8, w), F32)],
        compiler_params=_params(("arbitrary",)),
    )(a, b)


SSD_QQ = SSD_HEADS * SSD_CHUNK
SSD_GP = SSD_WIDTH // 2
SSD_GQ = SSD_QQ // 2


def _ssd_spread():
    h = jnp.arange(LANE)[:, None]
    spread_p = (jnp.arange(SSD_WIDTH)[None, :] // SSD_HEAD_DIM == h).astype(BF16)
    spread_q = (jnp.arange(SSD_QQ)[None, :] // SSD_CHUNK == h).astype(BF16)
    return spread_p, spread_q


def _ssd_prologue(dt_ref, prow_ref, sp_ref, sq_ref):
    q = SSD_CHUNK
    r = lax.broadcasted_iota(jnp.int32, (q, q), 0)
    c = lax.broadcasted_iota(jnp.int32, (q, q), 1)
    raw_c = dt_ref[...] + prow_ref[0:1, :]
    dt_c = _softplus(raw_c)
    a_r = -jnp.exp(prow_ref[1:2, :])
    cs_c = _dot_mask((r >= c).astype(F32), dt_c * a_r, mask_left=True, parts=3)
    both = _dot_mask(jnp.concatenate([dt_c, cs_c], axis=0), sp_ref[...], mask_left=False, parts=3)
    dt_x, cs_x = both[:q], both[q:]
    csx = _dot_mask(cs_c, sq_ref[...], mask_left=False, parts=3)
    rr = lax.broadcasted_iota(jnp.int32, (q, SSD_QQ), 0)
    ss = lax.broadcasted_iota(jnp.int32, (q, SSD_QQ), 1) & (q - 1)
    diag = rr == ss
    cs_row = _colsum(jnp.where(diag, csx, 0.0))
    lcat = jnp.exp(jnp.where(rr >= ss, csx - cs_row, -1e30))
    cl = cs_x[q - 1:q, :]
    return dict(raw_c=raw_c, dt_c=dt_c, a_r=a_r, dt_x=dt_x, cs_x=cs_x, lcat=lcat, diag=diag,
                ecs=jnp.exp(cs_x), wdec=jnp.exp(cl - cs_x), ecl=jnp.exp(cl), triu=(r <= c).astype(F32))


def _ssd_group(xbc_ref, g, lcat, xdt):
    ns, q = SSD_STATE, SSD_CHUNK
    bm = xbc_ref[:, pl.ds(SSD_WIDTH + g * ns, ns)]
    cm = xbc_ref[:, pl.ds(SSD_WIDTH + 2 * ns + g * ns, ns)]
    cb = _dot(cm, bm, _NT)
    lg = lcat[:, g * SSD_GQ:(g + 1) * SSD_GQ]
    wcat = jnp.concatenate([cb] * 4, axis=1) * lg
    head = lax.broadcasted_iota(jnp.int32, (1, SSD_GP), 1) // SSD_HEAD_DIM
    xg = xdt[:, g * SSD_GP:(g + 1) * SSD_GP]
    xbd = jnp.concatenate([jnp.where(head == j, xg, 0.0) for j in range(4)], axis=0)
    return bm, cm, lg, wcat, xbd, head


def _ssd_conv_taps(x_ref, halo_ref, first):
    x = x_ref[...]
    halo = jnp.where(first, 0.0, halo_ref[...])
    rows8 = lax.broadcasted_iota(jnp.int32, halo.shape, 0)
    taps = [x]
    for j in (1, 2, 3):
        r = pltpu.roll(x, j, 0)
        top = jnp.where(rows8 < j, pltpu.roll(halo, j, 0), r[0:8])
        taps.append(jnp.concatenate([top, r[8:]], axis=0))
    return taps


def _ssd_conv_pre(taps, cw_ref, cb_ref):
    wv = cw_ref[...]
    pre = cb_ref[...] + wv[3:4, :] * taps[0]
    for j in (1, 2, 3):
        pre = pre + wv[3 - j:4 - j, :] * taps[j]
    return pre


def _ssd_gate(yraw, z, nw):
    yg = yraw * _silu(z)
    r = lax.rsqrt(jnp.mean(yg * yg, axis=1, keepdims=True) + LN_EPS)
    return yg, r


def _ssd_specs(q, idx):
    return [pl.BlockSpec((q, SSD_XBC), lambda i: (idx(i), P_XBC // SSD_XBC)),
            pl.BlockSpec((8, SSD_XBC), lambda i: (jnp.maximum(idx(i) * (q // 8) - 1, 0), P_XBC // SSD_XBC)),
            pl.BlockSpec((q, SSD_WIDTH), lambda i: (idx(i), P_Z // SSD_WIDTH)),
            pl.BlockSpec((q, LANE), lambda i: (idx(i), P_DT // LANE)),
            pl.BlockSpec((4, SSD_XBC), lambda i: (0, 0)), pl.BlockSpec((1, SSD_XBC), lambda i: (0, 0)),
            pl.BlockSpec((8, LANE), lambda i: (0, 0)), pl.BlockSpec((1, SSD_WIDTH), lambda i: (0, 0)),
            pl.BlockSpec((1, SSD_WIDTH), lambda i: (0, 0)),
            pl.BlockSpec((LANE, SSD_WIDTH), lambda i: (0, 0)), pl.BlockSpec((LANE, SSD_QQ), lambda i: (0, 0))]


def ssd_fwd(proj, cw, cb, prow, d_x, nw, *, name):
    t = proj.shape[0]
    q, ns = SSD_CHUNK, SSD_STATE
    nc = t // q
    spread_p, spread_q = _ssd_spread()

    def body(x_ref, halo_ref, z_ref, dt_ref, cw_ref, cb_ref, prow_ref, dx_ref, nw_ref, sp_ref, sq_ref,
             y_ref, yraw_ref, sall_ref, s_ref, xbc_ref):
        @pl.when(pl.program_id(0) == 0)
        def _():
            s_ref[...] = jnp.zeros_like(s_ref)

        sall_ref[0] = s_ref[...]
        xbc_ref[...] = _silu(_ssd_conv_pre(_ssd_conv_taps(x_ref, halo_ref, pl.program_id(0) == 0), cw_ref, cb_ref))
        pr = _ssd_prologue(dt_ref, prow_ref, sp_ref, sq_ref)
        xs = xbc_ref[:, pl.ds(0, SSD_WIDTH)]
        xdt = xs * pr["dt_x"]
        xw = xdt * pr["wdec"]
        ys = []
        for g in range(2):
            gp = slice(g * SSD_GP, (g + 1) * SSD_GP)
            bm, cm, lg, wcat, xbd, head = _ssd_group(xbc_ref, g, pr["lcat"], xdt)
            st = s_ref[:, gp]
            ys.append(_dot(wcat, xbd) + pr["ecs"][:, gp] * _dot(cm, st) + xs[:, gp] * dx_ref[:, gp])
            s_ref[:, gp] = pr["ecl"][:, gp] * st + _dot(bm, xw[:, gp], _TN)
        yraw = jnp.concatenate(ys, axis=1)
        yraw_ref[...] = yraw
        yg, r = _ssd_gate(yraw, z_ref[...], nw_ref[...])
        y_ref[...] = (yg * r * nw_ref[...]).astype(y_ref.dtype)

    row = pl.BlockSpec((q, SSD_WIDTH), lambda i: (i, 0))
    return pl.pallas_call(
        body, name=name, grid=(nc,),
        in_specs=_ssd_specs(q, lambda i: i),
        out_specs=[row, row, pl.BlockSpec((1, ns, SSD_WIDTH), lambda i: (i, 0, 0))],
        out_shape=[jax.ShapeDtypeStruct((t, SSD_WIDTH), BF16), jax.ShapeDtypeStruct((t, SSD_WIDTH), F32),
                   jax.ShapeDtypeStruct((nc, ns, SSD_WIDTH), F32)],
        scratch_shapes=[pltpu.VMEM((ns, SSD_WIDTH), F32), pltpu.VMEM((q, SSD_XBC), F32)],
        compiler_params=_params(("arbitrary",)),
    )(proj, proj, proj, proj, cw, cb, prow, d_x, nw, spread_p, spread_q)


def ssd_bwd(proj, cw, cb, prow, d_x, nw, yraw, sall, dout, *, name):
    t = proj.shape[0]
    q, ns = SSD_CHUNK, SSD_STATE
    nc = t // q
    spread_p, spread_q = _ssd_spread()

    def body(x_ref, halo_ref, z_ref, dt_ref, cw_ref, cb_ref, prow_ref, dx_ref, nw_ref, sp_ref, sq_ref, yraw_ref, sall_ref, dout_ref,
             dxraw_ref, dz_ref, ddt_ref, dprm_ref, ddx_ref, dnw_ref, dcw_ref, dcb_ref, ds_ref, xbc_ref, dxbc_ref, nxt_ref):
        @pl.when(pl.program_id(0) == 0)
        def _():
            for r in (ds_ref, dprm_ref, ddx_ref, dnw_ref, dcw_ref, dcb_ref, nxt_ref):
                r[...] = jnp.zeros_like(r)

        taps = _ssd_conv_taps(x_ref, halo_ref, pl.program_id(0) == nc - 1)
        conv_pre = _ssd_conv_pre(taps, cw_ref, cb_ref)
        xbc_ref[...] = _silu(conv_pre)

        yraw, z, nwv, dout = yraw_ref[...], z_ref[...], nw_ref[...], dout_ref[...]
        yg, r = _ssd_gate(yraw, z, nwv)
        dnw_ref[...] += _colsum(dout * yg * r)
        dyn = dout * nwv
        dyg = r * dyn - yg * (r * r * r) * jnp.mean(dyn * yg, axis=1, keepdims=True)
        dy = dyg * _silu(z)
        dz_ref[...] = (dyg * yraw * _dsilu(z)).astype(dz_ref.dtype)

        pr = _ssd_prologue(dt_ref, prow_ref, sp_ref, sq_ref)
        xs = xbc_ref[:, pl.ds(0, SSD_WIDTH)]
        xdt = xs * pr["dt_x"]
        wdec, ecl = pr["wdec"], pr["ecl"]
        xw = xdt * wdec
        dzm_all = pr["ecs"] * dy
        last = (lax.broadcasted_iota(jnp.int32, (q, 1), 0) == q - 1).astype(F32)
        dxs, dcsxs, es = [], [], []
        for g in range(2):
            gp = slice(g * SSD_GP, (g + 1) * SSD_GP)
            bm, cm, lg, wcat, xbd, head = _ssd_group(xbc_ref, g, pr["lcat"], xdt)
            dyg_ = dy[:, gp]
            dwcat = _dot(dyg_, xbd, _NT)
            dxbd = _dot(wcat, dyg_, _TN)
            dxg = sum(jnp.where(head == j, dxbd[j * q:(j + 1) * q], 0.0) for j in range(4))
            es.append(dwcat * wcat)
            dmm = dwcat * lg
            dm = dmm[:, 0:q] + dmm[:, q:2 * q] + dmm[:, 2 * q:3 * q] + dmm[:, 3 * q:4 * q]
            dcm = _dot(dm, bm)
            dbm = _dot(dm, cm, _TN)
            st = sall_ref[0, :, gp]
            zmat = _dot(cm, st)
            dzm = dzm_all[:, gp]
            dcm = dcm + _dot(dzm, st, _NT)
            dst = _dot(cm, dzm, _TN)
            dcsx = dzm * zmat
            dsn = ds_ref[:, gp]
            dst = dst + ecl[:, gp] * dsn
            dclx = _colsum(dsn * st) * ecl[:, gp]
            dxw = _dot(bm, dsn)
            dbm = dbm + _dot(xw[:, gp], dsn, _NT)
            dxg = dxg + wdec[:, gp] * dxw
            tw = dxw * xdt[:, gp] * wdec[:, gp]
            dclx = dclx + _colsum(tw)
            dcsxs.append(dcsx - tw + last * dclx)
            ds_ref[:, gp] = dst
            dxs.append(dxg)
            dxbc_ref[:, pl.ds(SSD_WIDTH + g * ns, ns)] = dbm
            dxbc_ref[:, pl.ds(SSD_WIDTH + 2 * ns + g * ns, ns)] = dcm
        dx = jnp.concatenate(dxs, axis=1)
        dxbc_ref[:, pl.ds(0, SSD_WIDTH)] = dx * pr["dt_x"] + dy * dx_ref[...]
        ddx_ref[...] += _colsum(dy * xs)
        red = _dot_mask(jnp.concatenate([jnp.concatenate(dcsxs, axis=1), dx * xs], axis=0), sp_ref[...], _NT,
                        mask_left=False, parts=2)
        e_all = jnp.concatenate(es, axis=1)
        e_red = _dot_mask(e_all - jnp.where(pr["diag"], _colsum(e_all), 0.0), sq_ref[...], _NT, mask_left=False, parts=2)
        dadt = _dot_mask(pr["triu"], red[:q] + e_red, mask_left=True, parts=2)
        draw = (red[q:] + dadt * pr["a_r"]) * _sigmoid(pr["raw_c"])
        ddt_ref[...] = draw.astype(ddt_ref.dtype)
        zero = jnp.zeros((6, LANE), F32)
        dprm_ref[...] += jnp.concatenate([_colsum(draw), _colsum(dadt * pr["dt_c"]) * pr["a_r"], zero], axis=0)
        dpre = dxbc_ref[...] * _dsilu(conv_pre)
        rows8 = lax.broadcasted_iota(jnp.int32, (8, SSD_XBC), 0)
        wv = cw_ref[...]
        dxr = wv[3:4, :] * dpre
        for j in (1, 2, 3):
            r = pltpu.roll(dpre, q - j, 0)
            bottom = jnp.where(rows8 >= 8 - j, pltpu.roll(nxt_ref[...], 8 - j, 0), r[q - 8:q])
            dxr = dxr + wv[3 - j:4 - j, :] * jnp.concatenate([r[:q - 8], bottom], axis=0)
        dxraw_ref[...] = dxr.astype(dxraw_ref.dtype)
        dcw_ref[...] += jnp.concatenate([_colsum(dpre * taps[3 - kk]) for kk in range(4)], axis=0)
        dcb_ref[...] += _colsum(dpre)
        nxt_ref[...] = dpre[0:8]

    rev = lambda i: nc - 1 - i
    row = lambda w: pl.BlockSpec((q, w), lambda i: (rev(i), 0))
    fixed = lambda shape: pl.BlockSpec(shape, lambda i: (0, 0))
    return pl.pallas_call(
        body, name=name, grid=(nc,),
        in_specs=_ssd_specs(q, rev) + [row(SSD_WIDTH), pl.BlockSpec((1, ns, SSD_WIDTH), lambda i: (rev(i), 0, 0)),
                                       row(SSD_WIDTH)],
        out_specs=[row(SSD_XBC), row(SSD_WIDTH), row(LANE), fixed((8, LANE)), fixed((1, SSD_WIDTH)), fixed((1, SSD_WIDTH)),
                   fixed((4, SSD_XBC)), fixed((1, SSD_XBC))],
        out_shape=[jax.ShapeDtypeStruct((t, SSD_XBC), BF16), jax.ShapeDtypeStruct((t, SSD_WIDTH), BF16),
                   jax.ShapeDtypeStruct((t, LANE), BF16), jax.ShapeDtypeStruct((8, LANE), F32),
                   jax.ShapeDtypeStruct((1, SSD_WIDTH), F32), jax.ShapeDtypeStruct((1, SSD_WIDTH), F32),
                   jax.ShapeDtypeStruct((4, SSD_XBC), F32), jax.ShapeDtypeStruct((1, SSD_XBC), F32)],
        scratch_shapes=[pltpu.VMEM((ns, SSD_WIDTH), F32), pltpu.VMEM((q, SSD_XBC), F32), pltpu.VMEM((q, SSD_XBC), F32),
                        pltpu.VMEM((8, SSD_XBC), F32)],
        compiler_params=_params(("arbitrary",)),
    )(proj, proj, proj, proj, cw, cb, prow, d_x, nw, spread_p, spread_q, yraw, sall, dout)


def _me():
    return lax.axis_index("x"), lax.axis_index("y"), lax.axis_index("c")


_ANY = pl.BlockSpec(memory_space=pl.ANY)
_MESH = pl.DeviceIdType.MESH


_AG_SEMS = [pltpu.SemaphoreType.DMA((7,)), pltpu.SemaphoreType.DMA((7,)), pltpu.SemaphoreType.DMA(())]
_RS_SEMS = [pltpu.SemaphoreType.DMA((3,)), pltpu.SemaphoreType.DMA((3,)), pltpu.SemaphoreType.DMA(())]


def _ag_phases(src, dst, send_sems, recv_sems, local_sem):
    x, y, c = _me()
    me, sibling = (x, y, c), (x, y, 1 - c)
    chips = [(1 - x, y), (x, 1 - y), (1 - x, 1 - y)]

    def slot(px, py, pc):
        return dst.at[4 * px + 2 * py + pc]

    def copy(kk, blk, to, from_src=False):
        return pltpu.make_async_remote_copy(
            src_ref=src if from_src else slot(*blk), dst_ref=slot(*blk),
            send_sem=send_sems.at[kk], recv_sem=recv_sems.at[kk], device_id=to, device_id_type=_MESH)

    mine = lambda: pltpu.make_async_copy(src, slot(*me), local_sem)
    first = lambda: [copy(0, me, sibling, True)] + [copy(1 + j, me, (*chip, c), True) for j, chip in enumerate(chips)]
    passed = lambda j: copy(4 + j, (*chips[j], c), sibling)

    def start():
        mine().start()
        for cp in first():
            cp.start()

    def forward():
        for j, chip in enumerate(chips):
            copy(1 + j, (*chip, c), me).wait_recv()
            passed(j).start()

    def finish():
        copy(0, sibling, me).wait_recv()
        for j, chip in enumerate(chips):
            copy(4 + j, (*chip, 1 - c), me).wait_recv()
        for cp in first() + [passed(j) for j in range(3)]:
            cp.wait_send()
        mine().wait()

    return start, forward, finish


def _rs_chip_phases(src, dst, send_sems, recv_sems, local_sem, rows=None):
    x, y, c = _me()
    q_me = 2 * x + y
    pick = (lambda q: src.at[q]) if rows is None else (lambda q: src.at[q, rows])
    local = lambda: pltpu.make_async_copy(pick(q_me), dst.at[q_me], local_sem)
    copies = lambda: [pltpu.make_async_remote_copy(src_ref=pick(2 * px + py), dst_ref=dst.at[q_me], send_sem=send_sems.at[j],
                                                   recv_sem=recv_sems.at[j], device_id=(px, py, c), device_id_type=_MESH)
                      for j, (px, py) in enumerate([(1 - x, y), (x, 1 - y), (1 - x, 1 - y)])]

    def start():
        local().start()
        for cp in copies():
            cp.start()

    def finish():
        for cp in copies():
            cp.wait()
        local().wait()

    return start, finish


def all_gather(block, *, name):
    def body(src, dst, send_sems, recv_sems, local_sem):
        for phase in _ag_phases(src, dst, send_sems, recv_sems, local_sem):
            phase()

    return pl.pallas_call(
        body, name=name, in_specs=[_ANY], out_specs=_ANY,
        out_shape=jax.ShapeDtypeStruct((N_DEV,) + block.shape, block.dtype), scratch_shapes=list(_AG_SEMS),
    )(block)


RS_PIECES = 4


def rs_sibling_exchange(halves, *, name):
    _, nq, r, l = halves.shape
    rows = r // RS_PIECES
    assert r % RS_PIECES == 0 and rows % 16 == 0

    def body(src, dst, send_sems, recv_sems):
        x, y, c = _me()
        copies = []
        for q in range(nq):
            for i in range(RS_PIECES):
                kk = q * RS_PIECES + i
                cp = pltpu.make_async_remote_copy(
                    src_ref=src.at[1 - c, q, pl.ds(i * rows, rows)], dst_ref=dst.at[q, pl.ds(i * rows, rows)],
                    send_sem=send_sems.at[kk], recv_sem=recv_sems.at[kk], device_id=(x, y, 1 - c), device_id_type=_MESH)
                cp.start()
                copies.append(cp)
        for cp in copies:
            cp.wait()

    n_copies = nq * RS_PIECES
    return pl.pallas_call(
        body, name=name, in_specs=[_ANY], out_specs=_ANY,
        out_shape=jax.ShapeDtypeStruct((nq, r, l), halves.dtype),
        scratch_shapes=[pltpu.SemaphoreType.DMA((n_copies,)), pltpu.SemaphoreType.DMA((n_copies,))],
    )(halves)


def pair_sum_bf16(halves, theirs, *, name, tt=128):
    _, nq, r, wd = halves.shape
    tt = min(tt, r)
    parity = lax.axis_index("c").astype(jnp.int32).reshape(1)

    def body(c_ref, own_ref, sib_ref, o_ref):
        o_ref[...] = (own_ref[...] + sib_ref[...]).astype(BF16)

    return pl.pallas_call(
        body, name=name,
        grid_spec=pltpu.PrefetchScalarGridSpec(
            num_scalar_prefetch=1, grid=(nq, r // tt),
            in_specs=[pl.BlockSpec((None, None, tt, wd), lambda q, i, c: (c[0], q, i, 0)),
                      pl.BlockSpec((None, tt, wd), lambda q, i, c: (q, i, 0))],
            out_specs=pl.BlockSpec((None, tt, wd), lambda q, i, c: (q, i, 0))),
        out_shape=jax.ShapeDtypeStruct((nq, r, wd), BF16),
        compiler_params=_params(("parallel", "parallel")),
    )(parity, halves, theirs)


def rs_chip_exchange(part, *, name):
    def body(src, dst, send_sems, recv_sems, local_sem):
        for phase in _rs_chip_phases(src, dst, send_sems, recv_sems, local_sem):
            phase()

    return pl.pallas_call(
        body, name=name, in_specs=[_ANY], out_specs=_ANY,
        out_shape=jax.ShapeDtypeStruct(part.shape, part.dtype), scratch_shapes=list(_RS_SEMS),
    )(part)


def adamw(slabs, w, m, v, *, name, tt):
    ns, (r, wd) = slabs.shape[0], w.shape
    tt = min(tt, r)
    assert r % tt == 0

    def body(s_ref, w_ref, m_ref, v_ref, g_ref, d_ref, nm_ref, nv_ref):
        g = s_ref[0].astype(F32)
        for kdev in range(1, ns):
            g = g + s_ref[kdev].astype(F32)
        wv = w_ref[...]
        nm = ADAM_B1 * m_ref[...] + (1.0 - ADAM_B1) * g
        nv = ADAM_B2 * v_ref[...] + (1.0 - ADAM_B2) * (g * g)
        m_hat = nm / (1.0 - ADAM_B1 ** ADAM_STEP)
        v_hat = nv / (1.0 - ADAM_B2 ** ADAM_STEP)
        g_ref[...] = g
        d_ref[...] = -ADAM_LR * (m_hat / (jnp.sqrt(v_hat) + ADAM_EPS) + ADAM_WD * wv)
        nm_ref[...] = nm
        nv_ref[...] = nv

    spec = pl.BlockSpec((tt, wd), lambda i: (i, 0))
    return pl.pallas_call(
        body, name=name, grid=(r // tt,),
        in_specs=[pl.BlockSpec((ns, tt, wd), lambda i: (0, i, 0)), spec, spec, spec],
        out_specs=[spec] * 4, out_shape=[jax.ShapeDtypeStruct((r, wd), F32)] * 4,
        compiler_params=_params(("parallel",)),
    )(slabs, w, m, v)


WIDE = 1024
BIG = [("w_in", True, 289), ("w_out", False, 128), ("xa_wq", False, 128), ("xa_wk", False, 128), ("xa_wv", False, 128),
       ("xa_wo", False, 128), ("mlp_w2", False, 512), ("mlp_w1", True, 512)]
TINY = [("ssd_conv_w", 2), ("s5_glu_w", 1), ("rg_conv_w", 2)]
KEEP_F32 = ("ssd_conv_w", "rg_conv_w")
TINY_ROWS = 32
SHARDED = [name for name, _, _ in BIG] + [name for name, _ in TINY]
SMALL = ["ssd_conv_b", "ssd_dt_bias", "ssd_a_log", "ssd_d", "ssd_norm_w", "s5_lam_re", "s5_lam_im",
         "s5_log_step", "s5_b_re", "s5_b_im", "s5_c_re", "s5_c_im", "s5_d", "s5_glu_b", "rg_conv_b",
         "rg_wa", "rg_ba", "rg_wx", "rg_bx", "rg_lambda", "ln1_g", "ln1_b", "ln2_g", "ln2_b", "ln3_g", "ln3_b"]
WEIGHTS = ['w_in', 'w_out', 'ssd_conv_w', 'ssd_conv_b', 'ssd_dt_bias', 'ssd_a_log', 'ssd_d', 'ssd_norm_w',
           's5_lam_re', 's5_lam_im', 's5_log_step', 's5_b_re', 's5_b_im', 's5_c_re', 's5_c_im', 's5_d',
           's5_glu_w', 's5_glu_b', 'rg_conv_w', 'rg_conv_b', 'rg_wa', 'rg_ba', 'rg_wx', 'rg_bx', 'rg_lambda',
           'ln1_g', 'ln1_b', 'xa_wq', 'xa_wk', 'xa_wv', 'xa_wo', 'ln2_g', 'ln2_b', 'mlp_w1', 'mlp_w2',
           'ln3_g', 'ln3_b']


def _pad16(rows):
    return -(-rows // 16) * 16


def _pack_rows(flat, mult):
    n = flat.shape[-1]
    r = -(-n // (LANE * mult)) * mult
    pad = [(0, 0)] * (flat.ndim - 1) + [(0, r * LANE - n)]
    return jnp.pad(flat, pad).reshape(flat.shape[:-1] + (r, LANE))


def _unpack(packed, shapes):
    lead = packed.shape[:-2]
    flat = packed.reshape(lead + (-1,))
    out, off = [], 0
    for s in shapes:
        n = math.prod(s)
        out.append(flat[..., off:off + n].reshape(lead + tuple(s)))
        off += n
    return out


PACK_ROWS = 2048


def _tiny_block(flat):
    pad = [(0, 0)] * (flat.ndim - 1) + [(0, TINY_ROWS * WIDE - flat.shape[-1])]
    return jnp.pad(flat, pad).reshape(flat.shape[:-1] + (TINY_ROWS, WIDE))


def _pack_layer(big, tiny_flat=None):
    blocks, used = [], 0
    some = big[BIG[0][0]]

    def zeros(rows):
        return jnp.zeros(some.shape[:-2] + (rows, WIDE), some.dtype)

    for name, _, rows in BIG:
        blocks.append(jnp.pad(big[name], [(0, 0)] * (some.ndim - 2) + [(0, _pad16(rows) - rows), (0, 0)]))
        used += _pad16(rows)
    if tiny_flat is not None:
        blocks.append(_tiny_block(tiny_flat))
        used += TINY_ROWS
    return jnp.concatenate(blocks + [zeros(PACK_ROWS - used)], axis=-2)


def _unpack_layer(packed):
    big, off = {}, 0
    for name, _, rows in BIG:
        big[name] = packed[..., off:off + rows, :]
        off += _pad16(rows)
    return big, packed[..., off:off + TINY_ROWS, :].reshape(packed.shape[:-2] + (TINY_ROWS * WIDE,))


def _split_flat(flat, shapes):
    out, off = [], 0
    for s in shapes:
        n = math.prod(s)
        out.append(flat[..., off:off + n].reshape(flat.shape[:-1] + tuple(s)))
        off += n
    return out


def _to_full(gathered, axis):
    g = jnp.moveaxis(gathered, 0, axis)
    s = g.shape
    return g.reshape(s[:axis] + (s[axis] * s[axis + 1],) + s[axis + 2:])


def _to_slabs(full, axis):
    s = full.shape
    g = full.reshape(s[:axis] + (N_DEV, s[axis] // N_DEV) + s[axis + 1:])
    return jnp.moveaxis(g, axis, 0)


def _blockdiag(w):
    h, i, j = w.shape
    eye = jnp.eye(h, dtype=w.dtype)
    return (w[:, :, None, :] * eye[:, None, :, None]).reshape(h * i, h * j)


def _blockdiag_extract(m, h):
    i, j = m.shape[0] // h, m.shape[1] // h
    eye = jnp.eye(h, dtype=m.dtype)
    return (m.reshape(h, i, h, j) * eye[:, None, :, None]).sum(axis=2)


def _s5_disc(lr, li, ls, bre, bim):
    step = jnp.exp(ls)[:, None]
    er = jnp.exp(lr * step)
    ar, ai = er * jnp.cos(li * step), er * jnp.sin(li * step)
    nr, ni, den = ar - 1.0, ai, lr * lr + li * li
    qr, qi = (nr * lr + ni * li) / den, (ni * lr - nr * li) / den
    bbr = qr[..., None] * bre - qi[..., None] * bim
    bbi = qr[..., None] * bim + qi[..., None] * bre
    return ar, ai, bbr, bbi


def _row(v, width=None):
    v = v.reshape(1, -1)
    if width is not None and v.shape[1] < width:
        v = jnp.pad(v, ((0, 0), (0, width - v.shape[1])))
    return v


def _relu2(a):
    r = jnp.maximum(a, 0.0)
    return r * r


def _add_alpha(acc, d):
    return acc + ALPHA * d


def _shift_rows_down(x):
    return jnp.concatenate([jnp.zeros((1, x.shape[1]), x.dtype), x[:-1]], axis=0)


def _shift_rows_up(x):
    return jnp.concatenate([x[1:], jnp.zeros((1, x.shape[1]), x.dtype)], axis=0)


def _layer_params(full, small, l):
    p = {}
    w_in = full["w_in"]
    z, xbc, dt, u, xr, g = w_in[0:512], w_in[512:1536], w_in[1536:1544], w_in[1544:1800], w_in[1800:2056], w_in[2056:2312]
    p["w_inp"] = jnp.concatenate([xbc, z, u, xr, g, dt, jnp.zeros((D_INP - P_DT - 8, D_MODEL), w_in.dtype)], axis=0)
    for k_ in ("w_out", "xa_wq", "xa_wk", "xa_wv", "xa_wo", "mlp_w1", "mlp_w2"):
        p[k_] = full[k_]
    p["s5_glu_w"] = full["s5_glu_w"][l]
    p["ssd_cw"], p["ssd_cb"] = full["ssd_conv_w"][l], _row(small["ssd_conv_b"][l])
    dtb, alog, dsk = small["ssd_dt_bias"][l], small["ssd_a_log"][l], small["ssd_d"][l]
    p["prow"] = jnp.concatenate([_row(dtb, LANE), _row(alog, LANE), jnp.zeros((6, LANE), F32)], axis=0)
    p["ssd_dx"] = _row(jnp.repeat(dsk, SSD_HEAD_DIM))
    p["ssd_nw"] = _row(small["ssd_norm_w"][l])
    s5_in = (small["s5_lam_re"][l], small["s5_lam_im"][l], small["s5_log_step"][l], small["s5_b_re"][l], small["s5_b_im"][l])
    (ar, ai, bbr, bbi), p["s5_vjp"] = jax.vjp(_s5_disc, *s5_in)
    p["lam_fwd"] = jnp.concatenate([_row(ar), _row(ai)], axis=1)
    p["lam_adj"] = jnp.concatenate([_row(ar), _row(-ai)], axis=1)
    p["bcat"] = jnp.concatenate([_blockdiag(jnp.swapaxes(bbr, 1, 2)), _blockdiag(jnp.swapaxes(bbi, 1, 2))], axis=1)
    p["ccat"] = jnp.concatenate([_blockdiag(jnp.swapaxes(small["s5_c_re"][l], 1, 2)),
                                 -_blockdiag(jnp.swapaxes(small["s5_c_im"][l], 1, 2))], axis=0)
    p["s5_d"], p["s5_glu_b"] = _row(small["s5_d"][l]), _row(small["s5_glu_b"][l])
    p["rg_cw"], p["rg_cb"] = full["rg_conv_w"][l], _row(small["rg_conv_b"][l])
    p["rg_wa"], p["rg_wx"] = _blockdiag(small["rg_wa"][l]), _blockdiag(small["rg_wx"][l])
    p["rg_ba"], p["rg_bx"], p["rg_lam"] = _row(small["rg_ba"][l]), _row(small["rg_bx"][l]), _row(small["rg_lambda"][l])
    for i in (1, 2, 3):
        p[f"g{i}"], p[f"b{i}"] = _row(small[f"ln{i}_g"][l]), _row(small[f"ln{i}_b"][l])
    return p


def _take_side(res, n_out, got):
    res = res if isinstance(res, (list, tuple)) else (res,)
    got.extend(res[n_out:])
    return res[0] if n_out == 1 else res[:n_out]


def _layer_fwd(h0, mem, p, sides={}):
    t = h0.shape[0]
    s = {"h0": h0}
    got = []
    proj = _take_side(mm(h0, p["w_inp"], tb=True, name="in_proj", side=sides.get("in_proj")), 1, got)
    y_ssd, yraw, sall = ssd_fwd(proj, p["ssd_cw"], p["ssd_cb"], p["prow"], p["ssd_dx"], p["ssd_nw"], name="ssd_fwd")
    hs5, ylin = s5_fwd(proj, p["bcat"], p["lam_fwd"], p["ccat"], name="s5_fwd")
    (y_s5,), _ = rowk(_s5_post_fwd_fn, [(ylin, S5_WIDTH, 0), (proj, S5_WIDTH, P_U // S5_WIDTH)],
                      [p["s5_d"], p["s5_glu_w"], p["s5_glu_b"]], [S5_WIDTH], [], rows=t, name="s5_post_fwd", out_dtypes=[BF16])
    xc = conv_fwd(proj, P_XR // RG_WIDTH, p["rg_cw"], p["rg_cb"], width=RG_WIDTH, act=False, name="rg_conv_fwd")
    rg_full = [p["rg_wa"], p["rg_wx"], p["rg_ba"], p["rg_bx"], p["rg_lam"]]
    (a_rg, b_rg), _ = rowk(_rg_pre_fwd_fn, [(xc, RG_WIDTH, 0)], rg_full, [RG_WIDTH, RG_WIDTH], [], rows=t, name="rg_pre_fwd")
    h_rg = scan_real(a_rg, b_rg, reverse=False, name="rg_scan_fwd")
    (y_rg,), _ = rowk(_rg_out_fwd_fn, [(h_rg, RG_WIDTH, 0), (proj, RG_WIDTH, P_G // RG_WIDTH)], [], [RG_WIDTH], [],
                      rows=t, name="rg_out_fwd", out_dtypes=[BF16])
    ycat = jnp.concatenate([y_ssd, y_s5, y_rg], axis=1)
    pre1, h1 = mm_ln(ycat, p["w_out"], h0, p["g1"], p["b1"], name="out_proj")
    q = mm(h1, p["xa_wq"], name="xa_q", out_dtype=BF16)
    k = mm(mem, p["xa_wk"], name="xa_kv")
    v = mm(mem, p["xa_wv"], name="xa_kv")
    (o,), _ = rowk(_attn_fwd_fn, [(q, D_MODEL, 0)], [k, v], [D_MODEL], [], rows=t, name="xa_fwd", out_dtypes=[BF16])
    pre2, h2 = mm_ln(o, p["xa_wo"], h1, p["g2"], p["b2"], name="xa_o")
    a_mlp = _take_side(mm(h2, p["mlp_w1"], tb=True, name="mlp_up", side=sides.get("mlp_up")), 1, got)
    pre3, h3 = _take_side(mm_ln(a_mlp, p["mlp_w2"], h2, p["g3"], p["b3"], fa=_relu2, name="mlp_down",
                                side=sides.get("mlp_down")), 2, got)
    s.update(proj=proj, yraw=yraw, sall=sall, hs5=hs5, ylin=ylin, xc=xc, a_rg=a_rg, h_rg=h_rg,
             ycat=ycat, pre1=pre1, h1=h1, q=q, k=k, v=v, o=o, pre2=pre2, h2=h2, a_mlp=a_mlp, pre3=pre3)
    return h3, s, got


def _layer_bwd(dh3, mem, p, s, l, gfull, gsmall, sides={}):
    t = dh3.shape[0]
    proj = s["proj"]
    dpre3, dg3, db3 = ln_bwd(s["pre3"], dh3, p["g3"], name="ln_bwd")
    got = []
    da = _take_side(mm(dpre3, p["mlp_w2"], tb=True, o_extra=(s["a_mlp"],), fo=lambda acc, a: acc * 2.0 * jnp.maximum(a, 0.0),
                       name="mlp_da", out_dtype=BF16, side=sides.get("mlp_da")), 1, got)
    gfull["mlp_w2"][l] = mm(s["a_mlp"], dpre3, ta=True, fa=_relu2, name="mlp_dw2")
    gfull["mlp_w1"][l] = mm(da, s["h2"], ta=True, name="mlp_dw1")
    dh2 = _take_side(mm(da, p["mlp_w1"], o_extra=(dpre3,), fo=_add_alpha, name="mlp_dx", side=sides.get("mlp_dx")), 1, got)
    dpre2, dg2, db2 = ln_bwd(s["pre2"], dh2, p["g2"], name="ln_bwd")
    do = _take_side(mm(dpre2, p["xa_wo"], tb=True, name="xa_do", out_dtype=BF16, side=sides.get("xa_do")), 1, got)
    gfull["xa_wo"][l] = mm(s["o"], dpre2, ta=True, name="dw_sq")
    (dq,), (dk, dv) = rowk(_attn_bwd_fn, [(s["q"], D_MODEL, 0), (do, D_MODEL, 0)], [s["k"], s["v"]], [D_MODEL],
                           [(256, D_MODEL), (256, D_MODEL)], rows=t, name="xa_bwd", out_dtypes=[BF16])
    gfull["xa_wq"][l] = mm(s["h1"], dq, ta=True, name="dw_sq")
    gfull["xa_wk"][l] = mm(mem, dk, ta=True, name="dw_kv")
    gfull["xa_wv"][l] = mm(mem, dv, ta=True, name="dw_kv")
    dh1 = mm(dq, p["xa_wq"], tb=True, o_extra=(dpre2,), fo=_add_alpha, name="dx_sq")
    dpre1, dg1, db1 = ln_bwd(s["pre1"], dh1, p["g1"], name="ln_bwd")
    dycat = mm(dpre1, p["w_out"], tb=True, name="xa_do")
    gfull["w_out"][l] = mm(s["ycat"], dpre1, ta=True, name="dw_sq")
    (dh_rg, dg_rg), _ = rowk(_rg_out_bwd_fn, [(s["h_rg"], RG_WIDTH, 0), (proj, RG_WIDTH, P_G // RG_WIDTH), (dycat, RG_WIDTH, 3)],
                             [], [RG_WIDTH, RG_WIDTH], [], rows=t, name="rg_out_bwd", out_dtypes=[F32, BF16])
    g_rg = scan_real(_shift_rows_up(s["a_rg"]), dh_rg, reverse=True, name="rg_scan_bwd")
    rg_full = [p["rg_wa"], p["rg_wx"], p["rg_ba"], p["rg_bx"], p["rg_lam"]]
    (dxc,), (dwa, dwx, dba, dbx, dlam) = rowk(
        _rg_pre_bwd_fn, [(s["xc"], RG_WIDTH, 0), (g_rg, RG_WIDTH, 0), (_shift_rows_down(s["h_rg"]), RG_WIDTH, 0)], rg_full,
        [RG_WIDTH], [(RG_WIDTH, RG_WIDTH), (RG_WIDTH, RG_WIDTH), (1, RG_WIDTH), (1, RG_WIDTH), (1, RG_WIDTH)],
        rows=t, name="rg_pre_bwd")
    dxr, d_rgcw, d_rgcb = conv_bwd(proj, P_XR // RG_WIDTH, dxc, p["rg_cw"], p["rg_cb"], width=RG_WIDTH, act=False, name="rg_conv_bwd")
    (dylin, du_a), (d_s5d, d_gluw, d_glub) = rowk(
        _s5_post_bwd_fn, [(s["ylin"], S5_WIDTH, 0), (proj, S5_WIDTH, P_U // S5_WIDTH), (dycat, S5_WIDTH, 2)],
        [p["s5_d"], p["s5_glu_w"], p["s5_glu_b"]], [S5_WIDTH, S5_WIDTH],
        [(1, S5_WIDTH), (S5_WIDTH, S5_WIDTH), (1, S5_WIDTH)], rows=t, name="s5_post_bwd")
    du, dccat, dbcat, dar, dai = s5_bwd(dylin, du_a, s["hs5"], proj, p["bcat"], p["lam_adj"], p["ccat"], name="s5_bwd")
    dxbc, dz, ddt, dprm, ddx, dnw, d_scw, d_scb = ssd_bwd(proj, p["ssd_cw"], p["ssd_cb"], p["prow"], p["ssd_dx"], p["ssd_nw"],
                                                         s["yraw"], s["sall"], dycat, name="ssd_bwd")
    dproj = jnp.concatenate([dxbc, dz, du, dxr, dg_rg, ddt, jnp.zeros((t, D_INP - P_DT - LANE), BF16)], axis=1)
    dh0 = mm(dproj, p["w_inp"], o_extra=(dpre1,), fo=_add_alpha, name="in_proj_dx")
    dwp = mm(dproj, s["h0"], ta=True, name="in_proj_dw")
    gfull["w_in"][l] = jnp.concatenate([dwp[P_Z:P_Z + 512], dwp[P_XBC:P_XBC + 1024], dwp[P_DT:P_DT + 8],
                                        dwp[P_U:P_U + 256], dwp[P_XR:P_XR + 256], dwp[P_G:P_G + 256]], axis=0)
    gfull["ssd_conv_w"][l], gfull["rg_conv_w"][l], gfull["s5_glu_w"][l] = d_scw, d_rgcw, d_gluw
    ng, ns = S5_GROUPS, S5_STATE
    dbbr = jnp.swapaxes(_blockdiag_extract(dbcat[:, :S5_NSTATE], ng), 1, 2)
    dbbi = jnp.swapaxes(_blockdiag_extract(dbcat[:, S5_NSTATE:], ng), 1, 2)
    d_lr, d_li, d_ls, d_bre, d_bim = p["s5_vjp"]((dar.reshape(ng, ns), dai.reshape(ng, ns), dbbr, dbbi))
    gsmall["s5_lam_re"][l], gsmall["s5_lam_im"][l], gsmall["s5_log_step"][l] = d_lr, d_li, d_ls
    gsmall["s5_b_re"][l], gsmall["s5_b_im"][l] = d_bre, d_bim
    gsmall["s5_c_re"][l] = jnp.swapaxes(_blockdiag_extract(dccat[:S5_NSTATE], ng), 1, 2)
    gsmall["s5_c_im"][l] = -jnp.swapaxes(_blockdiag_extract(dccat[S5_NSTATE:], ng), 1, 2)
    gsmall["s5_d"][l], gsmall["s5_glu_b"][l] = d_s5d[0], d_glub[0]
    gsmall["ssd_conv_b"][l], gsmall["rg_conv_b"][l] = d_scb[0], d_rgcb[0]
    gsmall["ssd_dt_bias"][l], gsmall["ssd_a_log"][l] = dprm[0, :8], dprm[1, :8]
    gsmall["ssd_d"][l] = ddx.reshape(SSD_HEADS, SSD_HEAD_DIM).sum(axis=1)
    gsmall["ssd_norm_w"][l] = dnw[0]
    gsmall["rg_wa"][l], gsmall["rg_wx"][l] = _blockdiag_extract(dwa, RG_BLOCKS), _blockdiag_extract(dwx, RG_BLOCKS)
    gsmall["rg_ba"][l], gsmall["rg_bx"][l] = dba.reshape(RG_BLOCKS, RG_BLOCK_DIM), dbx.reshape(RG_BLOCKS, RG_BLOCK_DIM)
    gsmall["rg_lambda"][l] = dlam[0]
    for i, (dg, db) in zip((1, 2, 3), ((dg1, db1), (dg2, db2), (dg3, db3))):
        gsmall[f"ln{i}_g"][l], gsmall[f"ln{i}_b"][l] = dg[0], db[0]
    return dh0, got


def _step(a):
    h = a["x"][0]
    mem = a["mem"][0]
    t = h.shape[0]
    r4, r3 = PACK_ROWS // 4, 3 * PACK_ROWS // 8

    def my_shards(pre):
        return ({name: (jnp.swapaxes(a[pre + name], 1, 2) if tr else a[pre + name]) for name, tr, _ in BIG},
                [a[pre + name] for name, _ in TINY])

    def my_pack(pre, l):
        big, tiny = my_shards(pre)
        return _pack_layer({name: w[l] for name, w in big.items()},
                           jnp.concatenate([w.reshape(-1) for w in tiny]) if l == 0 else None)

    big, tiny = my_shards("")
    tiny16 = [(lax.bitcast_convert_type(w, BF16) if name in KEEP_F32 else w.astype(BF16)).reshape(-1)
              for (name, _), w in zip(TINY, tiny)]
    packed = [_pack_layer({name: w[l].astype(BF16) for name, w in big.items()}, jnp.concatenate(tiny16) if l == 0 else None)
              for l in range(DEPTH)]
    small = {name: a[name] for name in SMALL}

    def gathered_weights(g):
        gbig, gtiny = _unpack_layer(g)
        return {name: w.reshape(-1, WIDE) for name, w in gbig.items()}, gtiny

    full, gtiny = gathered_weights(all_gather(packed[0], name="ag_weights"))
    tiny_shapes = [w.shape + ((2,) if name in KEEP_F32 else ()) for (name, _), w in zip(TINY, tiny)]
    tiny_full = {name: _to_full(lax.bitcast_convert_type(g, F32) if name in KEEP_F32 else g, axis)
                 for (name, axis), g in zip(TINY, _split_flat(gtiny, tiny_shapes))}
    p0 = _layer_params({**full, **tiny_full}, small, 0)
    h, s0, got = _layer_fwd(h, mem, p0, sides={"in_proj": ("gather", packed[1], 0, r4), "mlp_up": ("gather", packed[1], r4, r3),
                                                "mlp_down": ("gather", packed[1], r4 + r3, r3)})
    full, _ = gathered_weights(jnp.concatenate(got, axis=1))
    p1 = _layer_params({**full, **tiny_full}, small, 1)
    h, s1, _ = _layer_fwd(h, mem, p1)
    (dh,), (loss_part,) = rowk(_loss_fn, [(h, D_MODEL, 0), (a["loss_target"][0], D_MODEL, 0)], [], [D_MODEL], [(1, 1)],
                               rows=t, name="loss_head")
    loss = lax.psum(loss_part[0, 0], ("x", "y", "c"))
    gfull = {name: [None] * DEPTH for name in SHARDED}
    gsmall = {name: [None] * DEPTH for name in SMALL}

    def chip_partials(l):
        gbig = {name: gfull[name][l].reshape(N_DEV, rows, WIDE) for name, _, rows in BIG}
        gtiny = None
        if l == 0:
            gtiny = jnp.concatenate([_to_slabs(jnp.stack(gfull[name]), axis).reshape(N_DEV, -1) for name, axis in TINY], axis=1)
        slabs = _pack_layer(gbig, gtiny)
        halves = jnp.swapaxes(slabs.reshape((4, 2) + slabs.shape[1:]), 0, 1)
        theirs = rs_sibling_exchange(halves, name="rs_sibling")
        return pair_sum_bf16(halves, theirs, name="rs_pair_sum")

    dh, _ = _layer_bwd(dh, mem, p1, s1, 1, gfull, gsmall)
    part1 = chip_partials(1)
    dh, got = _layer_bwd(dh, mem, p0, s0, 0, gfull, gsmall, sides={"mlp_da": ("chips", part1, 0, r3), "mlp_dx": ("chips", part1, r3, r3),
                                                                    "xa_do": ("chips", part1, 2 * r3, r4)})
    grad_x = dh[None]
    landed = [rs_chip_exchange(chip_partials(0), name="rs_chips"), jnp.concatenate(got, axis=1)]
    bigs = [adamw(landed[l], my_pack("", l), my_pack("m_", l), my_pack("v_", l), name="adamw_sharded", tt=128) for l in range(DEPTH)]
    gs = _pack_rows(jnp.concatenate([jnp.stack(gsmall[name]).reshape(-1) for name in SMALL]), 8)
    gs = all_gather(gs, name="ag_small_grads")
    pks = lambda pre: _pack_rows(jnp.concatenate([a[pre + name].reshape(-1) for name in SMALL]), 8)
    sm = adamw(gs, pks(""), pks("m_"), pks("v_"), name="adamw_replicated", tt=gs.shape[1])
    out = {}
    for i, kind in enumerate(("grad_", "delta_", "new_m_", "new_v_")):
        layers = [_unpack_layer(bigs[l][i]) for l in range(DEPTH)]
        for name, tr, _ in BIG:
            arr = jnp.stack([layers[l][0][name] for l in range(DEPTH)])
            out[kind + name] = jnp.swapaxes(arr, 1, 2) if tr else arr
        for (name, _), arr in zip(TINY, _split_flat(layers[0][1], [w.shape for w in tiny])):
            out[kind + name] = arr
        for name, arr in zip(SMALL, _unpack(sm[i], [a[name].shape for name in SMALL])):
            out[kind + name] = arr
    return (loss, grad_x) + tuple(out[kind + name] for kind in ("grad_", "delta_", "new_m_", "new_v_") for name in WEIGHTS)


def kernel(x, mem, w_in, w_out, ssd_conv_w, ssd_conv_b, ssd_dt_bias, ssd_a_log, ssd_d, ssd_norm_w, s5_lam_re, s5_lam_im, s5_log_step, s5_b_re, s5_b_im, s5_c_re, s5_c_im, s5_d, s5_glu_w, s5_glu_b, rg_conv_w, rg_conv_b, rg_wa, rg_ba, rg_wx, rg_bx, rg_lambda, ln1_g, ln1_b, xa_wq, xa_wk, xa_wv, xa_wo, ln2_g, ln2_b, mlp_w1, mlp_w2, ln3_g, ln3_b, loss_target, m_w_in, m_w_out, m_ssd_conv_w, m_ssd_conv_b, m_ssd_dt_bias, m_ssd_a_log, m_ssd_d, m_ssd_norm_w, m_s5_lam_re, m_s5_lam_im, m_s5_log_step, m_s5_b_re, m_s5_b_im, m_s5_c_re, m_s5_c_im, m_s5_d, m_s5_glu_w, m_s5_glu_b, m_rg_conv_w, m_rg_conv_b, m_rg_wa, m_rg_ba, m_rg_wx, m_rg_bx, m_rg_lambda, m_ln1_g, m_ln1_b, m_xa_wq, m_xa_wk, m_xa_wv, m_xa_wo, m_ln2_g, m_ln2_b, m_mlp_w1, m_mlp_w2, m_ln3_g, m_ln3_b, v_w_in, v_w_out, v_ssd_conv_w, v_ssd_conv_b, v_ssd_dt_bias, v_ssd_a_log, v_ssd_d, v_ssd_norm_w, v_s5_lam_re, v_s5_lam_im, v_s5_log_step, v_s5_b_re, v_s5_b_im, v_s5_c_re, v_s5_c_im, v_s5_d, v_s5_glu_w, v_s5_glu_b, v_rg_conv_w, v_rg_conv_b, v_rg_wa, v_rg_ba, v_rg_wx, v_rg_bx, v_rg_lambda, v_ln1_g, v_ln1_b, v_xa_wq, v_xa_wk, v_xa_wv, v_xa_wo, v_ln2_g, v_ln2_b, v_mlp_w1, v_mlp_w2, v_ln3_g, v_ln3_b):
    return _step(dict(locals()))
```

```python
import math

import jax
import jax.numpy as jnp
from jax import lax
from jax.experimental import pallas as pl
from jax.experimental.pallas import tpu as pltpu

F32 = jnp.float32
BF16 = jnp.bfloat16

N_DEV = 8
D_MODEL = 1024
DEPTH = 2
SSD_WIDTH = 512
SSD_HEADS = 8
SSD_HEAD_DIM = 64
SSD_STATE = 128
SSD_CHUNK = 128
SSD_XBC = 1024
S5_WIDTH = 256
S5_GROUPS = 16
S5_GROUP_CH = 16
S5_STATE = 64
S5_NSTATE = S5_GROUPS * S5_STATE
RG_WIDTH = 256
RG_BLOCKS = 4
RG_BLOCK_DIM = 64
RG_C = 8.0
XA_HEADS = 4
XA_HEAD_DIM = 256
ALPHA = (2.0 * DEPTH) ** 0.25
LN_EPS = 1e-5
ADAM_LR, ADAM_B1, ADAM_B2, ADAM_EPS, ADAM_WD, ADAM_STEP = 0.001, 0.9, 0.999, 1e-08, 0.01, 10

P_XBC, P_Z, P_U, P_XR, P_G, P_DT = 0, 1024, 1536, 1792, 2048, 2304
D_INP = 2560
LANE = 128
VMEM_LIMIT = 56 * 1024 * 1024
ROW_TILE = 512

_NN = ((1,), (0,))
_NT = ((1,), (1,))
_TN = ((0,), (0,))


def _dot(a, b, dims=_NN):
    return lax.dot_general(a.astype(BF16), b.astype(BF16), (dims, ((), ())), preferred_element_type=F32)


def _split_bf16(x, parts):
    out, rem = [], x
    for _ in range(parts):
        piece = rem.astype(BF16)
        out.append(piece)
        rem = rem - piece.astype(F32)
    return out


def _dot_mask(a, b, dims=_NN, *, mask_left, parts):
    if mask_left:
        return sum(_dot(a, piece, dims) for piece in _split_bf16(b, parts))
    return sum(_dot(piece, b, dims) for piece in _split_bf16(a, parts))


def _sigmoid(x):
    return 1.0 / (1.0 + jnp.exp(-x))


def _silu(x):
    return x * _sigmoid(x)


def _dsilu(x):
    s = _sigmoid(x)
    return s * (1.0 + x * (1.0 - s))


_GK = math.sqrt(2.0 / math.pi)
_GC = 0.044715


def _gelu(x):
    return 0.5 * x * (1.0 + jnp.tanh(_GK * (x + _GC * x * x * x)))


def _dgelu(x):
    th = jnp.tanh(_GK * (x + _GC * x * x * x))
    return 0.5 * (1.0 + th) + 0.5 * x * (1.0 - th * th) * _GK * (1.0 + 3.0 * _GC * x * x)


def _log1p_pos(e):
    return jnp.where(e < 1e-2, e * (1.0 - e * (0.5 - e * (1.0 / 3.0))), jnp.log(1.0 + e))


def _softplus(x):
    return jnp.maximum(x, 0.0) + _log1p_pos(jnp.exp(-jnp.abs(x)))


def _neg_expm1(x):
    poly = -x * (1.0 + x * (0.5 + x * (1.0 / 6.0 + x * (1.0 / 24.0 + x * (1.0 / 120.0)))))
    return jnp.where(x > -0.05, poly, 1.0 - jnp.exp(x))


def _params(sem):
    return pltpu.CompilerParams(dimension_semantics=sem, vmem_limit_bytes=VMEM_LIMIT)


RESIDENT_BYTES = 8 * 1024 * 1024
STREAM_BYTES = 4 * 1024 * 1024


def _halve_to_fit(dims, bytes_per, limit):
    dims = list(dims)
    while math.prod(dims) * bytes_per > limit:
        i = max(range(len(dims)), key=lambda d: dims[d])
        assert dims[i] % 256 == 0, dims
        dims[i] //= 2
    return dims


def _side_exchange(side, src, dst, sems, step, nsteps):
    kind, _, r0, rows = side
    span = pl.ds(r0, rows)
    if kind == "gather":
        phases = lambda: _ag_phases(src.at[span], dst, *sems)
        when = (0, (3 * nsteps) // 4, nsteps - 1)
    else:
        phases = lambda: _rs_chip_phases(src, dst, *sems, rows=span)
        when = (0, nsteps - 1)
    for idx, at in enumerate(when):
        pl.when(step == at)(lambda idx=idx: phases()[idx]())


def mm(a, b, *, name, ta=False, tb=False, a_extra=(), fa=None, o_extra=(), r_extra=(), fo=None, n_out=1,
       a_off=0, m=None, k=None, out_dtype=F32, side=None):
    n = b.shape[0] if tb else b.shape[1]
    na, no, nr = 1 + len(a_extra), len(o_extra), len(r_extra)
    if not ta:
        assert m is None
        m, kdim = a.shape[0], (a.shape[1] if k is None else k)
        assert a_off % kdim == 0
        (tn,) = _halve_to_fit([n], kdim * b.dtype.itemsize, RESIDENT_BYTES)
        (tm,) = _halve_to_fit([min(512, m)], max(tn, kdim) * 4, STREAM_BYTES)
        a_spec = pl.BlockSpec((tm, kdim), lambda i, j: (i, a_off // kdim))
        b_spec = pl.BlockSpec((tn, kdim), lambda i, j: (j, 0)) if tb else pl.BlockSpec((kdim, tn), lambda i, j: (0, j))
        o_spec = pl.BlockSpec((tm, tn), lambda i, j: (i, j))
        dims = _NT if tb else _NN

        r_spec = pl.BlockSpec((1, tn), lambda i, j: (0, j))

        grid = (m // tm, n // tn)
        nin = na + 1 + no + nr

        def body(*refs):
            a_refs, b_ref, e_refs, out_refs = refs[:na], refs[na], refs[na + 1:nin], refs[nin + (side is not None):nin + (side is not None) + n_out]
            if side is not None:
                _side_exchange(side, refs[nin], refs[nin + 1 + n_out], refs[nin + 2 + n_out:],
                               pl.program_id(0) * grid[1] + pl.program_id(1), grid[0] * grid[1])
            av = a_refs[0][...] if fa is None else fa(*[r[...] for r in a_refs])
            acc = _dot(av, b_ref[...], dims)
            res = acc if fo is None else fo(acc, *[r[...] for r in e_refs])
            for r, v in zip(out_refs, res if n_out > 1 else (res,)):
                r[...] = v.astype(r.dtype)

        sem = ("parallel", "parallel") if side is None else ("arbitrary", "arbitrary")
    else:
        assert k is None and not tb and fo is None and not o_extra and not r_extra and n_out == 1 and out_dtype == F32
        assert side is None
        kdim, m = a.shape[0], (a.shape[1] if m is None else m)
        r_spec = None
        tm, tn = _halve_to_fit([m, n], 4, RESIDENT_BYTES)
        (tk,) = _halve_to_fit([min(512, kdim)], max(tm, tn) * 4, STREAM_BYTES)
        assert a_off % tm == 0
        a_spec = pl.BlockSpec((tk, tm), lambda i, j, kk: (kk, i + a_off // tm))
        b_spec = pl.BlockSpec((tk, tn), lambda i, j, kk: (kk, j))
        o_spec = pl.BlockSpec((tm, tn), lambda i, j, kk: (i, j))

        def body(*refs):
            a_refs, b_ref, out_ref = refs[:na], refs[na], refs[na + 1]

            @pl.when(pl.program_id(2) == 0)
            def _():
                out_ref[...] = jnp.zeros_like(out_ref)

            av = a_refs[0][...] if fa is None else fa(*[r[...] for r in a_refs])
            out_ref[...] += _dot(av, b_ref[...], _TN)

        grid, sem = (m // tm, n // tn, kdim // tk), ("parallel", "parallel", "arbitrary")
    assert m % tm == 0 and n % tn == 0, (name, m, n, tm, tn)
    out = jax.ShapeDtypeStruct((m, n), out_dtype)
    if side is None:
        return pl.pallas_call(
            body, name=name, grid=grid,
            in_specs=[a_spec] * na + [b_spec] + [o_spec] * no + [r_spec] * nr,
            out_specs=o_spec if n_out == 1 else [o_spec] * n_out, out_shape=out if n_out == 1 else [out] * n_out,
            compiler_params=_params(sem),
        )(a, *a_extra, b, *o_extra, *r_extra)
    kind, arr, _, rows = side
    landed = jax.ShapeDtypeStruct(((N_DEV, rows) if kind == "gather" else (4, rows)) + arr.shape[-1:], arr.dtype)
    return pl.pallas_call(
        body, name=name, grid=grid,
        in_specs=[a_spec] * na + [b_spec] + [o_spec] * no + [r_spec] * nr + [_ANY],
        out_specs=[o_spec] * n_out + [_ANY], out_shape=[out] * n_out + [landed],
        scratch_shapes=list(_AG_SEMS if kind == "gather" else _RS_SEMS),
        compiler_params=_params(sem),
    )(a, *a_extra, b, *o_extra, *r_extra, arr)


def rowk(fn, tiled, full, out_w, acc_shapes, *, rows, name, out_dtypes=None):
    tt = min(ROW_TILE, rows)
    n = rows // tt
    assert rows % tt == 0
    nt, nf, no = len(tiled), len(full), len(out_w)

    def tspec(w, cb):
        return pl.BlockSpec((tt, w), lambda i: (i, cb))

    def fspec(a):
        nd = a.ndim
        return pl.BlockSpec(a.shape, lambda i: (0,) * nd)

    def body(*refs):
        ins, fulls = refs[:nt], refs[nt:nt + nf]
        outs, accs = refs[nt + nf:nt + nf + no], refs[nt + nf + no:]
        res_t, res_a = fn(*[r[...] for r in ins], *[r[...] for r in fulls])
        for r, v in zip(outs, res_t):
            r[...] = v.astype(r.dtype)
        if accs:
            @pl.when(pl.program_id(0) == 0)
            def _():
                for r in accs:
                    r[...] = jnp.zeros_like(r)
            for r, v in zip(accs, res_a):
                r[...] += v

    outs = pl.pallas_call(
        body, name=name, grid=(n,),
        in_specs=[tspec(w, cb) for (_, w, cb) in tiled] + [fspec(a) for a in full],
        out_specs=[tspec(w, 0) for w in out_w] + [pl.BlockSpec(s, lambda i, nd=len(s): (0,) * nd) for s in acc_shapes],
        out_shape=[jax.ShapeDtypeStruct((rows, w), dt) for w, dt in zip(out_w, out_dtypes or [F32] * no)]
        + [jax.ShapeDtypeStruct(s, F32) for s in acc_shapes],
        compiler_params=_params(("arbitrary",)),
    )(*[a for (a, _, _) in tiled], *full)
    return outs[:no], outs[no:]


def _colsum(x):
    return jnp.sum(x, axis=0, keepdims=True)


def _rowsum(x):
    return jnp.sum(x, axis=1, keepdims=True)


def _ln_epilogue(acc, resid, g, b):
    pre = ALPHA * resid + acc
    mu = jnp.mean(pre, axis=1, keepdims=True)
    xc = pre - mu
    var = jnp.mean(xc * xc, axis=1, keepdims=True)
    return pre, xc * lax.rsqrt(var + LN_EPS) * g + b


def _ln_bwd_fn(pre, dout, g):
    mu = jnp.mean(pre, axis=1, keepdims=True)
    xc = pre - mu
    var = jnp.mean(xc * xc, axis=1, keepdims=True)
    rstd = lax.rsqrt(var + LN_EPS)
    xhat = xc * rstd
    dxh = dout * g
    dpre = rstd * (dxh - jnp.mean(dxh, axis=1, keepdims=True) - xhat * jnp.mean(dxh * xhat, axis=1, keepdims=True))
    return (dpre,), (_colsum(dout * xhat), _colsum(dout))


def mm_ln(a, w, resid, g, b, *, name, fa=None, side=None):
    assert w.shape[1] == D_MODEL
    return mm(a, w, fa=fa, o_extra=(resid,), r_extra=(g, b), fo=_ln_epilogue, n_out=2, name=name, side=side)


def ln_bwd(pre, dout, g, *, name):
    (dpre,), (dg, db) = rowk(_ln_bwd_fn, [(pre, D_MODEL, 0), (dout, D_MODEL, 0)], [g],
                             [D_MODEL], [(1, D_MODEL), (1, D_MODEL)], rows=pre.shape[0], name=name)
    return dpre, dg, db


def _loss_fn(y, tgt):
    e = y - tgt
    part = _colsum(_rowsum(e * e)) * (0.5 / D_MODEL)
    return (e * (1.0 / D_MODEL),), (part,)


_XA_SCALE = 1.0 / math.sqrt(XA_HEAD_DIM)


def _attn_probs(qh, kh):
    s = _dot(qh, kh, _NT) * _XA_SCALE
    e = jnp.exp(s - jnp.max(s, axis=1, keepdims=True))
    return e / _rowsum(e)


def _attn_fwd_fn(q, k, v):
    outs = []
    for hd in range(XA_HEADS):
        sl = slice(hd * XA_HEAD_DIM, (hd + 1) * XA_HEAD_DIM)
        outs.append(_dot(_attn_probs(q[:, sl], k[:, sl]), v[:, sl]))
    return (jnp.concatenate(outs, axis=1),), ()


def _attn_bwd_fn(q, do, k, v):
    dqs, dks, dvs = [], [], []
    for hd in range(XA_HEADS):
        sl = slice(hd * XA_HEAD_DIM, (hd + 1) * XA_HEAD_DIM)
        qh, kh, vh, doh = q[:, sl], k[:, sl], v[:, sl], do[:, sl]
        p = _attn_probs(qh, kh)
        dp = _dot(doh, vh, _NT)
        ds = p * (dp - _rowsum(p * dp)) * _XA_SCALE
        dqs.append(_dot(ds, kh))
        dks.append(_dot(ds, qh, _TN))
        dvs.append(_dot(p, doh, _TN))
    cat = lambda xs: jnp.concatenate(xs, axis=1)
    return (cat(dqs),), (cat(dks), cat(dvs))


def _s5_post_fwd_fn(ylin, u, dskip, gw, gb):
    yg = _gelu(ylin + dskip * u)
    return (yg * _sigmoid(_dot(yg, gw) + gb),), ()


def _s5_post_bwd_fn(ylin, u, dout, dskip, gw, gb):
    pre = ylin + dskip * u
    yg = _gelu(pre)
    sg = _sigmoid(_dot(yg, gw) + gb)
    dlin = dout * yg * sg * (1.0 - sg)
    dyg = dout * sg + _dot(dlin, gw, _NT)
    dpre = dyg * _dgelu(pre)
    return (dpre, dpre * dskip), (_colsum(dpre * u), _dot(yg, dlin, _TN), _colsum(dlin))


def _rg_gates(xc, wa, wx, ba, bx, lam):
    r = _sigmoid(_dot(xc, wa) + ba)
    i = _sigmoid(_dot(xc, wx) + bx)
    sp = _softplus(-lam)
    log_a = -RG_C * r * sp
    a = jnp.exp(log_a)
    mult = jnp.sqrt(_neg_expm1(2.0 * log_a))
    return r, i, sp, a, mult


def _rg_pre_bwd_fn(xc, gsc, hprev, wa, wx, ba, bx, lam):
    r, i, sp, a, mult = _rg_gates(xc, wa, wx, ba, bx, lam)
    da = gsc * hprev
    db = gsc
    dmult = db * i * xc
    di = db * mult * xc
    dxc = db * mult * i
    dlog_a = da * a - a * a * dmult / mult
    dr = dlog_a * (-RG_C * sp)
    dsp = _colsum(dlog_a * (-RG_C * r))
    dlam = dsp * (-_sigmoid(-lam))
    dpr = dr * r * (1.0 - r)
    dpi = di * i * (1.0 - i)
    dxc = dxc + _dot(dpr, wa, _NT) + _dot(dpi, wx, _NT)
    return (dxc,), (_dot(xc, dpr, _TN), _dot(xc, dpi, _TN), _colsum(dpr), _colsum(dpi), dlam)


def _conv_taps(x_ref, halo_ref, first):
    x = x_ref[...]
    halo = jnp.where(first, 0.0, halo_ref[...])
    rows8 = lax.broadcasted_iota(jnp.int32, halo.shape, 0)
    taps = [x]
    for j in (1, 2, 3):
        r = pltpu.roll(x, j, 0)
        top = jnp.where(rows8 < j, pltpu.roll(halo, j, 0), r[0:8])
        taps.append(jnp.concatenate([top, r[8:]], axis=0))
    return taps


def _conv_pre(taps, cw_ref, cb_ref):
    wv = cw_ref[...]
    pre = cb_ref[...] + wv[3:4, :] * taps[0]
    for j in (1, 2, 3):
        pre = pre + wv[3 - j:4 - j, :] * taps[j]
    return pre


def _conv_back(dpre, taps, cw_ref, nxt_ref):
    q = dpre.shape[0]
    rows8 = lax.broadcasted_iota(jnp.int32, (8, dpre.shape[1]), 0)
    wv = cw_ref[...]
    dx = wv[3:4, :] * dpre
    for j in (1, 2, 3):
        r = pltpu.roll(dpre, q - j, 0)
        bottom = jnp.where(rows8 >= 8 - j, pltpu.roll(nxt_ref[...], 8 - j, 0), r[q - 8:q])
        dx = dx + wv[3 - j:4 - j, :] * jnp.concatenate([r[:q - 8], bottom], axis=0)
    dw = jnp.concatenate([_colsum(dpre * taps[3 - kk]) for kk in range(4)], axis=0)
    nxt_ref[...] = dpre[0:8]
    return dx, dw, _colsum(dpre)


S5_CW = 256


def _cmul(ar, ai, br, bi):
    return ar * br - ai * bi, ar * bi + ai * br


def _scan8_complex(src_ref, dst_ref, lam_ref, st_ref, *, w, nb, reverse):
    rows = lax.broadcasted_iota(jnp.int32, (8, S5_CW), 0)
    b8 = lambda v: jnp.broadcast_to(v, (8, S5_CW))

    def shift(x, k):
        if reverse:
            return jnp.where(rows < 8 - k, pltpu.roll(x, 8 - k, 0), 0.0)
        return jnp.where(rows >= k, pltpu.roll(x, k, 0), 0.0)

    for c0 in range(0, w, S5_CW):
        re, im = pl.ds(c0, S5_CW), pl.ds(w + c0, S5_CW)
        pw = [(lam_ref[:, re], lam_ref[:, im])]
        for _ in range(7):
            pw.append(_cmul(*pw[-1], *pw[0]))
        pr, pi = b8(pw[7][0]), b8(pw[7][1])
        for j in range(7):
            sel = rows == (7 - j if reverse else j)
            pr, pi = jnp.where(sel, b8(pw[j][0]), pr), jnp.where(sel, b8(pw[j][1]), pi)
        steps = [(k, b8(pw[k - 1][0]), b8(pw[k - 1][1])) for k in (1, 2, 4)]
        edge = 0 if reverse else 7

        def blk(i, carry):
            hr, hi = carry
            base = pl.multiple_of((nb // 2 - 1 - i if reverse else i) * 16, 16)
            pend = []
            for off in ((8, 0) if reverse else (0, 8)):
                at = pl.ds(base + off, 8)
                xr, xi = src_ref[at, re], src_ref[at, im]
                for k, kr, ki in steps:
                    sr, si = shift(xr, k), shift(xi, k)
                    xr, xi = xr + kr * sr - ki * si, xi + kr * si + ki * sr
                pend.append((at, xr, xi))
            for at, xr, xi in pend:
                xr, xi = xr + pr * hr - pi * hi, xi + pr * hi + pi * hr
                dst_ref[at, re] = xr
                dst_ref[at, im] = xi
                hr, hi = b8(xr[edge:edge + 1, :]), b8(xi[edge:edge + 1, :])
            return hr, hi

        hr, hi = lax.fori_loop(0, nb // 2, blk, (st_ref[:, re], st_ref[:, im]))
        st_ref[:, re] = hr
        st_ref[:, im] = hi


def s5_fwd(proj, bcat, lam, ccat, *, name):
    t = proj.shape[0]
    tt = min(ROW_TILE, t)
    w2 = bcat.shape[1]

    def body(u_ref, b_ref, lam_ref, c_ref, h_ref, y_ref, bu_ref, st_ref):
        @pl.when(pl.program_id(0) == 0)
        def _():
            st_ref[...] = jnp.zeros_like(st_ref)

        bu_ref[...] = _dot(u_ref[...], b_ref[...])
        _scan8_complex(bu_ref, h_ref, lam_ref, st_ref, w=w2 // 2, nb=tt // 8, reverse=False)
        y_ref[...] = _dot(h_ref[...], c_ref[...])

    fixed = lambda a: pl.BlockSpec(a.shape, lambda i: (0, 0))
    return pl.pallas_call(
        body, name=name, grid=(t // tt,),
        in_specs=[pl.BlockSpec((tt, S5_WIDTH), lambda i: (i, P_U // S5_WIDTH)), fixed(bcat), fixed(lam), fixed(ccat)],
        out_specs=[pl.BlockSpec((tt, w2), lambda i: (i, 0)), pl.BlockSpec((tt, S5_WIDTH), lambda i: (i, 0))],
        out_shape=[jax.ShapeDtypeStruct((t, w2), F32), jax.ShapeDtypeStruct((t, S5_WIDTH), F32)],
        scratch_shapes=[pltpu.VMEM((tt, w2), F32), pltpu.VMEM((8, w2), F32)],
        compiler_params=_params(("arbitrary",)),
    )(proj, bcat, lam, ccat)


def s5_bwd(dylin, du_a, hs, proj, bcat, lam_adj, ccat, *, name):
    t = proj.shape[0]
    tt = min(ROW_TILE, t)
    n, w2 = t // tt, bcat.shape[1]
    w = w2 // 2

    def body(dy_ref, dua_ref, h_ref, hp_ref, u_ref, b_ref, lam_ref, c_ref,
             du_ref, dc_ref, db_ref, dar_ref, dai_ref, g_ref, st_ref):
        i = pl.program_id(0)

        @pl.when(i == 0)
        def _():
            for r in (st_ref, dc_ref, db_ref, dar_ref, dai_ref):
                r[...] = jnp.zeros_like(r)

        dy, h = dy_ref[...], h_ref[...]
        g_ref[...] = _dot(dy, c_ref[...], _NT)
        dc_ref[...] += _dot(h, dy, _TN)
        _scan8_complex(g_ref, g_ref, lam_ref, st_ref, w=w, nb=tt // 8, reverse=True)
        g = g_ref[...]
        du_ref[...] = (dua_ref[...] + _dot(g, b_ref[...], _NT)).astype(du_ref.dtype)
        db_ref[...] += _dot(u_ref[...], g, _TN)
        rows = lax.broadcasted_iota(jnp.int32, (tt, w2), 0)
        before = jnp.where(i == n - 1, 0.0, hp_ref[7:8, :])
        hprev = jnp.where(rows == 0, before, pltpu.roll(h, 1, 0))
        gr, gi, hr, hi = g[:, :w], g[:, w:], hprev[:, :w], hprev[:, w:]
        dar_ref[...] += _colsum(gr * hr + gi * hi)
        dai_ref[...] += _colsum(gi * hr - gr * hi)

    rev = lambda i: n - 1 - i
    row = lambda wd, cb=0: pl.BlockSpec((tt, wd), lambda i: (rev(i), cb))
    fixed = lambda shape: pl.BlockSpec(shape, lambda i: (0, 0))
    return pl.pallas_call(
        body, name=name, grid=(n,),
        in_specs=[row(S5_WIDTH), row(S5_WIDTH), row(w2),
                  pl.BlockSpec((8, w2), lambda i: (jnp.maximum(rev(i) * (tt // 8) - 1, 0), 0)),
                  row(S5_WIDTH, P_U // S5_WIDTH), fixed(bcat.shape), fixed(lam_adj.shape), fixed(ccat.shape)],
        out_specs=[row(S5_WIDTH), fixed(ccat.shape), fixed(bcat.shape), fixed((1, w)), fixed((1, w))],
        out_shape=[jax.ShapeDtypeStruct((t, S5_WIDTH), BF16), jax.ShapeDtypeStruct(ccat.shape, F32),
                   jax.ShapeDtypeStruct(bcat.shape, F32), jax.ShapeDtypeStruct((1, w), F32), jax.ShapeDtypeStruct((1, w), F32)],
        scratch_shapes=[pltpu.VMEM((tt, w2), F32), pltpu.VMEM((8, w2), F32)],
        compiler_params=_params(("arbitrary",)),
    )(dylin, du_a, hs, hs, proj, bcat, lam_adj, ccat)


def _scan8_real(a_ref, b_ref, o_ref, st_ref, *, nb, reverse):
    w = o_ref.shape[1]
    rows = lax.broadcasted_iota(jnp.int32, (8, w), 0)

    def blk(i, h):
        at = pl.ds(pl.multiple_of((nb - 1 - i if reverse else i) * 8, 8), 8)
        ta_, tb_ = a_ref[at, :], b_ref[at, :]
        out = jnp.zeros((8, w), F32)
        for j in (range(7, -1, -1) if reverse else range(8)):
            h = jnp.broadcast_to(ta_[j:j + 1, :], (8, w)) * h + jnp.broadcast_to(tb_[j:j + 1, :], (8, w))
            out = jnp.where(rows == j, h, out)
        o_ref[at, :] = out
        return h

    st_ref[...] = lax.fori_loop(0, nb, blk, st_ref[...])


def _rg_specs(tt, idx):
    return [pl.BlockSpec((tt, RG_WIDTH), lambda i: (idx(i), P_XR // RG_WIDTH)),
            pl.BlockSpec((8, RG_WIDTH), lambda i: (jnp.maximum(idx(i) * (tt // 8) - 1, 0), P_XR // RG_WIDTH)),
            pl.BlockSpec((tt, RG_WIDTH), lambda i: (idx(i), P_G // RG_WIDTH))]


def rg_fwd(proj, cw, cb, wa, wx, ba, bx, lam, *, name):
    t = proj.shape[0]
    tt = min(ROW_TILE, t)
    w = RG_WIDTH

    def body(x_ref, halo_ref, g_ref, cw_ref, cb_ref, wa_ref, wx_ref, ba_ref, bx_ref, lam_ref,
             y_ref, xc_ref, a_ref, h_ref, b_ref, st_ref):
        @pl.when(pl.program_id(0) == 0)
        def _():
            st_ref[...] = jnp.zeros_like(st_ref)

        xc = _conv_pre(_conv_taps(x_ref, halo_ref, pl.program_id(0) == 0), cw_ref, cb_ref)
        xc_ref[...] = xc
        r, i, sp, a, mult = _rg_gates(xc, wa_ref[...], wx_ref[...], ba_ref[...], bx_ref[...], lam_ref[...])
        a_ref[...] = a
        b_ref[...] = mult * (i * xc)
        _scan8_real(a_ref, b_ref, h_ref, st_ref, nb=tt // 8, reverse=False)
        y_ref[...] = (h_ref[...] * _gelu(g_ref[...])).astype(y_ref.dtype)

    fixed = lambda a: pl.BlockSpec(a.shape, lambda i: (0, 0))
    row = pl.BlockSpec((tt, w), lambda i: (i, 0))
    return pl.pallas_call(
        body, name=name, grid=(t // tt,),
        in_specs=_rg_specs(tt, lambda i: i) + [fixed(x) for x in (cw, cb, wa, wx, ba, bx, lam)],
        out_specs=[row] * 4,
        out_shape=[jax.ShapeDtypeStruct((t, w), BF16)] + [jax.ShapeDtypeStruct((t, w), F32)] * 3,
        scratch_shapes=[pltpu.VMEM((tt, w), F32), pltpu.VMEM((8, w), F32)],
        compiler_params=_params(("arbitrary",)),
    )(proj, proj, proj, cw, cb, wa, wx, ba, bx, lam)


def rg_bwd(proj, dycat, xc, a, h, cw, cb, wa, wx, ba, bx, lam, *, name):
    t = proj.shape[0]
    tt = min(ROW_TILE, t)
    n, w = t // tt, RG_WIDTH

    def body(x_ref, halo_ref, g_ref, dy_ref, xc_ref, a_ref, h_ref, hp_ref, cw_ref, wa_ref, wx_ref, ba_ref, bx_ref, lam_ref,
             dx_ref, dg_ref, dcw_ref, dcb_ref, dwa_ref, dwx_ref, dba_ref, dbx_ref, dlam_ref,
             au_ref, dh_ref, gs_ref, st_ref, anx_ref, nxt_ref):
        i = pl.program_id(0)
        accs = (dcw_ref, dcb_ref, dwa_ref, dwx_ref, dba_ref, dbx_ref, dlam_ref)

        @pl.when(i == 0)
        def _():
            for r in accs + (st_ref, anx_ref, nxt_ref):
                r[...] = jnp.zeros_like(r)

        h, g, dy, a = h_ref[...], g_ref[...], dy_ref[...], a_ref[...]
        dh_ref[...] = dy * _gelu(g)
        dg_ref[...] = (dy * h * _dgelu(g)).astype(dg_ref.dtype)
        rows = lax.broadcasted_iota(jnp.int32, (tt, w), 0)
        au_ref[...] = jnp.where(rows == tt - 1, anx_ref[0:1, :], pltpu.roll(a, tt - 1, 0))
        _scan8_real(au_ref, dh_ref, gs_ref, st_ref, nb=tt // 8, reverse=True)
        before = jnp.where(i == n - 1, 0.0, hp_ref[7:8, :])
        hprev = jnp.where(rows == 0, before, pltpu.roll(h, 1, 0))
        (dxc,), small = _rg_pre_bwd_fn(xc_ref[...], gs_ref[...], hprev, wa_ref[...], wx_ref[...], ba_ref[...], bx_ref[...], lam_ref[...])
        dx, dcw, dcb = _conv_back(dxc, _conv_taps(x_ref, halo_ref, i == n - 1), cw_ref, nxt_ref)
        dx_ref[...] = dx.astype(dx_ref.dtype)
        for r, v in zip(accs, (dcw, dcb) + tuple(small)):
            r[...] += v
        anx_ref[...] = a[0:8]

    rev = lambda i: n - 1 - i
    row = lambda cb_=0: pl.BlockSpec((tt, w), lambda i: (rev(i), cb_))
    fixed = lambda shape: pl.BlockSpec(shape, lambda i: (0, 0))
    acc_shapes = [(4, w), (1, w), (w, w), (w, w), (1, w), (1, w), (1, w)]
    return pl.pallas_call(
        body, name=name, grid=(n,),
        in_specs=_rg_specs(tt, rev) + [row(3), row(), row(), row(),
                                       pl.BlockSpec((8, w), lambda i: (jnp.maximum(rev(i) * (tt // 8) - 1, 0), 0))]
        + [fixed(x.shape) for x in (cw, wa, wx, ba, bx, lam)],
        out_specs=[row(), row()] + [fixed(sh) for sh in acc_shapes],
        out_shape=[jax.ShapeDtypeStruct((t, w), BF16)] * 2 + [jax.ShapeDtypeStruct(sh, F32) for sh in acc_shapes],
        scratch_shapes=[pltpu.VMEM((tt, w), F32)] * 3 + [pltpu.VMEM((8, w), F32)] * 3,
        compiler_params=_params(("arbitrary",)),
    )(proj, proj, proj, dycat, xc, a, h, h, cw, wa, wx, ba, bx, lam)


SSD_QQ = SSD_HEADS * SSD_CHUNK
SSD_GP = SSD_WIDTH // 2
SSD_GQ = SSD_QQ // 2


def _ssd_spread():
    h = jnp.arange(LANE)[:, None]
    spread_p = (jnp.arange(SSD_WIDTH)[None, :] // SSD_HEAD_DIM == h).astype(BF16)
    spread_q = (jnp.arange(SSD_QQ)[None, :] // SSD_CHUNK == h).astype(BF16)
    return spread_p, spread_q


def _ssd_prologue(dt_ref, prow_ref, sp_ref, sq_ref):
    q = SSD_CHUNK
    r = lax.broadcasted_iota(jnp.int32, (q, q), 0)
    c = lax.broadcasted_iota(jnp.int32, (q, q), 1)
    raw_c = dt_ref[...] + prow_ref[0:1, :]
    dt_c = _softplus(raw_c)
    a_r = -jnp.exp(prow_ref[1:2, :])
    cs_c = _dot_mask((r >= c).astype(F32), dt_c * a_r, mask_left=True, parts=3)
    both = _dot_mask(jnp.concatenate([dt_c, cs_c], axis=0), sp_ref[...], mask_left=False, parts=3)
    dt_x, cs_x = both[:q], both[q:]
    csx = _dot_mask(cs_c, sq_ref[...], mask_left=False, parts=3)
    rr = lax.broadcasted_iota(jnp.int32, (q, SSD_QQ), 0)
    ss = lax.broadcasted_iota(jnp.int32, (q, SSD_QQ), 1) & (q - 1)
    diag = rr == ss
    cs_row = _colsum(jnp.where(diag, csx, 0.0))
    lcat = jnp.exp(jnp.where(rr >= ss, csx - cs_row, -1e30))
    cl = cs_x[q - 1:q, :]
    return dict(raw_c=raw_c, dt_c=dt_c, a_r=a_r, dt_x=dt_x, cs_x=cs_x, lcat=lcat, diag=diag,
                ecs=jnp.exp(cs_x), wdec=jnp.exp(cl - cs_x), ecl=jnp.exp(cl), triu=(r <= c).astype(F32))


def _ssd_group(xbc_ref, g, lcat, xdt):
    ns, q = SSD_STATE, SSD_CHUNK
    bm = xbc_ref[:, pl.ds(SSD_WIDTH + g * ns, ns)]
    cm = xbc_ref[:, pl.ds(SSD_WIDTH + 2 * ns + g * ns, ns)]
    cb = _dot(cm, bm, _NT)
    lg = lcat[:, g * SSD_GQ:(g + 1) * SSD_GQ]
    wcat = jnp.concatenate([cb] * 4, axis=1) * lg
    head = lax.broadcasted_iota(jnp.int32, (1, SSD_GP), 1) // SSD_HEAD_DIM
    xg = xdt[:, g * SSD_GP:(g + 1) * SSD_GP]
    xbd = jnp.concatenate([jnp.where(head == j, xg, 0.0) for j in range(4)], axis=0)
    return bm, cm, lg, wcat, xbd, head


def _ssd_gate(yraw, z, nw):
    yg = yraw * _silu(z)
    r = lax.rsqrt(jnp.mean(yg * yg, axis=1, keepdims=True) + LN_EPS)
    return yg, r


def _ssd_specs(q, idx):
    return [pl.BlockSpec((q, SSD_XBC), lambda i: (idx(i), P_XBC // SSD_XBC)),
            pl.BlockSpec((8, SSD_XBC), lambda i: (jnp.maximum(idx(i) * (q // 8) - 1, 0), P_XBC // SSD_XBC)),
            pl.BlockSpec((q, SSD_WIDTH), lambda i: (idx(i), P_Z // SSD_WIDTH)),
            pl.BlockSpec((q, LANE), lambda i: (idx(i), P_DT // LANE)),
            pl.BlockSpec((4, SSD_XBC), lambda i: (0, 0)), pl.BlockSpec((1, SSD_XBC), lambda i: (0, 0)),
            pl.BlockSpec((8, LANE), lambda i: (0, 0)), pl.BlockSpec((1, SSD_WIDTH), lambda i: (0, 0)),
            pl.BlockSpec((1, SSD_WIDTH), lambda i: (0, 0)),
            pl.BlockSpec((LANE, SSD_WIDTH), lambda i: (0, 0)), pl.BlockSpec((LANE, SSD_QQ), lambda i: (0, 0))]


def ssd_fwd(proj, cw, cb, prow, d_x, nw, *, name):
    t = proj.shape[0]
    q, ns = SSD_CHUNK, SSD_STATE
    nc = t // q
    spread_p, spread_q = _ssd_spread()

    def body(x_ref, halo_ref, z_ref, dt_ref, cw_ref, cb_ref, prow_ref, dx_ref, nw_ref, sp_ref, sq_ref,
             y_ref, yraw_ref, sall_ref, s_ref, xbc_ref):
        @pl.when(pl.program_id(0) == 0)
        def _():
            s_ref[...] = jnp.zeros_like(s_ref)

        sall_ref[0] = s_ref[...]
        xbc_ref[...] = _silu(_conv_pre(_conv_taps(x_ref, halo_ref, pl.program_id(0) == 0), cw_ref, cb_ref))
        pr = _ssd_prologue(dt_ref, prow_ref, sp_ref, sq_ref)
        xs = xbc_ref[:, pl.ds(0, SSD_WIDTH)]
        xdt = xs * pr["dt_x"]
        xw = xdt * pr["wdec"]
        ys = []
        for g in range(2):
            gp = slice(g * SSD_GP, (g + 1) * SSD_GP)
            bm, cm, lg, wcat, xbd, head = _ssd_group(xbc_ref, g, pr["lcat"], xdt)
            st = s_ref[:, gp]
            ys.append(_dot(wcat, xbd) + pr["ecs"][:, gp] * _dot(cm, st) + xs[:, gp] * dx_ref[:, gp])
            s_ref[:, gp] = pr["ecl"][:, gp] * st + _dot(bm, xw[:, gp], _TN)
        yraw = jnp.concatenate(ys, axis=1)
        yraw_ref[...] = yraw
        yg, r = _ssd_gate(yraw, z_ref[...], nw_ref[...])
        y_ref[...] = (yg * r * nw_ref[...]).astype(y_ref.dtype)

    row = pl.BlockSpec((q, SSD_WIDTH), lambda i: (i, 0))
    return pl.pallas_call(
        body, name=name, grid=(nc,),
        in_specs=_ssd_specs(q, lambda i: i),
        out_specs=[row, row, pl.BlockSpec((1, ns, SSD_WIDTH), lambda i: (i, 0, 0))],
        out_shape=[jax.ShapeDtypeStruct((t, SSD_WIDTH), BF16), jax.ShapeDtypeStruct((t, SSD_WIDTH), F32),
                   jax.ShapeDtypeStruct((nc, ns, SSD_WIDTH), F32)],
        scratch_shapes=[pltpu.VMEM((ns, SSD_WIDTH), F32), pltpu.VMEM((q, SSD_XBC), F32)],
        compiler_params=_params(("arbitrary",)),
    )(proj, proj, proj, proj, cw, cb, prow, d_x, nw, spread_p, spread_q)


def ssd_bwd(proj, cw, cb, prow, d_x, nw, yraw, sall, dout, *, name):
    t = proj.shape[0]
    q, ns = SSD_CHUNK, SSD_STATE
    nc = t // q
    spread_p, spread_q = _ssd_spread()

    def body(x_ref, halo_ref, z_ref, dt_ref, cw_ref, cb_ref, prow_ref, dx_ref, nw_ref, sp_ref, sq_ref, yraw_ref, sall_ref, dout_ref,
             dxraw_ref, dz_ref, ddt_ref, dprm_ref, ddx_ref, dnw_ref, dcw_ref, dcb_ref, ds_ref, xbc_ref, dxbc_ref, nxt_ref):
        @pl.when(pl.program_id(0) == 0)
        def _():
            for r in (ds_ref, dprm_ref, ddx_ref, dnw_ref, dcw_ref, dcb_ref, nxt_ref):
                r[...] = jnp.zeros_like(r)

        taps = _conv_taps(x_ref, halo_ref, pl.program_id(0) == nc - 1)
        conv_pre = _conv_pre(taps, cw_ref, cb_ref)
        xbc_ref[...] = _silu(conv_pre)

        yraw, z, nwv, dout = yraw_ref[...], z_ref[...], nw_ref[...], dout_ref[...]
        yg, r = _ssd_gate(yraw, z, nwv)
        dnw_ref[...] += _colsum(dout * yg * r)
        dyn = dout * nwv
        dyg = r * dyn - yg * (r * r * r) * jnp.mean(dyn * yg, axis=1, keepdims=True)
        dy = dyg * _silu(z)
        dz_ref[...] = (dyg * yraw * _dsilu(z)).astype(dz_ref.dtype)

        pr = _ssd_prologue(dt_ref, prow_ref, sp_ref, sq_ref)
        xs = xbc_ref[:, pl.ds(0, SSD_WIDTH)]
        xdt = xs * pr["dt_x"]
        wdec, ecl = pr["wdec"], pr["ecl"]
        xw = xdt * wdec
        dzm_all = pr["ecs"] * dy
        last = (lax.broadcasted_iota(jnp.int32, (q, 1), 0) == q - 1).astype(F32)
        dxs, dcsxs, es = [], [], []
        for g in range(2):
            gp = slice(g * SSD_GP, (g + 1) * SSD_GP)
            bm, cm, lg, wcat, xbd, head = _ssd_group(xbc_ref, g, pr["lcat"], xdt)
            dyg_ = dy[:, gp]
            dwcat = _dot(dyg_, xbd, _NT)
            dxbd = _dot(wcat, dyg_, _TN)
            dxg = sum(jnp.where(head == j, dxbd[j * q:(j + 1) * q], 0.0) for j in range(4))
            es.append(dwcat * wcat)
            dmm = dwcat * lg
            dm = dmm[:, 0:q] + dmm[:, q:2 * q] + dmm[:, 2 * q:3 * q] + dmm[:, 3 * q:4 * q]
            dcm = _dot(dm, bm)
            dbm = _dot(dm, cm, _TN)
            st = sall_ref[0, :, gp]
            zmat = _dot(cm, st)
            dzm = dzm_all[:, gp]
            dcm = dcm + _dot(dzm, st, _NT)
            dst = _dot(cm, dzm, _TN)
            dcsx = dzm * zmat
            dsn = ds_ref[:, gp]
            dst = dst + ecl[:, gp] * dsn
            dclx = _colsum(dsn * st) * ecl[:, gp]
            dxw = _dot(bm, dsn)
            dbm = dbm + _dot(xw[:, gp], dsn, _NT)
            dxg = dxg + wdec[:, gp] * dxw
            tw = dxw * xdt[:, gp] * wdec[:, gp]
            dclx = dclx + _colsum(tw)
            dcsxs.append(dcsx - tw + last * dclx)
            ds_ref[:, gp] = dst
            dxs.append(dxg)
            dxbc_ref[:, pl.ds(SSD_WIDTH + g * ns, ns)] = dbm
            dxbc_ref[:, pl.ds(SSD_WIDTH + 2 * ns + g * ns, ns)] = dcm
        dx = jnp.concatenate(dxs, axis=1)
        dxbc_ref[:, pl.ds(0, SSD_WIDTH)] = dx * pr["dt_x"] + dy * dx_ref[...]
        ddx_ref[...] += _colsum(dy * xs)
        red = _dot_mask(jnp.concatenate([jnp.concatenate(dcsxs, axis=1), dx * xs], axis=0), sp_ref[...], _NT,
                        mask_left=False, parts=2)
        e_all = jnp.concatenate(es, axis=1)
        e_red = _dot_mask(e_all - jnp.where(pr["diag"], _colsum(e_all), 0.0), sq_ref[...], _NT, mask_left=False, parts=2)
        dadt = _dot_mask(pr["triu"], red[:q] + e_red, mask_left=True, parts=2)
        draw = (red[q:] + dadt * pr["a_r"]) * _sigmoid(pr["raw_c"])
        ddt_ref[...] = draw.astype(ddt_ref.dtype)
        zero = jnp.zeros((6, LANE), F32)
        dprm_ref[...] += jnp.concatenate([_colsum(draw), _colsum(dadt * pr["dt_c"]) * pr["a_r"], zero], axis=0)
        dxr, dcw, dcb = _conv_back(dxbc_ref[...] * _dsilu(conv_pre), taps, cw_ref, nxt_ref)
        dxraw_ref[...] = dxr.astype(dxraw_ref.dtype)
        dcw_ref[...] += dcw
        dcb_ref[...] += dcb

    rev = lambda i: nc - 1 - i
    row = lambda w: pl.BlockSpec((q, w), lambda i: (rev(i), 0))
    fixed = lambda shape: pl.BlockSpec(shape, lambda i: (0, 0))
    return pl.pallas_call(
        body, name=name, grid=(nc,),
        in_specs=_ssd_specs(q, rev) + [row(SSD_WIDTH), pl.BlockSpec((1, ns, SSD_WIDTH), lambda i: (rev(i), 0, 0)),
                                       row(SSD_WIDTH)],
        out_specs=[row(SSD_XBC), row(SSD_WIDTH), row(LANE), fixed((8, LANE)), fixed((1, SSD_WIDTH)), fixed((1, SSD_WIDTH)),
                   fixed((4, SSD_XBC)), fixed((1, SSD_XBC))],
        out_shape=[jax.ShapeDtypeStruct((t, SSD_XBC), BF16), jax.ShapeDtypeStruct((t, SSD_WIDTH), BF16),
                   jax.ShapeDtypeStruct((t, LANE), BF16), jax.ShapeDtypeStruct((8, LANE), F32),
                   jax.ShapeDtypeStruct((1, SSD_WIDTH), F32), jax.ShapeDtypeStruct((1, SSD_WIDTH), F32),
                   jax.ShapeDtypeStruct((4, SSD_XBC), F32), jax.ShapeDtypeStruct((1, SSD_XBC), F32)],
        scratch_shapes=[pltpu.VMEM((ns, SSD_WIDTH), F32), pltpu.VMEM((q, SSD_XBC), F32), pltpu.VMEM((q, SSD_XBC), F32),
                        pltpu.VMEM((8, SSD_XBC), F32)],
        compiler_params=_params(("arbitrary",)),
    )(proj, proj, proj, proj, cw, cb, prow, d_x, nw, spread_p, spread_q, yraw, sall, dout)


def _me():
    return lax.axis_index("x"), lax.axis_index("y"), lax.axis_index("c")


_ANY = pl.BlockSpec(memory_space=pl.ANY)
_MESH = pl.DeviceIdType.MESH


_AG_SEMS = [pltpu.SemaphoreType.DMA((7,)), pltpu.SemaphoreType.DMA((7,)), pltpu.SemaphoreType.DMA(())]
_RS_SEMS = [pltpu.SemaphoreType.DMA((3,)), pltpu.SemaphoreType.DMA((3,)), pltpu.SemaphoreType.DMA(())]


def _ag_phases(src, dst, send_sems, recv_sems, local_sem):
    x, y, c = _me()
    me, sibling = (x, y, c), (x, y, 1 - c)
    chips = [(1 - x, y), (x, 1 - y), (1 - x, 1 - y)]

    def slot(px, py, pc):
        return dst.at[4 * px + 2 * py + pc]

    def copy(kk, blk, to, from_src=False):
        return pltpu.make_async_remote_copy(
            src_ref=src if from_src else slot(*blk), dst_ref=slot(*blk),
            send_sem=send_sems.at[kk], recv_sem=recv_sems.at[kk], device_id=to, device_id_type=_MESH)

    mine = lambda: pltpu.make_async_copy(src, slot(*me), local_sem)
    first = lambda: [copy(0, me, sibling, True)] + [copy(1 + j, me, (*chip, c), True) for j, chip in enumerate(chips)]
    passed = lambda j: copy(4 + j, (*chips[j], c), sibling)

    def start():
        mine().start()
        for cp in first():
            cp.start()

    def forward():
        for j, chip in enumerate(chips):
            copy(1 + j, (*chip, c), me).wait_recv()
            passed(j).start()

    def finish():
        copy(0, sibling, me).wait_recv()
        for j, chip in enumerate(chips):
            copy(4 + j, (*chip, 1 - c), me).wait_recv()
        for cp in first() + [passed(j) for j in range(3)]:
            cp.wait_send()
        mine().wait()

    return start, forward, finish


def _rs_chip_phases(src, dst, send_sems, recv_sems, local_sem, rows=None):
    x, y, c = _me()
    q_me = 2 * x + y
    pick = (lambda q: src.at[q]) if rows is None else (lambda q: src.at[q, rows])
    local = lambda: pltpu.make_async_copy(pick(q_me), dst.at[q_me], local_sem)
    copies = lambda: [pltpu.make_async_remote_copy(src_ref=pick(2 * px + py), dst_ref=dst.at[q_me], send_sem=send_sems.at[j],
                                                   recv_sem=recv_sems.at[j], device_id=(px, py, c), device_id_type=_MESH)
                      for j, (px, py) in enumerate([(1 - x, y), (x, 1 - y), (1 - x, 1 - y)])]

    def start():
        local().start()
        for cp in copies():
            cp.start()

    def finish():
        for cp in copies():
            cp.wait()
        local().wait()

    return start, finish


def all_gather(block, *, name):
    def body(src, dst, send_sems, recv_sems, local_sem):
        for phase in _ag_phases(src, dst, send_sems, recv_sems, local_sem):
            phase()

    return pl.pallas_call(
        body, name=name, in_specs=[_ANY], out_specs=_ANY,
        out_shape=jax.ShapeDtypeStruct((N_DEV,) + block.shape, block.dtype), scratch_shapes=list(_AG_SEMS),
    )(block)


RS_PIECES = 4


def rs_sibling_exchange(halves, *, name):
    _, nq, r, l = halves.shape
    rows = r // RS_PIECES
    assert r % RS_PIECES == 0 and rows % 16 == 0

    def body(src, dst, send_sems, recv_sems):
        x, y, c = _me()
        copies = []
        for q in range(nq):
            for i in range(RS_PIECES):
                kk = q * RS_PIECES + i
                cp = pltpu.make_async_remote_copy(
                    src_ref=src.at[1 - c, q, pl.ds(i * rows, rows)], dst_ref=dst.at[q, pl.ds(i * rows, rows)],
                    send_sem=send_sems.at[kk], recv_sem=recv_sems.at[kk], device_id=(x, y, 1 - c), device_id_type=_MESH)
                cp.start()
                copies.append(cp)
        for cp in copies:
            cp.wait()

    n_copies = nq * RS_PIECES
    return pl.pallas_call(
        body, name=name, in_specs=[_ANY], out_specs=_ANY,
        out_shape=jax.ShapeDtypeStruct((nq, r, l), halves.dtype),
        scratch_shapes=[pltpu.SemaphoreType.DMA((n_copies,)), pltpu.SemaphoreType.DMA((n_copies,))],
    )(halves)


def pair_sum_bf16(halves, theirs, *, name, tt=128):
    _, nq, r, wd = halves.shape
    tt = min(tt, r)
    parity = lax.axis_index("c").astype(jnp.int32).reshape(1)

    def body(c_ref, own_ref, sib_ref, o_ref):
        o_ref[...] = (own_ref[...] + sib_ref[...]).astype(BF16)

    return pl.pallas_call(
        body, name=name,
        grid_spec=pltpu.PrefetchScalarGridSpec(
            num_scalar_prefetch=1, grid=(nq, r // tt),
            in_specs=[pl.BlockSpec((None, None, tt, wd), lambda q, i, c: (c[0], q, i, 0)),
                      pl.BlockSpec((None, tt, wd), lambda q, i, c: (q, i, 0))],
            out_specs=pl.BlockSpec((None, tt, wd), lambda q, i, c: (q, i, 0))),
        out_shape=jax.ShapeDtypeStruct((nq, r, wd), BF16),
        compiler_params=_params(("parallel", "parallel")),
    )(parity, halves, theirs)


def rs_chip_exchange(part, *, name):
    def body(src, dst, send_sems, recv_sems, local_sem):
        for phase in _rs_chip_phases(src, dst, send_sems, recv_sems, local_sem):
            phase()

    return pl.pallas_call(
        body, name=name, in_specs=[_ANY], out_specs=_ANY,
        out_shape=jax.ShapeDtypeStruct(part.shape, part.dtype), scratch_shapes=list(_RS_SEMS),
    )(part)


def adamw(slabs, w, m, v, *, name, tt):
    ns, (r, wd) = slabs.shape[0], w.shape
    tt = min(tt, r)
    assert r % tt == 0

    def body(s_ref, w_ref, m_ref, v_ref, g_ref, d_ref, nm_ref, nv_ref):
        g = s_ref[0].astype(F32)
        for kdev in range(1, ns):
            g = g + s_ref[kdev].astype(F32)
        wv = w_ref[...]
        nm = ADAM_B1 * m_ref[...] + (1.0 - ADAM_B1) * g
        nv = ADAM_B2 * v_ref[...] + (1.0 - ADAM_B2) * (g * g)
        m_hat = nm / (1.0 - ADAM_B1 ** ADAM_STEP)
        v_hat = nv / (1.0 - ADAM_B2 ** ADAM_STEP)
        g_ref[...] = g
        d_ref[...] = -ADAM_LR * (m_hat / (jnp.sqrt(v_hat) + ADAM_EPS) + ADAM_WD * wv)
        nm_ref[...] = nm
        nv_ref[...] = nv

    spec = pl.BlockSpec((tt, wd), lambda i: (i, 0))
    return pl.pallas_call(
        body, name=name, grid=(r // tt,),
        in_specs=[pl.BlockSpec((ns, tt, wd), lambda i: (0, i, 0)), spec, spec, spec],
        out_specs=[spec] * 4, out_shape=[jax.ShapeDtypeStruct((r, wd), F32)] * 4,
        compiler_params=_params(("parallel",)),
    )(slabs, w, m, v)


WIDE = 1024
BIG = [("w_in", True, 289), ("w_out", False, 128), ("xa_wq", False, 128), ("xa_wk", False, 128), ("xa_wv", False, 128),
       ("xa_wo", False, 128), ("mlp_w2", False, 512), ("mlp_w1", True, 512)]
TINY = [("ssd_conv_w", 2), ("s5_glu_w", 1), ("rg_conv_w", 2)]
KEEP_F32 = ("ssd_conv_w", "rg_conv_w")
TINY_ROWS = 32
SHARDED = [name for name, _, _ in BIG] + [name for name, _ in TINY]
SMALL = ["ssd_conv_b", "ssd_dt_bias", "ssd_a_log", "ssd_d", "ssd_norm_w", "s5_lam_re", "s5_lam_im",
         "s5_log_step", "s5_b_re", "s5_b_im", "s5_c_re", "s5_c_im", "s5_d", "s5_glu_b", "rg_conv_b",
         "rg_wa", "rg_ba", "rg_wx", "rg_bx", "rg_lambda", "ln1_g", "ln1_b", "ln2_g", "ln2_b", "ln3_g", "ln3_b"]
WEIGHTS = ['w_in', 'w_out', 'ssd_conv_w', 'ssd_conv_b', 'ssd_dt_bias', 'ssd_a_log', 'ssd_d', 'ssd_norm_w',
           's5_lam_re', 's5_lam_im', 's5_log_step', 's5_b_re', 's5_b_im', 's5_c_re', 's5_c_im', 's5_d',
           's5_glu_w', 's5_glu_b', 'rg_conv_w', 'rg_conv_b', 'rg_wa', 'rg_ba', 'rg_wx', 'rg_bx', 'rg_lambda',
           'ln1_g', 'ln1_b', 'xa_wq', 'xa_wk', 'xa_wv', 'xa_wo', 'ln2_g', 'ln2_b', 'mlp_w1', 'mlp_w2',
           'ln3_g', 'ln3_b']


def _pad16(rows):
    return -(-rows // 16) * 16


def _pack_rows(flat, mult):
    n = flat.shape[-1]
    r = -(-n // (LANE * mult)) * mult
    pad = [(0, 0)] * (flat.ndim - 1) + [(0, r * LANE - n)]
    return jnp.pad(flat, pad).reshape(flat.shape[:-1] + (r, LANE))


def _unpack(packed, shapes):
    lead = packed.shape[:-2]
    flat = packed.reshape(lead + (-1,))
    out, off = [], 0
    for s in shapes:
        n = math.prod(s)
        out.append(flat[..., off:off + n].reshape(lead + tuple(s)))
        off += n
    return out


PACK_ROWS = 2048


def _tiny_block(flat):
    pad = [(0, 0)] * (flat.ndim - 1) + [(0, TINY_ROWS * WIDE - flat.shape[-1])]
    return jnp.pad(flat, pad).reshape(flat.shape[:-1] + (TINY_ROWS, WIDE))


def _pack_layer(big, tiny_flat=None):
    blocks, used = [], 0
    some = big[BIG[0][0]]

    def zeros(rows):
        return jnp.zeros(some.shape[:-2] + (rows, WIDE), some.dtype)

    for name, _, rows in BIG:
        blocks.append(jnp.pad(big[name], [(0, 0)] * (some.ndim - 2) + [(0, _pad16(rows) - rows), (0, 0)]))
        used += _pad16(rows)
    if tiny_flat is not None:
        blocks.append(_tiny_block(tiny_flat))
        used += TINY_ROWS
    return jnp.concatenate(blocks + [zeros(PACK_ROWS - used)], axis=-2)


def _unpack_layer(packed):
    big, off = {}, 0
    for name, _, rows in BIG:
        big[name] = packed[..., off:off + rows, :]
        off += _pad16(rows)
    return big, packed[..., off:off + TINY_ROWS, :].reshape(packed.shape[:-2] + (TINY_ROWS * WIDE,))


def _split_flat(flat, shapes):
    out, off = [], 0
    for s in shapes:
        n = math.prod(s)
        out.append(flat[..., off:off + n].reshape(flat.shape[:-1] + tuple(s)))
        off += n
    return out


def _to_full(gathered, axis):
    g = jnp.moveaxis(gathered, 0, axis)
    s = g.shape
    return g.reshape(s[:axis] + (s[axis] * s[axis + 1],) + s[axis + 2:])


def _to_slabs(full, axis):
    s = full.shape
    g = full.reshape(s[:axis] + (N_DEV, s[axis] // N_DEV) + s[axis + 1:])
    return jnp.moveaxis(g, axis, 0)


def _blockdiag(w):
    h, i, j = w.shape
    eye = jnp.eye(h, dtype=w.dtype)
    return (w[:, :, None, :] * eye[:, None, :, None]).reshape(h * i, h * j)


def _blockdiag_extract(m, h):
    i, j = m.shape[0] // h, m.shape[1] // h
    eye = jnp.eye(h, dtype=m.dtype)
    return (m.reshape(h, i, h, j) * eye[:, None, :, None]).sum(axis=2)


def _s5_disc(lr, li, ls, bre, bim):
    step = jnp.exp(ls)[:, None]
    er = jnp.exp(lr * step)
    ar, ai = er * jnp.cos(li * step), er * jnp.sin(li * step)
    nr, ni, den = ar - 1.0, ai, lr * lr + li * li
    qr, qi = (nr * lr + ni * li) / den, (ni * lr - nr * li) / den
    bbr = qr[..., None] * bre - qi[..., None] * bim
    bbi = qr[..., None] * bim + qi[..., None] * bre
    return ar, ai, bbr, bbi


def _row(v, width=None):
    v = v.reshape(1, -1)
    if width is not None and v.shape[1] < width:
        v = jnp.pad(v, ((0, 0), (0, width - v.shape[1])))
    return v


def _relu2(a):
    r = jnp.maximum(a, 0.0)
    return r * r


def _add_alpha(acc, d):
    return acc + ALPHA * d


def _layer_params(full, small, l):
    p = {}
    w_in = full["w_in"]
    z, xbc, dt, u, xr, g = w_in[0:512], w_in[512:1536], w_in[1536:1544], w_in[1544:1800], w_in[1800:2056], w_in[2056:2312]
    p["w_inp"] = jnp.concatenate([xbc, z, u, xr, g, dt, jnp.zeros((D_INP - P_DT - 8, D_MODEL), w_in.dtype)], axis=0)
    for k_ in ("w_out", "xa_wq", "xa_wk", "xa_wv", "xa_wo", "mlp_w1", "mlp_w2"):
        p[k_] = full[k_]
    p["s5_glu_w"] = full["s5_glu_w"][l]
    p["ssd_cw"], p["ssd_cb"] = full["ssd_conv_w"][l], _row(small["ssd_conv_b"][l])
    dtb, alog, dsk = small["ssd_dt_bias"][l], small["ssd_a_log"][l], small["ssd_d"][l]
    p["prow"] = jnp.concatenate([_row(dtb, LANE), _row(alog, LANE), jnp.zeros((6, LANE), F32)], axis=0)
    p["ssd_dx"] = _row(jnp.repeat(dsk, SSD_HEAD_DIM))
    p["ssd_nw"] = _row(small["ssd_norm_w"][l])
    s5_in = (small["s5_lam_re"][l], small["s5_lam_im"][l], small["s5_log_step"][l], small["s5_b_re"][l], small["s5_b_im"][l])
    (ar, ai, bbr, bbi), p["s5_vjp"] = jax.vjp(_s5_disc, *s5_in)
    p["lam_fwd"] = jnp.concatenate([_row(ar), _row(ai)], axis=1)
    p["lam_adj"] = jnp.concatenate([_row(ar), _row(-ai)], axis=1)
    p["bcat"] = jnp.concatenate([_blockdiag(jnp.swapaxes(bbr, 1, 2)), _blockdiag(jnp.swapaxes(bbi, 1, 2))], axis=1)
    p["ccat"] = jnp.concatenate([_blockdiag(jnp.swapaxes(small["s5_c_re"][l], 1, 2)),
                                 -_blockdiag(jnp.swapaxes(small["s5_c_im"][l], 1, 2))], axis=0)
    p["s5_d"], p["s5_glu_b"] = _row(small["s5_d"][l]), _row(small["s5_glu_b"][l])
    p["rg_cw"], p["rg_cb"] = full["rg_conv_w"][l], _row(small["rg_conv_b"][l])
    p["rg_wa"], p["rg_wx"] = _blockdiag(small["rg_wa"][l]), _blockdiag(small["rg_wx"][l])
    p["rg_ba"], p["rg_bx"], p["rg_lam"] = _row(small["rg_ba"][l]), _row(small["rg_bx"][l]), _row(small["rg_lambda"][l])
    for i in (1, 2, 3):
        p[f"g{i}"], p[f"b{i}"] = _row(small[f"ln{i}_g"][l]), _row(small[f"ln{i}_b"][l])
    return p


def _take_side(res, n_out, got):
    res = res if isinstance(res, (list, tuple)) else (res,)
    got.extend(res[n_out:])
    return res[0] if n_out == 1 else res[:n_out]


def _layer_fwd(h0, mem, p, sides={}):
    t = h0.shape[0]
    s = {"h0": h0}
    got = []
    proj = _take_side(mm(h0, p["w_inp"], tb=True, name="in_proj", side=sides.get("in_proj")), 1, got)
    y_ssd, yraw, sall = ssd_fwd(proj, p["ssd_cw"], p["ssd_cb"], p["prow"], p["ssd_dx"], p["ssd_nw"], name="ssd_fwd")
    hs5, ylin = s5_fwd(proj, p["bcat"], p["lam_fwd"], p["ccat"], name="s5_fwd")
    (y_s5,), _ = rowk(_s5_post_fwd_fn, [(ylin, S5_WIDTH, 0), (proj, S5_WIDTH, P_U // S5_WIDTH)],
                      [p["s5_d"], p["s5_glu_w"], p["s5_glu_b"]], [S5_WIDTH], [], rows=t, name="s5_post_fwd", out_dtypes=[BF16])
    rg_prm = (p["rg_cw"], p["rg_cb"], p["rg_wa"], p["rg_wx"], p["rg_ba"], p["rg_bx"], p["rg_lam"])
    y_rg, xc, a_rg, h_rg = rg_fwd(proj, *rg_prm, name="rg_fwd")
    ycat = jnp.concatenate([y_ssd, y_s5, y_rg], axis=1)
    pre1, h1 = mm_ln(ycat, p["w_out"], h0, p["g1"], p["b1"], name="out_proj")
    q = mm(h1, p["xa_wq"], name="xa_q", out_dtype=BF16)
    k = mm(mem, p["xa_wk"], name="xa_kv")
    v = mm(mem, p["xa_wv"], name="xa_kv")
    (o,), _ = rowk(_attn_fwd_fn, [(q, D_MODEL, 0)], [k, v], [D_MODEL], [], rows=t, name="xa_fwd", out_dtypes=[BF16])
    pre2, h2 = mm_ln(o, p["xa_wo"], h1, p["g2"], p["b2"], name="xa_o")
    a_mlp = _take_side(mm(h2, p["mlp_w1"], tb=True, name="mlp_up", side=sides.get("mlp_up")), 1, got)
    pre3, h3 = _take_side(mm_ln(a_mlp, p["mlp_w2"], h2, p["g3"], p["b3"], fa=_relu2, name="mlp_down",
                                side=sides.get("mlp_down")), 2, got)
    s.update(proj=proj, yraw=yraw, sall=sall, hs5=hs5, ylin=ylin, xc=xc, a_rg=a_rg, h_rg=h_rg,
             ycat=ycat, pre1=pre1, h1=h1, q=q, k=k, v=v, o=o, pre2=pre2, h2=h2, a_mlp=a_mlp, pre3=pre3)
    return h3, s, got


def _layer_bwd(dh3, mem, p, s, l, gfull, gsmall, sides={}):
    t = dh3.shape[0]
    proj = s["proj"]
    dpre3, dg3, db3 = ln_bwd(s["pre3"], dh3, p["g3"], name="ln_bwd")
    got = []
    da = _take_side(mm(dpre3, p["mlp_w2"], tb=True, o_extra=(s["a_mlp"],), fo=lambda acc, a: acc * 2.0 * jnp.maximum(a, 0.0),
                       name="mlp_da", out_dtype=BF16, side=sides.get("mlp_da")), 1, got)
    gfull["mlp_w2"][l] = mm(s["a_mlp"], dpre3, ta=True, fa=_relu2, name="mlp_dw2")
    gfull["mlp_w1"][l] = mm(da, s["h2"], ta=True, name="mlp_dw1")
    dh2 = _take_side(mm(da, p["mlp_w1"], o_extra=(dpre3,), fo=_add_alpha, name="mlp_dx", side=sides.get("mlp_dx")), 1, got)
    dpre2, dg2, db2 = ln_bwd(s["pre2"], dh2, p["g2"], name="ln_bwd")
    do = _take_side(mm(dpre2, p["xa_wo"], tb=True, name="xa_do", out_dtype=BF16, side=sides.get("xa_do")), 1, got)
    gfull["xa_wo"][l] = mm(s["o"], dpre2, ta=True, name="dw_sq")
    (dq,), (dk, dv) = rowk(_attn_bwd_fn, [(s["q"], D_MODEL, 0), (do, D_MODEL, 0)], [s["k"], s["v"]], [D_MODEL],
                           [(256, D_MODEL), (256, D_MODEL)], rows=t, name="xa_bwd", out_dtypes=[BF16])
    gfull["xa_wq"][l] = mm(s["h1"], dq, ta=True, name="dw_sq")
    gfull["xa_wk"][l] = mm(mem, dk, ta=True, name="dw_kv")
    gfull["xa_wv"][l] = mm(mem, dv, ta=True, name="dw_kv")
    dh1 = mm(dq, p["xa_wq"], tb=True, o_extra=(dpre2,), fo=_add_alpha, name="dx_sq")
    dpre1, dg1, db1 = ln_bwd(s["pre1"], dh1, p["g1"], name="ln_bwd")
    dycat = mm(dpre1, p["w_out"], tb=True, name="xa_do")
    gfull["w_out"][l] = mm(s["ycat"], dpre1, ta=True, name="dw_sq")
    rg_prm = (p["rg_cw"], p["rg_cb"], p["rg_wa"], p["rg_wx"], p["rg_ba"], p["rg_bx"], p["rg_lam"])
    dxr, dg_rg, d_rgcw, d_rgcb, dwa, dwx, dba, dbx, dlam = rg_bwd(proj, dycat, s["xc"], s["a_rg"], s["h_rg"], *rg_prm, name="rg_bwd")
    (dylin, du_a), (d_s5d, d_gluw, d_glub) = rowk(
        _s5_post_bwd_fn, [(s["ylin"], S5_WIDTH, 0), (proj, S5_WIDTH, P_U // S5_WIDTH), (dycat, S5_WIDTH, 2)],
        [p["s5_d"], p["s5_glu_w"], p["s5_glu_b"]], [S5_WIDTH, S5_WIDTH],
        [(1, S5_WIDTH), (S5_WIDTH, S5_WIDTH), (1, S5_WIDTH)], rows=t, name="s5_post_bwd")
    du, dccat, dbcat, dar, dai = s5_bwd(dylin, du_a, s["hs5"], proj, p["bcat"], p["lam_adj"], p["ccat"], name="s5_bwd")
    dxbc, dz, ddt, dprm, ddx, dnw, d_scw, d_scb = ssd_bwd(proj, p["ssd_cw"], p["ssd_cb"], p["prow"], p["ssd_dx"], p["ssd_nw"],
                                                         s["yraw"], s["sall"], dycat, name="ssd_bwd")
    dproj = jnp.concatenate([dxbc, dz, du, dxr, dg_rg, ddt, jnp.zeros((t, D_INP - P_DT - LANE), BF16)], axis=1)
    dh0 = mm(dproj, p["w_inp"], o_extra=(dpre1,), fo=_add_alpha, name="in_proj_dx")
    dwp = mm(dproj, s["h0"], ta=True, name="in_proj_dw")
    gfull["w_in"][l] = jnp.concatenate([dwp[P_Z:P_Z + 512], dwp[P_XBC:P_XBC + 1024], dwp[P_DT:P_DT + 8],
                                        dwp[P_U:P_U + 256], dwp[P_XR:P_XR + 256], dwp[P_G:P_G + 256]], axis=0)
    gfull["ssd_conv_w"][l], gfull["rg_conv_w"][l], gfull["s5_glu_w"][l] = d_scw, d_rgcw, d_gluw
    ng, ns = S5_GROUPS, S5_STATE
    dbbr = jnp.swapaxes(_blockdiag_extract(dbcat[:, :S5_NSTATE], ng), 1, 2)
    dbbi = jnp.swapaxes(_blockdiag_extract(dbcat[:, S5_NSTATE:], ng), 1, 2)
    d_lr, d_li, d_ls, d_bre, d_bim = p["s5_vjp"]((dar.reshape(ng, ns), dai.reshape(ng, ns), dbbr, dbbi))
    gsmall["s5_lam_re"][l], gsmall["s5_lam_im"][l], gsmall["s5_log_step"][l] = d_lr, d_li, d_ls
    gsmall["s5_b_re"][l], gsmall["s5_b_im"][l] = d_bre, d_bim
    gsmall["s5_c_re"][l] = jnp.swapaxes(_blockdiag_extract(dccat[:S5_NSTATE], ng), 1, 2)
    gsmall["s5_c_im"][l] = -jnp.swapaxes(_blockdiag_extract(dccat[S5_NSTATE:], ng), 1, 2)
    gsmall["s5_d"][l], gsmall["s5_glu_b"][l] = d_s5d[0], d_glub[0]
    gsmall["ssd_conv_b"][l], gsmall["rg_conv_b"][l] = d_scb[0], d_rgcb[0]
    gsmall["ssd_dt_bias"][l], gsmall["ssd_a_log"][l] = dprm[0, :8], dprm[1, :8]
    gsmall["ssd_d"][l] = ddx.reshape(SSD_HEADS, SSD_HEAD_DIM).sum(axis=1)
    gsmall["ssd_norm_w"][l] = dnw[0]
    gsmall["rg_wa"][l], gsmall["rg_wx"][l] = _blockdiag_extract(dwa, RG_BLOCKS), _blockdiag_extract(dwx, RG_BLOCKS)
    gsmall["rg_ba"][l], gsmall["rg_bx"][l] = dba.reshape(RG_BLOCKS, RG_BLOCK_DIM), dbx.reshape(RG_BLOCKS, RG_BLOCK_DIM)
    gsmall["rg_lambda"][l] = dlam[0]
    for i, (dg, db) in zip((1, 2, 3), ((dg1, db1), (dg2, db2), (dg3, db3))):
        gsmall[f"ln{i}_g"][l], gsmall[f"ln{i}_b"][l] = dg[0], db[0]
    return dh0, got


def _step(a):
    h = a["x"][0]
    mem = a["mem"][0]
    t = h.shape[0]
    r4, r3 = PACK_ROWS // 4, 3 * PACK_ROWS // 8

    def my_shards(pre):
        return ({name: (jnp.swapaxes(a[pre + name], 1, 2) if tr else a[pre + name]) for name, tr, _ in BIG},
                [a[pre + name] for name, _ in TINY])

    def my_pack(pre, l):
        big, tiny = my_shards(pre)
        return _pack_layer({name: w[l] for name, w in big.items()},
                           jnp.concatenate([w.reshape(-1) for w in tiny]) if l == 0 else None)

    big, tiny = my_shards("")
    tiny16 = [(lax.bitcast_convert_type(w, BF16) if name in KEEP_F32 else w.astype(BF16)).reshape(-1)
              for (name, _), w in zip(TINY, tiny)]
    packed = [_pack_layer({name: w[l].astype(BF16) for name, w in big.items()}, jnp.concatenate(tiny16) if l == 0 else None)
              for l in range(DEPTH)]
    small = {name: a[name] for name in SMALL}

    def gathered_weights(g):
        gbig, gtiny = _unpack_layer(g)
        return {name: w.reshape(-1, WIDE) for name, w in gbig.items()}, gtiny

    full, gtiny = gathered_weights(all_gather(packed[0], name="ag_weights"))
    tiny_shapes = [w.shape + ((2,) if name in KEEP_F32 else ()) for (name, _), w in zip(TINY, tiny)]
    tiny_full = {name: _to_full(lax.bitcast_convert_type(g, F32) if name in KEEP_F32 else g, axis)
                 for (name, axis), g in zip(TINY, _split_flat(gtiny, tiny_shapes))}
    p0 = _layer_params({**full, **tiny_full}, small, 0)
    h, s0, got = _layer_fwd(h, mem, p0, sides={"in_proj": ("gather", packed[1], 0, r4), "mlp_up": ("gather", packed[1], r4, r3),
                                                "mlp_down": ("gather", packed[1], r4 + r3, r3)})
    full, _ = gathered_weights(jnp.concatenate(got, axis=1))
    p1 = _layer_params({**full, **tiny_full}, small, 1)
    h, s1, _ = _layer_fwd(h, mem, p1)
    (dh,), (loss_part,) = rowk(_loss_fn, [(h, D_MODEL, 0), (a["loss_target"][0], D_MODEL, 0)], [], [D_MODEL], [(1, 1)],
                               rows=t, name="loss_head")
    loss = lax.psum(loss_part[0, 0], ("x", "y", "c"))
    gfull = {name: [None] * DEPTH for name in SHARDED}
    gsmall = {name: [None] * DEPTH for name in SMALL}

    def chip_partials(l):
        gbig = {name: gfull[name][l].reshape(N_DEV, rows, WIDE) for name, _, rows in BIG}
        gtiny = None
        if l == 0:
            gtiny = jnp.concatenate([_to_slabs(jnp.stack(gfull[name]), axis).reshape(N_DEV, -1) for name, axis in TINY], axis=1)
        slabs = _pack_layer(gbig, gtiny)
        halves = jnp.swapaxes(slabs.reshape((4, 2) + slabs.shape[1:]), 0, 1)
        theirs = rs_sibling_exchange(halves, name="rs_sibling")
        return pair_sum_bf16(halves, theirs, name="rs_pair_sum")

    dh, _ = _layer_bwd(dh, mem, p1, s1, 1, gfull, gsmall)
    part1 = chip_partials(1)
    dh, got = _layer_bwd(dh, mem, p0, s0, 0, gfull, gsmall, sides={"mlp_da": ("chips", part1, 0, r3), "mlp_dx": ("chips", part1, r3, r3),
                                                                    "xa_do": ("chips", part1, 2 * r3, r4)})
    grad_x = dh[None]
    landed = [rs_chip_exchange(chip_partials(0), name="rs_chips"), jnp.concatenate(got, axis=1)]
    bigs = [adamw(landed[l], my_pack("", l), my_pack("m_", l), my_pack("v_", l), name="adamw_sharded", tt=128) for l in range(DEPTH)]
    gs = _pack_rows(jnp.concatenate([jnp.stack(gsmall[name]).reshape(-1) for name in SMALL]), 8)
    gs = all_gather(gs, name="ag_small_grads")
    pks = lambda pre: _pack_rows(jnp.concatenate([a[pre + name].reshape(-1) for name in SMALL]), 8)
    sm = adamw(gs, pks(""), pks("m_"), pks("v_"), name="adamw_replicated", tt=gs.shape[1])
    out = {}
    for i, kind in enumerate(("grad_", "delta_", "new_m_", "new_v_")):
        layers = [_unpack_layer(bigs[l][i]) for l in range(DEPTH)]
        for name, tr, _ in BIG:
            arr = jnp.stack([layers[l][0][name] for l in range(DEPTH)])
            out[kind + name] = jnp.swapaxes(arr, 1, 2) if tr else arr
        for (name, _), arr in zip(TINY, _split_flat(layers[0][1], [w.shape for w in tiny])):
            out[kind + name] = arr
        for name, arr in zip(SMALL, _unpack(sm[i], [a[name].shape for name in SMALL])):
            out[kind + name] = arr
    return (loss, grad_x) + tuple(out[kind + name] for kind in ("grad_", "delta_", "new_m_", "new_v_") for name in WEIGHTS)


def kernel(x, mem, w_in, w_out, ssd_conv_w, ssd_conv_b, ssd_dt_bias, ssd_a_log, ssd_d, ssd_norm_w, s5_lam_re, s5_lam_im, s5_log_step, s5_b_re, s5_b_im, s5_c_re, s5_c_im, s5_d, s5_glu_w, s5_glu_b, rg_conv_w, rg_conv_b, rg_wa, rg_ba, rg_wx, rg_bx, rg_lambda, ln1_g, ln1_b, xa_wq, xa_wk, xa_wv, xa_wo, ln2_g, ln2_b, mlp_w1, mlp_w2, ln3_g, ln3_b, loss_target, m_w_in, m_w_out, m_ssd_conv_w, m_ssd_conv_b, m_ssd_dt_bias, m_ssd_a_log, m_ssd_d, m_ssd_norm_w, m_s5_lam_re, m_s5_lam_im, m_s5_log_step, m_s5_b_re, m_s5_b_im, m_s5_c_re, m_s5_c_im, m_s5_d, m_s5_glu_w, m_s5_glu_b, m_rg_conv_w, m_rg_conv_b, m_rg_wa, m_rg_ba, m_rg_wx, m_rg_bx, m_rg_lambda, m_ln1_g, m_ln1_b, m_xa_wq, m_xa_wk, m_xa_wv, m_xa_wo, m_ln2_g, m_ln2_b, m_mlp_w1, m_mlp_w2, m_ln3_g, m_ln3_b, v_w_in, v_w_out, v_ssd_conv_w, v_ssd_conv_b, v_ssd_dt_bias, v_ssd_a_log, v_ssd_d, v_ssd_norm_w, v_s5_lam_re, v_s5_lam_im, v_s5_log_step, v_s5_b_re, v_s5_b_im, v_s5_c_re, v_s5_c_im, v_s5_d, v_s5_glu_w, v_s5_glu_b, v_rg_conv_w, v_rg_conv_b, v_rg_wa, v_rg_ba, v_rg_wx, v_rg_bx, v_rg_lambda, v_ln1_g, v_ln1_b, v_xa_wq, v_xa_wk, v_xa_wv, v_xa_wo, v_ln2_g, v_ln2_b, v_mlp_w1, v_mlp_w2, v_ln3_g, v_ln3_b):
    return _step(dict(locals()))
```

```python
import math

import jax
import jax.numpy as jnp
from jax import lax
from jax.experimental import pallas as pl
from jax.experimental.pallas import tpu as pltpu

F32 = jnp.float32
BF16 = jnp.bfloat16

N_DEV = 8
D_MODEL = 1024
DEPTH = 2
SSD_WIDTH = 512
SSD_HEADS = 8
SSD_HEAD_DIM = 64
SSD_STATE = 128
SSD_CHUNK = 128
SSD_XBC = 1024
S5_WIDTH = 256
S5_GROUPS = 16
S5_GROUP_CH = 16
S5_STATE = 64
S5_NSTATE = S5_GROUPS * S5_STATE
RG_WIDTH = 256
RG_BLOCKS = 4
RG_BLOCK_DIM = 64
RG_C = 8.0
XA_HEADS = 4
XA_HEAD_DIM = 256
ALPHA = (2.0 * DEPTH) ** 0.25
LN_EPS = 1e-5
ADAM_LR, ADAM_B1, ADAM_B2, ADAM_EPS, ADAM_WD, ADAM_STEP = 0.001, 0.9, 0.999, 1e-08, 0.01, 10

P_XBC, P_Z, P_U, P_XR, P_G, P_DT = 0, 1024, 1536, 1792, 2048, 2304
D_INP = 2560
LANE = 128
VMEM_LIMIT = 56 * 1024 * 1024
ROW_TILE = 512

_NN = ((1,), (0,))
_NT = ((1,), (1,))
_TN = ((0,), (0,))


def _dot(a, b, dims=_NN):
    return lax.dot_general(a.astype(BF16), b.astype(BF16), (dims, ((), ())), preferred_element_type=F32)


def _split_bf16(x, parts):
    out, rem = [], x
    for _ in range(parts):
        piece = rem.astype(BF16)
        out.append(piece)
        rem = rem - piece.astype(F32)
    return out


def _dot_mask(a, b, dims=_NN, *, mask_left, parts):
    if mask_left:
        return sum(_dot(a, piece, dims) for piece in _split_bf16(b, parts))
    return sum(_dot(piece, b, dims) for piece in _split_bf16(a, parts))


def _sigmoid(x):
    return 1.0 / (1.0 + jnp.exp(-x))


def _silu(x):
    return x * _sigmoid(x)


def _dsilu(x):
    s = _sigmoid(x)
    return s * (1.0 + x * (1.0 - s))


_GK = math.sqrt(2.0 / math.pi)
_GC = 0.044715


def _gelu(x):
    return 0.5 * x * (1.0 + jnp.tanh(_GK * (x + _GC * x * x * x)))


def _dgelu(x):
    th = jnp.tanh(_GK * (x + _GC * x * x * x))
    return 0.5 * (1.0 + th) + 0.5 * x * (1.0 - th * th) * _GK * (1.0 + 3.0 * _GC * x * x)


def _log1p_pos(e):
    return jnp.where(e < 1e-2, e * (1.0 - e * (0.5 - e * (1.0 / 3.0))), jnp.log(1.0 + e))


def _softplus(x):
    return jnp.maximum(x, 0.0) + _log1p_pos(jnp.exp(-jnp.abs(x)))


def _neg_expm1(x):
    poly = -x * (1.0 + x * (0.5 + x * (1.0 / 6.0 + x * (1.0 / 24.0 + x * (1.0 / 120.0)))))
    return jnp.where(x > -0.05, poly, 1.0 - jnp.exp(x))


def _params(sem):
    return pltpu.CompilerParams(dimension_semantics=sem, vmem_limit_bytes=VMEM_LIMIT)


RESIDENT_BYTES = 8 * 1024 * 1024
STREAM_BYTES = 4 * 1024 * 1024


def _halve_to_fit(dims, bytes_per, limit):
    dims = list(dims)
    while math.prod(dims) * bytes_per > limit:
        i = max(range(len(dims)), key=lambda d: dims[d])
        assert dims[i] % 256 == 0, dims
        dims[i] //= 2
    return dims


def _side_exchange(side, src, dst, sems, step, nsteps):
    kind, _, r0, rows = side
    span = pl.ds(r0, rows)
    if kind == "gather":
        phases = lambda: _ag_phases(src.at[span], dst, *sems)
        when = (0, (3 * nsteps) // 4, nsteps - 1)
    else:
        phases = lambda: _rs_chip_phases(src, dst, *sems, rows=span)
        when = (0, nsteps - 1)
    for idx, at in enumerate(when):
        pl.when(step == at)(lambda idx=idx: phases()[idx]())


def mm(a, b, *, name, ta=False, tb=False, a_extra=(), fa=None, o_extra=(), r_extra=(), fo=None, n_out=1,
       a_off=0, m=None, k=None, out_dtype=F32, side=None):
    n = b.shape[0] if tb else b.shape[1]
    na, no, nr = 1 + len(a_extra), len(o_extra), len(r_extra)
    if not ta:
        assert m is None
        m, kdim = a.shape[0], (a.shape[1] if k is None else k)
        assert a_off % kdim == 0
        (tn,) = _halve_to_fit([n], kdim * b.dtype.itemsize, RESIDENT_BYTES)
        (tm,) = _halve_to_fit([min(512, m)], max(tn, kdim) * 4, STREAM_BYTES)
        a_spec = pl.BlockSpec((tm, kdim), lambda i, j: (i, a_off // kdim))
        b_spec = pl.BlockSpec((tn, kdim), lambda i, j: (j, 0)) if tb else pl.BlockSpec((kdim, tn), lambda i, j: (0, j))
        o_spec = pl.BlockSpec((tm, tn), lambda i, j: (i, j))
        dims = _NT if tb else _NN

        r_spec = pl.BlockSpec((1, tn), lambda i, j: (0, j))

        grid = (m // tm, n // tn)
        nin = na + 1 + no + nr

        def body(*refs):
            a_refs, b_ref, e_refs, out_refs = refs[:na], refs[na], refs[na + 1:nin], refs[nin + (side is not None):nin + (side is not None) + n_out]
            if side is not None:
                _side_exchange(side, refs[nin], refs[nin + 1 + n_out], refs[nin + 2 + n_out:],
                               pl.program_id(0) * grid[1] + pl.program_id(1), grid[0] * grid[1])
            av = a_refs[0][...] if fa is None else fa(*[r[...] for r in a_refs])
            acc = _dot(av, b_ref[...], dims)
            res = acc if fo is None else fo(acc, *[r[...] for r in e_refs])
            for r, v in zip(out_refs, res if n_out > 1 else (res,)):
                r[...] = v.astype(r.dtype)

        sem = ("parallel", "parallel") if side is None else ("arbitrary", "arbitrary")
    else:
        assert k is None and not tb and fo is None and not o_extra and not r_extra and n_out == 1 and out_dtype == F32
        assert side is None
        kdim, m = a.shape[0], (a.shape[1] if m is None else m)
        r_spec = None
        tm, tn = _halve_to_fit([m, n], 4, RESIDENT_BYTES)
        (tk,) = _halve_to_fit([min(512, kdim)], max(tm, tn) * 4, STREAM_BYTES)
        assert a_off % tm == 0
        a_spec = pl.BlockSpec((tk, tm), lambda i, j, kk: (kk, i + a_off // tm))
        b_spec = pl.BlockSpec((tk, tn), lambda i, j, kk: (kk, j))
        o_spec = pl.BlockSpec((tm, tn), lambda i, j, kk: (i, j))

        def body(*refs):
            a_refs, b_ref, out_ref = refs[:na], refs[na], refs[na + 1]

            @pl.when(pl.program_id(2) == 0)
            def _():
                out_ref[...] = jnp.zeros_like(out_ref)

            av = a_refs[0][...] if fa is None else fa(*[r[...] for r in a_refs])
            out_ref[...] += _dot(av, b_ref[...], _TN)

        grid, sem = (m // tm, n // tn, kdim // tk), ("parallel", "parallel", "arbitrary")
    assert m % tm == 0 and n % tn == 0, (name, m, n, tm, tn)
    out = jax.ShapeDtypeStruct((m, n), out_dtype)
    if side is None:
        return pl.pallas_call(
            body, name=name, grid=grid,
            in_specs=[a_spec] * na + [b_spec] + [o_spec] * no + [r_spec] * nr,
            out_specs=o_spec if n_out == 1 else [o_spec] * n_out, out_shape=out if n_out == 1 else [out] * n_out,
            compiler_params=_params(sem),
        )(a, *a_extra, b, *o_extra, *r_extra)
    kind, arr, _, rows = side
    landed = jax.ShapeDtypeStruct(((N_DEV, rows) if kind == "gather" else (4, rows)) + arr.shape[-1:], arr.dtype)
    return pl.pallas_call(
        body, name=name, grid=grid,
        in_specs=[a_spec] * na + [b_spec] + [o_spec] * no + [r_spec] * nr + [_ANY],
        out_specs=[o_spec] * n_out + [_ANY], out_shape=[out] * n_out + [landed],
        scratch_shapes=list(_AG_SEMS if kind == "gather" else _RS_SEMS),
        compiler_params=_params(sem),
    )(a, *a_extra, b, *o_extra, *r_extra, arr)


def rowk(fn, tiled, full, out_w, acc_shapes, *, rows, name, out_dtypes=None):
    tt = min(ROW_TILE, rows)
    n = rows // tt
    assert rows % tt == 0
    nt, nf, no = len(tiled), len(full), len(out_w)

    def tspec(w, cb):
        return pl.BlockSpec((tt, w), lambda i: (i, cb))

    def fspec(a):
        nd = a.ndim
        return pl.BlockSpec(a.shape, lambda i: (0,) * nd)

    def body(*refs):
        ins, fulls = refs[:nt], refs[nt:nt + nf]
        outs, accs = refs[nt + nf:nt + nf + no], refs[nt + nf + no:]
        res_t, res_a = fn(*[r[...] for r in ins], *[r[...] for r in fulls])
        for r, v in zip(outs, res_t):
            r[...] = v.astype(r.dtype)
        if accs:
            @pl.when(pl.program_id(0) == 0)
            def _():
                for r in accs:
                    r[...] = jnp.zeros_like(r)
            for r, v in zip(accs, res_a):
                r[...] += v

    outs = pl.pallas_call(
        body, name=name, grid=(n,),
        in_specs=[tspec(w, cb) for (_, w, cb) in tiled] + [fspec(a) for a in full],
        out_specs=[tspec(w, 0) for w in out_w] + [pl.BlockSpec(s, lambda i, nd=len(s): (0,) * nd) for s in acc_shapes],
        out_shape=[jax.ShapeDtypeStruct((rows, w), dt) for w, dt in zip(out_w, out_dtypes or [F32] * no)]
        + [jax.ShapeDtypeStruct(s, F32) for s in acc_shapes],
        compiler_params=_params(("arbitrary",)),
    )(*[a for (a, _, _) in tiled], *full)
    return outs[:no], outs[no:]


def _colsum(x):
    return jnp.sum(x, axis=0, keepdims=True)


def _rowsum(x):
    return jnp.sum(x, axis=1, keepdims=True)


def _ln_epilogue(acc, resid, g, b):
    pre = ALPHA * resid + acc
    mu = jnp.mean(pre, axis=1, keepdims=True)
    xc = pre - mu
    var = jnp.mean(xc * xc, axis=1, keepdims=True)
    return pre, xc * lax.rsqrt(var + LN_EPS) * g + b


def _ln_bwd_fn(pre, dout, g):
    mu = jnp.mean(pre, axis=1, keepdims=True)
    xc = pre - mu
    var = jnp.mean(xc * xc, axis=1, keepdims=True)
    rstd = lax.rsqrt(var + LN_EPS)
    xhat = xc * rstd
    dxh = dout * g
    dpre = rstd * (dxh - jnp.mean(dxh, axis=1, keepdims=True) - xhat * jnp.mean(dxh * xhat, axis=1, keepdims=True))
    return (dpre,), (_colsum(dout * xhat), _colsum(dout))


def mm_ln(a, w, resid, g, b, *, name, fa=None, side=None):
    assert w.shape[1] == D_MODEL
    return mm(a, w, fa=fa, o_extra=(resid,), r_extra=(g, b), fo=_ln_epilogue, n_out=2, name=name, side=side)


def ln_bwd(pre, dout, g, *, name):
    (dpre,), (dg, db) = rowk(_ln_bwd_fn, [(pre, D_MODEL, 0), (dout, D_MODEL, 0)], [g],
                             [D_MODEL], [(1, D_MODEL), (1, D_MODEL)], rows=pre.shape[0], name=name)
    return dpre, dg, db


def _loss_fn(y, tgt):
    e = y - tgt
    part = _colsum(_rowsum(e * e)) * (0.5 / D_MODEL)
    return (e * (1.0 / D_MODEL),), (part,)


_XA_SCALE = 1.0 / math.sqrt(XA_HEAD_DIM)


def _attn_probs(qh, kh):
    s = _dot(qh, kh, _NT) * _XA_SCALE
    e = jnp.exp(s - jnp.max(s, axis=1, keepdims=True))
    return e / _rowsum(e)


def _attn_fwd_fn(q, k, v):
    outs = []
    for hd in range(XA_HEADS):
        sl = slice(hd * XA_HEAD_DIM, (hd + 1) * XA_HEAD_DIM)
        outs.append(_dot(_attn_probs(q[:, sl], k[:, sl]), v[:, sl]))
    return (jnp.concatenate(outs, axis=1),), ()


def _attn_bwd_fn(q, do, k, v):
    dqs, dks, dvs = [], [], []
    for hd in range(XA_HEADS):
        sl = slice(hd * XA_HEAD_DIM, (hd + 1) * XA_HEAD_DIM)
        qh, kh, vh, doh = q[:, sl], k[:, sl], v[:, sl], do[:, sl]
        p = _attn_probs(qh, kh)
        dp = _dot(doh, vh, _NT)
        ds = p * (dp - _rowsum(p * dp)) * _XA_SCALE
        dqs.append(_dot(ds, kh))
        dks.append(_dot(ds, qh, _TN))
        dvs.append(_dot(p, doh, _TN))
    cat = lambda xs: jnp.concatenate(xs, axis=1)
    return (cat(dqs),), (cat(dks), cat(dvs))


def _s5_post_fwd_fn(ylin, u, dskip, gw, gb):
    yg = _gelu(ylin + dskip * u)
    return (yg * _sigmoid(_dot(yg, gw) + gb),), ()


def _s5_post_bwd_fn(ylin, u, dout, dskip, gw, gb):
    pre = ylin + dskip * u
    yg = _gelu(pre)
    sg = _sigmoid(_dot(yg, gw) + gb)
    dlin = dout * yg * sg * (1.0 - sg)
    dyg = dout * sg + _dot(dlin, gw, _NT)
    dpre = dyg * _dgelu(pre)
    return (dpre, dpre * dskip), (_colsum(dpre * u), _dot(yg, dlin, _TN), _colsum(dlin))


def _rg_gates(xc, wa, wx, ba, bx, lam):
    r = _sigmoid(_dot(xc, wa) + ba)
    i = _sigmoid(_dot(xc, wx) + bx)
    sp = _softplus(-lam)
    log_a = -RG_C * r * sp
    a = jnp.exp(log_a)
    mult = jnp.sqrt(_neg_expm1(2.0 * log_a))
    return r, i, sp, a, mult


def _rg_pre_bwd_fn(xc, gsc, hprev, wa, wx, ba, bx, lam):
    r, i, sp, a, mult = _rg_gates(xc, wa, wx, ba, bx, lam)
    da = gsc * hprev
    db = gsc
    dmult = db * i * xc
    di = db * mult * xc
    dxc = db * mult * i
    dlog_a = da * a - a * a * dmult / mult
    dr = dlog_a * (-RG_C * sp)
    dsp = _colsum(dlog_a * (-RG_C * r))
    dlam = dsp * (-_sigmoid(-lam))
    dpr = dr * r * (1.0 - r)
    dpi = di * i * (1.0 - i)
    dxc = dxc + _dot(dpr, wa, _NT) + _dot(dpi, wx, _NT)
    return (dxc,), (_dot(xc, dpr, _TN), _dot(xc, dpi, _TN), _colsum(dpr), _colsum(dpi), dlam)


def _conv_taps(x_ref, halo_ref, first):
    x = x_ref[...]
    halo = jnp.where(first, 0.0, halo_ref[...])
    rows8 = lax.broadcasted_iota(jnp.int32, halo.shape, 0)
    taps = [x]
    for j in (1, 2, 3):
        r = pltpu.roll(x, j, 0)
        top = jnp.where(rows8 < j, pltpu.roll(halo, j, 0), r[0:8])
        taps.append(jnp.concatenate([top, r[8:]], axis=0))
    return taps


def _conv_pre(taps, cw_ref, cb_ref):
    wv = cw_ref[...]
    pre = cb_ref[...] + wv[3:4, :] * taps[0]
    for j in (1, 2, 3):
        pre = pre + wv[3 - j:4 - j, :] * taps[j]
    return pre


def _conv_back(dpre, taps, cw_ref, nxt_ref):
    q = dpre.shape[0]
    rows8 = lax.broadcasted_iota(jnp.int32, (8, dpre.shape[1]), 0)
    wv = cw_ref[...]
    dx = wv[3:4, :] * dpre
    for j in (1, 2, 3):
        r = pltpu.roll(dpre, q - j, 0)
        bottom = jnp.where(rows8 >= 8 - j, pltpu.roll(nxt_ref[...], 8 - j, 0), r[q - 8:q])
        dx = dx + wv[3 - j:4 - j, :] * jnp.concatenate([r[:q - 8], bottom], axis=0)
    dw = jnp.concatenate([_colsum(dpre * taps[3 - kk]) for kk in range(4)], axis=0)
    nxt_ref[...] = dpre[0:8]
    return dx, dw, _colsum(dpre)


S5_CW = 256


def _cmul(ar, ai, br, bi):
    return ar * br - ai * bi, ar * bi + ai * br


def _scan8_complex(src_ref, dst_ref, lam_ref, st_ref, *, w, nb, reverse):
    rows = lax.broadcasted_iota(jnp.int32, (8, S5_CW), 0)
    b8 = lambda v: jnp.broadcast_to(v, (8, S5_CW))

    def shift(x, k):
        if reverse:
            return jnp.where(rows < 8 - k, pltpu.roll(x, 8 - k, 0), 0.0)
        return jnp.where(rows >= k, pltpu.roll(x, k, 0), 0.0)

    for c0 in range(0, w, S5_CW):
        re, im = pl.ds(c0, S5_CW), pl.ds(w + c0, S5_CW)
        pw = [(lam_ref[:, re], lam_ref[:, im])]
        for _ in range(7):
            pw.append(_cmul(*pw[-1], *pw[0]))
        pr, pi = b8(pw[7][0]), b8(pw[7][1])
        for j in range(7):
            sel = rows == (7 - j if reverse else j)
            pr, pi = jnp.where(sel, b8(pw[j][0]), pr), jnp.where(sel, b8(pw[j][1]), pi)
        steps = [(k, b8(pw[k - 1][0]), b8(pw[k - 1][1])) for k in (1, 2, 4)]
        edge = 0 if reverse else 7

        def blk(i, carry):
            hr, hi = carry
            base = pl.multiple_of((nb // 2 - 1 - i if reverse else i) * 16, 16)
            pend = []
            for off in ((8, 0) if reverse else (0, 8)):
                at = pl.ds(base + off, 8)
                xr, xi = src_ref[at, re], src_ref[at, im]
                for k, kr, ki in steps:
                    sr, si = shift(xr, k), shift(xi, k)
                    xr, xi = xr + kr * sr - ki * si, xi + kr * si + ki * sr
                pend.append((at, xr, xi))
            for at, xr, xi in pend:
                xr, xi = xr + pr * hr - pi * hi, xi + pr * hi + pi * hr
                dst_ref[at, re] = xr
                dst_ref[at, im] = xi
                hr, hi = b8(xr[edge:edge + 1, :]), b8(xi[edge:edge + 1, :])
            return hr, hi

        hr, hi = lax.fori_loop(0, nb // 2, blk, (st_ref[:, re], st_ref[:, im]))
        st_ref[:, re] = hr
        st_ref[:, im] = hi


def s5_fwd(proj, bcat, lam, ccat, dskip, gw, gb, *, name):
    t = proj.shape[0]
    tt = min(ROW_TILE, t)
    w2 = bcat.shape[1]

    def body(u_ref, b_ref, lam_ref, c_ref, d_ref, gw_ref, gb_ref, h_ref, y_ref, o_ref, bu_ref, st_ref):
        @pl.when(pl.program_id(0) == 0)
        def _():
            st_ref[...] = jnp.zeros_like(st_ref)

        u = u_ref[...]
        bu_ref[...] = _dot(u, b_ref[...])
        _scan8_complex(bu_ref, h_ref, lam_ref, st_ref, w=w2 // 2, nb=tt // 8, reverse=False)
        ylin = _dot(h_ref[...], c_ref[...])
        y_ref[...] = ylin
        (out,), _ = _s5_post_fwd_fn(ylin, u, d_ref[...], gw_ref[...], gb_ref[...])
        o_ref[...] = out.astype(o_ref.dtype)

    fixed = lambda a: pl.BlockSpec(a.shape, lambda i: (0, 0))
    row = pl.BlockSpec((tt, S5_WIDTH), lambda i: (i, 0))
    return pl.pallas_call(
        body, name=name, grid=(t // tt,),
        in_specs=[pl.BlockSpec((tt, S5_WIDTH), lambda i: (i, P_U // S5_WIDTH))] + [fixed(x) for x in (bcat, lam, ccat, dskip, gw, gb)],
        out_specs=[pl.BlockSpec((tt, w2), lambda i: (i, 0)), row, row],
        out_shape=[jax.ShapeDtypeStruct((t, w2), F32), jax.ShapeDtypeStruct((t, S5_WIDTH), F32),
                   jax.ShapeDtypeStruct((t, S5_WIDTH), BF16)],
        scratch_shapes=[pltpu.VMEM((tt, w2), F32), pltpu.VMEM((8, w2), F32)],
        compiler_params=_params(("arbitrary",)),
    )(proj, bcat, lam, ccat, dskip, gw, gb)


def s5_bwd(dycat, ylin, hs, proj, bcat, lam_adj, ccat, dskip, gw, gb, *, name):
    t = proj.shape[0]
    tt = min(ROW_TILE, t)
    n, w2 = t // tt, bcat.shape[1]
    w = w2 // 2

    def body(dout_ref, yl_ref, h_ref, hp_ref, u_ref, b_ref, lam_ref, c_ref, d_ref, gw_ref, gb_ref,
             du_ref, dc_ref, db_ref, dar_ref, dai_ref, dd_ref, dgw_ref, dgb_ref, g_ref, st_ref):
        i = pl.program_id(0)

        @pl.when(i == 0)
        def _():
            for r in (st_ref, dc_ref, db_ref, dar_ref, dai_ref, dd_ref, dgw_ref, dgb_ref):
                r[...] = jnp.zeros_like(r)

        (dy, du_a), post = _s5_post_bwd_fn(yl_ref[...], u_ref[...], dout_ref[...], d_ref[...], gw_ref[...], gb_ref[...])
        for r, v in zip((dd_ref, dgw_ref, dgb_ref), post):
            r[...] += v
        h = h_ref[...]
        g_ref[...] = _dot(dy, c_ref[...], _NT)
        dc_ref[...] += _dot(h, dy, _TN)
        _scan8_complex(g_ref, g_ref, lam_ref, st_ref, w=w, nb=tt // 8, reverse=True)
        g = g_ref[...]
        du_ref[...] = (du_a + _dot(g, b_ref[...], _NT)).astype(du_ref.dtype)
        db_ref[...] += _dot(u_ref[...], g, _TN)
        rows = lax.broadcasted_iota(jnp.int32, (tt, w2), 0)
        before = jnp.where(i == n - 1, 0.0, hp_ref[7:8, :])
        hprev = jnp.where(rows == 0, before, pltpu.roll(h, 1, 0))
        gr, gi, hr, hi = g[:, :w], g[:, w:], hprev[:, :w], hprev[:, w:]
        dar_ref[...] += _colsum(gr * hr + gi * hi)
        dai_ref[...] += _colsum(gi * hr - gr * hi)

    rev = lambda i: n - 1 - i
    row = lambda wd, cb=0: pl.BlockSpec((tt, wd), lambda i: (rev(i), cb))
    fixed = lambda shape: pl.BlockSpec(shape, lambda i: (0, 0))
    return pl.pallas_call(
        body, name=name, grid=(n,),
        in_specs=[row(S5_WIDTH, 2), row(S5_WIDTH), row(w2),
                  pl.BlockSpec((8, w2), lambda i: (jnp.maximum(rev(i) * (tt // 8) - 1, 0), 0)),
                  row(S5_WIDTH, P_U // S5_WIDTH)] + [fixed(x.shape) for x in (bcat, lam_adj, ccat, dskip, gw, gb)],
        out_specs=[row(S5_WIDTH), fixed(ccat.shape), fixed(bcat.shape), fixed((1, w)), fixed((1, w)),
                   fixed((1, S5_WIDTH)), fixed((S5_WIDTH, S5_WIDTH)), fixed((1, S5_WIDTH))],
        out_shape=[jax.ShapeDtypeStruct((t, S5_WIDTH), BF16), jax.ShapeDtypeStruct(ccat.shape, F32),
                   jax.ShapeDtypeStruct(bcat.shape, F32), jax.ShapeDtypeStruct((1, w), F32), jax.ShapeDtypeStruct((1, w), F32),
                   jax.ShapeDtypeStruct((1, S5_WIDTH), F32), jax.ShapeDtypeStruct((S5_WIDTH, S5_WIDTH), F32),
                   jax.ShapeDtypeStruct((1, S5_WIDTH), F32)],
        scratch_shapes=[pltpu.VMEM((tt, w2), F32), pltpu.VMEM((8, w2), F32)],
        compiler_params=_params(("arbitrary",)),
    )(dycat, ylin, hs, hs, proj, bcat, lam_adj, ccat, dskip, gw, gb)


def _scan8_real(a_ref, b_ref, o_ref, st_ref, *, nb, reverse):
    w = o_ref.shape[1]
    rows = lax.broadcasted_iota(jnp.int32, (8, w), 0)

    def blk(i, h):
        at = pl.ds(pl.multiple_of((nb - 1 - i if reverse else i) * 8, 8), 8)
        ta_, tb_ = a_ref[at, :], b_ref[at, :]
        out = jnp.zeros((8, w), F32)
        for j in (range(7, -1, -1) if reverse else range(8)):
            h = jnp.broadcast_to(ta_[j:j + 1, :], (8, w)) * h + jnp.broadcast_to(tb_[j:j + 1, :], (8, w))
            out = jnp.where(rows == j, h, out)
        o_ref[at, :] = out
        return h

    st_ref[...] = lax.fori_loop(0, nb, blk, st_ref[...])


def _rg_specs(tt, idx):
    return [pl.BlockSpec((tt, RG_WIDTH), lambda i: (idx(i), P_XR // RG_WIDTH)),
            pl.BlockSpec((8, RG_WIDTH), lambda i: (jnp.maximum(idx(i) * (tt // 8) - 1, 0), P_XR // RG_WIDTH)),
            pl.BlockSpec((tt, RG_WIDTH), lambda i: (idx(i), P_G // RG_WIDTH))]


def rg_fwd(proj, cw, cb, wa, wx, ba, bx, lam, *, name):
    t = proj.shape[0]
    tt = min(ROW_TILE, t)
    w = RG_WIDTH

    def body(x_ref, halo_ref, g_ref, cw_ref, cb_ref, wa_ref, wx_ref, ba_ref, bx_ref, lam_ref,
             y_ref, xc_ref, a_ref, h_ref, b_ref, st_ref):
        @pl.when(pl.program_id(0) == 0)
        def _():
            st_ref[...] = jnp.zeros_like(st_ref)

        xc = _conv_pre(_conv_taps(x_ref, halo_ref, pl.program_id(0) == 0), cw_ref, cb_ref)
        xc_ref[...] = xc
        r, i, sp, a, mult = _rg_gates(xc, wa_ref[...], wx_ref[...], ba_ref[...], bx_ref[...], lam_ref[...])
        a_ref[...] = a
        b_ref[...] = mult * (i * xc)
        _scan8_real(a_ref, b_ref, h_ref, st_ref, nb=tt // 8, reverse=False)
        y_ref[...] = (h_ref[...] * _gelu(g_ref[...])).astype(y_ref.dtype)

    fixed = lambda a: pl.BlockSpec(a.shape, lambda i: (0, 0))
    row = pl.BlockSpec((tt, w), lambda i: (i, 0))
    return pl.pallas_call(
        body, name=name, grid=(t // tt,),
        in_specs=_rg_specs(tt, lambda i: i) + [fixed(x) for x in (cw, cb, wa, wx, ba, bx, lam)],
        out_specs=[row] * 4,
        out_shape=[jax.ShapeDtypeStruct((t, w), BF16)] + [jax.ShapeDtypeStruct((t, w), F32)] * 3,
        scratch_shapes=[pltpu.VMEM((tt, w), F32), pltpu.VMEM((8, w), F32)],
        compiler_params=_params(("arbitrary",)),
    )(proj, proj, proj, cw, cb, wa, wx, ba, bx, lam)


def rg_bwd(proj, dycat, xc, a, h, cw, cb, wa, wx, ba, bx, lam, *, name):
    t = proj.shape[0]
    tt = min(ROW_TILE, t)
    n, w = t // tt, RG_WIDTH

    def body(x_ref, halo_ref, g_ref, dy_ref, xc_ref, a_ref, h_ref, hp_ref, cw_ref, wa_ref, wx_ref, ba_ref, bx_ref, lam_ref,
             dx_ref, dg_ref, dcw_ref, dcb_ref, dwa_ref, dwx_ref, dba_ref, dbx_ref, dlam_ref,
             au_ref, dh_ref, gs_ref, st_ref, anx_ref, nxt_ref):
        i = pl.program_id(0)
        accs = (dcw_ref, dcb_ref, dwa_ref, dwx_ref, dba_ref, dbx_ref, dlam_ref)

        @pl.when(i == 0)
        def _():
            for r in accs + (st_ref, anx_ref, nxt_ref):
                r[...] = jnp.zeros_like(r)

        h, g, dy, a = h_ref[...], g_ref[...], dy_ref[...], a_ref[...]
        dh_ref[...] = dy * _gelu(g)
        dg_ref[...] = (dy * h * _dgelu(g)).astype(dg_ref.dtype)
        rows = lax.broadcasted_iota(jnp.int32, (tt, w), 0)
        au_ref[...] = jnp.where(rows == tt - 1, anx_ref[0:1, :], pltpu.roll(a, tt - 1, 0))
        _scan8_real(au_ref, dh_ref, gs_ref, st_ref, nb=tt // 8, reverse=True)
        before = jnp.where(i == n - 1, 0.0, hp_ref[7:8, :])
        hprev = jnp.where(rows == 0, before, pltpu.roll(h, 1, 0))
        (dxc,), small = _rg_pre_bwd_fn(xc_ref[...], gs_ref[...], hprev, wa_ref[...], wx_ref[...], ba_ref[...], bx_ref[...], lam_ref[...])
        dx, dcw, dcb = _conv_back(dxc, _conv_taps(x_ref, halo_ref, i == n - 1), cw_ref, nxt_ref)
        dx_ref[...] = dx.astype(dx_ref.dtype)
        for r, v in zip(accs, (dcw, dcb) + tuple(small)):
            r[...] += v
        anx_ref[...] = a[0:8]

    rev = lambda i: n - 1 - i
    row = lambda cb_=0: pl.BlockSpec((tt, w), lambda i: (rev(i), cb_))
    fixed = lambda shape: pl.BlockSpec(shape, lambda i: (0, 0))
    acc_shapes = [(4, w), (1, w), (w, w), (w, w), (1, w), (1, w), (1, w)]
    return pl.pallas_call(
        body, name=name, grid=(n,),
        in_specs=_rg_specs(tt, rev) + [row(3), row(), row(), row(),
                                       pl.BlockSpec((8, w), lambda i: (jnp.maximum(rev(i) * (tt // 8) - 1, 0), 0))]
        + [fixed(x.shape) for x in (cw, wa, wx, ba, bx, lam)],
        out_specs=[row(), row()] + [fixed(sh) for sh in acc_shapes],
        out_shape=[jax.ShapeDtypeStruct((t, w), BF16)] * 2 + [jax.ShapeDtypeStruct(sh, F32) for sh in acc_shapes],
        scratch_shapes=[pltpu.VMEM((tt, w), F32)] * 3 + [pltpu.VMEM((8, w), F32)] * 3,
        compiler_params=_params(("arbitrary",)),
    )(proj, proj, proj, dycat, xc, a, h, h, cw, wa, wx, ba, bx, lam)


SSD_QQ = SSD_HEADS * SSD_CHUNK
SSD_GP = SSD_WIDTH // 2
SSD_GQ = SSD_QQ // 2


def _ssd_spread():
    h = jnp.arange(LANE)[:, None]
    spread_p = (jnp.arange(SSD_WIDTH)[None, :] // SSD_HEAD_DIM == h).astype(BF16)
    spread_q = (jnp.arange(SSD_QQ)[None, :] // SSD_CHUNK == h).astype(BF16)
    return spread_p, spread_q


def _ssd_prologue(dt_ref, prow_ref, sp_ref, sq_ref):
    q = SSD_CHUNK
    r = lax.broadcasted_iota(jnp.int32, (q, q), 0)
    c = lax.broadcasted_iota(jnp.int32, (q, q), 1)
    raw_c = dt_ref[...] + prow_ref[0:1, :]
    dt_c = _softplus(raw_c)
    a_r = -jnp.exp(prow_ref[1:2, :])
    cs_c = _dot_mask((r >= c).astype(F32), dt_c * a_r, mask_left=True, parts=3)
    both = _dot_mask(jnp.concatenate([dt_c, cs_c], axis=0), sp_ref[...], mask_left=False, parts=3)
    dt_x, cs_x = both[:q], both[q:]
    csx = _dot_mask(cs_c, sq_ref[...], mask_left=False, parts=3)
    rr = lax.broadcasted_iota(jnp.int32, (q, SSD_QQ), 0)
    ss = lax.broadcasted_iota(jnp.int32, (q, SSD_QQ), 1) & (q - 1)
    diag = rr == ss
    cs_row = _colsum(jnp.where(diag, csx, 0.0))
    lcat = jnp.exp(jnp.where(rr >= ss, csx - cs_row, -1e30))
    cl = cs_x[q - 1:q, :]
    return dict(raw_c=raw_c, dt_c=dt_c, a_r=a_r, dt_x=dt_x, cs_x=cs_x, lcat=lcat, diag=diag,
                ecs=jnp.exp(cs_x), wdec=jnp.exp(cl - cs_x), ecl=jnp.exp(cl), triu=(r <= c).astype(F32))


def _ssd_group(xbc_ref, g, lcat, xdt):
    ns, q = SSD_STATE, SSD_CHUNK
    bm = xbc_ref[:, pl.ds(SSD_WIDTH + g * ns, ns)]
    cm = xbc_ref[:, pl.ds(SSD_WIDTH + 2 * ns + g * ns, ns)]
    cb = _dot(cm, bm, _NT)
    lg = lcat[:, g * SSD_GQ:(g + 1) * SSD_GQ]
    wcat = jnp.concatenate([cb] * 4, axis=1) * lg
    head = lax.broadcasted_iota(jnp.int32, (1, SSD_GP), 1) // SSD_HEAD_DIM
    xg = xdt[:, g * SSD_GP:(g + 1) * SSD_GP]
    xbd = jnp.concatenate([jnp.where(head == j, xg, 0.0) for j in range(4)], axis=0)
    return bm, cm, lg, wcat, xbd, head


def _ssd_gate(yraw, z, nw):
    yg = yraw * _silu(z)
    r = lax.rsqrt(jnp.mean(yg * yg, axis=1, keepdims=True) + LN_EPS)
    return yg, r


def _ssd_specs(q, idx):
    return [pl.BlockSpec((q, SSD_XBC), lambda i: (idx(i), P_XBC // SSD_XBC)),
            pl.BlockSpec((8, SSD_XBC), lambda i: (jnp.maximum(idx(i) * (q // 8) - 1, 0), P_XBC // SSD_XBC)),
            pl.BlockSpec((q, SSD_WIDTH), lambda i: (idx(i), P_Z // SSD_WIDTH)),
            pl.BlockSpec((q, LANE), lambda i: (idx(i), P_DT // LANE)),
            pl.BlockSpec((4, SSD_XBC), lambda i: (0, 0)), pl.BlockSpec((1, SSD_XBC), lambda i: (0, 0)),
            pl.BlockSpec((8, LANE), lambda i: (0, 0)), pl.BlockSpec((1, SSD_WIDTH), lambda i: (0, 0)),
            pl.BlockSpec((1, SSD_WIDTH), lambda i: (0, 0)),
            pl.BlockSpec((LANE, SSD_WIDTH), lambda i: (0, 0)), pl.BlockSpec((LANE, SSD_QQ), lambda i: (0, 0))]


def ssd_fwd(proj, cw, cb, prow, d_x, nw, *, name):
    t = proj.shape[0]
    q, ns = SSD_CHUNK, SSD_STATE
    nc = t // q
    spread_p, spread_q = _ssd_spread()

    def body(x_ref, halo_ref, z_ref, dt_ref, cw_ref, cb_ref, prow_ref, dx_ref, nw_ref, sp_ref, sq_ref,
             y_ref, yraw_ref, sall_ref, s_ref, xbc_ref):
        @pl.when(pl.program_id(0) == 0)
        def _():
            s_ref[...] = jnp.zeros_like(s_ref)

        sall_ref[0] = s_ref[...]
        xbc_ref[...] = _silu(_conv_pre(_conv_taps(x_ref, halo_ref, pl.program_id(0) == 0), cw_ref, cb_ref))
        pr = _ssd_prologue(dt_ref, prow_ref, sp_ref, sq_ref)
        xs = xbc_ref[:, pl.ds(0, SSD_WIDTH)]
        xdt = xs * pr["dt_x"]
        xw = xdt * pr["wdec"]
        ys = []
        for g in range(2):
            gp = slice(g * SSD_GP, (g + 1) * SSD_GP)
            bm, cm, lg, wcat, xbd, head = _ssd_group(xbc_ref, g, pr["lcat"], xdt)
            st = s_ref[:, gp]
            ys.append(_dot(wcat, xbd) + pr["ecs"][:, gp] * _dot(cm, st) + xs[:, gp] * dx_ref[:, gp])
            s_ref[:, gp] = pr["ecl"][:, gp] * st + _dot(bm, xw[:, gp], _TN)
        yraw = jnp.concatenate(ys, axis=1)
        yraw_ref[...] = yraw
        yg, r = _ssd_gate(yraw, z_ref[...], nw_ref[...])
        y_ref[...] = (yg * r * nw_ref[...]).astype(y_ref.dtype)

    row = pl.BlockSpec((q, SSD_WIDTH), lambda i: (i, 0))
    return pl.pallas_call(
        body, name=name, grid=(nc,),
        in_specs=_ssd_specs(q, lambda i: i),
        out_specs=[row, row, pl.BlockSpec((1, ns, SSD_WIDTH), lambda i: (i, 0, 0))],
        out_shape=[jax.ShapeDtypeStruct((t, SSD_WIDTH), BF16), jax.ShapeDtypeStruct((t, SSD_WIDTH), F32),
                   jax.ShapeDtypeStruct((nc, ns, SSD_WIDTH), F32)],
        scratch_shapes=[pltpu.VMEM((ns, SSD_WIDTH), F32), pltpu.VMEM((q, SSD_XBC), F32)],
        compiler_params=_params(("arbitrary",)),
    )(proj, proj, proj, proj, cw, cb, prow, d_x, nw, spread_p, spread_q)


def ssd_bwd(proj, cw, cb, prow, d_x, nw, yraw, sall, dout, *, name):
    t = proj.shape[0]
    q, ns = SSD_CHUNK, SSD_STATE
    nc = t // q
    spread_p, spread_q = _ssd_spread()

    def body(x_ref, halo_ref, z_ref, dt_ref, cw_ref, cb_ref, prow_ref, dx_ref, nw_ref, sp_ref, sq_ref, yraw_ref, sall_ref, dout_ref,
             dxraw_ref, dz_ref, ddt_ref, dprm_ref, ddx_ref, dnw_ref, dcw_ref, dcb_ref, ds_ref, xbc_ref, dxbc_ref, nxt_ref):
        @pl.when(pl.program_id(0) == 0)
        def _():
            for r in (ds_ref, dprm_ref, ddx_ref, dnw_ref, dcw_ref, dcb_ref, nxt_ref):
                r[...] = jnp.zeros_like(r)

        taps = _conv_taps(x_ref, halo_ref, pl.program_id(0) == nc - 1)
        conv_pre = _conv_pre(taps, cw_ref, cb_ref)
        xbc_ref[...] = _silu(conv_pre)

        yraw, z, nwv, dout = yraw_ref[...], z_ref[...], nw_ref[...], dout_ref[...]
        yg, r = _ssd_gate(yraw, z, nwv)
        dnw_ref[...] += _colsum(dout * yg * r)
        dyn = dout * nwv
        dyg = r * dyn - yg * (r * r * r) * jnp.mean(dyn * yg, axis=1, keepdims=True)
        dy = dyg * _silu(z)
        dz_ref[...] = (dyg * yraw * _dsilu(z)).astype(dz_ref.dtype)

        pr = _ssd_prologue(dt_ref, prow_ref, sp_ref, sq_ref)
        xs = xbc_ref[:, pl.ds(0, SSD_WIDTH)]
        xdt = xs * pr["dt_x"]
        wdec, ecl = pr["wdec"], pr["ecl"]
        xw = xdt * wdec
        dzm_all = pr["ecs"] * dy
        last = (lax.broadcasted_iota(jnp.int32, (q, 1), 0) == q - 1).astype(F32)
        dxs, dcsxs, es = [], [], []
        for g in range(2):
            gp = slice(g * SSD_GP, (g + 1) * SSD_GP)
            bm, cm, lg, wcat, xbd, head = _ssd_group(xbc_ref, g, pr["lcat"], xdt)
            dyg_ = dy[:, gp]
            dwcat = _dot(dyg_, xbd, _NT)
            dxbd = _dot(wcat, dyg_, _TN)
            dxg = sum(jnp.where(head == j, dxbd[j * q:(j + 1) * q], 0.0) for j in range(4))
            es.append(dwcat * wcat)
            dmm = dwcat * lg
            dm = dmm[:, 0:q] + dmm[:, q:2 * q] + dmm[:, 2 * q:3 * q] + dmm[:, 3 * q:4 * q]
            dcm = _dot(dm, bm)
            dbm = _dot(dm, cm, _TN)
            st = sall_ref[0, :, gp]
            zmat = _dot(cm, st)
            dzm = dzm_all[:, gp]
            dcm = dcm + _dot(dzm, st, _NT)
            dst = _dot(cm, dzm, _TN)
            dcsx = dzm * zmat
            dsn = ds_ref[:, gp]
            dst = dst + ecl[:, gp] * dsn
            dclx = _colsum(dsn * st) * ecl[:, gp]
            dxw = _dot(bm, dsn)
            dbm = dbm + _dot(xw[:, gp], dsn, _NT)
            dxg = dxg + wdec[:, gp] * dxw
            tw = dxw * xdt[:, gp] * wdec[:, gp]
            dclx = dclx + _colsum(tw)
            dcsxs.append(dcsx - tw + last * dclx)
            ds_ref[:, gp] = dst
            dxs.append(dxg)
            dxbc_ref[:, pl.ds(SSD_WIDTH + g * ns, ns)] = dbm
            dxbc_ref[:, pl.ds(SSD_WIDTH + 2 * ns + g * ns, ns)] = dcm
        dx = jnp.concatenate(dxs, axis=1)
        dxbc_ref[:, pl.ds(0, SSD_WIDTH)] = dx * pr["dt_x"] + dy * dx_ref[...]
        ddx_ref[...] += _colsum(dy * xs)
        red = _dot_mask(jnp.concatenate([jnp.concatenate(dcsxs, axis=1), dx * xs], axis=0), sp_ref[...], _NT,
                        mask_left=False, parts=2)
        e_all = jnp.concatenate(es, axis=1)
        e_red = _dot_mask(e_all - jnp.where(pr["diag"], _colsum(e_all), 0.0), sq_ref[...], _NT, mask_left=False, parts=2)
        dadt = _dot_mask(pr["triu"], red[:q] + e_red, mask_left=True, parts=2)
        draw = (red[q:] + dadt * pr["a_r"]) * _sigmoid(pr["raw_c"])
        ddt_ref[...] = draw.astype(ddt_ref.dtype)
        zero = jnp.zeros((6, LANE), F32)
        dprm_ref[...] += jnp.concatenate([_colsum(draw), _colsum(dadt * pr["dt_c"]) * pr["a_r"], zero], axis=0)
        dxr, dcw, dcb = _conv_back(dxbc_ref[...] * _dsilu(conv_pre), taps, cw_ref, nxt_ref)
        dxraw_ref[...] = dxr.astype(dxraw_ref.dtype)
        dcw_ref[...] += dcw
        dcb_ref[...] += dcb

    rev = lambda i: nc - 1 - i
    row = lambda w: pl.BlockSpec((q, w), lambda i: (rev(i), 0))
    fixed = lambda shape: pl.BlockSpec(shape, lambda i: (0, 0))
    return pl.pallas_call(
        body, name=name, grid=(nc,),
        in_specs=_ssd_specs(q, rev) + [row(SSD_WIDTH), pl.BlockSpec((1, ns, SSD_WIDTH), lambda i: (rev(i), 0, 0)),
                                       row(SSD_WIDTH)],
        out_specs=[row(SSD_XBC), row(SSD_WIDTH), row(LANE), fixed((8, LANE)), fixed((1, SSD_WIDTH)), fixed((1, SSD_WIDTH)),
                   fixed((4, SSD_XBC)), fixed((1, SSD_XBC))],
        out_shape=[jax.ShapeDtypeStruct((t, SSD_XBC), BF16), jax.ShapeDtypeStruct((t, SSD_WIDTH), BF16),
                   jax.ShapeDtypeStruct((t, LANE), BF16), jax.ShapeDtypeStruct((8, LANE), F32),
                   jax.ShapeDtypeStruct((1, SSD_WIDTH), F32), jax.ShapeDtypeStruct((1, SSD_WIDTH), F32),
                   jax.ShapeDtypeStruct((4, SSD_XBC), F32), jax.ShapeDtypeStruct((1, SSD_XBC), F32)],
        scratch_shapes=[pltpu.VMEM((ns, SSD_WIDTH), F32), pltpu.VMEM((q, SSD_XBC), F32), pltpu.VMEM((q, SSD_XBC), F32),
                        pltpu.VMEM((8, SSD_XBC), F32)],
        compiler_params=_params(("arbitrary",)),
    )(proj, proj, proj, proj, cw, cb, prow, d_x, nw, spread_p, spread_q, yraw, sall, dout)


def _me():
    return lax.axis_index("x"), lax.axis_index("y"), lax.axis_index("c")


_ANY = pl.BlockSpec(memory_space=pl.ANY)
_MESH = pl.DeviceIdType.MESH


_AG_SEMS = [pltpu.SemaphoreType.DMA((7,)), pltpu.SemaphoreType.DMA((7,)), pltpu.SemaphoreType.DMA(())]
_RS_SEMS = [pltpu.SemaphoreType.DMA((3,)), pltpu.SemaphoreType.DMA((3,)), pltpu.SemaphoreType.DMA(())]


def _ag_phases(src, dst, send_sems, recv_sems, local_sem):
    x, y, c = _me()
    me, sibling = (x, y, c), (x, y, 1 - c)
    chips = [(1 - x, y), (x, 1 - y), (1 - x, 1 - y)]

    def slot(px, py, pc):
        return dst.at[4 * px + 2 * py + pc]

    def copy(kk, blk, to, from_src=False):
        return pltpu.make_async_remote_copy(
            src_ref=src if from_src else slot(*blk), dst_ref=slot(*blk),
            send_sem=send_sems.at[kk], recv_sem=recv_sems.at[kk], device_id=to, device_id_type=_MESH)

    mine = lambda: pltpu.make_async_copy(src, slot(*me), local_sem)
    first = lambda: [copy(0, me, sibling, True)] + [copy(1 + j, me, (*chip, c), True) for j, chip in enumerate(chips)]
    passed = lambda j: copy(4 + j, (*chips[j], c), sibling)

    def start():
        mine().start()
        for cp in first():
            cp.start()

    def forward():
        for j, chip in enumerate(chips):
            copy(1 + j, (*chip, c), me).wait_recv()
            passed(j).start()

    def finish():
        copy(0, sibling, me).wait_recv()
        for j, chip in enumerate(chips):
            copy(4 + j, (*chip, 1 - c), me).wait_recv()
        for cp in first() + [passed(j) for j in range(3)]:
            cp.wait_send()
        mine().wait()

    return start, forward, finish


def _rs_chip_phases(src, dst, send_sems, recv_sems, local_sem, rows=None):
    x, y, c = _me()
    q_me = 2 * x + y
    pick = (lambda q: src.at[q]) if rows is None else (lambda q: src.at[q, rows])
    local = lambda: pltpu.make_async_copy(pick(q_me), dst.at[q_me], local_sem)
    copies = lambda: [pltpu.make_async_remote_copy(src_ref=pick(2 * px + py), dst_ref=dst.at[q_me], send_sem=send_sems.at[j],
                                                   recv_sem=recv_sems.at[j], device_id=(px, py, c), device_id_type=_MESH)
                      for j, (px, py) in enumerate([(1 - x, y), (x, 1 - y), (1 - x, 1 - y)])]

    def start():
        local().start()
        for cp in copies():
            cp.start()

    def finish():
        for cp in copies():
            cp.wait()
        local().wait()

    return start, finish


def all_gather(block, *, name):
    def body(src, dst, send_sems, recv_sems, local_sem):
        for phase in _ag_phases(src, dst, send_sems, recv_sems, local_sem):
            phase()

    return pl.pallas_call(
        body, name=name, in_specs=[_ANY], out_specs=_ANY,
        out_shape=jax.ShapeDtypeStruct((N_DEV,) + block.shape, block.dtype), scratch_shapes=list(_AG_SEMS),
    )(block)


RS_PIECES = 4


def rs_sibling_exchange(halves, *, name):
    _, nq, r, l = halves.shape
    rows = r // RS_PIECES
    assert r % RS_PIECES == 0 and rows % 16 == 0

    def body(src, dst, send_sems, recv_sems):
        x, y, c = _me()
        copies = []
        for q in range(nq):
            for i in range(RS_PIECES):
                kk = q * RS_PIECES + i
                cp = pltpu.make_async_remote_copy(
                    src_ref=src.at[1 - c, q, pl.ds(i * rows, rows)], dst_ref=dst.at[q, pl.ds(i * rows, rows)],
                    send_sem=send_sems.at[kk], recv_sem=recv_sems.at[kk], device_id=(x, y, 1 - c), device_id_type=_MESH)
                cp.start()
                copies.append(cp)
        for cp in copies:
            cp.wait()

    n_copies = nq * RS_PIECES
    return pl.pallas_call(
        body, name=name, in_specs=[_ANY], out_specs=_ANY,
        out_shape=jax.ShapeDtypeStruct((nq, r, l), halves.dtype),
        scratch_shapes=[pltpu.SemaphoreType.DMA((n_copies,)), pltpu.SemaphoreType.DMA((n_copies,))],
    )(halves)


def pair_sum_bf16(halves, theirs, *, name, tt=128):
    _, nq, r, wd = halves.shape
    tt = min(tt, r)
    parity = lax.axis_index("c").astype(jnp.int32).reshape(1)

    def body(c_ref, own_ref, sib_ref, o_ref):
        o_ref[...] = (own_ref[...] + sib_ref[...]).astype(BF16)

    return pl.pallas_call(
        body, name=name,
        grid_spec=pltpu.PrefetchScalarGridSpec(
            num_scalar_prefetch=1, grid=(nq, r // tt),
            in_specs=[pl.BlockSpec((None, None, tt, wd), lambda q, i, c: (c[0], q, i, 0)),
                      pl.BlockSpec((None, tt, wd), lambda q, i, c: (q, i, 0))],
            out_specs=pl.BlockSpec((None, tt, wd), lambda q, i, c: (q, i, 0))),
        out_shape=jax.ShapeDtypeStruct((nq, r, wd), BF16),
        compiler_params=_params(("parallel", "parallel")),
    )(parity, halves, theirs)


def rs_chip_exchange(part, *, name):
    def body(src, dst, send_sems, recv_sems, local_sem):
        for phase in _rs_chip_phases(src, dst, send_sems, recv_sems, local_sem):
            phase()

    return pl.pallas_call(
        body, name=name, in_specs=[_ANY], out_specs=_ANY,
        out_shape=jax.ShapeDtypeStruct(part.shape, part.dtype), scratch_shapes=list(_RS_SEMS),
    )(part)


def adamw(slabs, w, m, v, *, name, tt):
    ns, (r, wd) = slabs.shape[0], w.shape
    tt = min(tt, r)
    assert r % tt == 0

    def body(s_ref, w_ref, m_ref, v_ref, g_ref, d_ref, nm_ref, nv_ref):
        g = s_ref[0].astype(F32)
        for kdev in range(1, ns):
            g = g + s_ref[kdev].astype(F32)
        wv = w_ref[...]
        nm = ADAM_B1 * m_ref[...] + (1.0 - ADAM_B1) * g
        nv = ADAM_B2 * v_ref[...] + (1.0 - ADAM_B2) * (g * g)
        m_hat = nm / (1.0 - ADAM_B1 ** ADAM_STEP)
        v_hat = nv / (1.0 - ADAM_B2 ** ADAM_STEP)
        g_ref[...] = g
        d_ref[...] = -ADAM_LR * (m_hat / (jnp.sqrt(v_hat) + ADAM_EPS) + ADAM_WD * wv)
        nm_ref[...] = nm
        nv_ref[...] = nv

    spec = pl.BlockSpec((tt, wd), lambda i: (i, 0))
    return pl.pallas_call(
        body, name=name, grid=(r // tt,),
        in_specs=[pl.BlockSpec((ns, tt, wd), lambda i: (0, i, 0)), spec, spec, spec],
        out_specs=[spec] * 4, out_shape=[jax.ShapeDtypeStruct((r, wd), F32)] * 4,
        compiler_params=_params(("parallel",)),
    )(slabs, w, m, v)


WIDE = 1024
BIG = [("w_in", True, 289), ("w_out", False, 128), ("xa_wq", False, 128), ("xa_wk", False, 128), ("xa_wv", False, 128),
       ("xa_wo", False, 128), ("mlp_w2", False, 512), ("mlp_w1", True, 512)]
TINY = [("ssd_conv_w", 2), ("s5_glu_w", 1), ("rg_conv_w", 2)]
KEEP_F32 = ("ssd_conv_w", "rg_conv_w")
TINY_ROWS = 32
SHARDED = [name for name, _, _ in BIG] + [name for name, _ in TINY]
SMALL = ["ssd_conv_b", "ssd_dt_bias", "ssd_a_log", "ssd_d", "ssd_norm_w", "s5_lam_re", "s5_lam_im",
         "s5_log_step", "s5_b_re", "s5_b_im", "s5_c_re", "s5_c_im", "s5_d", "s5_glu_b", "rg_conv_b",
         "rg_wa", "rg_ba", "rg_wx", "rg_bx", "rg_lambda", "ln1_g", "ln1_b", "ln2_g", "ln2_b", "ln3_g", "ln3_b"]
WEIGHTS = ['w_in', 'w_out', 'ssd_conv_w', 'ssd_conv_b', 'ssd_dt_bias', 'ssd_a_log', 'ssd_d', 'ssd_norm_w',
           's5_lam_re', 's5_lam_im', 's5_log_step', 's5_b_re', 's5_b_im', 's5_c_re', 's5_c_im', 's5_d',
           's5_glu_w', 's5_glu_b', 'rg_conv_w', 'rg_conv_b', 'rg_wa', 'rg_ba', 'rg_wx', 'rg_bx', 'rg_lambda',
           'ln1_g', 'ln1_b', 'xa_wq', 'xa_wk', 'xa_wv', 'xa_wo', 'ln2_g', 'ln2_b', 'mlp_w1', 'mlp_w2',
           'ln3_g', 'ln3_b']


def _pad16(rows):
    return -(-rows // 16) * 16


def _pack_rows(flat, mult):
    n = flat.shape[-1]
    r = -(-n // (LANE * mult)) * mult
    pad = [(0, 0)] * (flat.ndim - 1) + [(0, r * LANE - n)]
    return jnp.pad(flat, pad).reshape(flat.shape[:-1] + (r, LANE))


def _unpack(packed, shapes):
    lead = packed.shape[:-2]
    flat = packed.reshape(lead + (-1,))
    out, off = [], 0
    for s in shapes:
        n = math.prod(s)
        out.append(flat[..., off:off + n].reshape(lead + tuple(s)))
        off += n
    return out


PACK_ROWS = 2048


def _tiny_block(flat):
    pad = [(0, 0)] * (flat.ndim - 1) + [(0, TINY_ROWS * WIDE - flat.shape[-1])]
    return jnp.pad(flat, pad).reshape(flat.shape[:-1] + (TINY_ROWS, WIDE))


def _pack_layer(big, tiny_flat=None):
    blocks, used = [], 0
    some = big[BIG[0][0]]

    def zeros(rows):
        return jnp.zeros(some.shape[:-2] + (rows, WIDE), some.dtype)

    for name, _, rows in BIG:
        blocks.append(jnp.pad(big[name], [(0, 0)] * (some.ndim - 2) + [(0, _pad16(rows) - rows), (0, 0)]))
        used += _pad16(rows)
    if tiny_flat is not None:
        blocks.append(_tiny_block(tiny_flat))
        used += TINY_ROWS
    return jnp.concatenate(blocks + [zeros(PACK_ROWS - used)], axis=-2)


def _unpack_layer(packed):
    big, off = {}, 0
    for name, _, rows in BIG:
        big[name] = packed[..., off:off + rows, :]
        off += _pad16(rows)
    return big, packed[..., off:off + TINY_ROWS, :].reshape(packed.shape[:-2] + (TINY_ROWS * WIDE,))


def _split_flat(flat, shapes):
    out, off = [], 0
    for s in shapes:
        n = math.prod(s)
        out.append(flat[..., off:off + n].reshape(flat.shape[:-1] + tuple(s)))
        off += n
    return out


def _to_full(gathered, axis):
    g = jnp.moveaxis(gathered, 0, axis)
    s = g.shape
    return g.reshape(s[:axis] + (s[axis] * s[axis + 1],) + s[axis + 2:])


def _to_slabs(full, axis):
    s = full.shape
    g = full.reshape(s[:axis] + (N_DEV, s[axis] // N_DEV) + s[axis + 1:])
    return jnp.moveaxis(g, axis, 0)


def _blockdiag(w):
    h, i, j = w.shape
    eye = jnp.eye(h, dtype=w.dtype)
    return (w[:, :, None, :] * eye[:, None, :, None]).reshape(h * i, h * j)


def _blockdiag_extract(m, h):
    i, j = m.shape[0] // h, m.shape[1] // h
    eye = jnp.eye(h, dtype=m.dtype)
    return (m.reshape(h, i, h, j) * eye[:, None, :, None]).sum(axis=2)


def _s5_disc(lr, li, ls, bre, bim):
    step = jnp.exp(ls)[:, None]
    er = jnp.exp(lr * step)
    ar, ai = er * jnp.cos(li * step), er * jnp.sin(li * step)
    nr, ni, den = ar - 1.0, ai, lr * lr + li * li
    qr, qi = (nr * lr + ni * li) / den, (ni * lr - nr * li) / den
    bbr = qr[..., None] * bre - qi[..., None] * bim
    bbi = qr[..., None] * bim + qi[..., None] * bre
    return ar, ai, bbr, bbi


def _row(v, width=None):
    v = v.reshape(1, -1)
    if width is not None and v.shape[1] < width:
        v = jnp.pad(v, ((0, 0), (0, width - v.shape[1])))
    return v


def _relu2(a):
    r = jnp.maximum(a, 0.0)
    return r * r


def _add_alpha(acc, d):
    return acc + ALPHA * d


def _layer_params(full, small, l):
    p = {}
    w_in = full["w_in"]
    z, xbc, dt, u, xr, g = w_in[0:512], w_in[512:1536], w_in[1536:1544], w_in[1544:1800], w_in[1800:2056], w_in[2056:2312]
    p["w_inp"] = jnp.concatenate([xbc, z, u, xr, g, dt, jnp.zeros((D_INP - P_DT - 8, D_MODEL), w_in.dtype)], axis=0)
    for k_ in ("w_out", "xa_wq", "xa_wk", "xa_wv", "xa_wo", "mlp_w1", "mlp_w2"):
        p[k_] = full[k_]
    p["s5_glu_w"] = full["s5_glu_w"][l]
    p["ssd_cw"], p["ssd_cb"] = full["ssd_conv_w"][l], _row(small["ssd_conv_b"][l])
    dtb, alog, dsk = small["ssd_dt_bias"][l], small["ssd_a_log"][l], small["ssd_d"][l]
    p["prow"] = jnp.concatenate([_row(dtb, LANE), _row(alog, LANE), jnp.zeros((6, LANE), F32)], axis=0)
    p["ssd_dx"] = _row(jnp.repeat(dsk, SSD_HEAD_DIM))
    p["ssd_nw"] = _row(small["ssd_norm_w"][l])
    s5_in = (small["s5_lam_re"][l], small["s5_lam_im"][l], small["s5_log_step"][l], small["s5_b_re"][l], small["s5_b_im"][l])
    (ar, ai, bbr, bbi), p["s5_vjp"] = jax.vjp(_s5_disc, *s5_in)
    p["lam_fwd"] = jnp.concatenate([_row(ar), _row(ai)], axis=1)
    p["lam_adj"] = jnp.concatenate([_row(ar), _row(-ai)], axis=1)
    p["bcat"] = jnp.concatenate([_blockdiag(jnp.swapaxes(bbr, 1, 2)), _blockdiag(jnp.swapaxes(bbi, 1, 2))], axis=1)
    p["ccat"] = jnp.concatenate([_blockdiag(jnp.swapaxes(small["s5_c_re"][l], 1, 2)),
                                 -_blockdiag(jnp.swapaxes(small["s5_c_im"][l], 1, 2))], axis=0)
    p["s5_d"], p["s5_glu_b"] = _row(small["s5_d"][l]), _row(small["s5_glu_b"][l])
    p["rg_cw"], p["rg_cb"] = full["rg_conv_w"][l], _row(small["rg_conv_b"][l])
    p["rg_wa"], p["rg_wx"] = _blockdiag(small["rg_wa"][l]), _blockdiag(small["rg_wx"][l])
    p["rg_ba"], p["rg_bx"], p["rg_lam"] = _row(small["rg_ba"][l]), _row(small["rg_bx"][l]), _row(small["rg_lambda"][l])
    for i in (1, 2, 3):
        p[f"g{i}"], p[f"b{i}"] = _row(small[f"ln{i}_g"][l]), _row(small[f"ln{i}_b"][l])
    return p


def _take_side(res, n_out, got):
    res = res if isinstance(res, (list, tuple)) else (res,)
    got.extend(res[n_out:])
    return res[0] if n_out == 1 else res[:n_out]


def _layer_fwd(h0, mem, p, sides={}):
    t = h0.shape[0]
    s = {"h0": h0}
    got = []
    proj = _take_side(mm(h0, p["w_inp"], tb=True, name="in_proj", side=sides.get("in_proj")), 1, got)
    y_ssd, yraw, sall = ssd_fwd(proj, p["ssd_cw"], p["ssd_cb"], p["prow"], p["ssd_dx"], p["ssd_nw"], name="ssd_fwd")
    hs5, ylin, y_s5 = s5_fwd(proj, p["bcat"], p["lam_fwd"], p["ccat"], p["s5_d"], p["s5_glu_w"], p["s5_glu_b"], name="s5_fwd")
    rg_prm = (p["rg_cw"], p["rg_cb"], p["rg_wa"], p["rg_wx"], p["rg_ba"], p["rg_bx"], p["rg_lam"])
    y_rg, xc, a_rg, h_rg = rg_fwd(proj, *rg_prm, name="rg_fwd")
    ycat = jnp.concatenate([y_ssd, y_s5, y_rg], axis=1)
    pre1, h1 = mm_ln(ycat, p["w_out"], h0, p["g1"], p["b1"], name="out_proj")
    q = mm(h1, p["xa_wq"], name="xa_q", out_dtype=BF16)
    k = mm(mem, p["xa_wk"], name="xa_kv")
    v = mm(mem, p["xa_wv"], name="xa_kv")
    (o,), _ = rowk(_attn_fwd_fn, [(q, D_MODEL, 0)], [k, v], [D_MODEL], [], rows=t, name="xa_fwd", out_dtypes=[BF16])
    pre2, h2 = mm_ln(o, p["xa_wo"], h1, p["g2"], p["b2"], name="xa_o")
    a_mlp = _take_side(mm(h2, p["mlp_w1"], tb=True, name="mlp_up", side=sides.get("mlp_up")), 1, got)
    pre3, h3 = _take_side(mm_ln(a_mlp, p["mlp_w2"], h2, p["g3"], p["b3"], fa=_relu2, name="mlp_down",
                                side=sides.get("mlp_down")), 2, got)
    s.update(proj=proj, yraw=yraw, sall=sall, hs5=hs5, ylin=ylin, xc=xc, a_rg=a_rg, h_rg=h_rg,
             ycat=ycat, pre1=pre1, h1=h1, q=q, k=k, v=v, o=o, pre2=pre2, h2=h2, a_mlp=a_mlp, pre3=pre3)
    return h3, s, got


def _layer_bwd(dh3, mem, p, s, l, gfull, gsmall, sides={}):
    t = dh3.shape[0]
    proj = s["proj"]
    dpre3, dg3, db3 = ln_bwd(s["pre3"], dh3, p["g3"], name="ln_bwd")
    got = []
    da = _take_side(mm(dpre3, p["mlp_w2"], tb=True, o_extra=(s["a_mlp"],), fo=lambda acc, a: acc * 2.0 * jnp.maximum(a, 0.0),
                       name="mlp_da", out_dtype=BF16, side=sides.get("mlp_da")), 1, got)
    gfull["mlp_w2"][l] = mm(s["a_mlp"], dpre3, ta=True, fa=_relu2, name="mlp_dw2")
    gfull["mlp_w1"][l] = mm(da, s["h2"], ta=True, name="mlp_dw1")
    dh2 = _take_side(mm(da, p["mlp_w1"], o_extra=(dpre3,), fo=_add_alpha, name="mlp_dx", side=sides.get("mlp_dx")), 1, got)
    dpre2, dg2, db2 = ln_bwd(s["pre2"], dh2, p["g2"], name="ln_bwd")
    do = _take_side(mm(dpre2, p["xa_wo"], tb=True, name="xa_do", out_dtype=BF16, side=sides.get("xa_do")), 1, got)
    gfull["xa_wo"][l] = mm(s["o"], dpre2, ta=True, name="dw_sq")
    (dq,), (dk, dv) = rowk(_attn_bwd_fn, [(s["q"], D_MODEL, 0), (do, D_MODEL, 0)], [s["k"], s["v"]], [D_MODEL],
                           [(256, D_MODEL), (256, D_MODEL)], rows=t, name="xa_bwd", out_dtypes=[BF16])
    gfull["xa_wq"][l] = mm(s["h1"], dq, ta=True, name="dw_sq")
    gfull["xa_wk"][l] = mm(mem, dk, ta=True, name="dw_kv")
    gfull["xa_wv"][l] = mm(mem, dv, ta=True, name="dw_kv")
    dh1 = mm(dq, p["xa_wq"], tb=True, o_extra=(dpre2,), fo=_add_alpha, name="dx_sq")
    dpre1, dg1, db1 = ln_bwd(s["pre1"], dh1, p["g1"], name="ln_bwd")
    dycat = mm(dpre1, p["w_out"], tb=True, name="xa_do")
    gfull["w_out"][l] = mm(s["ycat"], dpre1, ta=True, name="dw_sq")
    rg_prm = (p["rg_cw"], p["rg_cb"], p["rg_wa"], p["rg_wx"], p["rg_ba"], p["rg_bx"], p["rg_lam"])
    dxr, dg_rg, d_rgcw, d_rgcb, dwa, dwx, dba, dbx, dlam = rg_bwd(proj, dycat, s["xc"], s["a_rg"], s["h_rg"], *rg_prm, name="rg_bwd")
    du, dccat, dbcat, dar, dai, d_s5d, d_gluw, d_glub = s5_bwd(dycat, s["ylin"], s["hs5"], proj, p["bcat"], p["lam_adj"], p["ccat"],
                                                               p["s5_d"], p["s5_glu_w"], p["s5_glu_b"], name="s5_bwd")
    dxbc, dz, ddt, dprm, ddx, dnw, d_scw, d_scb = ssd_bwd(proj, p["ssd_cw"], p["ssd_cb"], p["prow"], p["ssd_dx"], p["ssd_nw"],
                                                         s["yraw"], s["sall"], dycat, name="ssd_bwd")
    dproj = jnp.concatenate([dxbc, dz, du, dxr, dg_rg, ddt, jnp.zeros((t, D_INP - P_DT - LANE), BF16)], axis=1)
    dh0 = mm(dproj, p["w_inp"], o_extra=(dpre1,), fo=_add_alpha, name="in_proj_dx")
    dwp = mm(dproj, s["h0"], ta=True, name="in_proj_dw")
    gfull["w_in"][l] = jnp.concatenate([dwp[P_Z:P_Z + 512], dwp[P_XBC:P_XBC + 1024], dwp[P_DT:P_DT + 8],
                                        dwp[P_U:P_U + 256], dwp[P_XR:P_XR + 256], dwp[P_G:P_G + 256]], axis=0)
    gfull["ssd_conv_w"][l], gfull["rg_conv_w"][l], gfull["s5_glu_w"][l] = d_scw, d_rgcw, d_gluw
    ng, ns = S5_GROUPS, S5_STATE
    dbbr = jnp.swapaxes(_blockdiag_extract(dbcat[:, :S5_NSTATE], ng), 1, 2)
    dbbi = jnp.swapaxes(_blockdiag_extract(dbcat[:, S5_NSTATE:], ng), 1, 2)
    d_lr, d_li, d_ls, d_bre, d_bim = p["s5_vjp"]((dar.reshape(ng, ns), dai.reshape(ng, ns), dbbr, dbbi))
    gsmall["s5_lam_re"][l], gsmall["s5_lam_im"][l], gsmall["s5_log_step"][l] = d_lr, d_li, d_ls
    gsmall["s5_b_re"][l], gsmall["s5_b_im"][l] = d_bre, d_bim
    gsmall["s5_c_re"][l] = jnp.swapaxes(_blockdiag_extract(dccat[:S5_NSTATE], ng), 1, 2)
    gsmall["s5_c_im"][l] = -jnp.swapaxes(_blockdiag_extract(dccat[S5_NSTATE:], ng), 1, 2)
    gsmall["s5_d"][l], gsmall["s5_glu_b"][l] = d_s5d[0], d_glub[0]
    gsmall["ssd_conv_b"][l], gsmall["rg_conv_b"][l] = d_scb[0], d_rgcb[0]
    gsmall["ssd_dt_bias"][l], gsmall["ssd_a_log"][l] = dprm[0, :8], dprm[1, :8]
    gsmall["ssd_d"][l] = ddx.reshape(SSD_HEADS, SSD_HEAD_DIM).sum(axis=1)
    gsmall["ssd_norm_w"][l] = dnw[0]
    gsmall["rg_wa"][l], gsmall["rg_wx"][l] = _blockdiag_extract(dwa, RG_BLOCKS), _blockdiag_extract(dwx, RG_BLOCKS)
    gsmall["rg_ba"][l], gsmall["rg_bx"][l] = dba.reshape(RG_BLOCKS, RG_BLOCK_DIM), dbx.reshape(RG_BLOCKS, RG_BLOCK_DIM)
    gsmall["rg_lambda"][l] = dlam[0]
    for i, (dg, db) in zip((1, 2, 3), ((dg1, db1), (dg2, db2), (dg3, db3))):
        gsmall[f"ln{i}_g"][l], gsmall[f"ln{i}_b"][l] = dg[0], db[0]
    return dh0, got


def _step(a):
    h = a["x"][0]
    mem = a["mem"][0]
    t = h.shape[0]
    r4, r3 = PACK_ROWS // 4, 3 * PACK_ROWS // 8

    def my_shards(pre):
        return ({name: (jnp.swapaxes(a[pre + name], 1, 2) if tr else a[pre + name]) for name, tr, _ in BIG},
                [a[pre + name] for name, _ in TINY])

    def my_pack(pre, l):
        big, tiny = my_shards(pre)
        return _pack_layer({name: w[l] for name, w in big.items()},
                           jnp.concatenate([w.reshape(-1) for w in tiny]) if l == 0 else None)

    big, tiny = my_shards("")
    tiny16 = [(lax.bitcast_convert_type(w, BF16) if name in KEEP_F32 else w.astype(BF16)).reshape(-1)
              for (name, _), w in zip(TINY, tiny)]
    packed = [_pack_layer({name: w[l].astype(BF16) for name, w in big.items()}, jnp.concatenate(tiny16) if l == 0 else None)
              for l in range(DEPTH)]
    small = {name: a[name] for name in SMALL}

    def gathered_weights(g):
        gbig, gtiny = _unpack_layer(g)
        return {name: w.reshape(-1, WIDE) for name, w in gbig.items()}, gtiny

    full, gtiny = gathered_weights(all_gather(packed[0], name="ag_weights"))
    tiny_shapes = [w.shape + ((2,) if name in KEEP_F32 else ()) for (name, _), w in zip(TINY, tiny)]
    tiny_full = {name: _to_full(lax.bitcast_convert_type(g, F32) if name in KEEP_F32 else g, axis)
                 for (name, axis), g in zip(TINY, _split_flat(gtiny, tiny_shapes))}
    p0 = _layer_params({**full, **tiny_full}, small, 0)
    h, s0, got = _layer_fwd(h, mem, p0, sides={"in_proj": ("gather", packed[1], 0, r4), "mlp_up": ("gather", packed[1], r4, r3),
                                                "mlp_down": ("gather", packed[1], r4 + r3, r3)})
    full, _ = gathered_weights(jnp.concatenate(got, axis=1))
    p1 = _layer_params({**full, **tiny_full}, small, 1)
    h, s1, _ = _layer_fwd(h, mem, p1)
    (dh,), (loss_part,) = rowk(_loss_fn, [(h, D_MODEL, 0), (a["loss_target"][0], D_MODEL, 0)], [], [D_MODEL], [(1, 1)],
                               rows=t, name="loss_head")
    loss = lax.psum(loss_part[0, 0], ("x", "y", "c"))
    gfull = {name: [None] * DEPTH for name in SHARDED}
    gsmall = {name: [None] * DEPTH for name in SMALL}

    def chip_partials(l):
        gbig = {name: gfull[name][l].reshape(N_DEV, rows, WIDE) for name, _, rows in BIG}
        gtiny = None
        if l == 0:
            gtiny = jnp.concatenate([_to_slabs(jnp.stack(gfull[name]), axis).reshape(N_DEV, -1) for name, axis in TINY], axis=1)
        slabs = _pack_layer(gbig, gtiny)
        halves = jnp.swapaxes(slabs.reshape((4, 2) + slabs.shape[1:]), 0, 1)
        theirs = rs_sibling_exchange(halves, name="rs_sibling")
        return pair_sum_bf16(halves, theirs, name="rs_pair_sum")

    dh, _ = _layer_bwd(dh, mem, p1, s1, 1, gfull, gsmall)
    part1 = chip_partials(1)
    dh, got = _layer_bwd(dh, mem, p0, s0, 0, gfull, gsmall, sides={"mlp_da": ("chips", part1, 0, r3), "mlp_dx": ("chips", part1, r3, r3),
                                                                    "xa_do": ("chips", part1, 2 * r3, r4)})
    grad_x = dh[None]
    landed = [rs_chip_exchange(chip_partials(0), name="rs_chips"), jnp.concatenate(got, axis=1)]
    bigs = [adamw(landed[l], my_pack("", l), my_pack("m_", l), my_pack("v_", l), name="adamw_sharded", tt=128) for l in range(DEPTH)]
    gs = _pack_rows(jnp.concatenate([jnp.stack(gsmall[name]).reshape(-1) for name in SMALL]), 8)
    gs = all_gather(gs, name="ag_small_grads")
    pks = lambda pre: _pack_rows(jnp.concatenate([a[pre + name].reshape(-1) for name in SMALL]), 8)
    sm = adamw(gs, pks(""), pks("m_"), pks("v_"), name="adamw_replicated", tt=gs.shape[1])
    out = {}
    for i, kind in enumerate(("grad_", "delta_", "new_m_", "new_v_")):
        layers = [_unpack_layer(bigs[l][i]) for l in range(DEPTH)]
        for name, tr, _ in BIG:
            arr = jnp.stack([layers[l][0][name] for l in range(DEPTH)])
            out[kind + name] = jnp.swapaxes(arr, 1, 2) if tr else arr
        for (name, _), arr in zip(TINY, _split_flat(layers[0][1], [w.shape for w in tiny])):
            out[kind + name] = arr
        for name, arr in zip(SMALL, _unpack(sm[i], [a[name].shape for name in SMALL])):
            out[kind + name] = arr
    return (loss, grad_x) + tuple(out[kind + name] for kind in ("grad_", "delta_", "new_m_", "new_v_") for name in WEIGHTS)


def kernel(x, mem, w_in, w_out, ssd_conv_w, ssd_conv_b, ssd_dt_bias, ssd_a_log, ssd_d, ssd_norm_w, s5_lam_re, s5_lam_im, s5_log_step, s5_b_re, s5_b_im, s5_c_re, s5_c_im, s5_d, s5_glu_w, s5_glu_b, rg_conv_w, rg_conv_b, rg_wa, rg_ba, rg_wx, rg_bx, rg_lambda, ln1_g, ln1_b, xa_wq, xa_wk, xa_wv, xa_wo, ln2_g, ln2_b, mlp_w1, mlp_w2, ln3_g, ln3_b, loss_target, m_w_in, m_w_out, m_ssd_conv_w, m_ssd_conv_b, m_ssd_dt_bias, m_ssd_a_log, m_ssd_d, m_ssd_norm_w, m_s5_lam_re, m_s5_lam_im, m_s5_log_step, m_s5_b_re, m_s5_b_im, m_s5_c_re, m_s5_c_im, m_s5_d, m_s5_glu_w, m_s5_glu_b, m_rg_conv_w, m_rg_conv_b, m_rg_wa, m_rg_ba, m_rg_wx, m_rg_bx, m_rg_lambda, m_ln1_g, m_ln1_b, m_xa_wq, m_xa_wk, m_xa_wv, m_xa_wo, m_ln2_g, m_ln2_b, m_mlp_w1, m_mlp_w2, m_ln3_g, m_ln3_b, v_w_in, v_w_out, v_ssd_conv_w, v_ssd_conv_b, v_ssd_dt_bias, v_ssd_a_log, v_ssd_d, v_ssd_norm_w, v_s5_lam_re, v_s5_lam_im, v_s5_log_step, v_s5_b_re, v_s5_b_im, v_s5_c_re, v_s5_c_im, v_s5_d, v_s5_glu_w, v_s5_glu_b, v_rg_conv_w, v_rg_conv_b, v_rg_wa, v_rg_ba, v_rg_wx, v_rg_bx, v_rg_lambda, v_ln1_g, v_ln1_b, v_xa_wq, v_xa_wk, v_xa_wv, v_xa_wo, v_ln2_g, v_ln2_b, v_mlp_w1, v_mlp_w2, v_ln3_g, v_ln3_b):
    return _step(dict(locals()))
```

```python
import math

import jax
import jax.numpy as jnp
from jax import lax
from jax.experimental import pallas as pl
from jax.experimental.pallas import tpu as pltpu

F32 = jnp.float32
BF16 = jnp.bfloat16

N_DEV = 8
D_MODEL = 1024
DEPTH = 2
SSD_WIDTH = 512
SSD_HEADS = 8
SSD_HEAD_DIM = 64
SSD_STATE = 128
SSD_CHUNK = 128
SSD_XBC = 1024
S5_WIDTH = 256
S5_GROUPS = 16
S5_GROUP_CH = 16
S5_STATE = 64
S5_NSTATE = S5_GROUPS * S5_STATE
RG_WIDTH = 256
RG_BLOCKS = 4
RG_BLOCK_DIM = 64
RG_C = 8.0
XA_HEADS = 4
XA_HEAD_DIM = 256
ALPHA = (2.0 * DEPTH) ** 0.25
LN_EPS = 1e-5
ADAM_LR, ADAM_B1, ADAM_B2, ADAM_EPS, ADAM_WD, ADAM_STEP = 0.001, 0.9, 0.999, 1e-08, 0.01, 10

P_XBC, P_Z, P_U, P_XR, P_G, P_DT = 0, 1024, 1536, 1792, 2048, 2304
D_INP = 2560
LANE = 128
VMEM_LIMIT = 56 * 1024 * 1024
ROW_TILE = 512

_NN = ((1,), (0,))
_NT = ((1,), (1,))
_TN = ((0,), (0,))


def _dot(a, b, dims=_NN):
    return lax.dot_general(a.astype(BF16), b.astype(BF16), (dims, ((), ())), preferred_element_type=F32)


def _split_bf16(x, parts):
    out, rem = [], x
    for _ in range(parts):
        piece = rem.astype(BF16)
        out.append(piece)
        rem = rem - piece.astype(F32)
    return out


def _dot_mask(a, b, dims=_NN, *, mask_left, parts):
    if mask_left:
        return sum(_dot(a, piece, dims) for piece in _split_bf16(b, parts))
    return sum(_dot(piece, b, dims) for piece in _split_bf16(a, parts))


def _sigmoid(x):
    return 1.0 / (1.0 + jnp.exp(-x))


def _silu(x):
    return x * _sigmoid(x)


def _dsilu(x):
    s = _sigmoid(x)
    return s * (1.0 + x * (1.0 - s))


_GK = math.sqrt(2.0 / math.pi)
_GC = 0.044715


def _gelu(x):
    return 0.5 * x * (1.0 + jnp.tanh(_GK * (x + _GC * x * x * x)))


def _dgelu(x):
    th = jnp.tanh(_GK * (x + _GC * x * x * x))
    return 0.5 * (1.0 + th) + 0.5 * x * (1.0 - th * th) * _GK * (1.0 + 3.0 * _GC * x * x)


def _log1p_pos(e):
    return jnp.where(e < 1e-2, e * (1.0 - e * (0.5 - e * (1.0 / 3.0))), jnp.log(1.0 + e))


def _softplus(x):
    return jnp.maximum(x, 0.0) + _log1p_pos(jnp.exp(-jnp.abs(x)))


def _neg_expm1(x):
    poly = -x * (1.0 + x * (0.5 + x * (1.0 / 6.0 + x * (1.0 / 24.0 + x * (1.0 / 120.0)))))
    return jnp.where(x > -0.05, poly, 1.0 - jnp.exp(x))


def _params(sem):
    return pltpu.CompilerParams(dimension_semantics=sem, vmem_limit_bytes=VMEM_LIMIT)


RESIDENT_BYTES = 8 * 1024 * 1024
STREAM_BYTES = 4 * 1024 * 1024


def _halve_to_fit(dims, bytes_per, limit):
    dims = list(dims)
    while math.prod(dims) * bytes_per > limit:
        i = max(range(len(dims)), key=lambda d: dims[d])
        assert dims[i] % 256 == 0, dims
        dims[i] //= 2
    return dims


def _side_exchange(side, src, dst, sems, step, nsteps):
    kind, _, r0, rows = side
    span = pl.ds(r0, rows)
    if kind == "gather":
        phases = lambda: _ag_phases(src.at[span], dst, *sems)
        when = (0, (3 * nsteps) // 4, nsteps - 1)
    else:
        phases = lambda: _rs_chip_phases(src, dst, *sems, rows=span)
        when = (0, nsteps - 1)
    for idx, at in enumerate(when):
        pl.when(step == at)(lambda idx=idx: phases()[idx]())


def mm(a, b, *, name, ta=False, tb=False, a_extra=(), fa=None, o_extra=(), r_extra=(), fo=None, n_out=1,
       a_off=0, m=None, k=None, out_dtype=F32, side=None):
    n = b.shape[0] if tb else b.shape[1]
    na, no, nr = 1 + len(a_extra), len(o_extra), len(r_extra)
    if not ta:
        assert m is None
        m, kdim = a.shape[0], (a.shape[1] if k is None else k)
        assert a_off % kdim == 0
        (tn,) = _halve_to_fit([n], kdim * b.dtype.itemsize, RESIDENT_BYTES)
        (tm,) = _halve_to_fit([min(512, m)], max(tn, kdim) * 4, STREAM_BYTES)
        a_spec = pl.BlockSpec((tm, kdim), lambda i, j: (i, a_off // kdim))
        b_spec = pl.BlockSpec((tn, kdim), lambda i, j: (j, 0)) if tb else pl.BlockSpec((kdim, tn), lambda i, j: (0, j))
        o_spec = pl.BlockSpec((tm, tn), lambda i, j: (i, j))
        dims = _NT if tb else _NN

        r_spec = pl.BlockSpec((1, tn), lambda i, j: (0, j))

        grid = (m // tm, n // tn)
        nin = na + 1 + no + nr

        def body(*refs):
            a_refs, b_ref, e_refs, out_refs = refs[:na], refs[na], refs[na + 1:nin], refs[nin + (side is not None):nin + (side is not None) + n_out]
            if side is not None:
                _side_exchange(side, refs[nin], refs[nin + 1 + n_out], refs[nin + 2 + n_out:],
                               pl.program_id(0) * grid[1] + pl.program_id(1), grid[0] * grid[1])
            av = a_refs[0][...] if fa is None else fa(*[r[...] for r in a_refs])
            acc = _dot(av, b_ref[...], dims)
            res = acc if fo is None else fo(acc, *[r[...] for r in e_refs])
            for r, v in zip(out_refs, res if n_out > 1 else (res,)):
                r[...] = v.astype(r.dtype)

        sem = ("parallel", "parallel") if side is None else ("arbitrary", "arbitrary")
    else:
        assert k is None and not tb and fo is None and not o_extra and not r_extra and n_out == 1 and out_dtype == F32
        assert side is None
        kdim, m = a.shape[0], (a.shape[1] if m is None else m)
        r_spec = None
        tm, tn = _halve_to_fit([m, n], 4, RESIDENT_BYTES)
        (tk,) = _halve_to_fit([min(512, kdim)], max(tm, tn) * 4, STREAM_BYTES)
        assert a_off % tm == 0
        a_spec = pl.BlockSpec((tk, tm), lambda i, j, kk: (kk, i + a_off // tm))
        b_spec = pl.BlockSpec((tk, tn), lambda i, j, kk: (kk, j))
        o_spec = pl.BlockSpec((tm, tn), lambda i, j, kk: (i, j))

        def body(*refs):
            a_refs, b_ref, out_ref = refs[:na], refs[na], refs[na + 1]

            @pl.when(pl.program_id(2) == 0)
            def _():
                out_ref[...] = jnp.zeros_like(out_ref)

            av = a_refs[0][...] if fa is None else fa(*[r[...] for r in a_refs])
            out_ref[...] += _dot(av, b_ref[...], _TN)

        grid, sem = (m // tm, n // tn, kdim // tk), ("parallel", "parallel", "arbitrary")
    assert m % tm == 0 and n % tn == 0, (name, m, n, tm, tn)
    out = jax.ShapeDtypeStruct((m, n), out_dtype)
    if side is None:
        return pl.pallas_call(
            body, name=name, grid=grid,
            in_specs=[a_spec] * na + [b_spec] + [o_spec] * no + [r_spec] * nr,
            out_specs=o_spec if n_out == 1 else [o_spec] * n_out, out_shape=out if n_out == 1 else [out] * n_out,
            compiler_params=_params(sem),
        )(a, *a_extra, b, *o_extra, *r_extra)
    kind, arr, _, rows = side
    landed = jax.ShapeDtypeStruct(((N_DEV, rows) if kind == "gather" else (4, rows)) + arr.shape[-1:], arr.dtype)
    return pl.pallas_call(
        body, name=name, grid=grid,
        in_specs=[a_spec] * na + [b_spec] + [o_spec] * no + [r_spec] * nr + [_ANY],
        out_specs=[o_spec] * n_out + [_ANY], out_shape=[out] * n_out + [landed],
        scratch_shapes=list(_AG_SEMS if kind == "gather" else _RS_SEMS),
        compiler_params=_params(sem),
    )(a, *a_extra, b, *o_extra, *r_extra, arr)


def rowk(fn, tiled, full, out_w, acc_shapes, *, rows, name, out_dtypes=None):
    tt = min(ROW_TILE, rows)
    n = rows // tt
    assert rows % tt == 0
    nt, nf, no = len(tiled), len(full), len(out_w)

    def tspec(w, cb):
        return pl.BlockSpec((tt, w), lambda i: (i, cb))

    def fspec(a):
        nd = a.ndim
        return pl.BlockSpec(a.shape, lambda i: (0,) * nd)

    def body(*refs):
        ins, fulls = refs[:nt], refs[nt:nt + nf]
        outs, accs = refs[nt + nf:nt + nf + no], refs[nt + nf + no:]
        res_t, res_a = fn(*[r[...] for r in ins], *[r[...] for r in fulls])
        for r, v in zip(outs, res_t):
            r[...] = v.astype(r.dtype)
        if accs:
            @pl.when(pl.program_id(0) == 0)
            def _():
                for r in accs:
                    r[...] = jnp.zeros_like(r)
            for r, v in zip(accs, res_a):
                r[...] += v

    outs = pl.pallas_call(
        body, name=name, grid=(n,),
        in_specs=[tspec(w, cb) for (_, w, cb) in tiled] + [fspec(a) for a in full],
        out_specs=[tspec(w, 0) for w in out_w] + [pl.BlockSpec(s, lambda i, nd=len(s): (0,) * nd) for s in acc_shapes],
        out_shape=[jax.ShapeDtypeStruct((rows, w), dt) for w, dt in zip(out_w, out_dtypes or [F32] * no)]
        + [jax.ShapeDtypeStruct(s, F32) for s in acc_shapes],
        compiler_params=_params(("arbitrary",)),
    )(*[a for (a, _, _) in tiled], *full)
    return outs[:no], outs[no:]


def _colsum(x):
    return jnp.sum(x, axis=0, keepdims=True)


def _rowsum(x):
    return jnp.sum(x, axis=1, keepdims=True)


def _ln_epilogue(acc, resid, g, b):
    pre = ALPHA * resid + acc
    mu = jnp.mean(pre, axis=1, keepdims=True)
    xc = pre - mu
    var = jnp.mean(xc * xc, axis=1, keepdims=True)
    return pre, xc * lax.rsqrt(var + LN_EPS) * g + b


def _ln_bwd_fn(pre, dout, g):
    mu = jnp.mean(pre, axis=1, keepdims=True)
    xc = pre - mu
    var = jnp.mean(xc * xc, axis=1, keepdims=True)
    rstd = lax.rsqrt(var + LN_EPS)
    xhat = xc * rstd
    dxh = dout * g
    dpre = rstd * (dxh - jnp.mean(dxh, axis=1, keepdims=True) - xhat * jnp.mean(dxh * xhat, axis=1, keepdims=True))
    return (dpre,), (_colsum(dout * xhat), _colsum(dout))


def mm_ln(a, w, resid, g, b, *, name, fa=None, side=None):
    assert w.shape[1] == D_MODEL
    return mm(a, w, fa=fa, o_extra=(resid,), r_extra=(g, b), fo=_ln_epilogue, n_out=2, name=name, side=side)


def ln_bwd(pre, dout, g, *, name):
    (dpre,), (dg, db) = rowk(_ln_bwd_fn, [(pre, D_MODEL, 0), (dout, D_MODEL, 0)], [g],
                             [D_MODEL], [(1, D_MODEL), (1, D_MODEL)], rows=pre.shape[0], name=name)
    return dpre, dg, db


def _loss_fn(y, tgt):
    e = y - tgt
    part = _colsum(_rowsum(e * e)) * (0.5 / D_MODEL)
    return (e * (1.0 / D_MODEL),), (part,)


_XA_SCALE = 1.0 / math.sqrt(XA_HEAD_DIM)


def _attn_probs(qh, kh):
    s = _dot(qh, kh, _NT) * _XA_SCALE
    e = jnp.exp(s - jnp.max(s, axis=1, keepdims=True))
    return e / _rowsum(e)


def _attn_fwd_fn(q, k, v):
    outs = []
    for hd in range(XA_HEADS):
        sl = slice(hd * XA_HEAD_DIM, (hd + 1) * XA_HEAD_DIM)
        outs.append(_dot(_attn_probs(q[:, sl], k[:, sl]), v[:, sl]))
    return (jnp.concatenate(outs, axis=1),), ()


def _attn_bwd_fn(q, do, k, v):
    dqs, dks, dvs = [], [], []
    for hd in range(XA_HEADS):
        sl = slice(hd * XA_HEAD_DIM, (hd + 1) * XA_HEAD_DIM)
        qh, kh, vh, doh = q[:, sl], k[:, sl], v[:, sl], do[:, sl]
        p = _attn_probs(qh, kh)
        dp = _dot(doh, vh, _NT)
        ds = p * (dp - _rowsum(p * dp)) * _XA_SCALE
        dqs.append(_dot(ds, kh))
        dks.append(_dot(ds, qh, _TN))
        dvs.append(_dot(p, doh, _TN))
    cat = lambda xs: jnp.concatenate(xs, axis=1)
    return (cat(dqs),), (cat(dks), cat(dvs))


def _s5_post_fwd_fn(ylin, u, dskip, gw, gb):
    yg = _gelu(ylin + dskip * u)
    return (yg * _sigmoid(_dot(yg, gw) + gb),), ()


def _s5_post_bwd_fn(ylin, u, dout, dskip, gw, gb):
    pre = ylin + dskip * u
    yg = _gelu(pre)
    sg = _sigmoid(_dot(yg, gw) + gb)
    dlin = dout * yg * sg * (1.0 - sg)
    dyg = dout * sg + _dot(dlin, gw, _NT)
    dpre = dyg * _dgelu(pre)
    return (dpre, dpre * dskip), (_colsum(dpre * u), _dot(yg, dlin, _TN), _colsum(dlin))


def _rg_gates(xc, wa, wx, ba, bx, lam):
    r = _sigmoid(_dot(xc, wa) + ba)
    i = _sigmoid(_dot(xc, wx) + bx)
    sp = _softplus(-lam)
    log_a = -RG_C * r * sp
    a = jnp.exp(log_a)
    mult = jnp.sqrt(_neg_expm1(2.0 * log_a))
    return r, i, sp, a, mult


def _rg_pre_bwd_fn(xc, gsc, hprev, wa, wx, ba, bx, lam):
    r, i, sp, a, mult = _rg_gates(xc, wa, wx, ba, bx, lam)
    da = gsc * hprev
    db = gsc
    dmult = db * i * xc
    di = db * mult * xc
    dxc = db * mult * i
    dlog_a = da * a - a * a * dmult / mult
    dr = dlog_a * (-RG_C * sp)
    dsp = _colsum(dlog_a * (-RG_C * r))
    dlam = dsp * (-_sigmoid(-lam))
    dpr = dr * r * (1.0 - r)
    dpi = di * i * (1.0 - i)
    dxc = dxc + _dot(dpr, wa, _NT) + _dot(dpi, wx, _NT)
    return (dxc,), (_dot(xc, dpr, _TN), _dot(xc, dpi, _TN), _colsum(dpr), _colsum(dpi), dlam)


def _conv_taps(x_ref, halo_ref, first):
    x = x_ref[...]
    halo = jnp.where(first, 0.0, halo_ref[...])
    rows8 = lax.broadcasted_iota(jnp.int32, halo.shape, 0)
    taps = [x]
    for j in (1, 2, 3):
        r = pltpu.roll(x, j, 0)
        top = jnp.where(rows8 < j, pltpu.roll(halo, j, 0), r[0:8])
        taps.append(jnp.concatenate([top, r[8:]], axis=0))
    return taps


def _conv_pre(taps, cw_ref, cb_ref):
    wv = cw_ref[...]
    pre = cb_ref[...] + wv[3:4, :] * taps[0]
    for j in (1, 2, 3):
        pre = pre + wv[3 - j:4 - j, :] * taps[j]
    return pre


def _conv_back(dpre, taps, cw_ref, nxt_ref):
    q = dpre.shape[0]
    rows8 = lax.broadcasted_iota(jnp.int32, (8, dpre.shape[1]), 0)
    wv = cw_ref[...]
    dx = wv[3:4, :] * dpre
    for j in (1, 2, 3):
        r = pltpu.roll(dpre, q - j, 0)
        bottom = jnp.where(rows8 >= 8 - j, pltpu.roll(nxt_ref[...], 8 - j, 0), r[q - 8:q])
        dx = dx + wv[3 - j:4 - j, :] * jnp.concatenate([r[:q - 8], bottom], axis=0)
    dw = jnp.concatenate([_colsum(dpre * taps[3 - kk]) for kk in range(4)], axis=0)
    nxt_ref[...] = dpre[0:8]
    return dx, dw, _colsum(dpre)


S5_CW = 256


def _cmul(ar, ai, br, bi):
    return ar * br - ai * bi, ar * bi + ai * br


def _scan8_complex(src_ref, dst_ref, lam_ref, st_ref, *, w, nb, reverse):
    rows = lax.broadcasted_iota(jnp.int32, (8, S5_CW), 0)
    b8 = lambda v: jnp.broadcast_to(v, (8, S5_CW))

    def shift(x, k):
        if reverse:
            return jnp.where(rows < 8 - k, pltpu.roll(x, 8 - k, 0), 0.0)
        return jnp.where(rows >= k, pltpu.roll(x, k, 0), 0.0)

    for c0 in range(0, w, S5_CW):
        re, im = pl.ds(c0, S5_CW), pl.ds(w + c0, S5_CW)
        pw = [(lam_ref[:, re], lam_ref[:, im])]
        for _ in range(7):
            pw.append(_cmul(*pw[-1], *pw[0]))
        pr, pi = b8(pw[7][0]), b8(pw[7][1])
        for j in range(7):
            sel = rows == (7 - j if reverse else j)
            pr, pi = jnp.where(sel, b8(pw[j][0]), pr), jnp.where(sel, b8(pw[j][1]), pi)
        steps = [(k, b8(pw[k - 1][0]), b8(pw[k - 1][1])) for k in (1, 2, 4)]
        edge = 0 if reverse else 7

        def blk(i, carry):
            hr, hi = carry
            base = pl.multiple_of((nb // 2 - 1 - i if reverse else i) * 16, 16)
            pend = []
            for off in ((8, 0) if reverse else (0, 8)):
                at = pl.ds(base + off, 8)
                xr, xi = src_ref[at, re], src_ref[at, im]
                for k, kr, ki in steps:
                    sr, si = shift(xr, k), shift(xi, k)
                    xr, xi = xr + kr * sr - ki * si, xi + kr * si + ki * sr
                pend.append((at, xr, xi))
            for at, xr, xi in pend:
                xr, xi = xr + pr * hr - pi * hi, xi + pr * hi + pi * hr
                dst_ref[at, re] = xr
                dst_ref[at, im] = xi
                hr, hi = b8(xr[edge:edge + 1, :]), b8(xi[edge:edge + 1, :])
            return hr, hi

        hr, hi = lax.fori_loop(0, nb // 2, blk, (st_ref[:, re], st_ref[:, im]))
        st_ref[:, re] = hr
        st_ref[:, im] = hi


def s5_fwd(proj, bcat, lam, ccat, dskip, gw, gb, *, name):
    t = proj.shape[0]
    tt = min(ROW_TILE, t)
    w2 = bcat.shape[1]

    def body(u_ref, b_ref, lam_ref, c_ref, d_ref, gw_ref, gb_ref, h_ref, y_ref, o_ref, bu_ref, st_ref):
        @pl.when(pl.program_id(0) == 0)
        def _():
            st_ref[...] = jnp.zeros_like(st_ref)

        u = u_ref[...]
        bu_ref[...] = _dot(u, b_ref[...])
        _scan8_complex(bu_ref, h_ref, lam_ref, st_ref, w=w2 // 2, nb=tt // 8, reverse=False)
        ylin = _dot(h_ref[...], c_ref[...])
        y_ref[...] = ylin
        (out,), _ = _s5_post_fwd_fn(ylin, u, d_ref[...], gw_ref[...], gb_ref[...])
        o_ref[...] = out.astype(o_ref.dtype)

    fixed = lambda a: pl.BlockSpec(a.shape, lambda i: (0, 0))
    row = pl.BlockSpec((tt, S5_WIDTH), lambda i: (i, 0))
    return pl.pallas_call(
        body, name=name, grid=(t // tt,),
        in_specs=[pl.BlockSpec((tt, S5_WIDTH), lambda i: (i, P_U // S5_WIDTH))] + [fixed(x) for x in (bcat, lam, ccat, dskip, gw, gb)],
        out_specs=[pl.BlockSpec((tt, w2), lambda i: (i, 0)), row, row],
        out_shape=[jax.ShapeDtypeStruct((t, w2), F32), jax.ShapeDtypeStruct((t, S5_WIDTH), F32),
                   jax.ShapeDtypeStruct((t, S5_WIDTH), BF16)],
        scratch_shapes=[pltpu.VMEM((tt, w2), F32), pltpu.VMEM((8, w2), F32)],
        compiler_params=_params(("arbitrary",)),
    )(proj, bcat, lam, ccat, dskip, gw, gb)


def s5_bwd(dycat, ylin, hs, proj, bcat, lam_adj, ccat, dskip, gw, gb, *, name):
    t = proj.shape[0]
    tt = min(ROW_TILE, t)
    n, w2 = t // tt, bcat.shape[1]
    w = w2 // 2

    def body(dout_ref, yl_ref, h_ref, hp_ref, u_ref, b_ref, lam_ref, c_ref, d_ref, gw_ref, gb_ref,
             du_ref, dc_ref, db_ref, dar_ref, dai_ref, dd_ref, dgw_ref, dgb_ref, g_ref, st_ref):
        i = pl.program_id(0)

        @pl.when(i == 0)
        def _():
            for r in (st_ref, dc_ref, db_ref, dar_ref, dai_ref, dd_ref, dgw_ref, dgb_ref):
                r[...] = jnp.zeros_like(r)

        (dy, du_a), post = _s5_post_bwd_fn(yl_ref[...], u_ref[...], dout_ref[...], d_ref[...], gw_ref[...], gb_ref[...])
        for r, v in zip((dd_ref, dgw_ref, dgb_ref), post):
            r[...] += v
        h = h_ref[...]
        g_ref[...] = _dot(dy, c_ref[...], _NT)
        dc_ref[...] += _dot(h, dy, _TN)
        _scan8_complex(g_ref, g_ref, lam_ref, st_ref, w=w, nb=tt // 8, reverse=True)
        g = g_ref[...]
        du_ref[...] = (du_a + _dot(g, b_ref[...], _NT)).astype(du_ref.dtype)
        db_ref[...] += _dot(u_ref[...], g, _TN)
        rows = lax.broadcasted_iota(jnp.int32, (tt, w2), 0)
        before = jnp.where(i == n - 1, 0.0, hp_ref[7:8, :])
        hprev = jnp.where(rows == 0, before, pltpu.roll(h, 1, 0))
        gr, gi, hr, hi = g[:, :w], g[:, w:], hprev[:, :w], hprev[:, w:]
        dar_ref[...] += _colsum(gr * hr + gi * hi)
        dai_ref[...] += _colsum(gi * hr - gr * hi)

    rev = lambda i: n - 1 - i
    row = lambda wd, cb=0: pl.BlockSpec((tt, wd), lambda i: (rev(i), cb))
    fixed = lambda shape: pl.BlockSpec(shape, lambda i: (0, 0))
    return pl.pallas_call(
        body, name=name, grid=(n,),
        in_specs=[row(S5_WIDTH, 2), row(S5_WIDTH), row(w2),
                  pl.BlockSpec((8, w2), lambda i: (jnp.maximum(rev(i) * (tt // 8) - 1, 0), 0)),
                  row(S5_WIDTH, P_U // S5_WIDTH)] + [fixed(x.shape) for x in (bcat, lam_adj, ccat, dskip, gw, gb)],
        out_specs=[row(S5_WIDTH), fixed(ccat.shape), fixed(bcat.shape), fixed((1, w)), fixed((1, w)),
                   fixed((1, S5_WIDTH)), fixed((S5_WIDTH, S5_WIDTH)), fixed((1, S5_WIDTH))],
        out_shape=[jax.ShapeDtypeStruct((t, S5_WIDTH), BF16), jax.ShapeDtypeStruct(ccat.shape, F32),
                   jax.ShapeDtypeStruct(bcat.shape, F32), jax.ShapeDtypeStruct((1, w), F32), jax.ShapeDtypeStruct((1, w), F32),
                   jax.ShapeDtypeStruct((1, S5_WIDTH), F32), jax.ShapeDtypeStruct((S5_WIDTH, S5_WIDTH), F32),
                   jax.ShapeDtypeStruct((1, S5_WIDTH), F32)],
        scratch_shapes=[pltpu.VMEM((tt, w2), F32), pltpu.VMEM((8, w2), F32)],
        compiler_params=_params(("arbitrary",)),
    )(dycat, ylin, hs, hs, proj, bcat, lam_adj, ccat, dskip, gw, gb)


def _scan8_real(a_ref, b_ref, o_ref, st_ref, *, nb, reverse):
    w = o_ref.shape[1]
    rows = lax.broadcasted_iota(jnp.int32, (8, w), 0)
    edge = 0 if reverse else 7

    def shift(x, k, fill):
        if reverse:
            return jnp.where(rows < 8 - k, pltpu.roll(x, 8 - k, 0), fill)
        return jnp.where(rows >= k, pltpu.roll(x, k, 0), fill)

    def blk(i, h):
        at = pl.ds(pl.multiple_of((nb - 1 - i if reverse else i) * 8, 8), 8)
        a, b = a_ref[at, :], b_ref[at, :]
        for k in (1, 2, 4):
            a, b = a * shift(a, k, 1.0), b + a * shift(b, k, 0.0)
        out = b + a * h
        o_ref[at, :] = out
        return jnp.broadcast_to(out[edge:edge + 1, :], (8, w))

    st_ref[...] = lax.fori_loop(0, nb, blk, st_ref[...])


def _rg_specs(tt, idx):
    return [pl.BlockSpec((tt, RG_WIDTH), lambda i: (idx(i), P_XR // RG_WIDTH)),
            pl.BlockSpec((8, RG_WIDTH), lambda i: (jnp.maximum(idx(i) * (tt // 8) - 1, 0), P_XR // RG_WIDTH)),
            pl.BlockSpec((tt, RG_WIDTH), lambda i: (idx(i), P_G // RG_WIDTH))]


def rg_fwd(proj, cw, cb, wa, wx, ba, bx, lam, *, name):
    t = proj.shape[0]
    tt = min(ROW_TILE, t)
    w = RG_WIDTH

    def body(x_ref, halo_ref, g_ref, cw_ref, cb_ref, wa_ref, wx_ref, ba_ref, bx_ref, lam_ref,
             y_ref, xc_ref, a_ref, h_ref, b_ref, st_ref):
        @pl.when(pl.program_id(0) == 0)
        def _():
            st_ref[...] = jnp.zeros_like(st_ref)

        xc = _conv_pre(_conv_taps(x_ref, halo_ref, pl.program_id(0) == 0), cw_ref, cb_ref)
        xc_ref[...] = xc
        r, i, sp, a, mult = _rg_gates(xc, wa_ref[...], wx_ref[...], ba_ref[...], bx_ref[...], lam_ref[...])
        a_ref[...] = a
        b_ref[...] = mult * (i * xc)
        _scan8_real(a_ref, b_ref, h_ref, st_ref, nb=tt // 8, reverse=False)
        y_ref[...] = (h_ref[...] * _gelu(g_ref[...])).astype(y_ref.dtype)

    fixed = lambda a: pl.BlockSpec(a.shape, lambda i: (0, 0))
    row = pl.BlockSpec((tt, w), lambda i: (i, 0))
    return pl.pallas_call(
        body, name=name, grid=(t // tt,),
        in_specs=_rg_specs(tt, lambda i: i) + [fixed(x) for x in (cw, cb, wa, wx, ba, bx, lam)],
        out_specs=[row] * 4,
        out_shape=[jax.ShapeDtypeStruct((t, w), BF16)] + [jax.ShapeDtypeStruct((t, w), F32)] * 3,
        scratch_shapes=[pltpu.VMEM((tt, w), F32), pltpu.VMEM((8, w), F32)],
        compiler_params=_params(("arbitrary",)),
    )(proj, proj, proj, cw, cb, wa, wx, ba, bx, lam)


def rg_bwd(proj, dycat, xc, a, h, cw, cb, wa, wx, ba, bx, lam, *, name):
    t = proj.shape[0]
    tt = min(ROW_TILE, t)
    n, w = t // tt, RG_WIDTH

    def body(x_ref, halo_ref, g_ref, dy_ref, xc_ref, a_ref, h_ref, hp_ref, cw_ref, wa_ref, wx_ref, ba_ref, bx_ref, lam_ref,
             dx_ref, dg_ref, dcw_ref, dcb_ref, dwa_ref, dwx_ref, dba_ref, dbx_ref, dlam_ref,
             au_ref, dh_ref, gs_ref, st_ref, anx_ref, nxt_ref):
        i = pl.program_id(0)
        accs = (dcw_ref, dcb_ref, dwa_ref, dwx_ref, dba_ref, dbx_ref, dlam_ref)

        @pl.when(i == 0)
        def _():
            for r in accs + (st_ref, anx_ref, nxt_ref):
                r[...] = jnp.zeros_like(r)

        h, g, dy, a = h_ref[...], g_ref[...], dy_ref[...], a_ref[...]
        dh_ref[...] = dy * _gelu(g)
        dg_ref[...] = (dy * h * _dgelu(g)).astype(dg_ref.dtype)
        rows = lax.broadcasted_iota(jnp.int32, (tt, w), 0)
        au_ref[...] = jnp.where(rows == tt - 1, anx_ref[0:1, :], pltpu.roll(a, tt - 1, 0))
        _scan8_real(au_ref, dh_ref, gs_ref, st_ref, nb=tt // 8, reverse=True)
        before = jnp.where(i == n - 1, 0.0, hp_ref[7:8, :])
        hprev = jnp.where(rows == 0, before, pltpu.roll(h, 1, 0))
        (dxc,), small = _rg_pre_bwd_fn(xc_ref[...], gs_ref[...], hprev, wa_ref[...], wx_ref[...], ba_ref[...], bx_ref[...], lam_ref[...])
        dx, dcw, dcb = _conv_back(dxc, _conv_taps(x_ref, halo_ref, i == n - 1), cw_ref, nxt_ref)
        dx_ref[...] = dx.astype(dx_ref.dtype)
        for r, v in zip(accs, (dcw, dcb) + tuple(small)):
            r[...] += v
        anx_ref[...] = a[0:8]

    rev = lambda i: n - 1 - i
    row = lambda cb_=0: pl.BlockSpec((tt, w), lambda i: (rev(i), cb_))
    fixed = lambda shape: pl.BlockSpec(shape, lambda i: (0, 0))
    acc_shapes = [(4, w), (1, w), (w, w), (w, w), (1, w), (1, w), (1, w)]
    return pl.pallas_call(
        body, name=name, grid=(n,),
        in_specs=_rg_specs(tt, rev) + [row(3), row(), row(), row(),
                                       pl.BlockSpec((8, w), lambda i: (jnp.maximum(rev(i) * (tt // 8) - 1, 0), 0))]
        + [fixed(x.shape) for x in (cw, wa, wx, ba, bx, lam)],
        out_specs=[row(), row()] + [fixed(sh) for sh in acc_shapes],
        out_shape=[jax.ShapeDtypeStruct((t, w), BF16)] * 2 + [jax.ShapeDtypeStruct(sh, F32) for sh in acc_shapes],
        scratch_shapes=[pltpu.VMEM((tt, w), F32)] * 3 + [pltpu.VMEM((8, w), F32)] * 3,
        compiler_params=_params(("arbitrary",)),
    )(proj, proj, proj, dycat, xc, a, h, h, cw, wa, wx, ba, bx, lam)


SSD_QQ = SSD_HEADS * SSD_CHUNK
SSD_GP = SSD_WIDTH // 2
SSD_GQ = SSD_QQ // 2


def _ssd_spread():
    h = jnp.arange(LANE)[:, None]
    spread_p = (jnp.arange(SSD_WIDTH)[None, :] // SSD_HEAD_DIM == h).astype(BF16)
    spread_q = (jnp.arange(SSD_QQ)[None, :] // SSD_CHUNK == h).astype(BF16)
    return spread_p, spread_q


def _ssd_prologue(dt_ref, prow_ref, sp_ref, sq_ref):
    q = SSD_CHUNK
    r = lax.broadcasted_iota(jnp.int32, (q, q), 0)
    c = lax.broadcasted_iota(jnp.int32, (q, q), 1)
    raw_c = dt_ref[...] + prow_ref[0:1, :]
    dt_c = _softplus(raw_c)
    a_r = -jnp.exp(prow_ref[1:2, :])
    cs_c = _dot_mask((r >= c).astype(F32), dt_c * a_r, mask_left=True, parts=3)
    both = _dot_mask(jnp.concatenate([dt_c, cs_c], axis=0), sp_ref[...], mask_left=False, parts=3)
    dt_x, cs_x = both[:q], both[q:]
    csx = _dot_mask(cs_c, sq_ref[...], mask_left=False, parts=3)
    rr = lax.broadcasted_iota(jnp.int32, (q, SSD_QQ), 0)
    ss = lax.broadcasted_iota(jnp.int32, (q, SSD_QQ), 1) & (q - 1)
    diag = rr == ss
    cs_row = _colsum(jnp.where(diag, csx, 0.0))
    lcat = jnp.exp(jnp.where(rr >= ss, csx - cs_row, -1e30))
    cl = cs_x[q - 1:q, :]
    return dict(raw_c=raw_c, dt_c=dt_c, a_r=a_r, dt_x=dt_x, cs_x=cs_x, lcat=lcat, diag=diag,
                ecs=jnp.exp(cs_x), wdec=jnp.exp(cl - cs_x), ecl=jnp.exp(cl), triu=(r <= c).astype(F32))


def _ssd_group(xbc_ref, g, lcat, xdt):
    ns, q = SSD_STATE, SSD_CHUNK
    bm = xbc_ref[:, pl.ds(SSD_WIDTH + g * ns, ns)]
    cm = xbc_ref[:, pl.ds(SSD_WIDTH + 2 * ns + g * ns, ns)]
    cb = _dot(cm, bm, _NT)
    lg = lcat[:, g * SSD_GQ:(g + 1) * SSD_GQ]
    wcat = jnp.concatenate([cb] * 4, axis=1) * lg
    head = lax.broadcasted_iota(jnp.int32, (1, SSD_GP), 1) // SSD_HEAD_DIM
    xg = xdt[:, g * SSD_GP:(g + 1) * SSD_GP]
    xbd = jnp.concatenate([jnp.where(head == j, xg, 0.0) for j in range(4)], axis=0)
    return bm, cm, lg, wcat, xbd, head


def _ssd_gate(yraw, z, nw):
    yg = yraw * _silu(z)
    r = lax.rsqrt(jnp.mean(yg * yg, axis=1, keepdims=True) + LN_EPS)
    return yg, r


def _ssd_specs(q, idx):
    return [pl.BlockSpec((q, SSD_XBC), lambda i: (idx(i), P_XBC // SSD_XBC)),
            pl.BlockSpec((8, SSD_XBC), lambda i: (jnp.maximum(idx(i) * (q // 8) - 1, 0), P_XBC // SSD_XBC)),
            pl.BlockSpec((q, SSD_WIDTH), lambda i: (idx(i), P_Z // SSD_WIDTH)),
            pl.BlockSpec((q, LANE), lambda i: (idx(i), P_DT // LANE)),
            pl.BlockSpec((4, SSD_XBC), lambda i: (0, 0)), pl.BlockSpec((1, SSD_XBC), lambda i: (0, 0)),
            pl.BlockSpec((8, LANE), lambda i: (0, 0)), pl.BlockSpec((1, SSD_WIDTH), lambda i: (0, 0)),
            pl.BlockSpec((1, SSD_WIDTH), lambda i: (0, 0)),
            pl.BlockSpec((LANE, SSD_WIDTH), lambda i: (0, 0)), pl.BlockSpec((LANE, SSD_QQ), lambda i: (0, 0))]


def ssd_fwd(proj, cw, cb, prow, d_x, nw, *, name):
    t = proj.shape[0]
    q, ns = SSD_CHUNK, SSD_STATE
    nc = t // q
    spread_p, spread_q = _ssd_spread()

    def body(x_ref, halo_ref, z_ref, dt_ref, cw_ref, cb_ref, prow_ref, dx_ref, nw_ref, sp_ref, sq_ref,
             y_ref, yraw_ref, sall_ref, s_ref, xbc_ref):
        @pl.when(pl.program_id(0) == 0)
        def _():
            s_ref[...] = jnp.zeros_like(s_ref)

        sall_ref[0] = s_ref[...]
        xbc_ref[...] = _silu(_conv_pre(_conv_taps(x_ref, halo_ref, pl.program_id(0) == 0), cw_ref, cb_ref))
        pr = _ssd_prologue(dt_ref, prow_ref, sp_ref, sq_ref)
        xs = xbc_ref[:, pl.ds(0, SSD_WIDTH)]
        xdt = xs * pr["dt_x"]
        xw = xdt * pr["wdec"]
        ys = []
        for g in range(2):
            gp = slice(g * SSD_GP, (g + 1) * SSD_GP)
            bm, cm, lg, wcat, xbd, head = _ssd_group(xbc_ref, g, pr["lcat"], xdt)
            st = s_ref[:, gp]
            ys.append(_dot(wcat, xbd) + pr["ecs"][:, gp] * _dot(cm, st) + xs[:, gp] * dx_ref[:, gp])
            s_ref[:, gp] = pr["ecl"][:, gp] * st + _dot(bm, xw[:, gp], _TN)
        yraw = jnp.concatenate(ys, axis=1)
        yraw_ref[...] = yraw
        yg, r = _ssd_gate(yraw, z_ref[...], nw_ref[...])
        y_ref[...] = (yg * r * nw_ref[...]).astype(y_ref.dtype)

    row = pl.BlockSpec((q, SSD_WIDTH), lambda i: (i, 0))
    return pl.pallas_call(
        body, name=name, grid=(nc,),
        in_specs=_ssd_specs(q, lambda i: i),
        out_specs=[row, row, pl.BlockSpec((1, ns, SSD_WIDTH), lambda i: (i, 0, 0))],
        out_shape=[jax.ShapeDtypeStruct((t, SSD_WIDTH), BF16), jax.ShapeDtypeStruct((t, SSD_WIDTH), F32),
                   jax.ShapeDtypeStruct((nc, ns, SSD_WIDTH), F32)],
        scratch_shapes=[pltpu.VMEM((ns, SSD_WIDTH), F32), pltpu.VMEM((q, SSD_XBC), F32)],
        compiler_params=_params(("arbitrary",)),
    )(proj, proj, proj, proj, cw, cb, prow, d_x, nw, spread_p, spread_q)


def ssd_bwd(proj, cw, cb, prow, d_x, nw, yraw, sall, dout, *, name):
    t = proj.shape[0]
    q, ns = SSD_CHUNK, SSD_STATE
    nc = t // q
    spread_p, spread_q = _ssd_spread()

    def body(x_ref, halo_ref, z_ref, dt_ref, cw_ref, cb_ref, prow_ref, dx_ref, nw_ref, sp_ref, sq_ref, yraw_ref, sall_ref, dout_ref,
             dxraw_ref, dz_ref, ddt_ref, dprm_ref, ddx_ref, dnw_ref, dcw_ref, dcb_ref, ds_ref, xbc_ref, dxbc_ref, nxt_ref):
        @pl.when(pl.program_id(0) == 0)
        def _():
            for r in (ds_ref, dprm_ref, ddx_ref, dnw_ref, dcw_ref, dcb_ref, nxt_ref):
                r[...] = jnp.zeros_like(r)

        taps = _conv_taps(x_ref, halo_ref, pl.program_id(0) == nc - 1)
        conv_pre = _conv_pre(taps, cw_ref, cb_ref)
        xbc_ref[...] = _silu(conv_pre)

        yraw, z, nwv, dout = yraw_ref[...], z_ref[...], nw_ref[...], dout_ref[...]
        yg, r = _ssd_gate(yraw, z, nwv)
        dnw_ref[...] += _colsum(dout * yg * r)
        dyn = dout * nwv
        dyg = r * dyn - yg * (r * r * r) * jnp.mean(dyn * yg, axis=1, keepdims=True)
        dy = dyg * _silu(z)
        dz_ref[...] = (dyg * yraw * _dsilu(z)).astype(dz_ref.dtype)

        pr = _ssd_prologue(dt_ref, prow_ref, sp_ref, sq_ref)
        xs = xbc_ref[:, pl.ds(0, SSD_WIDTH)]
        xdt = xs * pr["dt_x"]
        wdec, ecl = pr["wdec"], pr["ecl"]
        xw = xdt * wdec
        dzm_all = pr["ecs"] * dy
        last = (lax.broadcasted_iota(jnp.int32, (q, 1), 0) == q - 1).astype(F32)
        dxs, dcsxs, es = [], [], []
        for g in range(2):
            gp = slice(g * SSD_GP, (g + 1) * SSD_GP)
            bm, cm, lg, wcat, xbd, head = _ssd_group(xbc_ref, g, pr["lcat"], xdt)
            dyg_ = dy[:, gp]
            dwcat = _dot(dyg_, xbd, _NT)
            dxbd = _dot(wcat, dyg_, _TN)
            dxg = sum(jnp.where(head == j, dxbd[j * q:(j + 1) * q], 0.0) for j in range(4))
            es.append(dwcat * wcat)
            dmm = dwcat * lg
            dm = dmm[:, 0:q] + dmm[:, q:2 * q] + dmm[:, 2 * q:3 * q] + dmm[:, 3 * q:4 * q]
            dcm = _dot(dm, bm)
            dbm = _dot(dm, cm, _TN)
            st = sall_ref[0, :, gp]
            zmat = _dot(cm, st)
            dzm = dzm_all[:, gp]
            dcm = dcm + _dot(dzm, st, _NT)
            dst = _dot(cm, dzm, _TN)
            dcsx = dzm * zmat
            dsn = ds_ref[:, gp]
            dst = dst + ecl[:, gp] * dsn
            dclx = _colsum(dsn * st) * ecl[:, gp]
            dxw = _dot(bm, dsn)
            dbm = dbm + _dot(xw[:, gp], dsn, _NT)
            dxg = dxg + wdec[:, gp] * dxw
            tw = dxw * xdt[:, gp] * wdec[:, gp]
            dclx = dclx + _colsum(tw)
            dcsxs.append(dcsx - tw + last * dclx)
            ds_ref[:, gp] = dst
            dxs.append(dxg)
            dxbc_ref[:, pl.ds(SSD_WIDTH + g * ns, ns)] = dbm
            dxbc_ref[:, pl.ds(SSD_WIDTH + 2 * ns + g * ns, ns)] = dcm
        dx = jnp.concatenate(dxs, axis=1)
        dxbc_ref[:, pl.ds(0, SSD_WIDTH)] = dx * pr["dt_x"] + dy * dx_ref[...]
        ddx_ref[...] += _colsum(dy * xs)
        red = _dot_mask(jnp.concatenate([jnp.concatenate(dcsxs, axis=1), dx * xs], axis=0), sp_ref[...], _NT,
                        mask_left=False, parts=2)
        e_all = jnp.concatenate(es, axis=1)
        e_red = _dot_mask(e_all - jnp.where(pr["diag"], _colsum(e_all), 0.0), sq_ref[...], _NT, mask_left=False, parts=2)
        dadt = _dot_mask(pr["triu"], red[:q] + e_red, mask_left=True, parts=2)
        draw = (red[q:] + dadt * pr["a_r"]) * _sigmoid(pr["raw_c"])
        ddt_ref[...] = draw.astype(ddt_ref.dtype)
        zero = jnp.zeros((6, LANE), F32)
        dprm_ref[...] += jnp.concatenate([_colsum(draw), _colsum(dadt * pr["dt_c"]) * pr["a_r"], zero], axis=0)
        dxr, dcw, dcb = _conv_back(dxbc_ref[...] * _dsilu(conv_pre), taps, cw_ref, nxt_ref)
        dxraw_ref[...] = dxr.astype(dxraw_ref.dtype)
        dcw_ref[...] += dcw
        dcb_ref[...] += dcb

    rev = lambda i: nc - 1 - i
    row = lambda w: pl.BlockSpec((q, w), lambda i: (rev(i), 0))
    fixed = lambda shape: pl.BlockSpec(shape, lambda i: (0, 0))
    return pl.pallas_call(
        body, name=name, grid=(nc,),
        in_specs=_ssd_specs(q, rev) + [row(SSD_WIDTH), pl.BlockSpec((1, ns, SSD_WIDTH), lambda i: (rev(i), 0, 0)),
                                       row(SSD_WIDTH)],
        out_specs=[row(SSD_XBC), row(SSD_WIDTH), row(LANE), fixed((8, LANE)), fixed((1, SSD_WIDTH)), fixed((1, SSD_WIDTH)),
                   fixed((4, SSD_XBC)), fixed((1, SSD_XBC))],
        out_shape=[jax.ShapeDtypeStruct((t, SSD_XBC), BF16), jax.ShapeDtypeStruct((t, SSD_WIDTH), BF16),
                   jax.ShapeDtypeStruct((t, LANE), BF16), jax.ShapeDtypeStruct((8, LANE), F32),
                   jax.ShapeDtypeStruct((1, SSD_WIDTH), F32), jax.ShapeDtypeStruct((1, SSD_WIDTH), F32),
                   jax.ShapeDtypeStruct((4, SSD_XBC), F32), jax.ShapeDtypeStruct((1, SSD_XBC), F32)],
        scratch_shapes=[pltpu.VMEM((ns, SSD_WIDTH), F32), pltpu.VMEM((q, SSD_XBC), F32), pltpu.VMEM((q, SSD_XBC), F32),
                        pltpu.VMEM((8, SSD_XBC), F32)],
        compiler_params=_params(("arbitrary",)),
    )(proj, proj, proj, proj, cw, cb, prow, d_x, nw, spread_p, spread_q, yraw, sall, dout)


def _me():
    return lax.axis_index("x"), lax.axis_index("y"), lax.axis_index("c")


_ANY = pl.BlockSpec(memory_space=pl.ANY)
_MESH = pl.DeviceIdType.MESH


_AG_SEMS = [pltpu.SemaphoreType.DMA((7,)), pltpu.SemaphoreType.DMA((7,)), pltpu.SemaphoreType.DMA(())]
_RS_SEMS = [pltpu.SemaphoreType.DMA((3,)), pltpu.SemaphoreType.DMA((3,)), pltpu.SemaphoreType.DMA(())]


def _ag_phases(src, dst, send_sems, recv_sems, local_sem):
    x, y, c = _me()
    me, sibling = (x, y, c), (x, y, 1 - c)
    chips = [(1 - x, y), (x, 1 - y), (1 - x, 1 - y)]

    def slot(px, py, pc):
        return dst.at[4 * px + 2 * py + pc]

    def copy(kk, blk, to, from_src=False):
        return pltpu.make_async_remote_copy(
            src_ref=src if from_src else slot(*blk), dst_ref=slot(*blk),
            send_sem=send_sems.at[kk], recv_sem=recv_sems.at[kk], device_id=to, device_id_type=_MESH)

    mine = lambda: pltpu.make_async_copy(src, slot(*me), local_sem)
    first = lambda: [copy(0, me, sibling, True)] + [copy(1 + j, me, (*chip, c), True) for j, chip in enumerate(chips)]
    passed = lambda j: copy(4 + j, (*chips[j], c), sibling)

    def start():
        mine().start()
        for cp in first():
            cp.start()

    def forward():
        for j, chip in enumerate(chips):
            copy(1 + j, (*chip, c), me).wait_recv()
            passed(j).start()

    def finish():
        copy(0, sibling, me).wait_recv()
        for j, chip in enumerate(chips):
            copy(4 + j, (*chip, 1 - c), me).wait_recv()
        for cp in first() + [passed(j) for j in range(3)]:
            cp.wait_send()
        mine().wait()

    return start, forward, finish


def _rs_chip_phases(src, dst, send_sems, recv_sems, local_sem, rows=None):
    x, y, c = _me()
    q_me = 2 * x + y
    pick = (lambda q: src.at[q]) if rows is None else (lambda q: src.at[q, rows])
    local = lambda: pltpu.make_async_copy(pick(q_me), dst.at[q_me], local_sem)
    copies = lambda: [pltpu.make_async_remote_copy(src_ref=pick(2 * px + py), dst_ref=dst.at[q_me], send_sem=send_sems.at[j],
                                                   recv_sem=recv_sems.at[j], device_id=(px, py, c), device_id_type=_MESH)
                      for j, (px, py) in enumerate([(1 - x, y), (x, 1 - y), (1 - x, 1 - y)])]

    def start():
        local().start()
        for cp in copies():
            cp.start()

    def finish():
        for cp in copies():
            cp.wait()
        local().wait()

    return start, finish


def all_gather(block, *, name):
    def body(src, dst, send_sems, recv_sems, local_sem):
        for phase in _ag_phases(src, dst, send_sems, recv_sems, local_sem):
            phase()

    return pl.pallas_call(
        body, name=name, in_specs=[_ANY], out_specs=_ANY,
        out_shape=jax.ShapeDtypeStruct((N_DEV,) + block.shape, block.dtype), scratch_shapes=list(_AG_SEMS),
    )(block)


RS_PIECES = 4


def rs_sibling_exchange(halves, *, name):
    _, nq, r, l = halves.shape
    rows = r // RS_PIECES
    assert r % RS_PIECES == 0 and rows % 16 == 0

    def body(src, dst, send_sems, recv_sems):
        x, y, c = _me()
        copies = []
        for q in range(nq):
            for i in range(RS_PIECES):
                kk = q * RS_PIECES + i
                cp = pltpu.make_async_remote_copy(
                    src_ref=src.at[1 - c, q, pl.ds(i * rows, rows)], dst_ref=dst.at[q, pl.ds(i * rows, rows)],
                    send_sem=send_sems.at[kk], recv_sem=recv_sems.at[kk], device_id=(x, y, 1 - c), device_id_type=_MESH)
                cp.start()
                copies.append(cp)
        for cp in copies:
            cp.wait()

    n_copies = nq * RS_PIECES
    return pl.pallas_call(
        body, name=name, in_specs=[_ANY], out_specs=_ANY,
        out_shape=jax.ShapeDtypeStruct((nq, r, l), halves.dtype),
        scratch_shapes=[pltpu.SemaphoreType.DMA((n_copies,)), pltpu.SemaphoreType.DMA((n_copies,))],
    )(halves)


def pair_sum_bf16(halves, theirs, *, name, tt=128):
    _, nq, r, wd = halves.shape
    tt = min(tt, r)
    parity = lax.axis_index("c").astype(jnp.int32).reshape(1)

    def body(c_ref, own_ref, sib_ref, o_ref):
        o_ref[...] = (own_ref[...] + sib_ref[...]).astype(BF16)

    return pl.pallas_call(
        body, name=name,
        grid_spec=pltpu.PrefetchScalarGridSpec(
            num_scalar_prefetch=1, grid=(nq, r // tt),
            in_specs=[pl.BlockSpec((None, None, tt, wd), lambda q, i, c: (c[0], q, i, 0)),
                      pl.BlockSpec((None, tt, wd), lambda q, i, c: (q, i, 0))],
            out_specs=pl.BlockSpec((None, tt, wd), lambda q, i, c: (q, i, 0))),
        out_shape=jax.ShapeDtypeStruct((nq, r, wd), BF16),
        compiler_params=_params(("parallel", "parallel")),
    )(parity, halves, theirs)


def rs_chip_exchange(part, *, name):
    def body(src, dst, send_sems, recv_sems, local_sem):
        for phase in _rs_chip_phases(src, dst, send_sems, recv_sems, local_sem):
            phase()

    return pl.pallas_call(
        body, name=name, in_specs=[_ANY], out_specs=_ANY,
        out_shape=jax.ShapeDtypeStruct(part.shape, part.dtype), scratch_shapes=list(_RS_SEMS),
    )(part)


def adamw(slabs, w, m, v, *, name, tt):
    ns, (r, wd) = slabs.shape[0], w.shape
    tt = min(tt, r)
    assert r % tt == 0

    def body(s_ref, w_ref, m_ref, v_ref, g_ref, d_ref, nm_ref, nv_ref):
        g = s_ref[0].astype(F32)
        for kdev in range(1, ns):
            g = g + s_ref[kdev].astype(F32)
        wv = w_ref[...]
        nm = ADAM_B1 * m_ref[...] + (1.0 - ADAM_B1) * g
        nv = ADAM_B2 * v_ref[...] + (1.0 - ADAM_B2) * (g * g)
        m_hat = nm / (1.0 - ADAM_B1 ** ADAM_STEP)
        v_hat = nv / (1.0 - ADAM_B2 ** ADAM_STEP)
        g_ref[...] = g
        d_ref[...] = -ADAM_LR * (m_hat / (jnp.sqrt(v_hat) + ADAM_EPS) + ADAM_WD * wv)
        nm_ref[...] = nm
        nv_ref[...] = nv

    spec = pl.BlockSpec((tt, wd), lambda i: (i, 0))
    return pl.pallas_call(
        body, name=name, grid=(r // tt,),
        in_specs=[pl.BlockSpec((ns, tt, wd), lambda i: (0, i, 0)), spec, spec, spec],
        out_specs=[spec] * 4, out_shape=[jax.ShapeDtypeStruct((r, wd), F32)] * 4,
        compiler_params=_params(("parallel",)),
    )(slabs, w, m, v)


WIDE = 1024
BIG = [("w_in", True, 289), ("w_out", False, 128), ("xa_wq", False, 128), ("xa_wk", False, 128), ("xa_wv", False, 128),
       ("xa_wo", False, 128), ("mlp_w2", False, 512), ("mlp_w1", True, 512)]
TINY = [("ssd_conv_w", 2), ("s5_glu_w", 1), ("rg_conv_w", 2)]
KEEP_F32 = ("ssd_conv_w", "rg_conv_w")
TINY_ROWS = 32
SHARDED = [name for name, _, _ in BIG] + [name for name, _ in TINY]
SMALL = ["ssd_conv_b", "ssd_dt_bias", "ssd_a_log", "ssd_d", "ssd_norm_w", "s5_lam_re", "s5_lam_im",
         "s5_log_step", "s5_b_re", "s5_b_im", "s5_c_re", "s5_c_im", "s5_d", "s5_glu_b", "rg_conv_b",
         "rg_wa", "rg_ba", "rg_wx", "rg_bx", "rg_lambda", "ln1_g", "ln1_b", "ln2_g", "ln2_b", "ln3_g", "ln3_b"]
WEIGHTS = ['w_in', 'w_out', 'ssd_conv_w', 'ssd_conv_b', 'ssd_dt_bias', 'ssd_a_log', 'ssd_d', 'ssd_norm_w',
           's5_lam_re', 's5_lam_im', 's5_log_step', 's5_b_re', 's5_b_im', 's5_c_re', 's5_c_im', 's5_d',
           's5_glu_w', 's5_glu_b', 'rg_conv_w', 'rg_conv_b', 'rg_wa', 'rg_ba', 'rg_wx', 'rg_bx', 'rg_lambda',
           'ln1_g', 'ln1_b', 'xa_wq', 'xa_wk', 'xa_wv', 'xa_wo', 'ln2_g', 'ln2_b', 'mlp_w1', 'mlp_w2',
           'ln3_g', 'ln3_b']


def _pad16(rows):
    return -(-rows // 16) * 16


def _pack_rows(flat, mult):
    n = flat.shape[-1]
    r = -(-n // (LANE * mult)) * mult
    pad = [(0, 0)] * (flat.ndim - 1) + [(0, r * LANE - n)]
    return jnp.pad(flat, pad).reshape(flat.shape[:-1] + (r, LANE))


def _unpack(packed, shapes):
    lead = packed.shape[:-2]
    flat = packed.reshape(lead + (-1,))
    out, off = [], 0
    for s in shapes:
        n = math.prod(s)
        out.append(flat[..., off:off + n].reshape(lead + tuple(s)))
        off += n
    return out


PACK_ROWS = 2048


def _tiny_block(flat):
    pad = [(0, 0)] * (flat.ndim - 1) + [(0, TINY_ROWS * WIDE - flat.shape[-1])]
    return jnp.pad(flat, pad).reshape(flat.shape[:-1] + (TINY_ROWS, WIDE))


def _pack_layer(big, tiny_flat=None):
    blocks, used = [], 0
    some = big[BIG[0][0]]

    def zeros(rows):
        return jnp.zeros(some.shape[:-2] + (rows, WIDE), some.dtype)

    for name, _, rows in BIG:
        blocks.append(jnp.pad(big[name], [(0, 0)] * (some.ndim - 2) + [(0, _pad16(rows) - rows), (0, 0)]))
        used += _pad16(rows)
    if tiny_flat is not None:
        blocks.append(_tiny_block(tiny_flat))
        used += TINY_ROWS
    return jnp.concatenate(blocks + [zeros(PACK_ROWS - used)], axis=-2)


def _unpack_layer(packed):
    big, off = {}, 0
    for name, _, rows in BIG:
        big[name] = packed[..., off:off + rows, :]
        off += _pad16(rows)
    return big, packed[..., off:off + TINY_ROWS, :].reshape(packed.shape[:-2] + (TINY_ROWS * WIDE,))


def _split_flat(flat, shapes):
    out, off = [], 0
    for s in shapes:
        n = math.prod(s)
        out.append(flat[..., off:off + n].reshape(flat.shape[:-1] + tuple(s)))
        off += n
    return out


def _to_full(gathered, axis):
    g = jnp.moveaxis(gathered, 0, axis)
    s = g.shape
    return g.reshape(s[:axis] + (s[axis] * s[axis + 1],) + s[axis + 2:])


def _to_slabs(full, axis):
    s = full.shape
    g = full.reshape(s[:axis] + (N_DEV, s[axis] // N_DEV) + s[axis + 1:])
    return jnp.moveaxis(g, axis, 0)


def _blockdiag(w):
    h, i, j = w.shape
    eye = jnp.eye(h, dtype=w.dtype)
    return (w[:, :, None, :] * eye[:, None, :, None]).reshape(h * i, h * j)


def _blockdiag_extract(m, h):
    i, j = m.shape[0] // h, m.shape[1] // h
    eye = jnp.eye(h, dtype=m.dtype)
    return (m.reshape(h, i, h, j) * eye[:, None, :, None]).sum(axis=2)


def _s5_disc(lr, li, ls, bre, bim):
    step = jnp.exp(ls)[:, None]
    er = jnp.exp(lr * step)
    ar, ai = er * jnp.cos(li * step), er * jnp.sin(li * step)
    nr, ni, den = ar - 1.0, ai, lr * lr + li * li
    qr, qi = (nr * lr + ni * li) / den, (ni * lr - nr * li) / den
    bbr = qr[..., None] * bre - qi[..., None] * bim
    bbi = qr[..., None] * bim + qi[..., None] * bre
    return ar, ai, bbr, bbi


def _row(v, width=None):
    v = v.reshape(1, -1)
    if width is not None and v.shape[1] < width:
        v = jnp.pad(v, ((0, 0), (0, width - v.shape[1])))
    return v


def _relu2(a):
    r = jnp.maximum(a, 0.0)
    return r * r


def _add_alpha(acc, d):
    return acc + ALPHA * d


def _layer_params(full, small, l):
    p = {}
    w_in = full["w_in"]
    z, xbc, dt, u, xr, g = w_in[0:512], w_in[512:1536], w_in[1536:1544], w_in[1544:1800], w_in[1800:2056], w_in[2056:2312]
    p["w_inp"] = jnp.concatenate([xbc, z, u, xr, g, dt, jnp.zeros((D_INP - P_DT - 8, D_MODEL), w_in.dtype)], axis=0)
    for k_ in ("w_out", "xa_wq", "xa_wk", "xa_wv", "xa_wo", "mlp_w1", "mlp_w2"):
        p[k_] = full[k_]
    p["s5_glu_w"] = full["s5_glu_w"][l]
    p["ssd_cw"], p["ssd_cb"] = full["ssd_conv_w"][l], _row(small["ssd_conv_b"][l])
    dtb, alog, dsk = small["ssd_dt_bias"][l], small["ssd_a_log"][l], small["ssd_d"][l]
    p["prow"] = jnp.concatenate([_row(dtb, LANE), _row(alog, LANE), jnp.zeros((6, LANE), F32)], axis=0)
    p["ssd_dx"] = _row(jnp.repeat(dsk, SSD_HEAD_DIM))
    p["ssd_nw"] = _row(small["ssd_norm_w"][l])
    s5_in = (small["s5_lam_re"][l], small["s5_lam_im"][l], small["s5_log_step"][l], small["s5_b_re"][l], small["s5_b_im"][l])
    (ar, ai, bbr, bbi), p["s5_vjp"] = jax.vjp(_s5_disc, *s5_in)
    p["lam_fwd"] = jnp.concatenate([_row(ar), _row(ai)], axis=1)
    p["lam_adj"] = jnp.concatenate([_row(ar), _row(-ai)], axis=1)
    p["bcat"] = jnp.concatenate([_blockdiag(jnp.swapaxes(bbr, 1, 2)), _blockdiag(jnp.swapaxes(bbi, 1, 2))], axis=1)
    p["ccat"] = jnp.concatenate([_blockdiag(jnp.swapaxes(small["s5_c_re"][l], 1, 2)),
                                 -_blockdiag(jnp.swapaxes(small["s5_c_im"][l], 1, 2))], axis=0)
    p["s5_d"], p["s5_glu_b"] = _row(small["s5_d"][l]), _row(small["s5_glu_b"][l])
    p["rg_cw"], p["rg_cb"] = full["rg_conv_w"][l], _row(small["rg_conv_b"][l])
    p["rg_wa"], p["rg_wx"] = _blockdiag(small["rg_wa"][l]), _blockdiag(small["rg_wx"][l])
    p["rg_ba"], p["rg_bx"], p["rg_lam"] = _row(small["rg_ba"][l]), _row(small["rg_bx"][l]), _row(small["rg_lambda"][l])
    for i in (1, 2, 3):
        p[f"g{i}"], p[f"b{i}"] = _row(small[f"ln{i}_g"][l]), _row(small[f"ln{i}_b"][l])
    return p


def _take_side(res, n_out, got):
    res = res if isinstance(res, (list, tuple)) else (res,)
    got.extend(res[n_out:])
    return res[0] if n_out == 1 else res[:n_out]


def _layer_fwd(h0, mem, p, sides={}):
    t = h0.shape[0]
    s = {"h0": h0}
    got = []
    proj = _take_side(mm(h0, p["w_inp"], tb=True, name="in_proj", side=sides.get("in_proj")), 1, got)
    y_ssd, yraw, sall = ssd_fwd(proj, p["ssd_cw"], p["ssd_cb"], p["prow"], p["ssd_dx"], p["ssd_nw"], name="ssd_fwd")
    hs5, ylin, y_s5 = s5_fwd(proj, p["bcat"], p["lam_fwd"], p["ccat"], p["s5_d"], p["s5_glu_w"], p["s5_glu_b"], name="s5_fwd")
    rg_prm = (p["rg_cw"], p["rg_cb"], p["rg_wa"], p["rg_wx"], p["rg_ba"], p["rg_bx"], p["rg_lam"])
    y_rg, xc, a_rg, h_rg = rg_fwd(proj, *rg_prm, name="rg_fwd")
    ycat = jnp.concatenate([y_ssd, y_s5, y_rg], axis=1)
    pre1, h1 = mm_ln(ycat, p["w_out"], h0, p["g1"], p["b1"], name="out_proj")
    q = mm(h1, p["xa_wq"], name="xa_q", out_dtype=BF16)
    k = mm(mem, p["xa_wk"], name="xa_kv")
    v = mm(mem, p["xa_wv"], name="xa_kv")
    (o,), _ = rowk(_attn_fwd_fn, [(q, D_MODEL, 0)], [k, v], [D_MODEL], [], rows=t, name="xa_fwd", out_dtypes=[BF16])
    pre2, h2 = mm_ln(o, p["xa_wo"], h1, p["g2"], p["b2"], name="xa_o")
    a_mlp = _take_side(mm(h2, p["mlp_w1"], tb=True, name="mlp_up", side=sides.get("mlp_up")), 1, got)
    pre3, h3 = _take_side(mm_ln(a_mlp, p["mlp_w2"], h2, p["g3"], p["b3"], fa=_relu2, name="mlp_down",
                                side=sides.get("mlp_down")), 2, got)
    s.update(proj=proj, yraw=yraw, sall=sall, hs5=hs5, ylin=ylin, xc=xc, a_rg=a_rg, h_rg=h_rg,
             ycat=ycat, pre1=pre1, h1=h1, q=q, k=k, v=v, o=o, pre2=pre2, h2=h2, a_mlp=a_mlp, pre3=pre3)
    return h3, s, got


def _layer_bwd(dh3, mem, p, s, l, gfull, gsmall, sides={}):
    t = dh3.shape[0]
    proj = s["proj"]
    dpre3, dg3, db3 = ln_bwd(s["pre3"], dh3, p["g3"], name="ln_bwd")
    got = []
    da = _take_side(mm(dpre3, p["mlp_w2"], tb=True, o_extra=(s["a_mlp"],), fo=lambda acc, a: acc * 2.0 * jnp.maximum(a, 0.0),
                       name="mlp_da", out_dtype=BF16, side=sides.get("mlp_da")), 1, got)
    gfull["mlp_w2"][l] = mm(s["a_mlp"], dpre3, ta=True, fa=_relu2, name="mlp_dw2")
    gfull["mlp_w1"][l] = mm(da, s["h2"], ta=True, name="mlp_dw1")
    dh2 = _take_side(mm(da, p["mlp_w1"], o_extra=(dpre3,), fo=_add_alpha, name="mlp_dx", side=sides.get("mlp_dx")), 1, got)
    dpre2, dg2, db2 = ln_bwd(s["pre2"], dh2, p["g2"], name="ln_bwd")
    do = _take_side(mm(dpre2, p["xa_wo"], tb=True, name="xa_do", out_dtype=BF16, side=sides.get("xa_do")), 1, got)
    gfull["xa_wo"][l] = mm(s["o"], dpre2, ta=True, name="dw_sq")
    (dq,), (dk, dv) = rowk(_attn_bwd_fn, [(s["q"], D_MODEL, 0), (do, D_MODEL, 0)], [s["k"], s["v"]], [D_MODEL],
                           [(256, D_MODEL), (256, D_MODEL)], rows=t, name="xa_bwd", out_dtypes=[BF16])
    gfull["xa_wq"][l] = mm(s["h1"], dq, ta=True, name="dw_sq")
    gfull["xa_wk"][l] = mm(mem, dk, ta=True, name="dw_kv")
    gfull["xa_wv"][l] = mm(mem, dv, ta=True, name="dw_kv")
    dh1 = mm(dq, p["xa_wq"], tb=True, o_extra=(dpre2,), fo=_add_alpha, name="dx_sq")
    dpre1, dg1, db1 = ln_bwd(s["pre1"], dh1, p["g1"], name="ln_bwd")
    dycat = mm(dpre1, p["w_out"], tb=True, name="xa_do")
    gfull["w_out"][l] = mm(s["ycat"], dpre1, ta=True, name="dw_sq")
    rg_prm = (p["rg_cw"], p["rg_cb"], p["rg_wa"], p["rg_wx"], p["rg_ba"], p["rg_bx"], p["rg_lam"])
    dxr, dg_rg, d_rgcw, d_rgcb, dwa, dwx, dba, dbx, dlam = rg_bwd(proj, dycat, s["xc"], s["a_rg"], s["h_rg"], *rg_prm, name="rg_bwd")
    du, dccat, dbcat, dar, dai, d_s5d, d_gluw, d_glub = s5_bwd(dycat, s["ylin"], s["hs5"], proj, p["bcat"], p["lam_adj"], p["ccat"],
                                                               p["s5_d"], p["s5_glu_w"], p["s5_glu_b"], name="s5_bwd")
    dxbc, dz, ddt, dprm, ddx, dnw, d_scw, d_scb = ssd_bwd(proj, p["ssd_cw"], p["ssd_cb"], p["prow"], p["ssd_dx"], p["ssd_nw"],
                                                         s["yraw"], s["sall"], dycat, name="ssd_bwd")
    dproj = jnp.concatenate([dxbc, dz, du, dxr, dg_rg, ddt, jnp.zeros((t, D_INP - P_DT - LANE), BF16)], axis=1)
    dh0 = mm(dproj, p["w_inp"], o_extra=(dpre1,), fo=_add_alpha, name="in_proj_dx")
    dwp = mm(dproj, s["h0"], ta=True, name="in_proj_dw")
    gfull["w_in"][l] = jnp.concatenate([dwp[P_Z:P_Z + 512], dwp[P_XBC:P_XBC + 1024], dwp[P_DT:P_DT + 8],
                                        dwp[P_U:P_U + 256], dwp[P_XR:P_XR + 256], dwp[P_G:P_G + 256]], axis=0)
    gfull["ssd_conv_w"][l], gfull["rg_conv_w"][l], gfull["s5_glu_w"][l] = d_scw, d_rgcw, d_gluw
    ng, ns = S5_GROUPS, S5_STATE
    dbbr = jnp.swapaxes(_blockdiag_extract(dbcat[:, :S5_NSTATE], ng), 1, 2)
    dbbi = jnp.swapaxes(_blockdiag_extract(dbcat[:, S5_NSTATE:], ng), 1, 2)
    d_lr, d_li, d_ls, d_bre, d_bim = p["s5_vjp"]((dar.reshape(ng, ns), dai.reshape(ng, ns), dbbr, dbbi))
    gsmall["s5_lam_re"][l], gsmall["s5_lam_im"][l], gsmall["s5_log_step"][l] = d_lr, d_li, d_ls
    gsmall["s5_b_re"][l], gsmall["s5_b_im"][l] = d_bre, d_bim
    gsmall["s5_c_re"][l] = jnp.swapaxes(_blockdiag_extract(dccat[:S5_NSTATE], ng), 1, 2)
    gsmall["s5_c_im"][l] = -jnp.swapaxes(_blockdiag_extract(dccat[S5_NSTATE:], ng), 1, 2)
    gsmall["s5_d"][l], gsmall["s5_glu_b"][l] = d_s5d[0], d_glub[0]
    gsmall["ssd_conv_b"][l], gsmall["rg_conv_b"][l] = d_scb[0], d_rgcb[0]
    gsmall["ssd_dt_bias"][l], gsmall["ssd_a_log"][l] = dprm[0, :8], dprm[1, :8]
    gsmall["ssd_d"][l] = ddx.reshape(SSD_HEADS, SSD_HEAD_DIM).sum(axis=1)
    gsmall["ssd_norm_w"][l] = dnw[0]
    gsmall["rg_wa"][l], gsmall["rg_wx"][l] = _blockdiag_extract(dwa, RG_BLOCKS), _blockdiag_extract(dwx, RG_BLOCKS)
    gsmall["rg_ba"][l], gsmall["rg_bx"][l] = dba.reshape(RG_BLOCKS, RG_BLOCK_DIM), dbx.reshape(RG_BLOCKS, RG_BLOCK_DIM)
    gsmall["rg_lambda"][l] = dlam[0]
    for i, (dg, db) in zip((1, 2, 3), ((dg1, db1), (dg2, db2), (dg3, db3))):
        gsmall[f"ln{i}_g"][l], gsmall[f"ln{i}_b"][l] = dg[0], db[0]
    return dh0, got


def _step(a):
    h = a["x"][0]
    mem = a["mem"][0]
    t = h.shape[0]
    r4, r3 = PACK_ROWS // 4, 3 * PACK_ROWS // 8

    def my_shards(pre):
        return ({name: (jnp.swapaxes(a[pre + name], 1, 2) if tr else a[pre + name]) for name, tr, _ in BIG},
                [a[pre + name] for name, _ in TINY])

    def my_pack(pre, l):
        big, tiny = my_shards(pre)
        return _pack_layer({name: w[l] for name, w in big.items()},
                           jnp.concatenate([w.reshape(-1) for w in tiny]) if l == 0 else None)

    big, tiny = my_shards("")
    tiny16 = [(lax.bitcast_convert_type(w, BF16) if name in KEEP_F32 else w.astype(BF16)).reshape(-1)
              for (name, _), w in zip(TINY, tiny)]
    packed = [_pack_layer({name: w[l].astype(BF16) for name, w in big.items()}, jnp.concatenate(tiny16) if l == 0 else None)
              for l in range(DEPTH)]
    small = {name: a[name] for name in SMALL}

    def gathered_weights(g):
        gbig, gtiny = _unpack_layer(g)
        return {name: w.reshape(-1, WIDE) for name, w in gbig.items()}, gtiny

    full, gtiny = gathered_weights(all_gather(packed[0], name="ag_weights"))
    tiny_shapes = [w.shape + ((2,) if name in KEEP_F32 else ()) for (name, _), w in zip(TINY, tiny)]
    tiny_full = {name: _to_full(lax.bitcast_convert_type(g, F32) if name in KEEP_F32 else g, axis)
                 for (name, axis), g in zip(TINY, _split_flat(gtiny, tiny_shapes))}
    p0 = _layer_params({**full, **tiny_full}, small, 0)
    h, s0, got = _layer_fwd(h, mem, p0, sides={"in_proj": ("gather", packed[1], 0, r4), "mlp_up": ("gather", packed[1], r4, r3),
                                                "mlp_down": ("gather", packed[1], r4 + r3, r3)})
    full, _ = gathered_weights(jnp.concatenate(got, axis=1))
    p1 = _layer_params({**full, **tiny_full}, small, 1)
    h, s1, _ = _layer_fwd(h, mem, p1)
    (dh,), (loss_part,) = rowk(_loss_fn, [(h, D_MODEL, 0), (a["loss_target"][0], D_MODEL, 0)], [], [D_MODEL], [(1, 1)],
                               rows=t, name="loss_head")
    loss = lax.psum(loss_part[0, 0], ("x", "y", "c"))
    gfull = {name: [None] * DEPTH for name in SHARDED}
    gsmall = {name: [None] * DEPTH for name in SMALL}

    def chip_partials(l):
        gbig = {name: gfull[name][l].reshape(N_DEV, rows, WIDE) for name, _, rows in BIG}
        gtiny = None
        if l == 0:
            gtiny = jnp.concatenate([_to_slabs(jnp.stack(gfull[name]), axis).reshape(N_DEV, -1) for name, axis in TINY], axis=1)
        slabs = _pack_layer(gbig, gtiny)
        halves = jnp.swapaxes(slabs.reshape((4, 2) + slabs.shape[1:]), 0, 1)
        theirs = rs_sibling_exchange(halves, name="rs_sibling")
        return pair_sum_bf16(halves, theirs, name="rs_pair_sum")

    dh, _ = _layer_bwd(dh, mem, p1, s1, 1, gfull, gsmall)
    part1 = chip_partials(1)
    dh, got = _layer_bwd(dh, mem, p0, s0, 0, gfull, gsmall, sides={"mlp_da": ("chips", part1, 0, r3), "mlp_dx": ("chips", part1, r3, r3),
                                                                    "xa_do": ("chips", part1, 2 * r3, r4)})
    grad_x = dh[None]
    landed = [rs_chip_exchange(chip_partials(0), name="rs_chips"), jnp.concatenate(got, axis=1)]
    bigs = [adamw(landed[l], my_pack("", l), my_pack("m_", l), my_pack("v_", l), name="adamw_sharded", tt=128) for l in range(DEPTH)]
    gs = _pack_rows(jnp.concatenate([jnp.stack(gsmall[name]).reshape(-1) for name in SMALL]), 8)
    gs = all_gather(gs, name="ag_small_grads")
    pks = lambda pre: _pack_rows(jnp.concatenate([a[pre + name].reshape(-1) for name in SMALL]), 8)
    sm = adamw(gs, pks(""), pks("m_"), pks("v_"), name="adamw_replicated", tt=gs.shape[1])
    out = {}
    for i, kind in enumerate(("grad_", "delta_", "new_m_", "new_v_")):
        layers = [_unpack_layer(bigs[l][i]) for l in range(DEPTH)]
        for name, tr, _ in BIG:
            arr = jnp.stack([layers[l][0][name] for l in range(DEPTH)])
            out[kind + name] = jnp.swapaxes(arr, 1, 2) if tr else arr
        for (name, _), arr in zip(TINY, _split_flat(layers[0][1], [w.shape for w in tiny])):
            out[kind + name] = arr
        for name, arr in zip(SMALL, _unpack(sm[i], [a[name].shape for name in SMALL])):
            out[kind + name] = arr
    return (loss, grad_x) + tuple(out[kind + name] for kind in ("grad_", "delta_", "new_m_", "new_v_") for name in WEIGHTS)


def kernel(x, mem, w_in, w_out, ssd_conv_w, ssd_conv_b, ssd_dt_bias, ssd_a_log, ssd_d, ssd_norm_w, s5_lam_re, s5_lam_im, s5_log_step, s5_b_re, s5_b_im, s5_c_re, s5_c_im, s5_d, s5_glu_w, s5_glu_b, rg_conv_w, rg_conv_b, rg_wa, rg_ba, rg_wx, rg_bx, rg_lambda, ln1_g, ln1_b, xa_wq, xa_wk, xa_wv, xa_wo, ln2_g, ln2_b, mlp_w1, mlp_w2, ln3_g, ln3_b, loss_target, m_w_in, m_w_out, m_ssd_conv_w, m_ssd_conv_b, m_ssd_dt_bias, m_ssd_a_log, m_ssd_d, m_ssd_norm_w, m_s5_lam_re, m_s5_lam_im, m_s5_log_step, m_s5_b_re, m_s5_b_im, m_s5_c_re, m_s5_c_im, m_s5_d, m_s5_glu_w, m_s5_glu_b, m_rg_conv_w, m_rg_conv_b, m_rg_wa, m_rg_ba, m_rg_wx, m_rg_bx, m_rg_lambda, m_ln1_g, m_ln1_b, m_xa_wq, m_xa_wk, m_xa_wv, m_xa_wo, m_ln2_g, m_ln2_b, m_mlp_w1, m_mlp_w2, m_ln3_g, m_ln3_b, v_w_in, v_w_out, v_ssd_conv_w, v_ssd_conv_b, v_ssd_dt_bias, v_ssd_a_log, v_ssd_d, v_ssd_norm_w, v_s5_lam_re, v_s5_lam_im, v_s5_log_step, v_s5_b_re, v_s5_b_im, v_s5_c_re, v_s5_c_im, v_s5_d, v_s5_glu_w, v_s5_glu_b, v_rg_conv_w, v_rg_conv_b, v_rg_wa, v_rg_ba, v_rg_wx, v_rg_bx, v_rg_lambda, v_ln1_g, v_ln1_b, v_xa_wq, v_xa_wk, v_xa_wv, v_xa_wo, v_ln2_g, v_ln2_b, v_mlp_w1, v_mlp_w2, v_ln3_g, v_ln3_b):
    return _step(dict(locals()))
```

```python
import math

import jax
import jax.numpy as jnp
from jax import lax
from jax.experimental import pallas as pl
from jax.experimental.pallas import tpu as pltpu

F32 = jnp.float32
BF16 = jnp.bfloat16

N_DEV = 8
D_MODEL = 1024
DEPTH = 2
SSD_WIDTH = 512
SSD_HEADS = 8
SSD_HEAD_DIM = 64
SSD_STATE = 128
SSD_CHUNK = 128
SSD_XBC = 1024
S5_WIDTH = 256
S5_GROUPS = 16
S5_STATE = 64
S5_NSTATE = S5_GROUPS * S5_STATE
RG_WIDTH = 256
RG_BLOCKS = 4
RG_BLOCK_DIM = 64
RG_C = 8.0
XA_HEADS = 4
XA_HEAD_DIM = 256
ALPHA = (2.0 * DEPTH) ** 0.25
LN_EPS = 1e-5
ADAM_LR, ADAM_B1, ADAM_B2, ADAM_EPS, ADAM_WD, ADAM_STEP = 0.001, 0.9, 0.999, 1e-08, 0.01, 10

P_XBC, P_Z, P_U, P_XR, P_G, P_DT = 0, 1024, 1536, 1792, 2048, 2304
D_INP = 2560
LANE = 128
VMEM_LIMIT = 56 * 1024 * 1024
ROW_TILE = 512

_NN = ((1,), (0,))
_NT = ((1,), (1,))
_TN = ((0,), (0,))


def _dot(a, b, dims=_NN):
    return lax.dot_general(a.astype(BF16), b.astype(BF16), (dims, ((), ())), preferred_element_type=F32)


def _split_bf16(x, parts):
    out, rem = [], x
    for _ in range(parts):
        piece = rem.astype(BF16)
        out.append(piece)
        rem = rem - piece.astype(F32)
    return out


def _dot_mask(a, b, dims=_NN, *, mask_left, parts):
    if mask_left:
        return sum(_dot(a, piece, dims) for piece in _split_bf16(b, parts))
    return sum(_dot(piece, b, dims) for piece in _split_bf16(a, parts))


def _sigmoid(x):
    return 1.0 / (1.0 + jnp.exp(-x))


def _silu(x):
    return x * _sigmoid(x)


def _dsilu(x):
    s = _sigmoid(x)
    return s * (1.0 + x * (1.0 - s))


_GK = math.sqrt(2.0 / math.pi)
_GC = 0.044715


def _gelu(x):
    return 0.5 * x * (1.0 + jnp.tanh(_GK * (x + _GC * x * x * x)))


def _dgelu(x):
    th = jnp.tanh(_GK * (x + _GC * x * x * x))
    return 0.5 * (1.0 + th) + 0.5 * x * (1.0 - th * th) * _GK * (1.0 + 3.0 * _GC * x * x)


def _log1p_pos(e):
    return jnp.where(e < 1e-2, e * (1.0 - e * (0.5 - e * (1.0 / 3.0))), jnp.log(1.0 + e))


def _softplus(x):
    return jnp.maximum(x, 0.0) + _log1p_pos(jnp.exp(-jnp.abs(x)))


def _neg_expm1(x):
    poly = -x * (1.0 + x * (0.5 + x * (1.0 / 6.0 + x * (1.0 / 24.0 + x * (1.0 / 120.0)))))
    return jnp.where(x > -0.05, poly, 1.0 - jnp.exp(x))


def _params(sem):
    return pltpu.CompilerParams(dimension_semantics=sem, vmem_limit_bytes=VMEM_LIMIT)


RESIDENT_BYTES = 8 * 1024 * 1024
STREAM_BYTES = 4 * 1024 * 1024


def _halve_to_fit(dims, bytes_per, limit):
    dims = list(dims)
    while math.prod(dims) * bytes_per > limit:
        i = max(range(len(dims)), key=lambda d: dims[d])
        assert dims[i] % 256 == 0, dims
        dims[i] //= 2
    return dims


def _side_exchange(side, src, dst, sems, step, nsteps):
    kind, _, r0, rows = side
    span = pl.ds(r0, rows)
    if kind == "gather":
        phases = lambda: _ag_phases(src.at[span], dst, *sems)
        when = (0, (3 * nsteps) // 4, nsteps - 1)
    else:
        phases = lambda: _rs_chip_phases(src, dst, *sems, rows=span)
        when = (0, nsteps - 1)
    for idx, at in enumerate(when):
        pl.when(step == at)(lambda idx=idx: phases()[idx]())


def mm(a, b, *, name, ta=False, tb=False, a_extra=(), fa=None, o_extra=(), r_extra=(), fo=None, n_out=1,
       a_off=0, m=None, k=None, out_dtype=F32, side=None):
    n = b.shape[0] if tb else b.shape[1]
    na, no, nr = 1 + len(a_extra), len(o_extra), len(r_extra)
    if not ta:
        assert m is None
        m, kdim = a.shape[0], (a.shape[1] if k is None else k)
        assert a_off % kdim == 0
        (tn,) = _halve_to_fit([n], kdim * b.dtype.itemsize, RESIDENT_BYTES)
        (tm,) = _halve_to_fit([min(512, m)], max(tn, kdim) * 4, STREAM_BYTES)
        a_spec = pl.BlockSpec((tm, kdim), lambda i, j: (i, a_off // kdim))
        b_spec = pl.BlockSpec((tn, kdim), lambda i, j: (j, 0)) if tb else pl.BlockSpec((kdim, tn), lambda i, j: (0, j))
        o_spec = pl.BlockSpec((tm, tn), lambda i, j: (i, j))
        dims = _NT if tb else _NN

        r_spec = pl.BlockSpec((1, tn), lambda i, j: (0, j))

        grid = (m // tm, n // tn)
        nin = na + 1 + no + nr

        def body(*refs):
            a_refs, b_ref, e_refs, out_refs = refs[:na], refs[na], refs[na + 1:nin], refs[nin + (side is not None):nin + (side is not None) + n_out]
            if side is not None:
                _side_exchange(side, refs[nin], refs[nin + 1 + n_out], refs[nin + 2 + n_out:],
                               pl.program_id(0) * grid[1] + pl.program_id(1), grid[0] * grid[1])
            av = a_refs[0][...] if fa is None else fa(*[r[...] for r in a_refs])
            acc = _dot(av, b_ref[...], dims)
            res = acc if fo is None else fo(acc, *[r[...] for r in e_refs])
            for r, v in zip(out_refs, res if n_out > 1 else (res,)):
                r[...] = v.astype(r.dtype)

        sem = ("parallel", "parallel") if side is None else ("arbitrary", "arbitrary")
    else:
        assert k is None and not tb and fo is None and not o_extra and not r_extra and n_out == 1 and out_dtype == F32
        assert side is None
        kdim, m = a.shape[0], (a.shape[1] if m is None else m)
        r_spec = None
        tm, tn = _halve_to_fit([m, n], 4, RESIDENT_BYTES)
        (tk,) = _halve_to_fit([min(512, kdim)], max(tm, tn) * 4, STREAM_BYTES)
        assert a_off % tm == 0
        a_spec = pl.BlockSpec((tk, tm), lambda i, j, kk: (kk, i + a_off // tm))
        b_spec = pl.BlockSpec((tk, tn), lambda i, j, kk: (kk, j))
        o_spec = pl.BlockSpec((tm, tn), lambda i, j, kk: (i, j))

        def body(*refs):
            a_refs, b_ref, out_ref = refs[:na], refs[na], refs[na + 1]

            @pl.when(pl.program_id(2) == 0)
            def _():
                out_ref[...] = jnp.zeros_like(out_ref)

            av = a_refs[0][...] if fa is None else fa(*[r[...] for r in a_refs])
            out_ref[...] += _dot(av, b_ref[...], _TN)

        grid, sem = (m // tm, n // tn, kdim // tk), ("parallel", "parallel", "arbitrary")
    assert m % tm == 0 and n % tn == 0, (name, m, n, tm, tn)
    out = jax.ShapeDtypeStruct((m, n), out_dtype)
    if side is None:
        return pl.pallas_call(
            body, name=name, grid=grid,
            in_specs=[a_spec] * na + [b_spec] + [o_spec] * no + [r_spec] * nr,
            out_specs=o_spec if n_out == 1 else [o_spec] * n_out, out_shape=out if n_out == 1 else [out] * n_out,
            compiler_params=_params(sem),
        )(a, *a_extra, b, *o_extra, *r_extra)
    kind, arr, _, rows = side
    landed = jax.ShapeDtypeStruct(((N_DEV, rows) if kind == "gather" else (4, rows)) + arr.shape[-1:], arr.dtype)
    return pl.pallas_call(
        body, name=name, grid=grid,
        in_specs=[a_spec] * na + [b_spec] + [o_spec] * no + [r_spec] * nr + [_ANY],
        out_specs=[o_spec] * n_out + [_ANY], out_shape=[out] * n_out + [landed],
        scratch_shapes=list(_AG_SEMS if kind == "gather" else _RS_SEMS),
        compiler_params=_params(sem),
    )(a, *a_extra, b, *o_extra, *r_extra, arr)


def rowk(fn, tiled, full, out_w, acc_shapes, *, rows, name, out_dtypes=None):
    tt = min(ROW_TILE, rows)
    n = rows // tt
    assert rows % tt == 0
    nt, nf, no = len(tiled), len(full), len(out_w)

    def tspec(w, cb):
        return pl.BlockSpec((tt, w), lambda i: (i, cb))

    def fspec(a):
        nd = a.ndim
        return pl.BlockSpec(a.shape, lambda i: (0,) * nd)

    def body(*refs):
        ins, fulls = refs[:nt], refs[nt:nt + nf]
        outs, accs = refs[nt + nf:nt + nf + no], refs[nt + nf + no:]
        res_t, res_a = fn(*[r[...] for r in ins], *[r[...] for r in fulls])
        for r, v in zip(outs, res_t):
            r[...] = v.astype(r.dtype)
        if accs:
            @pl.when(pl.program_id(0) == 0)
            def _():
                for r in accs:
                    r[...] = jnp.zeros_like(r)
            for r, v in zip(accs, res_a):
                r[...] += v

    outs = pl.pallas_call(
        body, name=name, grid=(n,),
        in_specs=[tspec(w, cb) for (_, w, cb) in tiled] + [fspec(a) for a in full],
        out_specs=[tspec(w, 0) for w in out_w] + [pl.BlockSpec(s, lambda i, nd=len(s): (0,) * nd) for s in acc_shapes],
        out_shape=[jax.ShapeDtypeStruct((rows, w), dt) for w, dt in zip(out_w, out_dtypes or [F32] * no)]
        + [jax.ShapeDtypeStruct(s, F32) for s in acc_shapes],
        compiler_params=_params(("arbitrary",)),
    )(*[a for (a, _, _) in tiled], *full)
    return outs[:no], outs[no:]


def _colsum(x):
    return jnp.sum(x, axis=0, keepdims=True)


def _rowsum(x):
    return jnp.sum(x, axis=1, keepdims=True)


def _ln_epilogue(acc, resid, g, b):
    pre = ALPHA * resid + acc
    mu = jnp.mean(pre, axis=1, keepdims=True)
    xc = pre - mu
    var = jnp.mean(xc * xc, axis=1, keepdims=True)
    return pre, xc * lax.rsqrt(var + LN_EPS) * g + b


def _ln_bwd_fn(pre, dout, g):
    mu = jnp.mean(pre, axis=1, keepdims=True)
    xc = pre - mu
    var = jnp.mean(xc * xc, axis=1, keepdims=True)
    rstd = lax.rsqrt(var + LN_EPS)
    xhat = xc * rstd
    dxh = dout * g
    dpre = rstd * (dxh - jnp.mean(dxh, axis=1, keepdims=True) - xhat * jnp.mean(dxh * xhat, axis=1, keepdims=True))
    return (dpre,), (_colsum(dout * xhat), _colsum(dout))


def mm_ln(a, w, resid, g, b, *, name, fa=None, side=None):
    assert w.shape[1] == D_MODEL
    return mm(a, w, fa=fa, o_extra=(resid,), r_extra=(g, b), fo=_ln_epilogue, n_out=2, name=name, side=side)


def ln_bwd(pre, dout, g, *, name):
    (dpre,), (dg, db) = rowk(_ln_bwd_fn, [(pre, D_MODEL, 0), (dout, D_MODEL, 0)], [g],
                             [D_MODEL], [(1, D_MODEL), (1, D_MODEL)], rows=pre.shape[0], name=name)
    return dpre, dg, db


def _loss_fn(y, tgt):
    e = y - tgt
    part = _colsum(_rowsum(e * e)) * (0.5 / D_MODEL)
    return (e * (1.0 / D_MODEL),), (part,)


_XA_SCALE = 1.0 / math.sqrt(XA_HEAD_DIM)


def _attn_probs(qh, kh):
    s = _dot(qh, kh, _NT) * _XA_SCALE
    e = jnp.exp(s - jnp.max(s, axis=1, keepdims=True))
    return e / _rowsum(e)


def _attn_fwd_fn(q, k, v):
    outs = []
    for hd in range(XA_HEADS):
        sl = slice(hd * XA_HEAD_DIM, (hd + 1) * XA_HEAD_DIM)
        outs.append(_dot(_attn_probs(q[:, sl], k[:, sl]), v[:, sl]))
    return (jnp.concatenate(outs, axis=1),), ()


def _attn_bwd_fn(q, do, k, v):
    dqs, dks, dvs = [], [], []
    for hd in range(XA_HEADS):
        sl = slice(hd * XA_HEAD_DIM, (hd + 1) * XA_HEAD_DIM)
        qh, kh, vh, doh = q[:, sl], k[:, sl], v[:, sl], do[:, sl]
        p = _attn_probs(qh, kh)
        dp = _dot(doh, vh, _NT)
        ds = p * (dp - _rowsum(p * dp)) * _XA_SCALE
        dqs.append(_dot(ds, kh))
        dks.append(_dot(ds, qh, _TN))
        dvs.append(_dot(p, doh, _TN))
    cat = lambda xs: jnp.concatenate(xs, axis=1)
    return (cat(dqs),), (cat(dks), cat(dvs))


def _s5_post_fwd_fn(ylin, u, dskip, gw, gb):
    yg = _gelu(ylin + dskip * u)
    return (yg * _sigmoid(_dot(yg, gw) + gb),), ()


def _s5_post_bwd_fn(ylin, u, dout, dskip, gw, gb):
    pre = ylin + dskip * u
    yg = _gelu(pre)
    sg = _sigmoid(_dot(yg, gw) + gb)
    dlin = dout * yg * sg * (1.0 - sg)
    dyg = dout * sg + _dot(dlin, gw, _NT)
    dpre = dyg * _dgelu(pre)
    return (dpre, dpre * dskip), (_colsum(dpre * u), _dot(yg, dlin, _TN), _colsum(dlin))


def _rg_gates(xc, wa, wx, ba, bx, lam):
    r = _sigmoid(_dot(xc, wa) + ba)
    i = _sigmoid(_dot(xc, wx) + bx)
    sp = _softplus(-lam)
    log_a = -RG_C * r * sp
    a = jnp.exp(log_a)
    mult = jnp.sqrt(_neg_expm1(2.0 * log_a))
    return r, i, sp, a, mult


def _rg_pre_bwd_fn(xc, gsc, hprev, wa, wx, ba, bx, lam):
    r, i, sp, a, mult = _rg_gates(xc, wa, wx, ba, bx, lam)
    da = gsc * hprev
    db = gsc
    dmult = db * i * xc
    di = db * mult * xc
    dxc = db * mult * i
    dlog_a = da * a - a * a * dmult / mult
    dr = dlog_a * (-RG_C * sp)
    dsp = _colsum(dlog_a * (-RG_C * r))
    dlam = dsp * (-_sigmoid(-lam))
    dpr = dr * r * (1.0 - r)
    dpi = di * i * (1.0 - i)
    dxc = dxc + _dot(dpr, wa, _NT) + _dot(dpi, wx, _NT)
    return (dxc,), (_dot(xc, dpr, _TN), _dot(xc, dpi, _TN), _colsum(dpr), _colsum(dpi), dlam)


def _conv_taps(x_ref, halo_ref, first):
    x = x_ref[...]
    halo = jnp.where(first, 0.0, halo_ref[...])
    rows8 = lax.broadcasted_iota(jnp.int32, halo.shape, 0)
    taps = [x]
    for j in (1, 2, 3):
        r = pltpu.roll(x, j, 0)
        top = jnp.where(rows8 < j, pltpu.roll(halo, j, 0), r[0:8])
        taps.append(jnp.concatenate([top, r[8:]], axis=0))
    return taps


def _conv_pre(taps, cw_ref, cb_ref):
    wv = cw_ref[...]
    pre = cb_ref[...] + wv[3:4, :] * taps[0]
    for j in (1, 2, 3):
        pre = pre + wv[3 - j:4 - j, :] * taps[j]
    return pre


def _conv_back(dpre, taps, cw_ref, nxt_ref):
    q = dpre.shape[0]
    rows8 = lax.broadcasted_iota(jnp.int32, (8, dpre.shape[1]), 0)
    wv = cw_ref[...]
    dx = wv[3:4, :] * dpre
    for j in (1, 2, 3):
        r = pltpu.roll(dpre, q - j, 0)
        bottom = jnp.where(rows8 >= 8 - j, pltpu.roll(nxt_ref[...], 8 - j, 0), r[q - 8:q])
        dx = dx + wv[3 - j:4 - j, :] * jnp.concatenate([r[:q - 8], bottom], axis=0)
    dw = jnp.concatenate([_colsum(dpre * taps[3 - kk]) for kk in range(4)], axis=0)
    nxt_ref[...] = dpre[0:8]
    return dx, dw, _colsum(dpre)


S5_CW = 256


def _cmul(ar, ai, br, bi):
    return ar * br - ai * bi, ar * bi + ai * br


def _scan8_complex(src_ref, dst_ref, lam_ref, st_ref, *, w, nb, reverse):
    rows = lax.broadcasted_iota(jnp.int32, (8, S5_CW), 0)
    b8 = lambda v: jnp.broadcast_to(v, (8, S5_CW))

    def shift(x, k):
        if reverse:
            return jnp.where(rows < 8 - k, pltpu.roll(x, 8 - k, 0), 0.0)
        return jnp.where(rows >= k, pltpu.roll(x, k, 0), 0.0)

    for c0 in range(0, w, S5_CW):
        re, im = pl.ds(c0, S5_CW), pl.ds(w + c0, S5_CW)
        pw = [(lam_ref[:, re], lam_ref[:, im])]
        for _ in range(7):
            pw.append(_cmul(*pw[-1], *pw[0]))
        pr, pi = b8(pw[7][0]), b8(pw[7][1])
        for j in range(7):
            sel = rows == (7 - j if reverse else j)
            pr, pi = jnp.where(sel, b8(pw[j][0]), pr), jnp.where(sel, b8(pw[j][1]), pi)
        steps = [(k, b8(pw[k - 1][0]), b8(pw[k - 1][1])) for k in (1, 2, 4)]
        edge = 0 if reverse else 7

        def blk(i, carry):
            hr, hi = carry
            base = pl.multiple_of((nb // 2 - 1 - i if reverse else i) * 16, 16)
            pend = []
            for off in ((8, 0) if reverse else (0, 8)):
                at = pl.ds(base + off, 8)
                xr, xi = src_ref[at, re], src_ref[at, im]
                for k, kr, ki in steps:
                    sr, si = shift(xr, k), shift(xi, k)
                    xr, xi = xr + kr * sr - ki * si, xi + kr * si + ki * sr
                pend.append((at, xr, xi))
            for at, xr, xi in pend:
                xr, xi = xr + pr * hr - pi * hi, xi + pr * hi + pi * hr
                dst_ref[at, re] = xr
                dst_ref[at, im] = xi
                hr, hi = b8(xr[edge:edge + 1, :]), b8(xi[edge:edge + 1, :])
            return hr, hi

        hr, hi = lax.fori_loop(0, nb // 2, blk, (st_ref[:, re], st_ref[:, im]))
        st_ref[:, re] = hr
        st_ref[:, im] = hi


def s5_fwd(proj, bcat, lam, ccat, dskip, gw, gb, *, name):
    t = proj.shape[0]
    tt = min(ROW_TILE, t)
    w2 = bcat.shape[1]

    def body(u_ref, b_ref, lam_ref, c_ref, d_ref, gw_ref, gb_ref, h_ref, y_ref, o_ref, bu_ref, st_ref):
        @pl.when(pl.program_id(0) == 0)
        def _():
            st_ref[...] = jnp.zeros_like(st_ref)

        u = u_ref[...]
        bu_ref[...] = _dot(u, b_ref[...])
        _scan8_complex(bu_ref, h_ref, lam_ref, st_ref, w=w2 // 2, nb=tt // 8, reverse=False)
        ylin = _dot(h_ref[...], c_ref[...])
        y_ref[...] = ylin
        (out,), _ = _s5_post_fwd_fn(ylin, u, d_ref[...], gw_ref[...], gb_ref[...])
        o_ref[...] = out.astype(o_ref.dtype)

    fixed = lambda a: pl.BlockSpec(a.shape, lambda i: (0, 0))
    row = pl.BlockSpec((tt, S5_WIDTH), lambda i: (i, 0))
    return pl.pallas_call(
        body, name=name, grid=(t // tt,),
        in_specs=[pl.BlockSpec((tt, S5_WIDTH), lambda i: (i, P_U // S5_WIDTH))] + [fixed(x) for x in (bcat, lam, ccat, dskip, gw, gb)],
        out_specs=[pl.BlockSpec((tt, w2), lambda i: (i, 0)), row, row],
        out_shape=[jax.ShapeDtypeStruct((t, w2), F32), jax.ShapeDtypeStruct((t, S5_WIDTH), F32),
                   jax.ShapeDtypeStruct((t, S5_WIDTH), BF16)],
        scratch_shapes=[pltpu.VMEM((tt, w2), F32), pltpu.VMEM((8, w2), F32)],
        compiler_params=_params(("arbitrary",)),
    )(proj, bcat, lam, ccat, dskip, gw, gb)


def s5_bwd(dycat, ylin, hs, proj, bcat, lam_adj, ccat, dskip, gw, gb, *, name):
    t = proj.shape[0]
    tt = min(ROW_TILE, t)
    n, w2 = t // tt, bcat.shape[1]
    w = w2 // 2

    def body(dout_ref, yl_ref, h_ref, hp_ref, u_ref, b_ref, lam_ref, c_ref, d_ref, gw_ref, gb_ref,
             du_ref, dc_ref, db_ref, dar_ref, dai_ref, dd_ref, dgw_ref, dgb_ref, g_ref, st_ref):
        i = pl.program_id(0)

        @pl.when(i == 0)
        def _():
            for r in (st_ref, dc_ref, db_ref, dar_ref, dai_ref, dd_ref, dgw_ref, dgb_ref):
                r[...] = jnp.zeros_like(r)

        (dy, du_a), post = _s5_post_bwd_fn(yl_ref[...], u_ref[...], dout_ref[...], d_ref[...], gw_ref[...], gb_ref[...])
        for r, v in zip((dd_ref, dgw_ref, dgb_ref), post):
            r[...] += v
        h = h_ref[...]
        g_ref[...] = _dot(dy, c_ref[...], _NT)
        dc_ref[...] += _dot(h, dy, _TN)
        _scan8_complex(g_ref, g_ref, lam_ref, st_ref, w=w, nb=tt // 8, reverse=True)
        g = g_ref[...]
        du_ref[...] = (du_a + _dot(g, b_ref[...], _NT)).astype(du_ref.dtype)
        db_ref[...] += _dot(u_ref[...], g, _TN)
        rows = lax.broadcasted_iota(jnp.int32, (tt, w2), 0)
        before = jnp.where(i == n - 1, 0.0, hp_ref[7:8, :])
        hprev = jnp.where(rows == 0, before, pltpu.roll(h, 1, 0))
        gr, gi, hr, hi = g[:, :w], g[:, w:], hprev[:, :w], hprev[:, w:]
        dar_ref[...] += _colsum(gr * hr + gi * hi)
        dai_ref[...] += _colsum(gi * hr - gr * hi)

    rev = lambda i: n - 1 - i
    row = lambda wd, cb=0: pl.BlockSpec((tt, wd), lambda i: (rev(i), cb))
    fixed = lambda shape: pl.BlockSpec(shape, lambda i: (0, 0))
    return pl.pallas_call(
        body, name=name, grid=(n,),
        in_specs=[row(S5_WIDTH, 2), row(S5_WIDTH), row(w2),
                  pl.BlockSpec((8, w2), lambda i: (jnp.maximum(rev(i) * (tt // 8) - 1, 0), 0)),
                  row(S5_WIDTH, P_U // S5_WIDTH)] + [fixed(x.shape) for x in (bcat, lam_adj, ccat, dskip, gw, gb)],
        out_specs=[row(S5_WIDTH), fixed(ccat.shape), fixed(bcat.shape), fixed((1, w)), fixed((1, w)),
                   fixed((1, S5_WIDTH)), fixed((S5_WIDTH, S5_WIDTH)), fixed((1, S5_WIDTH))],
        out_shape=[jax.ShapeDtypeStruct((t, S5_WIDTH), BF16), jax.ShapeDtypeStruct(ccat.shape, F32),
                   jax.ShapeDtypeStruct(bcat.shape, F32), jax.ShapeDtypeStruct((1, w), F32), jax.ShapeDtypeStruct((1, w), F32),
                   jax.ShapeDtypeStruct((1, S5_WIDTH), F32), jax.ShapeDtypeStruct((S5_WIDTH, S5_WIDTH), F32),
                   jax.ShapeDtypeStruct((1, S5_WIDTH), F32)],
        scratch_shapes=[pltpu.VMEM((tt, w2), F32), pltpu.VMEM((8, w2), F32)],
        compiler_params=_params(("arbitrary",)),
    )(dycat, ylin, hs, hs, proj, bcat, lam_adj, ccat, dskip, gw, gb)


def _scan8_real(a_ref, b_ref, o_ref, st_ref, *, nb, reverse):
    w = o_ref.shape[1]
    rows = lax.broadcasted_iota(jnp.int32, (8, w), 0)
    edge = 0 if reverse else 7

    def shift(x, k, fill):
        if reverse:
            return jnp.where(rows < 8 - k, pltpu.roll(x, 8 - k, 0), fill)
        return jnp.where(rows >= k, pltpu.roll(x, k, 0), fill)

    def blk(i, h):
        at = pl.ds(pl.multiple_of((nb - 1 - i if reverse else i) * 8, 8), 8)
        a, b = a_ref[at, :], b_ref[at, :]
        for k in (1, 2, 4):
            a, b = a * shift(a, k, 1.0), b + a * shift(b, k, 0.0)
        out = b + a * h
        o_ref[at, :] = out
        return jnp.broadcast_to(out[edge:edge + 1, :], (8, w))

    st_ref[...] = lax.fori_loop(0, nb, blk, st_ref[...])


def _rg_specs(tt, idx):
    return [pl.BlockSpec((tt, RG_WIDTH), lambda i: (idx(i), P_XR // RG_WIDTH)),
            pl.BlockSpec((8, RG_WIDTH), lambda i: (jnp.maximum(idx(i) * (tt // 8) - 1, 0), P_XR // RG_WIDTH)),
            pl.BlockSpec((tt, RG_WIDTH), lambda i: (idx(i), P_G // RG_WIDTH))]


def rg_fwd(proj, cw, cb, wa, wx, ba, bx, lam, *, name):
    t = proj.shape[0]
    tt = min(ROW_TILE, t)
    w = RG_WIDTH

    def body(x_ref, halo_ref, g_ref, cw_ref, cb_ref, wa_ref, wx_ref, ba_ref, bx_ref, lam_ref,
             y_ref, xc_ref, a_ref, h_ref, b_ref, st_ref):
        @pl.when(pl.program_id(0) == 0)
        def _():
            st_ref[...] = jnp.zeros_like(st_ref)

        xc = _conv_pre(_conv_taps(x_ref, halo_ref, pl.program_id(0) == 0), cw_ref, cb_ref)
        xc_ref[...] = xc
        r, i, sp, a, mult = _rg_gates(xc, wa_ref[...], wx_ref[...], ba_ref[...], bx_ref[...], lam_ref[...])
        a_ref[...] = a
        b_ref[...] = mult * (i * xc)
        _scan8_real(a_ref, b_ref, h_ref, st_ref, nb=tt // 8, reverse=False)
        y_ref[...] = (h_ref[...] * _gelu(g_ref[...])).astype(y_ref.dtype)

    fixed = lambda a: pl.BlockSpec(a.shape, lambda i: (0, 0))
    row = pl.BlockSpec((tt, w), lambda i: (i, 0))
    return pl.pallas_call(
        body, name=name, grid=(t // tt,),
        in_specs=_rg_specs(tt, lambda i: i) + [fixed(x) for x in (cw, cb, wa, wx, ba, bx, lam)],
        out_specs=[row] * 4,
        out_shape=[jax.ShapeDtypeStruct((t, w), BF16)] + [jax.ShapeDtypeStruct((t, w), F32)] * 3,
        scratch_shapes=[pltpu.VMEM((tt, w), F32), pltpu.VMEM((8, w), F32)],
        compiler_params=_params(("arbitrary",)),
    )(proj, proj, proj, cw, cb, wa, wx, ba, bx, lam)


def rg_bwd(proj, dycat, xc, a, h, cw, cb, wa, wx, ba, bx, lam, *, name):
    t = proj.shape[0]
    tt = min(ROW_TILE, t)
    n, w = t // tt, RG_WIDTH

    def body(x_ref, halo_ref, g_ref, dy_ref, xc_ref, a_ref, h_ref, hp_ref, cw_ref, wa_ref, wx_ref, ba_ref, bx_ref, lam_ref,
             dx_ref, dg_ref, dcw_ref, dcb_ref, dwa_ref, dwx_ref, dba_ref, dbx_ref, dlam_ref,
             au_ref, dh_ref, gs_ref, st_ref, anx_ref, nxt_ref):
        i = pl.program_id(0)
        accs = (dcw_ref, dcb_ref, dwa_ref, dwx_ref, dba_ref, dbx_ref, dlam_ref)

        @pl.when(i == 0)
        def _():
            for r in accs + (st_ref, anx_ref, nxt_ref):
                r[...] = jnp.zeros_like(r)

        h, g, dy, a = h_ref[...], g_ref[...], dy_ref[...], a_ref[...]
        dh_ref[...] = dy * _gelu(g)
        dg_ref[...] = (dy * h * _dgelu(g)).astype(dg_ref.dtype)
        rows = lax.broadcasted_iota(jnp.int32, (tt, w), 0)
        au_ref[...] = jnp.where(rows == tt - 1, anx_ref[0:1, :], pltpu.roll(a, tt - 1, 0))
        _scan8_real(au_ref, dh_ref, gs_ref, st_ref, nb=tt // 8, reverse=True)
        before = jnp.where(i == n - 1, 0.0, hp_ref[7:8, :])
        hprev = jnp.where(rows == 0, before, pltpu.roll(h, 1, 0))
        (dxc,), small = _rg_pre_bwd_fn(xc_ref[...], gs_ref[...], hprev, wa_ref[...], wx_ref[...], ba_ref[...], bx_ref[...], lam_ref[...])
        dx, dcw, dcb = _conv_back(dxc, _conv_taps(x_ref, halo_ref, i == n - 1), cw_ref, nxt_ref)
        dx_ref[...] = dx.astype(dx_ref.dtype)
        for r, v in zip(accs, (dcw, dcb) + tuple(small)):
            r[...] += v
        anx_ref[...] = a[0:8]

    rev = lambda i: n - 1 - i
    row = lambda cb_=0: pl.BlockSpec((tt, w), lambda i: (rev(i), cb_))
    fixed = lambda shape: pl.BlockSpec(shape, lambda i: (0, 0))
    acc_shapes = [(4, w), (1, w), (w, w), (w, w), (1, w), (1, w), (1, w)]
    return pl.pallas_call(
        body, name=name, grid=(n,),
        in_specs=_rg_specs(tt, rev) + [row(3), row(), row(), row(),
                                       pl.BlockSpec((8, w), lambda i: (jnp.maximum(rev(i) * (tt // 8) - 1, 0), 0))]
        + [fixed(x.shape) for x in (cw, wa, wx, ba, bx, lam)],
        out_specs=[row(), row()] + [fixed(sh) for sh in acc_shapes],
        out_shape=[jax.ShapeDtypeStruct((t, w), BF16)] * 2 + [jax.ShapeDtypeStruct(sh, F32) for sh in acc_shapes],
        scratch_shapes=[pltpu.VMEM((tt, w), F32)] * 3 + [pltpu.VMEM((8, w), F32)] * 3,
        compiler_params=_params(("arbitrary",)),
    )(proj, proj, proj, dycat, xc, a, h, h, cw, wa, wx, ba, bx, lam)


SSD_QQ = SSD_HEADS * SSD_CHUNK
SSD_GP = SSD_WIDTH // 2
SSD_GQ = SSD_QQ // 2


def _ssd_spread():
    h = jnp.arange(LANE)[:, None]
    spread_p = (jnp.arange(SSD_WIDTH)[None, :] // SSD_HEAD_DIM == h).astype(BF16)
    spread_q = (jnp.arange(SSD_QQ)[None, :] // SSD_CHUNK == h).astype(BF16)
    return spread_p, spread_q


def _ssd_prologue(dt_ref, prow_ref, sp_ref, sq_ref):
    q = SSD_CHUNK
    r = lax.broadcasted_iota(jnp.int32, (q, q), 0)
    c = lax.broadcasted_iota(jnp.int32, (q, q), 1)
    raw_c = dt_ref[...] + prow_ref[0:1, :]
    dt_c = _softplus(raw_c)
    a_r = -jnp.exp(prow_ref[1:2, :])
    cs_c = _dot_mask((r >= c).astype(F32), dt_c * a_r, mask_left=True, parts=3)
    both = _dot_mask(jnp.concatenate([dt_c, cs_c], axis=0), sp_ref[...], mask_left=False, parts=3)
    dt_x, cs_x = both[:q], both[q:]
    csx = _dot_mask(cs_c, sq_ref[...], mask_left=False, parts=3)
    rr = lax.broadcasted_iota(jnp.int32, (q, SSD_QQ), 0)
    ss = lax.broadcasted_iota(jnp.int32, (q, SSD_QQ), 1) & (q - 1)
    diag = rr == ss
    cs_row = _colsum(jnp.where(diag, csx, 0.0))
    lcat = jnp.exp(jnp.where(rr >= ss, csx - cs_row, -1e30))
    cl = cs_x[q - 1:q, :]
    return dict(raw_c=raw_c, dt_c=dt_c, a_r=a_r, dt_x=dt_x, cs_x=cs_x, lcat=lcat, diag=diag,
                ecs=jnp.exp(cs_x), wdec=jnp.exp(cl - cs_x), ecl=jnp.exp(cl), triu=(r <= c).astype(F32))


def _ssd_group(xbc_ref, g, lcat, xdt):
    ns, q = SSD_STATE, SSD_CHUNK
    bm = xbc_ref[:, pl.ds(SSD_WIDTH + g * ns, ns)]
    cm = xbc_ref[:, pl.ds(SSD_WIDTH + 2 * ns + g * ns, ns)]
    cb = _dot(cm, bm, _NT)
    lg = lcat[:, g * SSD_GQ:(g + 1) * SSD_GQ]
    wcat = jnp.concatenate([cb] * 4, axis=1) * lg
    head = lax.broadcasted_iota(jnp.int32, (1, SSD_GP), 1) // SSD_HEAD_DIM
    xg = xdt[:, g * SSD_GP:(g + 1) * SSD_GP]
    xbd = jnp.concatenate([jnp.where(head == j, xg, 0.0) for j in range(4)], axis=0)
    return bm, cm, lg, wcat, xbd, head


def _ssd_gate(yraw, z, nw):
    yg = yraw * _silu(z)
    r = lax.rsqrt(jnp.mean(yg * yg, axis=1, keepdims=True) + LN_EPS)
    return yg, r


def _ssd_specs(q, idx):
    return [pl.BlockSpec((q, SSD_XBC), lambda i: (idx(i), P_XBC // SSD_XBC)),
            pl.BlockSpec((8, SSD_XBC), lambda i: (jnp.maximum(idx(i) * (q // 8) - 1, 0), P_XBC // SSD_XBC)),
            pl.BlockSpec((q, SSD_WIDTH), lambda i: (idx(i), P_Z // SSD_WIDTH)),
            pl.BlockSpec((q, LANE), lambda i: (idx(i), P_DT // LANE)),
            pl.BlockSpec((4, SSD_XBC), lambda i: (0, 0)), pl.BlockSpec((1, SSD_XBC), lambda i: (0, 0)),
            pl.BlockSpec((8, LANE), lambda i: (0, 0)), pl.BlockSpec((1, SSD_WIDTH), lambda i: (0, 0)),
            pl.BlockSpec((1, SSD_WIDTH), lambda i: (0, 0)),
            pl.BlockSpec((LANE, SSD_WIDTH), lambda i: (0, 0)), pl.BlockSpec((LANE, SSD_QQ), lambda i: (0, 0))]


def ssd_fwd(proj, cw, cb, prow, d_x, nw, *, name):
    t = proj.shape[0]
    q, ns = SSD_CHUNK, SSD_STATE
    nc = t // q
    spread_p, spread_q = _ssd_spread()

    def body(x_ref, halo_ref, z_ref, dt_ref, cw_ref, cb_ref, prow_ref, dx_ref, nw_ref, sp_ref, sq_ref,
             y_ref, yraw_ref, sall_ref, s_ref, xbc_ref):
        @pl.when(pl.program_id(0) == 0)
        def _():
            s_ref[...] = jnp.zeros_like(s_ref)

        sall_ref[0] = s_ref[...]
        xbc_ref[...] = _silu(_conv_pre(_conv_taps(x_ref, halo_ref, pl.program_id(0) == 0), cw_ref, cb_ref))
        pr = _ssd_prologue(dt_ref, prow_ref, sp_ref, sq_ref)
        xs = xbc_ref[:, pl.ds(0, SSD_WIDTH)]
        xdt = xs * pr["dt_x"]
        xw = xdt * pr["wdec"]
        ys = []
        for g in range(2):
            gp = slice(g * SSD_GP, (g + 1) * SSD_GP)
            bm, cm, lg, wcat, xbd, head = _ssd_group(xbc_ref, g, pr["lcat"], xdt)
            st = s_ref[:, gp]
            ys.append(_dot(wcat, xbd) + pr["ecs"][:, gp] * _dot(cm, st) + xs[:, gp] * dx_ref[:, gp])
            s_ref[:, gp] = pr["ecl"][:, gp] * st + _dot(bm, xw[:, gp], _TN)
        yraw = jnp.concatenate(ys, axis=1)
        yraw_ref[...] = yraw
        yg, r = _ssd_gate(yraw, z_ref[...], nw_ref[...])
        y_ref[...] = (yg * r * nw_ref[...]).astype(y_ref.dtype)

    row = pl.BlockSpec((q, SSD_WIDTH), lambda i: (i, 0))
    return pl.pallas_call(
        body, name=name, grid=(nc,),
        in_specs=_ssd_specs(q, lambda i: i),
        out_specs=[row, row, pl.BlockSpec((1, ns, SSD_WIDTH), lambda i: (i, 0, 0))],
        out_shape=[jax.ShapeDtypeStruct((t, SSD_WIDTH), BF16), jax.ShapeDtypeStruct((t, SSD_WIDTH), F32),
                   jax.ShapeDtypeStruct((nc, ns, SSD_WIDTH), F32)],
        scratch_shapes=[pltpu.VMEM((ns, SSD_WIDTH), F32), pltpu.VMEM((q, SSD_XBC), F32)],
        compiler_params=_params(("arbitrary",)),
    )(proj, proj, proj, proj, cw, cb, prow, d_x, nw, spread_p, spread_q)


def ssd_bwd(proj, cw, cb, prow, d_x, nw, yraw, sall, dout, *, name):
    t = proj.shape[0]
    q, ns = SSD_CHUNK, SSD_STATE
    nc = t // q
    spread_p, spread_q = _ssd_spread()

    def body(x_ref, halo_ref, z_ref, dt_ref, cw_ref, cb_ref, prow_ref, dx_ref, nw_ref, sp_ref, sq_ref, yraw_ref, sall_ref, dout_ref,
             dxraw_ref, dz_ref, ddt_ref, dprm_ref, ddx_ref, dnw_ref, dcw_ref, dcb_ref, ds_ref, xbc_ref, dxbc_ref, nxt_ref):
        @pl.when(pl.program_id(0) == 0)
        def _():
            for r in (ds_ref, dprm_ref, ddx_ref, dnw_ref, dcw_ref, dcb_ref, nxt_ref):
                r[...] = jnp.zeros_like(r)

        taps = _conv_taps(x_ref, halo_ref, pl.program_id(0) == nc - 1)
        conv_pre = _conv_pre(taps, cw_ref, cb_ref)
        xbc_ref[...] = _silu(conv_pre)

        yraw, z, nwv, dout = yraw_ref[...], z_ref[...], nw_ref[...], dout_ref[...]
        yg, r = _ssd_gate(yraw, z, nwv)
        dnw_ref[...] += _colsum(dout * yg * r)
        dyn = dout * nwv
        dyg = r * dyn - yg * (r * r * r) * jnp.mean(dyn * yg, axis=1, keepdims=True)
        dy = dyg * _silu(z)
        dz_ref[...] = (dyg * yraw * _dsilu(z)).astype(dz_ref.dtype)

        pr = _ssd_prologue(dt_ref, prow_ref, sp_ref, sq_ref)
        xs = xbc_ref[:, pl.ds(0, SSD_WIDTH)]
        xdt = xs * pr["dt_x"]
        wdec, ecl = pr["wdec"], pr["ecl"]
        xw = xdt * wdec
        dzm_all = pr["ecs"] * dy
        last = (lax.broadcasted_iota(jnp.int32, (q, 1), 0) == q - 1).astype(F32)
        dxs, dcsxs, es = [], [], []
        for g in range(2):
            gp = slice(g * SSD_GP, (g + 1) * SSD_GP)
            bm, cm, lg, wcat, xbd, head = _ssd_group(xbc_ref, g, pr["lcat"], xdt)
            dyg_ = dy[:, gp]
            dwcat = _dot(dyg_, xbd, _NT)
            dxbd = _dot(wcat, dyg_, _TN)
            dxg = sum(jnp.where(head == j, dxbd[j * q:(j + 1) * q], 0.0) for j in range(4))
            es.append(dwcat * wcat)
            dmm = dwcat * lg
            dm = dmm[:, 0:q] + dmm[:, q:2 * q] + dmm[:, 2 * q:3 * q] + dmm[:, 3 * q:4 * q]
            dcm = _dot(dm, bm)
            dbm = _dot(dm, cm, _TN)
            st = sall_ref[0, :, gp]
            zmat = _dot(cm, st)
            dzm = dzm_all[:, gp]
            dcm = dcm + _dot(dzm, st, _NT)
            dst = _dot(cm, dzm, _TN)
            dcsx = dzm * zmat
            dsn = ds_ref[:, gp]
            dst = dst + ecl[:, gp] * dsn
            dclx = _colsum(dsn * st) * ecl[:, gp]
            dxw = _dot(bm, dsn)
            dbm = dbm + _dot(xw[:, gp], dsn, _NT)
            dxg = dxg + wdec[:, gp] * dxw
            tw = dxw * xdt[:, gp] * wdec[:, gp]
            dclx = dclx + _colsum(tw)
            dcsxs.append(dcsx - tw + last * dclx)
            ds_ref[:, gp] = dst
            dxs.append(dxg)
            dxbc_ref[:, pl.ds(SSD_WIDTH + g * ns, ns)] = dbm
            dxbc_ref[:, pl.ds(SSD_WIDTH + 2 * ns + g * ns, ns)] = dcm
        dx = jnp.concatenate(dxs, axis=1)
        dxbc_ref[:, pl.ds(0, SSD_WIDTH)] = dx * pr["dt_x"] + dy * dx_ref[...]
        ddx_ref[...] += _colsum(dy * xs)
        red = _dot_mask(jnp.concatenate([jnp.concatenate(dcsxs, axis=1), dx * xs], axis=0), sp_ref[...], _NT,
                        mask_left=False, parts=2)
        e_all = jnp.concatenate(es, axis=1)
        e_red = _dot_mask(e_all - jnp.where(pr["diag"], _colsum(e_all), 0.0), sq_ref[...], _NT, mask_left=False, parts=2)
        dadt = _dot_mask(pr["triu"], red[:q] + e_red, mask_left=True, parts=2)
        draw = (red[q:] + dadt * pr["a_r"]) * _sigmoid(pr["raw_c"])
        ddt_ref[...] = draw.astype(ddt_ref.dtype)
        zero = jnp.zeros((6, LANE), F32)
        dprm_ref[...] += jnp.concatenate([_colsum(draw), _colsum(dadt * pr["dt_c"]) * pr["a_r"], zero], axis=0)
        dxr, dcw, dcb = _conv_back(dxbc_ref[...] * _dsilu(conv_pre), taps, cw_ref, nxt_ref)
        dxraw_ref[...] = dxr.astype(dxraw_ref.dtype)
        dcw_ref[...] += dcw
        dcb_ref[...] += dcb

    rev = lambda i: nc - 1 - i
    row = lambda w: pl.BlockSpec((q, w), lambda i: (rev(i), 0))
    fixed = lambda shape: pl.BlockSpec(shape, lambda i: (0, 0))
    return pl.pallas_call(
        body, name=name, grid=(nc,),
        in_specs=_ssd_specs(q, rev) + [row(SSD_WIDTH), pl.BlockSpec((1, ns, SSD_WIDTH), lambda i: (rev(i), 0, 0)),
                                       row(SSD_WIDTH)],
        out_specs=[row(SSD_XBC), row(SSD_WIDTH), row(LANE), fixed((8, LANE)), fixed((1, SSD_WIDTH)), fixed((1, SSD_WIDTH)),
                   fixed((4, SSD_XBC)), fixed((1, SSD_XBC))],
        out_shape=[jax.ShapeDtypeStruct((t, SSD_XBC), BF16), jax.ShapeDtypeStruct((t, SSD_WIDTH), BF16),
                   jax.ShapeDtypeStruct((t, LANE), BF16), jax.ShapeDtypeStruct((8, LANE), F32),
                   jax.ShapeDtypeStruct((1, SSD_WIDTH), F32), jax.ShapeDtypeStruct((1, SSD_WIDTH), F32),
                   jax.ShapeDtypeStruct((4, SSD_XBC), F32), jax.ShapeDtypeStruct((1, SSD_XBC), F32)],
        scratch_shapes=[pltpu.VMEM((ns, SSD_WIDTH), F32), pltpu.VMEM((q, SSD_XBC), F32), pltpu.VMEM((q, SSD_XBC), F32),
                        pltpu.VMEM((8, SSD_XBC), F32)],
        compiler_params=_params(("arbitrary",)),
    )(proj, proj, proj, proj, cw, cb, prow, d_x, nw, spread_p, spread_q, yraw, sall, dout)


def _me():
    return lax.axis_index("x"), lax.axis_index("y"), lax.axis_index("c")


_ANY = pl.BlockSpec(memory_space=pl.ANY)
_MESH = pl.DeviceIdType.MESH


_AG_SEMS = [pltpu.SemaphoreType.DMA((7,)), pltpu.SemaphoreType.DMA((7,)), pltpu.SemaphoreType.DMA(())]
_RS_SEMS = [pltpu.SemaphoreType.DMA((3,)), pltpu.SemaphoreType.DMA((3,)), pltpu.SemaphoreType.DMA(())]


def _ag_phases(src, dst, send_sems, recv_sems, local_sem):
    x, y, c = _me()
    me, sibling = (x, y, c), (x, y, 1 - c)
    chips = [(1 - x, y), (x, 1 - y), (1 - x, 1 - y)]

    def slot(px, py, pc):
        return dst.at[4 * px + 2 * py + pc]

    def copy(kk, blk, to, from_src=False):
        return pltpu.make_async_remote_copy(
            src_ref=src if from_src else slot(*blk), dst_ref=slot(*blk),
            send_sem=send_sems.at[kk], recv_sem=recv_sems.at[kk], device_id=to, device_id_type=_MESH)

    mine = lambda: pltpu.make_async_copy(src, slot(*me), local_sem)
    first = lambda: [copy(0, me, sibling, True)] + [copy(1 + j, me, (*chip, c), True) for j, chip in enumerate(chips)]
    passed = lambda j: copy(4 + j, (*chips[j], c), sibling)

    def start():
        mine().start()
        for cp in first():
            cp.start()

    def forward():
        for j, chip in enumerate(chips):
            copy(1 + j, (*chip, c), me).wait_recv()
            passed(j).start()

    def finish():
        copy(0, sibling, me).wait_recv()
        for j, chip in enumerate(chips):
            copy(4 + j, (*chip, 1 - c), me).wait_recv()
        for cp in first() + [passed(j) for j in range(3)]:
            cp.wait_send()
        mine().wait()

    return start, forward, finish


def _rs_chip_phases(src, dst, send_sems, recv_sems, local_sem, rows=None):
    x, y, c = _me()
    q_me = 2 * x + y
    pick = (lambda q: src.at[q]) if rows is None else (lambda q: src.at[q, rows])
    local = lambda: pltpu.make_async_copy(pick(q_me), dst.at[q_me], local_sem)
    copies = lambda: [pltpu.make_async_remote_copy(src_ref=pick(2 * px + py), dst_ref=dst.at[q_me], send_sem=send_sems.at[j],
                                                   recv_sem=recv_sems.at[j], device_id=(px, py, c), device_id_type=_MESH)
                      for j, (px, py) in enumerate([(1 - x, y), (x, 1 - y), (1 - x, 1 - y)])]

    def start():
        local().start()
        for cp in copies():
            cp.start()

    def finish():
        for cp in copies():
            cp.wait()
        local().wait()

    return start, finish


def all_gather(block, *, name):
    def body(src, dst, send_sems, recv_sems, local_sem):
        for phase in _ag_phases(src, dst, send_sems, recv_sems, local_sem):
            phase()

    return pl.pallas_call(
        body, name=name, in_specs=[_ANY], out_specs=_ANY,
        out_shape=jax.ShapeDtypeStruct((N_DEV,) + block.shape, block.dtype), scratch_shapes=list(_AG_SEMS),
    )(block)


RS_PIECES = 4


def rs_sibling_exchange(halves, *, name):
    _, nq, r, l = halves.shape
    rows = r // RS_PIECES
    assert r % RS_PIECES == 0 and rows % 16 == 0

    def body(src, dst, send_sems, recv_sems):
        x, y, c = _me()
        copies = []
        for q in range(nq):
            for i in range(RS_PIECES):
                kk = q * RS_PIECES + i
                cp = pltpu.make_async_remote_copy(
                    src_ref=src.at[1 - c, q, pl.ds(i * rows, rows)], dst_ref=dst.at[q, pl.ds(i * rows, rows)],
                    send_sem=send_sems.at[kk], recv_sem=recv_sems.at[kk], device_id=(x, y, 1 - c), device_id_type=_MESH)
                cp.start()
                copies.append(cp)
        for cp in copies:
            cp.wait()

    n_copies = nq * RS_PIECES
    return pl.pallas_call(
        body, name=name, in_specs=[_ANY], out_specs=_ANY,
        out_shape=jax.ShapeDtypeStruct((nq, r, l), halves.dtype),
        scratch_shapes=[pltpu.SemaphoreType.DMA((n_copies,)), pltpu.SemaphoreType.DMA((n_copies,))],
    )(halves)


def pair_sum_bf16(halves, theirs, *, name, tt=512):
    _, nq, r, wd = halves.shape
    tt = min(tt, r)
    parity = lax.axis_index("c").astype(jnp.int32).reshape(1)

    def body(c_ref, own_ref, sib_ref, o_ref):
        o_ref[...] = (own_ref[...] + sib_ref[...]).astype(BF16)

    return pl.pallas_call(
        body, name=name,
        grid_spec=pltpu.PrefetchScalarGridSpec(
            num_scalar_prefetch=1, grid=(nq, r // tt),
            in_specs=[pl.BlockSpec((None, None, tt, wd), lambda q, i, c: (c[0], q, i, 0)),
                      pl.BlockSpec((None, tt, wd), lambda q, i, c: (q, i, 0))],
            out_specs=pl.BlockSpec((None, tt, wd), lambda q, i, c: (q, i, 0))),
        out_shape=jax.ShapeDtypeStruct((nq, r, wd), BF16),
        compiler_params=_params(("parallel", "parallel")),
    )(parity, halves, theirs)


def rs_chip_exchange(part, *, name):
    def body(src, dst, send_sems, recv_sems, local_sem):
        for phase in _rs_chip_phases(src, dst, send_sems, recv_sems, local_sem):
            phase()

    return pl.pallas_call(
        body, name=name, in_specs=[_ANY], out_specs=_ANY,
        out_shape=jax.ShapeDtypeStruct(part.shape, part.dtype), scratch_shapes=list(_RS_SEMS),
    )(part)


def adamw(slabs, w, m, v, *, name, tt):
    ns, (r, wd) = slabs.shape[0], w.shape
    tt = min(tt, r)
    assert r % tt == 0

    def body(s_ref, w_ref, m_ref, v_ref, g_ref, d_ref, nm_ref, nv_ref):
        g = s_ref[0].astype(F32)
        for kdev in range(1, ns):
            g = g + s_ref[kdev].astype(F32)
        wv = w_ref[...]
        nm = ADAM_B1 * m_ref[...] + (1.0 - ADAM_B1) * g
        nv = ADAM_B2 * v_ref[...] + (1.0 - ADAM_B2) * (g * g)
        m_hat = nm / (1.0 - ADAM_B1 ** ADAM_STEP)
        v_hat = nv / (1.0 - ADAM_B2 ** ADAM_STEP)
        g_ref[...] = g
        d_ref[...] = -ADAM_LR * (m_hat / (jnp.sqrt(v_hat) + ADAM_EPS) + ADAM_WD * wv)
        nm_ref[...] = nm
        nv_ref[...] = nv

    spec = pl.BlockSpec((tt, wd), lambda i: (i, 0))
    return pl.pallas_call(
        body, name=name, grid=(r // tt,),
        in_specs=[pl.BlockSpec((ns, tt, wd), lambda i: (0, i, 0)), spec, spec, spec],
        out_specs=[spec] * 4, out_shape=[jax.ShapeDtypeStruct((r, wd), F32)] * 4,
        compiler_params=_params(("parallel",)),
    )(slabs, w, m, v)


WIDE = 1024
BIG = [("w_in", True, 289), ("w_out", False, 128), ("xa_wq", False, 128), ("xa_wk", False, 128), ("xa_wv", False, 128),
       ("xa_wo", False, 128), ("mlp_w2", False, 512), ("mlp_w1", True, 512)]
TINY = [("ssd_conv_w", 2), ("s5_glu_w", 1), ("rg_conv_w", 2)]
KEEP_F32 = ("ssd_conv_w", "rg_conv_w")
TINY_ROWS = 32
SHARDED = [name for name, _, _ in BIG] + [name for name, _ in TINY]
SMALL = ["ssd_conv_b", "ssd_dt_bias", "ssd_a_log", "ssd_d", "ssd_norm_w", "s5_lam_re", "s5_lam_im",
         "s5_log_step", "s5_b_re", "s5_b_im", "s5_c_re", "s5_c_im", "s5_d", "s5_glu_b", "rg_conv_b",
         "rg_wa", "rg_ba", "rg_wx", "rg_bx", "rg_lambda", "ln1_g", "ln1_b", "ln2_g", "ln2_b", "ln3_g", "ln3_b"]
WEIGHTS = ['w_in', 'w_out', 'ssd_conv_w', 'ssd_conv_b', 'ssd_dt_bias', 'ssd_a_log', 'ssd_d', 'ssd_norm_w',
           's5_lam_re', 's5_lam_im', 's5_log_step', 's5_b_re', 's5_b_im', 's5_c_re', 's5_c_im', 's5_d',
           's5_glu_w', 's5_glu_b', 'rg_conv_w', 'rg_conv_b', 'rg_wa', 'rg_ba', 'rg_wx', 'rg_bx', 'rg_lambda',
           'ln1_g', 'ln1_b', 'xa_wq', 'xa_wk', 'xa_wv', 'xa_wo', 'ln2_g', 'ln2_b', 'mlp_w1', 'mlp_w2',
           'ln3_g', 'ln3_b']


def _pad16(rows):
    return -(-rows // 16) * 16


def _pack_rows(flat, mult):
    n = flat.shape[-1]
    r = -(-n // (LANE * mult)) * mult
    pad = [(0, 0)] * (flat.ndim - 1) + [(0, r * LANE - n)]
    return jnp.pad(flat, pad).reshape(flat.shape[:-1] + (r, LANE))


def _unpack(packed, shapes):
    lead = packed.shape[:-2]
    flat = packed.reshape(lead + (-1,))
    out, off = [], 0
    for s in shapes:
        n = math.prod(s)
        out.append(flat[..., off:off + n].reshape(lead + tuple(s)))
        off += n
    return out


PACK_ROWS = 2048


def _tiny_block(flat):
    pad = [(0, 0)] * (flat.ndim - 1) + [(0, TINY_ROWS * WIDE - flat.shape[-1])]
    return jnp.pad(flat, pad).reshape(flat.shape[:-1] + (TINY_ROWS, WIDE))


def _pack_layer(big, tiny_flat=None):
    blocks, used = [], 0
    some = big[BIG[0][0]]

    def zeros(rows):
        return jnp.zeros(some.shape[:-2] + (rows, WIDE), some.dtype)

    for name, _, rows in BIG:
        blocks.append(jnp.pad(big[name], [(0, 0)] * (some.ndim - 2) + [(0, _pad16(rows) - rows), (0, 0)]))
        used += _pad16(rows)
    if tiny_flat is not None:
        blocks.append(_tiny_block(tiny_flat))
        used += TINY_ROWS
    return jnp.concatenate(blocks + [zeros(PACK_ROWS - used)], axis=-2)


def _unpack_layer(packed):
    big, off = {}, 0
    for name, _, rows in BIG:
        big[name] = packed[..., off:off + rows, :]
        off += _pad16(rows)
    return big, packed[..., off:off + TINY_ROWS, :].reshape(packed.shape[:-2] + (TINY_ROWS * WIDE,))


def _split_flat(flat, shapes):
    out, off = [], 0
    for s in shapes:
        n = math.prod(s)
        out.append(flat[..., off:off + n].reshape(flat.shape[:-1] + tuple(s)))
        off += n
    return out


def _to_full(gathered, axis):
    g = jnp.moveaxis(gathered, 0, axis)
    s = g.shape
    return g.reshape(s[:axis] + (s[axis] * s[axis + 1],) + s[axis + 2:])


def _to_slabs(full, axis):
    s = full.shape
    g = full.reshape(s[:axis] + (N_DEV, s[axis] // N_DEV) + s[axis + 1:])
    return jnp.moveaxis(g, axis, 0)


def _blockdiag(w):
    h, i, j = w.shape
    eye = jnp.eye(h, dtype=w.dtype)
    return (w[:, :, None, :] * eye[:, None, :, None]).reshape(h * i, h * j)


def _blockdiag_extract(m, h):
    i, j = m.shape[0] // h, m.shape[1] // h
    eye = jnp.eye(h, dtype=m.dtype)
    return (m.reshape(h, i, h, j) * eye[:, None, :, None]).sum(axis=2)


def _s5_disc(lr, li, ls, bre, bim):
    step = jnp.exp(ls)[:, None]
    er = jnp.exp(lr * step)
    ar, ai = er * jnp.cos(li * step), er * jnp.sin(li * step)
    nr, ni, den = ar - 1.0, ai, lr * lr + li * li
    qr, qi = (nr * lr + ni * li) / den, (ni * lr - nr * li) / den
    bbr = qr[..., None] * bre - qi[..., None] * bim
    bbi = qr[..., None] * bim + qi[..., None] * bre
    return ar, ai, bbr, bbi


def _row(v, width=None):
    v = v.reshape(1, -1)
    if width is not None and v.shape[1] < width:
        v = jnp.pad(v, ((0, 0), (0, width - v.shape[1])))
    return v


def _relu2(a):
    r = jnp.maximum(a, 0.0)
    return r * r


def _add_alpha(acc, d):
    return acc + ALPHA * d


def _layer_params(full, small, l):
    p = {}
    w_in = full["w_in"]
    z, xbc, dt, u, xr, g = w_in[0:512], w_in[512:1536], w_in[1536:1544], w_in[1544:1800], w_in[1800:2056], w_in[2056:2312]
    p["w_inp"] = jnp.concatenate([xbc, z, u, xr, g, dt, jnp.zeros((D_INP - P_DT - 8, D_MODEL), w_in.dtype)], axis=0)
    for k_ in ("w_out", "xa_wq", "xa_wk", "xa_wv", "xa_wo", "mlp_w1", "mlp_w2"):
        p[k_] = full[k_]
    p["s5_glu_w"] = full["s5_glu_w"][l]
    p["ssd_cw"], p["ssd_cb"] = full["ssd_conv_w"][l], _row(small["ssd_conv_b"][l])
    dtb, alog, dsk = small["ssd_dt_bias"][l], small["ssd_a_log"][l], small["ssd_d"][l]
    p["prow"] = jnp.concatenate([_row(dtb, LANE), _row(alog, LANE), jnp.zeros((6, LANE), F32)], axis=0)
    p["ssd_dx"] = _row(jnp.repeat(dsk, SSD_HEAD_DIM))
    p["ssd_nw"] = _row(small["ssd_norm_w"][l])
    s5_in = (small["s5_lam_re"][l], small["s5_lam_im"][l], small["s5_log_step"][l], small["s5_b_re"][l], small["s5_b_im"][l])
    (ar, ai, bbr, bbi), p["s5_vjp"] = jax.vjp(_s5_disc, *s5_in)
    p["lam_fwd"] = jnp.concatenate([_row(ar), _row(ai)], axis=1)
    p["lam_adj"] = jnp.concatenate([_row(ar), _row(-ai)], axis=1)
    p["bcat"] = jnp.concatenate([_blockdiag(jnp.swapaxes(bbr, 1, 2)), _blockdiag(jnp.swapaxes(bbi, 1, 2))], axis=1)
    p["ccat"] = jnp.concatenate([_blockdiag(jnp.swapaxes(small["s5_c_re"][l], 1, 2)),
                                 -_blockdiag(jnp.swapaxes(small["s5_c_im"][l], 1, 2))], axis=0)
    p["s5_d"], p["s5_glu_b"] = _row(small["s5_d"][l]), _row(small["s5_glu_b"][l])
    p["rg_cw"], p["rg_cb"] = full["rg_conv_w"][l], _row(small["rg_conv_b"][l])
    p["rg_wa"], p["rg_wx"] = _blockdiag(small["rg_wa"][l]), _blockdiag(small["rg_wx"][l])
    p["rg_ba"], p["rg_bx"], p["rg_lam"] = _row(small["rg_ba"][l]), _row(small["rg_bx"][l]), _row(small["rg_lambda"][l])
    for i in (1, 2, 3):
        p[f"g{i}"], p[f"b{i}"] = _row(small[f"ln{i}_g"][l]), _row(small[f"ln{i}_b"][l])
    return p


def _take_side(res, n_out, got):
    res = res if isinstance(res, (list, tuple)) else (res,)
    got.extend(res[n_out:])
    return res[0] if n_out == 1 else res[:n_out]


def _layer_fwd(h0, mem, p, sides={}):
    t = h0.shape[0]
    s = {"h0": h0}
    got = []
    proj = _take_side(mm(h0, p["w_inp"], tb=True, name="in_proj", side=sides.get("in_proj")), 1, got)
    y_ssd, yraw, sall = ssd_fwd(proj, p["ssd_cw"], p["ssd_cb"], p["prow"], p["ssd_dx"], p["ssd_nw"], name="ssd_fwd")
    hs5, ylin, y_s5 = s5_fwd(proj, p["bcat"], p["lam_fwd"], p["ccat"], p["s5_d"], p["s5_glu_w"], p["s5_glu_b"], name="s5_fwd")
    rg_prm = (p["rg_cw"], p["rg_cb"], p["rg_wa"], p["rg_wx"], p["rg_ba"], p["rg_bx"], p["rg_lam"])
    y_rg, xc, a_rg, h_rg = rg_fwd(proj, *rg_prm, name="rg_fwd")
    ycat = jnp.concatenate([y_ssd, y_s5, y_rg], axis=1)
    pre1, h1 = mm_ln(ycat, p["w_out"], h0, p["g1"], p["b1"], name="out_proj")
    q = mm(h1, p["xa_wq"], name="xa_q", out_dtype=BF16)
    k = mm(mem, p["xa_wk"], name="xa_kv")
    v = mm(mem, p["xa_wv"], name="xa_kv")
    (o,), _ = rowk(_attn_fwd_fn, [(q, D_MODEL, 0)], [k, v], [D_MODEL], [], rows=t, name="xa_fwd", out_dtypes=[BF16])
    pre2, h2 = mm_ln(o, p["xa_wo"], h1, p["g2"], p["b2"], name="xa_o")
    a_mlp = _take_side(mm(h2, p["mlp_w1"], tb=True, name="mlp_up", side=sides.get("mlp_up")), 1, got)
    pre3, h3 = _take_side(mm_ln(a_mlp, p["mlp_w2"], h2, p["g3"], p["b3"], fa=_relu2, name="mlp_down",
                                side=sides.get("mlp_down")), 2, got)
    s.update(proj=proj, yraw=yraw, sall=sall, hs5=hs5, ylin=ylin, xc=xc, a_rg=a_rg, h_rg=h_rg,
             ycat=ycat, pre1=pre1, h1=h1, q=q, k=k, v=v, o=o, pre2=pre2, h2=h2, a_mlp=a_mlp, pre3=pre3)
    return h3, s, got


def _layer_bwd(dh3, mem, p, s, l, gfull, gsmall, sides={}):
    t = dh3.shape[0]
    proj = s["proj"]
    dpre3, dg3, db3 = ln_bwd(s["pre3"], dh3, p["g3"], name="ln_bwd")
    got = []
    da = _take_side(mm(dpre3, p["mlp_w2"], tb=True, o_extra=(s["a_mlp"],), fo=lambda acc, a: acc * 2.0 * jnp.maximum(a, 0.0),
                       name="mlp_da", out_dtype=BF16, side=sides.get("mlp_da")), 1, got)
    gfull["mlp_w2"][l] = mm(s["a_mlp"], dpre3, ta=True, fa=_relu2, name="mlp_dw2")
    gfull["mlp_w1"][l] = mm(da, s["h2"], ta=True, name="mlp_dw1")
    dh2 = _take_side(mm(da, p["mlp_w1"], o_extra=(dpre3,), fo=_add_alpha, name="mlp_dx", side=sides.get("mlp_dx")), 1, got)
    dpre2, dg2, db2 = ln_bwd(s["pre2"], dh2, p["g2"], name="ln_bwd")
    do = _take_side(mm(dpre2, p["xa_wo"], tb=True, name="xa_do", out_dtype=BF16, side=sides.get("xa_do")), 1, got)
    gfull["xa_wo"][l] = mm(s["o"], dpre2, ta=True, name="dw_sq")
    (dq,), (dk, dv) = rowk(_attn_bwd_fn, [(s["q"], D_MODEL, 0), (do, D_MODEL, 0)], [s["k"], s["v"]], [D_MODEL],
                           [(256, D_MODEL), (256, D_MODEL)], rows=t, name="xa_bwd", out_dtypes=[BF16])
    gfull["xa_wq"][l] = mm(s["h1"], dq, ta=True, name="dw_sq")
    gfull["xa_wk"][l] = mm(mem, dk, ta=True, name="dw_kv")
    gfull["xa_wv"][l] = mm(mem, dv, ta=True, name="dw_kv")
    dh1 = mm(dq, p["xa_wq"], tb=True, o_extra=(dpre2,), fo=_add_alpha, name="dx_sq")
    dpre1, dg1, db1 = ln_bwd(s["pre1"], dh1, p["g1"], name="ln_bwd")
    dycat = mm(dpre1, p["w_out"], tb=True, name="xa_do")
    gfull["w_out"][l] = mm(s["ycat"], dpre1, ta=True, name="dw_sq")
    rg_prm = (p["rg_cw"], p["rg_cb"], p["rg_wa"], p["rg_wx"], p["rg_ba"], p["rg_bx"], p["rg_lam"])
    dxr, dg_rg, d_rgcw, d_rgcb, dwa, dwx, dba, dbx, dlam = rg_bwd(proj, dycat, s["xc"], s["a_rg"], s["h_rg"], *rg_prm, name="rg_bwd")
    du, dccat, dbcat, dar, dai, d_s5d, d_gluw, d_glub = s5_bwd(dycat, s["ylin"], s["hs5"], proj, p["bcat"], p["lam_adj"], p["ccat"],
                                                               p["s5_d"], p["s5_glu_w"], p["s5_glu_b"], name="s5_bwd")
    dxbc, dz, ddt, dprm, ddx, dnw, d_scw, d_scb = ssd_bwd(proj, p["ssd_cw"], p["ssd_cb"], p["prow"], p["ssd_dx"], p["ssd_nw"],
                                                         s["yraw"], s["sall"], dycat, name="ssd_bwd")
    dproj = jnp.concatenate([dxbc, dz, du, dxr, dg_rg, ddt, jnp.zeros((t, D_INP - P_DT - LANE), BF16)], axis=1)
    dh0 = mm(dproj, p["w_inp"], o_extra=(dpre1,), fo=_add_alpha, name="in_proj_dx")
    dwp = mm(dproj, s["h0"], ta=True, name="in_proj_dw")
    gfull["w_in"][l] = jnp.concatenate([dwp[P_Z:P_Z + 512], dwp[P_XBC:P_XBC + 1024], dwp[P_DT:P_DT + 8],
                                        dwp[P_U:P_U + 256], dwp[P_XR:P_XR + 256], dwp[P_G:P_G + 256]], axis=0)
    gfull["ssd_conv_w"][l], gfull["rg_conv_w"][l], gfull["s5_glu_w"][l] = d_scw, d_rgcw, d_gluw
    ng, ns = S5_GROUPS, S5_STATE
    dbbr = jnp.swapaxes(_blockdiag_extract(dbcat[:, :S5_NSTATE], ng), 1, 2)
    dbbi = jnp.swapaxes(_blockdiag_extract(dbcat[:, S5_NSTATE:], ng), 1, 2)
    d_lr, d_li, d_ls, d_bre, d_bim = p["s5_vjp"]((dar.reshape(ng, ns), dai.reshape(ng, ns), dbbr, dbbi))
    gsmall["s5_lam_re"][l], gsmall["s5_lam_im"][l], gsmall["s5_log_step"][l] = d_lr, d_li, d_ls
    gsmall["s5_b_re"][l], gsmall["s5_b_im"][l] = d_bre, d_bim
    gsmall["s5_c_re"][l] = jnp.swapaxes(_blockdiag_extract(dccat[:S5_NSTATE], ng), 1, 2)
    gsmall["s5_c_im"][l] = -jnp.swapaxes(_blockdiag_extract(dccat[S5_NSTATE:], ng), 1, 2)
    gsmall["s5_d"][l], gsmall["s5_glu_b"][l] = d_s5d[0], d_glub[0]
    gsmall["ssd_conv_b"][l], gsmall["rg_conv_b"][l] = d_scb[0], d_rgcb[0]
    gsmall["ssd_dt_bias"][l], gsmall["ssd_a_log"][l] = dprm[0, :8], dprm[1, :8]
    gsmall["ssd_d"][l] = ddx.reshape(SSD_HEADS, SSD_HEAD_DIM).sum(axis=1)
    gsmall["ssd_norm_w"][l] = dnw[0]
    gsmall["rg_wa"][l], gsmall["rg_wx"][l] = _blockdiag_extract(dwa, RG_BLOCKS), _blockdiag_extract(dwx, RG_BLOCKS)
    gsmall["rg_ba"][l], gsmall["rg_bx"][l] = dba.reshape(RG_BLOCKS, RG_BLOCK_DIM), dbx.reshape(RG_BLOCKS, RG_BLOCK_DIM)
    gsmall["rg_lambda"][l] = dlam[0]
    for i, (dg, db) in zip((1, 2, 3), ((dg1, db1), (dg2, db2), (dg3, db3))):
        gsmall[f"ln{i}_g"][l], gsmall[f"ln{i}_b"][l] = dg[0], db[0]
    return dh0, got


def _step(a):
    h = a["x"][0]
    mem = a["mem"][0]
    t = h.shape[0]
    r4, r3 = PACK_ROWS // 4, 3 * PACK_ROWS // 8

    def my_shards(pre):
        return ({name: (jnp.swapaxes(a[pre + name], 1, 2) if tr else a[pre + name]) for name, tr, _ in BIG},
                [a[pre + name] for name, _ in TINY])

    def my_pack(pre, l):
        big, tiny = my_shards(pre)
        return _pack_layer({name: w[l] for name, w in big.items()},
                           jnp.concatenate([w.reshape(-1) for w in tiny]) if l == 0 else None)

    big, tiny = my_shards("")
    tiny16 = [(lax.bitcast_convert_type(w, BF16) if name in KEEP_F32 else w.astype(BF16)).reshape(-1)
              for (name, _), w in zip(TINY, tiny)]
    packed = [_pack_layer({name: w[l].astype(BF16) for name, w in big.items()}, jnp.concatenate(tiny16) if l == 0 else None)
              for l in range(DEPTH)]
    small = {name: a[name] for name in SMALL}

    def gathered_weights(g):
        gbig, gtiny = _unpack_layer(g)
        return {name: w.reshape(-1, WIDE) for name, w in gbig.items()}, gtiny

    full, gtiny = gathered_weights(all_gather(packed[0], name="ag_weights"))
    tiny_shapes = [w.shape + ((2,) if name in KEEP_F32 else ()) for (name, _), w in zip(TINY, tiny)]
    tiny_full = {name: _to_full(lax.bitcast_convert_type(g, F32) if name in KEEP_F32 else g, axis)
                 for (name, axis), g in zip(TINY, _split_flat(gtiny, tiny_shapes))}
    p0 = _layer_params({**full, **tiny_full}, small, 0)
    h, s0, got = _layer_fwd(h, mem, p0, sides={"in_proj": ("gather", packed[1], 0, r4), "mlp_up": ("gather", packed[1], r4, r3),
                                                "mlp_down": ("gather", packed[1], r4 + r3, r3)})
    full, _ = gathered_weights(jnp.concatenate(got, axis=1))
    p1 = _layer_params({**full, **tiny_full}, small, 1)
    h, s1, _ = _layer_fwd(h, mem, p1)
    (dh,), (loss_part,) = rowk(_loss_fn, [(h, D_MODEL, 0), (a["loss_target"][0], D_MODEL, 0)], [], [D_MODEL], [(1, 1)],
                               rows=t, name="loss_head")
    loss = lax.psum(loss_part[0, 0], ("x", "y", "c"))
    gfull = {name: [None] * DEPTH for name in SHARDED}
    gsmall = {name: [None] * DEPTH for name in SMALL}

    def chip_partials(l):
        gbig = {name: gfull[name][l].reshape(N_DEV, rows, WIDE) for name, _, rows in BIG}
        gtiny = None
        if l == 0:
            gtiny = jnp.concatenate([_to_slabs(jnp.stack(gfull[name]), axis).reshape(N_DEV, -1) for name, axis in TINY], axis=1)
        slabs = _pack_layer(gbig, gtiny)
        halves = jnp.swapaxes(slabs.reshape((4, 2) + slabs.shape[1:]), 0, 1)
        theirs = rs_sibling_exchange(halves, name="rs_sibling")
        return pair_sum_bf16(halves, theirs, name="rs_pair_sum")

    dh, _ = _layer_bwd(dh, mem, p1, s1, 1, gfull, gsmall)
    part1 = chip_partials(1)
    dh, got = _layer_bwd(dh, mem, p0, s0, 0, gfull, gsmall, sides={"mlp_da": ("chips", part1, 0, r3), "mlp_dx": ("chips", part1, r3, r3),
                                                                    "xa_do": ("chips", part1, 2 * r3, r4)})
    grad_x = dh[None]
    landed = [rs_chip_exchange(chip_partials(0), name="rs_chips"), jnp.concatenate(got, axis=1)]
    bigs = [adamw(landed[l], my_pack("", l), my_pack("m_", l), my_pack("v_", l), name="adamw_sharded", tt=256) for l in range(DEPTH)]
    gs = _pack_rows(jnp.concatenate([jnp.stack(gsmall[name]).reshape(-1) for name in SMALL]), 8)
    gs = all_gather(gs, name="ag_small_grads")
    pks = lambda pre: _pack_rows(jnp.concatenate([a[pre + name].reshape(-1) for name in SMALL]), 8)
    sm = adamw(gs, pks(""), pks("m_"), pks("v_"), name="adamw_replicated", tt=gs.shape[1])
    out = {}
    for i, kind in enumerate(("grad_", "delta_", "new_m_", "new_v_")):
        layers = [_unpack_layer(bigs[l][i]) for l in range(DEPTH)]
        for name, tr, _ in BIG:
            arr = jnp.stack([layers[l][0][name] for l in range(DEPTH)])
            out[kind + name] = jnp.swapaxes(arr, 1, 2) if tr else arr
        for (name, _), arr in zip(TINY, _split_flat(layers[0][1], [w.shape for w in tiny])):
            out[kind + name] = arr
        for name, arr in zip(SMALL, _unpack(sm[i], [a[name].shape for name in SMALL])):
            out[kind + name] = arr
    return (loss, grad_x) + tuple(out[kind + name] for kind in ("grad_", "delta_", "new_m_", "new_v_") for name in WEIGHTS)


def kernel(x, mem, w_in, w_out, ssd_conv_w, ssd_conv_b, ssd_dt_bias, ssd_a_log, ssd_d, ssd_norm_w, s5_lam_re, s5_lam_im, s5_log_step, s5_b_re, s5_b_im, s5_c_re, s5_c_im, s5_d, s5_glu_w, s5_glu_b, rg_conv_w, rg_conv_b, rg_wa, rg_ba, rg_wx, rg_bx, rg_lambda, ln1_g, ln1_b, xa_wq, xa_wk, xa_wv, xa_wo, ln2_g, ln2_b, mlp_w1, mlp_w2, ln3_g, ln3_b, loss_target, m_w_in, m_w_out, m_ssd_conv_w, m_ssd_conv_b, m_ssd_dt_bias, m_ssd_a_log, m_ssd_d, m_ssd_norm_w, m_s5_lam_re, m_s5_lam_im, m_s5_log_step, m_s5_b_re, m_s5_b_im, m_s5_c_re, m_s5_c_im, m_s5_d, m_s5_glu_w, m_s5_glu_b, m_rg_conv_w, m_rg_conv_b, m_rg_wa, m_rg_ba, m_rg_wx, m_rg_bx, m_rg_lambda, m_ln1_g, m_ln1_b, m_xa_wq, m_xa_wk, m_xa_wv, m_xa_wo, m_ln2_g, m_ln2_b, m_mlp_w1, m_mlp_w2, m_ln3_g, m_ln3_b, v_w_in, v_w_out, v_ssd_conv_w, v_ssd_conv_b, v_ssd_dt_bias, v_ssd_a_log, v_ssd_d, v_ssd_norm_w, v_s5_lam_re, v_s5_lam_im, v_s5_log_step, v_s5_b_re, v_s5_b_im, v_s5_c_re, v_s5_c_im, v_s5_d, v_s5_glu_w, v_s5_glu_b, v_rg_conv_w, v_rg_conv_b, v_rg_wa, v_rg_ba, v_rg_wx, v_rg_bx, v_rg_lambda, v_ln1_g, v_ln1_b, v_xa_wq, v_xa_wk, v_xa_wv, v_xa_wo, v_ln2_g, v_ln2_b, v_mlp_w1, v_mlp_w2, v_ln3_g, v_ln3_b):
    return _step(dict(locals()))
```

```python
import math

import jax
import jax.numpy as jnp
from jax import lax
from jax.experimental import pallas as pl
from jax.experimental.pallas import tpu as pltpu

F32 = jnp.float32
BF16 = jnp.bfloat16

N_DEV = 8
D_MODEL = 1024
DEPTH = 2
SSD_WIDTH = 512
SSD_HEADS = 8
SSD_HEAD_DIM = 64
SSD_STATE = 128
SSD_CHUNK = 128
SSD_XBC = 1024
S5_WIDTH = 256
S5_GROUPS = 16
S5_STATE = 64
S5_NSTATE = S5_GROUPS * S5_STATE
RG_WIDTH = 256
RG_BLOCKS = 4
RG_BLOCK_DIM = 64
RG_C = 8.0
XA_HEADS = 4
XA_HEAD_DIM = 256
ALPHA = (2.0 * DEPTH) ** 0.25
LN_EPS = 1e-5
ADAM_LR, ADAM_B1, ADAM_B2, ADAM_EPS, ADAM_WD, ADAM_STEP = 0.001, 0.9, 0.999, 1e-08, 0.01, 10

P_XBC, P_Z, P_U, P_XR, P_G, P_DT = 0, 1024, 1536, 1792, 2048, 2304
D_INP = 2560
LANE = 128
VMEM_LIMIT = 56 * 1024 * 1024
ROW_TILE = 512

_NN = ((1,), (0,))
_NT = ((1,), (1,))
_TN = ((0,), (0,))


def _dot(a, b, dims=_NN):
    return lax.dot_general(a.astype(BF16), b.astype(BF16), (dims, ((), ())), preferred_element_type=F32)


def _split_bf16(x, parts):
    out, rem = [], x
    for _ in range(parts):
        piece = rem.astype(BF16)
        out.append(piece)
        rem = rem - piece.astype(F32)
    return out


def _dot_mask(a, b, dims=_NN, *, mask_left, parts):
    if mask_left:
        return sum(_dot(a, piece, dims) for piece in _split_bf16(b, parts))
    return sum(_dot(piece, b, dims) for piece in _split_bf16(a, parts))


def _sigmoid(x):
    return 1.0 / (1.0 + jnp.exp(-x))


def _silu(x):
    return x * _sigmoid(x)


def _dsilu(x):
    s = _sigmoid(x)
    return s * (1.0 + x * (1.0 - s))


_GK = math.sqrt(2.0 / math.pi)
_GC = 0.044715


def _gelu(x):
    return 0.5 * x * (1.0 + jnp.tanh(_GK * (x + _GC * x * x * x)))


def _dgelu(x):
    th = jnp.tanh(_GK * (x + _GC * x * x * x))
    return 0.5 * (1.0 + th) + 0.5 * x * (1.0 - th * th) * _GK * (1.0 + 3.0 * _GC * x * x)


def _log1p_pos(e):
    return jnp.where(e < 1e-2, e * (1.0 - e * (0.5 - e * (1.0 / 3.0))), jnp.log(1.0 + e))


def _softplus(x):
    return jnp.maximum(x, 0.0) + _log1p_pos(jnp.exp(-jnp.abs(x)))


def _neg_expm1(x):
    poly = -x * (1.0 + x * (0.5 + x * (1.0 / 6.0 + x * (1.0 / 24.0 + x * (1.0 / 120.0)))))
    return jnp.where(x > -0.05, poly, 1.0 - jnp.exp(x))


def _params(sem):
    return pltpu.CompilerParams(dimension_semantics=sem, vmem_limit_bytes=VMEM_LIMIT)


RESIDENT_BYTES = 8 * 1024 * 1024
STREAM_BYTES = 8 * 1024 * 1024


def _halve_to_fit(dims, bytes_per, limit):
    dims = list(dims)
    while math.prod(dims) * bytes_per > limit:
        i = max(range(len(dims)), key=lambda d: dims[d])
        assert dims[i] % 256 == 0, dims
        dims[i] //= 2
    return dims


def _side_exchange(side, src, dst, sems, step, nsteps):
    kind, _, r0, rows = side
    span = pl.ds(r0, rows)
    if kind == "gather":
        phases = lambda: _ag_phases(src.at[span], dst, *sems)
        when = (0, (3 * nsteps) // 4, nsteps - 1)
    else:
        phases = lambda: _rs_chip_phases(src, dst, *sems, rows=span)
        when = (0, nsteps - 1)
    for idx, at in enumerate(when):
        pl.when(step == at)(lambda idx=idx: phases()[idx]())


def mm(a, b, *, name, ta=False, tb=False, a_extra=(), fa=None, o_extra=(), r_extra=(), fo=None, n_out=1,
       a_off=0, m=None, k=None, out_dtype=F32, side=None):
    n = b.shape[0] if tb else b.shape[1]
    na, no, nr = 1 + len(a_extra), len(o_extra), len(r_extra)
    if not ta:
        assert m is None
        m, kdim = a.shape[0], (a.shape[1] if k is None else k)
        assert a_off % kdim == 0
        (tn,) = _halve_to_fit([n], kdim * b.dtype.itemsize, RESIDENT_BYTES)
        (tm,) = _halve_to_fit([min(512, m)], max(tn, kdim) * 4, STREAM_BYTES)
        a_spec = pl.BlockSpec((tm, kdim), lambda i, j: (i, a_off // kdim))
        b_spec = pl.BlockSpec((tn, kdim), lambda i, j: (j, 0)) if tb else pl.BlockSpec((kdim, tn), lambda i, j: (0, j))
        o_spec = pl.BlockSpec((tm, tn), lambda i, j: (i, j))
        dims = _NT if tb else _NN

        r_spec = pl.BlockSpec((1, tn), lambda i, j: (0, j))

        grid = (m // tm, n // tn)
        nin = na + 1 + no + nr

        def body(*refs):
            a_refs, b_ref, e_refs, out_refs = refs[:na], refs[na], refs[na + 1:nin], refs[nin + (side is not None):nin + (side is not None) + n_out]
            if side is not None:
                _side_exchange(side, refs[nin], refs[nin + 1 + n_out], refs[nin + 2 + n_out:],
                               pl.program_id(0) * grid[1] + pl.program_id(1), grid[0] * grid[1])
            av = a_refs[0][...] if fa is None else fa(*[r[...] for r in a_refs])
            acc = _dot(av, b_ref[...], dims)
            res = acc if fo is None else fo(acc, *[r[...] for r in e_refs])
            for r, v in zip(out_refs, res if n_out > 1 else (res,)):
                r[...] = v.astype(r.dtype)

        sem = ("parallel", "parallel") if side is None else ("arbitrary", "arbitrary")
    else:
        assert k is None and not tb and fo is None and not o_extra and not r_extra and n_out == 1 and out_dtype == F32
        assert side is None
        kdim, m = a.shape[0], (a.shape[1] if m is None else m)
        r_spec = None
        tm, tn = _halve_to_fit([m, n], 4, RESIDENT_BYTES)
        (tk,) = _halve_to_fit([min(512, kdim)], max(tm, tn) * 4, STREAM_BYTES)
        assert a_off % tm == 0
        a_spec = pl.BlockSpec((tk, tm), lambda i, j, kk: (kk, i + a_off // tm))
        b_spec = pl.BlockSpec((tk, tn), lambda i, j, kk: (kk, j))
        o_spec = pl.BlockSpec((tm, tn), lambda i, j, kk: (i, j))

        def body(*refs):
            a_refs, b_ref, out_ref = refs[:na], refs[na], refs[na + 1]

            @pl.when(pl.program_id(2) == 0)
            def _():
                out_ref[...] = jnp.zeros_like(out_ref)

            av = a_refs[0][...] if fa is None else fa(*[r[...] for r in a_refs])
            out_ref[...] += _dot(av, b_ref[...], _TN)

        grid, sem = (m // tm, n // tn, kdim // tk), ("parallel", "parallel", "arbitrary")
    assert m % tm == 0 and n % tn == 0, (name, m, n, tm, tn)
    out = jax.ShapeDtypeStruct((m, n), out_dtype)
    if side is None:
        return pl.pallas_call(
            body, name=name, grid=grid,
            in_specs=[a_spec] * na + [b_spec] + [o_spec] * no + [r_spec] * nr,
            out_specs=o_spec if n_out == 1 else [o_spec] * n_out, out_shape=out if n_out == 1 else [out] * n_out,
            compiler_params=_params(sem),
        )(a, *a_extra, b, *o_extra, *r_extra)
    kind, arr, _, rows = side
    landed = jax.ShapeDtypeStruct(((N_DEV, rows) if kind == "gather" else (4, rows)) + arr.shape[-1:], arr.dtype)
    return pl.pallas_call(
        body, name=name, grid=grid,
        in_specs=[a_spec] * na + [b_spec] + [o_spec] * no + [r_spec] * nr + [_ANY],
        out_specs=[o_spec] * n_out + [_ANY], out_shape=[out] * n_out + [landed],
        scratch_shapes=list(_AG_SEMS if kind == "gather" else _RS_SEMS),
        compiler_params=_params(sem),
    )(a, *a_extra, b, *o_extra, *r_extra, arr)


def rowk(fn, tiled, full, out_w, acc_shapes, *, rows, name, out_dtypes=None):
    tt = min(ROW_TILE, rows)
    n = rows // tt
    assert rows % tt == 0
    nt, nf, no = len(tiled), len(full), len(out_w)

    def tspec(w, cb):
        return pl.BlockSpec((tt, w), lambda i: (i, cb))

    def fspec(a):
        nd = a.ndim
        return pl.BlockSpec(a.shape, lambda i: (0,) * nd)

    def body(*refs):
        ins, fulls = refs[:nt], refs[nt:nt + nf]
        outs, accs = refs[nt + nf:nt + nf + no], refs[nt + nf + no:]
        res_t, res_a = fn(*[r[...] for r in ins], *[r[...] for r in fulls])
        for r, v in zip(outs, res_t):
            r[...] = v.astype(r.dtype)
        if accs:
            @pl.when(pl.program_id(0) == 0)
            def _():
                for r in accs:
                    r[...] = jnp.zeros_like(r)
            for r, v in zip(accs, res_a):
                r[...] += v

    outs = pl.pallas_call(
        body, name=name, grid=(n,),
        in_specs=[tspec(w, cb) for (_, w, cb) in tiled] + [fspec(a) for a in full],
        out_specs=[tspec(w, 0) for w in out_w] + [pl.BlockSpec(s, lambda i, nd=len(s): (0,) * nd) for s in acc_shapes],
        out_shape=[jax.ShapeDtypeStruct((rows, w), dt) for w, dt in zip(out_w, out_dtypes or [F32] * no)]
        + [jax.ShapeDtypeStruct(s, F32) for s in acc_shapes],
        compiler_params=_params(("arbitrary",)),
    )(*[a for (a, _, _) in tiled], *full)
    return outs[:no], outs[no:]


def _colsum(x):
    return jnp.sum(x, axis=0, keepdims=True)


def _rowsum(x):
    return jnp.sum(x, axis=1, keepdims=True)


def _ln_epilogue(acc, resid, g, b):
    pre = ALPHA * resid + acc
    mu = jnp.mean(pre, axis=1, keepdims=True)
    xc = pre - mu
    var = jnp.mean(xc * xc, axis=1, keepdims=True)
    return pre, xc * lax.rsqrt(var + LN_EPS) * g + b


def _ln_bwd_fn(pre, dout, g):
    mu = jnp.mean(pre, axis=1, keepdims=True)
    xc = pre - mu
    var = jnp.mean(xc * xc, axis=1, keepdims=True)
    rstd = lax.rsqrt(var + LN_EPS)
    xhat = xc * rstd
    dxh = dout * g
    dpre = rstd * (dxh - jnp.mean(dxh, axis=1, keepdims=True) - xhat * jnp.mean(dxh * xhat, axis=1, keepdims=True))
    return (dpre,), (_colsum(dout * xhat), _colsum(dout))


def mm_ln(a, w, resid, g, b, *, name, fa=None, side=None):
    assert w.shape[1] == D_MODEL
    return mm(a, w, fa=fa, o_extra=(resid,), r_extra=(g, b), fo=_ln_epilogue, n_out=2, name=name, side=side)


def ln_bwd(pre, dout, g, *, name):
    (dpre,), (dg, db) = rowk(_ln_bwd_fn, [(pre, D_MODEL, 0), (dout, D_MODEL, 0)], [g],
                             [D_MODEL], [(1, D_MODEL), (1, D_MODEL)], rows=pre.shape[0], name=name)
    return dpre, dg, db


def _loss_fn(y, tgt):
    e = y - tgt
    part = _colsum(_rowsum(e * e)) * (0.5 / D_MODEL)
    return (e * (1.0 / D_MODEL),), (part,)


_XA_SCALE = 1.0 / math.sqrt(XA_HEAD_DIM)


def _attn_probs(qh, kh):
    s = _dot(qh, kh, _NT) * _XA_SCALE
    e = jnp.exp(s - jnp.max(s, axis=1, keepdims=True))
    return e / _rowsum(e)


def _attn_fwd_fn(q, k, v):
    outs = []
    for hd in range(XA_HEADS):
        sl = slice(hd * XA_HEAD_DIM, (hd + 1) * XA_HEAD_DIM)
        outs.append(_dot(_attn_probs(q[:, sl], k[:, sl]), v[:, sl]))
    return (jnp.concatenate(outs, axis=1),), ()


def _attn_bwd_fn(q, do, k, v):
    dqs, dks, dvs = [], [], []
    for hd in range(XA_HEADS):
        sl = slice(hd * XA_HEAD_DIM, (hd + 1) * XA_HEAD_DIM)
        qh, kh, vh, doh = q[:, sl], k[:, sl], v[:, sl], do[:, sl]
        p = _attn_probs(qh, kh)
        dp = _dot(doh, vh, _NT)
        ds = p * (dp - _rowsum(p * dp)) * _XA_SCALE
        dqs.append(_dot(ds, kh))
        dks.append(_dot(ds, qh, _TN))
        dvs.append(_dot(p, doh, _TN))
    cat = lambda xs: jnp.concatenate(xs, axis=1)
    return (cat(dqs),), (cat(dks), cat(dvs))


def _s5_post_fwd_fn(ylin, u, dskip, gw, gb):
    yg = _gelu(ylin + dskip * u)
    return (yg * _sigmoid(_dot(yg, gw) + gb),), ()


def _s5_post_bwd_fn(ylin, u, dout, dskip, gw, gb):
    pre = ylin + dskip * u
    yg = _gelu(pre)
    sg = _sigmoid(_dot(yg, gw) + gb)
    dlin = dout * yg * sg * (1.0 - sg)
    dyg = dout * sg + _dot(dlin, gw, _NT)
    dpre = dyg * _dgelu(pre)
    return (dpre, dpre * dskip), (_colsum(dpre * u), _dot(yg, dlin, _TN), _colsum(dlin))


def _rg_gates(xc, wa, wx, ba, bx, lam):
    r = _sigmoid(_dot(xc, wa) + ba)
    i = _sigmoid(_dot(xc, wx) + bx)
    sp = _softplus(-lam)
    log_a = -RG_C * r * sp
    a = jnp.exp(log_a)
    mult = jnp.sqrt(_neg_expm1(2.0 * log_a))
    return r, i, sp, a, mult


def _rg_pre_bwd_fn(xc, gsc, hprev, wa, wx, ba, bx, lam):
    r, i, sp, a, mult = _rg_gates(xc, wa, wx, ba, bx, lam)
    da = gsc * hprev
    db = gsc
    dmult = db * i * xc
    di = db * mult * xc
    dxc = db * mult * i
    dlog_a = da * a - a * a * dmult / mult
    dr = dlog_a * (-RG_C * sp)
    dsp = _colsum(dlog_a * (-RG_C * r))
    dlam = dsp * (-_sigmoid(-lam))
    dpr = dr * r * (1.0 - r)
    dpi = di * i * (1.0 - i)
    dxc = dxc + _dot(dpr, wa, _NT) + _dot(dpi, wx, _NT)
    return (dxc,), (_dot(xc, dpr, _TN), _dot(xc, dpi, _TN), _colsum(dpr), _colsum(dpi), dlam)


def _conv_taps(x_ref, halo_ref, first):
    x = x_ref[...]
    halo = jnp.where(first, 0.0, halo_ref[...])
    rows8 = lax.broadcasted_iota(jnp.int32, halo.shape, 0)
    taps = [x]
    for j in (1, 2, 3):
        r = pltpu.roll(x, j, 0)
        top = jnp.where(rows8 < j, pltpu.roll(halo, j, 0), r[0:8])
        taps.append(jnp.concatenate([top, r[8:]], axis=0))
    return taps


def _conv_pre(taps, cw_ref, cb_ref):
    wv = cw_ref[...]
    pre = cb_ref[...] + wv[3:4, :] * taps[0]
    for j in (1, 2, 3):
        pre = pre + wv[3 - j:4 - j, :] * taps[j]
    return pre


def _conv_back(dpre, taps, cw_ref, nxt_ref):
    q = dpre.shape[0]
    rows8 = lax.broadcasted_iota(jnp.int32, (8, dpre.shape[1]), 0)
    wv = cw_ref[...]
    dx = wv[3:4, :] * dpre
    for j in (1, 2, 3):
        r = pltpu.roll(dpre, q - j, 0)
        bottom = jnp.where(rows8 >= 8 - j, pltpu.roll(nxt_ref[...], 8 - j, 0), r[q - 8:q])
        dx = dx + wv[3 - j:4 - j, :] * jnp.concatenate([r[:q - 8], bottom], axis=0)
    dw = jnp.concatenate([_colsum(dpre * taps[3 - kk]) for kk in range(4)], axis=0)
    nxt_ref[...] = dpre[0:8]
    return dx, dw, _colsum(dpre)


S5_CW = 256


def _cmul(ar, ai, br, bi):
    return ar * br - ai * bi, ar * bi + ai * br


def _scan8_complex(src_ref, dst_ref, lam_ref, st_ref, *, w, nb, reverse):
    rows = lax.broadcasted_iota(jnp.int32, (8, S5_CW), 0)
    b8 = lambda v: jnp.broadcast_to(v, (8, S5_CW))

    def shift(x, k):
        if reverse:
            return jnp.where(rows < 8 - k, pltpu.roll(x, 8 - k, 0), 0.0)
        return jnp.where(rows >= k, pltpu.roll(x, k, 0), 0.0)

    for c0 in range(0, w, S5_CW):
        re, im = pl.ds(c0, S5_CW), pl.ds(w + c0, S5_CW)
        pw = [(lam_ref[:, re], lam_ref[:, im])]
        for _ in range(7):
            pw.append(_cmul(*pw[-1], *pw[0]))
        pr, pi = b8(pw[7][0]), b8(pw[7][1])
        for j in range(7):
            sel = rows == (7 - j if reverse else j)
            pr, pi = jnp.where(sel, b8(pw[j][0]), pr), jnp.where(sel, b8(pw[j][1]), pi)
        steps = [(k, b8(pw[k - 1][0]), b8(pw[k - 1][1])) for k in (1, 2, 4)]
        edge = 0 if reverse else 7

        def blk(i, carry):
            hr, hi = carry
            base = pl.multiple_of((nb // 2 - 1 - i if reverse else i) * 16, 16)
            pend = []
            for off in ((8, 0) if reverse else (0, 8)):
                at = pl.ds(base + off, 8)
                xr, xi = src_ref[at, re], src_ref[at, im]
                for k, kr, ki in steps:
                    sr, si = shift(xr, k), shift(xi, k)
                    xr, xi = xr + kr * sr - ki * si, xi + kr * si + ki * sr
                pend.append((at, xr, xi))
            for at, xr, xi in pend:
                xr, xi = xr + pr * hr - pi * hi, xi + pr * hi + pi * hr
                dst_ref[at, re] = xr
                dst_ref[at, im] = xi
                hr, hi = b8(xr[edge:edge + 1, :]), b8(xi[edge:edge + 1, :])
            return hr, hi

        hr, hi = lax.fori_loop(0, nb // 2, blk, (st_ref[:, re], st_ref[:, im]))
        st_ref[:, re] = hr
        st_ref[:, im] = hi


def s5_fwd(proj, bcat, lam, ccat, dskip, gw, gb, *, name):
    t = proj.shape[0]
    tt = min(ROW_TILE, t)
    w2 = bcat.shape[1]

    def body(u_ref, b_ref, lam_ref, c_ref, d_ref, gw_ref, gb_ref, h_ref, y_ref, o_ref, bu_ref, st_ref):
        @pl.when(pl.program_id(0) == 0)
        def _():
            st_ref[...] = jnp.zeros_like(st_ref)

        u = u_ref[...]
        bu_ref[...] = _dot(u, b_ref[...])
        _scan8_complex(bu_ref, h_ref, lam_ref, st_ref, w=w2 // 2, nb=tt // 8, reverse=False)
        ylin = _dot(h_ref[...], c_ref[...])
        y_ref[...] = ylin
        (out,), _ = _s5_post_fwd_fn(ylin, u, d_ref[...], gw_ref[...], gb_ref[...])
        o_ref[...] = out.astype(o_ref.dtype)

    fixed = lambda a: pl.BlockSpec(a.shape, lambda i: (0, 0))
    row = pl.BlockSpec((tt, S5_WIDTH), lambda i: (i, 0))
    return pl.pallas_call(
        body, name=name, grid=(t // tt,),
        in_specs=[pl.BlockSpec((tt, S5_WIDTH), lambda i: (i, P_U // S5_WIDTH))] + [fixed(x) for x in (bcat, lam, ccat, dskip, gw, gb)],
        out_specs=[pl.BlockSpec((tt, w2), lambda i: (i, 0)), row, row],
        out_shape=[jax.ShapeDtypeStruct((t, w2), F32), jax.ShapeDtypeStruct((t, S5_WIDTH), F32),
                   jax.ShapeDtypeStruct((t, S5_WIDTH), BF16)],
        scratch_shapes=[pltpu.VMEM((tt, w2), F32), pltpu.VMEM((8, w2), F32)],
        compiler_params=_params(("arbitrary",)),
    )(proj, bcat, lam, ccat, dskip, gw, gb)


def s5_bwd(dycat, ylin, hs, proj, bcat, lam_adj, ccat, dskip, gw, gb, *, name):
    t = proj.shape[0]
    tt = min(ROW_TILE, t)
    n, w2 = t // tt, bcat.shape[1]
    w = w2 // 2

    def body(dout_ref, yl_ref, h_ref, hp_ref, u_ref, b_ref, lam_ref, c_ref, d_ref, gw_ref, gb_ref,
             du_ref, dc_ref, db_ref, dar_ref, dai_ref, dd_ref, dgw_ref, dgb_ref, g_ref, st_ref):
        i = pl.program_id(0)

        @pl.when(i == 0)
        def _():
            for r in (st_ref, dc_ref, db_ref, dar_ref, dai_ref, dd_ref, dgw_ref, dgb_ref):
                r[...] = jnp.zeros_like(r)

        (dy, du_a), post = _s5_post_bwd_fn(yl_ref[...], u_ref[...], dout_ref[...], d_ref[...], gw_ref[...], gb_ref[...])
        for r, v in zip((dd_ref, dgw_ref, dgb_ref), post):
            r[...] += v
        h = h_ref[...]
        g_ref[...] = _dot(dy, c_ref[...], _NT)
        dc_ref[...] += _dot(h, dy, _TN)
        _scan8_complex(g_ref, g_ref, lam_ref, st_ref, w=w, nb=tt // 8, reverse=True)
        g = g_ref[...]
        du_ref[...] = (du_a + _dot(g, b_ref[...], _NT)).astype(du_ref.dtype)
        db_ref[...] += _dot(u_ref[...], g, _TN)
        rows = lax.broadcasted_iota(jnp.int32, (tt, w2), 0)
        before = jnp.where(i == n - 1, 0.0, hp_ref[7:8, :])
        hprev = jnp.where(rows == 0, before, pltpu.roll(h, 1, 0))
        gr, gi, hr, hi = g[:, :w], g[:, w:], hprev[:, :w], hprev[:, w:]
        dar_ref[...] += _colsum(gr * hr + gi * hi)
        dai_ref[...] += _colsum(gi * hr - gr * hi)

    rev = lambda i: n - 1 - i
    row = lambda wd, cb=0: pl.BlockSpec((tt, wd), lambda i: (rev(i), cb))
    fixed = lambda shape: pl.BlockSpec(shape, lambda i: (0, 0))
    return pl.pallas_call(
        body, name=name, grid=(n,),
        in_specs=[row(S5_WIDTH, 2), row(S5_WIDTH), row(w2),
                  pl.BlockSpec((8, w2), lambda i: (jnp.maximum(rev(i) * (tt // 8) - 1, 0), 0)),
                  row(S5_WIDTH, P_U // S5_WIDTH)] + [fixed(x.shape) for x in (bcat, lam_adj, ccat, dskip, gw, gb)],
        out_specs=[row(S5_WIDTH), fixed(ccat.shape), fixed(bcat.shape), fixed((1, w)), fixed((1, w)),
                   fixed((1, S5_WIDTH)), fixed((S5_WIDTH, S5_WIDTH)), fixed((1, S5_WIDTH))],
        out_shape=[jax.ShapeDtypeStruct((t, S5_WIDTH), BF16), jax.ShapeDtypeStruct(ccat.shape, F32),
                   jax.ShapeDtypeStruct(bcat.shape, F32), jax.ShapeDtypeStruct((1, w), F32), jax.ShapeDtypeStruct((1, w), F32),
                   jax.ShapeDtypeStruct((1, S5_WIDTH), F32), jax.ShapeDtypeStruct((S5_WIDTH, S5_WIDTH), F32),
                   jax.ShapeDtypeStruct((1, S5_WIDTH), F32)],
        scratch_shapes=[pltpu.VMEM((tt, w2), F32), pltpu.VMEM((8, w2), F32)],
        compiler_params=_params(("arbitrary",)),
    )(dycat, ylin, hs, hs, proj, bcat, lam_adj, ccat, dskip, gw, gb)


def _scan8_real(a_ref, b_ref, o_ref, st_ref, *, nb, reverse):
    w = o_ref.shape[1]
    rows = lax.broadcasted_iota(jnp.int32, (8, w), 0)
    edge = 0 if reverse else 7

    def shift(x, k, fill):
        if reverse:
            return jnp.where(rows < 8 - k, pltpu.roll(x, 8 - k, 0), fill)
        return jnp.where(rows >= k, pltpu.roll(x, k, 0), fill)

    def blk(i, h):
        at = pl.ds(pl.multiple_of((nb - 1 - i if reverse else i) * 8, 8), 8)
        a, b = a_ref[at, :], b_ref[at, :]
        for k in (1, 2, 4):
            a, b = a * shift(a, k, 1.0), b + a * shift(b, k, 0.0)
        out = b + a * h
        o_ref[at, :] = out
        return jnp.broadcast_to(out[edge:edge + 1, :], (8, w))

    st_ref[...] = lax.fori_loop(0, nb, blk, st_ref[...])


def _rg_specs(tt, idx):
    return [pl.BlockSpec((tt, RG_WIDTH), lambda i: (idx(i), P_XR // RG_WIDTH)),
            pl.BlockSpec((8, RG_WIDTH), lambda i: (jnp.maximum(idx(i) * (tt // 8) - 1, 0), P_XR // RG_WIDTH)),
            pl.BlockSpec((tt, RG_WIDTH), lambda i: (idx(i), P_G // RG_WIDTH))]


def rg_fwd(proj, cw, cb, wa, wx, ba, bx, lam, *, name):
    t = proj.shape[0]
    tt = min(ROW_TILE, t)
    w = RG_WIDTH

    def body(x_ref, halo_ref, g_ref, cw_ref, cb_ref, wa_ref, wx_ref, ba_ref, bx_ref, lam_ref,
             y_ref, xc_ref, a_ref, h_ref, b_ref, st_ref):
        @pl.when(pl.program_id(0) == 0)
        def _():
            st_ref[...] = jnp.zeros_like(st_ref)

        xc = _conv_pre(_conv_taps(x_ref, halo_ref, pl.program_id(0) == 0), cw_ref, cb_ref)
        xc_ref[...] = xc
        r, i, sp, a, mult = _rg_gates(xc, wa_ref[...], wx_ref[...], ba_ref[...], bx_ref[...], lam_ref[...])
        a_ref[...] = a
        b_ref[...] = mult * (i * xc)
        _scan8_real(a_ref, b_ref, h_ref, st_ref, nb=tt // 8, reverse=False)
        y_ref[...] = (h_ref[...] * _gelu(g_ref[...])).astype(y_ref.dtype)

    fixed = lambda a: pl.BlockSpec(a.shape, lambda i: (0, 0))
    row = pl.BlockSpec((tt, w), lambda i: (i, 0))
    return pl.pallas_call(
        body, name=name, grid=(t // tt,),
        in_specs=_rg_specs(tt, lambda i: i) + [fixed(x) for x in (cw, cb, wa, wx, ba, bx, lam)],
        out_specs=[row] * 4,
        out_shape=[jax.ShapeDtypeStruct((t, w), BF16)] + [jax.ShapeDtypeStruct((t, w), F32)] * 3,
        scratch_shapes=[pltpu.VMEM((tt, w), F32), pltpu.VMEM((8, w), F32)],
        compiler_params=_params(("arbitrary",)),
    )(proj, proj, proj, cw, cb, wa, wx, ba, bx, lam)


def rg_bwd(proj, dycat, xc, a, h, cw, cb, wa, wx, ba, bx, lam, *, name):
    t = proj.shape[0]
    tt = min(ROW_TILE, t)
    n, w = t // tt, RG_WIDTH

    def body(x_ref, halo_ref, g_ref, dy_ref, xc_ref, a_ref, h_ref, hp_ref, cw_ref, wa_ref, wx_ref, ba_ref, bx_ref, lam_ref,
             dx_ref, dg_ref, dcw_ref, dcb_ref, dwa_ref, dwx_ref, dba_ref, dbx_ref, dlam_ref,
             au_ref, dh_ref, gs_ref, st_ref, anx_ref, nxt_ref):
        i = pl.program_id(0)
        accs = (dcw_ref, dcb_ref, dwa_ref, dwx_ref, dba_ref, dbx_ref, dlam_ref)

        @pl.when(i == 0)
        def _():
            for r in accs + (st_ref, anx_ref, nxt_ref):
                r[...] = jnp.zeros_like(r)

        h, g, dy, a = h_ref[...], g_ref[...], dy_ref[...], a_ref[...]
        dh_ref[...] = dy * _gelu(g)
        dg_ref[...] = (dy * h * _dgelu(g)).astype(dg_ref.dtype)
        rows = lax.broadcasted_iota(jnp.int32, (tt, w), 0)
        au_ref[...] = jnp.where(rows == tt - 1, anx_ref[0:1, :], pltpu.roll(a, tt - 1, 0))
        _scan8_real(au_ref, dh_ref, gs_ref, st_ref, nb=tt // 8, reverse=True)
        before = jnp.where(i == n - 1, 0.0, hp_ref[7:8, :])
        hprev = jnp.where(rows == 0, before, pltpu.roll(h, 1, 0))
        (dxc,), small = _rg_pre_bwd_fn(xc_ref[...], gs_ref[...], hprev, wa_ref[...], wx_ref[...], ba_ref[...], bx_ref[...], lam_ref[...])
        dx, dcw, dcb = _conv_back(dxc, _conv_taps(x_ref, halo_ref, i == n - 1), cw_ref, nxt_ref)
        dx_ref[...] = dx.astype(dx_ref.dtype)
        for r, v in zip(accs, (dcw, dcb) + tuple(small)):
            r[...] += v
        anx_ref[...] = a[0:8]

    rev = lambda i: n - 1 - i
    row = lambda cb_=0: pl.BlockSpec((tt, w), lambda i: (rev(i), cb_))
    fixed = lambda shape: pl.BlockSpec(shape, lambda i: (0, 0))
    acc_shapes = [(4, w), (1, w), (w, w), (w, w), (1, w), (1, w), (1, w)]
    return pl.pallas_call(
        body, name=name, grid=(n,),
        in_specs=_rg_specs(tt, rev) + [row(3), row(), row(), row(),
                                       pl.BlockSpec((8, w), lambda i: (jnp.maximum(rev(i) * (tt // 8) - 1, 0), 0))]
        + [fixed(x.shape) for x in (cw, wa, wx, ba, bx, lam)],
        out_specs=[row(), row()] + [fixed(sh) for sh in acc_shapes],
        out_shape=[jax.ShapeDtypeStruct((t, w), BF16)] * 2 + [jax.ShapeDtypeStruct(sh, F32) for sh in acc_shapes],
        scratch_shapes=[pltpu.VMEM((tt, w), F32)] * 3 + [pltpu.VMEM((8, w), F32)] * 3,
        compiler_params=_params(("arbitrary",)),
    )(proj, proj, proj, dycat, xc, a, h, h, cw, wa, wx, ba, bx, lam)


SSD_QQ = SSD_HEADS * SSD_CHUNK
SSD_GP = SSD_WIDTH // 2
SSD_GQ = SSD_QQ // 2


def _ssd_spread():
    h = jnp.arange(LANE)[:, None]
    spread_p = (jnp.arange(SSD_WIDTH)[None, :] // SSD_HEAD_DIM == h).astype(BF16)
    spread_q = (jnp.arange(SSD_QQ)[None, :] // SSD_CHUNK == h).astype(BF16)
    return spread_p, spread_q


def _ssd_prologue(dt_ref, prow_ref, sp_ref, sq_ref):
    q = SSD_CHUNK
    r = lax.broadcasted_iota(jnp.int32, (q, q), 0)
    c = lax.broadcasted_iota(jnp.int32, (q, q), 1)
    raw_c = dt_ref[...] + prow_ref[0:1, :]
    dt_c = _softplus(raw_c)
    a_r = -jnp.exp(prow_ref[1:2, :])
    cs_c = _dot_mask((r >= c).astype(F32), dt_c * a_r, mask_left=True, parts=3)
    both = _dot_mask(jnp.concatenate([dt_c, cs_c], axis=0), sp_ref[...], mask_left=False, parts=3)
    dt_x, cs_x = both[:q], both[q:]
    csx = _dot_mask(cs_c, sq_ref[...], mask_left=False, parts=3)
    rr = lax.broadcasted_iota(jnp.int32, (q, SSD_QQ), 0)
    ss = lax.broadcasted_iota(jnp.int32, (q, SSD_QQ), 1) & (q - 1)
    diag = rr == ss
    cs_row = _colsum(jnp.where(diag, csx, 0.0))
    lcat = jnp.exp(jnp.where(rr >= ss, csx - cs_row, -1e30))
    cl = cs_x[q - 1:q, :]
    return dict(raw_c=raw_c, dt_c=dt_c, a_r=a_r, dt_x=dt_x, cs_x=cs_x, lcat=lcat, diag=diag,
                ecs=jnp.exp(cs_x), wdec=jnp.exp(cl - cs_x), ecl=jnp.exp(cl), triu=(r <= c).astype(F32))


def _ssd_group(xbc_ref, g, lcat, xdt):
    ns, q = SSD_STATE, SSD_CHUNK
    bm = xbc_ref[:, pl.ds(SSD_WIDTH + g * ns, ns)]
    cm = xbc_ref[:, pl.ds(SSD_WIDTH + 2 * ns + g * ns, ns)]
    cb = _dot(cm, bm, _NT)
    lg = lcat[:, g * SSD_GQ:(g + 1) * SSD_GQ]
    wcat = jnp.concatenate([cb] * 4, axis=1) * lg
    head = lax.broadcasted_iota(jnp.int32, (1, SSD_GP), 1) // SSD_HEAD_DIM
    xg = xdt[:, g * SSD_GP:(g + 1) * SSD_GP]
    xbd = jnp.concatenate([jnp.where(head == j, xg, 0.0) for j in range(4)], axis=0)
    return bm, cm, lg, wcat, xbd, head


def _ssd_gate(yraw, z, nw):
    yg = yraw * _silu(z)
    r = lax.rsqrt(jnp.mean(yg * yg, axis=1, keepdims=True) + LN_EPS)
    return yg, r


def _ssd_specs(q, idx):
    return [pl.BlockSpec((q, SSD_XBC), lambda i: (idx(i), P_XBC // SSD_XBC)),
            pl.BlockSpec((8, SSD_XBC), lambda i: (jnp.maximum(idx(i) * (q // 8) - 1, 0), P_XBC // SSD_XBC)),
            pl.BlockSpec((q, SSD_WIDTH), lambda i: (idx(i), P_Z // SSD_WIDTH)),
            pl.BlockSpec((q, LANE), lambda i: (idx(i), P_DT // LANE)),
            pl.BlockSpec((4, SSD_XBC), lambda i: (0, 0)), pl.BlockSpec((1, SSD_XBC), lambda i: (0, 0)),
            pl.BlockSpec((8, LANE), lambda i: (0, 0)), pl.BlockSpec((1, SSD_WIDTH), lambda i: (0, 0)),
            pl.BlockSpec((1, SSD_WIDTH), lambda i: (0, 0)),
            pl.BlockSpec((LANE, SSD_WIDTH), lambda i: (0, 0)), pl.BlockSpec((LANE, SSD_QQ), lambda i: (0, 0))]


def ssd_fwd(proj, cw, cb, prow, d_x, nw, *, name):
    t = proj.shape[0]
    q, ns = SSD_CHUNK, SSD_STATE
    nc = t // q
    spread_p, spread_q = _ssd_spread()

    def body(x_ref, halo_ref, z_ref, dt_ref, cw_ref, cb_ref, prow_ref, dx_ref, nw_ref, sp_ref, sq_ref,
             y_ref, yraw_ref, sall_ref, s_ref, xbc_ref):
        @pl.when(pl.program_id(0) == 0)
        def _():
            s_ref[...] = jnp.zeros_like(s_ref)

        sall_ref[0] = s_ref[...]
        xbc_ref[...] = _silu(_conv_pre(_conv_taps(x_ref, halo_ref, pl.program_id(0) == 0), cw_ref, cb_ref))
        pr = _ssd_prologue(dt_ref, prow_ref, sp_ref, sq_ref)
        xs = xbc_ref[:, pl.ds(0, SSD_WIDTH)]
        xdt = xs * pr["dt_x"]
        xw = xdt * pr["wdec"]
        ys = []
        for g in range(2):
            gp = slice(g * SSD_GP, (g + 1) * SSD_GP)
            bm, cm, lg, wcat, xbd, head = _ssd_group(xbc_ref, g, pr["lcat"], xdt)
            st = s_ref[:, gp]
            ys.append(_dot(wcat, xbd) + pr["ecs"][:, gp] * _dot(cm, st) + xs[:, gp] * dx_ref[:, gp])
            s_ref[:, gp] = pr["ecl"][:, gp] * st + _dot(bm, xw[:, gp], _TN)
        yraw = jnp.concatenate(ys, axis=1)
        yraw_ref[...] = yraw
        yg, r = _ssd_gate(yraw, z_ref[...], nw_ref[...])
        y_ref[...] = (yg * r * nw_ref[...]).astype(y_ref.dtype)

    row = pl.BlockSpec((q, SSD_WIDTH), lambda i: (i, 0))
    return pl.pallas_call(
        body, name=name, grid=(nc,),
        in_specs=_ssd_specs(q, lambda i: i),
        out_specs=[row, row, pl.BlockSpec((1, ns, SSD_WIDTH), lambda i: (i, 0, 0))],
        out_shape=[jax.ShapeDtypeStruct((t, SSD_WIDTH), BF16), jax.ShapeDtypeStruct((t, SSD_WIDTH), F32),
                   jax.ShapeDtypeStruct((nc, ns, SSD_WIDTH), F32)],
        scratch_shapes=[pltpu.VMEM((ns, SSD_WIDTH), F32), pltpu.VMEM((q, SSD_XBC), F32)],
        compiler_params=_params(("arbitrary",)),
    )(proj, proj, proj, proj, cw, cb, prow, d_x, nw, spread_p, spread_q)


def ssd_bwd(proj, cw, cb, prow, d_x, nw, yraw, sall, dout, *, name):
    t = proj.shape[0]
    q, ns = SSD_CHUNK, SSD_STATE
    nc = t // q
    spread_p, spread_q = _ssd_spread()

    def body(x_ref, halo_ref, z_ref, dt_ref, cw_ref, cb_ref, prow_ref, dx_ref, nw_ref, sp_ref, sq_ref, yraw_ref, sall_ref, dout_ref,
             dxraw_ref, dz_ref, ddt_ref, dprm_ref, ddx_ref, dnw_ref, dcw_ref, dcb_ref, ds_ref, xbc_ref, dxbc_ref, nxt_ref):
        @pl.when(pl.program_id(0) == 0)
        def _():
            for r in (ds_ref, dprm_ref, ddx_ref, dnw_ref, dcw_ref, dcb_ref, nxt_ref):
                r[...] = jnp.zeros_like(r)

        taps = _conv_taps(x_ref, halo_ref, pl.program_id(0) == nc - 1)
        conv_pre = _conv_pre(taps, cw_ref, cb_ref)
        xbc_ref[...] = _silu(conv_pre)

        yraw, z, nwv, dout = yraw_ref[...], z_ref[...], nw_ref[...], dout_ref[...]
        yg, r = _ssd_gate(yraw, z, nwv)
        dnw_ref[...] += _colsum(dout * yg * r)
        dyn = dout * nwv
        dyg = r * dyn - yg * (r * r * r) * jnp.mean(dyn * yg, axis=1, keepdims=True)
        dy = dyg * _silu(z)
        dz_ref[...] = (dyg * yraw * _dsilu(z)).astype(dz_ref.dtype)

        pr = _ssd_prologue(dt_ref, prow_ref, sp_ref, sq_ref)
        xs = xbc_ref[:, pl.ds(0, SSD_WIDTH)]
        xdt = xs * pr["dt_x"]
        wdec, ecl = pr["wdec"], pr["ecl"]
        xw = xdt * wdec
        dzm_all = pr["ecs"] * dy
        last = (lax.broadcasted_iota(jnp.int32, (q, 1), 0) == q - 1).astype(F32)
        dxs, dcsxs, es = [], [], []
        for g in range(2):
            gp = slice(g * SSD_GP, (g + 1) * SSD_GP)
            bm, cm, lg, wcat, xbd, head = _ssd_group(xbc_ref, g, pr["lcat"], xdt)
            dyg_ = dy[:, gp]
            dwcat = _dot(dyg_, xbd, _NT)
            dxbd = _dot(wcat, dyg_, _TN)
            dxg = sum(jnp.where(head == j, dxbd[j * q:(j + 1) * q], 0.0) for j in range(4))
            es.append(dwcat * wcat)
            dmm = dwcat * lg
            dm = dmm[:, 0:q] + dmm[:, q:2 * q] + dmm[:, 2 * q:3 * q] + dmm[:, 3 * q:4 * q]
            dcm = _dot(dm, bm)
            dbm = _dot(dm, cm, _TN)
            st = sall_ref[0, :, gp]
            zmat = _dot(cm, st)
            dzm = dzm_all[:, gp]
            dcm = dcm + _dot(dzm, st, _NT)
            dst = _dot(cm, dzm, _TN)
            dcsx = dzm * zmat
            dsn = ds_ref[:, gp]
            dst = dst + ecl[:, gp] * dsn
            dclx = _colsum(dsn * st) * ecl[:, gp]
            dxw = _dot(bm, dsn)
            dbm = dbm + _dot(xw[:, gp], dsn, _NT)
            dxg = dxg + wdec[:, gp] * dxw
            tw = dxw * xdt[:, gp] * wdec[:, gp]
            dclx = dclx + _colsum(tw)
            dcsxs.append(dcsx - tw + last * dclx)
            ds_ref[:, gp] = dst
            dxs.append(dxg)
            dxbc_ref[:, pl.ds(SSD_WIDTH + g * ns, ns)] = dbm
            dxbc_ref[:, pl.ds(SSD_WIDTH + 2 * ns + g * ns, ns)] = dcm
        dx = jnp.concatenate(dxs, axis=1)
        dxbc_ref[:, pl.ds(0, SSD_WIDTH)] = dx * pr["dt_x"] + dy * dx_ref[...]
        ddx_ref[...] += _colsum(dy * xs)
        red = _dot_mask(jnp.concatenate([jnp.concatenate(dcsxs, axis=1), dx * xs], axis=0), sp_ref[...], _NT,
                        mask_left=False, parts=2)
        e_all = jnp.concatenate(es, axis=1)
        e_red = _dot_mask(e_all - jnp.where(pr["diag"], _colsum(e_all), 0.0), sq_ref[...], _NT, mask_left=False, parts=2)
        dadt = _dot_mask(pr["triu"], red[:q] + e_red, mask_left=True, parts=2)
        draw = (red[q:] + dadt * pr["a_r"]) * _sigmoid(pr["raw_c"])
        ddt_ref[...] = draw.astype(ddt_ref.dtype)
        zero = jnp.zeros((6, LANE), F32)
        dprm_ref[...] += jnp.concatenate([_colsum(draw), _colsum(dadt * pr["dt_c"]) * pr["a_r"], zero], axis=0)
        dxr, dcw, dcb = _conv_back(dxbc_ref[...] * _dsilu(conv_pre), taps, cw_ref, nxt_ref)
        dxraw_ref[...] = dxr.astype(dxraw_ref.dtype)
        dcw_ref[...] += dcw
        dcb_ref[...] += dcb

    rev = lambda i: nc - 1 - i
    row = lambda w: pl.BlockSpec((q, w), lambda i: (rev(i), 0))
    fixed = lambda shape: pl.BlockSpec(shape, lambda i: (0, 0))
    return pl.pallas_call(
        body, name=name, grid=(nc,),
        in_specs=_ssd_specs(q, rev) + [row(SSD_WIDTH), pl.BlockSpec((1, ns, SSD_WIDTH), lambda i: (rev(i), 0, 0)),
                                       row(SSD_WIDTH)],
        out_specs=[row(SSD_XBC), row(SSD_WIDTH), row(LANE), fixed((8, LANE)), fixed((1, SSD_WIDTH)), fixed((1, SSD_WIDTH)),
                   fixed((4, SSD_XBC)), fixed((1, SSD_XBC))],
        out_shape=[jax.ShapeDtypeStruct((t, SSD_XBC), BF16), jax.ShapeDtypeStruct((t, SSD_WIDTH), BF16),
                   jax.ShapeDtypeStruct((t, LANE), BF16), jax.ShapeDtypeStruct((8, LANE), F32),
                   jax.ShapeDtypeStruct((1, SSD_WIDTH), F32), jax.ShapeDtypeStruct((1, SSD_WIDTH), F32),
                   jax.ShapeDtypeStruct((4, SSD_XBC), F32), jax.ShapeDtypeStruct((1, SSD_XBC), F32)],
        scratch_shapes=[pltpu.VMEM((ns, SSD_WIDTH), F32), pltpu.VMEM((q, SSD_XBC), F32), pltpu.VMEM((q, SSD_XBC), F32),
                        pltpu.VMEM((8, SSD_XBC), F32)],
        compiler_params=_params(("arbitrary",)),
    )(proj, proj, proj, proj, cw, cb, prow, d_x, nw, spread_p, spread_q, yraw, sall, dout)


def _me():
    return lax.axis_index("x"), lax.axis_index("y"), lax.axis_index("c")


_ANY = pl.BlockSpec(memory_space=pl.ANY)
_MESH = pl.DeviceIdType.MESH


_AG_SEMS = [pltpu.SemaphoreType.DMA((7,)), pltpu.SemaphoreType.DMA((7,)), pltpu.SemaphoreType.DMA(())]
_RS_SEMS = [pltpu.SemaphoreType.DMA((3,)), pltpu.SemaphoreType.DMA((3,)), pltpu.SemaphoreType.DMA(())]


def _ag_phases(src, dst, send_sems, recv_sems, local_sem):
    x, y, c = _me()
    me, sibling = (x, y, c), (x, y, 1 - c)
    chips = [(1 - x, y), (x, 1 - y), (1 - x, 1 - y)]

    def slot(px, py, pc):
        return dst.at[4 * px + 2 * py + pc]

    def copy(kk, blk, to, from_src=False):
        return pltpu.make_async_remote_copy(
            src_ref=src if from_src else slot(*blk), dst_ref=slot(*blk),
            send_sem=send_sems.at[kk], recv_sem=recv_sems.at[kk], device_id=to, device_id_type=_MESH)

    mine = lambda: pltpu.make_async_copy(src, slot(*me), local_sem)
    first = lambda: [copy(0, me, sibling, True)] + [copy(1 + j, me, (*chip, c), True) for j, chip in enumerate(chips)]
    passed = lambda j: copy(4 + j, (*chips[j], c), sibling)

    def start():
        mine().start()
        for cp in first():
            cp.start()

    def forward():
        for j, chip in enumerate(chips):
            copy(1 + j, (*chip, c), me).wait_recv()
            passed(j).start()

    def finish():
        copy(0, sibling, me).wait_recv()
        for j, chip in enumerate(chips):
            copy(4 + j, (*chip, 1 - c), me).wait_recv()
        for cp in first() + [passed(j) for j in range(3)]:
            cp.wait_send()
        mine().wait()

    return start, forward, finish


def _rs_chip_phases(src, dst, send_sems, recv_sems, local_sem, rows=None):
    x, y, c = _me()
    q_me = 2 * x + y
    pick = (lambda q: src.at[q]) if rows is None else (lambda q: src.at[q, rows])
    local = lambda: pltpu.make_async_copy(pick(q_me), dst.at[q_me], local_sem)
    copies = lambda: [pltpu.make_async_remote_copy(src_ref=pick(2 * px + py), dst_ref=dst.at[q_me], send_sem=send_sems.at[j],
                                                   recv_sem=recv_sems.at[j], device_id=(px, py, c), device_id_type=_MESH)
                      for j, (px, py) in enumerate([(1 - x, y), (x, 1 - y), (1 - x, 1 - y)])]

    def start():
        local().start()
        for cp in copies():
            cp.start()

    def finish():
        for cp in copies():
            cp.wait()
        local().wait()

    return start, finish


def all_gather(block, *, name):
    def body(src, dst, send_sems, recv_sems, local_sem):
        for phase in _ag_phases(src, dst, send_sems, recv_sems, local_sem):
            phase()

    return pl.pallas_call(
        body, name=name, in_specs=[_ANY], out_specs=_ANY,
        out_shape=jax.ShapeDtypeStruct((N_DEV,) + block.shape, block.dtype), scratch_shapes=list(_AG_SEMS),
    )(block)


RS_PIECES = 4


def rs_sibling_exchange(halves, *, name):
    _, nq, r, l = halves.shape
    rows = r // RS_PIECES
    assert r % RS_PIECES == 0 and rows % 16 == 0

    def body(src, dst, send_sems, recv_sems):
        x, y, c = _me()
        copies = []
        for q in range(nq):
            for i in range(RS_PIECES):
                kk = q * RS_PIECES + i
                cp = pltpu.make_async_remote_copy(
                    src_ref=src.at[1 - c, q, pl.ds(i * rows, rows)], dst_ref=dst.at[q, pl.ds(i * rows, rows)],
                    send_sem=send_sems.at[kk], recv_sem=recv_sems.at[kk], device_id=(x, y, 1 - c), device_id_type=_MESH)
                cp.start()
                copies.append(cp)
        for cp in copies:
            cp.wait()

    n_copies = nq * RS_PIECES
    return pl.pallas_call(
        body, name=name, in_specs=[_ANY], out_specs=_ANY,
        out_shape=jax.ShapeDtypeStruct((nq, r, l), halves.dtype),
        scratch_shapes=[pltpu.SemaphoreType.DMA((n_copies,)), pltpu.SemaphoreType.DMA((n_copies,))],
    )(halves)


def pair_sum_bf16(halves, theirs, *, name, tt=512):
    _, nq, r, wd = halves.shape
    tt = min(tt, r)
    parity = lax.axis_index("c").astype(jnp.int32).reshape(1)

    def body(c_ref, own_ref, sib_ref, o_ref):
        o_ref[...] = (own_ref[...] + sib_ref[...]).astype(BF16)

    return pl.pallas_call(
        body, name=name,
        grid_spec=pltpu.PrefetchScalarGridSpec(
            num_scalar_prefetch=1, grid=(nq, r // tt),
            in_specs=[pl.BlockSpec((None, None, tt, wd), lambda q, i, c: (c[0], q, i, 0)),
                      pl.BlockSpec((None, tt, wd), lambda q, i, c: (q, i, 0))],
            out_specs=pl.BlockSpec((None, tt, wd), lambda q, i, c: (q, i, 0))),
        out_shape=jax.ShapeDtypeStruct((nq, r, wd), BF16),
        compiler_params=_params(("parallel", "parallel")),
    )(parity, halves, theirs)


def rs_chip_exchange(part, *, name):
    def body(src, dst, send_sems, recv_sems, local_sem):
        for phase in _rs_chip_phases(src, dst, send_sems, recv_sems, local_sem):
            phase()

    return pl.pallas_call(
        body, name=name, in_specs=[_ANY], out_specs=_ANY,
        out_shape=jax.ShapeDtypeStruct(part.shape, part.dtype), scratch_shapes=list(_RS_SEMS),
    )(part)


def adamw(slabs, w, m, v, *, name, tt):
    ns, (r, wd) = slabs.shape[0], w.shape
    tt = min(tt, r)
    assert r % tt == 0

    def body(s_ref, w_ref, m_ref, v_ref, g_ref, d_ref, nm_ref, nv_ref):
        g = s_ref[0].astype(F32)
        for kdev in range(1, ns):
            g = g + s_ref[kdev].astype(F32)
        wv = w_ref[...]
        nm = ADAM_B1 * m_ref[...] + (1.0 - ADAM_B1) * g
        nv = ADAM_B2 * v_ref[...] + (1.0 - ADAM_B2) * (g * g)
        m_hat = nm / (1.0 - ADAM_B1 ** ADAM_STEP)
        v_hat = nv / (1.0 - ADAM_B2 ** ADAM_STEP)
        g_ref[...] = g
        d_ref[...] = -ADAM_LR * (m_hat / (jnp.sqrt(v_hat) + ADAM_EPS) + ADAM_WD * wv)
        nm_ref[...] = nm
        nv_ref[...] = nv

    spec = pl.BlockSpec((tt, wd), lambda i: (i, 0))
    return pl.pallas_call(
        body, name=name, grid=(r // tt,),
        in_specs=[pl.BlockSpec((ns, tt, wd), lambda i: (0, i, 0)), spec, spec, spec],
        out_specs=[spec] * 4, out_shape=[jax.ShapeDtypeStruct((r, wd), F32)] * 4,
        compiler_params=_params(("parallel",)),
    )(slabs, w, m, v)


WIDE = 1024
BIG = [("w_in", True, 289), ("w_out", False, 128), ("xa_wq", False, 128), ("xa_wk", False, 128), ("xa_wv", False, 128),
       ("xa_wo", False, 128), ("mlp_w2", False, 512), ("mlp_w1", True, 512)]
TINY = [("ssd_conv_w", 2), ("s5_glu_w", 1), ("rg_conv_w", 2)]
KEEP_F32 = ("ssd_conv_w", "rg_conv_w")
TINY_ROWS = 32
SHARDED = [name for name, _, _ in BIG] + [name for name, _ in TINY]
SMALL = ["ssd_conv_b", "ssd_dt_bias", "ssd_a_log", "ssd_d", "ssd_norm_w", "s5_lam_re", "s5_lam_im",
         "s5_log_step", "s5_b_re", "s5_b_im", "s5_c_re", "s5_c_im", "s5_d", "s5_glu_b", "rg_conv_b",
         "rg_wa", "rg_ba", "rg_wx", "rg_bx", "rg_lambda", "ln1_g", "ln1_b", "ln2_g", "ln2_b", "ln3_g", "ln3_b"]
WEIGHTS = ['w_in', 'w_out', 'ssd_conv_w', 'ssd_conv_b', 'ssd_dt_bias', 'ssd_a_log', 'ssd_d', 'ssd_norm_w',
           's5_lam_re', 's5_lam_im', 's5_log_step', 's5_b_re', 's5_b_im', 's5_c_re', 's5_c_im', 's5_d',
           's5_glu_w', 's5_glu_b', 'rg_conv_w', 'rg_conv_b', 'rg_wa', 'rg_ba', 'rg_wx', 'rg_bx', 'rg_lambda',
           'ln1_g', 'ln1_b', 'xa_wq', 'xa_wk', 'xa_wv', 'xa_wo', 'ln2_g', 'ln2_b', 'mlp_w1', 'mlp_w2',
           'ln3_g', 'ln3_b']


def _pad16(rows):
    return -(-rows // 16) * 16


def _pack_rows(flat, mult):
    n = flat.shape[-1]
    r = -(-n // (LANE * mult)) * mult
    pad = [(0, 0)] * (flat.ndim - 1) + [(0, r * LANE - n)]
    return jnp.pad(flat, pad).reshape(flat.shape[:-1] + (r, LANE))


def _unpack(packed, shapes):
    lead = packed.shape[:-2]
    flat = packed.reshape(lead + (-1,))
    out, off = [], 0
    for s in shapes:
        n = math.prod(s)
        out.append(flat[..., off:off + n].reshape(lead + tuple(s)))
        off += n
    return out


PACK_ROWS = 2048


def _tiny_block(flat):
    pad = [(0, 0)] * (flat.ndim - 1) + [(0, TINY_ROWS * WIDE - flat.shape[-1])]
    return jnp.pad(flat, pad).reshape(flat.shape[:-1] + (TINY_ROWS, WIDE))


def _pack_layer(big, tiny_flat=None):
    blocks, used = [], 0
    some = big[BIG[0][0]]

    def zeros(rows):
        return jnp.zeros(some.shape[:-2] + (rows, WIDE), some.dtype)

    for name, _, rows in BIG:
        blocks.append(jnp.pad(big[name], [(0, 0)] * (some.ndim - 2) + [(0, _pad16(rows) - rows), (0, 0)]))
        used += _pad16(rows)
    if tiny_flat is not None:
        blocks.append(_tiny_block(tiny_flat))
        used += TINY_ROWS
    return jnp.concatenate(blocks + [zeros(PACK_ROWS - used)], axis=-2)


def _unpack_layer(packed):
    big, off = {}, 0
    for name, _, rows in BIG:
        big[name] = packed[..., off:off + rows, :]
        off += _pad16(rows)
    return big, packed[..., off:off + TINY_ROWS, :].reshape(packed.shape[:-2] + (TINY_ROWS * WIDE,))


def _split_flat(flat, shapes):
    out, off = [], 0
    for s in shapes:
        n = math.prod(s)
        out.append(flat[..., off:off + n].reshape(flat.shape[:-1] + tuple(s)))
        off += n
    return out


def _to_full(gathered, axis):
    g = jnp.moveaxis(gathered, 0, axis)
    s = g.shape
    return g.reshape(s[:axis] + (s[axis] * s[axis + 1],) + s[axis + 2:])


def _to_slabs(full, axis):
    s = full.shape
    g = full.reshape(s[:axis] + (N_DEV, s[axis] // N_DEV) + s[axis + 1:])
    return jnp.moveaxis(g, axis, 0)


def _blockdiag(w):
    h, i, j = w.shape
    eye = jnp.eye(h, dtype=w.dtype)
    return (w[:, :, None, :] * eye[:, None, :, None]).reshape(h * i, h * j)


def _blockdiag_extract(m, h):
    i, j = m.shape[0] // h, m.shape[1] // h
    eye = jnp.eye(h, dtype=m.dtype)
    return (m.reshape(h, i, h, j) * eye[:, None, :, None]).sum(axis=2)


def _s5_disc(lr, li, ls, bre, bim):
    step = jnp.exp(ls)[:, None]
    er = jnp.exp(lr * step)
    ar, ai = er * jnp.cos(li * step), er * jnp.sin(li * step)
    nr, ni, den = ar - 1.0, ai, lr * lr + li * li
    qr, qi = (nr * lr + ni * li) / den, (ni * lr - nr * li) / den
    bbr = qr[..., None] * bre - qi[..., None] * bim
    bbi = qr[..., None] * bim + qi[..., None] * bre
    return ar, ai, bbr, bbi


def _row(v, width=None):
    v = v.reshape(1, -1)
    if width is not None and v.shape[1] < width:
        v = jnp.pad(v, ((0, 0), (0, width - v.shape[1])))
    return v


def _relu2(a):
    r = jnp.maximum(a, 0.0)
    return r * r


def _add_alpha(acc, d):
    return acc + ALPHA * d


def _layer_params(full, small, l):
    p = {}
    w_in = full["w_in"]
    z, xbc, dt, u, xr, g = w_in[0:512], w_in[512:1536], w_in[1536:1544], w_in[1544:1800], w_in[1800:2056], w_in[2056:2312]
    p["w_inp"] = jnp.concatenate([xbc, z, u, xr, g, dt, jnp.zeros((D_INP - P_DT - 8, D_MODEL), w_in.dtype)], axis=0)
    for k_ in ("w_out", "xa_wq", "xa_wk", "xa_wv", "xa_wo", "mlp_w1", "mlp_w2"):
        p[k_] = full[k_]
    p["s5_glu_w"] = full["s5_glu_w"][l]
    p["ssd_cw"], p["ssd_cb"] = full["ssd_conv_w"][l], _row(small["ssd_conv_b"][l])
    dtb, alog, dsk = small["ssd_dt_bias"][l], small["ssd_a_log"][l], small["ssd_d"][l]
    p["prow"] = jnp.concatenate([_row(dtb, LANE), _row(alog, LANE), jnp.zeros((6, LANE), F32)], axis=0)
    p["ssd_dx"] = _row(jnp.repeat(dsk, SSD_HEAD_DIM))
    p["ssd_nw"] = _row(small["ssd_norm_w"][l])
    s5_in = (small["s5_lam_re"][l], small["s5_lam_im"][l], small["s5_log_step"][l], small["s5_b_re"][l], small["s5_b_im"][l])
    (ar, ai, bbr, bbi), p["s5_vjp"] = jax.vjp(_s5_disc, *s5_in)
    p["lam_fwd"] = jnp.concatenate([_row(ar), _row(ai)], axis=1)
    p["lam_adj"] = jnp.concatenate([_row(ar), _row(-ai)], axis=1)
    p["bcat"] = jnp.concatenate([_blockdiag(jnp.swapaxes(bbr, 1, 2)), _blockdiag(jnp.swapaxes(bbi, 1, 2))], axis=1)
    p["ccat"] = jnp.concatenate([_blockdiag(jnp.swapaxes(small["s5_c_re"][l], 1, 2)),
                                 -_blockdiag(jnp.swapaxes(small["s5_c_im"][l], 1, 2))], axis=0)
    p["s5_d"], p["s5_glu_b"] = _row(small["s5_d"][l]), _row(small["s5_glu_b"][l])
    p["rg_cw"], p["rg_cb"] = full["rg_conv_w"][l], _row(small["rg_conv_b"][l])
    p["rg_wa"], p["rg_wx"] = _blockdiag(small["rg_wa"][l]), _blockdiag(small["rg_wx"][l])
    p["rg_ba"], p["rg_bx"], p["rg_lam"] = _row(small["rg_ba"][l]), _row(small["rg_bx"][l]), _row(small["rg_lambda"][l])
    for i in (1, 2, 3):
        p[f"g{i}"], p[f"b{i}"] = _row(small[f"ln{i}_g"][l]), _row(small[f"ln{i}_b"][l])
    return p


def _take_side(res, n_out, got):
    res = res if isinstance(res, (list, tuple)) else (res,)
    got.extend(res[n_out:])
    return res[0] if n_out == 1 else res[:n_out]


def _layer_fwd(h0, mem, p, sides={}):
    t = h0.shape[0]
    s = {"h0": h0}
    got = []
    proj = _take_side(mm(h0, p["w_inp"], tb=True, name="in_proj", side=sides.get("in_proj")), 1, got)
    y_ssd, yraw, sall = ssd_fwd(proj, p["ssd_cw"], p["ssd_cb"], p["prow"], p["ssd_dx"], p["ssd_nw"], name="ssd_fwd")
    hs5, ylin, y_s5 = s5_fwd(proj, p["bcat"], p["lam_fwd"], p["ccat"], p["s5_d"], p["s5_glu_w"], p["s5_glu_b"], name="s5_fwd")
    rg_prm = (p["rg_cw"], p["rg_cb"], p["rg_wa"], p["rg_wx"], p["rg_ba"], p["rg_bx"], p["rg_lam"])
    y_rg, xc, a_rg, h_rg = rg_fwd(proj, *rg_prm, name="rg_fwd")
    ycat = jnp.concatenate([y_ssd, y_s5, y_rg], axis=1)
    pre1, h1 = mm_ln(ycat, p["w_out"], h0, p["g1"], p["b1"], name="out_proj")
    q = mm(h1, p["xa_wq"], name="xa_q", out_dtype=BF16)
    k = mm(mem, p["xa_wk"], name="xa_kv")
    v = mm(mem, p["xa_wv"], name="xa_kv")
    (o,), _ = rowk(_attn_fwd_fn, [(q, D_MODEL, 0)], [k, v], [D_MODEL], [], rows=t, name="xa_fwd", out_dtypes=[BF16])
    pre2, h2 = mm_ln(o, p["xa_wo"], h1, p["g2"], p["b2"], name="xa_o")
    a_mlp = _take_side(mm(h2, p["mlp_w1"], tb=True, name="mlp_up", side=sides.get("mlp_up")), 1, got)
    pre3, h3 = _take_side(mm_ln(a_mlp, p["mlp_w2"], h2, p["g3"], p["b3"], fa=_relu2, name="mlp_down",
                                side=sides.get("mlp_down")), 2, got)
    s.update(proj=proj, yraw=yraw, sall=sall, hs5=hs5, ylin=ylin, xc=xc, a_rg=a_rg, h_rg=h_rg,
             ycat=ycat, pre1=pre1, h1=h1, q=q, k=k, v=v, o=o, pre2=pre2, h2=h2, a_mlp=a_mlp, pre3=pre3)
    return h3, s, got


def _layer_bwd(dh3, mem, p, s, l, gfull, gsmall, sides={}):
    t = dh3.shape[0]
    proj = s["proj"]
    dpre3, dg3, db3 = ln_bwd(s["pre3"], dh3, p["g3"], name="ln_bwd")
    got = []
    da = _take_side(mm(dpre3, p["mlp_w2"], tb=True, o_extra=(s["a_mlp"],), fo=lambda acc, a: acc * 2.0 * jnp.maximum(a, 0.0),
                       name="mlp_da", out_dtype=BF16, side=sides.get("mlp_da")), 1, got)
    gfull["mlp_w2"][l] = mm(s["a_mlp"], dpre3, ta=True, fa=_relu2, name="mlp_dw2")
    gfull["mlp_w1"][l] = mm(da, s["h2"], ta=True, name="mlp_dw1")
    dh2 = _take_side(mm(da, p["mlp_w1"], o_extra=(dpre3,), fo=_add_alpha, name="mlp_dx", side=sides.get("mlp_dx")), 1, got)
    dpre2, dg2, db2 = ln_bwd(s["pre2"], dh2, p["g2"], name="ln_bwd")
    do = _take_side(mm(dpre2, p["xa_wo"], tb=True, name="xa_do", out_dtype=BF16, side=sides.get("xa_do")), 1, got)
    gfull["xa_wo"][l] = mm(s["o"], dpre2, ta=True, name="dw_sq")
    (dq,), (dk, dv) = rowk(_attn_bwd_fn, [(s["q"], D_MODEL, 0), (do, D_MODEL, 0)], [s["k"], s["v"]], [D_MODEL],
                           [(256, D_MODEL), (256, D_MODEL)], rows=t, name="xa_bwd", out_dtypes=[BF16])
    gfull["xa_wq"][l] = mm(s["h1"], dq, ta=True, name="dw_sq")
    gfull["xa_wk"][l] = mm(mem, dk, ta=True, name="dw_kv")
    gfull["xa_wv"][l] = mm(mem, dv, ta=True, name="dw_kv")
    dh1 = mm(dq, p["xa_wq"], tb=True, o_extra=(dpre2,), fo=_add_alpha, name="dx_sq")
    dpre1, dg1, db1 = ln_bwd(s["pre1"], dh1, p["g1"], name="ln_bwd")
    dycat = mm(dpre1, p["w_out"], tb=True, name="xa_do")
    gfull["w_out"][l] = mm(s["ycat"], dpre1, ta=True, name="dw_sq")
    rg_prm = (p["rg_cw"], p["rg_cb"], p["rg_wa"], p["rg_wx"], p["rg_ba"], p["rg_bx"], p["rg_lam"])
    dxr, dg_rg, d_rgcw, d_rgcb, dwa, dwx, dba, dbx, dlam = rg_bwd(proj, dycat, s["xc"], s["a_rg"], s["h_rg"], *rg_prm, name="rg_bwd")
    du, dccat, dbcat, dar, dai, d_s5d, d_gluw, d_glub = s5_bwd(dycat, s["ylin"], s["hs5"], proj, p["bcat"], p["lam_adj"], p["ccat"],
                                                               p["s5_d"], p["s5_glu_w"], p["s5_glu_b"], name="s5_bwd")
    dxbc, dz, ddt, dprm, ddx, dnw, d_scw, d_scb = ssd_bwd(proj, p["ssd_cw"], p["ssd_cb"], p["prow"], p["ssd_dx"], p["ssd_nw"],
                                                         s["yraw"], s["sall"], dycat, name="ssd_bwd")
    dproj = jnp.concatenate([dxbc, dz, du, dxr, dg_rg, ddt, jnp.zeros((t, D_INP - P_DT - LANE), BF16)], axis=1)
    dh0 = mm(dproj, p["w_inp"], o_extra=(dpre1,), fo=_add_alpha, name="in_proj_dx")
    dwp = mm(dproj, s["h0"], ta=True, name="in_proj_dw")
    gfull["w_in"][l] = jnp.concatenate([dwp[P_Z:P_Z + 512], dwp[P_XBC:P_XBC + 1024], dwp[P_DT:P_DT + 8],
                                        dwp[P_U:P_U + 256], dwp[P_XR:P_XR + 256], dwp[P_G:P_G + 256]], axis=0)
    gfull["ssd_conv_w"][l], gfull["rg_conv_w"][l], gfull["s5_glu_w"][l] = d_scw, d_rgcw, d_gluw
    ng, ns = S5_GROUPS, S5_STATE
    dbbr = jnp.swapaxes(_blockdiag_extract(dbcat[:, :S5_NSTATE], ng), 1, 2)
    dbbi = jnp.swapaxes(_blockdiag_extract(dbcat[:, S5_NSTATE:], ng), 1, 2)
    d_lr, d_li, d_ls, d_bre, d_bim = p["s5_vjp"]((dar.reshape(ng, ns), dai.reshape(ng, ns), dbbr, dbbi))
    gsmall["s5_lam_re"][l], gsmall["s5_lam_im"][l], gsmall["s5_log_step"][l] = d_lr, d_li, d_ls
    gsmall["s5_b_re"][l], gsmall["s5_b_im"][l] = d_bre, d_bim
    gsmall["s5_c_re"][l] = jnp.swapaxes(_blockdiag_extract(dccat[:S5_NSTATE], ng), 1, 2)
    gsmall["s5_c_im"][l] = -jnp.swapaxes(_blockdiag_extract(dccat[S5_NSTATE:], ng), 1, 2)
    gsmall["s5_d"][l], gsmall["s5_glu_b"][l] = d_s5d[0], d_glub[0]
    gsmall["ssd_conv_b"][l], gsmall["rg_conv_b"][l] = d_scb[0], d_rgcb[0]
    gsmall["ssd_dt_bias"][l], gsmall["ssd_a_log"][l] = dprm[0, :8], dprm[1, :8]
    gsmall["ssd_d"][l] = ddx.reshape(SSD_HEADS, SSD_HEAD_DIM).sum(axis=1)
    gsmall["ssd_norm_w"][l] = dnw[0]
    gsmall["rg_wa"][l], gsmall["rg_wx"][l] = _blockdiag_extract(dwa, RG_BLOCKS), _blockdiag_extract(dwx, RG_BLOCKS)
    gsmall["rg_ba"][l], gsmall["rg_bx"][l] = dba.reshape(RG_BLOCKS, RG_BLOCK_DIM), dbx.reshape(RG_BLOCKS, RG_BLOCK_DIM)
    gsmall["rg_lambda"][l] = dlam[0]
    for i, (dg, db) in zip((1, 2, 3), ((dg1, db1), (dg2, db2), (dg3, db3))):
        gsmall[f"ln{i}_g"][l], gsmall[f"ln{i}_b"][l] = dg[0], db[0]
    return dh0, got


def _step(a):
    h = a["x"][0]
    mem = a["mem"][0]
    t = h.shape[0]
    r4, r3 = PACK_ROWS // 4, 3 * PACK_ROWS // 8

    def my_shards(pre):
        return ({name: (jnp.swapaxes(a[pre + name], 1, 2) if tr else a[pre + name]) for name, tr, _ in BIG},
                [a[pre + name] for name, _ in TINY])

    def my_pack(pre, l):
        big, tiny = my_shards(pre)
        return _pack_layer({name: w[l] for name, w in big.items()},
                           jnp.concatenate([w.reshape(-1) for w in tiny]) if l == 0 else None)

    big, tiny = my_shards("")
    tiny16 = [(lax.bitcast_convert_type(w, BF16) if name in KEEP_F32 else w.astype(BF16)).reshape(-1)
              for (name, _), w in zip(TINY, tiny)]
    packed = [_pack_layer({name: w[l].astype(BF16) for name, w in big.items()}, jnp.concatenate(tiny16) if l == 0 else None)
              for l in range(DEPTH)]
    small = {name: a[name] for name in SMALL}

    def gathered_weights(g):
        gbig, gtiny = _unpack_layer(g)
        return {name: w.reshape(-1, WIDE) for name, w in gbig.items()}, gtiny

    full, gtiny = gathered_weights(all_gather(packed[0], name="ag_weights"))
    tiny_shapes = [w.shape + ((2,) if name in KEEP_F32 else ()) for (name, _), w in zip(TINY, tiny)]
    tiny_full = {name: _to_full(lax.bitcast_convert_type(g, F32) if name in KEEP_F32 else g, axis)
                 for (name, axis), g in zip(TINY, _split_flat(gtiny, tiny_shapes))}
    p0 = _layer_params({**full, **tiny_full}, small, 0)
    h, s0, got = _layer_fwd(h, mem, p0, sides={"in_proj": ("gather", packed[1], 0, r4), "mlp_up": ("gather", packed[1], r4, r3),
                                                "mlp_down": ("gather", packed[1], r4 + r3, r3)})
    full, _ = gathered_weights(jnp.concatenate(got, axis=1))
    p1 = _layer_params({**full, **tiny_full}, small, 1)
    h, s1, _ = _layer_fwd(h, mem, p1)
    (dh,), (loss_part,) = rowk(_loss_fn, [(h, D_MODEL, 0), (a["loss_target"][0], D_MODEL, 0)], [], [D_MODEL], [(1, 1)],
                               rows=t, name="loss_head")
    loss = lax.psum(loss_part[0, 0], ("x", "y", "c"))
    gfull = {name: [None] * DEPTH for name in SHARDED}
    gsmall = {name: [None] * DEPTH for name in SMALL}

    def chip_partials(l):
        gbig = {name: gfull[name][l].reshape(N_DEV, rows, WIDE) for name, _, rows in BIG}
        gtiny = None
        if l == 0:
            gtiny = jnp.concatenate([_to_slabs(jnp.stack(gfull[name]), axis).reshape(N_DEV, -1) for name, axis in TINY], axis=1)
        slabs = _pack_layer(gbig, gtiny)
        halves = jnp.swapaxes(slabs.reshape((4, 2) + slabs.shape[1:]), 0, 1)
        theirs = rs_sibling_exchange(halves, name="rs_sibling")
        return pair_sum_bf16(halves, theirs, name="rs_pair_sum")

    dh, _ = _layer_bwd(dh, mem, p1, s1, 1, gfull, gsmall)
    part1 = chip_partials(1)
    dh, got = _layer_bwd(dh, mem, p0, s0, 0, gfull, gsmall, sides={"mlp_da": ("chips", part1, 0, r3), "mlp_dx": ("chips", part1, r3, r3),
                                                                    "xa_do": ("chips", part1, 2 * r3, r4)})
    grad_x = dh[None]
    landed = [rs_chip_exchange(chip_partials(0), name="rs_chips"), jnp.concatenate(got, axis=1)]
    bigs = [adamw(landed[l], my_pack("", l), my_pack("m_", l), my_pack("v_", l), name="adamw_sharded", tt=256) for l in range(DEPTH)]
    gs = _pack_rows(jnp.concatenate([jnp.stack(gsmall[name]).reshape(-1) for name in SMALL]), 8)
    gs = all_gather(gs, name="ag_small_grads")
    pks = lambda pre: _pack_rows(jnp.concatenate([a[pre + name].reshape(-1) for name in SMALL]), 8)
    sm = adamw(gs, pks(""), pks("m_"), pks("v_"), name="adamw_replicated", tt=gs.shape[1])
    out = {}
    for i, kind in enumerate(("grad_", "delta_", "new_m_", "new_v_")):
        layers = [_unpack_layer(bigs[l][i]) for l in range(DEPTH)]
        for name, tr, _ in BIG:
            arr = jnp.stack([layers[l][0][name] for l in range(DEPTH)])
            out[kind + name] = jnp.swapaxes(arr, 1, 2) if tr else arr
        for (name, _), arr in zip(TINY, _split_flat(layers[0][1], [w.shape for w in tiny])):
            out[kind + name] = arr
        for name, arr in zip(SMALL, _unpack(sm[i], [a[name].shape for name in SMALL])):
            out[kind + name] = arr
    return (loss, grad_x) + tuple(out[kind + name] for kind in ("grad_", "delta_", "new_m_", "new_v_") for name in WEIGHTS)


def kernel(x, mem, w_in, w_out, ssd_conv_w, ssd_conv_b, ssd_dt_bias, ssd_a_log, ssd_d, ssd_norm_w, s5_lam_re, s5_lam_im, s5_log_step, s5_b_re, s5_b_im, s5_c_re, s5_c_im, s5_d, s5_glu_w, s5_glu_b, rg_conv_w, rg_conv_b, rg_wa, rg_ba, rg_wx, rg_bx, rg_lambda, ln1_g, ln1_b, xa_wq, xa_wk, xa_wv, xa_wo, ln2_g, ln2_b, mlp_w1, mlp_w2, ln3_g, ln3_b, loss_target, m_w_in, m_w_out, m_ssd_conv_w, m_ssd_conv_b, m_ssd_dt_bias, m_ssd_a_log, m_ssd_d, m_ssd_norm_w, m_s5_lam_re, m_s5_lam_im, m_s5_log_step, m_s5_b_re, m_s5_b_im, m_s5_c_re, m_s5_c_im, m_s5_d, m_s5_glu_w, m_s5_glu_b, m_rg_conv_w, m_rg_conv_b, m_rg_wa, m_rg_ba, m_rg_wx, m_rg_bx, m_rg_lambda, m_ln1_g, m_ln1_b, m_xa_wq, m_xa_wk, m_xa_wv, m_xa_wo, m_ln2_g, m_ln2_b, m_mlp_w1, m_mlp_w2, m_ln3_g, m_ln3_b, v_w_in, v_w_out, v_ssd_conv_w, v_ssd_conv_b, v_ssd_dt_bias, v_ssd_a_log, v_ssd_d, v_ssd_norm_w, v_s5_lam_re, v_s5_lam_im, v_s5_log_step, v_s5_b_re, v_s5_b_im, v_s5_c_re, v_s5_c_im, v_s5_d, v_s5_glu_w, v_s5_glu_b, v_rg_conv_w, v_rg_conv_b, v_rg_wa, v_rg_ba, v_rg_wx, v_rg_bx, v_rg_lambda, v_ln1_g, v_ln1_b, v_xa_wq, v_xa_wk, v_xa_wv, v_xa_wo, v_ln2_g, v_ln2_b, v_mlp_w1, v_mlp_w2, v_ln3_g, v_ln3_b):
    return _step(dict(locals()))
```

```python
import math

import jax
import jax.numpy as jnp
from jax import lax
from jax.experimental import pallas as pl
from jax.experimental.pallas import tpu as pltpu

F32 = jnp.float32
BF16 = jnp.bfloat16

N_DEV = 8
D_MODEL = 1024
DEPTH = 2
SSD_WIDTH = 512
SSD_HEADS = 8
SSD_HEAD_DIM = 64
SSD_STATE = 128
SSD_CHUNK = 128
SSD_XBC = 1024
S5_WIDTH = 256
S5_GROUPS = 16
S5_STATE = 64
S5_NSTATE = S5_GROUPS * S5_STATE
RG_WIDTH = 256
RG_BLOCKS = 4
RG_BLOCK_DIM = 64
RG_C = 8.0
XA_HEADS = 4
XA_HEAD_DIM = 256
ALPHA = (2.0 * DEPTH) ** 0.25
LN_EPS = 1e-5
ADAM_LR, ADAM_B1, ADAM_B2, ADAM_EPS, ADAM_WD, ADAM_STEP = 0.001, 0.9, 0.999, 1e-08, 0.01, 10

P_XBC, P_Z, P_U, P_XR, P_G, P_DT = 0, 1024, 1536, 1792, 2048, 2304
D_INP = 2560
LANE = 128
VMEM_LIMIT = 56 * 1024 * 1024
ROW_TILE = 512

_NN = ((1,), (0,))
_NT = ((1,), (1,))
_TN = ((0,), (0,))


def _dot(a, b, dims=_NN):
    return lax.dot_general(a.astype(BF16), b.astype(BF16), (dims, ((), ())), preferred_element_type=F32)


def _split_bf16(x, parts):
    out, rem = [], x
    for _ in range(parts):
        piece = rem.astype(BF16)
        out.append(piece)
        rem = rem - piece.astype(F32)
    return out


def _dot_mask(a, b, dims=_NN, *, mask_left, parts):
    if mask_left:
        return sum(_dot(a, piece, dims) for piece in _split_bf16(b, parts))
    return sum(_dot(piece, b, dims) for piece in _split_bf16(a, parts))


def _sigmoid(x):
    return 1.0 / (1.0 + jnp.exp(-x))


def _silu(x):
    return x * _sigmoid(x)


def _dsilu(x):
    s = _sigmoid(x)
    return s * (1.0 + x * (1.0 - s))


_GK = math.sqrt(2.0 / math.pi)
_GC = 0.044715


def _gelu(x):
    return 0.5 * x * (1.0 + jnp.tanh(_GK * (x + _GC * x * x * x)))


def _dgelu(x):
    th = jnp.tanh(_GK * (x + _GC * x * x * x))
    return 0.5 * (1.0 + th) + 0.5 * x * (1.0 - th * th) * _GK * (1.0 + 3.0 * _GC * x * x)


def _log1p_pos(e):
    return jnp.where(e < 1e-2, e * (1.0 - e * (0.5 - e * (1.0 / 3.0))), jnp.log(1.0 + e))


def _softplus(x):
    return jnp.maximum(x, 0.0) + _log1p_pos(jnp.exp(-jnp.abs(x)))


def _neg_expm1(x):
    poly = -x * (1.0 + x * (0.5 + x * (1.0 / 6.0 + x * (1.0 / 24.0 + x * (1.0 / 120.0)))))
    return jnp.where(x > -0.05, poly, 1.0 - jnp.exp(x))


def _params(sem):
    return pltpu.CompilerParams(dimension_semantics=sem, vmem_limit_bytes=VMEM_LIMIT)


RESIDENT_BYTES = 8 * 1024 * 1024
STREAM_BYTES = 8 * 1024 * 1024


def _halve_to_fit(dims, bytes_per, limit):
    dims = list(dims)
    while math.prod(dims) * bytes_per > limit:
        i = max(range(len(dims)), key=lambda d: dims[d])
        assert dims[i] % 256 == 0, dims
        dims[i] //= 2
    return dims


def _side_exchange(side, src, dst, sems, step, nsteps):
    kind, _, r0, rows = side
    span = pl.ds(r0, rows)
    if kind == "gather":
        phases = lambda: _ag_phases(src.at[span], dst, *sems)
        when = (0, (3 * nsteps) // 4, nsteps - 1)
    else:
        phases = lambda: _rs_chip_phases(src, dst, *sems, rows=span)
        when = (0, nsteps - 1)
    for idx, at in enumerate(when):
        pl.when(step == at)(lambda idx=idx: phases()[idx]())


def mm(a, b, *, name, ta=False, tb=False, a_extra=(), fa=None, o_extra=(), r_extra=(), fo=None, n_out=1,
       a_off=0, m=None, k=None, out_dtype=F32, side=None):
    n = b.shape[0] if tb else b.shape[1]
    na, no, nr = 1 + len(a_extra), len(o_extra), len(r_extra)
    if not ta:
        assert m is None
        m, kdim = a.shape[0], (a.shape[1] if k is None else k)
        assert a_off % kdim == 0
        (tn,) = _halve_to_fit([n], kdim * b.dtype.itemsize, RESIDENT_BYTES)
        (tm,) = _halve_to_fit([min(512, m)], max(tn, kdim) * 4, STREAM_BYTES)
        a_spec = pl.BlockSpec((tm, kdim), lambda i, j: (i, a_off // kdim))
        b_spec = pl.BlockSpec((tn, kdim), lambda i, j: (j, 0)) if tb else pl.BlockSpec((kdim, tn), lambda i, j: (0, j))
        o_spec = pl.BlockSpec((tm, tn), lambda i, j: (i, j))
        dims = _NT if tb else _NN

        r_spec = pl.BlockSpec((1, tn), lambda i, j: (0, j))

        grid = (m // tm, n // tn)
        nin = na + 1 + no + nr

        def body(*refs):
            a_refs, b_ref, e_refs, out_refs = refs[:na], refs[na], refs[na + 1:nin], refs[nin + (side is not None):nin + (side is not None) + n_out]
            if side is not None:
                _side_exchange(side, refs[nin], refs[nin + 1 + n_out], refs[nin + 2 + n_out:],
                               pl.program_id(0) * grid[1] + pl.program_id(1), grid[0] * grid[1])
            av = a_refs[0][...] if fa is None else fa(*[r[...] for r in a_refs])
            acc = _dot(av, b_ref[...], dims)
            res = acc if fo is None else fo(acc, *[r[...] for r in e_refs])
            for r, v in zip(out_refs, res if n_out > 1 else (res,)):
                r[...] = v.astype(r.dtype)

        sem = ("parallel", "parallel") if side is None else ("arbitrary", "arbitrary")
    else:
        assert k is None and not tb and fo is None and not o_extra and not r_extra and n_out == 1 and out_dtype == F32
        assert side is None
        kdim, m = a.shape[0], (a.shape[1] if m is None else m)
        r_spec = None
        tm, tn = _halve_to_fit([m, n], 4, RESIDENT_BYTES)
        (tk,) = _halve_to_fit([min(512, kdim)], max(tm, tn) * 4, STREAM_BYTES)
        assert a_off % tm == 0
        a_spec = pl.BlockSpec((tk, tm), lambda i, j, kk: (kk, i + a_off // tm))
        b_spec = pl.BlockSpec((tk, tn), lambda i, j, kk: (kk, j))
        o_spec = pl.BlockSpec((tm, tn), lambda i, j, kk: (i, j))

        def body(*refs):
            a_refs, b_ref, out_ref = refs[:na], refs[na], refs[na + 1]

            @pl.when(pl.program_id(2) == 0)
            def _():
                out_ref[...] = jnp.zeros_like(out_ref)

            av = a_refs[0][...] if fa is None else fa(*[r[...] for r in a_refs])
            out_ref[...] += _dot(av, b_ref[...], _TN)

        grid, sem = (m // tm, n // tn, kdim // tk), ("parallel", "parallel", "arbitrary")
    assert m % tm == 0 and n % tn == 0, (name, m, n, tm, tn)
    out = jax.ShapeDtypeStruct((m, n), out_dtype)
    if side is None:
        return pl.pallas_call(
            body, name=name, grid=grid,
            in_specs=[a_spec] * na + [b_spec] + [o_spec] * no + [r_spec] * nr,
            out_specs=o_spec if n_out == 1 else [o_spec] * n_out, out_shape=out if n_out == 1 else [out] * n_out,
            compiler_params=_params(sem),
        )(a, *a_extra, b, *o_extra, *r_extra)
    kind, arr, _, rows = side
    landed = jax.ShapeDtypeStruct(((N_DEV, rows) if kind == "gather" else (4, rows)) + arr.shape[-1:], arr.dtype)
    return pl.pallas_call(
        body, name=name, grid=grid,
        in_specs=[a_spec] * na + [b_spec] + [o_spec] * no + [r_spec] * nr + [_ANY],
        out_specs=[o_spec] * n_out + [_ANY], out_shape=[out] * n_out + [landed],
        scratch_shapes=list(_AG_SEMS if kind == "gather" else _RS_SEMS),
        compiler_params=_params(sem),
    )(a, *a_extra, b, *o_extra, *r_extra, arr)


def rowk(fn, tiled, full, out_w, acc_shapes, *, rows, name, out_dtypes=None, tt=ROW_TILE):
    tt = min(tt, rows)
    n = rows // tt
    assert rows % tt == 0
    nt, nf, no = len(tiled), len(full), len(out_w)

    def tspec(w, cb):
        return pl.BlockSpec((tt, w), lambda i: (i, cb))

    def fspec(a):
        nd = a.ndim
        return pl.BlockSpec(a.shape, lambda i: (0,) * nd)

    def body(*refs):
        ins, fulls = refs[:nt], refs[nt:nt + nf]
        outs, accs = refs[nt + nf:nt + nf + no], refs[nt + nf + no:]
        res_t, res_a = fn(*[r[...] for r in ins], *[r[...] for r in fulls])
        for r, v in zip(outs, res_t):
            r[...] = v.astype(r.dtype)
        if accs:
            @pl.when(pl.program_id(0) == 0)
            def _():
                for r in accs:
                    r[...] = jnp.zeros_like(r)
            for r, v in zip(accs, res_a):
                r[...] += v

    outs = pl.pallas_call(
        body, name=name, grid=(n,),
        in_specs=[tspec(w, cb) for (_, w, cb) in tiled] + [fspec(a) for a in full],
        out_specs=[tspec(w, 0) for w in out_w] + [pl.BlockSpec(s, lambda i, nd=len(s): (0,) * nd) for s in acc_shapes],
        out_shape=[jax.ShapeDtypeStruct((rows, w), dt) for w, dt in zip(out_w, out_dtypes or [F32] * no)]
        + [jax.ShapeDtypeStruct(s, F32) for s in acc_shapes],
        compiler_params=_params(("arbitrary",)),
    )(*[a for (a, _, _) in tiled], *full)
    return outs[:no], outs[no:]


def _colsum(x):
    return jnp.sum(x, axis=0, keepdims=True)


def _rowsum(x):
    return jnp.sum(x, axis=1, keepdims=True)


def _ln_epilogue(acc, resid, g, b):
    pre = ALPHA * resid + acc
    mu = jnp.mean(pre, axis=1, keepdims=True)
    xc = pre - mu
    var = jnp.mean(xc * xc, axis=1, keepdims=True)
    return pre, xc * lax.rsqrt(var + LN_EPS) * g + b


def _ln_bwd_fn(pre, dout, g):
    mu = jnp.mean(pre, axis=1, keepdims=True)
    xc = pre - mu
    var = jnp.mean(xc * xc, axis=1, keepdims=True)
    rstd = lax.rsqrt(var + LN_EPS)
    xhat = xc * rstd
    dxh = dout * g
    dpre = rstd * (dxh - jnp.mean(dxh, axis=1, keepdims=True) - xhat * jnp.mean(dxh * xhat, axis=1, keepdims=True))
    return (dpre,), (_colsum(dout * xhat), _colsum(dout))


def mm_ln(a, w, resid, g, b, *, name, fa=None, side=None):
    assert w.shape[1] == D_MODEL
    return mm(a, w, fa=fa, o_extra=(resid,), r_extra=(g, b), fo=_ln_epilogue, n_out=2, name=name, side=side)


def ln_bwd(pre, dout, g, *, name):
    (dpre,), (dg, db) = rowk(_ln_bwd_fn, [(pre, D_MODEL, 0), (dout, D_MODEL, 0)], [g],
                             [D_MODEL], [(1, D_MODEL), (1, D_MODEL)], rows=pre.shape[0], name=name, tt=2 * ROW_TILE)
    return dpre, dg, db


def _loss_fn(y, tgt):
    e = y - tgt
    part = _colsum(_rowsum(e * e)) * (0.5 / D_MODEL)
    return (e * (1.0 / D_MODEL),), (part,)


_XA_SCALE = 1.0 / math.sqrt(XA_HEAD_DIM)


def _attn_probs(qh, kh):
    s = _dot(qh, kh, _NT) * _XA_SCALE
    e = jnp.exp(s - jnp.max(s, axis=1, keepdims=True))
    return e / _rowsum(e)


def _attn_fwd_fn(q, k, v):
    outs = []
    for hd in range(XA_HEADS):
        sl = slice(hd * XA_HEAD_DIM, (hd + 1) * XA_HEAD_DIM)
        outs.append(_dot(_attn_probs(q[:, sl], k[:, sl]), v[:, sl]))
    return (jnp.concatenate(outs, axis=1),), ()


def _attn_bwd_fn(q, do, k, v):
    dqs, dks, dvs = [], [], []
    for hd in range(XA_HEADS):
        sl = slice(hd * XA_HEAD_DIM, (hd + 1) * XA_HEAD_DIM)
        qh, kh, vh, doh = q[:, sl], k[:, sl], v[:, sl], do[:, sl]
        p = _attn_probs(qh, kh)
        dp = _dot(doh, vh, _NT)
        ds = p * (dp - _rowsum(p * dp)) * _XA_SCALE
        dqs.append(_dot(ds, kh))
        dks.append(_dot(ds, qh, _TN))
        dvs.append(_dot(p, doh, _TN))
    cat = lambda xs: jnp.concatenate(xs, axis=1)
    return (cat(dqs),), (cat(dks), cat(dvs))


def _s5_post_fwd_fn(ylin, u, dskip, gw, gb):
    yg = _gelu(ylin + dskip * u)
    return (yg * _sigmoid(_dot(yg, gw) + gb),), ()


def _s5_post_bwd_fn(ylin, u, dout, dskip, gw, gb):
    pre = ylin + dskip * u
    yg = _gelu(pre)
    sg = _sigmoid(_dot(yg, gw) + gb)
    dlin = dout * yg * sg * (1.0 - sg)
    dyg = dout * sg + _dot(dlin, gw, _NT)
    dpre = dyg * _dgelu(pre)
    return (dpre, dpre * dskip), (_colsum(dpre * u), _dot(yg, dlin, _TN), _colsum(dlin))


def _rg_gates(xc, wa, wx, ba, bx, lam):
    r = _sigmoid(_dot(xc, wa) + ba)
    i = _sigmoid(_dot(xc, wx) + bx)
    sp = _softplus(-lam)
    log_a = -RG_C * r * sp
    a = jnp.exp(log_a)
    mult = jnp.sqrt(_neg_expm1(2.0 * log_a))
    return r, i, sp, a, mult


def _rg_pre_bwd_fn(xc, gsc, hprev, wa, wx, ba, bx, lam):
    r, i, sp, a, mult = _rg_gates(xc, wa, wx, ba, bx, lam)
    da = gsc * hprev
    db = gsc
    dmult = db * i * xc
    di = db * mult * xc
    dxc = db * mult * i
    dlog_a = da * a - a * a * dmult / mult
    dr = dlog_a * (-RG_C * sp)
    dsp = _colsum(dlog_a * (-RG_C * r))
    dlam = dsp * (-_sigmoid(-lam))
    dpr = dr * r * (1.0 - r)
    dpi = di * i * (1.0 - i)
    dxc = dxc + _dot(dpr, wa, _NT) + _dot(dpi, wx, _NT)
    return (dxc,), (_dot(xc, dpr, _TN), _dot(xc, dpi, _TN), _colsum(dpr), _colsum(dpi), dlam)


def _conv_taps(x_ref, halo_ref, first):
    x = x_ref[...]
    halo = jnp.where(first, 0.0, halo_ref[...])
    rows8 = lax.broadcasted_iota(jnp.int32, halo.shape, 0)
    taps = [x]
    for j in (1, 2, 3):
        r = pltpu.roll(x, j, 0)
        top = jnp.where(rows8 < j, pltpu.roll(halo, j, 0), r[0:8])
        taps.append(jnp.concatenate([top, r[8:]], axis=0))
    return taps


def _conv_pre(taps, cw_ref, cb_ref):
    wv = cw_ref[...]
    pre = cb_ref[...] + wv[3:4, :] * taps[0]
    for j in (1, 2, 3):
        pre = pre + wv[3 - j:4 - j, :] * taps[j]
    return pre


def _conv_back(dpre, taps, cw_ref, nxt_ref):
    q = dpre.shape[0]
    rows8 = lax.broadcasted_iota(jnp.int32, (8, dpre.shape[1]), 0)
    wv = cw_ref[...]
    dx = wv[3:4, :] * dpre
    for j in (1, 2, 3):
        r = pltpu.roll(dpre, q - j, 0)
        bottom = jnp.where(rows8 >= 8 - j, pltpu.roll(nxt_ref[...], 8 - j, 0), r[q - 8:q])
        dx = dx + wv[3 - j:4 - j, :] * jnp.concatenate([r[:q - 8], bottom], axis=0)
    dw = jnp.concatenate([_colsum(dpre * taps[3 - kk]) for kk in range(4)], axis=0)
    nxt_ref[...] = dpre[0:8]
    return dx, dw, _colsum(dpre)


S5_CW = 256


def _cmul(ar, ai, br, bi):
    return ar * br - ai * bi, ar * bi + ai * br


def _scan8_complex(src_ref, dst_ref, lam_ref, st_ref, *, w, nb, reverse):
    rows = lax.broadcasted_iota(jnp.int32, (8, S5_CW), 0)
    b8 = lambda v: jnp.broadcast_to(v, (8, S5_CW))

    def shift(x, k):
        if reverse:
            return jnp.where(rows < 8 - k, pltpu.roll(x, 8 - k, 0), 0.0)
        return jnp.where(rows >= k, pltpu.roll(x, k, 0), 0.0)

    for c0 in range(0, w, S5_CW):
        re, im = pl.ds(c0, S5_CW), pl.ds(w + c0, S5_CW)
        pw = [(lam_ref[:, re], lam_ref[:, im])]
        for _ in range(7):
            pw.append(_cmul(*pw[-1], *pw[0]))
        pr, pi = b8(pw[7][0]), b8(pw[7][1])
        for j in range(7):
            sel = rows == (7 - j if reverse else j)
            pr, pi = jnp.where(sel, b8(pw[j][0]), pr), jnp.where(sel, b8(pw[j][1]), pi)
        steps = [(k, b8(pw[k - 1][0]), b8(pw[k - 1][1])) for k in (1, 2, 4)]
        edge = 0 if reverse else 7

        def blk(i, carry):
            hr, hi = carry
            base = pl.multiple_of((nb // 2 - 1 - i if reverse else i) * 16, 16)
            pend = []
            for off in ((8, 0) if reverse else (0, 8)):
                at = pl.ds(base + off, 8)
                xr, xi = src_ref[at, re], src_ref[at, im]
                for k, kr, ki in steps:
                    sr, si = shift(xr, k), shift(xi, k)
                    xr, xi = xr + kr * sr - ki * si, xi + kr * si + ki * sr
                pend.append((at, xr, xi))
            for at, xr, xi in pend:
                xr, xi = xr + pr * hr - pi * hi, xi + pr * hi + pi * hr
                dst_ref[at, re] = xr
                dst_ref[at, im] = xi
                hr, hi = b8(xr[edge:edge + 1, :]), b8(xi[edge:edge + 1, :])
            return hr, hi

        hr, hi = lax.fori_loop(0, nb // 2, blk, (st_ref[:, re], st_ref[:, im]))
        st_ref[:, re] = hr
        st_ref[:, im] = hi


def s5_fwd(proj, bcat, lam, ccat, dskip, gw, gb, *, name):
    t = proj.shape[0]
    tt = min(ROW_TILE, t)
    w2 = bcat.shape[1]

    def body(u_ref, b_ref, lam_ref, c_ref, d_ref, gw_ref, gb_ref, h_ref, y_ref, o_ref, bu_ref, st_ref):
        @pl.when(pl.program_id(0) == 0)
        def _():
            st_ref[...] = jnp.zeros_like(st_ref)

        u = u_ref[...]
        bu_ref[...] = _dot(u, b_ref[...])
        _scan8_complex(bu_ref, h_ref, lam_ref, st_ref, w=w2 // 2, nb=tt // 8, reverse=False)
        ylin = _dot(h_ref[...], c_ref[...])
        y_ref[...] = ylin
        (out,), _ = _s5_post_fwd_fn(ylin, u, d_ref[...], gw_ref[...], gb_ref[...])
        o_ref[...] = out.astype(o_ref.dtype)

    fixed = lambda a: pl.BlockSpec(a.shape, lambda i: (0, 0))
    row = pl.BlockSpec((tt, S5_WIDTH), lambda i: (i, 0))
    return pl.pallas_call(
        body, name=name, grid=(t // tt,),
        in_specs=[pl.BlockSpec((tt, S5_WIDTH), lambda i: (i, P_U // S5_WIDTH))] + [fixed(x) for x in (bcat, lam, ccat, dskip, gw, gb)],
        out_specs=[pl.BlockSpec((tt, w2), lambda i: (i, 0)), row, row],
        out_shape=[jax.ShapeDtypeStruct((t, w2), F32), jax.ShapeDtypeStruct((t, S5_WIDTH), F32),
                   jax.ShapeDtypeStruct((t, S5_WIDTH), BF16)],
        scratch_shapes=[pltpu.VMEM((tt, w2), F32), pltpu.VMEM((8, w2), F32)],
        compiler_params=_params(("arbitrary",)),
    )(proj, bcat, lam, ccat, dskip, gw, gb)


def s5_bwd(dycat, ylin, hs, proj, bcat, lam_adj, ccat, dskip, gw, gb, *, name):
    t = proj.shape[0]
    tt = min(ROW_TILE, t)
    n, w2 = t // tt, bcat.shape[1]
    w = w2 // 2

    def body(dout_ref, yl_ref, h_ref, hp_ref, u_ref, b_ref, lam_ref, c_ref, d_ref, gw_ref, gb_ref,
             du_ref, dc_ref, db_ref, dar_ref, dai_ref, dd_ref, dgw_ref, dgb_ref, g_ref, st_ref):
        i = pl.program_id(0)

        @pl.when(i == 0)
        def _():
            for r in (st_ref, dc_ref, db_ref, dar_ref, dai_ref, dd_ref, dgw_ref, dgb_ref):
                r[...] = jnp.zeros_like(r)

        (dy, du_a), post = _s5_post_bwd_fn(yl_ref[...], u_ref[...], dout_ref[...], d_ref[...], gw_ref[...], gb_ref[...])
        for r, v in zip((dd_ref, dgw_ref, dgb_ref), post):
            r[...] += v
        h = h_ref[...]
        g_ref[...] = _dot(dy, c_ref[...], _NT)
        dc_ref[...] += _dot(h, dy, _TN)
        _scan8_complex(g_ref, g_ref, lam_ref, st_ref, w=w, nb=tt // 8, reverse=True)
        g = g_ref[...]
        du_ref[...] = (du_a + _dot(g, b_ref[...], _NT)).astype(du_ref.dtype)
        db_ref[...] += _dot(u_ref[...], g, _TN)
        rows = lax.broadcasted_iota(jnp.int32, (tt, w2), 0)
        before = jnp.where(i == n - 1, 0.0, hp_ref[7:8, :])
        hprev = jnp.where(rows == 0, before, pltpu.roll(h, 1, 0))
        gr, gi, hr, hi = g[:, :w], g[:, w:], hprev[:, :w], hprev[:, w:]
        dar_ref[...] += _colsum(gr * hr + gi * hi)
        dai_ref[...] += _colsum(gi * hr - gr * hi)

    rev = lambda i: n - 1 - i
    row = lambda wd, cb=0: pl.BlockSpec((tt, wd), lambda i: (rev(i), cb))
    fixed = lambda shape: pl.BlockSpec(shape, lambda i: (0, 0))
    return pl.pallas_call(
        body, name=name, grid=(n,),
        in_specs=[row(S5_WIDTH, 2), row(S5_WIDTH), row(w2),
                  pl.BlockSpec((8, w2), lambda i: (jnp.maximum(rev(i) * (tt // 8) - 1, 0), 0)),
                  row(S5_WIDTH, P_U // S5_WIDTH)] + [fixed(x.shape) for x in (bcat, lam_adj, ccat, dskip, gw, gb)],
        out_specs=[row(S5_WIDTH), fixed(ccat.shape), fixed(bcat.shape), fixed((1, w)), fixed((1, w)),
                   fixed((1, S5_WIDTH)), fixed((S5_WIDTH, S5_WIDTH)), fixed((1, S5_WIDTH))],
        out_shape=[jax.ShapeDtypeStruct((t, S5_WIDTH), BF16), jax.ShapeDtypeStruct(ccat.shape, F32),
                   jax.ShapeDtypeStruct(bcat.shape, F32), jax.ShapeDtypeStruct((1, w), F32), jax.ShapeDtypeStruct((1, w), F32),
                   jax.ShapeDtypeStruct((1, S5_WIDTH), F32), jax.ShapeDtypeStruct((S5_WIDTH, S5_WIDTH), F32),
                   jax.ShapeDtypeStruct((1, S5_WIDTH), F32)],
        scratch_shapes=[pltpu.VMEM((tt, w2), F32), pltpu.VMEM((8, w2), F32)],
        compiler_params=_params(("arbitrary",)),
    )(dycat, ylin, hs, hs, proj, bcat, lam_adj, ccat, dskip, gw, gb)


def _scan8_real(a_ref, b_ref, o_ref, st_ref, *, nb, reverse):
    w = o_ref.shape[1]
    rows = lax.broadcasted_iota(jnp.int32, (8, w), 0)
    edge = 0 if reverse else 7

    def shift(x, k, fill):
        if reverse:
            return jnp.where(rows < 8 - k, pltpu.roll(x, 8 - k, 0), fill)
        return jnp.where(rows >= k, pltpu.roll(x, k, 0), fill)

    def blk(i, h):
        at = pl.ds(pl.multiple_of((nb - 1 - i if reverse else i) * 8, 8), 8)
        a, b = a_ref[at, :], b_ref[at, :]
        for k in (1, 2, 4):
            a, b = a * shift(a, k, 1.0), b + a * shift(b, k, 0.0)
        out = b + a * h
        o_ref[at, :] = out
        return jnp.broadcast_to(out[edge:edge + 1, :], (8, w))

    st_ref[...] = lax.fori_loop(0, nb, blk, st_ref[...])


def _rg_specs(tt, idx):
    return [pl.BlockSpec((tt, RG_WIDTH), lambda i: (idx(i), P_XR // RG_WIDTH)),
            pl.BlockSpec((8, RG_WIDTH), lambda i: (jnp.maximum(idx(i) * (tt // 8) - 1, 0), P_XR // RG_WIDTH)),
            pl.BlockSpec((tt, RG_WIDTH), lambda i: (idx(i), P_G // RG_WIDTH))]


def rg_fwd(proj, cw, cb, wa, wx, ba, bx, lam, *, name):
    t = proj.shape[0]
    tt = min(ROW_TILE, t)
    w = RG_WIDTH

    def body(x_ref, halo_ref, g_ref, cw_ref, cb_ref, wa_ref, wx_ref, ba_ref, bx_ref, lam_ref,
             y_ref, xc_ref, a_ref, h_ref, b_ref, st_ref):
        @pl.when(pl.program_id(0) == 0)
        def _():
            st_ref[...] = jnp.zeros_like(st_ref)

        xc = _conv_pre(_conv_taps(x_ref, halo_ref, pl.program_id(0) == 0), cw_ref, cb_ref)
        xc_ref[...] = xc
        r, i, sp, a, mult = _rg_gates(xc, wa_ref[...], wx_ref[...], ba_ref[...], bx_ref[...], lam_ref[...])
        a_ref[...] = a
        b_ref[...] = mult * (i * xc)
        _scan8_real(a_ref, b_ref, h_ref, st_ref, nb=tt // 8, reverse=False)
        y_ref[...] = (h_ref[...] * _gelu(g_ref[...])).astype(y_ref.dtype)

    fixed = lambda a: pl.BlockSpec(a.shape, lambda i: (0, 0))
    row = pl.BlockSpec((tt, w), lambda i: (i, 0))
    return pl.pallas_call(
        body, name=name, grid=(t // tt,),
        in_specs=_rg_specs(tt, lambda i: i) + [fixed(x) for x in (cw, cb, wa, wx, ba, bx, lam)],
        out_specs=[row] * 4,
        out_shape=[jax.ShapeDtypeStruct((t, w), BF16)] + [jax.ShapeDtypeStruct((t, w), F32)] * 3,
        scratch_shapes=[pltpu.VMEM((tt, w), F32), pltpu.VMEM((8, w), F32)],
        compiler_params=_params(("arbitrary",)),
    )(proj, proj, proj, cw, cb, wa, wx, ba, bx, lam)


def rg_bwd(proj, dycat, xc, a, h, cw, cb, wa, wx, ba, bx, lam, *, name):
    t = proj.shape[0]
    tt = min(ROW_TILE, t)
    n, w = t // tt, RG_WIDTH

    def body(x_ref, halo_ref, g_ref, dy_ref, xc_ref, a_ref, h_ref, hp_ref, cw_ref, wa_ref, wx_ref, ba_ref, bx_ref, lam_ref,
             dx_ref, dg_ref, dcw_ref, dcb_ref, dwa_ref, dwx_ref, dba_ref, dbx_ref, dlam_ref,
             au_ref, dh_ref, gs_ref, st_ref, anx_ref, nxt_ref):
        i = pl.program_id(0)
        accs = (dcw_ref, dcb_ref, dwa_ref, dwx_ref, dba_ref, dbx_ref, dlam_ref)

        @pl.when(i == 0)
        def _():
            for r in accs + (st_ref, anx_ref, nxt_ref):
                r[...] = jnp.zeros_like(r)

        h, g, dy, a = h_ref[...], g_ref[...], dy_ref[...], a_ref[...]
        dh_ref[...] = dy * _gelu(g)
        dg_ref[...] = (dy * h * _dgelu(g)).astype(dg_ref.dtype)
        rows = lax.broadcasted_iota(jnp.int32, (tt, w), 0)
        au_ref[...] = jnp.where(rows == tt - 1, anx_ref[0:1, :], pltpu.roll(a, tt - 1, 0))
        _scan8_real(au_ref, dh_ref, gs_ref, st_ref, nb=tt // 8, reverse=True)
        before = jnp.where(i == n - 1, 0.0, hp_ref[7:8, :])
        hprev = jnp.where(rows == 0, before, pltpu.roll(h, 1, 0))
        (dxc,), small = _rg_pre_bwd_fn(xc_ref[...], gs_ref[...], hprev, wa_ref[...], wx_ref[...], ba_ref[...], bx_ref[...], lam_ref[...])
        dx, dcw, dcb = _conv_back(dxc, _conv_taps(x_ref, halo_ref, i == n - 1), cw_ref, nxt_ref)
        dx_ref[...] = dx.astype(dx_ref.dtype)
        for r, v in zip(accs, (dcw, dcb) + tuple(small)):
            r[...] += v
        anx_ref[...] = a[0:8]

    rev = lambda i: n - 1 - i
    row = lambda cb_=0: pl.BlockSpec((tt, w), lambda i: (rev(i), cb_))
    fixed = lambda shape: pl.BlockSpec(shape, lambda i: (0, 0))
    acc_shapes = [(4, w), (1, w), (w, w), (w, w), (1, w), (1, w), (1, w)]
    return pl.pallas_call(
        body, name=name, grid=(n,),
        in_specs=_rg_specs(tt, rev) + [row(3), row(), row(), row(),
                                       pl.BlockSpec((8, w), lambda i: (jnp.maximum(rev(i) * (tt // 8) - 1, 0), 0))]
        + [fixed(x.shape) for x in (cw, wa, wx, ba, bx, lam)],
        out_specs=[row(), row()] + [fixed(sh) for sh in acc_shapes],
        out_shape=[jax.ShapeDtypeStruct((t, w), BF16)] * 2 + [jax.ShapeDtypeStruct(sh, F32) for sh in acc_shapes],
        scratch_shapes=[pltpu.VMEM((tt, w), F32)] * 3 + [pltpu.VMEM((8, w), F32)] * 3,
        compiler_params=_params(("arbitrary",)),
    )(proj, proj, proj, dycat, xc, a, h, h, cw, wa, wx, ba, bx, lam)


SSD_QQ = SSD_HEADS * SSD_CHUNK
SSD_GP = SSD_WIDTH // 2
SSD_GQ = SSD_QQ // 2


def _ssd_spread():
    h = jnp.arange(LANE)[:, None]
    spread_p = (jnp.arange(SSD_WIDTH)[None, :] // SSD_HEAD_DIM == h).astype(BF16)
    spread_q = (jnp.arange(SSD_QQ)[None, :] // SSD_CHUNK == h).astype(BF16)
    return spread_p, spread_q


def _ssd_prologue(dt_ref, prow_ref, sp_ref, sq_ref):
    q = SSD_CHUNK
    r = lax.broadcasted_iota(jnp.int32, (q, q), 0)
    c = lax.broadcasted_iota(jnp.int32, (q, q), 1)
    raw_c = dt_ref[...] + prow_ref[0:1, :]
    dt_c = _softplus(raw_c)
    a_r = -jnp.exp(prow_ref[1:2, :])
    cs_c = _dot_mask((r >= c).astype(F32), dt_c * a_r, mask_left=True, parts=3)
    both = _dot_mask(jnp.concatenate([dt_c, cs_c], axis=0), sp_ref[...], mask_left=False, parts=3)
    dt_x, cs_x = both[:q], both[q:]
    csx = _dot_mask(cs_c, sq_ref[...], mask_left=False, parts=3)
    rr = lax.broadcasted_iota(jnp.int32, (q, SSD_QQ), 0)
    ss = lax.broadcasted_iota(jnp.int32, (q, SSD_QQ), 1) & (q - 1)
    diag = rr == ss
    cs_row = _colsum(jnp.where(diag, csx, 0.0))
    lcat = jnp.exp(jnp.where(rr >= ss, csx - cs_row, -1e30))
    cl = cs_x[q - 1:q, :]
    return dict(raw_c=raw_c, dt_c=dt_c, a_r=a_r, dt_x=dt_x, cs_x=cs_x, lcat=lcat, diag=diag,
                ecs=jnp.exp(cs_x), wdec=jnp.exp(cl - cs_x), ecl=jnp.exp(cl), triu=(r <= c).astype(F32))


def _ssd_group(xbc_ref, g, lcat, xdt):
    ns, q = SSD_STATE, SSD_CHUNK
    bm = xbc_ref[:, pl.ds(SSD_WIDTH + g * ns, ns)]
    cm = xbc_ref[:, pl.ds(SSD_WIDTH + 2 * ns + g * ns, ns)]
    cb = _dot(cm, bm, _NT)
    lg = lcat[:, g * SSD_GQ:(g + 1) * SSD_GQ]
    wcat = jnp.concatenate([cb] * 4, axis=1) * lg
    head = lax.broadcasted_iota(jnp.int32, (1, SSD_GP), 1) // SSD_HEAD_DIM
    xg = xdt[:, g * SSD_GP:(g + 1) * SSD_GP]
    xbd = jnp.concatenate([jnp.where(head == j, xg, 0.0) for j in range(4)], axis=0)
    return bm, cm, lg, wcat, xbd, head


def _ssd_gate(yraw, z, nw):
    yg = yraw * _silu(z)
    r = lax.rsqrt(jnp.mean(yg * yg, axis=1, keepdims=True) + LN_EPS)
    return yg, r


def _ssd_specs(q, idx):
    return [pl.BlockSpec((q, SSD_XBC), lambda i: (idx(i), P_XBC // SSD_XBC)),
            pl.BlockSpec((8, SSD_XBC), lambda i: (jnp.maximum(idx(i) * (q // 8) - 1, 0), P_XBC // SSD_XBC)),
            pl.BlockSpec((q, SSD_WIDTH), lambda i: (idx(i), P_Z // SSD_WIDTH)),
            pl.BlockSpec((q, LANE), lambda i: (idx(i), P_DT // LANE)),
            pl.BlockSpec((4, SSD_XBC), lambda i: (0, 0)), pl.BlockSpec((1, SSD_XBC), lambda i: (0, 0)),
            pl.BlockSpec((8, LANE), lambda i: (0, 0)), pl.BlockSpec((1, SSD_WIDTH), lambda i: (0, 0)),
            pl.BlockSpec((1, SSD_WIDTH), lambda i: (0, 0)),
            pl.BlockSpec((LANE, SSD_WIDTH), lambda i: (0, 0)), pl.BlockSpec((LANE, SSD_QQ), lambda i: (0, 0))]


def ssd_fwd(proj, cw, cb, prow, d_x, nw, *, name):
    t = proj.shape[0]
    q, ns = SSD_CHUNK, SSD_STATE
    nc = t // q
    spread_p, spread_q = _ssd_spread()

    def body(x_ref, halo_ref, z_ref, dt_ref, cw_ref, cb_ref, prow_ref, dx_ref, nw_ref, sp_ref, sq_ref,
             y_ref, yraw_ref, sall_ref, s_ref, xbc_ref):
        @pl.when(pl.program_id(0) == 0)
        def _():
            s_ref[...] = jnp.zeros_like(s_ref)

        sall_ref[0] = s_ref[...]
        xbc_ref[...] = _silu(_conv_pre(_conv_taps(x_ref, halo_ref, pl.program_id(0) == 0), cw_ref, cb_ref))
        pr = _ssd_prologue(dt_ref, prow_ref, sp_ref, sq_ref)
        xs = xbc_ref[:, pl.ds(0, SSD_WIDTH)]
        xdt = xs * pr["dt_x"]
        xw = xdt * pr["wdec"]
        ys = []
        for g in range(2):
            gp = slice(g * SSD_GP, (g + 1) * SSD_GP)
            bm, cm, lg, wcat, xbd, head = _ssd_group(xbc_ref, g, pr["lcat"], xdt)
            st = s_ref[:, gp]
            ys.append(_dot(wcat, xbd) + pr["ecs"][:, gp] * _dot(cm, st) + xs[:, gp] * dx_ref[:, gp])
            s_ref[:, gp] = pr["ecl"][:, gp] * st + _dot(bm, xw[:, gp], _TN)
        yraw = jnp.concatenate(ys, axis=1)
        yraw_ref[...] = yraw
        yg, r = _ssd_gate(yraw, z_ref[...], nw_ref[...])
        y_ref[...] = (yg * r * nw_ref[...]).astype(y_ref.dtype)

    row = pl.BlockSpec((q, SSD_WIDTH), lambda i: (i, 0))
    return pl.pallas_call(
        body, name=name, grid=(nc,),
        in_specs=_ssd_specs(q, lambda i: i),
        out_specs=[row, row, pl.BlockSpec((1, ns, SSD_WIDTH), lambda i: (i, 0, 0))],
        out_shape=[jax.ShapeDtypeStruct((t, SSD_WIDTH), BF16), jax.ShapeDtypeStruct((t, SSD_WIDTH), F32),
                   jax.ShapeDtypeStruct((nc, ns, SSD_WIDTH), F32)],
        scratch_shapes=[pltpu.VMEM((ns, SSD_WIDTH), F32), pltpu.VMEM((q, SSD_XBC), F32)],
        compiler_params=_params(("arbitrary",)),
    )(proj, proj, proj, proj, cw, cb, prow, d_x, nw, spread_p, spread_q)


def ssd_bwd(proj, cw, cb, prow, d_x, nw, yraw, sall, dout, *, name):
    t = proj.shape[0]
    q, ns = SSD_CHUNK, SSD_STATE
    nc = t // q
    spread_p, spread_q = _ssd_spread()

    def body(x_ref, halo_ref, z_ref, dt_ref, cw_ref, cb_ref, prow_ref, dx_ref, nw_ref, sp_ref, sq_ref, yraw_ref, sall_ref, dout_ref,
             dxraw_ref, dz_ref, ddt_ref, dprm_ref, ddx_ref, dnw_ref, dcw_ref, dcb_ref, ds_ref, xbc_ref, dxbc_ref, nxt_ref):
        @pl.when(pl.program_id(0) == 0)
        def _():
            for r in (ds_ref, dprm_ref, ddx_ref, dnw_ref, dcw_ref, dcb_ref, nxt_ref):
                r[...] = jnp.zeros_like(r)

        taps = _conv_taps(x_ref, halo_ref, pl.program_id(0) == nc - 1)
        conv_pre = _conv_pre(taps, cw_ref, cb_ref)
        xbc_ref[...] = _silu(conv_pre)

        yraw, z, nwv, dout = yraw_ref[...], z_ref[...], nw_ref[...], dout_ref[...]
        yg, r = _ssd_gate(yraw, z, nwv)
        dnw_ref[...] += _colsum(dout * yg * r)
        dyn = dout * nwv
        dyg = r * dyn - yg * (r * r * r) * jnp.mean(dyn * yg, axis=1, keepdims=True)
        dy = dyg * _silu(z)
        dz_ref[...] = (dyg * yraw * _dsilu(z)).astype(dz_ref.dtype)

        pr = _ssd_prologue(dt_ref, prow_ref, sp_ref, sq_ref)
        xs = xbc_ref[:, pl.ds(0, SSD_WIDTH)]
        xdt = xs * pr["dt_x"]
        wdec, ecl = pr["wdec"], pr["ecl"]
        xw = xdt * wdec
        dzm_all = pr["ecs"] * dy
        last = (lax.broadcasted_iota(jnp.int32, (q, 1), 0) == q - 1).astype(F32)
        dxs, dcsxs, es = [], [], []
        for g in range(2):
            gp = slice(g * SSD_GP, (g + 1) * SSD_GP)
            bm, cm, lg, wcat, xbd, head = _ssd_group(xbc_ref, g, pr["lcat"], xdt)
            dyg_ = dy[:, gp]
            dwcat = _dot(dyg_, xbd, _NT)
            dxbd = _dot(wcat, dyg_, _TN)
            dxg = sum(jnp.where(head == j, dxbd[j * q:(j + 1) * q], 0.0) for j in range(4))
            es.append(dwcat * wcat)
            dmm = dwcat * lg
            dm = dmm[:, 0:q] + dmm[:, q:2 * q] + dmm[:, 2 * q:3 * q] + dmm[:, 3 * q:4 * q]
            dcm = _dot(dm, bm)
            dbm = _dot(dm, cm, _TN)
            st = sall_ref[0, :, gp]
            zmat = _dot(cm, st)
            dzm = dzm_all[:, gp]
            dcm = dcm + _dot(dzm, st, _NT)
            dst = _dot(cm, dzm, _TN)
            dcsx = dzm * zmat
            dsn = ds_ref[:, gp]
            dst = dst + ecl[:, gp] * dsn
            dclx = _colsum(dsn * st) * ecl[:, gp]
            dxw = _dot(bm, dsn)
            dbm = dbm + _dot(xw[:, gp], dsn, _NT)
            dxg = dxg + wdec[:, gp] * dxw
            tw = dxw * xdt[:, gp] * wdec[:, gp]
            dclx = dclx + _colsum(tw)
            dcsxs.append(dcsx - tw + last * dclx)
            ds_ref[:, gp] = dst
            dxs.append(dxg)
            dxbc_ref[:, pl.ds(SSD_WIDTH + g * ns, ns)] = dbm
            dxbc_ref[:, pl.ds(SSD_WIDTH + 2 * ns + g * ns, ns)] = dcm
        dx = jnp.concatenate(dxs, axis=1)
        dxbc_ref[:, pl.ds(0, SSD_WIDTH)] = dx * pr["dt_x"] + dy * dx_ref[...]
        ddx_ref[...] += _colsum(dy * xs)
        red = _dot_mask(jnp.concatenate([jnp.concatenate(dcsxs, axis=1), dx * xs], axis=0), sp_ref[...], _NT,
                        mask_left=False, parts=2)
        e_all = jnp.concatenate(es, axis=1)
        e_red = _dot_mask(e_all - jnp.where(pr["diag"], _colsum(e_all), 0.0), sq_ref[...], _NT, mask_left=False, parts=2)
        dadt = _dot_mask(pr["triu"], red[:q] + e_red, mask_left=True, parts=2)
        draw = (red[q:] + dadt * pr["a_r"]) * _sigmoid(pr["raw_c"])
        ddt_ref[...] = draw.astype(ddt_ref.dtype)
        zero = jnp.zeros((6, LANE), F32)
        dprm_ref[...] += jnp.concatenate([_colsum(draw), _colsum(dadt * pr["dt_c"]) * pr["a_r"], zero], axis=0)
        dxr, dcw, dcb = _conv_back(dxbc_ref[...] * _dsilu(conv_pre), taps, cw_ref, nxt_ref)
        dxraw_ref[...] = dxr.astype(dxraw_ref.dtype)
        dcw_ref[...] += dcw
        dcb_ref[...] += dcb

    rev = lambda i: nc - 1 - i
    row = lambda w: pl.BlockSpec((q, w), lambda i: (rev(i), 0))
    fixed = lambda shape: pl.BlockSpec(shape, lambda i: (0, 0))
    return pl.pallas_call(
        body, name=name, grid=(nc,),
        in_specs=_ssd_specs(q, rev) + [row(SSD_WIDTH), pl.BlockSpec((1, ns, SSD_WIDTH), lambda i: (rev(i), 0, 0)),
                                       row(SSD_WIDTH)],
        out_specs=[row(SSD_XBC), row(SSD_WIDTH), row(LANE), fixed((8, LANE)), fixed((1, SSD_WIDTH)), fixed((1, SSD_WIDTH)),
                   fixed((4, SSD_XBC)), fixed((1, SSD_XBC))],
        out_shape=[jax.ShapeDtypeStruct((t, SSD_XBC), BF16), jax.ShapeDtypeStruct((t, SSD_WIDTH), BF16),
                   jax.ShapeDtypeStruct((t, LANE), BF16), jax.ShapeDtypeStruct((8, LANE), F32),
                   jax.ShapeDtypeStruct((1, SSD_WIDTH), F32), jax.ShapeDtypeStruct((1, SSD_WIDTH), F32),
                   jax.ShapeDtypeStruct((4, SSD_XBC), F32), jax.ShapeDtypeStruct((1, SSD_XBC), F32)],
        scratch_shapes=[pltpu.VMEM((ns, SSD_WIDTH), F32), pltpu.VMEM((q, SSD_XBC), F32), pltpu.VMEM((q, SSD_XBC), F32),
                        pltpu.VMEM((8, SSD_XBC), F32)],
        compiler_params=_params(("arbitrary",)),
    )(proj, proj, proj, proj, cw, cb, prow, d_x, nw, spread_p, spread_q, yraw, sall, dout)


def _me():
    return lax.axis_index("x"), lax.axis_index("y"), lax.axis_index("c")


_ANY = pl.BlockSpec(memory_space=pl.ANY)
_MESH = pl.DeviceIdType.MESH


_AG_SEMS = [pltpu.SemaphoreType.DMA((7,)), pltpu.SemaphoreType.DMA((7,)), pltpu.SemaphoreType.DMA(())]
_RS_SEMS = [pltpu.SemaphoreType.DMA((3,)), pltpu.SemaphoreType.DMA((3,)), pltpu.SemaphoreType.DMA(())]


def _ag_phases(src, dst, send_sems, recv_sems, local_sem):
    x, y, c = _me()
    me, sibling = (x, y, c), (x, y, 1 - c)
    chips = [(1 - x, y), (x, 1 - y), (1 - x, 1 - y)]

    def slot(px, py, pc):
        return dst.at[4 * px + 2 * py + pc]

    def copy(kk, blk, to, from_src=False):
        return pltpu.make_async_remote_copy(
            src_ref=src if from_src else slot(*blk), dst_ref=slot(*blk),
            send_sem=send_sems.at[kk], recv_sem=recv_sems.at[kk], device_id=to, device_id_type=_MESH)

    mine = lambda: pltpu.make_async_copy(src, slot(*me), local_sem)
    first = lambda: [copy(0, me, sibling, True)] + [copy(1 + j, me, (*chip, c), True) for j, chip in enumerate(chips)]
    passed = lambda j: copy(4 + j, (*chips[j], c), sibling)

    def start():
        mine().start()
        for cp in first():
            cp.start()

    def forward():
        for j, chip in enumerate(chips):
            copy(1 + j, (*chip, c), me).wait_recv()
            passed(j).start()

    def finish():
        copy(0, sibling, me).wait_recv()
        for j, chip in enumerate(chips):
            copy(4 + j, (*chip, 1 - c), me).wait_recv()
        for cp in first() + [passed(j) for j in range(3)]:
            cp.wait_send()
        mine().wait()

    return start, forward, finish


def _rs_chip_phases(src, dst, send_sems, recv_sems, local_sem, rows=None):
    x, y, c = _me()
    q_me = 2 * x + y
    pick = (lambda q: src.at[q]) if rows is None else (lambda q: src.at[q, rows])
    local = lambda: pltpu.make_async_copy(pick(q_me), dst.at[q_me], local_sem)
    copies = lambda: [pltpu.make_async_remote_copy(src_ref=pick(2 * px + py), dst_ref=dst.at[q_me], send_sem=send_sems.at[j],
                                                   recv_sem=recv_sems.at[j], device_id=(px, py, c), device_id_type=_MESH)
                      for j, (px, py) in enumerate([(1 - x, y), (x, 1 - y), (1 - x, 1 - y)])]

    def start():
        local().start()
        for cp in copies():
            cp.start()

    def finish():
        for cp in copies():
            cp.wait()
        local().wait()

    return start, finish


def all_gather(block, *, name):
    def body(src, dst, send_sems, recv_sems, local_sem):
        for phase in _ag_phases(src, dst, send_sems, recv_sems, local_sem):
            phase()

    return pl.pallas_call(
        body, name=name, in_specs=[_ANY], out_specs=_ANY,
        out_shape=jax.ShapeDtypeStruct((N_DEV,) + block.shape, block.dtype), scratch_shapes=list(_AG_SEMS),
    )(block)


RS_PIECES = 4


def rs_sibling_exchange(halves, *, name):
    _, nq, r, l = halves.shape
    rows = r // RS_PIECES
    assert r % RS_PIECES == 0 and rows % 16 == 0

    def body(src, dst, send_sems, recv_sems):
        x, y, c = _me()
        copies = []
        for q in range(nq):
            for i in range(RS_PIECES):
                kk = q * RS_PIECES + i
                cp = pltpu.make_async_remote_copy(
                    src_ref=src.at[1 - c, q, pl.ds(i * rows, rows)], dst_ref=dst.at[q, pl.ds(i * rows, rows)],
                    send_sem=send_sems.at[kk], recv_sem=recv_sems.at[kk], device_id=(x, y, 1 - c), device_id_type=_MESH)
                cp.start()
                copies.append(cp)
        for cp in copies:
            cp.wait()

    n_copies = nq * RS_PIECES
    return pl.pallas_call(
        body, name=name, in_specs=[_ANY], out_specs=_ANY,
        out_shape=jax.ShapeDtypeStruct((nq, r, l), halves.dtype),
        scratch_shapes=[pltpu.SemaphoreType.DMA((n_copies,)), pltpu.SemaphoreType.DMA((n_copies,))],
    )(halves)


def pair_sum_bf16(halves, theirs, *, name, tt=512):
    _, nq, r, wd = halves.shape
    tt = min(tt, r)
    parity = lax.axis_index("c").astype(jnp.int32).reshape(1)

    def body(c_ref, own_ref, sib_ref, o_ref):
        o_ref[...] = (own_ref[...] + sib_ref[...]).astype(BF16)

    return pl.pallas_call(
        body, name=name,
        grid_spec=pltpu.PrefetchScalarGridSpec(
            num_scalar_prefetch=1, grid=(nq, r // tt),
            in_specs=[pl.BlockSpec((None, None, tt, wd), lambda q, i, c: (c[0], q, i, 0)),
                      pl.BlockSpec((None, tt, wd), lambda q, i, c: (q, i, 0))],
            out_specs=pl.BlockSpec((None, tt, wd), lambda q, i, c: (q, i, 0))),
        out_shape=jax.ShapeDtypeStruct((nq, r, wd), BF16),
        compiler_params=_params(("parallel", "parallel")),
    )(parity, halves, theirs)


def rs_chip_exchange(part, *, name):
    def body(src, dst, send_sems, recv_sems, local_sem):
        for phase in _rs_chip_phases(src, dst, send_sems, recv_sems, local_sem):
            phase()

    return pl.pallas_call(
        body, name=name, in_specs=[_ANY], out_specs=_ANY,
        out_shape=jax.ShapeDtypeStruct(part.shape, part.dtype), scratch_shapes=list(_RS_SEMS),
    )(part)


def adamw(slabs, w, m, v, *, name, tt):
    ns, (r, wd) = slabs.shape[0], w.shape
    tt = min(tt, r)
    assert r % tt == 0

    def body(s_ref, w_ref, m_ref, v_ref, g_ref, d_ref, nm_ref, nv_ref):
        g = s_ref[0].astype(F32)
        for kdev in range(1, ns):
            g = g + s_ref[kdev].astype(F32)
        wv = w_ref[...]
        nm = ADAM_B1 * m_ref[...] + (1.0 - ADAM_B1) * g
        nv = ADAM_B2 * v_ref[...] + (1.0 - ADAM_B2) * (g * g)
        m_hat = nm / (1.0 - ADAM_B1 ** ADAM_STEP)
        v_hat = nv / (1.0 - ADAM_B2 ** ADAM_STEP)
        g_ref[...] = g
        d_ref[...] = -ADAM_LR * (m_hat / (jnp.sqrt(v_hat) + ADAM_EPS) + ADAM_WD * wv)
        nm_ref[...] = nm
        nv_ref[...] = nv

    spec = pl.BlockSpec((tt, wd), lambda i: (i, 0))
    return pl.pallas_call(
        body, name=name, grid=(r // tt,),
        in_specs=[pl.BlockSpec((ns, tt, wd), lambda i: (0, i, 0)), spec, spec, spec],
        out_specs=[spec] * 4, out_shape=[jax.ShapeDtypeStruct((r, wd), F32)] * 4,
        compiler_params=_params(("parallel",)),
    )(slabs, w, m, v)


WIDE = 1024
BIG = [("w_in", True, 289), ("w_out", False, 128), ("xa_wq", False, 128), ("xa_wk", False, 128), ("xa_wv", False, 128),
       ("xa_wo", False, 128), ("mlp_w2", False, 512), ("mlp_w1", True, 512)]
TINY = [("ssd_conv_w", 2), ("s5_glu_w", 1), ("rg_conv_w", 2)]
KEEP_F32 = ("ssd_conv_w", "rg_conv_w")
TINY_ROWS = 32
SHARDED = [name for name, _, _ in BIG] + [name for name, _ in TINY]
SMALL = ["ssd_conv_b", "ssd_dt_bias", "ssd_a_log", "ssd_d", "ssd_norm_w", "s5_lam_re", "s5_lam_im",
         "s5_log_step", "s5_b_re", "s5_b_im", "s5_c_re", "s5_c_im", "s5_d", "s5_glu_b", "rg_conv_b",
         "rg_wa", "rg_ba", "rg_wx", "rg_bx", "rg_lambda", "ln1_g", "ln1_b", "ln2_g", "ln2_b", "ln3_g", "ln3_b"]
WEIGHTS = ['w_in', 'w_out', 'ssd_conv_w', 'ssd_conv_b', 'ssd_dt_bias', 'ssd_a_log', 'ssd_d', 'ssd_norm_w',
           's5_lam_re', 's5_lam_im', 's5_log_step', 's5_b_re', 's5_b_im', 's5_c_re', 's5_c_im', 's5_d',
           's5_glu_w', 's5_glu_b', 'rg_conv_w', 'rg_conv_b', 'rg_wa', 'rg_ba', 'rg_wx', 'rg_bx', 'rg_lambda',
           'ln1_g', 'ln1_b', 'xa_wq', 'xa_wk', 'xa_wv', 'xa_wo', 'ln2_g', 'ln2_b', 'mlp_w1', 'mlp_w2',
           'ln3_g', 'ln3_b']


def _pad16(rows):
    return -(-rows // 16) * 16


def _pack_rows(flat, mult):
    n = flat.shape[-1]
    r = -(-n // (LANE * mult)) * mult
    pad = [(0, 0)] * (flat.ndim - 1) + [(0, r * LANE - n)]
    return jnp.pad(flat, pad).reshape(flat.shape[:-1] + (r, LANE))


def _unpack(packed, shapes):
    lead = packed.shape[:-2]
    flat = packed.reshape(lead + (-1,))
    out, off = [], 0
    for s in shapes:
        n = math.prod(s)
        out.append(flat[..., off:off + n].reshape(lead + tuple(s)))
        off += n
    return out


PACK_ROWS = 2048


def _tiny_block(flat):
    pad = [(0, 0)] * (flat.ndim - 1) + [(0, TINY_ROWS * WIDE - flat.shape[-1])]
    return jnp.pad(flat, pad).reshape(flat.shape[:-1] + (TINY_ROWS, WIDE))


def _pack_layer(big, tiny_flat=None):
    blocks, used = [], 0
    some = big[BIG[0][0]]

    def zeros(rows):
        return jnp.zeros(some.shape[:-2] + (rows, WIDE), some.dtype)

    for name, _, rows in BIG:
        blocks.append(jnp.pad(big[name], [(0, 0)] * (some.ndim - 2) + [(0, _pad16(rows) - rows), (0, 0)]))
        used += _pad16(rows)
    if tiny_flat is not None:
        blocks.append(_tiny_block(tiny_flat))
        used += TINY_ROWS
    return jnp.concatenate(blocks + [zeros(PACK_ROWS - used)], axis=-2)


def _unpack_layer(packed):
    big, off = {}, 0
    for name, _, rows in BIG:
        big[name] = packed[..., off:off + rows, :]
        off += _pad16(rows)
    return big, packed[..., off:off + TINY_ROWS, :].reshape(packed.shape[:-2] + (TINY_ROWS * WIDE,))


def _split_flat(flat, shapes):
    out, off = [], 0
    for s in shapes:
        n = math.prod(s)
        out.append(flat[..., off:off + n].reshape(flat.shape[:-1] + tuple(s)))
        off += n
    return out


def _to_full(gathered, axis):
    g = jnp.moveaxis(gathered, 0, axis)
    s = g.shape
    return g.reshape(s[:axis] + (s[axis] * s[axis + 1],) + s[axis + 2:])


def _to_slabs(full, axis):
    s = full.shape
    g = full.reshape(s[:axis] + (N_DEV, s[axis] // N_DEV) + s[axis + 1:])
    return jnp.moveaxis(g, axis, 0)


def _blockdiag(w):
    h, i, j = w.shape
    eye = jnp.eye(h, dtype=w.dtype)
    return (w[:, :, None, :] * eye[:, None, :, None]).reshape(h * i, h * j)


def _blockdiag_extract(m, h):
    i, j = m.shape[0] // h, m.shape[1] // h
    eye = jnp.eye(h, dtype=m.dtype)
    return (m.reshape(h, i, h, j) * eye[:, None, :, None]).sum(axis=2)


def _s5_disc(lr, li, ls, bre, bim):
    step = jnp.exp(ls)[:, None]
    er = jnp.exp(lr * step)
    ar, ai = er * jnp.cos(li * step), er * jnp.sin(li * step)
    nr, ni, den = ar - 1.0, ai, lr * lr + li * li
    qr, qi = (nr * lr + ni * li) / den, (ni * lr - nr * li) / den
    bbr = qr[..., None] * bre - qi[..., None] * bim
    bbi = qr[..., None] * bim + qi[..., None] * bre
    return ar, ai, bbr, bbi


def _row(v, width=None):
    v = v.reshape(1, -1)
    if width is not None and v.shape[1] < width:
        v = jnp.pad(v, ((0, 0), (0, width - v.shape[1])))
    return v


def _relu2(a):
    r = jnp.maximum(a, 0.0)
    return r * r


def _add_alpha(acc, d):
    return acc + ALPHA * d


def _layer_params(full, small, l):
    p = {}
    w_in = full["w_in"]
    z, xbc, dt, u, xr, g = w_in[0:512], w_in[512:1536], w_in[1536:1544], w_in[1544:1800], w_in[1800:2056], w_in[2056:2312]
    p["w_inp"] = jnp.concatenate([xbc, z, u, xr, g, dt, jnp.zeros((D_INP - P_DT - 8, D_MODEL), w_in.dtype)], axis=0)
    for k_ in ("w_out", "xa_wq", "xa_wk", "xa_wv", "xa_wo", "mlp_w1", "mlp_w2"):
        p[k_] = full[k_]
    p["s5_glu_w"] = full["s5_glu_w"][l]
    p["ssd_cw"], p["ssd_cb"] = full["ssd_conv_w"][l], _row(small["ssd_conv_b"][l])
    dtb, alog, dsk = small["ssd_dt_bias"][l], small["ssd_a_log"][l], small["ssd_d"][l]
    p["prow"] = jnp.concatenate([_row(dtb, LANE), _row(alog, LANE), jnp.zeros((6, LANE), F32)], axis=0)
    p["ssd_dx"] = _row(jnp.repeat(dsk, SSD_HEAD_DIM))
    p["ssd_nw"] = _row(small["ssd_norm_w"][l])
    s5_in = (small["s5_lam_re"][l], small["s5_lam_im"][l], small["s5_log_step"][l], small["s5_b_re"][l], small["s5_b_im"][l])
    (ar, ai, bbr, bbi), p["s5_vjp"] = jax.vjp(_s5_disc, *s5_in)
    p["lam_fwd"] = jnp.concatenate([_row(ar), _row(ai)], axis=1)
    p["lam_adj"] = jnp.concatenate([_row(ar), _row(-ai)], axis=1)
    p["bcat"] = jnp.concatenate([_blockdiag(jnp.swapaxes(bbr, 1, 2)), _blockdiag(jnp.swapaxes(bbi, 1, 2))], axis=1)
    p["ccat"] = jnp.concatenate([_blockdiag(jnp.swapaxes(small["s5_c_re"][l], 1, 2)),
                                 -_blockdiag(jnp.swapaxes(small["s5_c_im"][l], 1, 2))], axis=0)
    p["s5_d"], p["s5_glu_b"] = _row(small["s5_d"][l]), _row(small["s5_glu_b"][l])
    p["rg_cw"], p["rg_cb"] = full["rg_conv_w"][l], _row(small["rg_conv_b"][l])
    p["rg_wa"], p["rg_wx"] = _blockdiag(small["rg_wa"][l]), _blockdiag(small["rg_wx"][l])
    p["rg_ba"], p["rg_bx"], p["rg_lam"] = _row(small["rg_ba"][l]), _row(small["rg_bx"][l]), _row(small["rg_lambda"][l])
    for i in (1, 2, 3):
        p[f"g{i}"], p[f"b{i}"] = _row(small[f"ln{i}_g"][l]), _row(small[f"ln{i}_b"][l])
    return p


def _take_side(res, n_out, got):
    res = res if isinstance(res, (list, tuple)) else (res,)
    got.extend(res[n_out:])
    return res[0] if n_out == 1 else res[:n_out]


def _layer_fwd(h0, mem, p, sides={}):
    t = h0.shape[0]
    s = {"h0": h0}
    got = []
    proj = _take_side(mm(h0, p["w_inp"], tb=True, name="in_proj", side=sides.get("in_proj")), 1, got)
    y_ssd, yraw, sall = ssd_fwd(proj, p["ssd_cw"], p["ssd_cb"], p["prow"], p["ssd_dx"], p["ssd_nw"], name="ssd_fwd")
    hs5, ylin, y_s5 = s5_fwd(proj, p["bcat"], p["lam_fwd"], p["ccat"], p["s5_d"], p["s5_glu_w"], p["s5_glu_b"], name="s5_fwd")
    rg_prm = (p["rg_cw"], p["rg_cb"], p["rg_wa"], p["rg_wx"], p["rg_ba"], p["rg_bx"], p["rg_lam"])
    y_rg, xc, a_rg, h_rg = rg_fwd(proj, *rg_prm, name="rg_fwd")
    ycat = jnp.concatenate([y_ssd, y_s5, y_rg], axis=1)
    pre1, h1 = mm_ln(ycat, p["w_out"], h0, p["g1"], p["b1"], name="out_proj")
    q = mm(h1, p["xa_wq"], name="xa_q", out_dtype=BF16)
    k = mm(mem, p["xa_wk"], name="xa_kv")
    v = mm(mem, p["xa_wv"], name="xa_kv")
    (o,), _ = rowk(_attn_fwd_fn, [(q, D_MODEL, 0)], [k, v], [D_MODEL], [], rows=t, name="xa_fwd", out_dtypes=[BF16])
    pre2, h2 = mm_ln(o, p["xa_wo"], h1, p["g2"], p["b2"], name="xa_o")
    a_mlp = _take_side(mm(h2, p["mlp_w1"], tb=True, name="mlp_up", side=sides.get("mlp_up")), 1, got)
    pre3, h3 = _take_side(mm_ln(a_mlp, p["mlp_w2"], h2, p["g3"], p["b3"], fa=_relu2, name="mlp_down",
                                side=sides.get("mlp_down")), 2, got)
    s.update(proj=proj, yraw=yraw, sall=sall, hs5=hs5, ylin=ylin, xc=xc, a_rg=a_rg, h_rg=h_rg,
             ycat=ycat, pre1=pre1, h1=h1, q=q, k=k, v=v, o=o, pre2=pre2, h2=h2, a_mlp=a_mlp, pre3=pre3)
    return h3, s, got


def _layer_bwd(dh3, mem, p, s, l, gfull, gsmall, sides={}):
    t = dh3.shape[0]
    proj = s["proj"]
    dpre3, dg3, db3 = ln_bwd(s["pre3"], dh3, p["g3"], name="ln_bwd")
    got = []
    da = _take_side(mm(dpre3, p["mlp_w2"], tb=True, o_extra=(s["a_mlp"],), fo=lambda acc, a: acc * 2.0 * jnp.maximum(a, 0.0),
                       name="mlp_da", out_dtype=BF16, side=sides.get("mlp_da")), 1, got)
    gfull["mlp_w2"][l] = mm(s["a_mlp"], dpre3, ta=True, fa=_relu2, name="mlp_dw2")
    gfull["mlp_w1"][l] = mm(da, s["h2"], ta=True, name="mlp_dw1")
    dh2 = _take_side(mm(da, p["mlp_w1"], o_extra=(dpre3,), fo=_add_alpha, name="mlp_dx", side=sides.get("mlp_dx")), 1, got)
    dpre2, dg2, db2 = ln_bwd(s["pre2"], dh2, p["g2"], name="ln_bwd")
    do = _take_side(mm(dpre2, p["xa_wo"], tb=True, name="xa_do", out_dtype=BF16, side=sides.get("xa_do")), 1, got)
    gfull["xa_wo"][l] = mm(s["o"], dpre2, ta=True, name="dw_sq")
    (dq,), (dk, dv) = rowk(_attn_bwd_fn, [(s["q"], D_MODEL, 0), (do, D_MODEL, 0)], [s["k"], s["v"]], [D_MODEL],
                           [(256, D_MODEL), (256, D_MODEL)], rows=t, name="xa_bwd", out_dtypes=[BF16])
    gfull["xa_wq"][l] = mm(s["h1"], dq, ta=True, name="dw_sq")
    gfull["xa_wk"][l] = mm(mem, dk, ta=True, name="dw_kv")
    gfull["xa_wv"][l] = mm(mem, dv, ta=True, name="dw_kv")
    dh1 = mm(dq, p["xa_wq"], tb=True, o_extra=(dpre2,), fo=_add_alpha, name="dx_sq")
    dpre1, dg1, db1 = ln_bwd(s["pre1"], dh1, p["g1"], name="ln_bwd")
    dycat = mm(dpre1, p["w_out"], tb=True, name="xa_do")
    gfull["w_out"][l] = mm(s["ycat"], dpre1, ta=True, name="dw_sq")
    rg_prm = (p["rg_cw"], p["rg_cb"], p["rg_wa"], p["rg_wx"], p["rg_ba"], p["rg_bx"], p["rg_lam"])
    dxr, dg_rg, d_rgcw, d_rgcb, dwa, dwx, dba, dbx, dlam = rg_bwd(proj, dycat, s["xc"], s["a_rg"], s["h_rg"], *rg_prm, name="rg_bwd")
    du, dccat, dbcat, dar, dai, d_s5d, d_gluw, d_glub = s5_bwd(dycat, s["ylin"], s["hs5"], proj, p["bcat"], p["lam_adj"], p["ccat"],
                                                               p["s5_d"], p["s5_glu_w"], p["s5_glu_b"], name="s5_bwd")
    dxbc, dz, ddt, dprm, ddx, dnw, d_scw, d_scb = ssd_bwd(proj, p["ssd_cw"], p["ssd_cb"], p["prow"], p["ssd_dx"], p["ssd_nw"],
                                                         s["yraw"], s["sall"], dycat, name="ssd_bwd")
    dproj = jnp.concatenate([dxbc, dz, du, dxr, dg_rg, ddt, jnp.zeros((t, D_INP - P_DT - LANE), BF16)], axis=1)
    dh0 = mm(dproj, p["w_inp"], o_extra=(dpre1,), fo=_add_alpha, name="in_proj_dx")
    dwp = mm(dproj, s["h0"], ta=True, name="in_proj_dw")
    gfull["w_in"][l] = jnp.concatenate([dwp[P_Z:P_Z + 512], dwp[P_XBC:P_XBC + 1024], dwp[P_DT:P_DT + 8],
                                        dwp[P_U:P_U + 256], dwp[P_XR:P_XR + 256], dwp[P_G:P_G + 256]], axis=0)
    gfull["ssd_conv_w"][l], gfull["rg_conv_w"][l], gfull["s5_glu_w"][l] = d_scw, d_rgcw, d_gluw
    ng, ns = S5_GROUPS, S5_STATE
    dbbr = jnp.swapaxes(_blockdiag_extract(dbcat[:, :S5_NSTATE], ng), 1, 2)
    dbbi = jnp.swapaxes(_blockdiag_extract(dbcat[:, S5_NSTATE:], ng), 1, 2)
    d_lr, d_li, d_ls, d_bre, d_bim = p["s5_vjp"]((dar.reshape(ng, ns), dai.reshape(ng, ns), dbbr, dbbi))
    gsmall["s5_lam_re"][l], gsmall["s5_lam_im"][l], gsmall["s5_log_step"][l] = d_lr, d_li, d_ls
    gsmall["s5_b_re"][l], gsmall["s5_b_im"][l] = d_bre, d_bim
    gsmall["s5_c_re"][l] = jnp.swapaxes(_blockdiag_extract(dccat[:S5_NSTATE], ng), 1, 2)
    gsmall["s5_c_im"][l] = -jnp.swapaxes(_blockdiag_extract(dccat[S5_NSTATE:], ng), 1, 2)
    gsmall["s5_d"][l], gsmall["s5_glu_b"][l] = d_s5d[0], d_glub[0]
    gsmall["ssd_conv_b"][l], gsmall["rg_conv_b"][l] = d_scb[0], d_rgcb[0]
    gsmall["ssd_dt_bias"][l], gsmall["ssd_a_log"][l] = dprm[0, :8], dprm[1, :8]
    gsmall["ssd_d"][l] = ddx.reshape(SSD_HEADS, SSD_HEAD_DIM).sum(axis=1)
    gsmall["ssd_norm_w"][l] = dnw[0]
    gsmall["rg_wa"][l], gsmall["rg_wx"][l] = _blockdiag_extract(dwa, RG_BLOCKS), _blockdiag_extract(dwx, RG_BLOCKS)
    gsmall["rg_ba"][l], gsmall["rg_bx"][l] = dba.reshape(RG_BLOCKS, RG_BLOCK_DIM), dbx.reshape(RG_BLOCKS, RG_BLOCK_DIM)
    gsmall["rg_lambda"][l] = dlam[0]
    for i, (dg, db) in zip((1, 2, 3), ((dg1, db1), (dg2, db2), (dg3, db3))):
        gsmall[f"ln{i}_g"][l], gsmall[f"ln{i}_b"][l] = dg[0], db[0]
    return dh0, got


def _step(a):
    h = a["x"][0]
    mem = a["mem"][0]
    t = h.shape[0]
    r4, r3 = PACK_ROWS // 4, 3 * PACK_ROWS // 8

    def my_shards(pre):
        return ({name: (jnp.swapaxes(a[pre + name], 1, 2) if tr else a[pre + name]) for name, tr, _ in BIG},
                [a[pre + name] for name, _ in TINY])

    def my_pack(pre, l):
        big, tiny = my_shards(pre)
        return _pack_layer({name: w[l] for name, w in big.items()},
                           jnp.concatenate([w.reshape(-1) for w in tiny]) if l == 0 else None)

    big, tiny = my_shards("")
    tiny16 = [(lax.bitcast_convert_type(w, BF16) if name in KEEP_F32 else w.astype(BF16)).reshape(-1)
              for (name, _), w in zip(TINY, tiny)]
    packed = [_pack_layer({name: w[l].astype(BF16) for name, w in big.items()}, jnp.concatenate(tiny16) if l == 0 else None)
              for l in range(DEPTH)]
    small = {name: a[name] for name in SMALL}

    def gathered_weights(g):
        gbig, gtiny = _unpack_layer(g)
        return {name: w.reshape(-1, WIDE) for name, w in gbig.items()}, gtiny

    full, gtiny = gathered_weights(all_gather(packed[0], name="ag_weights"))
    tiny_shapes = [w.shape + ((2,) if name in KEEP_F32 else ()) for (name, _), w in zip(TINY, tiny)]
    tiny_full = {name: _to_full(lax.bitcast_convert_type(g, F32) if name in KEEP_F32 else g, axis)
                 for (name, axis), g in zip(TINY, _split_flat(gtiny, tiny_shapes))}
    p0 = _layer_params({**full, **tiny_full}, small, 0)
    h, s0, got = _layer_fwd(h, mem, p0, sides={"in_proj": ("gather", packed[1], 0, r4), "mlp_up": ("gather", packed[1], r4, r3),
                                                "mlp_down": ("gather", packed[1], r4 + r3, r3)})
    full, _ = gathered_weights(jnp.concatenate(got, axis=1))
    p1 = _layer_params({**full, **tiny_full}, small, 1)
    h, s1, _ = _layer_fwd(h, mem, p1)
    (dh,), (loss_part,) = rowk(_loss_fn, [(h, D_MODEL, 0), (a["loss_target"][0], D_MODEL, 0)], [], [D_MODEL], [(1, 1)],
                               rows=t, name="loss_head", tt=2 * ROW_TILE)
    loss = lax.psum(loss_part[0, 0], ("x", "y", "c"))
    gfull = {name: [None] * DEPTH for name in SHARDED}
    gsmall = {name: [None] * DEPTH for name in SMALL}

    def chip_partials(l):
        gbig = {name: gfull[name][l].reshape(N_DEV, rows, WIDE) for name, _, rows in BIG}
        gtiny = None
        if l == 0:
            gtiny = jnp.concatenate([_to_slabs(jnp.stack(gfull[name]), axis).reshape(N_DEV, -1) for name, axis in TINY], axis=1)
        slabs = _pack_layer(gbig, gtiny)
        halves = jnp.swapaxes(slabs.reshape((4, 2) + slabs.shape[1:]), 0, 1)
        theirs = rs_sibling_exchange(halves, name="rs_sibling")
        return pair_sum_bf16(halves, theirs, name="rs_pair_sum")

    dh, _ = _layer_bwd(dh, mem, p1, s1, 1, gfull, gsmall)
    part1 = chip_partials(1)
    dh, got = _layer_bwd(dh, mem, p0, s0, 0, gfull, gsmall, sides={"mlp_da": ("chips", part1, 0, r3), "mlp_dx": ("chips", part1, r3, r3),
                                                                    "xa_do": ("chips", part1, 2 * r3, r4)})
    grad_x = dh[None]
    landed = [rs_chip_exchange(chip_partials(0), name="rs_chips"), jnp.concatenate(got, axis=1)]
    bigs = [adamw(landed[l], my_pack("", l), my_pack("m_", l), my_pack("v_", l), name="adamw_sharded", tt=256) for l in range(DEPTH)]
    gs = _pack_rows(jnp.concatenate([jnp.stack(gsmall[name]).reshape(-1) for name in SMALL]), 8)
    gs = all_gather(gs, name="ag_small_grads")
    pks = lambda pre: _pack_rows(jnp.concatenate([a[pre + name].reshape(-1) for name in SMALL]), 8)
    sm = adamw(gs, pks(""), pks("m_"), pks("v_"), name="adamw_replicated", tt=gs.shape[1])
    out = {}
    for i, kind in enumerate(("grad_", "delta_", "new_m_", "new_v_")):
        layers = [_unpack_layer(bigs[l][i]) for l in range(DEPTH)]
        for name, tr, _ in BIG:
            arr = jnp.stack([layers[l][0][name] for l in range(DEPTH)])
            out[kind + name] = jnp.swapaxes(arr, 1, 2) if tr else arr
        for (name, _), arr in zip(TINY, _split_flat(layers[0][1], [w.shape for w in tiny])):
            out[kind + name] = arr
        for name, arr in zip(SMALL, _unpack(sm[i], [a[name].shape for name in SMALL])):
            out[kind + name] = arr
    return (loss, grad_x) + tuple(out[kind + name] for kind in ("grad_", "delta_", "new_m_", "new_v_") for name in WEIGHTS)


def kernel(x, mem, w_in, w_out, ssd_conv_w, ssd_conv_b, ssd_dt_bias, ssd_a_log, ssd_d, ssd_norm_w, s5_lam_re, s5_lam_im, s5_log_step, s5_b_re, s5_b_im, s5_c_re, s5_c_im, s5_d, s5_glu_w, s5_glu_b, rg_conv_w, rg_conv_b, rg_wa, rg_ba, rg_wx, rg_bx, rg_lambda, ln1_g, ln1_b, xa_wq, xa_wk, xa_wv, xa_wo, ln2_g, ln2_b, mlp_w1, mlp_w2, ln3_g, ln3_b, loss_target, m_w_in, m_w_out, m_ssd_conv_w, m_ssd_conv_b, m_ssd_dt_bias, m_ssd_a_log, m_ssd_d, m_ssd_norm_w, m_s5_lam_re, m_s5_lam_im, m_s5_log_step, m_s5_b_re, m_s5_b_im, m_s5_c_re, m_s5_c_im, m_s5_d, m_s5_glu_w, m_s5_glu_b, m_rg_conv_w, m_rg_conv_b, m_rg_wa, m_rg_ba, m_rg_wx, m_rg_bx, m_rg_lambda, m_ln1_g, m_ln1_b, m_xa_wq, m_xa_wk, m_xa_wv, m_xa_wo, m_ln2_g, m_ln2_b, m_mlp_w1, m_mlp_w2, m_ln3_g, m_ln3_b, v_w_in, v_w_out, v_ssd_conv_w, v_ssd_conv_b, v_ssd_dt_bias, v_ssd_a_log, v_ssd_d, v_ssd_norm_w, v_s5_lam_re, v_s5_lam_im, v_s5_log_step, v_s5_b_re, v_s5_b_im, v_s5_c_re, v_s5_c_im, v_s5_d, v_s5_glu_w, v_s5_glu_b, v_rg_conv_w, v_rg_conv_b, v_rg_wa, v_rg_ba, v_rg_wx, v_rg_bx, v_rg_lambda, v_ln1_g, v_ln1_b, v_xa_wq, v_xa_wk, v_xa_wv, v_xa_wo, v_ln2_g, v_ln2_b, v_mlp_w1, v_mlp_w2, v_ln3_g, v_ln3_b):
    return _step(dict(locals()))
```

```python
import math

import jax
import jax.numpy as jnp
from jax import lax
from jax.experimental import pallas as pl
from jax.experimental.pallas import tpu as pltpu

F32 = jnp.float32
BF16 = jnp.bfloat16

N_DEV = 8
D_MODEL = 1024
DEPTH = 2
SSD_WIDTH = 512
SSD_HEADS = 8
SSD_HEAD_DIM = 64
SSD_STATE = 128
SSD_CHUNK = 128
SSD_XBC = 1024
S5_WIDTH = 256
S5_GROUPS = 16
S5_STATE = 64
S5_NSTATE = S5_GROUPS * S5_STATE
RG_WIDTH = 256
RG_BLOCKS = 4
RG_BLOCK_DIM = 64
RG_C = 8.0
XA_HEADS = 4
XA_HEAD_DIM = 256
ALPHA = (2.0 * DEPTH) ** 0.25
LN_EPS = 1e-5
ADAM_LR, ADAM_B1, ADAM_B2, ADAM_EPS, ADAM_WD, ADAM_STEP = 0.001, 0.9, 0.999, 1e-08, 0.01, 10

P_XBC, P_Z, P_U, P_XR, P_G, P_DT = 0, 1024, 1536, 1792, 2048, 2304
D_INP = 2560
LANE = 128
VMEM_LIMIT = 56 * 1024 * 1024
ROW_TILE = 512

_NN = ((1,), (0,))
_NT = ((1,), (1,))
_TN = ((0,), (0,))


def _dot(a, b, dims=_NN):
    return lax.dot_general(a.astype(BF16), b.astype(BF16), (dims, ((), ())), preferred_element_type=F32)


def _split_bf16(x, parts):
    out, rem = [], x
    for _ in range(parts):
        piece = rem.astype(BF16)
        out.append(piece)
        rem = rem - piece.astype(F32)
    return out


def _dot_mask(a, b, dims=_NN, *, mask_left, parts):
    if mask_left:
        return sum(_dot(a, piece, dims) for piece in _split_bf16(b, parts))
    return sum(_dot(piece, b, dims) for piece in _split_bf16(a, parts))


def _sigmoid(x):
    return 1.0 / (1.0 + jnp.exp(-x))


def _silu(x):
    return x * _sigmoid(x)


def _dsilu(x):
    s = _sigmoid(x)
    return s * (1.0 + x * (1.0 - s))


_GK = math.sqrt(2.0 / math.pi)
_GC = 0.044715


def _gelu(x):
    return 0.5 * x * (1.0 + jnp.tanh(_GK * (x + _GC * x * x * x)))


def _dgelu(x):
    th = jnp.tanh(_GK * (x + _GC * x * x * x))
    return 0.5 * (1.0 + th) + 0.5 * x * (1.0 - th * th) * _GK * (1.0 + 3.0 * _GC * x * x)


def _log1p_pos(e):
    return jnp.where(e < 1e-2, e * (1.0 - e * (0.5 - e * (1.0 / 3.0))), jnp.log(1.0 + e))


def _softplus(x):
    return jnp.maximum(x, 0.0) + _log1p_pos(jnp.exp(-jnp.abs(x)))


def _neg_expm1(x):
    poly = -x * (1.0 + x * (0.5 + x * (1.0 / 6.0 + x * (1.0 / 24.0 + x * (1.0 / 120.0)))))
    return jnp.where(x > -0.05, poly, 1.0 - jnp.exp(x))


def _params(sem):
    return pltpu.CompilerParams(dimension_semantics=sem, vmem_limit_bytes=VMEM_LIMIT)


RESIDENT_BYTES = 8 * 1024 * 1024
STREAM_BYTES = 8 * 1024 * 1024


def _halve_to_fit(dims, bytes_per, limit):
    dims = list(dims)
    while math.prod(dims) * bytes_per > limit:
        i = max(range(len(dims)), key=lambda d: dims[d])
        assert dims[i] % 256 == 0, dims
        dims[i] //= 2
    return dims


def _side_exchange(side, src, dst, sems, step, nsteps):
    kind, _, r0, rows = side
    span = pl.ds(r0, rows)
    if kind == "gather":
        phases = lambda: _ag_phases(src.at[span], dst, *sems)
        when = (0, (3 * nsteps) // 4, nsteps - 1)
    else:
        phases = lambda: _rs_chip_phases(src, dst, *sems, rows=span)
        when = (0, nsteps - 1)
    for idx, at in enumerate(when):
        pl.when(step == at)(lambda idx=idx: phases()[idx]())


def mm(a, b, *, name, ta=False, tb=False, a_extra=(), fa=None, o_extra=(), r_extra=(), fo=None, n_out=1,
       a_off=0, m=None, k=None, out_dtype=F32, side=None):
    n = b.shape[0] if tb else b.shape[1]
    na, no, nr = 1 + len(a_extra), len(o_extra), len(r_extra)
    if not ta:
        assert m is None
        m, kdim = a.shape[0], (a.shape[1] if k is None else k)
        assert a_off % kdim == 0
        (tn,) = _halve_to_fit([n], kdim * b.dtype.itemsize, RESIDENT_BYTES)
        (tm,) = _halve_to_fit([min(512, m)], max(tn, kdim) * 4, STREAM_BYTES)
        a_spec = pl.BlockSpec((tm, kdim), lambda i, j: (i, a_off // kdim))
        b_spec = pl.BlockSpec((tn, kdim), lambda i, j: (j, 0)) if tb else pl.BlockSpec((kdim, tn), lambda i, j: (0, j))
        o_spec = pl.BlockSpec((tm, tn), lambda i, j: (i, j))
        dims = _NT if tb else _NN

        r_spec = pl.BlockSpec((1, tn), lambda i, j: (0, j))

        grid = (m // tm, n // tn)
        nin = na + 1 + no + nr

        def body(*refs):
            a_refs, b_ref, e_refs, out_refs = refs[:na], refs[na], refs[na + 1:nin], refs[nin + (side is not None):nin + (side is not None) + n_out]
            if side is not None:
                _side_exchange(side, refs[nin], refs[nin + 1 + n_out], refs[nin + 2 + n_out:],
                               pl.program_id(0) * grid[1] + pl.program_id(1), grid[0] * grid[1])
            av = a_refs[0][...] if fa is None else fa(*[r[...] for r in a_refs])
            acc = _dot(av, b_ref[...], dims)
            res = acc if fo is None else fo(acc, *[r[...] for r in e_refs])
            for r, v in zip(out_refs, res if n_out > 1 else (res,)):
                r[...] = v.astype(r.dtype)

        sem = ("parallel", "parallel") if side is None else ("arbitrary", "arbitrary")
    else:
        assert k is None and not tb and fo is None and not o_extra and not r_extra and n_out == 1 and out_dtype == F32
        assert side is None
        kdim, m = a.shape[0], (a.shape[1] if m is None else m)
        r_spec = None
        tm, tn = _halve_to_fit([m, n], 4, RESIDENT_BYTES)
        (tk,) = _halve_to_fit([min(512, kdim)], max(tm, tn) * 4, STREAM_BYTES)
        assert a_off % tm == 0
        a_spec = pl.BlockSpec((tk, tm), lambda i, j, kk: (kk, i + a_off // tm))
        b_spec = pl.BlockSpec((tk, tn), lambda i, j, kk: (kk, j))
        o_spec = pl.BlockSpec((tm, tn), lambda i, j, kk: (i, j))

        def body(*refs):
            a_refs, b_ref, out_ref = refs[:na], refs[na], refs[na + 1]

            @pl.when(pl.program_id(2) == 0)
            def _():
                out_ref[...] = jnp.zeros_like(out_ref)

            av = a_refs[0][...] if fa is None else fa(*[r[...] for r in a_refs])
            out_ref[...] += _dot(av, b_ref[...], _TN)

        grid, sem = (m // tm, n // tn, kdim // tk), ("parallel", "parallel", "arbitrary")
    assert m % tm == 0 and n % tn == 0, (name, m, n, tm, tn)
    out = jax.ShapeDtypeStruct((m, n), out_dtype)
    if side is None:
        return pl.pallas_call(
            body, name=name, grid=grid,
            in_specs=[a_spec] * na + [b_spec] + [o_spec] * no + [r_spec] * nr,
            out_specs=o_spec if n_out == 1 else [o_spec] * n_out, out_shape=out if n_out == 1 else [out] * n_out,
            compiler_params=_params(sem),
        )(a, *a_extra, b, *o_extra, *r_extra)
    kind, arr, _, rows = side
    landed = jax.ShapeDtypeStruct(((N_DEV, rows) if kind == "gather" else (4, rows)) + arr.shape[-1:], arr.dtype)
    return pl.pallas_call(
        body, name=name, grid=grid,
        in_specs=[a_spec] * na + [b_spec] + [o_spec] * no + [r_spec] * nr + [_ANY],
        out_specs=[o_spec] * n_out + [_ANY], out_shape=[out] * n_out + [landed],
        scratch_shapes=list(_AG_SEMS if kind == "gather" else _RS_SEMS),
        compiler_params=_params(sem),
    )(a, *a_extra, b, *o_extra, *r_extra, arr)


def rowk(fn, tiled, full, out_w, acc_shapes, *, rows, name, out_dtypes=None, tt=ROW_TILE):
    tt = min(tt, rows)
    n = rows // tt
    assert rows % tt == 0
    nt, nf, no = len(tiled), len(full), len(out_w)

    def tspec(w, cb):
        return pl.BlockSpec((tt, w), lambda i: (i, cb))

    def fspec(a):
        nd = a.ndim
        return pl.BlockSpec(a.shape, lambda i: (0,) * nd)

    def body(*refs):
        ins, fulls = refs[:nt], refs[nt:nt + nf]
        outs, accs = refs[nt + nf:nt + nf + no], refs[nt + nf + no:]
        res_t, res_a = fn(*[r[...] for r in ins], *[r[...] for r in fulls])
        for r, v in zip(outs, res_t):
            r[...] = v.astype(r.dtype)
        if accs:
            @pl.when(pl.program_id(0) == 0)
            def _():
                for r in accs:
                    r[...] = jnp.zeros_like(r)
            for r, v in zip(accs, res_a):
                r[...] += v

    outs = pl.pallas_call(
        body, name=name, grid=(n,),
        in_specs=[tspec(w, cb) for (_, w, cb) in tiled] + [fspec(a) for a in full],
        out_specs=[tspec(w, 0) for w in out_w] + [pl.BlockSpec(s, lambda i, nd=len(s): (0,) * nd) for s in acc_shapes],
        out_shape=[jax.ShapeDtypeStruct((rows, w), dt) for w, dt in zip(out_w, out_dtypes or [F32] * no)]
        + [jax.ShapeDtypeStruct(s, F32) for s in acc_shapes],
        compiler_params=_params(("arbitrary",)),
    )(*[a for (a, _, _) in tiled], *full)
    return outs[:no], outs[no:]


def _colsum(x):
    return jnp.sum(x, axis=0, keepdims=True)


def _rowsum(x):
    return jnp.sum(x, axis=1, keepdims=True)


def _ln_epilogue(acc, resid, g, b):
    pre = ALPHA * resid + acc
    mu = jnp.mean(pre, axis=1, keepdims=True)
    xc = pre - mu
    var = jnp.mean(xc * xc, axis=1, keepdims=True)
    return pre, xc * lax.rsqrt(var + LN_EPS) * g + b


def _ln_bwd_fn(pre, dout, g):
    mu = jnp.mean(pre, axis=1, keepdims=True)
    xc = pre - mu
    var = jnp.mean(xc * xc, axis=1, keepdims=True)
    rstd = lax.rsqrt(var + LN_EPS)
    xhat = xc * rstd
    dxh = dout * g
    dpre = rstd * (dxh - jnp.mean(dxh, axis=1, keepdims=True) - xhat * jnp.mean(dxh * xhat, axis=1, keepdims=True))
    return (dpre,), (_colsum(dout * xhat), _colsum(dout))


def mm_ln(a, w, resid, g, b, *, name, fa=None, side=None):
    assert w.shape[1] == D_MODEL
    return mm(a, w, fa=fa, o_extra=(resid,), r_extra=(g, b), fo=_ln_epilogue, n_out=2, name=name, side=side)


def ln_bwd(pre, dout, g, *, name):
    (dpre,), (dg, db) = rowk(_ln_bwd_fn, [(pre, D_MODEL, 0), (dout, D_MODEL, 0)], [g],
                             [D_MODEL], [(1, D_MODEL), (1, D_MODEL)], rows=pre.shape[0], name=name, tt=2 * ROW_TILE)
    return dpre, dg, db


def _loss_fn(y, tgt):
    e = y - tgt
    part = _colsum(_rowsum(e * e)) * (0.5 / D_MODEL)
    return (e * (1.0 / D_MODEL),), (part,)


_XA_SCALE = 1.0 / math.sqrt(XA_HEAD_DIM)


def _attn_probs(qh, kh):
    s = _dot(qh, kh, _NT) * _XA_SCALE
    e = jnp.exp(s - jnp.max(s, axis=1, keepdims=True))
    return e / _rowsum(e)


def _attn_fwd_fn(q, k, v):
    outs = []
    for hd in range(XA_HEADS):
        sl = slice(hd * XA_HEAD_DIM, (hd + 1) * XA_HEAD_DIM)
        outs.append(_dot(_attn_probs(q[:, sl], k[:, sl]), v[:, sl]))
    return (jnp.concatenate(outs, axis=1),), ()


def _attn_bwd_fn(q, do, k, v):
    dqs, dks, dvs = [], [], []
    for hd in range(XA_HEADS):
        sl = slice(hd * XA_HEAD_DIM, (hd + 1) * XA_HEAD_DIM)
        qh, kh, vh, doh = q[:, sl], k[:, sl], v[:, sl], do[:, sl]
        p = _attn_probs(qh, kh)
        dp = _dot(doh, vh, _NT)
        ds = p * (dp - _rowsum(p * dp)) * _XA_SCALE
        dqs.append(_dot(ds, kh))
        dks.append(_dot(ds, qh, _TN))
        dvs.append(_dot(p, doh, _TN))
    cat = lambda xs: jnp.concatenate(xs, axis=1)
    return (cat(dqs),), (cat(dks), cat(dvs))


def _s5_post_fwd_fn(ylin, u, dskip, gw, gb):
    yg = _gelu(ylin + dskip * u)
    return (yg * _sigmoid(_dot(yg, gw) + gb),), ()


def _s5_post_bwd_fn(ylin, u, dout, dskip, gw, gb):
    pre = ylin + dskip * u
    yg = _gelu(pre)
    sg = _sigmoid(_dot(yg, gw) + gb)
    dlin = dout * yg * sg * (1.0 - sg)
    dyg = dout * sg + _dot(dlin, gw, _NT)
    dpre = dyg * _dgelu(pre)
    return (dpre, dpre * dskip), (_colsum(dpre * u), _dot(yg, dlin, _TN), _colsum(dlin))


def _rg_gates(xc, wa, wx, ba, bx, lam):
    r = _sigmoid(_dot(xc, wa) + ba)
    i = _sigmoid(_dot(xc, wx) + bx)
    sp = _softplus(-lam)
    log_a = -RG_C * r * sp
    a = jnp.exp(log_a)
    mult = jnp.sqrt(_neg_expm1(2.0 * log_a))
    return r, i, sp, a, mult


def _rg_pre_bwd_fn(xc, gsc, hprev, wa, wx, ba, bx, lam):
    r, i, sp, a, mult = _rg_gates(xc, wa, wx, ba, bx, lam)
    da = gsc * hprev
    db = gsc
    dmult = db * i * xc
    di = db * mult * xc
    dxc = db * mult * i
    dlog_a = da * a - a * a * dmult / mult
    dr = dlog_a * (-RG_C * sp)
    dsp = _colsum(dlog_a * (-RG_C * r))
    dlam = dsp * (-_sigmoid(-lam))
    dpr = dr * r * (1.0 - r)
    dpi = di * i * (1.0 - i)
    dxc = dxc + _dot(dpr, wa, _NT) + _dot(dpi, wx, _NT)
    return (dxc,), (_dot(xc, dpr, _TN), _dot(xc, dpi, _TN), _colsum(dpr), _colsum(dpi), dlam)


def _conv_taps(x_ref, halo_ref, first):
    x = x_ref[...]
    halo = jnp.where(first, 0.0, halo_ref[...])
    rows8 = lax.broadcasted_iota(jnp.int32, halo.shape, 0)
    taps = [x]
    for j in (1, 2, 3):
        r = pltpu.roll(x, j, 0)
        top = jnp.where(rows8 < j, pltpu.roll(halo, j, 0), r[0:8])
        taps.append(jnp.concatenate([top, r[8:]], axis=0))
    return taps


def _conv_pre(taps, cw_ref, cb_ref):
    wv = cw_ref[...]
    pre = cb_ref[...] + wv[3:4, :] * taps[0]
    for j in (1, 2, 3):
        pre = pre + wv[3 - j:4 - j, :] * taps[j]
    return pre


def _conv_back(dpre, taps, cw_ref, nxt_ref):
    q = dpre.shape[0]
    rows8 = lax.broadcasted_iota(jnp.int32, (8, dpre.shape[1]), 0)
    wv = cw_ref[...]
    dx = wv[3:4, :] * dpre
    for j in (1, 2, 3):
        r = pltpu.roll(dpre, q - j, 0)
        bottom = jnp.where(rows8 >= 8 - j, pltpu.roll(nxt_ref[...], 8 - j, 0), r[q - 8:q])
        dx = dx + wv[3 - j:4 - j, :] * jnp.concatenate([r[:q - 8], bottom], axis=0)
    dw = jnp.concatenate([_colsum(dpre * taps[3 - kk]) for kk in range(4)], axis=0)
    nxt_ref[...] = dpre[0:8]
    return dx, dw, _colsum(dpre)


S5_CW = 256


def _cmul(ar, ai, br, bi):
    return ar * br - ai * bi, ar * bi + ai * br


def _scan8_complex(src_ref, dst_ref, lam_ref, st_ref, *, w, nb, reverse):
    rows = lax.broadcasted_iota(jnp.int32, (8, S5_CW), 0)
    b8 = lambda v: jnp.broadcast_to(v, (8, S5_CW))

    def shift(x, k):
        if reverse:
            return jnp.where(rows < 8 - k, pltpu.roll(x, 8 - k, 0), 0.0)
        return jnp.where(rows >= k, pltpu.roll(x, k, 0), 0.0)

    for c0 in range(0, w, S5_CW):
        re, im = pl.ds(c0, S5_CW), pl.ds(w + c0, S5_CW)
        pw = [(lam_ref[:, re], lam_ref[:, im])]
        for _ in range(7):
            pw.append(_cmul(*pw[-1], *pw[0]))
        pr, pi = b8(pw[7][0]), b8(pw[7][1])
        for j in range(7):
            sel = rows == (7 - j if reverse else j)
            pr, pi = jnp.where(sel, b8(pw[j][0]), pr), jnp.where(sel, b8(pw[j][1]), pi)
        steps = [(k, b8(pw[k - 1][0]), b8(pw[k - 1][1])) for k in (1, 2, 4)]
        edge = 0 if reverse else 7

        def blk(i, carry):
            hr, hi = carry
            base = pl.multiple_of((nb // 2 - 1 - i if reverse else i) * 16, 16)
            pend = []
            for off in ((8, 0) if reverse else (0, 8)):
                at = pl.ds(base + off, 8)
                xr, xi = src_ref[at, re], src_ref[at, im]
                for k, kr, ki in steps:
                    sr, si = shift(xr, k), shift(xi, k)
                    xr, xi = xr + kr * sr - ki * si, xi + kr * si + ki * sr
                pend.append((at, xr, xi))
            for at, xr, xi in pend:
                xr, xi = xr + pr * hr - pi * hi, xi + pr * hi + pi * hr
                dst_ref[at, re] = xr
                dst_ref[at, im] = xi
                hr, hi = b8(xr[edge:edge + 1, :]), b8(xi[edge:edge + 1, :])
            return hr, hi

        hr, hi = lax.fori_loop(0, nb // 2, blk, (st_ref[:, re], st_ref[:, im]))
        st_ref[:, re] = hr
        st_ref[:, im] = hi


def s5_fwd(proj, bcat, lam, ccat, dskip, gw, gb, *, name):
    t = proj.shape[0]
    tt = min(ROW_TILE, t)
    w2 = bcat.shape[1]

    def body(u_ref, b_ref, lam_ref, c_ref, d_ref, gw_ref, gb_ref, h_ref, y_ref, o_ref, bu_ref, st_ref):
        @pl.when(pl.program_id(0) == 0)
        def _():
            st_ref[...] = jnp.zeros_like(st_ref)

        u = u_ref[...]
        bu_ref[...] = _dot(u, b_ref[...])
        _scan8_complex(bu_ref, h_ref, lam_ref, st_ref, w=w2 // 2, nb=tt // 8, reverse=False)
        ylin = _dot(h_ref[...], c_ref[...])
        y_ref[...] = ylin
        (out,), _ = _s5_post_fwd_fn(ylin, u, d_ref[...], gw_ref[...], gb_ref[...])
        o_ref[...] = out.astype(o_ref.dtype)

    fixed = lambda a: pl.BlockSpec(a.shape, lambda i: (0, 0))
    row = pl.BlockSpec((tt, S5_WIDTH), lambda i: (i, 0))
    return pl.pallas_call(
        body, name=name, grid=(t // tt,),
        in_specs=[pl.BlockSpec((tt, S5_WIDTH), lambda i: (i, P_U // S5_WIDTH))] + [fixed(x) for x in (bcat, lam, ccat, dskip, gw, gb)],
        out_specs=[pl.BlockSpec((tt, w2), lambda i: (i, 0)), row, row],
        out_shape=[jax.ShapeDtypeStruct((t, w2), F32), jax.ShapeDtypeStruct((t, S5_WIDTH), F32),
                   jax.ShapeDtypeStruct((t, S5_WIDTH), BF16)],
        scratch_shapes=[pltpu.VMEM((tt, w2), F32), pltpu.VMEM((8, w2), F32)],
        compiler_params=_params(("arbitrary",)),
    )(proj, bcat, lam, ccat, dskip, gw, gb)


def s5_bwd(dycat, ylin, hs, proj, bcat, lam_adj, ccat, dskip, gw, gb, *, name):
    t = proj.shape[0]
    tt = min(ROW_TILE, t)
    n, w2 = t // tt, bcat.shape[1]
    w = w2 // 2

    def body(dout_ref, yl_ref, h_ref, hp_ref, u_ref, b_ref, lam_ref, c_ref, d_ref, gw_ref, gb_ref,
             du_ref, dc_ref, db_ref, dar_ref, dai_ref, dd_ref, dgw_ref, dgb_ref, g_ref, st_ref):
        i = pl.program_id(0)

        @pl.when(i == 0)
        def _():
            for r in (st_ref, dc_ref, db_ref, dar_ref, dai_ref, dd_ref, dgw_ref, dgb_ref):
                r[...] = jnp.zeros_like(r)

        (dy, du_a), post = _s5_post_bwd_fn(yl_ref[...], u_ref[...], dout_ref[...], d_ref[...], gw_ref[...], gb_ref[...])
        for r, v in zip((dd_ref, dgw_ref, dgb_ref), post):
            r[...] += v
        h = h_ref[...]
        g_ref[...] = _dot(dy, c_ref[...], _NT)
        dc_ref[...] += _dot(h, dy, _TN)
        _scan8_complex(g_ref, g_ref, lam_ref, st_ref, w=w, nb=tt // 8, reverse=True)
        g = g_ref[...]
        du_ref[...] = (du_a + _dot(g, b_ref[...], _NT)).astype(du_ref.dtype)
        db_ref[...] += _dot(u_ref[...], g, _TN)
        rows8 = lax.broadcasted_iota(jnp.int32, (8, w2), 0)

        def rows_of_8(kk, carry):
            ar, ai, last = carry
            at = pl.ds(pl.multiple_of(kk * 8, 8), 8)
            g8, h8 = g_ref[at, :], h_ref[at, :]
            hp = jnp.where(rows8 == 0, last, pltpu.roll(h8, 1, 0))
            gr, gi, hr, hi = g8[:, :w], g8[:, w:], hp[:, :w], hp[:, w:]
            return ar + gr * hr + gi * hi, ai + gi * hr - gr * hi, h8[7:8, :]

        zero = jnp.zeros((8, w), F32)
        ar, ai, _ = lax.fori_loop(0, tt // 8, rows_of_8, (zero, zero, jnp.where(i == n - 1, 0.0, hp_ref[7:8, :])))
        dar_ref[...] += _colsum(ar)
        dai_ref[...] += _colsum(ai)

    rev = lambda i: n - 1 - i
    row = lambda wd, cb=0: pl.BlockSpec((tt, wd), lambda i: (rev(i), cb))
    fixed = lambda shape: pl.BlockSpec(shape, lambda i: (0, 0))
    return pl.pallas_call(
        body, name=name, grid=(n,),
        in_specs=[row(S5_WIDTH, 2), row(S5_WIDTH), row(w2),
                  pl.BlockSpec((8, w2), lambda i: (jnp.maximum(rev(i) * (tt // 8) - 1, 0), 0)),
                  row(S5_WIDTH, P_U // S5_WIDTH)] + [fixed(x.shape) for x in (bcat, lam_adj, ccat, dskip, gw, gb)],
        out_specs=[row(S5_WIDTH), fixed(ccat.shape), fixed(bcat.shape), fixed((1, w)), fixed((1, w)),
                   fixed((1, S5_WIDTH)), fixed((S5_WIDTH, S5_WIDTH)), fixed((1, S5_WIDTH))],
        out_shape=[jax.ShapeDtypeStruct((t, S5_WIDTH), BF16), jax.ShapeDtypeStruct(ccat.shape, F32),
                   jax.ShapeDtypeStruct(bcat.shape, F32), jax.ShapeDtypeStruct((1, w), F32), jax.ShapeDtypeStruct((1, w), F32),
                   jax.ShapeDtypeStruct((1, S5_WIDTH), F32), jax.ShapeDtypeStruct((S5_WIDTH, S5_WIDTH), F32),
                   jax.ShapeDtypeStruct((1, S5_WIDTH), F32)],
        scratch_shapes=[pltpu.VMEM((tt, w2), F32), pltpu.VMEM((8, w2), F32)],
        compiler_params=_params(("arbitrary",)),
    )(dycat, ylin, hs, hs, proj, bcat, lam_adj, ccat, dskip, gw, gb)


def _scan8_real(a_ref, b_ref, o_ref, st_ref, *, nb, reverse):
    w = o_ref.shape[1]
    rows = lax.broadcasted_iota(jnp.int32, (8, w), 0)
    edge = 0 if reverse else 7

    def shift(x, k, fill):
        if reverse:
            return jnp.where(rows < 8 - k, pltpu.roll(x, 8 - k, 0), fill)
        return jnp.where(rows >= k, pltpu.roll(x, k, 0), fill)

    def blk(i, h):
        at = pl.ds(pl.multiple_of((nb - 1 - i if reverse else i) * 8, 8), 8)
        a, b = a_ref[at, :], b_ref[at, :]
        for k in (1, 2, 4):
            a, b = a * shift(a, k, 1.0), b + a * shift(b, k, 0.0)
        out = b + a * h
        o_ref[at, :] = out
        return jnp.broadcast_to(out[edge:edge + 1, :], (8, w))

    st_ref[...] = lax.fori_loop(0, nb, blk, st_ref[...])


def _rg_specs(tt, idx):
    return [pl.BlockSpec((tt, RG_WIDTH), lambda i: (idx(i), P_XR // RG_WIDTH)),
            pl.BlockSpec((8, RG_WIDTH), lambda i: (jnp.maximum(idx(i) * (tt // 8) - 1, 0), P_XR // RG_WIDTH)),
            pl.BlockSpec((tt, RG_WIDTH), lambda i: (idx(i), P_G // RG_WIDTH))]


def rg_fwd(proj, cw, cb, wa, wx, ba, bx, lam, *, name):
    t = proj.shape[0]
    tt = min(ROW_TILE, t)
    w = RG_WIDTH

    def body(x_ref, halo_ref, g_ref, cw_ref, cb_ref, wa_ref, wx_ref, ba_ref, bx_ref, lam_ref,
             y_ref, xc_ref, a_ref, h_ref, b_ref, st_ref):
        @pl.when(pl.program_id(0) == 0)
        def _():
            st_ref[...] = jnp.zeros_like(st_ref)

        xc = _conv_pre(_conv_taps(x_ref, halo_ref, pl.program_id(0) == 0), cw_ref, cb_ref)
        xc_ref[...] = xc
        r, i, sp, a, mult = _rg_gates(xc, wa_ref[...], wx_ref[...], ba_ref[...], bx_ref[...], lam_ref[...])
        a_ref[...] = a
        b_ref[...] = mult * (i * xc)
        _scan8_real(a_ref, b_ref, h_ref, st_ref, nb=tt // 8, reverse=False)
        y_ref[...] = (h_ref[...] * _gelu(g_ref[...])).astype(y_ref.dtype)

    fixed = lambda a: pl.BlockSpec(a.shape, lambda i: (0, 0))
    row = pl.BlockSpec((tt, w), lambda i: (i, 0))
    return pl.pallas_call(
        body, name=name, grid=(t // tt,),
        in_specs=_rg_specs(tt, lambda i: i) + [fixed(x) for x in (cw, cb, wa, wx, ba, bx, lam)],
        out_specs=[row] * 4,
        out_shape=[jax.ShapeDtypeStruct((t, w), BF16)] + [jax.ShapeDtypeStruct((t, w), F32)] * 3,
        scratch_shapes=[pltpu.VMEM((tt, w), F32), pltpu.VMEM((8, w), F32)],
        compiler_params=_params(("arbitrary",)),
    )(proj, proj, proj, cw, cb, wa, wx, ba, bx, lam)


def rg_bwd(proj, dycat, xc, a, h, cw, cb, wa, wx, ba, bx, lam, *, name):
    t = proj.shape[0]
    tt = min(ROW_TILE, t)
    n, w = t // tt, RG_WIDTH

    def body(x_ref, halo_ref, g_ref, dy_ref, xc_ref, a_ref, h_ref, hp_ref, cw_ref, wa_ref, wx_ref, ba_ref, bx_ref, lam_ref,
             dx_ref, dg_ref, dcw_ref, dcb_ref, dwa_ref, dwx_ref, dba_ref, dbx_ref, dlam_ref,
             au_ref, dh_ref, gs_ref, st_ref, anx_ref, nxt_ref):
        i = pl.program_id(0)
        accs = (dcw_ref, dcb_ref, dwa_ref, dwx_ref, dba_ref, dbx_ref, dlam_ref)

        @pl.when(i == 0)
        def _():
            for r in accs + (st_ref, anx_ref, nxt_ref):
                r[...] = jnp.zeros_like(r)

        h, g, dy, a = h_ref[...], g_ref[...], dy_ref[...], a_ref[...]
        dh_ref[...] = dy * _gelu(g)
        dg_ref[...] = (dy * h * _dgelu(g)).astype(dg_ref.dtype)
        rows = lax.broadcasted_iota(jnp.int32, (tt, w), 0)
        au_ref[...] = jnp.where(rows == tt - 1, anx_ref[0:1, :], pltpu.roll(a, tt - 1, 0))
        _scan8_real(au_ref, dh_ref, gs_ref, st_ref, nb=tt // 8, reverse=True)
        before = jnp.where(i == n - 1, 0.0, hp_ref[7:8, :])
        hprev = jnp.where(rows == 0, before, pltpu.roll(h, 1, 0))
        (dxc,), small = _rg_pre_bwd_fn(xc_ref[...], gs_ref[...], hprev, wa_ref[...], wx_ref[...], ba_ref[...], bx_ref[...], lam_ref[...])
        dx, dcw, dcb = _conv_back(dxc, _conv_taps(x_ref, halo_ref, i == n - 1), cw_ref, nxt_ref)
        dx_ref[...] = dx.astype(dx_ref.dtype)
        for r, v in zip(accs, (dcw, dcb) + tuple(small)):
            r[...] += v
        anx_ref[...] = a[0:8]

    rev = lambda i: n - 1 - i
    row = lambda cb_=0: pl.BlockSpec((tt, w), lambda i: (rev(i), cb_))
    fixed = lambda shape: pl.BlockSpec(shape, lambda i: (0, 0))
    acc_shapes = [(4, w), (1, w), (w, w), (w, w), (1, w), (1, w), (1, w)]
    return pl.pallas_call(
        body, name=name, grid=(n,),
        in_specs=_rg_specs(tt, rev) + [row(3), row(), row(), row(),
                                       pl.BlockSpec((8, w), lambda i: (jnp.maximum(rev(i) * (tt // 8) - 1, 0), 0))]
        + [fixed(x.shape) for x in (cw, wa, wx, ba, bx, lam)],
        out_specs=[row(), row()] + [fixed(sh) for sh in acc_shapes],
        out_shape=[jax.ShapeDtypeStruct((t, w), BF16)] * 2 + [jax.ShapeDtypeStruct(sh, F32) for sh in acc_shapes],
        scratch_shapes=[pltpu.VMEM((tt, w), F32)] * 3 + [pltpu.VMEM((8, w), F32)] * 3,
        compiler_params=_params(("arbitrary",)),
    )(proj, proj, proj, dycat, xc, a, h, h, cw, wa, wx, ba, bx, lam)


SSD_QQ = SSD_HEADS * SSD_CHUNK
SSD_GP = SSD_WIDTH // 2
SSD_GQ = SSD_QQ // 2


def _ssd_spread():
    h = jnp.arange(LANE)[:, None]
    spread_p = (jnp.arange(SSD_WIDTH)[None, :] // SSD_HEAD_DIM == h).astype(BF16)
    spread_q = (jnp.arange(SSD_QQ)[None, :] // SSD_CHUNK == h).astype(BF16)
    return spread_p, spread_q


def _ssd_prologue(dt_ref, prow_ref, sp_ref, sq_ref):
    q = SSD_CHUNK
    r = lax.broadcasted_iota(jnp.int32, (q, q), 0)
    c = lax.broadcasted_iota(jnp.int32, (q, q), 1)
    raw_c = dt_ref[...] + prow_ref[0:1, :]
    dt_c = _softplus(raw_c)
    a_r = -jnp.exp(prow_ref[1:2, :])
    cs_c = _dot_mask((r >= c).astype(F32), dt_c * a_r, mask_left=True, parts=3)
    both = _dot_mask(jnp.concatenate([dt_c, cs_c], axis=0), sp_ref[...], mask_left=False, parts=3)
    dt_x, cs_x = both[:q], both[q:]
    csx = _dot_mask(cs_c, sq_ref[...], mask_left=False, parts=3)
    rr = lax.broadcasted_iota(jnp.int32, (q, SSD_QQ), 0)
    ss = lax.broadcasted_iota(jnp.int32, (q, SSD_QQ), 1) & (q - 1)
    diag = rr == ss
    cs_row = _colsum(jnp.where(diag, csx, 0.0))
    lcat = jnp.exp(jnp.where(rr >= ss, csx - cs_row, -1e30))
    cl = cs_x[q - 1:q, :]
    return dict(raw_c=raw_c, dt_c=dt_c, a_r=a_r, dt_x=dt_x, cs_x=cs_x, lcat=lcat, diag=diag,
                ecs=jnp.exp(cs_x), wdec=jnp.exp(cl - cs_x), ecl=jnp.exp(cl), triu=(r <= c).astype(F32))


def _ssd_group(xbc_ref, g, lcat, xdt):
    ns, q = SSD_STATE, SSD_CHUNK
    bm = xbc_ref[:, pl.ds(SSD_WIDTH + g * ns, ns)]
    cm = xbc_ref[:, pl.ds(SSD_WIDTH + 2 * ns + g * ns, ns)]
    cb = _dot(cm, bm, _NT)
    lg = lcat[:, g * SSD_GQ:(g + 1) * SSD_GQ]
    wcat = jnp.concatenate([cb] * 4, axis=1) * lg
    head = lax.broadcasted_iota(jnp.int32, (1, SSD_GP), 1) // SSD_HEAD_DIM
    xg = xdt[:, g * SSD_GP:(g + 1) * SSD_GP]
    xbd = jnp.concatenate([jnp.where(head == j, xg, 0.0) for j in range(4)], axis=0)
    return bm, cm, lg, wcat, xbd, head


def _ssd_gate(yraw, z, nw):
    yg = yraw * _silu(z)
    r = lax.rsqrt(jnp.mean(yg * yg, axis=1, keepdims=True) + LN_EPS)
    return yg, r


def _ssd_specs(q, idx):
    return [pl.BlockSpec((q, SSD_XBC), lambda i: (idx(i), P_XBC // SSD_XBC)),
            pl.BlockSpec((8, SSD_XBC), lambda i: (jnp.maximum(idx(i) * (q // 8) - 1, 0), P_XBC // SSD_XBC)),
            pl.BlockSpec((q, SSD_WIDTH), lambda i: (idx(i), P_Z // SSD_WIDTH)),
            pl.BlockSpec((q, LANE), lambda i: (idx(i), P_DT // LANE)),
            pl.BlockSpec((4, SSD_XBC), lambda i: (0, 0)), pl.BlockSpec((1, SSD_XBC), lambda i: (0, 0)),
            pl.BlockSpec((8, LANE), lambda i: (0, 0)), pl.BlockSpec((1, SSD_WIDTH), lambda i: (0, 0)),
            pl.BlockSpec((1, SSD_WIDTH), lambda i: (0, 0)),
            pl.BlockSpec((LANE, SSD_WIDTH), lambda i: (0, 0)), pl.BlockSpec((LANE, SSD_QQ), lambda i: (0, 0))]


def ssd_fwd(proj, cw, cb, prow, d_x, nw, *, name):
    t = proj.shape[0]
    q, ns = SSD_CHUNK, SSD_STATE
    nc = t // q
    spread_p, spread_q = _ssd_spread()

    def body(x_ref, halo_ref, z_ref, dt_ref, cw_ref, cb_ref, prow_ref, dx_ref, nw_ref, sp_ref, sq_ref,
             y_ref, yraw_ref, sall_ref, s_ref, xbc_ref):
        @pl.when(pl.program_id(0) == 0)
        def _():
            s_ref[...] = jnp.zeros_like(s_ref)

        sall_ref[0] = s_ref[...]
        xbc_ref[...] = _silu(_conv_pre(_conv_taps(x_ref, halo_ref, pl.program_id(0) == 0), cw_ref, cb_ref))
        pr = _ssd_prologue(dt_ref, prow_ref, sp_ref, sq_ref)
        xs = xbc_ref[:, pl.ds(0, SSD_WIDTH)]
        xdt = xs * pr["dt_x"]
        xw = xdt * pr["wdec"]
        ys = []
        for g in range(2):
            gp = slice(g * SSD_GP, (g + 1) * SSD_GP)
            bm, cm, lg, wcat, xbd, head = _ssd_group(xbc_ref, g, pr["lcat"], xdt)
            st = s_ref[:, gp]
            ys.append(_dot(wcat, xbd) + pr["ecs"][:, gp] * _dot(cm, st) + xs[:, gp] * dx_ref[:, gp])
            s_ref[:, gp] = pr["ecl"][:, gp] * st + _dot(bm, xw[:, gp], _TN)
        yraw = jnp.concatenate(ys, axis=1)
        yraw_ref[...] = yraw
        yg, r = _ssd_gate(yraw, z_ref[...], nw_ref[...])
        y_ref[...] = (yg * r * nw_ref[...]).astype(y_ref.dtype)

    row = pl.BlockSpec((q, SSD_WIDTH), lambda i: (i, 0))
    return pl.pallas_call(
        body, name=name, grid=(nc,),
        in_specs=_ssd_specs(q, lambda i: i),
        out_specs=[row, row, pl.BlockSpec((1, ns, SSD_WIDTH), lambda i: (i, 0, 0))],
        out_shape=[jax.ShapeDtypeStruct((t, SSD_WIDTH), BF16), jax.ShapeDtypeStruct((t, SSD_WIDTH), F32),
                   jax.ShapeDtypeStruct((nc, ns, SSD_WIDTH), F32)],
        scratch_shapes=[pltpu.VMEM((ns, SSD_WIDTH), F32), pltpu.VMEM((q, SSD_XBC), F32)],
        compiler_params=_params(("arbitrary",)),
    )(proj, proj, proj, proj, cw, cb, prow, d_x, nw, spread_p, spread_q)


def ssd_bwd(proj, cw, cb, prow, d_x, nw, yraw, sall, dout, *, name):
    t = proj.shape[0]
    q, ns = SSD_CHUNK, SSD_STATE
    nc = t // q
    spread_p, spread_q = _ssd_spread()

    def body(x_ref, halo_ref, z_ref, dt_ref, cw_ref, cb_ref, prow_ref, dx_ref, nw_ref, sp_ref, sq_ref, yraw_ref, sall_ref, dout_ref,
             dxraw_ref, dz_ref, ddt_ref, dprm_ref, ddx_ref, dnw_ref, dcw_ref, dcb_ref, ds_ref, xbc_ref, dxbc_ref, nxt_ref):
        @pl.when(pl.program_id(0) == 0)
        def _():
            for r in (ds_ref, dprm_ref, ddx_ref, dnw_ref, dcw_ref, dcb_ref, nxt_ref):
                r[...] = jnp.zeros_like(r)

        taps = _conv_taps(x_ref, halo_ref, pl.program_id(0) == nc - 1)
        conv_pre = _conv_pre(taps, cw_ref, cb_ref)
        xbc_ref[...] = _silu(conv_pre)

        yraw, z, nwv, dout = yraw_ref[...], z_ref[...], nw_ref[...], dout_ref[...]
        yg, r = _ssd_gate(yraw, z, nwv)
        dnw_ref[...] += _colsum(dout * yg * r)
        dyn = dout * nwv
        dyg = r * dyn - yg * (r * r * r) * jnp.mean(dyn * yg, axis=1, keepdims=True)
        dy = dyg * _silu(z)
        dz_ref[...] = (dyg * yraw * _dsilu(z)).astype(dz_ref.dtype)

        pr = _ssd_prologue(dt_ref, prow_ref, sp_ref, sq_ref)
        xs = xbc_ref[:, pl.ds(0, SSD_WIDTH)]
        xdt = xs * pr["dt_x"]
        wdec, ecl = pr["wdec"], pr["ecl"]
        xw = xdt * wdec
        dzm_all = pr["ecs"] * dy
        last = (lax.broadcasted_iota(jnp.int32, (q, 1), 0) == q - 1).astype(F32)
        dxs, dcsxs, es = [], [], []
        for g in range(2):
            gp = slice(g * SSD_GP, (g + 1) * SSD_GP)
            bm, cm, lg, wcat, xbd, head = _ssd_group(xbc_ref, g, pr["lcat"], xdt)
            dyg_ = dy[:, gp]
            dwcat = _dot(dyg_, xbd, _NT)
            dxbd = _dot(wcat, dyg_, _TN)
            dxg = sum(jnp.where(head == j, dxbd[j * q:(j + 1) * q], 0.0) for j in range(4))
            es.append(dwcat * wcat)
            dmm = dwcat * lg
            dm = dmm[:, 0:q] + dmm[:, q:2 * q] + dmm[:, 2 * q:3 * q] + dmm[:, 3 * q:4 * q]
            dcm = _dot(dm, bm)
            dbm = _dot(dm, cm, _TN)
            st = sall_ref[0, :, gp]
            zmat = _dot(cm, st)
            dzm = dzm_all[:, gp]
            dcm = dcm + _dot(dzm, st, _NT)
            dst = _dot(cm, dzm, _TN)
            dcsx = dzm * zmat
            dsn = ds_ref[:, gp]
            dst = dst + ecl[:, gp] * dsn
            dclx = _colsum(dsn * st) * ecl[:, gp]
            dxw = _dot(bm, dsn)
            dbm = dbm + _dot(xw[:, gp], dsn, _NT)
            dxg = dxg + wdec[:, gp] * dxw
            tw = dxw * xdt[:, gp] * wdec[:, gp]
            dclx = dclx + _colsum(tw)
            dcsxs.append(dcsx - tw + last * dclx)
            ds_ref[:, gp] = dst
            dxs.append(dxg)
            dxbc_ref[:, pl.ds(SSD_WIDTH + g * ns, ns)] = dbm
            dxbc_ref[:, pl.ds(SSD_WIDTH + 2 * ns + g * ns, ns)] = dcm
        dx = jnp.concatenate(dxs, axis=1)
        dxbc_ref[:, pl.ds(0, SSD_WIDTH)] = dx * pr["dt_x"] + dy * dx_ref[...]
        ddx_ref[...] += _colsum(dy * xs)
        red = _dot_mask(jnp.concatenate([jnp.concatenate(dcsxs, axis=1), dx * xs], axis=0), sp_ref[...], _NT,
                        mask_left=False, parts=2)
        e_all = jnp.concatenate(es, axis=1)
        e_red = _dot_mask(e_all - jnp.where(pr["diag"], _colsum(e_all), 0.0), sq_ref[...], _NT, mask_left=False, parts=2)
        dadt = _dot_mask(pr["triu"], red[:q] + e_red, mask_left=True, parts=2)
        draw = (red[q:] + dadt * pr["a_r"]) * _sigmoid(pr["raw_c"])
        ddt_ref[...] = draw.astype(ddt_ref.dtype)
        zero = jnp.zeros((6, LANE), F32)
        dprm_ref[...] += jnp.concatenate([_colsum(draw), _colsum(dadt * pr["dt_c"]) * pr["a_r"], zero], axis=0)
        dxr, dcw, dcb = _conv_back(dxbc_ref[...] * _dsilu(conv_pre), taps, cw_ref, nxt_ref)
        dxraw_ref[...] = dxr.astype(dxraw_ref.dtype)
        dcw_ref[...] += dcw
        dcb_ref[...] += dcb

    rev = lambda i: nc - 1 - i
    row = lambda w: pl.BlockSpec((q, w), lambda i: (rev(i), 0))
    fixed = lambda shape: pl.BlockSpec(shape, lambda i: (0, 0))
    return pl.pallas_call(
        body, name=name, grid=(nc,),
        in_specs=_ssd_specs(q, rev) + [row(SSD_WIDTH), pl.BlockSpec((1, ns, SSD_WIDTH), lambda i: (rev(i), 0, 0)),
                                       row(SSD_WIDTH)],
        out_specs=[row(SSD_XBC), row(SSD_WIDTH), row(LANE), fixed((8, LANE)), fixed((1, SSD_WIDTH)), fixed((1, SSD_WIDTH)),
                   fixed((4, SSD_XBC)), fixed((1, SSD_XBC))],
        out_shape=[jax.ShapeDtypeStruct((t, SSD_XBC), BF16), jax.ShapeDtypeStruct((t, SSD_WIDTH), BF16),
                   jax.ShapeDtypeStruct((t, LANE), BF16), jax.ShapeDtypeStruct((8, LANE), F32),
                   jax.ShapeDtypeStruct((1, SSD_WIDTH), F32), jax.ShapeDtypeStruct((1, SSD_WIDTH), F32),
                   jax.ShapeDtypeStruct((4, SSD_XBC), F32), jax.ShapeDtypeStruct((1, SSD_XBC), F32)],
        scratch_shapes=[pltpu.VMEM((ns, SSD_WIDTH), F32), pltpu.VMEM((q, SSD_XBC), F32), pltpu.VMEM((q, SSD_XBC), F32),
                        pltpu.VMEM((8, SSD_XBC), F32)],
        compiler_params=_params(("arbitrary",)),
    )(proj, proj, proj, proj, cw, cb, prow, d_x, nw, spread_p, spread_q, yraw, sall, dout)


def _me():
    return lax.axis_index("x"), lax.axis_index("y"), lax.axis_index("c")


_ANY = pl.BlockSpec(memory_space=pl.ANY)
_MESH = pl.DeviceIdType.MESH


_AG_SEMS = [pltpu.SemaphoreType.DMA((7,)), pltpu.SemaphoreType.DMA((7,)), pltpu.SemaphoreType.DMA(())]
_RS_SEMS = [pltpu.SemaphoreType.DMA((3,)), pltpu.SemaphoreType.DMA((3,)), pltpu.SemaphoreType.DMA(())]


def _ag_phases(src, dst, send_sems, recv_sems, local_sem):
    x, y, c = _me()
    me, sibling = (x, y, c), (x, y, 1 - c)
    chips = [(1 - x, y), (x, 1 - y), (1 - x, 1 - y)]

    def slot(px, py, pc):
        return dst.at[4 * px + 2 * py + pc]

    def copy(kk, blk, to, from_src=False):
        return pltpu.make_async_remote_copy(
            src_ref=src if from_src else slot(*blk), dst_ref=slot(*blk),
            send_sem=send_sems.at[kk], recv_sem=recv_sems.at[kk], device_id=to, device_id_type=_MESH)

    mine = lambda: pltpu.make_async_copy(src, slot(*me), local_sem)
    first = lambda: [copy(0, me, sibling, True)] + [copy(1 + j, me, (*chip, c), True) for j, chip in enumerate(chips)]
    passed = lambda j: copy(4 + j, (*chips[j], c), sibling)

    def start():
        mine().start()
        for cp in first():
            cp.start()

    def forward():
        for j, chip in enumerate(chips):
            copy(1 + j, (*chip, c), me).wait_recv()
            passed(j).start()

    def finish():
        copy(0, sibling, me).wait_recv()
        for j, chip in enumerate(chips):
            copy(4 + j, (*chip, 1 - c), me).wait_recv()
        for cp in first() + [passed(j) for j in range(3)]:
            cp.wait_send()
        mine().wait()

    return start, forward, finish


def _rs_chip_phases(src, dst, send_sems, recv_sems, local_sem, rows=None):
    x, y, c = _me()
    q_me = 2 * x + y
    pick = (lambda q: src.at[q]) if rows is None else (lambda q: src.at[q, rows])
    local = lambda: pltpu.make_async_copy(pick(q_me), dst.at[q_me], local_sem)
    copies = lambda: [pltpu.make_async_remote_copy(src_ref=pick(2 * px + py), dst_ref=dst.at[q_me], send_sem=send_sems.at[j],
                                                   recv_sem=recv_sems.at[j], device_id=(px, py, c), device_id_type=_MESH)
                      for j, (px, py) in enumerate([(1 - x, y), (x, 1 - y), (1 - x, 1 - y)])]

    def start():
        local().start()
        for cp in copies():
            cp.start()

    def finish():
        for cp in copies():
            cp.wait()
        local().wait()

    return start, finish


def all_gather(block, *, name):
    def body(src, dst, send_sems, recv_sems, local_sem):
        for phase in _ag_phases(src, dst, send_sems, recv_sems, local_sem):
            phase()

    return pl.pallas_call(
        body, name=name, in_specs=[_ANY], out_specs=_ANY,
        out_shape=jax.ShapeDtypeStruct((N_DEV,) + block.shape, block.dtype), scratch_shapes=list(_AG_SEMS),
    )(block)


RS_PIECES = 4


def rs_sibling_exchange(halves, *, name):
    _, nq, r, l = halves.shape
    rows = r // RS_PIECES
    assert r % RS_PIECES == 0 and rows % 16 == 0

    def body(src, dst, send_sems, recv_sems):
        x, y, c = _me()
        copies = []
        for q in range(nq):
            for i in range(RS_PIECES):
                kk = q * RS_PIECES + i
                cp = pltpu.make_async_remote_copy(
                    src_ref=src.at[1 - c, q, pl.ds(i * rows, rows)], dst_ref=dst.at[q, pl.ds(i * rows, rows)],
                    send_sem=send_sems.at[kk], recv_sem=recv_sems.at[kk], device_id=(x, y, 1 - c), device_id_type=_MESH)
                cp.start()
                copies.append(cp)
        for cp in copies:
            cp.wait()

    n_copies = nq * RS_PIECES
    return pl.pallas_call(
        body, name=name, in_specs=[_ANY], out_specs=_ANY,
        out_shape=jax.ShapeDtypeStruct((nq, r, l), halves.dtype),
        scratch_shapes=[pltpu.SemaphoreType.DMA((n_copies,)), pltpu.SemaphoreType.DMA((n_copies,))],
    )(halves)


def pair_sum_bf16(halves, theirs, *, name, tt=512):
    _, nq, r, wd = halves.shape
    tt = min(tt, r)
    parity = lax.axis_index("c").astype(jnp.int32).reshape(1)

    def body(c_ref, own_ref, sib_ref, o_ref):
        o_ref[...] = (own_ref[...] + sib_ref[...]).astype(BF16)

    return pl.pallas_call(
        body, name=name,
        grid_spec=pltpu.PrefetchScalarGridSpec(
            num_scalar_prefetch=1, grid=(nq, r // tt),
            in_specs=[pl.BlockSpec((None, None, tt, wd), lambda q, i, c: (c[0], q, i, 0)),
                      pl.BlockSpec((None, tt, wd), lambda q, i, c: (q, i, 0))],
            out_specs=pl.BlockSpec((None, tt, wd), lambda q, i, c: (q, i, 0))),
        out_shape=jax.ShapeDtypeStruct((nq, r, wd), BF16),
        compiler_params=_params(("parallel", "parallel")),
    )(parity, halves, theirs)


def rs_chip_exchange(part, *, name):
    def body(src, dst, send_sems, recv_sems, local_sem):
        for phase in _rs_chip_phases(src, dst, send_sems, recv_sems, local_sem):
            phase()

    return pl.pallas_call(
        body, name=name, in_specs=[_ANY], out_specs=_ANY,
        out_shape=jax.ShapeDtypeStruct(part.shape, part.dtype), scratch_shapes=list(_RS_SEMS),
    )(part)


def adamw(slabs, w, m, v, *, name, tt):
    ns, (r, wd) = slabs.shape[0], w.shape
    tt = min(tt, r)
    assert r % tt == 0

    def body(s_ref, w_ref, m_ref, v_ref, g_ref, d_ref, nm_ref, nv_ref):
        g = s_ref[0].astype(F32)
        for kdev in range(1, ns):
            g = g + s_ref[kdev].astype(F32)
        wv = w_ref[...]
        nm = ADAM_B1 * m_ref[...] + (1.0 - ADAM_B1) * g
        nv = ADAM_B2 * v_ref[...] + (1.0 - ADAM_B2) * (g * g)
        m_hat = nm / (1.0 - ADAM_B1 ** ADAM_STEP)
        v_hat = nv / (1.0 - ADAM_B2 ** ADAM_STEP)
        g_ref[...] = g
        d_ref[...] = -ADAM_LR * (m_hat / (jnp.sqrt(v_hat) + ADAM_EPS) + ADAM_WD * wv)
        nm_ref[...] = nm
        nv_ref[...] = nv

    spec = pl.BlockSpec((tt, wd), lambda i: (i, 0))
    return pl.pallas_call(
        body, name=name, grid=(r // tt,),
        in_specs=[pl.BlockSpec((ns, tt, wd), lambda i: (0, i, 0)), spec, spec, spec],
        out_specs=[spec] * 4, out_shape=[jax.ShapeDtypeStruct((r, wd), F32)] * 4,
        compiler_params=_params(("parallel",)),
    )(slabs, w, m, v)


WIDE = 1024
BIG = [("w_in", True, 289), ("w_out", False, 128), ("xa_wq", False, 128), ("xa_wk", False, 128), ("xa_wv", False, 128),
       ("xa_wo", False, 128), ("mlp_w2", False, 512), ("mlp_w1", True, 512)]
TINY = [("ssd_conv_w", 2), ("s5_glu_w", 1), ("rg_conv_w", 2)]
KEEP_F32 = ("ssd_conv_w", "rg_conv_w")
TINY_ROWS = 32
SHARDED = [name for name, _, _ in BIG] + [name for name, _ in TINY]
SMALL = ["ssd_conv_b", "ssd_dt_bias", "ssd_a_log", "ssd_d", "ssd_norm_w", "s5_lam_re", "s5_lam_im",
         "s5_log_step", "s5_b_re", "s5_b_im", "s5_c_re", "s5_c_im", "s5_d", "s5_glu_b", "rg_conv_b",
         "rg_wa", "rg_ba", "rg_wx", "rg_bx", "rg_lambda", "ln1_g", "ln1_b", "ln2_g", "ln2_b", "ln3_g", "ln3_b"]
WEIGHTS = ['w_in', 'w_out', 'ssd_conv_w', 'ssd_conv_b', 'ssd_dt_bias', 'ssd_a_log', 'ssd_d', 'ssd_norm_w',
           's5_lam_re', 's5_lam_im', 's5_log_step', 's5_b_re', 's5_b_im', 's5_c_re', 's5_c_im', 's5_d',
           's5_glu_w', 's5_glu_b', 'rg_conv_w', 'rg_conv_b', 'rg_wa', 'rg_ba', 'rg_wx', 'rg_bx', 'rg_lambda',
           'ln1_g', 'ln1_b', 'xa_wq', 'xa_wk', 'xa_wv', 'xa_wo', 'ln2_g', 'ln2_b', 'mlp_w1', 'mlp_w2',
           'ln3_g', 'ln3_b']


def _pad16(rows):
    return -(-rows // 16) * 16


def _pack_rows(flat, mult):
    n = flat.shape[-1]
    r = -(-n // (LANE * mult)) * mult
    pad = [(0, 0)] * (flat.ndim - 1) + [(0, r * LANE - n)]
    return jnp.pad(flat, pad).reshape(flat.shape[:-1] + (r, LANE))


def _unpack(packed, shapes):
    lead = packed.shape[:-2]
    flat = packed.reshape(lead + (-1,))
    out, off = [], 0
    for s in shapes:
        n = math.prod(s)
        out.append(flat[..., off:off + n].reshape(lead + tuple(s)))
        off += n
    return out


PACK_ROWS = 2048


def _tiny_block(flat):
    pad = [(0, 0)] * (flat.ndim - 1) + [(0, TINY_ROWS * WIDE - flat.shape[-1])]
    return jnp.pad(flat, pad).reshape(flat.shape[:-1] + (TINY_ROWS, WIDE))


def _pack_layer(big, tiny_flat=None):
    blocks, used = [], 0
    some = big[BIG[0][0]]

    def zeros(rows):
        return jnp.zeros(some.shape[:-2] + (rows, WIDE), some.dtype)

    for name, _, rows in BIG:
        blocks.append(jnp.pad(big[name], [(0, 0)] * (some.ndim - 2) + [(0, _pad16(rows) - rows), (0, 0)]))
        used += _pad16(rows)
    if tiny_flat is not None:
        blocks.append(_tiny_block(tiny_flat))
        used += TINY_ROWS
    return jnp.concatenate(blocks + [zeros(PACK_ROWS - used)], axis=-2)


def _unpack_layer(packed):
    big, off = {}, 0
    for name, _, rows in BIG:
        big[name] = packed[..., off:off + rows, :]
        off += _pad16(rows)
    return big, packed[..., off:off + TINY_ROWS, :].reshape(packed.shape[:-2] + (TINY_ROWS * WIDE,))


def _split_flat(flat, shapes):
    out, off = [], 0
    for s in shapes:
        n = math.prod(s)
        out.append(flat[..., off:off + n].reshape(flat.shape[:-1] + tuple(s)))
        off += n
    return out


def _to_full(gathered, axis):
    g = jnp.moveaxis(gathered, 0, axis)
    s = g.shape
    return g.reshape(s[:axis] + (s[axis] * s[axis + 1],) + s[axis + 2:])


def _to_slabs(full, axis):
    s = full.shape
    g = full.reshape(s[:axis] + (N_DEV, s[axis] // N_DEV) + s[axis + 1:])
    return jnp.moveaxis(g, axis, 0)


def _blockdiag(w):
    h, i, j = w.shape
    eye = jnp.eye(h, dtype=w.dtype)
    return (w[:, :, None, :] * eye[:, None, :, None]).reshape(h * i, h * j)


def _blockdiag_extract(m, h):
    i, j = m.shape[0] // h, m.shape[1] // h
    eye = jnp.eye(h, dtype=m.dtype)
    return (m.reshape(h, i, h, j) * eye[:, None, :, None]).sum(axis=2)


def _s5_disc(lr, li, ls, bre, bim):
    step = jnp.exp(ls)[:, None]
    er = jnp.exp(lr * step)
    ar, ai = er * jnp.cos(li * step), er * jnp.sin(li * step)
    nr, ni, den = ar - 1.0, ai, lr * lr + li * li
    qr, qi = (nr * lr + ni * li) / den, (ni * lr - nr * li) / den
    bbr = qr[..., None] * bre - qi[..., None] * bim
    bbi = qr[..., None] * bim + qi[..., None] * bre
    return ar, ai, bbr, bbi


def _row(v, width=None):
    v = v.reshape(1, -1)
    if width is not None and v.shape[1] < width:
        v = jnp.pad(v, ((0, 0), (0, width - v.shape[1])))
    return v


def _relu2(a):
    r = jnp.maximum(a, 0.0)
    return r * r


def _add_alpha(acc, d):
    return acc + ALPHA * d


def _layer_params(full, small, l):
    p = {}
    w_in = full["w_in"]
    z, xbc, dt, u, xr, g = w_in[0:512], w_in[512:1536], w_in[1536:1544], w_in[1544:1800], w_in[1800:2056], w_in[2056:2312]
    p["w_inp"] = jnp.concatenate([xbc, z, u, xr, g, dt, jnp.zeros((D_INP - P_DT - 8, D_MODEL), w_in.dtype)], axis=0)
    for k_ in ("w_out", "xa_wq", "xa_wk", "xa_wv", "xa_wo", "mlp_w1", "mlp_w2"):
        p[k_] = full[k_]
    p["s5_glu_w"] = full["s5_glu_w"][l]
    p["ssd_cw"], p["ssd_cb"] = full["ssd_conv_w"][l], _row(small["ssd_conv_b"][l])
    dtb, alog, dsk = small["ssd_dt_bias"][l], small["ssd_a_log"][l], small["ssd_d"][l]
    p["prow"] = jnp.concatenate([_row(dtb, LANE), _row(alog, LANE), jnp.zeros((6, LANE), F32)], axis=0)
    p["ssd_dx"] = _row(jnp.repeat(dsk, SSD_HEAD_DIM))
    p["ssd_nw"] = _row(small["ssd_norm_w"][l])
    s5_in = (small["s5_lam_re"][l], small["s5_lam_im"][l], small["s5_log_step"][l], small["s5_b_re"][l], small["s5_b_im"][l])
    (ar, ai, bbr, bbi), p["s5_vjp"] = jax.vjp(_s5_disc, *s5_in)
    p["lam_fwd"] = jnp.concatenate([_row(ar), _row(ai)], axis=1)
    p["lam_adj"] = jnp.concatenate([_row(ar), _row(-ai)], axis=1)
    p["bcat"] = jnp.concatenate([_blockdiag(jnp.swapaxes(bbr, 1, 2)), _blockdiag(jnp.swapaxes(bbi, 1, 2))], axis=1)
    p["ccat"] = jnp.concatenate([_blockdiag(jnp.swapaxes(small["s5_c_re"][l], 1, 2)),
                                 -_blockdiag(jnp.swapaxes(small["s5_c_im"][l], 1, 2))], axis=0)
    p["s5_d"], p["s5_glu_b"] = _row(small["s5_d"][l]), _row(small["s5_glu_b"][l])
    p["rg_cw"], p["rg_cb"] = full["rg_conv_w"][l], _row(small["rg_conv_b"][l])
    p["rg_wa"], p["rg_wx"] = _blockdiag(small["rg_wa"][l]), _blockdiag(small["rg_wx"][l])
    p["rg_ba"], p["rg_bx"], p["rg_lam"] = _row(small["rg_ba"][l]), _row(small["rg_bx"][l]), _row(small["rg_lambda"][l])
    for i in (1, 2, 3):
        p[f"g{i}"], p[f"b{i}"] = _row(small[f"ln{i}_g"][l]), _row(small[f"ln{i}_b"][l])
    return p


def _take_side(res, n_out, got):
    res = res if isinstance(res, (list, tuple)) else (res,)
    got.extend(res[n_out:])
    return res[0] if n_out == 1 else res[:n_out]


def _layer_fwd(h0, mem, p, sides={}):
    t = h0.shape[0]
    s = {"h0": h0}
    got = []
    proj = _take_side(mm(h0, p["w_inp"], tb=True, name="in_proj", side=sides.get("in_proj")), 1, got)
    y_ssd, yraw, sall = ssd_fwd(proj, p["ssd_cw"], p["ssd_cb"], p["prow"], p["ssd_dx"], p["ssd_nw"], name="ssd_fwd")
    hs5, ylin, y_s5 = s5_fwd(proj, p["bcat"], p["lam_fwd"], p["ccat"], p["s5_d"], p["s5_glu_w"], p["s5_glu_b"], name="s5_fwd")
    rg_prm = (p["rg_cw"], p["rg_cb"], p["rg_wa"], p["rg_wx"], p["rg_ba"], p["rg_bx"], p["rg_lam"])
    y_rg, xc, a_rg, h_rg = rg_fwd(proj, *rg_prm, name="rg_fwd")
    ycat = jnp.concatenate([y_ssd, y_s5, y_rg], axis=1)
    pre1, h1 = mm_ln(ycat, p["w_out"], h0, p["g1"], p["b1"], name="out_proj")
    q = mm(h1, p["xa_wq"], name="xa_q", out_dtype=BF16)
    k = mm(mem, p["xa_wk"], name="xa_kv")
    v = mm(mem, p["xa_wv"], name="xa_kv")
    (o,), _ = rowk(_attn_fwd_fn, [(q, D_MODEL, 0)], [k, v], [D_MODEL], [], rows=t, name="xa_fwd", out_dtypes=[BF16])
    pre2, h2 = mm_ln(o, p["xa_wo"], h1, p["g2"], p["b2"], name="xa_o")
    a_mlp = _take_side(mm(h2, p["mlp_w1"], tb=True, name="mlp_up", side=sides.get("mlp_up")), 1, got)
    pre3, h3 = _take_side(mm_ln(a_mlp, p["mlp_w2"], h2, p["g3"], p["b3"], fa=_relu2, name="mlp_down",
                                side=sides.get("mlp_down")), 2, got)
    s.update(proj=proj, yraw=yraw, sall=sall, hs5=hs5, ylin=ylin, xc=xc, a_rg=a_rg, h_rg=h_rg,
             ycat=ycat, pre1=pre1, h1=h1, q=q, k=k, v=v, o=o, pre2=pre2, h2=h2, a_mlp=a_mlp, pre3=pre3)
    return h3, s, got


def _layer_bwd(dh3, mem, p, s, l, gfull, gsmall, sides={}):
    t = dh3.shape[0]
    proj = s["proj"]
    dpre3, dg3, db3 = ln_bwd(s["pre3"], dh3, p["g3"], name="ln_bwd")
    got = []
    da = _take_side(mm(dpre3, p["mlp_w2"], tb=True, o_extra=(s["a_mlp"],), fo=lambda acc, a: acc * 2.0 * jnp.maximum(a, 0.0),
                       name="mlp_da", out_dtype=BF16, side=sides.get("mlp_da")), 1, got)
    gfull["mlp_w2"][l] = mm(s["a_mlp"], dpre3, ta=True, fa=_relu2, name="mlp_dw2")
    gfull["mlp_w1"][l] = mm(da, s["h2"], ta=True, name="mlp_dw1")
    dh2 = _take_side(mm(da, p["mlp_w1"], o_extra=(dpre3,), fo=_add_alpha, name="mlp_dx", side=sides.get("mlp_dx")), 1, got)
    dpre2, dg2, db2 = ln_bwd(s["pre2"], dh2, p["g2"], name="ln_bwd")
    do = _take_side(mm(dpre2, p["xa_wo"], tb=True, name="xa_do", out_dtype=BF16, side=sides.get("xa_do")), 1, got)
    gfull["xa_wo"][l] = mm(s["o"], dpre2, ta=True, name="dw_sq")
    (dq,), (dk, dv) = rowk(_attn_bwd_fn, [(s["q"], D_MODEL, 0), (do, D_MODEL, 0)], [s["k"], s["v"]], [D_MODEL],
                           [(256, D_MODEL), (256, D_MODEL)], rows=t, name="xa_bwd", out_dtypes=[BF16])
    gfull["xa_wq"][l] = mm(s["h1"], dq, ta=True, name="dw_sq")
    gfull["xa_wk"][l] = mm(mem, dk, ta=True, name="dw_kv")
    gfull["xa_wv"][l] = mm(mem, dv, ta=True, name="dw_kv")
    dh1 = mm(dq, p["xa_wq"], tb=True, o_extra=(dpre2,), fo=_add_alpha, name="dx_sq")
    dpre1, dg1, db1 = ln_bwd(s["pre1"], dh1, p["g1"], name="ln_bwd")
    dycat = mm(dpre1, p["w_out"], tb=True, name="xa_do")
    gfull["w_out"][l] = mm(s["ycat"], dpre1, ta=True, name="dw_sq")
    rg_prm = (p["rg_cw"], p["rg_cb"], p["rg_wa"], p["rg_wx"], p["rg_ba"], p["rg_bx"], p["rg_lam"])
    dxr, dg_rg, d_rgcw, d_rgcb, dwa, dwx, dba, dbx, dlam = rg_bwd(proj, dycat, s["xc"], s["a_rg"], s["h_rg"], *rg_prm, name="rg_bwd")
    du, dccat, dbcat, dar, dai, d_s5d, d_gluw, d_glub = s5_bwd(dycat, s["ylin"], s["hs5"], proj, p["bcat"], p["lam_adj"], p["ccat"],
                                                               p["s5_d"], p["s5_glu_w"], p["s5_glu_b"], name="s5_bwd")
    dxbc, dz, ddt, dprm, ddx, dnw, d_scw, d_scb = ssd_bwd(proj, p["ssd_cw"], p["ssd_cb"], p["prow"], p["ssd_dx"], p["ssd_nw"],
                                                         s["yraw"], s["sall"], dycat, name="ssd_bwd")
    dproj = jnp.concatenate([dxbc, dz, du, dxr, dg_rg, ddt, jnp.zeros((t, D_INP - P_DT - LANE), BF16)], axis=1)
    dh0 = mm(dproj, p["w_inp"], o_extra=(dpre1,), fo=_add_alpha, name="in_proj_dx")
    dwp = mm(dproj, s["h0"], ta=True, name="in_proj_dw")
    gfull["w_in"][l] = jnp.concatenate([dwp[P_Z:P_Z + 512], dwp[P_XBC:P_XBC + 1024], dwp[P_DT:P_DT + 8],
                                        dwp[P_U:P_U + 256], dwp[P_XR:P_XR + 256], dwp[P_G:P_G + 256]], axis=0)
    gfull["ssd_conv_w"][l], gfull["rg_conv_w"][l], gfull["s5_glu_w"][l] = d_scw, d_rgcw, d_gluw
    ng, ns = S5_GROUPS, S5_STATE
    dbbr = jnp.swapaxes(_blockdiag_extract(dbcat[:, :S5_NSTATE], ng), 1, 2)
    dbbi = jnp.swapaxes(_blockdiag_extract(dbcat[:, S5_NSTATE:], ng), 1, 2)
    d_lr, d_li, d_ls, d_bre, d_bim = p["s5_vjp"]((dar.reshape(ng, ns), dai.reshape(ng, ns), dbbr, dbbi))
    gsmall["s5_lam_re"][l], gsmall["s5_lam_im"][l], gsmall["s5_log_step"][l] = d_lr, d_li, d_ls
    gsmall["s5_b_re"][l], gsmall["s5_b_im"][l] = d_bre, d_bim
    gsmall["s5_c_re"][l] = jnp.swapaxes(_blockdiag_extract(dccat[:S5_NSTATE], ng), 1, 2)
    gsmall["s5_c_im"][l] = -jnp.swapaxes(_blockdiag_extract(dccat[S5_NSTATE:], ng), 1, 2)
    gsmall["s5_d"][l], gsmall["s5_glu_b"][l] = d_s5d[0], d_glub[0]
    gsmall["ssd_conv_b"][l], gsmall["rg_conv_b"][l] = d_scb[0], d_rgcb[0]
    gsmall["ssd_dt_bias"][l], gsmall["ssd_a_log"][l] = dprm[0, :8], dprm[1, :8]
    gsmall["ssd_d"][l] = ddx.reshape(SSD_HEADS, SSD_HEAD_DIM).sum(axis=1)
    gsmall["ssd_norm_w"][l] = dnw[0]
    gsmall["rg_wa"][l], gsmall["rg_wx"][l] = _blockdiag_extract(dwa, RG_BLOCKS), _blockdiag_extract(dwx, RG_BLOCKS)
    gsmall["rg_ba"][l], gsmall["rg_bx"][l] = dba.reshape(RG_BLOCKS, RG_BLOCK_DIM), dbx.reshape(RG_BLOCKS, RG_BLOCK_DIM)
    gsmall["rg_lambda"][l] = dlam[0]
    for i, (dg, db) in zip((1, 2, 3), ((dg1, db1), (dg2, db2), (dg3, db3))):
        gsmall[f"ln{i}_g"][l], gsmall[f"ln{i}_b"][l] = dg[0], db[0]
    return dh0, got


def _step(a):
    h = a["x"][0]
    mem = a["mem"][0]
    t = h.shape[0]
    r4, r3 = PACK_ROWS // 4, 3 * PACK_ROWS // 8

    def my_shards(pre):
        return ({name: (jnp.swapaxes(a[pre + name], 1, 2) if tr else a[pre + name]) for name, tr, _ in BIG},
                [a[pre + name] for name, _ in TINY])

    def my_pack(pre, l):
        big, tiny = my_shards(pre)
        return _pack_layer({name: w[l] for name, w in big.items()},
                           jnp.concatenate([w.reshape(-1) for w in tiny]) if l == 0 else None)

    big, tiny = my_shards("")
    tiny16 = [(lax.bitcast_convert_type(w, BF16) if name in KEEP_F32 else w.astype(BF16)).reshape(-1)
              for (name, _), w in zip(TINY, tiny)]
    packed = [_pack_layer({name: w[l].astype(BF16) for name, w in big.items()}, jnp.concatenate(tiny16) if l == 0 else None)
              for l in range(DEPTH)]
    small = {name: a[name] for name in SMALL}

    def gathered_weights(g):
        gbig, gtiny = _unpack_layer(g)
        return {name: w.reshape(-1, WIDE) for name, w in gbig.items()}, gtiny

    full, gtiny = gathered_weights(all_gather(packed[0], name="ag_weights"))
    tiny_shapes = [w.shape + ((2,) if name in KEEP_F32 else ()) for (name, _), w in zip(TINY, tiny)]
    tiny_full = {name: _to_full(lax.bitcast_convert_type(g, F32) if name in KEEP_F32 else g, axis)
                 for (name, axis), g in zip(TINY, _split_flat(gtiny, tiny_shapes))}
    p0 = _layer_params({**full, **tiny_full}, small, 0)
    h, s0, got = _layer_fwd(h, mem, p0, sides={"in_proj": ("gather", packed[1], 0, r4), "mlp_up": ("gather", packed[1], r4, r3),
                                                "mlp_down": ("gather", packed[1], r4 + r3, r3)})
    full, _ = gathered_weights(jnp.concatenate(got, axis=1))
    p1 = _layer_params({**full, **tiny_full}, small, 1)
    h, s1, _ = _layer_fwd(h, mem, p1)
    (dh,), (loss_part,) = rowk(_loss_fn, [(h, D_MODEL, 0), (a["loss_target"][0], D_MODEL, 0)], [], [D_MODEL], [(1, 1)],
                               rows=t, name="loss_head", tt=2 * ROW_TILE)
    loss = lax.psum(loss_part[0, 0], ("x", "y", "c"))
    gfull = {name: [None] * DEPTH for name in SHARDED}
    gsmall = {name: [None] * DEPTH for name in SMALL}

    def chip_partials(l):
        gbig = {name: gfull[name][l].reshape(N_DEV, rows, WIDE) for name, _, rows in BIG}
        gtiny = None
        if l == 0:
            gtiny = jnp.concatenate([_to_slabs(jnp.stack(gfull[name]), axis).reshape(N_DEV, -1) for name, axis in TINY], axis=1)
        slabs = _pack_layer(gbig, gtiny)
        halves = jnp.swapaxes(slabs.reshape((4, 2) + slabs.shape[1:]), 0, 1)
        theirs = rs_sibling_exchange(halves, name="rs_sibling")
        return pair_sum_bf16(halves, theirs, name="rs_pair_sum")

    dh, _ = _layer_bwd(dh, mem, p1, s1, 1, gfull, gsmall)
    part1 = chip_partials(1)
    dh, got = _layer_bwd(dh, mem, p0, s0, 0, gfull, gsmall, sides={"mlp_da": ("chips", part1, 0, r3), "mlp_dx": ("chips", part1, r3, r3),
                                                                    "xa_do": ("chips", part1, 2 * r3, r4)})
    grad_x = dh[None]
    landed = [rs_chip_exchange(chip_partials(0), name="rs_chips"), jnp.concatenate(got, axis=1)]
    bigs = [adamw(landed[l], my_pack("", l), my_pack("m_", l), my_pack("v_", l), name="adamw_sharded", tt=256) for l in range(DEPTH)]
    gs = _pack_rows(jnp.concatenate([jnp.stack(gsmall[name]).reshape(-1) for name in SMALL]), 8)
    gs = all_gather(gs, name="ag_small_grads")
    pks = lambda pre: _pack_rows(jnp.concatenate([a[pre + name].reshape(-1) for name in SMALL]), 8)
    sm = adamw(gs, pks(""), pks("m_"), pks("v_"), name="adamw_replicated", tt=gs.shape[1])
    out = {}
    for i, kind in enumerate(("grad_", "delta_", "new_m_", "new_v_")):
        layers = [_unpack_layer(bigs[l][i]) for l in range(DEPTH)]
        for name, tr, _ in BIG:
            arr = jnp.stack([layers[l][0][name] for l in range(DEPTH)])
            out[kind + name] = jnp.swapaxes(arr, 1, 2) if tr else arr
        for (name, _), arr in zip(TINY, _split_flat(layers[0][1], [w.shape for w in tiny])):
            out[kind + name] = arr
        for name, arr in zip(SMALL, _unpack(sm[i], [a[name].shape for name in SMALL])):
            out[kind + name] = arr
    return (loss, grad_x) + tuple(out[kind + name] for kind in ("grad_", "delta_", "new_m_", "new_v_") for name in WEIGHTS)


def kernel(x, mem, w_in, w_out, ssd_conv_w, ssd_conv_b, ssd_dt_bias, ssd_a_log, ssd_d, ssd_norm_w, s5_lam_re, s5_lam_im, s5_log_step, s5_b_re, s5_b_im, s5_c_re, s5_c_im, s5_d, s5_glu_w, s5_glu_b, rg_conv_w, rg_conv_b, rg_wa, rg_ba, rg_wx, rg_bx, rg_lambda, ln1_g, ln1_b, xa_wq, xa_wk, xa_wv, xa_wo, ln2_g, ln2_b, mlp_w1, mlp_w2, ln3_g, ln3_b, loss_target, m_w_in, m_w_out, m_ssd_conv_w, m_ssd_conv_b, m_ssd_dt_bias, m_ssd_a_log, m_ssd_d, m_ssd_norm_w, m_s5_lam_re, m_s5_lam_im, m_s5_log_step, m_s5_b_re, m_s5_b_im, m_s5_c_re, m_s5_c_im, m_s5_d, m_s5_glu_w, m_s5_glu_b, m_rg_conv_w, m_rg_conv_b, m_rg_wa, m_rg_ba, m_rg_wx, m_rg_bx, m_rg_lambda, m_ln1_g, m_ln1_b, m_xa_wq, m_xa_wk, m_xa_wv, m_xa_wo, m_ln2_g, m_ln2_b, m_mlp_w1, m_mlp_w2, m_ln3_g, m_ln3_b, v_w_in, v_w_out, v_ssd_conv_w, v_ssd_conv_b, v_ssd_dt_bias, v_ssd_a_log, v_ssd_d, v_ssd_norm_w, v_s5_lam_re, v_s5_lam_im, v_s5_log_step, v_s5_b_re, v_s5_b_im, v_s5_c_re, v_s5_c_im, v_s5_d, v_s5_glu_w, v_s5_glu_b, v_rg_conv_w, v_rg_conv_b, v_rg_wa, v_rg_ba, v_rg_wx, v_rg_bx, v_rg_lambda, v_ln1_g, v_ln1_b, v_xa_wq, v_xa_wk, v_xa_wv, v_xa_wo, v_ln2_g, v_ln2_b, v_mlp_w1, v_mlp_w2, v_ln3_g, v_ln3_b):
    return _step(dict(locals()))
```
